```python
import math
import jax, jax.numpy as jnp
from jax import lax
import numpy as np

D_MODEL = 1024
BATCH = 32
SEQ = 2048
DEPTH = 2

N_EVEN = (DEPTH + 1) // 2
N_ODD = DEPTH // 2
NORM_EPS = 1e-6

A_HEADS = 4
A_DK = 128
A_DV = 128
A_WIDTH = A_HEADS * A_DK
A_CHUNK = 32
B_HEADS = 4
B_HD = 128
B_WIDTH = B_HEADS * B_HD
B_DILATIONS = ((128, 1), (512, 4), (2048, 16))
ROPE_THETA = 10000.0
IN_AB_WIDTH = 4 * A_WIDTH + 3 * B_WIDTH
MIX_WIDTH = A_WIDTH + B_WIDTH
C_GROUP = 16
C_GROUPS = D_MODEL // C_GROUP
C_STATE = 64
C_CHUNK = 128
C_MIN_NEG_RE = -1e-4
MEM_LEN = 256
X_HEADS = 4
X_HD = D_MODEL // X_HEADS
D_FF = -(-8 * D_MODEL // (3 * 256)) * 256

kernel_name = "hybrid_hgrn2_dilated_s5_block"


def rms_norm(x, w):
    xf = x.astype(jnp.float32)
    y = xf * lax.rsqrt(jnp.mean(xf * xf, axis=-1, keepdims=True) + NORM_EPS)
    return (y * w.astype(jnp.float32)).astype(x.dtype)


def split_heads(t, n_heads):
    b, l, _ = t.shape
    return t.reshape(b, l, n_heads, -1).transpose(0, 2, 1, 3)


def merge_heads(t):
    b, h, l, d = t.shape
    return t.transpose(0, 2, 1, 3).reshape(b, l, h * d)


def rotary(x, pos):
    half = x.shape[-1] // 2
    inv_freq = ROPE_THETA ** (-jnp.arange(half, dtype=jnp.float32) / half)
    ang = pos.astype(jnp.float32)[:, None] * inv_freq[None, :]
    cos, sin = jnp.cos(ang), jnp.sin(ang)
    x1, x2 = x[..., :half], x[..., half:]
    return jnp.concatenate([x1 * cos - x2 * sin, x1 * sin + x2 * cos], axis=-1)


def hgrn2_mixer(q, f_logit, i_val, g, lb, onorm_w):
    bsz, L, _ = q.shape
    nc = L // A_CHUNK
    f = lb + (1.0 - lb) * jax.nn.sigmoid(f_logit)
    log_f = jnp.log(f)
    k = 1.0 - f

    def chunked(t):
        return split_heads(t, A_HEADS).reshape(bsz, A_HEADS, nc, A_CHUNK, -1)

    qc, kc, vc, lfc = chunked(q), chunked(k), chunked(i_val), chunked(log_f)
    b = jnp.cumsum(lfc, axis=3)
    b_last = b[:, :, :, -1:, :]
    q_dec = qc * jnp.exp(b)
    k_inv = kc * jnp.exp(-b)
    k_end = kc * jnp.exp(b_last - b)
    decay = jnp.exp(b_last[:, :, :, 0, :])
    causal = jnp.tril(jnp.ones((A_CHUNK, A_CHUNK), dtype=bool))
    scores = jnp.einsum('bhncd,bhnsd->bhncs', q_dec, k_inv)
    scores = jnp.where(causal, scores, 0.0)
    o_intra = jnp.einsum('bhncs,bhnsv->bhncv', scores, vc)

    def step(S, inp):
        qd, ke, v, dec = inp
        o = jnp.einsum('bhcd,bhdv->bhcv', qd, S)
        S = dec[..., None] * S + jnp.einsum('bhcd,bhcv->bhdv', ke, v)
        return S, o

    S0 = jnp.zeros((bsz, A_HEADS, A_DK, A_DV), jnp.float32)
    xs = (jnp.moveaxis(q_dec, 2, 0), jnp.moveaxis(k_end, 2, 0),
          jnp.moveaxis(vc, 2, 0), jnp.moveaxis(decay, 2, 0))
    _, o_inter = lax.scan(step, S0, xs)
    o = (o_intra + jnp.moveaxis(o_inter, 0, 2)).reshape(bsz, A_HEADS, L, A_DV)
    o = o * lax.rsqrt(jnp.mean(o * o, axis=-1, keepdims=True) + NORM_EPS)
    o = merge_heads(o) * onorm_w.astype(jnp.float32)
    return o * jax.nn.silu(g)


def dilated_branch(q, k, v, dil, span):
    bsz, H, L, hd = q.shape
    Ls = L // dil
    nb = -(-Ls // span)
    pad = nb * span - Ls

    def strided(t):
        t = t.reshape(bsz, H, Ls, dil, hd).transpose(0, 1, 3, 2, 4)
        t = jnp.pad(t, ((0, 0), (0, 0), (0, 0), (0, pad), (0, 0)))
        return t.reshape(bsz, H, dil, nb, span, hd)

    def with_prev(t):
        prev = jnp.pad(t[:, :, :, :-1], ((0, 0), (0, 0), (0, 0), (1, 0), (0, 0), (0, 0)))
        return jnp.concatenate([prev, t], axis=4)

    qb = strided(q)
    kw, vw = with_prev(strided(k)), with_prev(strided(v))
    s = jnp.einsum('bhrnqd,bhrnkd->bhrnqk', qb, kw) * (hd ** -0.5)
    iq = jnp.arange(span)[:, None]
    ik = jnp.arange(2 * span)[None, :]
    delta = span + iq - ik
    kpos = (jnp.arange(nb)[:, None, None] - 1) * span + ik[None]
    mask = (delta >= 0) & (delta <= span) & (kpos >= 0)
    s = jnp.where(mask, s, -jnp.inf)
    m = jnp.max(s, axis=-1, keepdims=True)
    p = jnp.exp(s - m)
    l = jnp.sum(p, axis=-1)
    o = jnp.einsum('bhrnqk,bhrnkd->bhrnqd', p, vw) / l[..., None]
    lse = m[..., 0] + jnp.log(l)

    def unstride(t):
        t = t.reshape(bsz, H, dil, nb * span, -1)[:, :, :, :Ls]
        return t.transpose(0, 1, 3, 2, 4).reshape(bsz, H, L, -1)

    return unstride(o), unstride(lse[..., None])[..., 0]


def dilated_attention(q, k, v):
    outs, lses = [], []
    for window, dil in B_DILATIONS:
        o, lse = dilated_branch(q, k, v, dil, window // dil)
        outs.append(o)
        lses.append(lse)
    wts = jax.nn.softmax(jnp.stack(lses), axis=0)
    return jnp.einsum('gbhl,gbhld->bhld', wts, jnp.stack(outs))


def mix_ab(h, w_in, w_out, lb, onorm_w, pos):
    z = (h @ w_in).astype(jnp.float32)
    offs = np.cumsum([A_WIDTH] * 4 + [B_WIDTH] * 2).tolist()
    qa, fa, ia, ga, qb, kb, vb = jnp.split(z, offs, axis=-1)
    oa = hgrn2_mixer(qa, fa, ia, ga, lb, onorm_w)
    qh = rotary(split_heads(qb, B_HEADS), pos)
    kh = rotary(split_heads(kb, B_HEADS), pos)
    vh = split_heads(vb, B_HEADS)
    ob = merge_heads(dilated_attention(qh, kh, vh))
    y = jnp.concatenate([oa, ob], axis=-1).astype(h.dtype)
    return y @ w_out


def complex_affine(e1, e2):
    a1r, a1i, b1r, b1i = e1
    a2r, a2i, b2r, b2i = e2
    return (a2r * a1r - a2i * a1i,
            a2r * a1i + a2i * a1r,
            a2r * b1r - a2i * b1i + b2r,
            a2r * b1i + a2i * b1r + b2i)


def s5_mixer(h, lam_re, lam_im, log_dt, b_re, b_im, c_re, c_im, d_skip, w_glu):
    f32 = jnp.float32
    u = h.astype(f32)
    bsz, L, _ = u.shape
    lr = jnp.minimum(lam_re.astype(f32), C_MIN_NEG_RE)
    li = lam_im.astype(f32)
    dt = jnp.exp(log_dt.astype(f32))[:, None]
    mag = jnp.exp(dt * lr)
    ar, ai = mag * jnp.cos(dt * li), mag * jnp.sin(dt * li)
    den = lr * lr + li * li
    zr = ((ar - 1.0) * lr + ai * li) / den
    zi = (ai * lr - (ar - 1.0) * li) / den
    br, bi = b_re.astype(f32), b_im.astype(f32)
    bbr = zr[..., None] * br - zi[..., None] * bi
    bbi = zr[..., None] * bi + zi[..., None] * br
    cr, ci = c_re.astype(f32), c_im.astype(f32)
    nc = L // C_CHUNK
    ug = u.reshape(bsz, nc, C_CHUNK, C_GROUPS, C_GROUP).transpose(1, 0, 2, 3, 4)

    def step(state, uc):
        hr, hi = state
        xr = jnp.einsum('btgc,gpc->btgp', uc, bbr)
        xi = jnp.einsum('btgc,gpc->btgp', uc, bbi)
        a_r = jnp.broadcast_to(ar, xr.shape)
        a_i = jnp.broadcast_to(ai, xr.shape)
        pr, pi_, sr, si = lax.associative_scan(complex_affine, (a_r, a_i, xr, xi), axis=1)
        str_ = pr * hr[:, None] - pi_ * hi[:, None] + sr
        sti = pr * hi[:, None] + pi_ * hr[:, None] + si
        y = (jnp.einsum('btgp,gcp->btgc', str_, cr)
             - jnp.einsum('btgp,gcp->btgc', sti, ci))
        return (str_[:, -1], sti[:, -1]), y

    h0 = (jnp.zeros((bsz, C_GROUPS, C_STATE), f32), jnp.zeros((bsz, C_GROUPS, C_STATE), f32))
    _, y = lax.scan(step, h0, ug)
    y = y.transpose(1, 0, 2, 3, 4).reshape(bsz, L, D_MODEL)
    y = y + d_skip.astype(f32) * u
    gl = jax.nn.gelu(y, approximate=False).astype(h.dtype)
    z = gl @ w_glu
    return z[..., :D_MODEL] * jax.nn.sigmoid(z[..., D_MODEL:])


def memory_cross_attention(h, mem_n, wq, wkv, wo):
    q = split_heads(h @ wq, X_HEADS).astype(jnp.float32)
    kv = mem_n @ wkv
    k = split_heads(kv[..., :D_MODEL], X_HEADS).astype(jnp.float32)
    v = split_heads(kv[..., D_MODEL:], X_HEADS).astype(jnp.float32)
    s = jnp.einsum('bhqd,bhkd->bhqk', q, k) * (X_HD ** -0.5)
    p = jax.nn.softmax(s, axis=-1)
    o = jnp.einsum('bhqk,bhkd->bhqd', p, v).astype(h.dtype)
    return merge_heads(o) @ wo


def swiglu(h, w_in, w_out):
    z = h @ w_in
    return (jax.nn.silu(z[..., :D_FF]) * z[..., D_FF:]) @ w_out


def _fwd_setup_inputs(seed: int = 0) -> dict:
    key = jax.random.key(seed)
    ks = jax.random.split(key, 24)
    f32 = jnp.float32
    nrm = lambda k, shape, scale: jax.random.normal(k, shape, f32) * scale
    return {
        "x": nrm(ks[0], (BATCH, SEQ, D_MODEL), 1.0),
        "mem": nrm(ks[1], (BATCH, MEM_LEN, D_MODEL), 1.0),
        "norm_w": 1.0 + nrm(ks[2], (DEPTH, 6, D_MODEL), 0.05),
        "mem_norm_w": 1.0 + nrm(ks[3], (DEPTH, D_MODEL), 0.05),
        "ab_w_in": nrm(ks[4], (N_EVEN, D_MODEL, IN_AB_WIDTH), D_MODEL ** -0.5),
        "ab_w_out": nrm(ks[5], (N_EVEN, MIX_WIDTH, D_MODEL), MIX_WIDTH ** -0.5),
        "hgrn_lb_logits": nrm(ks[6], (DEPTH + 1, A_WIDTH), 0.1),
        "hgrn_out_norm_w": 1.0 + nrm(ks[7], (N_EVEN, A_WIDTH), 0.05),
        "s5_lambda_re": -0.5 + nrm(ks[8], (N_ODD, C_GROUPS, C_STATE), 0.01),
        "s5_lambda_im": np.pi * jnp.arange(C_STATE, dtype=f32) + nrm(ks[9], (N_ODD, C_GROUPS, C_STATE), 0.01),
        "s5_log_dt": jax.random.uniform(ks[10], (N_ODD, C_GROUPS), f32, math.log(1e-3), math.log(1e-1)),
        "s5_b_re": nrm(ks[11], (N_ODD, C_GROUPS, C_STATE, C_GROUP), (2 * C_GROUP) ** -0.5),
        "s5_b_im": nrm(ks[12], (N_ODD, C_GROUPS, C_STATE, C_GROUP), (2 * C_GROUP) ** -0.5),
        "s5_c_re": nrm(ks[13], (N_ODD, C_GROUPS, C_GROUP, C_STATE), 2.0 ** -0.5),
        "s5_c_im": nrm(ks[14], (N_ODD, C_GROUPS, C_GROUP, C_STATE), 2.0 ** -0.5),
        "s5_d": nrm(ks[15], (N_ODD, D_MODEL), 1.0),
        "s5_w_glu": nrm(ks[16], (N_ODD, D_MODEL, 2 * D_MODEL), D_MODEL ** -0.5),
        "xattn_wq": nrm(ks[17], (DEPTH, D_MODEL, D_MODEL), D_MODEL ** -0.5),
        "xattn_wkv": nrm(ks[18], (DEPTH, D_MODEL, 2 * D_MODEL), D_MODEL ** -0.5),
        "xattn_wo": nrm(ks[19], (DEPTH, D_MODEL, D_MODEL), D_MODEL ** -0.5),
        "ffn_w_in": nrm(ks[20], (DEPTH, D_MODEL, 2 * D_FF), D_MODEL ** -0.5),
        "ffn_w_out": nrm(ks[21], (DEPTH, D_FF, D_MODEL), D_FF ** -0.5),
    }


def _fwd_reference(x, mem, norm_w, mem_norm_w, ab_w_in, ab_w_out, hgrn_lb_logits,
              hgrn_out_norm_w, s5_lambda_re, s5_lambda_im, s5_log_dt, s5_b_re,
              s5_b_im, s5_c_re, s5_c_im, s5_d, s5_w_glu, xattn_wq, xattn_wkv,
              xattn_wo, ffn_w_in, ffn_w_out):
    L = x.shape[1]
    pos = jnp.arange(L, dtype=jnp.int32)
    lb_table = jnp.cumsum(jax.nn.softmax(hgrn_lb_logits.astype(jnp.float32), axis=0), axis=0)
    for layer in range(DEPTH):
        j = layer // 2
        h = rms_norm(x, norm_w[layer, 0])
        if layer % 2 == 0:
            y = mix_ab(h, ab_w_in[j], ab_w_out[j], lb_table[layer], hgrn_out_norm_w[j], pos)
        else:
            y = s5_mixer(h, s5_lambda_re[j], s5_lambda_im[j], s5_log_dt[j], s5_b_re[j],
                         s5_b_im[j], s5_c_re[j], s5_c_im[j], s5_d[j], s5_w_glu[j])
        x = x + rms_norm(y, norm_w[layer, 1])
        h = rms_norm(x, norm_w[layer, 2])
        mem_n = rms_norm(mem, mem_norm_w[layer])
        y = memory_cross_attention(h, mem_n, xattn_wq[layer], xattn_wkv[layer], xattn_wo[layer])
        x = x + rms_norm(y, norm_w[layer, 3])
        h = rms_norm(x, norm_w[layer, 4])
        x = x + rms_norm(swiglu(h, ffn_w_in[layer], ffn_w_out[layer]), norm_w[layer, 5])
    return x


import jax as _jax
import jax.numpy as _jnp

TWIN_FORMAT = 'train_step'
FWD_PARAMS = ['x', 'mem', 'norm_w', 'mem_norm_w', 'ab_w_in', 'ab_w_out', 'hgrn_lb_logits', 'hgrn_out_norm_w', 's5_lambda_re', 's5_lambda_im', 's5_log_dt', 's5_b_re', 's5_b_im', 's5_c_re', 's5_c_im', 's5_d', 's5_w_glu', 'xattn_wq', 'xattn_wkv', 'xattn_wo', 'ffn_w_in', 'ffn_w_out']
TWIN_WEIGHTS = ['norm_w', 'mem_norm_w', 'ab_w_in', 'ab_w_out', 'hgrn_lb_logits', 'hgrn_out_norm_w', 's5_lambda_re', 's5_lambda_im', 's5_log_dt', 's5_b_re', 's5_b_im', 's5_c_re', 's5_c_im', 's5_d', 's5_w_glu', 'xattn_wq', 'xattn_wkv', 'xattn_wo', 'ffn_w_in', 'ffn_w_out']
TWIN_DIFF_INPUT = 'x'
TWIN_INPUTS = ['x', 'mem', 'norm_w', 'mem_norm_w', 'ab_w_in', 'ab_w_out', 'hgrn_lb_logits', 'hgrn_out_norm_w', 's5_lambda_re', 's5_lambda_im', 's5_log_dt', 's5_b_re', 's5_b_im', 's5_c_re', 's5_c_im', 's5_d', 's5_w_glu', 'xattn_wq', 'xattn_wkv', 'xattn_wo', 'ffn_w_in', 'ffn_w_out', 'loss_target', 'm_norm_w', 'm_mem_norm_w', 'm_ab_w_in', 'm_ab_w_out', 'm_hgrn_lb_logits', 'm_hgrn_out_norm_w', 'm_s5_lambda_re', 'm_s5_lambda_im', 'm_s5_log_dt', 'm_s5_b_re', 'm_s5_b_im', 'm_s5_c_re', 'm_s5_c_im', 'm_s5_d', 'm_s5_w_glu', 'm_xattn_wq', 'm_xattn_wkv', 'm_xattn_wo', 'm_ffn_w_in', 'm_ffn_w_out', 'v_norm_w', 'v_mem_norm_w', 'v_ab_w_in', 'v_ab_w_out', 'v_hgrn_lb_logits', 'v_hgrn_out_norm_w', 'v_s5_lambda_re', 'v_s5_lambda_im', 'v_s5_log_dt', 'v_s5_b_re', 'v_s5_b_im', 'v_s5_c_re', 'v_s5_c_im', 'v_s5_d', 'v_s5_w_glu', 'v_xattn_wq', 'v_xattn_wkv', 'v_xattn_wo', 'v_ffn_w_in', 'v_ffn_w_out']
TWIN_OUTPUTS = ['loss', 'grad_x', 'grad_norm_w', 'grad_mem_norm_w', 'grad_ab_w_in', 'grad_ab_w_out', 'grad_hgrn_lb_logits', 'grad_hgrn_out_norm_w', 'grad_s5_lambda_re', 'grad_s5_lambda_im', 'grad_s5_log_dt', 'grad_s5_b_re', 'grad_s5_b_im', 'grad_s5_c_re', 'grad_s5_c_im', 'grad_s5_d', 'grad_s5_w_glu', 'grad_xattn_wq', 'grad_xattn_wkv', 'grad_xattn_wo', 'grad_ffn_w_in', 'grad_ffn_w_out', 'delta_norm_w', 'delta_mem_norm_w', 'delta_ab_w_in', 'delta_ab_w_out', 'delta_hgrn_lb_logits', 'delta_hgrn_out_norm_w', 'delta_s5_lambda_re', 'delta_s5_lambda_im', 'delta_s5_log_dt', 'delta_s5_b_re', 'delta_s5_b_im', 'delta_s5_c_re', 'delta_s5_c_im', 'delta_s5_d', 'delta_s5_w_glu', 'delta_xattn_wq', 'delta_xattn_wkv', 'delta_xattn_wo', 'delta_ffn_w_in', 'delta_ffn_w_out', 'new_m_norm_w', 'new_m_mem_norm_w', 'new_m_ab_w_in', 'new_m_ab_w_out', 'new_m_hgrn_lb_logits', 'new_m_hgrn_out_norm_w', 'new_m_s5_lambda_re', 'new_m_s5_lambda_im', 'new_m_s5_log_dt', 'new_m_s5_b_re', 'new_m_s5_b_im', 'new_m_s5_c_re', 'new_m_s5_c_im', 'new_m_s5_d', 'new_m_s5_w_glu', 'new_m_xattn_wq', 'new_m_xattn_wkv', 'new_m_xattn_wo', 'new_m_ffn_w_in', 'new_m_ffn_w_out', 'new_v_norm_w', 'new_v_mem_norm_w', 'new_v_ab_w_in', 'new_v_ab_w_out', 'new_v_hgrn_lb_logits', 'new_v_hgrn_out_norm_w', 'new_v_s5_lambda_re', 'new_v_s5_lambda_im', 'new_v_s5_log_dt', 'new_v_s5_b_re', 'new_v_s5_b_im', 'new_v_s5_c_re', 'new_v_s5_c_im', 'new_v_s5_d', 'new_v_s5_w_glu', 'new_v_xattn_wq', 'new_v_xattn_wkv', 'new_v_xattn_wo', 'new_v_ffn_w_in', 'new_v_ffn_w_out']
TWIN_LEAF_KINDS = {'loss': 'loss', 'grad_x': 'grad_x', 'grad_norm_w': 'grad_w', 'grad_mem_norm_w': 'grad_w', 'grad_ab_w_in': 'grad_w', 'grad_ab_w_out': 'grad_w', 'grad_hgrn_lb_logits': 'grad_w', 'grad_hgrn_out_norm_w': 'grad_w', 'grad_s5_lambda_re': 'grad_w', 'grad_s5_lambda_im': 'grad_w', 'grad_s5_log_dt': 'grad_w', 'grad_s5_b_re': 'grad_w', 'grad_s5_b_im': 'grad_w', 'grad_s5_c_re': 'grad_w', 'grad_s5_c_im': 'grad_w', 'grad_s5_d': 'grad_w', 'grad_s5_w_glu': 'grad_w', 'grad_xattn_wq': 'grad_w', 'grad_xattn_wkv': 'grad_w', 'grad_xattn_wo': 'grad_w', 'grad_ffn_w_in': 'grad_w', 'grad_ffn_w_out': 'grad_w', 'delta_norm_w': 'delta_w', 'delta_mem_norm_w': 'delta_w', 'delta_ab_w_in': 'delta_w', 'delta_ab_w_out': 'delta_w', 'delta_hgrn_lb_logits': 'delta_w', 'delta_hgrn_out_norm_w': 'delta_w', 'delta_s5_lambda_re': 'delta_w', 'delta_s5_lambda_im': 'delta_w', 'delta_s5_log_dt': 'delta_w', 'delta_s5_b_re': 'delta_w', 'delta_s5_b_im': 'delta_w', 'delta_s5_c_re': 'delta_w', 'delta_s5_c_im': 'delta_w', 'delta_s5_d': 'delta_w', 'delta_s5_w_glu': 'delta_w', 'delta_xattn_wq': 'delta_w', 'delta_xattn_wkv': 'delta_w', 'delta_xattn_wo': 'delta_w', 'delta_ffn_w_in': 'delta_w', 'delta_ffn_w_out': 'delta_w', 'new_m_norm_w': 'new_m', 'new_m_mem_norm_w': 'new_m', 'new_m_ab_w_in': 'new_m', 'new_m_ab_w_out': 'new_m', 'new_m_hgrn_lb_logits': 'new_m', 'new_m_hgrn_out_norm_w': 'new_m', 'new_m_s5_lambda_re': 'new_m', 'new_m_s5_lambda_im': 'new_m', 'new_m_s5_log_dt': 'new_m', 'new_m_s5_b_re': 'new_m', 'new_m_s5_b_im': 'new_m', 'new_m_s5_c_re': 'new_m', 'new_m_s5_c_im': 'new_m', 'new_m_s5_d': 'new_m', 'new_m_s5_w_glu': 'new_m', 'new_m_xattn_wq': 'new_m', 'new_m_xattn_wkv': 'new_m', 'new_m_xattn_wo': 'new_m', 'new_m_ffn_w_in': 'new_m', 'new_m_ffn_w_out': 'new_m', 'new_v_norm_w': 'new_v', 'new_v_mem_norm_w': 'new_v', 'new_v_ab_w_in': 'new_v', 'new_v_ab_w_out': 'new_v', 'new_v_hgrn_lb_logits': 'new_v', 'new_v_hgrn_out_norm_w': 'new_v', 'new_v_s5_lambda_re': 'new_v', 'new_v_s5_lambda_im': 'new_v', 'new_v_s5_log_dt': 'new_v', 'new_v_s5_b_re': 'new_v', 'new_v_s5_b_im': 'new_v', 'new_v_s5_c_re': 'new_v', 'new_v_s5_c_im': 'new_v', 'new_v_s5_d': 'new_v', 'new_v_s5_w_glu': 'new_v', 'new_v_xattn_wq': 'new_v', 'new_v_xattn_wkv': 'new_v', 'new_v_xattn_wo': 'new_v', 'new_v_ffn_w_in': 'new_v', 'new_v_ffn_w_out': 'new_v'}


def _forward(args):
    return _fwd_reference(*[args[k] for k in FWD_PARAMS])


def _output_shape():
    out = _jax.eval_shape(lambda: _forward(_fwd_setup_inputs(0)))
    return out.shape, out.dtype

N_MICROBATCH = 1
ADAM_LR = 0.001
ADAM_B1 = 0.9
ADAM_B2 = 0.999
ADAM_EPS = 1e-08
ADAM_WD = 0.01
ADAM_STEP = 10
PER_EXAMPLE_BATCH_AXIS = {'x': 0, 'mem': 0, 'loss_target': 0}
SHARED_INPUTS = []
_WEIGHT_DTYPES = {'norm_w': _jnp.float32, 'mem_norm_w': _jnp.float32, 'ab_w_in': _jnp.float32, 'ab_w_out': _jnp.float32, 'hgrn_lb_logits': _jnp.float32, 'hgrn_out_norm_w': _jnp.float32, 's5_lambda_re': _jnp.float32, 's5_lambda_im': _jnp.float32, 's5_log_dt': _jnp.float32, 's5_b_re': _jnp.float32, 's5_b_im': _jnp.float32, 's5_c_re': _jnp.float32, 's5_c_im': _jnp.float32, 's5_d': _jnp.float32, 's5_w_glu': _jnp.float32, 'xattn_wq': _jnp.float32, 'xattn_wkv': _jnp.float32, 'xattn_wo': _jnp.float32, 'ffn_w_in': _jnp.float32, 'ffn_w_out': _jnp.float32}
MOMENT_SCALE = {'norm_w': 4.678385e+01, 'mem_norm_w': 1.742135e+01, 'ab_w_in': 1.888811e+00, 'ab_w_out': 2.014151e+00, 'hgrn_lb_logits': 9.691436e-01, 'hgrn_out_norm_w': 3.445607e+00, 's5_lambda_re': 4.842399e+00, 's5_lambda_im': 4.271349e+00, 's5_log_dt': 3.293892e+02, 's5_b_re': 3.444362e+00, 's5_b_im': 3.519795e+00, 's5_c_re': 8.917527e-01, 's5_c_im': 8.677694e-01, 's5_d': 9.988350e+00, 's5_w_glu': 1.228527e+01, 'xattn_wq': 3.612312e+00, 'xattn_wkv': 1.263189e+01, 'xattn_wo': 1.829541e+01, 'ffn_w_in': 2.611395e+00, 'ffn_w_out': 5.298469e+00}


def _to_microbatches(a, axis):
    t = _jnp.moveaxis(a, axis, 0)
    t = t.reshape((N_MICROBATCH, t.shape[0] // N_MICROBATCH) + t.shape[1:])
    return _jnp.moveaxis(t, 1, axis + 1)


def setup_inputs(seed: int = 0) -> dict:
    inp = _fwd_setup_inputs(seed)
    key = _jax.random.fold_in(_jax.random.key(seed), 7919)
    shape, _ = _output_shape()
    out = dict(inp)
    out["loss_target"] = _jax.random.normal(_jax.random.fold_in(key, 0), shape, _jnp.float32)
    for i, name in enumerate(TWIN_WEIGHTS):
        w = inp[name].astype(_jnp.float32)
        if MOMENT_SCALE is None:
            s = _jnp.sqrt(_jnp.mean(_jnp.square(w)) + 1e-30)
        else:
            s = MOMENT_SCALE[name]
        km, kv = _jax.random.split(_jax.random.fold_in(key, i + 1))
        out[name] = w
        out["m_" + name] = s * _jax.random.normal(km, w.shape, _jnp.float32)
        out["v_" + name] = (s * s) * _jax.random.uniform(kv, w.shape, _jnp.float32, 0.5, 1.5)
    if N_MICROBATCH > 1:
        for name, axis in PER_EXAMPLE_BATCH_AXIS.items():
            out[name] = _to_microbatches(out[name], axis)
    return {'x': out['x'], 'mem': out['mem'], 'norm_w': out['norm_w'], 'mem_norm_w': out['mem_norm_w'], 'ab_w_in': out['ab_w_in'], 'ab_w_out': out['ab_w_out'], 'hgrn_lb_logits': out['hgrn_lb_logits'], 'hgrn_out_norm_w': out['hgrn_out_norm_w'], 's5_lambda_re': out['s5_lambda_re'], 's5_lambda_im': out['s5_lambda_im'], 's5_log_dt': out['s5_log_dt'], 's5_b_re': out['s5_b_re'], 's5_b_im': out['s5_b_im'], 's5_c_re': out['s5_c_re'], 's5_c_im': out['s5_c_im'], 's5_d': out['s5_d'], 's5_w_glu': out['s5_w_glu'], 'xattn_wq': out['xattn_wq'], 'xattn_wkv': out['xattn_wkv'], 'xattn_wo': out['xattn_wo'], 'ffn_w_in': out['ffn_w_in'], 'ffn_w_out': out['ffn_w_out'], 'loss_target': out['loss_target'], 'm_norm_w': out['m_norm_w'], 'm_mem_norm_w': out['m_mem_norm_w'], 'm_ab_w_in': out['m_ab_w_in'], 'm_ab_w_out': out['m_ab_w_out'], 'm_hgrn_lb_logits': out['m_hgrn_lb_logits'], 'm_hgrn_out_norm_w': out['m_hgrn_out_norm_w'], 'm_s5_lambda_re': out['m_s5_lambda_re'], 'm_s5_lambda_im': out['m_s5_lambda_im'], 'm_s5_log_dt': out['m_s5_log_dt'], 'm_s5_b_re': out['m_s5_b_re'], 'm_s5_b_im': out['m_s5_b_im'], 'm_s5_c_re': out['m_s5_c_re'], 'm_s5_c_im': out['m_s5_c_im'], 'm_s5_d': out['m_s5_d'], 'm_s5_w_glu': out['m_s5_w_glu'], 'm_xattn_wq': out['m_xattn_wq'], 'm_xattn_wkv': out['m_xattn_wkv'], 'm_xattn_wo': out['m_xattn_wo'], 'm_ffn_w_in': out['m_ffn_w_in'], 'm_ffn_w_out': out['m_ffn_w_out'], 'v_norm_w': out['v_norm_w'], 'v_mem_norm_w': out['v_mem_norm_w'], 'v_ab_w_in': out['v_ab_w_in'], 'v_ab_w_out': out['v_ab_w_out'], 'v_hgrn_lb_logits': out['v_hgrn_lb_logits'], 'v_hgrn_out_norm_w': out['v_hgrn_out_norm_w'], 'v_s5_lambda_re': out['v_s5_lambda_re'], 'v_s5_lambda_im': out['v_s5_lambda_im'], 'v_s5_log_dt': out['v_s5_log_dt'], 'v_s5_b_re': out['v_s5_b_re'], 'v_s5_b_im': out['v_s5_b_im'], 'v_s5_c_re': out['v_s5_c_re'], 'v_s5_c_im': out['v_s5_c_im'], 'v_s5_d': out['v_s5_d'], 'v_s5_w_glu': out['v_s5_w_glu'], 'v_xattn_wq': out['v_xattn_wq'], 'v_xattn_wkv': out['v_xattn_wkv'], 'v_xattn_wo': out['v_xattn_wo'], 'v_ffn_w_in': out['v_ffn_w_in'], 'v_ffn_w_out': out['v_ffn_w_out']}


def _loss(weights, diff, rest, loss_target):
    with _jax.named_scope("forward"):
        args = {**rest, TWIN_DIFF_INPUT: diff, **{k: w.astype(_WEIGHT_DTYPES[k]) for k, w in weights.items()}}
        y = _forward(args)
    with _jax.named_scope("loss_head"):
        err = _jnp.square(y.astype(_jnp.float32) - loss_target)
        return 0.5 * _jnp.sum(_jnp.mean(err, axis=-1)) if err.ndim else 0.5 * err


def _adamw(w, g, m, v):
    m = ADAM_B1 * m + (1.0 - ADAM_B1) * g
    v = ADAM_B2 * v + (1.0 - ADAM_B2) * _jnp.square(g)
    m_hat = m / (1.0 - ADAM_B1 ** ADAM_STEP)
    v_hat = v / (1.0 - ADAM_B2 ** ADAM_STEP)
    delta = -ADAM_LR * (m_hat / (_jnp.sqrt(v_hat) + ADAM_EPS) + ADAM_WD * w)
    return delta, m, v


def reference(x, mem, norm_w, mem_norm_w, ab_w_in, ab_w_out, hgrn_lb_logits, hgrn_out_norm_w, s5_lambda_re, s5_lambda_im, s5_log_dt, s5_b_re, s5_b_im, s5_c_re, s5_c_im, s5_d, s5_w_glu, xattn_wq, xattn_wkv, xattn_wo, ffn_w_in, ffn_w_out, loss_target, m_norm_w, m_mem_norm_w, m_ab_w_in, m_ab_w_out, m_hgrn_lb_logits, m_hgrn_out_norm_w, m_s5_lambda_re, m_s5_lambda_im, m_s5_log_dt, m_s5_b_re, m_s5_b_im, m_s5_c_re, m_s5_c_im, m_s5_d, m_s5_w_glu, m_xattn_wq, m_xattn_wkv, m_xattn_wo, m_ffn_w_in, m_ffn_w_out, v_norm_w, v_mem_norm_w, v_ab_w_in, v_ab_w_out, v_hgrn_lb_logits, v_hgrn_out_norm_w, v_s5_lambda_re, v_s5_lambda_im, v_s5_log_dt, v_s5_b_re, v_s5_b_im, v_s5_c_re, v_s5_c_im, v_s5_d, v_s5_w_glu, v_xattn_wq, v_xattn_wkv, v_xattn_wo, v_ffn_w_in, v_ffn_w_out):
    given = dict(x=x, mem=mem, norm_w=norm_w, mem_norm_w=mem_norm_w, ab_w_in=ab_w_in, ab_w_out=ab_w_out, hgrn_lb_logits=hgrn_lb_logits, hgrn_out_norm_w=hgrn_out_norm_w, s5_lambda_re=s5_lambda_re, s5_lambda_im=s5_lambda_im, s5_log_dt=s5_log_dt, s5_b_re=s5_b_re, s5_b_im=s5_b_im, s5_c_re=s5_c_re, s5_c_im=s5_c_im, s5_d=s5_d, s5_w_glu=s5_w_glu, xattn_wq=xattn_wq, xattn_wkv=xattn_wkv, xattn_wo=xattn_wo, ffn_w_in=ffn_w_in, ffn_w_out=ffn_w_out, loss_target=loss_target, m_norm_w=m_norm_w, m_mem_norm_w=m_mem_norm_w, m_ab_w_in=m_ab_w_in, m_ab_w_out=m_ab_w_out, m_hgrn_lb_logits=m_hgrn_lb_logits, m_hgrn_out_norm_w=m_hgrn_out_norm_w, m_s5_lambda_re=m_s5_lambda_re, m_s5_lambda_im=m_s5_lambda_im, m_s5_log_dt=m_s5_log_dt, m_s5_b_re=m_s5_b_re, m_s5_b_im=m_s5_b_im, m_s5_c_re=m_s5_c_re, m_s5_c_im=m_s5_c_im, m_s5_d=m_s5_d, m_s5_w_glu=m_s5_w_glu, m_xattn_wq=m_xattn_wq, m_xattn_wkv=m_xattn_wkv, m_xattn_wo=m_xattn_wo, m_ffn_w_in=m_ffn_w_in, m_ffn_w_out=m_ffn_w_out, v_norm_w=v_norm_w, v_mem_norm_w=v_mem_norm_w, v_ab_w_in=v_ab_w_in, v_ab_w_out=v_ab_w_out, v_hgrn_lb_logits=v_hgrn_lb_logits, v_hgrn_out_norm_w=v_hgrn_out_norm_w, v_s5_lambda_re=v_s5_lambda_re, v_s5_lambda_im=v_s5_lambda_im, v_s5_log_dt=v_s5_log_dt, v_s5_b_re=v_s5_b_re, v_s5_b_im=v_s5_b_im, v_s5_c_re=v_s5_c_re, v_s5_c_im=v_s5_c_im, v_s5_d=v_s5_d, v_s5_w_glu=v_s5_w_glu, v_xattn_wq=v_xattn_wq, v_xattn_wkv=v_xattn_wkv, v_xattn_wo=v_xattn_wo, v_ffn_w_in=v_ffn_w_in, v_ffn_w_out=v_ffn_w_out)
    weights = {n: given[n] for n in TWIN_WEIGHTS}
    shared = {n: given[n] for n in SHARED_INPUTS}
    per_example = {n: given[n] for n in ['x', 'mem']}
    grad_fn = _jax.value_and_grad(_loss, argnums=(0, 1))

    def one_microbatch(ex, loss_target):
        ex = dict(ex)
        diff = ex.pop(TWIN_DIFF_INPUT)
        return grad_fn(weights, diff, {**shared, **ex}, loss_target)

    if N_MICROBATCH == 1:
        loss, (grad_w, grad_x) = one_microbatch(per_example, given["loss_target"])
    else:
        def body(carry, xs):
            loss_sum, grad_sum = carry
            l_k, (gw_k, gx_k) = one_microbatch(xs[0], xs[1])
            with _jax.named_scope("update"):
                return (loss_sum + l_k, _jax.tree.map(_jnp.add, grad_sum, gw_k)), gx_k

        init = (_jnp.zeros((), _jnp.float32), _jax.tree.map(_jnp.zeros_like, weights))
        (loss, grad_w), grad_x = _jax.lax.scan(body, init, (per_example, given["loss_target"]))
    with _jax.named_scope("update"):
        delta_w, new_m, new_v = {}, {}, {}
        for n in TWIN_WEIGHTS:
            delta_w[n], new_m[n], new_v[n] = _adamw(weights[n], grad_w[n], given["m_" + n], given["v_" + n])
    return (loss, grad_x, *[grad_w[n] for n in TWIN_WEIGHTS], *[delta_w[n] for n in TWIN_WEIGHTS],
            *[new_m[n] for n in TWIN_WEIGHTS], *[new_v[n] for n in TWIN_WEIGHTS])
```

```python
import functools
import math

import numpy as np
import jax
import jax.numpy as jnp
from jax import lax
from jax.experimental import pallas as pl
from jax.experimental.pallas import tpu as pltpu

F32 = jnp.float32
BF16 = jnp.bfloat16
MXU_DTYPE = jnp.bfloat16

D_MODEL = 1024
NORM_EPS = 1e-6
A_HEADS, A_DK, A_CHUNK = 4, 128, 32
A_WIDTH = A_HEADS * A_DK
B_HEADS, B_HD = 4, 128
B_WIDTH = B_HEADS * B_HD
B_DILATIONS = ((128, 1), (512, 4), (2048, 16))
ROPE_THETA = 10000.0
C_GROUP, C_GROUPS, C_STATE, C_CHUNK = 16, 64, 64, 128
C_MIN_NEG_RE = -1e-4
MEM_LEN = 256
X_HEADS = 4
X_HD = D_MODEL // X_HEADS
D_FF = 2816
ADAM_LR, ADAM_B1, ADAM_B2, ADAM_EPS, ADAM_WD, ADAM_STEP = 0.001, 0.9, 0.999, 1e-08, 0.01, 10

N_CHIPS = 4
MESH = pl.DeviceIdType.MESH
ANY = pl.BlockSpec(memory_space=pl.ANY)
_RELS = ((1, 0), (0, 1), (1, 1))

WEIGHTS = ['norm_w', 'mem_norm_w', 'ab_w_in', 'ab_w_out', 'hgrn_lb_logits', 'hgrn_out_norm_w', 's5_lambda_re',
           's5_lambda_im', 's5_log_dt', 's5_b_re', 's5_b_im', 's5_c_re', 's5_c_im', 's5_d', 's5_w_glu', 'xattn_wq',
           'xattn_wkv', 'xattn_wo', 'ffn_w_in', 'ffn_w_out']
BIG = (('ab_w_in', 1), ('ab_w_out', 0), ('s5_w_glu', 1), ('xattn_wq', 0), ('xattn_wkv', 1), ('xattn_wo', 0),
       ('ffn_w_in', 1), ('ffn_w_out', 0))
BIG_NAMES = tuple(n for n, _ in BIG)
SMALL = tuple(n for n in WEIGHTS if n not in BIG_NAMES)
SHARDED_SMALL = ('norm_w', 's5_d')
PACK_COLS = 1024


def _pos():
    return lax.axis_index("x"), lax.axis_index("y"), lax.axis_index("c")


def _flip(v, d):
    return 1 - v if d else v


def _divisor(n, want):
    for t in (want, 1024, 512, 256, 128, 64, 32, 16, 8):
        if t <= want and n % t == 0:
            return t
    return n


def _rowwise(name, fn, rows, bcasts=(), out_rows=(), out_sums=(), tile=256):
    n = rows[0].shape[0]
    t = _divisor(n, tile)
    nr, nb, no, ns = len(rows), len(bcasts), len(out_rows), len(out_sums)

    def body(*refs):
        vals = [r[...] for r in refs[:nr + nb]]
        res = fn(*vals)
        if not isinstance(res, (tuple, list)):
            res = (res,)
        outs = refs[nr + nb:]
        for k in range(no):
            outs[k][...] = res[k].astype(outs[k].dtype)
        if ns:
            first = pl.program_id(0) == 0
            for k in range(ns):
                o, val = outs[no + k], res[no + k]

                @pl.when(first)
                def _():
                    o[...] = val

                @pl.when(jnp.logical_not(first))
                def _():
                    o[...] += val

    in_specs = [pl.BlockSpec((t, r.shape[1]), lambda i: (i, 0)) for r in rows]
    in_specs += [pl.BlockSpec(b.shape, lambda i: (0, 0)) for b in bcasts]
    out_specs = [pl.BlockSpec((t, c), lambda i: (i, 0)) for c, _ in out_rows]
    out_specs += [pl.BlockSpec((1, c), lambda i: (0, 0)) for c in out_sums]
    out_shape = [jax.ShapeDtypeStruct((n, c), dt) for c, dt in out_rows]
    out_shape += [jax.ShapeDtypeStruct((1, c), F32) for c in out_sums]
    res = pl.pallas_call(
        body, name=name, grid=(n // t,), in_specs=in_specs, out_specs=out_specs, out_shape=out_shape,
        compiler_params=pltpu.CompilerParams(dimension_semantics=("arbitrary",)),
    )(*rows, *bcasts)
    return res


def _rms(x, w):
    r = lax.rsqrt(jnp.mean(x * x, axis=-1, keepdims=True) + NORM_EPS)
    return x * r * w


def _rms_bwd(x, w, dy):
    r = lax.rsqrt(jnp.mean(x * x, axis=-1, keepdims=True) + NORM_EPS)
    xh = x * r
    dxh = dy * w
    dx = r * (dxh - xh * jnp.mean(dxh * xh, axis=-1, keepdims=True))
    return dx, jnp.sum(dy * xh, axis=0, keepdims=True)


def _silu(z):
    return z * jax.nn.sigmoid(z)


def _mm(name, a, b, ta=False, tb=False, out_dtype=F32, tm=512, tn=512, tk=1024):
    m, k = a.shape[::-1] if ta else a.shape
    k2, n = b.shape[::-1] if tb else b.shape
    assert k == k2, (name, a.shape, b.shape)
    tm, tn = _divisor(m, tm), _divisor(n, tn)
    tk = k if k <= 2816 and not ta else _divisor(k, tk)
    nk = k // tk
    dims = (((0 if ta else 1,), (1 if tb else 0,)), ((), ()))

    def prod(a_ref, b_ref):
        return lax.dot_general(a_ref[...].astype(MXU_DTYPE), b_ref[...].astype(MXU_DTYPE), dims,
                               preferred_element_type=F32)

    def body_one(a_ref, b_ref, o_ref):
        o_ref[...] = prod(a_ref, b_ref).astype(o_ref.dtype)

    def body_acc(a_ref, b_ref, o_ref, acc):
        kk = pl.program_id(2)

        @pl.when(kk == 0)
        def _():
            acc[...] = prod(a_ref, b_ref)

        @pl.when(kk > 0)
        def _():
            acc[...] += prod(a_ref, b_ref)

        @pl.when(kk == nk - 1)
        def _():
            o_ref[...] = acc[...].astype(o_ref.dtype)

    a_spec = pl.BlockSpec((tk, tm), lambda i, j, kk: (kk, i)) if ta else pl.BlockSpec((tm, tk), lambda i, j, kk: (i, kk))
    b_spec = pl.BlockSpec((tn, tk), lambda i, j, kk: (j, kk)) if tb else pl.BlockSpec((tk, tn), lambda i, j, kk: (kk, j))
    return pl.pallas_call(
        body_one if nk == 1 else body_acc, name=name, grid=(m // tm, n // tn, nk),
        in_specs=[a_spec, b_spec], out_specs=pl.BlockSpec((tm, tn), lambda i, j, kk: (i, j)),
        out_shape=jax.ShapeDtypeStruct((m, n), out_dtype),
        scratch_shapes=[] if nk == 1 else [pltpu.VMEM((tm, tn), F32)],
        compiler_params=pltpu.CompilerParams(dimension_semantics=("parallel", "parallel", "arbitrary")),
    )(a, b)


def _xattn_fwd(name, q, kv, tq=512):
    bsz, length, _ = q.shape
    tq = _divisor(length, tq)
    scale = X_HD ** -0.5

    def body(q_ref, k_ref, v_ref, o_ref):
        qv, kk, vv = q_ref[...].astype(MXU_DTYPE), k_ref[...].astype(MXU_DTYPE), v_ref[...].astype(MXU_DTYPE)
        s = lax.dot_general(qv, kk, (((1,), (1,)), ((), ())), preferred_element_type=F32) * scale
        p = jnp.exp(s - jnp.max(s, axis=-1, keepdims=True))
        p = p / jnp.sum(p, axis=-1, keepdims=True)
        o_ref[...] = jnp.dot(p.astype(MXU_DTYPE), vv, preferred_element_type=F32).astype(o_ref.dtype)

    return pl.pallas_call(
        body, name=name, grid=(bsz, X_HEADS, length // tq),
        in_specs=[pl.BlockSpec((None, tq, X_HD), lambda b, h, i: (b, i, h)),
                  pl.BlockSpec((None, MEM_LEN, X_HD), lambda b, h, i: (b, 0, h)),
                  pl.BlockSpec((None, MEM_LEN, X_HD), lambda b, h, i: (b, 0, X_HEADS + h))],
        out_specs=pl.BlockSpec((None, tq, X_HD), lambda b, h, i: (b, i, h)),
        out_shape=jax.ShapeDtypeStruct(q.shape, BF16),
        compiler_params=pltpu.CompilerParams(dimension_semantics=("parallel", "parallel", "arbitrary")),
    )(q, kv, kv)


def _xattn_bwd(name, q, kv, do, tq=512):
    bsz, length, _ = q.shape
    tq = _divisor(length, tq)
    scale = X_HD ** -0.5

    def body(q_ref, k_ref, v_ref, do_ref, dq_ref, dk_ref, dv_ref):
        qv, kk, vv = q_ref[...].astype(MXU_DTYPE), k_ref[...].astype(MXU_DTYPE), v_ref[...].astype(MXU_DTYPE)
        dov = do_ref[...].astype(MXU_DTYPE)
        s = lax.dot_general(qv, kk, (((1,), (1,)), ((), ())), preferred_element_type=F32) * scale
        p = jnp.exp(s - jnp.max(s, axis=-1, keepdims=True))
        p = p / jnp.sum(p, axis=-1, keepdims=True)
        dp = lax.dot_general(dov, vv, (((1,), (1,)), ((), ())), preferred_element_type=F32)
        ds = p * (dp - jnp.sum(dp * p, axis=-1, keepdims=True)) * scale
        dsb = ds.astype(MXU_DTYPE)
        dq_ref[...] = jnp.dot(dsb, kk, preferred_element_type=F32).astype(dq_ref.dtype)
        dk = lax.dot_general(dsb, qv, (((0,), (0,)), ((), ())), preferred_element_type=F32)
        dv = lax.dot_general(p.astype(MXU_DTYPE), dov, (((0,), (0,)), ((), ())), preferred_element_type=F32)
        first = pl.program_id(2) == 0

        @pl.when(first)
        def _():
            dk_ref[...] = dk
            dv_ref[...] = dv

        @pl.when(jnp.logical_not(first))
        def _():
            dk_ref[...] += dk
            dv_ref[...] += dv

    qspec = pl.BlockSpec((None, tq, X_HD), lambda b, h, i: (b, i, h))
    kspec = pl.BlockSpec((None, MEM_LEN, X_HD), lambda b, h, i: (b, 0, h))
    return pl.pallas_call(
        body, name=name, grid=(bsz, X_HEADS, length // tq),
        in_specs=[qspec, kspec, pl.BlockSpec((None, MEM_LEN, X_HD), lambda b, h, i: (b, 0, X_HEADS + h)), qspec],
        out_specs=[qspec, kspec, kspec],
        out_shape=[jax.ShapeDtypeStruct(q.shape, BF16), jax.ShapeDtypeStruct((bsz, MEM_LEN, D_MODEL), F32),
                   jax.ShapeDtypeStruct((bsz, MEM_LEN, D_MODEL), F32)],
        compiler_params=pltpu.CompilerParams(dimension_semantics=("parallel", "parallel", "arbitrary")),
    )(q, kv, kv, do)


def _dma_sems(*counts):
    return [pltpu.SemaphoreType.DMA((max(c, 1),)) for c in counts]


def _gather_chips(name, blocks, axes):
    n = len(blocks)
    shapes = [b.shape for b in blocks]

    def body(*refs):
        ins, outs = refs[:n], refs[n:2 * n]
        lsem, ssem, rsem, fssem, frsem = refs[2 * n:]
        x, y, c = _pos()
        me = 2 * x + y

        def region(a, chip, h):
            _, r, cc = shapes[a]
            hr = r // 2
            if axes[a] == 0:
                return outs[a].at[:, pl.ds(chip * r + h * hr, hr), :]
            return outs[a].at[:, pl.ds(h * hr, hr), pl.ds(chip * cc, cc)]

        def whole(a, chip):
            _, r, cc = shapes[a]
            if axes[a] == 0:
                return outs[a].at[:, pl.ds(chip * r, r), :]
            return outs[a].at[:, :, pl.ds(chip * cc, cc)]

        started = []
        for a in range(n):
            cp = pltpu.make_async_copy(ins[a], whole(a, me), lsem.at[a])
            cp.start()
            started.append(cp)
        sends = []
        for a in range(n):
            hr = shapes[a][1] // 2
            for k, (dx, dy) in enumerate(_RELS):
                cp = pltpu.make_async_remote_copy(
                    src_ref=ins[a].at[:, pl.ds(c * hr, hr), :], dst_ref=region(a, me, c),
                    send_sem=ssem.at[3 * a + k], recv_sem=rsem.at[3 * a + k],
                    device_id=(_flip(x, dx), _flip(y, dy), c), device_id_type=MESH)
                cp.start()
                sends.append(cp)
        for a in range(n):
            for k, (dx, dy) in enumerate(_RELS):
                px, py = _flip(x, dx), _flip(y, dy)
                got = region(a, 2 * px + py, c)
                pltpu.make_async_remote_copy(
                    src_ref=got, dst_ref=got, send_sem=ssem.at[3 * a + k], recv_sem=rsem.at[3 * a + k],
                    device_id=(px, py, c), device_id_type=MESH).wait_recv()
                cp = pltpu.make_async_remote_copy(
                    src_ref=got, dst_ref=got, send_sem=fssem.at[3 * a + k], recv_sem=frsem.at[3 * a + k],
                    device_id=(x, y, 1 - c), device_id_type=MESH)
                cp.start()
                sends.append(cp)
        for a in range(n):
            for k, (dx, dy) in enumerate(_RELS):
                got = region(a, 2 * _flip(x, dx) + _flip(y, dy), 1 - c)
                pltpu.make_async_remote_copy(
                    src_ref=got, dst_ref=got, send_sem=fssem.at[3 * a + k], recv_sem=frsem.at[3 * a + k],
                    device_id=(x, y, 1 - c), device_id_type=MESH).wait_recv()
        for cp in sends:
            cp.wait_send()
        for cp in started:
            cp.wait()

    out_shape = [jax.ShapeDtypeStruct((l, 4 * r, c) if ax == 0 else (l, r, 4 * c), b.dtype)
                 for (l, r, c), ax, b in zip(shapes, axes, blocks)]
    return pl.pallas_call(
        body, name=name, in_specs=[ANY] * n, out_specs=[ANY] * n, out_shape=out_shape,
        scratch_shapes=_dma_sems(n, 3 * n, 3 * n, 3 * n, 3 * n),
    )(*blocks)


def _sibling_halves(name, parts, axes):
    n = len(parts)
    shapes = [p.shape for p in parts]
    ncopy = sum(4 if ax == 0 else 1 for ax in axes)

    def body(*refs):
        ins, mine, theirs = refs[:n], refs[n:2 * n], refs[2 * n:3 * n]
        lsem, ssem, rsem = refs[3 * n:]
        x, y, c = _pos()
        pending, j = [], 0
        for a in range(n):
            rf, _ = shapes[a]
            if axes[a] == 0:
                hr = rf // 8
                pieces = [(ins[a].at[pl.ds((2 * s + c) * hr, hr), :], ins[a].at[pl.ds((2 * s + 1 - c) * hr, hr), :],
                           mine[a].at[s], theirs[a].at[s]) for s in range(N_CHIPS)]
            else:
                hr = rf // 2
                pieces = [(ins[a].at[pl.ds(c * hr, hr), :], ins[a].at[pl.ds((1 - c) * hr, hr), :], mine[a], theirs[a])]
            for keep, give, keep_dst, give_dst in pieces:
                lc = pltpu.make_async_copy(keep, keep_dst, lsem.at[j])
                rc = pltpu.make_async_remote_copy(src_ref=give, dst_ref=give_dst, send_sem=ssem.at[j],
                                                  recv_sem=rsem.at[j], device_id=(x, y, 1 - c), device_id_type=MESH)
                lc.start()
                rc.start()
                pending += [lc, rc]
                j += 1
        for cp in pending:
            cp.wait()

    def half_shape(s, ax):
        return (N_CHIPS, s[0] // 8, s[1]) if ax == 0 else (s[0] // 2, s[1])

    out_shape = [jax.ShapeDtypeStruct(half_shape(s, ax), p.dtype) for s, ax, p in zip(shapes, axes, parts)] * 2
    res = pl.pallas_call(
        body, name=name, in_specs=[ANY] * n, out_specs=[ANY] * (2 * n), out_shape=out_shape,
        scratch_shapes=_dma_sems(ncopy, ncopy, ncopy),
    )(*parts)
    return res[:n], res[n:]


def _chip_exchange(name, halves, axes):
    n = len(halves)
    shapes = [h.shape for h in halves]

    def body(*refs):
        ins, outs = refs[:n], refs[n:2 * n]
        lsem, ssem, rsem = refs[2 * n:]
        x, y, c = _pos()
        me = 2 * x + y

        def part(a, chip):
            if axes[a] == 0:
                return ins[a].at[chip]
            cc = shapes[a][1] // N_CHIPS
            return ins[a].at[:, pl.ds(chip * cc, cc)]

        pending, sends = [], []
        for a in range(n):
            lc = pltpu.make_async_copy(part(a, me), outs[a].at[me], lsem.at[a])
            lc.start()
            pending.append(lc)
            for k, (dx, dy) in enumerate(_RELS):
                px, py = _flip(x, dx), _flip(y, dy)
                rc = pltpu.make_async_remote_copy(
                    src_ref=part(a, 2 * px + py), dst_ref=outs[a].at[me], send_sem=ssem.at[3 * a + k],
                    recv_sem=rsem.at[3 * a + k], device_id=(px, py, c), device_id_type=MESH)
                rc.start()
                sends.append(rc)
        for a in range(n):
            for k, (dx, dy) in enumerate(_RELS):
                px, py = _flip(x, dx), _flip(y, dy)
                got = outs[a].at[2 * px + py]
                pltpu.make_async_remote_copy(
                    src_ref=got, dst_ref=got, send_sem=ssem.at[3 * a + k], recv_sem=rsem.at[3 * a + k],
                    device_id=(px, py, c), device_id_type=MESH).wait_recv()
        for cp in sends:
            cp.wait_send()
        for cp in pending:
            cp.wait()

    def slot_shape(s, ax):
        return s if ax == 0 else (N_CHIPS, s[0], s[1] // N_CHIPS)

    out_shape = [jax.ShapeDtypeStruct(slot_shape(s, ax), h.dtype) for s, ax, h in zip(shapes, axes, halves)]
    return pl.pallas_call(
        body, name=name, in_specs=[ANY] * n, out_specs=[ANY] * n, out_shape=out_shape,
        scratch_shapes=_dma_sems(n, 3 * n, 3 * n),
    )(*halves)


def _sibling_join(name, groups):
    flat = [h for g in groups for h in g]
    n = len(flat)

    def body(*refs):
        ins, outs = refs[:n], refs[n:n + len(groups)]
        lsem, ssem, rsem = refs[n + len(groups):]
        x, y, c = _pos()
        pending, j = [], 0
        for t, g in enumerate(groups):
            for layer, h in enumerate(g):
                hr = h.shape[0]
                dst = outs[t].at[layer, pl.ds(c * hr, hr), :]
                lc = pltpu.make_async_copy(ins[j], dst, lsem.at[j])
                rc = pltpu.make_async_remote_copy(src_ref=ins[j], dst_ref=dst, send_sem=ssem.at[j], recv_sem=rsem.at[j],
                                                  device_id=(x, y, 1 - c), device_id_type=MESH)
                lc.start()
                rc.start()
                pending += [lc, rc]
                j += 1
        for cp in pending:
            cp.wait()

    out_shape = [jax.ShapeDtypeStruct((len(g), 2 * g[0].shape[0], g[0].shape[1]), g[0].dtype) for g in groups]
    return pl.pallas_call(
        body, name=name, in_specs=[ANY] * n, out_specs=[ANY] * len(groups), out_shape=out_shape,
        scratch_shapes=_dma_sems(n, n, n),
    )(*flat)


def _sibling_swap(name, v):
    def body(v_ref, o_ref, ssem, rsem):
        x, y, c = _pos()
        cp = pltpu.make_async_remote_copy(src_ref=v_ref, dst_ref=o_ref, send_sem=ssem.at[0], recv_sem=rsem.at[0],
                                          device_id=(x, y, 1 - c), device_id_type=MESH)
        cp.start()
        cp.wait()

    return pl.pallas_call(body, name=name, in_specs=[ANY], out_specs=ANY, out_shape=jax.ShapeDtypeStruct(v.shape, v.dtype),
                          scratch_shapes=_dma_sems(1, 1))(v)


def _add2(name, a, b):
    shape = a.shape
    a2, b2 = a.reshape(-1, shape[-1]), b.reshape(-1, shape[-1])
    (o,) = _rowwise(name, lambda u, v: u + v, [a2, b2], out_rows=[(shape[-1], F32)], tile=512)
    return o.reshape(shape)


def _sum_slots(name, slots):
    _, hr, c = slots.shape
    t = _divisor(hr, 256)

    def body(s0, s1, s2, s3, o_ref):
        o_ref[...] = ((s0[...] + s1[...]) + s2[...]) + s3[...]

    return pl.pallas_call(
        body, name=name, grid=(hr // t,),
        in_specs=[pl.BlockSpec((None, t, c), functools.partial(lambda k, i: (k, i, 0), k)) for k in range(N_CHIPS)],
        out_specs=pl.BlockSpec((t, c), lambda i: (i, 0)), out_shape=jax.ShapeDtypeStruct((hr, c), F32),
        compiler_params=pltpu.CompilerParams(dimension_semantics=("arbitrary",)),
    )(slots, slots, slots, slots)


def _adam_tile(w, g, m, v):
    m = ADAM_B1 * m + (1.0 - ADAM_B1) * g
    v = ADAM_B2 * v + (1.0 - ADAM_B2) * (g * g)
    m_hat = m / (1.0 - ADAM_B1 ** ADAM_STEP)
    v_hat = v / (1.0 - ADAM_B2 ** ADAM_STEP)
    delta = -ADAM_LR * (m_hat / (jnp.sqrt(v_hat) + ADAM_EPS) + ADAM_WD * w)
    return delta, m, v


def _adam(name, w, g, m, v):
    shape = w.shape
    c = shape[-1]
    flat = [t.reshape(-1, c) for t in (w, g, m, v)]
    res = _rowwise(name, _adam_tile, flat, out_rows=[(c, F32)] * 3, tile=256)
    return [r.reshape(shape) for r in res]


def _split_heads(t, n_heads):
    b, l, _ = t.shape
    return t.reshape(b, l, n_heads, -1).transpose(0, 2, 1, 3)


def _merge_heads(t):
    b, h, l, d = t.shape
    return t.transpose(0, 2, 1, 3).reshape(b, l, h * d)


def _rotary(x, pos):
    half = x.shape[-1] // 2
    inv_freq = ROPE_THETA ** (-jnp.arange(half, dtype=F32) / half)
    ang = pos.astype(F32)[:, None] * inv_freq[None, :]
    cos, sin = jnp.cos(ang), jnp.sin(ang)
    x1, x2 = x[..., :half], x[..., half:]
    return jnp.concatenate([x1 * cos - x2 * sin, x1 * sin + x2 * cos], axis=-1)


def _hgrn2(q, f_logit, i_val, g, lb, onorm_w):
    bsz, length, _ = q.shape
    nc = length // A_CHUNK
    f = lb + (1.0 - lb) * jax.nn.sigmoid(f_logit)
    log_f = jnp.log(f)
    k = 1.0 - f

    def chunked(t):
        return _split_heads(t, A_HEADS).reshape(bsz, A_HEADS, nc, A_CHUNK, -1)

    qc, kc, vc, lfc = chunked(q), chunked(k), chunked(i_val), chunked(log_f)
    b = jnp.cumsum(lfc, axis=3)
    b_last = b[:, :, :, -1:, :]
    q_dec = qc * jnp.exp(b)
    k_inv = kc * jnp.exp(-b)
    k_end = kc * jnp.exp(b_last - b)
    decay = jnp.exp(b_last[:, :, :, 0, :])
    causal = jnp.tril(jnp.ones((A_CHUNK, A_CHUNK), dtype=bool))
    scores = jnp.einsum('bhncd,bhnsd->bhncs', q_dec, k_inv)
    scores = jnp.where(causal, scores, 0.0)
    o_intra = jnp.einsum('bhncs,bhnsv->bhncv', scores, vc)

    def step(state, inp):
        qd, ke, v, dec = inp
        o = jnp.einsum('bhcd,bhdv->bhcv', qd, state)
        state = dec[..., None] * state + jnp.einsum('bhcd,bhcv->bhdv', ke, v)
        return state, o

    s0 = jnp.zeros((bsz, A_HEADS, A_DK, A_DK), F32)
    xs = (jnp.moveaxis(q_dec, 2, 0), jnp.moveaxis(k_end, 2, 0), jnp.moveaxis(vc, 2, 0), jnp.moveaxis(decay, 2, 0))
    _, o_inter = lax.scan(step, s0, xs)
    o = (o_intra + jnp.moveaxis(o_inter, 0, 2)).reshape(bsz, A_HEADS, length, A_DK)
    o = o * lax.rsqrt(jnp.mean(o * o, axis=-1, keepdims=True) + NORM_EPS)
    o = _merge_heads(o) * onorm_w
    return o * jax.nn.silu(g)


def _dilated_branch(q, k, v, dil, span):
    bsz, heads, length, hd = q.shape
    ls = length // dil
    nb = -(-ls // span)
    pad = nb * span - ls

    def strided(t):
        t = t.reshape(bsz, heads, ls, dil, hd).transpose(0, 1, 3, 2, 4)
        t = jnp.pad(t, ((0, 0), (0, 0), (0, 0), (0, pad), (0, 0)))
        return t.reshape(bsz, heads, dil, nb, span, hd)

    def with_prev(t):
        prev = jnp.pad(t[:, :, :, :-1], ((0, 0), (0, 0), (0, 0), (1, 0), (0, 0), (0, 0)))
        return jnp.concatenate([prev, t], axis=4)

    qb = strided(q)
    kw, vw = with_prev(strided(k)), with_prev(strided(v))
    s = jnp.einsum('bhrnqd,bhrnkd->bhrnqk', qb, kw) * (hd ** -0.5)
    iq = jnp.arange(span)[:, None]
    ik = jnp.arange(2 * span)[None, :]
    delta = span + iq - ik
    kpos = (jnp.arange(nb)[:, None, None] - 1) * span + ik[None]
    mask = (delta >= 0) & (delta <= span) & (kpos >= 0)
    s = jnp.where(mask, s, -jnp.inf)
    m = jnp.max(s, axis=-1, keepdims=True)
    p = jnp.exp(s - m)
    l = jnp.sum(p, axis=-1)
    o = jnp.einsum('bhrnqk,bhrnkd->bhrnqd', p, vw) / l[..., None]
    lse = m[..., 0] + jnp.log(l)

    def unstride(t):
        t = t.reshape(bsz, heads, dil, nb * span, -1)[:, :, :, :ls]
        return t.transpose(0, 1, 3, 2, 4).reshape(bsz, heads, length, -1)

    return unstride(o), unstride(lse[..., None])[..., 0]


def _dilated_attention(q, k, v):
    outs, lses = [], []
    for window, dil in B_DILATIONS:
        o, lse = _dilated_branch(q, k, v, dil, window // dil)
        outs.append(o)
        lses.append(lse)
    wts = jax.nn.softmax(jnp.stack(lses), axis=0)
    return jnp.einsum('gbhl,gbhld->bhld', wts, jnp.stack(outs))


def _mix_ab_core(z, lb, onorm_w):
    pos = jnp.arange(z.shape[1], dtype=jnp.int32)
    offs = np.cumsum([A_WIDTH] * 4 + [B_WIDTH] * 2).tolist()
    qa, fa, ia, ga, qb, kb, vb = jnp.split(z, offs, axis=-1)
    oa = _hgrn2(qa, fa, ia, ga, lb, onorm_w)
    qh = _rotary(_split_heads(qb, B_HEADS), pos)
    kh = _rotary(_split_heads(kb, B_HEADS), pos)
    vh = _split_heads(vb, B_HEADS)
    ob = _merge_heads(_dilated_attention(qh, kh, vh))
    return jnp.concatenate([oa, ob], axis=-1)


S5_SEG = 8
S5_W = 256
S5_LANES = C_GROUPS * C_STATE


def _seg_permute(t, bsz):
    n, c = t.shape
    return t.reshape(bsz, S5_SEG, n // bsz // S5_SEG, c).transpose(0, 2, 1, 3).reshape(n, c)


def _seg_unpermute(t, bsz):
    n, c = t.shape
    return t.reshape(bsz, n // bsz // S5_SEG, S5_SEG, c).transpose(0, 2, 1, 3).reshape(n, c)


def _s5_weights(lam_re, lam_im, log_dt, b_re, b_im, c_re, c_im):
    lr = jnp.minimum(lam_re, C_MIN_NEG_RE)
    li = lam_im
    dt = jnp.exp(log_dt)[:, None]
    mag = jnp.exp(dt * lr)
    ar, ai = mag * jnp.cos(dt * li), mag * jnp.sin(dt * li)
    den = lr * lr + li * li
    zr = ((ar - 1.0) * lr + ai * li) / den
    zi = (ai * lr - (ar - 1.0) * li) / den
    bbr = zr[..., None] * b_re - zi[..., None] * b_im
    bbi = zr[..., None] * b_im + zi[..., None] * b_re
    eye = jnp.eye(C_GROUPS, dtype=F32)
    nblk = S5_LANES // S5_W
    wb = jnp.einsum('gh,rhpc->gcrhp', eye, jnp.stack([bbr, bbi]))
    wb = wb.reshape(D_MODEL, 2, nblk, S5_W).transpose(0, 2, 1, 3).reshape(D_MODEL, 2 * S5_LANES)
    wc = jnp.einsum('gh,rhcp->rhpgc', eye, jnp.stack([c_re, -c_im]))
    wc = wc.reshape(2, nblk, S5_W, D_MODEL).transpose(1, 0, 2, 3).reshape(2 * S5_LANES, D_MODEL)
    return ar.reshape(1, S5_LANES), ai.reshape(1, S5_LANES), wb, wc


def _s5_scan(name, bu, a_re, a_im, bsz, reverse):
    n, width = bu.shape
    length = n // bsz
    steps = length // S5_SEG
    assert steps & (steps - 1) == 0
    w = S5_W

    def body(bu_ref, ar_ref, ai_ref, x_ref):
        ar = jnp.broadcast_to(ar_ref[...], (S5_SEG, w))
        ai = jnp.broadcast_to(ai_ref[...], (S5_SEG, w))
        if reverse:
            ai = -ai
        zero = jnp.zeros((S5_SEG, w), F32)

        def rows_of(j):
            jj = steps - 1 - j if reverse else j
            return pl.ds(pl.multiple_of(jj * S5_SEG, S5_SEG), S5_SEG)

        def local_step(j, st):
            sr, si = st
            rows = rows_of(j)
            nr = ar * sr - ai * si + bu_ref[rows, 0:w]
            ni = ar * si + ai * sr + bu_ref[rows, w:2 * w]
            x_ref[rows, 0:w] = nr
            x_ref[rows, w:2 * w] = ni
            return nr, ni

        er, ei = lax.fori_loop(0, steps, local_step, (zero, zero))
        pr, pi = ar[0:1], ai[0:1]
        for _ in range(steps.bit_length() - 1):
            pr, pi = pr * pr - pi * pi, 2.0 * pr * pi
        row = lax.broadcasted_iota(jnp.int32, (S5_SEG, w), 0)
        cr, ci = zero, zero
        inr, ini = jnp.zeros((1, w), F32), jnp.zeros((1, w), F32)
        order = list(range(S5_SEG))[::-1] if reverse else list(range(S5_SEG))
        for idx, s in enumerate(order):
            if idx:
                cr = jnp.where(row == s, inr, cr)
                ci = jnp.where(row == s, ini, ci)
            inr, ini = er[s:s + 1] + pr * inr - pi * ini, ei[s:s + 1] + pr * ini + pi * inr

        def carry_step(j, st):
            qr, qi = st
            rows = rows_of(j)
            x_ref[rows, 0:w] += qr * cr - qi * ci
            x_ref[rows, w:2 * w] += qr * ci + qi * cr
            return qr * ar - qi * ai, qr * ai + qi * ar

        lax.fori_loop(0, steps, carry_step, (ar, ai))

    blk = pl.BlockSpec((length, 2 * w), lambda b, j: (b, j))
    aspec = pl.BlockSpec((1, w), lambda b, j: (0, j))
    return pl.pallas_call(
        body, name=name, grid=(bsz, width // (2 * w)), in_specs=[blk, aspec, aspec], out_specs=blk,
        out_shape=jax.ShapeDtypeStruct(bu.shape, F32),
        compiler_params=pltpu.CompilerParams(dimension_semantics=("parallel", "parallel")),
    )(bu, a_re, a_im)


def _s5_da(name, x, g, bsz):
    n, width = x.shape
    length = n // bsz
    steps = length // S5_SEG
    w = S5_W

    def body(x_ref, g_ref, o_ref):
        row = lax.broadcasted_iota(jnp.int32, (S5_SEG, w), 0)
        last = pl.ds((steps - 1) * S5_SEG, S5_SEG)
        xpr = jnp.where(row == 0, 0.0, pltpu.roll(x_ref[last, 0:w], 1, 0))
        xpi = jnp.where(row == 0, 0.0, pltpu.roll(x_ref[last, w:2 * w], 1, 0))
        zero = jnp.zeros((S5_SEG, w), F32)

        def step(j, st):
            pr, pi, accr, acci = st
            rows = pl.ds(pl.multiple_of(j * S5_SEG, S5_SEG), S5_SEG)
            gr, gi = g_ref[rows, 0:w], g_ref[rows, w:2 * w]
            return x_ref[rows, 0:w], x_ref[rows, w:2 * w], accr + gr * pr + gi * pi, acci + gi * pr - gr * pi

        _, _, accr, acci = lax.fori_loop(0, steps, step, (xpr, xpi, zero, zero))
        first = pl.program_id(1) == 0

        @pl.when(first)
        def _():
            o_ref[:, 0:w] = accr
            o_ref[:, w:2 * w] = acci

        @pl.when(jnp.logical_not(first))
        def _():
            o_ref[:, 0:w] += accr
            o_ref[:, w:2 * w] += acci

    blk = pl.BlockSpec((length, 2 * w), lambda j, b: (b, j))
    return pl.pallas_call(
        body, name=name, grid=(width // (2 * w), bsz), in_specs=[blk, blk],
        out_specs=pl.BlockSpec((S5_SEG, 2 * w), lambda j, b: (0, j)), out_shape=jax.ShapeDtypeStruct((S5_SEG, width), F32),
        compiler_params=pltpu.CompilerParams(dimension_semantics=("parallel", "arbitrary")),
    )(x, g)


def _gelu(y):
    return 0.5 * y * (1.0 + lax.erf(y * math.sqrt(0.5)))


def _gelu_grad(y):
    return 0.5 * (1.0 + lax.erf(y * math.sqrt(0.5))) + y * jnp.exp(-0.5 * y * y) * (1.0 / math.sqrt(2.0 * math.pi))


def _s5_fwd(h, params, d_skip, bsz):
    (a_re, a_im, wb, wc), w_vjp = jax.vjp(_s5_weights, *params)
    wb, wc = wb.astype(BF16), wc.astype(BF16)
    hp = _seg_permute(h, bsz)
    bu = _mm("s5_bu", hp, wb)
    xs = _s5_scan("s5_scan_f", bu, a_re, a_im, bsz, False)
    yc = _mm("s5_cx", xs, wc)
    ypre, glp = _rowwise("s5_gelu", lambda yy, uu, dd: (lambda t: (t, _gelu(t)))(yy + dd * uu), [yc, hp], [d_skip],
                         out_rows=[(D_MODEL, F32), (D_MODEL, BF16)])
    return _seg_unpermute(glp, bsz), dict(hp=hp, xs=xs, ypre=ypre, a_re=a_re, a_im=a_im, wb=wb, wc=wc, w_vjp=w_vjp)


def _s5_bwd(dgl, sv, d_skip, bsz):
    dyp, dskip, dd = _rowwise(
        "b_s5_gelu", lambda dg, yy, uu, ds: (lambda t: (t, t * ds, jnp.sum(t * uu, axis=0, keepdims=True)))(dg * _gelu_grad(yy)),
        [_seg_permute(dgl, bsz), sv["ypre"], sv["hp"]], [d_skip], out_rows=[(D_MODEL, BF16), (D_MODEL, F32)],
        out_sums=[D_MODEL])
    dxh = _mm("b_s5_cx_dx", dyp, sv["wc"], tb=True)
    dwc = _mm("b_s5_cx_dw", sv["xs"], dyp, ta=True)
    gs = _s5_scan("s5_scan_b", dxh, sv["a_re"], sv["a_im"], bsz, True)
    da = _s5_da("s5_da", sv["xs"], gs, bsz)
    du = _mm("b_s5_bu_dx", gs, sv["wb"], tb=True)
    dwb = _mm("b_s5_bu_dw", sv["hp"], gs, ta=True)
    da = jnp.sum(da, axis=0).reshape(S5_LANES // S5_W, 2, S5_W)
    dp = sv["w_vjp"]((da[:, 0].reshape(1, S5_LANES), da[:, 1].reshape(1, S5_LANES), dwb, dwc))
    return _seg_unpermute(du + dskip, bsz), dp, dd


def _pack_rows(arrays):
    rows = []
    for a in arrays:
        flat = a.reshape(-1).astype(F32)
        pad = (-flat.shape[0]) % PACK_COLS
        rows.append(jnp.pad(flat, (0, pad)).reshape(-1, PACK_COLS))
    out = jnp.concatenate(rows, axis=0)
    return jnp.pad(out, ((0, (-out.shape[0]) % 16), (0, 0)))


def _unpack_rows(packed, shapes):
    out, r = [], 0
    for s in shapes:
        size = int(np.prod(s))
        nr = -(-size // PACK_COLS)
        out.append(packed[r:r + nr].reshape(-1)[:size].reshape(s))
        r += nr
    return out


def kernel(x, mem, norm_w, mem_norm_w, ab_w_in, ab_w_out, hgrn_lb_logits, hgrn_out_norm_w, s5_lambda_re, s5_lambda_im, s5_log_dt, s5_b_re, s5_b_im, s5_c_re, s5_c_im, s5_d, s5_w_glu, xattn_wq, xattn_wkv, xattn_wo, ffn_w_in, ffn_w_out, loss_target, m_norm_w, m_mem_norm_w, m_ab_w_in, m_ab_w_out, m_hgrn_lb_logits, m_hgrn_out_norm_w, m_s5_lambda_re, m_s5_lambda_im, m_s5_log_dt, m_s5_b_re, m_s5_b_im, m_s5_c_re, m_s5_c_im, m_s5_d, m_s5_w_glu, m_xattn_wq, m_xattn_wkv, m_xattn_wo, m_ffn_w_in, m_ffn_w_out, v_norm_w, v_mem_norm_w, v_ab_w_in, v_ab_w_out, v_hgrn_lb_logits, v_hgrn_out_norm_w, v_s5_lambda_re, v_s5_lambda_im, v_s5_log_dt, v_s5_b_re, v_s5_b_im, v_s5_c_re, v_s5_c_im, v_s5_d, v_s5_w_glu, v_xattn_wq, v_xattn_wkv, v_xattn_wo, v_ffn_w_in, v_ffn_w_out):
    given = dict(locals())
    w = {n: given[n] for n in WEIGHTS}
    mom = {n: given["m_" + n] for n in WEIGHTS}
    var = {n: given["v_" + n] for n in WEIGHTS}
    bsz, length, _ = x.shape
    ntok = bsz * length
    chip = 2 * lax.axis_index("x") + lax.axis_index("y")

    big_axes = [ax for _, ax in BIG]
    full = _gather_chips("gather_weights", [w[n].astype(BF16) for n in BIG_NAMES], big_axes)
    wf = dict(zip(BIG_NAMES, full))
    small_block = jnp.concatenate([w['norm_w'].reshape(12, -1), w['s5_d'].reshape(1, -1), jnp.zeros((3, 256), F32)], axis=0)
    (small_full,) = _gather_chips("gather_norm_w", [small_block[None]], [1])
    nw = small_full[0, :12].reshape(2, 6, 1, D_MODEL)
    s5_d_full = small_full[0, 12:13]

    lb_table, lb_vjp = jax.vjp(lambda t: jnp.cumsum(jax.nn.softmax(t, axis=0), axis=0), w['hgrn_lb_logits'])
    xs = x.reshape(ntok, D_MODEL)
    mem2 = mem.reshape(bsz * MEM_LEN, D_MODEL)
    tgt = loss_target.reshape(ntok, D_MODEL)
    saved = []
    (h,) = _rowwise("norm_in", lambda a, g: _rms(a, g), [xs], [nw[0, 0]], out_rows=[(D_MODEL, BF16)])
    cur = xs
    for layer in range(2):
        sv = {"x": cur}
        if layer == 0:
            z = _mm("ab_in", h, wf['ab_w_in'][0])
            sv["h0"] = h
            core, sv["mix_vjp"] = jax.vjp(_mix_ab_core, z.reshape(bsz, length, -1), lb_table[0], w['hgrn_out_norm_w'][0])
            core = core.reshape(ntok, D_MODEL)
            sv["core"] = core
            y = _mm("ab_out", core, wf['ab_w_out'][0])
        else:
            s5p = [w[n][0] for n in ('s5_lambda_re', 's5_lambda_im', 's5_log_dt', 's5_b_re', 's5_b_im', 's5_c_re', 's5_c_im')]
            gl, sv["s5"] = _s5_fwd(h, s5p, s5_d_full, bsz)
            sv["gl"] = gl
            zg = _mm("s5_glu", gl, wf['s5_w_glu'][0])
            sv["zg"] = zg
            (y,) = _rowwise("s5_gate", lambda t: t[:, :D_MODEL] * jax.nn.sigmoid(t[:, D_MODEL:]), [zg],
                            out_rows=[(D_MODEL, F32)])
        sv["y1"] = y
        x1, h2 = _rowwise(f"resnorm_a{layer}", lambda a, b, g1, g2: (lambda s: (s, _rms(s, g2)))(a + _rms(b, g1)),
                          [cur, y], [nw[layer, 1], nw[layer, 2]], out_rows=[(D_MODEL, F32), (D_MODEL, BF16)])
        sv["x1"], sv["h2"] = x1, h2
        (mem_n,) = _rowwise(f"mem_norm{layer}", lambda a, g: _rms(a, g), [mem2], [w['mem_norm_w'][layer][None]],
                            out_rows=[(D_MODEL, BF16)])
        sv["mem_n"] = mem_n
        q = _mm(f"xq{layer}", h2, wf['xattn_wq'][layer])
        kv = _mm(f"xkv{layer}", mem_n, wf['xattn_wkv'][layer])
        sv["q"], sv["kv"] = q, kv
        o = _xattn_fwd(f"xattn_f{layer}", q.reshape(bsz, length, D_MODEL), kv.reshape(bsz, MEM_LEN, 2 * D_MODEL))
        o = o.reshape(ntok, D_MODEL)
        sv["o"] = o
        y2 = _mm(f"xo{layer}", o, wf['xattn_wo'][layer])
        sv["y2"] = y2
        x2, h4 = _rowwise(f"resnorm_b{layer}", lambda a, b, g1, g2: (lambda s: (s, _rms(s, g2)))(a + _rms(b, g1)),
                          [x1, y2], [nw[layer, 3], nw[layer, 4]], out_rows=[(D_MODEL, F32), (D_MODEL, BF16)])
        sv["x2"], sv["h4"] = x2, h4
        zf = _mm(f"ffn_in{layer}", h4, wf['ffn_w_in'][layer])
        sv["zf"] = zf
        (act,) = _rowwise(f"swiglu{layer}", lambda t: _silu(t[:, :D_FF]) * t[:, D_FF:], [zf], out_rows=[(D_FF, BF16)], tile=128)
        sv["act"] = act
        y3 = _mm(f"ffn_out{layer}", act, wf['ffn_w_out'][layer])
        sv["y3"] = y3
        saved.append(sv)
        if layer == 0:
            cur, h = _rowwise("resnorm_c0", lambda a, b, g1, g2: (lambda s: (s, _rms(s, g2)))(a + _rms(b, g1)),
                              [x2, y3], [nw[0, 5], nw[1, 0]], out_rows=[(D_MODEL, F32), (D_MODEL, F32)])
    g, sq = _rowwise("loss_head", lambda a, b, t, g1: (lambda e: (e * (1.0 / D_MODEL), jnp.sum(e * e, axis=0, keepdims=True)))(a + _rms(b, g1) - t),
                     [saved[1]["x2"], saved[1]["y3"], tgt], [nw[1, 5]], out_rows=[(D_MODEL, F32)], out_sums=[D_MODEL])
    loss = lax.psum(0.5 * jnp.sum(sq) / D_MODEL, ("x", "y", "c"))

    gbig = {}
    gnw = [[None] * 6 for _ in range(2)]
    gmemnw = [None, None]
    gsmall = {}
    for layer in (1, 0):
        sv = saved[layer]
        dy3, gnw[layer][5] = _rowwise(f"b_norm5_{layer}", lambda gg, yy, g1: _rms_bwd(yy, g1, gg), [g, sv["y3"]], [nw[layer, 5]],
                                      out_rows=[(D_MODEL, BF16)], out_sums=[D_MODEL])
        dact = _mm(f"b_ffn_out_dx{layer}", dy3, wf['ffn_w_out'][layer], tb=True)
        gw_out = _mm(f"b_ffn_out_dw{layer}", sv["act"], dy3, ta=True)

        def swiglu_bwd(t, da):
            a, b = t[:, :D_FF], t[:, D_FF:]
            sg = jax.nn.sigmoid(a)
            return jnp.concatenate([da * b * (sg * (1.0 + a * (1.0 - sg))), da * (a * sg)], axis=1)

        (dzf,) = _rowwise(f"b_swiglu{layer}", swiglu_bwd, [sv["zf"], dact], out_rows=[(2 * D_FF, BF16)], tile=128)
        dh4 = _mm(f"b_ffn_in_dx{layer}", dzf, wf['ffn_w_in'][layer], tb=True)
        gw_in = _mm(f"b_ffn_in_dw{layer}", sv["h4"], dzf, ta=True)
        gbig.setdefault('ffn_w_out', {})[layer] = gw_out
        gbig.setdefault('ffn_w_in', {})[layer] = gw_in

        def resnorm_bwd(gg, dh, xx, yy, g_in, g_res):
            dx, dw_in = _rms_bwd(xx, g_in, dh)
            tot = gg + dx
            dy, dw_res = _rms_bwd(yy, g_res, tot)
            return tot, dy, dw_in, dw_res

        g, dy2, gnw[layer][4], gnw[layer][3] = _rowwise(
            f"b_resnorm_b{layer}", resnorm_bwd, [g, dh4, sv["x2"], sv["y2"]], [nw[layer, 4], nw[layer, 3]],
            out_rows=[(D_MODEL, F32), (D_MODEL, BF16)], out_sums=[D_MODEL, D_MODEL])
        do = _mm(f"b_xo_dx{layer}", dy2, wf['xattn_wo'][layer], tb=True)
        gbig.setdefault('xattn_wo', {})[layer] = _mm(f"b_xo_dw{layer}", sv["o"], dy2, ta=True)
        dq, dk, dv = _xattn_bwd(f"xattn_b{layer}", sv["q"].reshape(bsz, length, D_MODEL),
                                sv["kv"].reshape(bsz, MEM_LEN, 2 * D_MODEL), do.reshape(bsz, length, D_MODEL))
        dq = dq.reshape(ntok, D_MODEL)
        dkv = jnp.concatenate([dk, dv], axis=-1).reshape(bsz * MEM_LEN, 2 * D_MODEL)
        dh2 = _mm(f"b_xq_dx{layer}", dq, wf['xattn_wq'][layer], tb=True)
        gbig.setdefault('xattn_wq', {})[layer] = _mm(f"b_xq_dw{layer}", sv["h2"], dq, ta=True)
        dmem_n = _mm(f"b_xkv_dx{layer}", dkv, wf['xattn_wkv'][layer], tb=True)
        gbig.setdefault('xattn_wkv', {})[layer] = _mm(f"b_xkv_dw{layer}", sv["mem_n"], dkv, ta=True)
        (gmemnw[layer],) = _rowwise(f"b_mem_norm{layer}", lambda dd, mm_, g1: _rms_bwd(mm_, g1, dd)[1], [dmem_n, mem2],
                                    [w['mem_norm_w'][layer][None]], out_sums=[D_MODEL])

        g, dy1, gnw[layer][2], gnw[layer][1] = _rowwise(
            f"b_resnorm_a{layer}", resnorm_bwd, [g, dh2, sv["x1"], sv["y1"]], [nw[layer, 2], nw[layer, 1]],
            out_rows=[(D_MODEL, F32), (D_MODEL, F32 if layer == 1 else BF16)], out_sums=[D_MODEL, D_MODEL])
        if layer == 1:
            def gate_bwd(t, dd):
                a, b = t[:, :D_MODEL], t[:, D_MODEL:]
                sg = jax.nn.sigmoid(b)
                return jnp.concatenate([dd * sg, dd * a * sg * (1.0 - sg)], axis=1)

            (dzg,) = _rowwise("b_s5_gate", gate_bwd, [sv["zg"], dy1], out_rows=[(2 * D_MODEL, BF16)])
            dgl = _mm("b_s5_glu_dx", dzg, wf['s5_w_glu'][0], tb=True)
            gbig['s5_w_glu'] = {0: _mm("b_s5_glu_dw", sv["gl"], dzg, ta=True)}
            dh0, dp, gsmall['s5_d'] = _s5_bwd(dgl, sv["s5"], s5_d_full, bsz)
            for n, t in zip(('s5_lambda_re', 's5_lambda_im', 's5_log_dt', 's5_b_re', 's5_b_im', 's5_c_re', 's5_c_im'), dp):
                gsmall[n] = t[None]
            g, gnw[1][0] = _rowwise("b_norm_in1", lambda gg, dh, xx, g1: (lambda r: (gg + r[0], r[1]))(_rms_bwd(xx, g1, dh)),
                                    [g, dh0, sv["x"]], [nw[1, 0]], out_rows=[(D_MODEL, F32)], out_sums=[D_MODEL])
        else:
            dcore = _mm("b_ab_out_dx", dy1, wf['ab_w_out'][0], tb=True)
            gbig['ab_w_out'] = {0: _mm("b_ab_out_dw", sv["core"], dy1, ta=True)}
            dz, dlb0, gsmall['hgrn_out_norm_w'] = sv["mix_vjp"](dcore.reshape(bsz, length, D_MODEL))
            gsmall['hgrn_out_norm_w'] = gsmall['hgrn_out_norm_w'][None]
            (gsmall['hgrn_lb_logits'],) = lb_vjp(jnp.zeros_like(lb_table).at[0].set(dlb0))
            dz = dz.reshape(ntok, -1)
            dh0 = _mm("b_ab_in_dx", dz, wf['ab_w_in'][0], tb=True)
            gbig['ab_w_in'] = {0: _mm("b_ab_in_dw", sv["h0"], dz, ta=True)}
            grad_x, gnw[0][0] = _rowwise("b_norm_in0", lambda gg, dh, xx, g1: (lambda r: (gg + r[0], r[1]))(_rms_bwd(xx, g1, dh)),
                                         [g, dh0, sv["x"]], [nw[0, 0]], out_rows=[(D_MODEL, F32)], out_sums=[D_MODEL])
    gsmall['norm_w'] = jnp.stack([jnp.concatenate(gnw[l], axis=0) for l in range(2)])
    gsmall['mem_norm_w'] = jnp.concatenate(gmemnw, axis=0)

    packed = _pack_rows([gsmall[n] for n in SMALL])
    theirs = _sibling_swap("small_swap", packed)
    chip_sum = _add2("small_pair_sum", packed, theirs)
    (all_chips,) = _gather_chips("small_gather", [chip_sum[None]], [0])
    small_sum = _sum_slots("small_sum", all_chips.reshape(N_CHIPS, packed.shape[0], PACK_COLS))
    full_shapes = [(2, 6, D_MODEL) if n == 'norm_w' else (1, D_MODEL) if n == 's5_d' else w[n].shape for n in SMALL]
    gs = dict(zip(SMALL, _unpack_rows(small_sum, full_shapes)))
    for n in SHARDED_SMALL:
        gs[n] = lax.dynamic_slice_in_dim(gs[n], chip * 256, 256, axis=gs[n].ndim - 1)

    flat_names = [(n, l) for n, _ in BIG for l in sorted(gbig[n])]
    flat_axes = [dict(BIG)[n] for n, _ in flat_names]
    mine, theirs = _sibling_halves("grad_pair_split", [gbig[n][l] for n, l in flat_names], flat_axes)
    pair = [_add2(f"grad_pair_sum_{n}{l}", a, b) for (n, l), a, b in zip(flat_names, mine, theirs)]
    slots = _chip_exchange("grad_chip_exchange", pair, flat_axes)
    halves = [_sum_slots(f"grad_chip_sum_{n}{l}", s) for (n, l), s in zip(flat_names, slots)]
    groups = [[halves[flat_names.index((n, l))] for l in sorted(gbig[n])] for n in BIG_NAMES]
    gfull = dict(zip(BIG_NAMES, _sibling_join("grad_pair_join", groups)))

    grads, deltas, new_m, new_v = {}, {}, {}, {}
    for n in BIG_NAMES:
        grads[n] = gfull[n]
        deltas[n], new_m[n], new_v[n] = _adam("adam_" + n, w[n], gfull[n], mom[n], var[n])
    pk = [_pack_rows([t[n] for n in SMALL]) for t in (w, gs, mom, var)]
    small_out = _adam("adam_small", *pk)
    shard_shapes = [w[n].shape for n in SMALL]
    for dst, packed_out in zip((deltas, new_m, new_v), small_out):
        dst.update(zip(SMALL, _unpack_rows(packed_out, shard_shapes)))
    grads.update(gs)
    return (loss, grad_x.reshape(x.shape), *[grads[n] for n in WEIGHTS], *[deltas[n] for n in WEIGHTS],
            *[new_m[n] for n in WEIGHTS], *[new_v[n] for n in WEIGHTS])
```

```python
import functools
import math

import numpy as np
import jax
import jax.numpy as jnp
from jax import lax
from jax.experimental import pallas as pl
from jax.experimental.pallas import tpu as pltpu

F32 = jnp.float32
BF16 = jnp.bfloat16
MXU_DTYPE = jnp.bfloat16

D_MODEL = 1024
NORM_EPS = 1e-6
A_HEADS, A_DK, A_CHUNK = 4, 128, 32
A_WIDTH = A_HEADS * A_DK
B_HEADS, B_HD = 4, 128
B_WIDTH = B_HEADS * B_HD
B_DILATIONS = ((128, 1), (512, 4), (2048, 16))
ROPE_THETA = 10000.0
C_GROUP, C_GROUPS, C_STATE, C_CHUNK = 16, 64, 64, 128
C_MIN_NEG_RE = -1e-4
MEM_LEN = 256
X_HEADS = 4
X_HD = D_MODEL // X_HEADS
D_FF = 2816
ADAM_LR, ADAM_B1, ADAM_B2, ADAM_EPS, ADAM_WD, ADAM_STEP = 0.001, 0.9, 0.999, 1e-08, 0.01, 10

N_CHIPS = 4
MESH = pl.DeviceIdType.MESH
ANY = pl.BlockSpec(memory_space=pl.ANY)
_RELS = ((1, 0), (0, 1), (1, 1))

WEIGHTS = ['norm_w', 'mem_norm_w', 'ab_w_in', 'ab_w_out', 'hgrn_lb_logits', 'hgrn_out_norm_w', 's5_lambda_re',
           's5_lambda_im', 's5_log_dt', 's5_b_re', 's5_b_im', 's5_c_re', 's5_c_im', 's5_d', 's5_w_glu', 'xattn_wq',
           'xattn_wkv', 'xattn_wo', 'ffn_w_in', 'ffn_w_out']
BIG = (('ab_w_in', 1), ('ab_w_out', 0), ('s5_w_glu', 1), ('xattn_wq', 0), ('xattn_wkv', 1), ('xattn_wo', 0),
       ('ffn_w_in', 1), ('ffn_w_out', 0))
BIG_NAMES = tuple(n for n, _ in BIG)
SMALL = tuple(n for n in WEIGHTS if n not in BIG_NAMES)
SHARDED_SMALL = ('norm_w', 's5_d')
PACK_COLS = 1024


def _pos():
    return lax.axis_index("x"), lax.axis_index("y"), lax.axis_index("c")


def _flip(v, d):
    return 1 - v if d else v


def _divisor(n, want):
    for t in (want, 1024, 512, 256, 128, 64, 32, 16, 8):
        if t <= want and n % t == 0:
            return t
    return n


def _rowwise(name, fn, rows, bcasts=(), out_rows=(), out_sums=(), tile=256):
    n = rows[0].shape[0]
    t = _divisor(n, tile)
    nr, nb, no, ns = len(rows), len(bcasts), len(out_rows), len(out_sums)

    def body(*refs):
        vals = [r[...] for r in refs[:nr + nb]]
        res = fn(*vals)
        if not isinstance(res, (tuple, list)):
            res = (res,)
        outs = refs[nr + nb:]
        for k in range(no):
            outs[k][...] = res[k].astype(outs[k].dtype)
        if ns:
            first = pl.program_id(0) == 0
            for k in range(ns):
                o, val = outs[no + k], res[no + k]

                @pl.when(first)
                def _():
                    o[...] = val

                @pl.when(jnp.logical_not(first))
                def _():
                    o[...] += val

    in_specs = [pl.BlockSpec((t, r.shape[1]), lambda i: (i, 0)) for r in rows]
    in_specs += [pl.BlockSpec(b.shape, lambda i: (0, 0)) for b in bcasts]
    out_specs = [pl.BlockSpec((t, c), lambda i: (i, 0)) for c, _ in out_rows]
    out_specs += [pl.BlockSpec((1, c), lambda i: (0, 0)) for c in out_sums]
    out_shape = [jax.ShapeDtypeStruct((n, c), dt) for c, dt in out_rows]
    out_shape += [jax.ShapeDtypeStruct((1, c), F32) for c in out_sums]
    res = pl.pallas_call(
        body, name=name, grid=(n // t,), in_specs=in_specs, out_specs=out_specs, out_shape=out_shape,
        compiler_params=pltpu.CompilerParams(dimension_semantics=("arbitrary",)),
    )(*rows, *bcasts)
    return res


def _rms(x, w):
    r = lax.rsqrt(jnp.mean(x * x, axis=-1, keepdims=True) + NORM_EPS)
    return x * r * w


def _rms_bwd(x, w, dy):
    r = lax.rsqrt(jnp.mean(x * x, axis=-1, keepdims=True) + NORM_EPS)
    xh = x * r
    dxh = dy * w
    dx = r * (dxh - xh * jnp.mean(dxh * xh, axis=-1, keepdims=True))
    return dx, jnp.sum(dy * xh, axis=0, keepdims=True)


def _silu(z):
    return z * jax.nn.sigmoid(z)


def _mm(name, a, b, ta=False, tb=False, out_dtype=F32, tm=512, tn=512, tk=1024):
    m, k = a.shape[::-1] if ta else a.shape
    k2, n = b.shape[::-1] if tb else b.shape
    assert k == k2, (name, a.shape, b.shape)
    tm, tn = _divisor(m, tm), _divisor(n, tn)
    tk = k if k <= 2816 and not ta else _divisor(k, tk)
    nk = k // tk
    dims = (((0 if ta else 1,), (1 if tb else 0,)), ((), ()))

    def prod(a_ref, b_ref):
        return lax.dot_general(a_ref[...].astype(MXU_DTYPE), b_ref[...].astype(MXU_DTYPE), dims,
                               preferred_element_type=F32)

    def body_one(a_ref, b_ref, o_ref):
        o_ref[...] = prod(a_ref, b_ref).astype(o_ref.dtype)

    def body_acc(a_ref, b_ref, o_ref, acc):
        kk = pl.program_id(2)

        @pl.when(kk == 0)
        def _():
            acc[...] = prod(a_ref, b_ref)

        @pl.when(kk > 0)
        def _():
            acc[...] += prod(a_ref, b_ref)

        @pl.when(kk == nk - 1)
        def _():
            o_ref[...] = acc[...].astype(o_ref.dtype)

    a_spec = pl.BlockSpec((tk, tm), lambda i, j, kk: (kk, i)) if ta else pl.BlockSpec((tm, tk), lambda i, j, kk: (i, kk))
    b_spec = pl.BlockSpec((tn, tk), lambda i, j, kk: (j, kk)) if tb else pl.BlockSpec((tk, tn), lambda i, j, kk: (kk, j))
    return pl.pallas_call(
        body_one if nk == 1 else body_acc, name=name, grid=(m // tm, n // tn, nk),
        in_specs=[a_spec, b_spec], out_specs=pl.BlockSpec((tm, tn), lambda i, j, kk: (i, j)),
        out_shape=jax.ShapeDtypeStruct((m, n), out_dtype),
        scratch_shapes=[] if nk == 1 else [pltpu.VMEM((tm, tn), F32)],
        compiler_params=pltpu.CompilerParams(dimension_semantics=("parallel", "parallel", "arbitrary")),
    )(a, b)


def _xattn_fwd(name, q, kv, tq=512):
    bsz, length, _ = q.shape
    tq = _divisor(length, tq)
    scale = X_HD ** -0.5

    def body(q_ref, k_ref, v_ref, o_ref):
        qv, kk, vv = q_ref[...].astype(MXU_DTYPE), k_ref[...].astype(MXU_DTYPE), v_ref[...].astype(MXU_DTYPE)
        s = lax.dot_general(qv, kk, (((1,), (1,)), ((), ())), preferred_element_type=F32) * scale
        p = jnp.exp(s - jnp.max(s, axis=-1, keepdims=True))
        p = p / jnp.sum(p, axis=-1, keepdims=True)
        o_ref[...] = jnp.dot(p.astype(MXU_DTYPE), vv, preferred_element_type=F32).astype(o_ref.dtype)

    return pl.pallas_call(
        body, name=name, grid=(bsz, X_HEADS, length // tq),
        in_specs=[pl.BlockSpec((None, tq, X_HD), lambda b, h, i: (b, i, h)),
                  pl.BlockSpec((None, MEM_LEN, X_HD), lambda b, h, i: (b, 0, h)),
                  pl.BlockSpec((None, MEM_LEN, X_HD), lambda b, h, i: (b, 0, X_HEADS + h))],
        out_specs=pl.BlockSpec((None, tq, X_HD), lambda b, h, i: (b, i, h)),
        out_shape=jax.ShapeDtypeStruct(q.shape, BF16),
        compiler_params=pltpu.CompilerParams(dimension_semantics=("parallel", "parallel", "arbitrary")),
    )(q, kv, kv)


def _xattn_bwd(name, q, kv, do, tq=512):
    bsz, length, _ = q.shape
    tq = _divisor(length, tq)
    scale = X_HD ** -0.5

    def body(q_ref, k_ref, v_ref, do_ref, dq_ref, dk_ref, dv_ref):
        qv, kk, vv = q_ref[...].astype(MXU_DTYPE), k_ref[...].astype(MXU_DTYPE), v_ref[...].astype(MXU_DTYPE)
        dov = do_ref[...].astype(MXU_DTYPE)
        s = lax.dot_general(qv, kk, (((1,), (1,)), ((), ())), preferred_element_type=F32) * scale
        p = jnp.exp(s - jnp.max(s, axis=-1, keepdims=True))
        p = p / jnp.sum(p, axis=-1, keepdims=True)
        dp = lax.dot_general(dov, vv, (((1,), (1,)), ((), ())), preferred_element_type=F32)
        ds = p * (dp - jnp.sum(dp * p, axis=-1, keepdims=True)) * scale
        dsb = ds.astype(MXU_DTYPE)
        dq_ref[...] = jnp.dot(dsb, kk, preferred_element_type=F32).astype(dq_ref.dtype)
        dk = lax.dot_general(dsb, qv, (((0,), (0,)), ((), ())), preferred_element_type=F32)
        dv = lax.dot_general(p.astype(MXU_DTYPE), dov, (((0,), (0,)), ((), ())), preferred_element_type=F32)
        first = pl.program_id(2) == 0

        @pl.when(first)
        def _():
            dk_ref[...] = dk
            dv_ref[...] = dv

        @pl.when(jnp.logical_not(first))
        def _():
            dk_ref[...] += dk
            dv_ref[...] += dv

    qspec = pl.BlockSpec((None, tq, X_HD), lambda b, h, i: (b, i, h))
    kspec = pl.BlockSpec((None, MEM_LEN, X_HD), lambda b, h, i: (b, 0, h))
    return pl.pallas_call(
        body, name=name, grid=(bsz, X_HEADS, length // tq),
        in_specs=[qspec, kspec, pl.BlockSpec((None, MEM_LEN, X_HD), lambda b, h, i: (b, 0, X_HEADS + h)), qspec],
        out_specs=[qspec, kspec, kspec],
        out_shape=[jax.ShapeDtypeStruct(q.shape, BF16), jax.ShapeDtypeStruct((bsz, MEM_LEN, D_MODEL), F32),
                   jax.ShapeDtypeStruct((bsz, MEM_LEN, D_MODEL), F32)],
        compiler_params=pltpu.CompilerParams(dimension_semantics=("parallel", "parallel", "arbitrary")),
    )(q, kv, kv, do)


def _dma_sems(*counts):
    return [pltpu.SemaphoreType.DMA((max(c, 1),)) for c in counts]


def _gather_chips(name, blocks, axes):
    n = len(blocks)
    shapes = [b.shape for b in blocks]

    def body(*refs):
        ins, outs = refs[:n], refs[n:2 * n]
        lsem, lrsem, ssem, rsem, fssem, frsem = refs[2 * n:]
        x, y, c = _pos()
        me = 2 * x + y

        def region(a, chip, h):
            _, r, cc = shapes[a]
            hr = r // 2
            if axes[a] == 0:
                return outs[a].at[:, pl.ds(chip * r + h * hr, hr), :]
            return outs[a].at[:, pl.ds(h * hr, hr), pl.ds(chip * cc, cc)]

        def whole(a, chip):
            _, r, cc = shapes[a]
            if axes[a] == 0:
                return outs[a].at[:, pl.ds(chip * r, r), :]
            return outs[a].at[:, :, pl.ds(chip * cc, cc)]

        sends = []
        for a in range(n):
            cp = pltpu.make_async_remote_copy(src_ref=ins[a], dst_ref=whole(a, me), send_sem=lsem.at[a], recv_sem=lrsem.at[a],
                                              device_id=(x, y, 1 - c), device_id_type=MESH)
            cp.start()
            sends.append(cp)
        for a in range(n):
            hr = shapes[a][1] // 2
            for k, (dx, dy) in enumerate(_RELS):
                cp = pltpu.make_async_remote_copy(
                    src_ref=ins[a].at[:, pl.ds(c * hr, hr), :], dst_ref=region(a, me, c),
                    send_sem=ssem.at[3 * a + k], recv_sem=rsem.at[3 * a + k],
                    device_id=(_flip(x, dx), _flip(y, dy), c), device_id_type=MESH)
                cp.start()
                sends.append(cp)
        for a in range(n):
            for k, (dx, dy) in enumerate(_RELS):
                px, py = _flip(x, dx), _flip(y, dy)
                got = region(a, 2 * px + py, c)
                pltpu.make_async_remote_copy(
                    src_ref=got, dst_ref=got, send_sem=ssem.at[3 * a + k], recv_sem=rsem.at[3 * a + k],
                    device_id=(px, py, c), device_id_type=MESH).wait_recv()
                cp = pltpu.make_async_remote_copy(
                    src_ref=got, dst_ref=got, send_sem=fssem.at[3 * a + k], recv_sem=frsem.at[3 * a + k],
                    device_id=(x, y, 1 - c), device_id_type=MESH)
                cp.start()
                sends.append(cp)
        for a in range(n):
            for k, (dx, dy) in enumerate(_RELS):
                got = region(a, 2 * _flip(x, dx) + _flip(y, dy), 1 - c)
                pltpu.make_async_remote_copy(
                    src_ref=got, dst_ref=got, send_sem=fssem.at[3 * a + k], recv_sem=frsem.at[3 * a + k],
                    device_id=(x, y, 1 - c), device_id_type=MESH).wait_recv()
        for a in range(n):
            pltpu.make_async_remote_copy(src_ref=ins[a], dst_ref=whole(a, me), send_sem=lsem.at[a], recv_sem=lrsem.at[a],
                                         device_id=(x, y, 1 - c), device_id_type=MESH).wait_recv()
        for cp in sends:
            cp.wait_send()

    out_shape = [jax.ShapeDtypeStruct((l, 4 * r, c) if ax == 0 else (l, r, 4 * c), b.dtype)
                 for (l, r, c), ax, b in zip(shapes, axes, blocks)]
    return pl.pallas_call(
        body, name=name, in_specs=[ANY] * n, out_specs=[ANY] * n, out_shape=out_shape,
        scratch_shapes=_dma_sems(n, n, 3 * n, 3 * n, 3 * n, 3 * n),
    )(*blocks)


def _pos_vec():
    x, y, c = _pos()
    return jnp.stack([c, 2 * x + y]).astype(jnp.int32)


def _pair_send(name, parts, axes):
    n = len(parts)
    shapes = [p.shape for p in parts]
    ncopy = sum(4 if ax == 0 else 1 for ax in axes)

    def body(*refs):
        ins, theirs = refs[:n], refs[n:2 * n]
        ssem, rsem = refs[2 * n:]
        x, y, c = _pos()
        pending, j = [], 0
        for a in range(n):
            _, rf, _ = shapes[a]
            if axes[a] == 0:
                hr = rf // 8
                pieces = [(ins[a].at[:, pl.ds((2 * s + 1 - c) * hr, hr), :], theirs[a].at[:, s]) for s in range(N_CHIPS)]
            else:
                hr = rf // 2
                pieces = [(ins[a].at[:, pl.ds((1 - c) * hr, hr), :], theirs[a])]
            for give, give_dst in pieces:
                rc = pltpu.make_async_remote_copy(src_ref=give, dst_ref=give_dst, send_sem=ssem.at[j],
                                                  recv_sem=rsem.at[j], device_id=(x, y, 1 - c), device_id_type=MESH)
                rc.start()
                pending.append(rc)
                j += 1
        for cp in pending:
            cp.wait()

    def half_shape(s, ax):
        return (s[0], N_CHIPS, s[1] // 8, s[2]) if ax == 0 else (s[0], s[1] // 2, s[2])

    out_shape = [jax.ShapeDtypeStruct(half_shape(s, ax), p.dtype) for s, ax, p in zip(shapes, axes, parts)]
    return pl.pallas_call(
        body, name=name, in_specs=[ANY] * n, out_specs=[ANY] * n, out_shape=out_shape,
        scratch_shapes=_dma_sems(ncopy, ncopy),
    )(*parts)


def _chip_exchange(name, halves, axes):
    n = len(halves)
    shapes = [h.shape for h in halves]

    def body(*refs):
        ins, outs = refs[:n], refs[n:2 * n]
        ssem, rsem = refs[2 * n:]
        x, y, c = _pos()

        def part(a, chip):
            if axes[a] == 0:
                return ins[a].at[:, chip]
            cc = shapes[a][2] // N_CHIPS
            return ins[a].at[:, :, pl.ds(chip * cc, cc)]

        sends = []
        for a in range(n):
            for k, (dx, dy) in enumerate(_RELS):
                px, py = _flip(x, dx), _flip(y, dy)
                rc = pltpu.make_async_remote_copy(
                    src_ref=part(a, 2 * px + py), dst_ref=outs[a].at[:, k], send_sem=ssem.at[3 * a + k],
                    recv_sem=rsem.at[3 * a + k], device_id=(px, py, c), device_id_type=MESH)
                rc.start()
                sends.append(rc)
        for cp in sends:
            cp.wait()

    def slot_shape(s, ax):
        return (s[0], 3, s[2], s[3]) if ax == 0 else (s[0], 3, s[1], s[2] // N_CHIPS)

    out_shape = [jax.ShapeDtypeStruct(slot_shape(s, ax), h.dtype) for s, ax, h in zip(shapes, axes, halves)]
    return pl.pallas_call(
        body, name=name, in_specs=[ANY] * n, out_specs=[ANY] * n, out_shape=out_shape,
        scratch_shapes=_dma_sems(3 * n, 3 * n),
    )(*halves)


def _pair_join(name, shards):
    n = len(shards)

    def body(*refs):
        outs = refs[n:2 * n]
        ssem, rsem = refs[2 * n:]
        x, y, c = _pos()
        pending = []
        for a in range(n):
            hr = shards[a].shape[1] // 2
            mine = outs[a].at[:, pl.ds(c * hr, hr), :]
            rc = pltpu.make_async_remote_copy(src_ref=mine, dst_ref=mine, send_sem=ssem.at[a], recv_sem=rsem.at[a],
                                              device_id=(x, y, 1 - c), device_id_type=MESH)
            rc.start()
            pending.append(rc)
        for a in range(n):
            hr = shards[a].shape[1] // 2
            got = outs[a].at[:, pl.ds((1 - c) * hr, hr), :]
            pltpu.make_async_remote_copy(src_ref=got, dst_ref=got, send_sem=ssem.at[a], recv_sem=rsem.at[a],
                                         device_id=(x, y, 1 - c), device_id_type=MESH).wait_recv()
        for cp in pending:
            cp.wait_send()

    return pl.pallas_call(
        body, name=name, in_specs=[ANY] * n, out_specs=[ANY] * n,
        out_shape=[jax.ShapeDtypeStruct(s.shape, s.dtype) for s in shards],
        input_output_aliases={a: a for a in range(n)}, scratch_shapes=_dma_sems(n, n),
    )(*shards)


def _pair_add(name, part, theirs, axis, pos):
    layers, rf, cf = part.shape

    def body(pos_ref, a_ref, b_ref, o_ref):
        o_ref[...] = a_ref[...] + b_ref[...]

    if axis == 0:
        hr = rf // 8
        grid = (layers, N_CHIPS)
        in_specs = [pl.BlockSpec((None, hr, cf), lambda l, s, p: (l, 2 * s + p[0], 0)),
                    pl.BlockSpec((None, None, hr, cf), lambda l, s, p: (l, s, 0, 0))]
        out_spec = pl.BlockSpec((None, None, hr, cf), lambda l, s, p: (l, s, 0, 0))
    else:
        hr, t = rf // 2, 128
        grid = (layers, hr // t)
        in_specs = [pl.BlockSpec((None, t, cf), lambda l, i, p: (l, p[0] * (hr // t) + i, 0)),
                    pl.BlockSpec((None, t, cf), lambda l, i, p: (l, i, 0))]
        out_spec = pl.BlockSpec((None, t, cf), lambda l, i, p: (l, i, 0))
    return pl.pallas_call(
        body, name=name, out_shape=jax.ShapeDtypeStruct(theirs.shape, F32),
        grid_spec=pltpu.PrefetchScalarGridSpec(num_scalar_prefetch=1, grid=grid, in_specs=in_specs, out_specs=out_spec),
        compiler_params=pltpu.CompilerParams(dimension_semantics=("arbitrary", "arbitrary")),
    )(pos, part, theirs)


def _chip_sum(name, half, slots, axis, pos):
    layers, _, hr, c = slots.shape

    def body(pos_ref, own, s0, s1, s2, o_ref):
        o_ref[...] = ((own[...] + s0[...]) + s1[...]) + s2[...]

    t = hr if axis == 0 else 128
    if axis == 0:
        own_spec = pl.BlockSpec((None, None, t, c), lambda l, i, p: (l, p[1], i, 0))
    else:
        own_spec = pl.BlockSpec((None, t, c), lambda l, i, p: (l, i, p[1]))
    slot_specs = [pl.BlockSpec((None, None, t, c), functools.partial(lambda k, l, i, p: (l, k, i, 0), k)) for k in range(3)]
    return pl.pallas_call(
        body, name=name, out_shape=jax.ShapeDtypeStruct((layers, 2 * hr, c), F32),
        grid_spec=pltpu.PrefetchScalarGridSpec(
            num_scalar_prefetch=1, grid=(layers, hr // t), in_specs=[own_spec] + slot_specs,
            out_specs=pl.BlockSpec((None, t, c), lambda l, i, p: (l, p[0] * (hr // t) + i, 0))),
        compiler_params=pltpu.CompilerParams(dimension_semantics=("arbitrary", "arbitrary")),
    )(pos, half, slots, slots, slots)


def _sibling_swap(name, v):
    def body(v_ref, o_ref, ssem, rsem):
        x, y, c = _pos()
        cp = pltpu.make_async_remote_copy(src_ref=v_ref, dst_ref=o_ref, send_sem=ssem.at[0], recv_sem=rsem.at[0],
                                          device_id=(x, y, 1 - c), device_id_type=MESH)
        cp.start()
        cp.wait()

    return pl.pallas_call(body, name=name, in_specs=[ANY], out_specs=ANY, out_shape=jax.ShapeDtypeStruct(v.shape, v.dtype),
                          scratch_shapes=_dma_sems(1, 1))(v)


def _add2(name, a, b):
    shape = a.shape
    a2, b2 = a.reshape(-1, shape[-1]), b.reshape(-1, shape[-1])
    (o,) = _rowwise(name, lambda u, v: u + v, [a2, b2], out_rows=[(shape[-1], F32)], tile=512)
    return o.reshape(shape)


def _sum_slots(name, slots):
    _, hr, c = slots.shape
    t = _divisor(hr, 256)

    def body(s0, s1, s2, s3, o_ref):
        o_ref[...] = ((s0[...] + s1[...]) + s2[...]) + s3[...]

    return pl.pallas_call(
        body, name=name, grid=(hr // t,),
        in_specs=[pl.BlockSpec((None, t, c), functools.partial(lambda k, i: (k, i, 0), k)) for k in range(N_CHIPS)],
        out_specs=pl.BlockSpec((t, c), lambda i: (i, 0)), out_shape=jax.ShapeDtypeStruct((hr, c), F32),
        compiler_params=pltpu.CompilerParams(dimension_semantics=("arbitrary",)),
    )(slots, slots, slots, slots)


def _adam_tile(w, g, m, v):
    m = ADAM_B1 * m + (1.0 - ADAM_B1) * g
    v = ADAM_B2 * v + (1.0 - ADAM_B2) * (g * g)
    m_hat = m / (1.0 - ADAM_B1 ** ADAM_STEP)
    v_hat = v / (1.0 - ADAM_B2 ** ADAM_STEP)
    delta = -ADAM_LR * (m_hat / (jnp.sqrt(v_hat) + ADAM_EPS) + ADAM_WD * w)
    return delta, m, v


def _adam(name, w, g, m, v):
    shape = w.shape
    c = shape[-1]
    flat = [t.reshape(-1, c) for t in (w, g, m, v)]
    res = _rowwise(name, _adam_tile, flat, out_rows=[(c, F32)] * 3, tile=256)
    return [r.reshape(shape) for r in res]


def _split_heads(t, n_heads):
    b, l, _ = t.shape
    return t.reshape(b, l, n_heads, -1).transpose(0, 2, 1, 3)


def _merge_heads(t):
    b, h, l, d = t.shape
    return t.transpose(0, 2, 1, 3).reshape(b, l, h * d)


def _rotary(x, pos):
    half = x.shape[-1] // 2
    inv_freq = ROPE_THETA ** (-jnp.arange(half, dtype=F32) / half)
    ang = pos.astype(F32)[:, None] * inv_freq[None, :]
    cos, sin = jnp.cos(ang), jnp.sin(ang)
    x1, x2 = x[..., :half], x[..., half:]
    return jnp.concatenate([x1 * cos - x2 * sin, x1 * sin + x2 * cos], axis=-1)


def _hgrn2(q, f_logit, i_val, g, lb, onorm_w):
    bsz, length, _ = q.shape
    nc = length // A_CHUNK
    f = lb + (1.0 - lb) * jax.nn.sigmoid(f_logit)
    log_f = jnp.log(f)
    k = 1.0 - f

    def chunked(t):
        return _split_heads(t, A_HEADS).reshape(bsz, A_HEADS, nc, A_CHUNK, -1)

    qc, kc, vc, lfc = chunked(q), chunked(k), chunked(i_val), chunked(log_f)
    b = jnp.cumsum(lfc, axis=3)
    b_last = b[:, :, :, -1:, :]
    q_dec = qc * jnp.exp(b)
    k_inv = kc * jnp.exp(-b)
    k_end = kc * jnp.exp(b_last - b)
    decay = jnp.exp(b_last[:, :, :, 0, :])
    causal = jnp.tril(jnp.ones((A_CHUNK, A_CHUNK), dtype=bool))
    scores = jnp.einsum('bhncd,bhnsd->bhncs', q_dec, k_inv)
    scores = jnp.where(causal, scores, 0.0)
    o_intra = jnp.einsum('bhncs,bhnsv->bhncv', scores, vc)

    def step(state, inp):
        qd, ke, v, dec = inp
        o = jnp.einsum('bhcd,bhdv->bhcv', qd, state)
        state = dec[..., None] * state + jnp.einsum('bhcd,bhcv->bhdv', ke, v)
        return state, o

    s0 = jnp.zeros((bsz, A_HEADS, A_DK, A_DK), F32)
    xs = (jnp.moveaxis(q_dec, 2, 0), jnp.moveaxis(k_end, 2, 0), jnp.moveaxis(vc, 2, 0), jnp.moveaxis(decay, 2, 0))
    _, o_inter = lax.scan(step, s0, xs)
    o = (o_intra + jnp.moveaxis(o_inter, 0, 2)).reshape(bsz, A_HEADS, length, A_DK)
    o = o * lax.rsqrt(jnp.mean(o * o, axis=-1, keepdims=True) + NORM_EPS)
    o = _merge_heads(o) * onorm_w
    return o * jax.nn.silu(g)


def _dilated_branch(q, k, v, dil, span):
    bsz, heads, length, hd = q.shape
    ls = length // dil
    nb = -(-ls // span)
    pad = nb * span - ls

    def strided(t):
        t = t.reshape(bsz, heads, ls, dil, hd).transpose(0, 1, 3, 2, 4)
        t = jnp.pad(t, ((0, 0), (0, 0), (0, 0), (0, pad), (0, 0)))
        return t.reshape(bsz, heads, dil, nb, span, hd)

    def with_prev(t):
        prev = jnp.pad(t[:, :, :, :-1], ((0, 0), (0, 0), (0, 0), (1, 0), (0, 0), (0, 0)))
        return jnp.concatenate([prev, t], axis=4)

    qb = strided(q)
    kw, vw = with_prev(strided(k)), with_prev(strided(v))
    s = jnp.einsum('bhrnqd,bhrnkd->bhrnqk', qb, kw) * (hd ** -0.5)
    iq = jnp.arange(span)[:, None]
    ik = jnp.arange(2 * span)[None, :]
    delta = span + iq - ik
    kpos = (jnp.arange(nb)[:, None, None] - 1) * span + ik[None]
    mask = (delta >= 0) & (delta <= span) & (kpos >= 0)
    s = jnp.where(mask, s, -jnp.inf)
    m = jnp.max(s, axis=-1, keepdims=True)
    p = jnp.exp(s - m)
    l = jnp.sum(p, axis=-1)
    o = jnp.einsum('bhrnqk,bhrnkd->bhrnqd', p, vw) / l[..., None]
    lse = m[..., 0] + jnp.log(l)

    def unstride(t):
        t = t.reshape(bsz, heads, dil, nb * span, -1)[:, :, :, :ls]
        return t.transpose(0, 1, 3, 2, 4).reshape(bsz, heads, length, -1)

    return unstride(o), unstride(lse[..., None])[..., 0]


def _dilated_attention(q, k, v):
    outs, lses = [], []
    for window, dil in B_DILATIONS:
        o, lse = _dilated_branch(q, k, v, dil, window // dil)
        outs.append(o)
        lses.append(lse)
    wts = jax.nn.softmax(jnp.stack(lses), axis=0)
    return jnp.einsum('gbhl,gbhld->bhld', wts, jnp.stack(outs))


def _mix_ab_core(z, lb, onorm_w):
    pos = jnp.arange(z.shape[1], dtype=jnp.int32)
    offs = np.cumsum([A_WIDTH] * 4 + [B_WIDTH] * 2).tolist()
    qa, fa, ia, ga, qb, kb, vb = jnp.split(z, offs, axis=-1)
    oa = _hgrn2(qa, fa, ia, ga, lb, onorm_w)
    qh = _rotary(_split_heads(qb, B_HEADS), pos)
    kh = _rotary(_split_heads(kb, B_HEADS), pos)
    vh = _split_heads(vb, B_HEADS)
    ob = _merge_heads(_dilated_attention(qh, kh, vh))
    return jnp.concatenate([oa, ob], axis=-1)


S5_SEG = 8
S5_W = 256
S5_LANES = C_GROUPS * C_STATE


def _seg_permute(t, bsz):
    n, c = t.shape
    return t.reshape(bsz, S5_SEG, n // bsz // S5_SEG, c).transpose(0, 2, 1, 3).reshape(n, c)


def _seg_unpermute(t, bsz):
    n, c = t.shape
    return t.reshape(bsz, n // bsz // S5_SEG, S5_SEG, c).transpose(0, 2, 1, 3).reshape(n, c)


def _s5_weights(lam_re, lam_im, log_dt, b_re, b_im, c_re, c_im):
    lr = jnp.minimum(lam_re, C_MIN_NEG_RE)
    li = lam_im
    dt = jnp.exp(log_dt)[:, None]
    mag = jnp.exp(dt * lr)
    ar, ai = mag * jnp.cos(dt * li), mag * jnp.sin(dt * li)
    den = lr * lr + li * li
    zr = ((ar - 1.0) * lr + ai * li) / den
    zi = (ai * lr - (ar - 1.0) * li) / den
    bbr = zr[..., None] * b_re - zi[..., None] * b_im
    bbi = zr[..., None] * b_im + zi[..., None] * b_re
    eye = jnp.eye(C_GROUPS, dtype=F32)
    nblk = S5_LANES // S5_W
    wb = jnp.einsum('gh,rhpc->gcrhp', eye, jnp.stack([bbr, bbi]))
    wb = wb.reshape(D_MODEL, 2, nblk, S5_W).transpose(0, 2, 1, 3).reshape(D_MODEL, 2 * S5_LANES)
    wc = jnp.einsum('gh,rhcp->rhpgc', eye, jnp.stack([c_re, -c_im]))
    wc = wc.reshape(2, nblk, S5_W, D_MODEL).transpose(1, 0, 2, 3).reshape(2 * S5_LANES, D_MODEL)
    return ar.reshape(1, S5_LANES), ai.reshape(1, S5_LANES), wb, wc


def _s5_scan(name, bu, a_re, a_im, bsz, reverse):
    n, width = bu.shape
    length = n // bsz
    steps = length // S5_SEG
    assert steps & (steps - 1) == 0
    w = S5_W

    def body(bu_ref, ar_ref, ai_ref, x_ref):
        ar = jnp.broadcast_to(ar_ref[...], (S5_SEG, w))
        ai = jnp.broadcast_to(ai_ref[...], (S5_SEG, w))
        if reverse:
            ai = -ai
        zero = jnp.zeros((S5_SEG, w), F32)

        def rows_of(j):
            jj = steps - 1 - j if reverse else j
            return pl.ds(pl.multiple_of(jj * S5_SEG, S5_SEG), S5_SEG)

        def local_step(j, st):
            sr, si = st
            rows = rows_of(j)
            nr = ar * sr - ai * si + bu_ref[rows, 0:w]
            ni = ar * si + ai * sr + bu_ref[rows, w:2 * w]
            x_ref[rows, 0:w] = nr
            x_ref[rows, w:2 * w] = ni
            return nr, ni

        er, ei = lax.fori_loop(0, steps, local_step, (zero, zero))
        pr, pi = ar[0:1], ai[0:1]
        for _ in range(steps.bit_length() - 1):
            pr, pi = pr * pr - pi * pi, 2.0 * pr * pi
        row = lax.broadcasted_iota(jnp.int32, (S5_SEG, w), 0)
        cr, ci = zero, zero
        inr, ini = jnp.zeros((1, w), F32), jnp.zeros((1, w), F32)
        order = list(range(S5_SEG))[::-1] if reverse else list(range(S5_SEG))
        for idx, s in enumerate(order):
            if idx:
                cr = jnp.where(row == s, inr, cr)
                ci = jnp.where(row == s, ini, ci)
            inr, ini = er[s:s + 1] + pr * inr - pi * ini, ei[s:s + 1] + pr * ini + pi * inr

        def carry_step(j, st):
            qr, qi = st
            rows = rows_of(j)
            x_ref[rows, 0:w] += qr * cr - qi * ci
            x_ref[rows, w:2 * w] += qr * ci + qi * cr
            return qr * ar - qi * ai, qr * ai + qi * ar

        lax.fori_loop(0, steps, carry_step, (ar, ai))

    blk = pl.BlockSpec((length, 2 * w), lambda b, j: (b, j))
    aspec = pl.BlockSpec((1, w), lambda b, j: (0, j))
    return pl.pallas_call(
        body, name=name, grid=(bsz, width // (2 * w)), in_specs=[blk, aspec, aspec], out_specs=blk,
        out_shape=jax.ShapeDtypeStruct(bu.shape, F32),
        compiler_params=pltpu.CompilerParams(dimension_semantics=("parallel", "parallel")),
    )(bu, a_re, a_im)


def _s5_da(name, x, g, bsz):
    n, width = x.shape
    length = n // bsz
    steps = length // S5_SEG
    w = S5_W

    def body(x_ref, g_ref, o_ref):
        row = lax.broadcasted_iota(jnp.int32, (S5_SEG, w), 0)
        last = pl.ds((steps - 1) * S5_SEG, S5_SEG)
        xpr = jnp.where(row == 0, 0.0, pltpu.roll(x_ref[last, 0:w], 1, 0))
        xpi = jnp.where(row == 0, 0.0, pltpu.roll(x_ref[last, w:2 * w], 1, 0))
        zero = jnp.zeros((S5_SEG, w), F32)

        def step(j, st):
            pr, pi, accr, acci = st
            rows = pl.ds(pl.multiple_of(j * S5_SEG, S5_SEG), S5_SEG)
            gr, gi = g_ref[rows, 0:w], g_ref[rows, w:2 * w]
            return x_ref[rows, 0:w], x_ref[rows, w:2 * w], accr + gr * pr + gi * pi, acci + gi * pr - gr * pi

        _, _, accr, acci = lax.fori_loop(0, steps, step, (xpr, xpi, zero, zero))
        first = pl.program_id(1) == 0

        @pl.when(first)
        def _():
            o_ref[:, 0:w] = accr
            o_ref[:, w:2 * w] = acci

        @pl.when(jnp.logical_not(first))
        def _():
            o_ref[:, 0:w] += accr
            o_ref[:, w:2 * w] += acci

    blk = pl.BlockSpec((length, 2 * w), lambda j, b: (b, j))
    return pl.pallas_call(
        body, name=name, grid=(width // (2 * w), bsz), in_specs=[blk, blk],
        out_specs=pl.BlockSpec((S5_SEG, 2 * w), lambda j, b: (0, j)), out_shape=jax.ShapeDtypeStruct((S5_SEG, width), F32),
        compiler_params=pltpu.CompilerParams(dimension_semantics=("parallel", "arbitrary")),
    )(x, g)


def _gelu(y):
    return 0.5 * y * (1.0 + lax.erf(y * math.sqrt(0.5)))


def _gelu_grad(y):
    return 0.5 * (1.0 + lax.erf(y * math.sqrt(0.5))) + y * jnp.exp(-0.5 * y * y) * (1.0 / math.sqrt(2.0 * math.pi))


def _s5_fwd(h, params, d_skip, bsz):
    (a_re, a_im, wb, wc), w_vjp = jax.vjp(_s5_weights, *params)
    wb, wc = wb.astype(BF16), wc.astype(BF16)
    hp = _seg_permute(h, bsz)
    bu = _mm("s5_bu", hp, wb)
    xs = _s5_scan("s5_scan_f", bu, a_re, a_im, bsz, False)
    yc = _mm("s5_cx", xs, wc)
    ypre, glp = _rowwise("s5_gelu", lambda yy, uu, dd: (lambda t: (t, _gelu(t)))(yy + dd * uu), [yc, hp], [d_skip],
                         out_rows=[(D_MODEL, F32), (D_MODEL, BF16)])
    return _seg_unpermute(glp, bsz), dict(hp=hp, xs=xs, ypre=ypre, a_re=a_re, a_im=a_im, wb=wb, wc=wc, w_vjp=w_vjp)


def _s5_bwd(dgl, sv, d_skip, bsz):
    dyp, dskip, dd = _rowwise(
        "b_s5_gelu", lambda dg, yy, uu, ds: (lambda t: (t, t * ds, jnp.sum(t * uu, axis=0, keepdims=True)))(dg * _gelu_grad(yy)),
        [_seg_permute(dgl, bsz), sv["ypre"], sv["hp"]], [d_skip], out_rows=[(D_MODEL, BF16), (D_MODEL, F32)],
        out_sums=[D_MODEL])
    dxh = _mm("b_s5_cx_dx", dyp, sv["wc"], tb=True)
    dwc = _mm("b_s5_cx_dw", sv["xs"], dyp, ta=True)
    gs = _s5_scan("s5_scan_b", dxh, sv["a_re"], sv["a_im"], bsz, True)
    da = _s5_da("s5_da", sv["xs"], gs, bsz)
    du = _mm("b_s5_bu_dx", gs, sv["wb"], tb=True)
    dwb = _mm("b_s5_bu_dw", sv["hp"], gs, ta=True)
    da = jnp.sum(da, axis=0).reshape(S5_LANES // S5_W, 2, S5_W)
    dp = sv["w_vjp"]((da[:, 0].reshape(1, S5_LANES), da[:, 1].reshape(1, S5_LANES), dwb, dwc))
    return _seg_unpermute(du + dskip, bsz), dp, dd


def _pack_rows(arrays):
    rows = []
    for a in arrays:
        flat = a.reshape(-1).astype(F32)
        pad = (-flat.shape[0]) % PACK_COLS
        rows.append(jnp.pad(flat, (0, pad)).reshape(-1, PACK_COLS))
    out = jnp.concatenate(rows, axis=0)
    return jnp.pad(out, ((0, (-out.shape[0]) % 16), (0, 0)))


def _unpack_rows(packed, shapes):
    out, r = [], 0
    for s in shapes:
        size = int(np.prod(s))
        nr = -(-size // PACK_COLS)
        out.append(packed[r:r + nr].reshape(-1)[:size].reshape(s))
        r += nr
    return out


def kernel(x, mem, norm_w, mem_norm_w, ab_w_in, ab_w_out, hgrn_lb_logits, hgrn_out_norm_w, s5_lambda_re, s5_lambda_im, s5_log_dt, s5_b_re, s5_b_im, s5_c_re, s5_c_im, s5_d, s5_w_glu, xattn_wq, xattn_wkv, xattn_wo, ffn_w_in, ffn_w_out, loss_target, m_norm_w, m_mem_norm_w, m_ab_w_in, m_ab_w_out, m_hgrn_lb_logits, m_hgrn_out_norm_w, m_s5_lambda_re, m_s5_lambda_im, m_s5_log_dt, m_s5_b_re, m_s5_b_im, m_s5_c_re, m_s5_c_im, m_s5_d, m_s5_w_glu, m_xattn_wq, m_xattn_wkv, m_xattn_wo, m_ffn_w_in, m_ffn_w_out, v_norm_w, v_mem_norm_w, v_ab_w_in, v_ab_w_out, v_hgrn_lb_logits, v_hgrn_out_norm_w, v_s5_lambda_re, v_s5_lambda_im, v_s5_log_dt, v_s5_b_re, v_s5_b_im, v_s5_c_re, v_s5_c_im, v_s5_d, v_s5_w_glu, v_xattn_wq, v_xattn_wkv, v_xattn_wo, v_ffn_w_in, v_ffn_w_out):
    given = dict(locals())
    w = {n: given[n] for n in WEIGHTS}
    mom = {n: given["m_" + n] for n in WEIGHTS}
    var = {n: given["v_" + n] for n in WEIGHTS}
    bsz, length, _ = x.shape
    ntok = bsz * length
    chip = 2 * lax.axis_index("x") + lax.axis_index("y")

    big_axes = [ax for _, ax in BIG]
    full = _gather_chips("gather_weights", [w[n].astype(BF16) for n in BIG_NAMES], big_axes)
    wf = dict(zip(BIG_NAMES, full))
    small_block = jnp.concatenate([w['norm_w'].reshape(12, -1), w['s5_d'].reshape(1, -1), jnp.zeros((3, 256), F32)], axis=0)
    (small_full,) = _gather_chips("gather_norm_w", [small_block[None]], [1])
    nw = small_full[0, :12].reshape(2, 6, 1, D_MODEL)
    s5_d_full = small_full[0, 12:13]

    lb_table, lb_vjp = jax.vjp(lambda t: jnp.cumsum(jax.nn.softmax(t, axis=0), axis=0), w['hgrn_lb_logits'])
    xs = x.reshape(ntok, D_MODEL)
    mem2 = mem.reshape(bsz * MEM_LEN, D_MODEL)
    tgt = loss_target.reshape(ntok, D_MODEL)
    saved = []
    (h,) = _rowwise("norm_in", lambda a, g: _rms(a, g), [xs], [nw[0, 0]], out_rows=[(D_MODEL, BF16)])
    cur = xs
    for layer in range(2):
        sv = {"x": cur}
        if layer == 0:
            z = _mm("ab_in", h, wf['ab_w_in'][0])
            sv["h0"] = h
            core, sv["mix_vjp"] = jax.vjp(_mix_ab_core, z.reshape(bsz, length, -1), lb_table[0], w['hgrn_out_norm_w'][0])
            core = core.reshape(ntok, D_MODEL)
            sv["core"] = core
            y = _mm("ab_out", core, wf['ab_w_out'][0])
        else:
            s5p = [w[n][0] for n in ('s5_lambda_re', 's5_lambda_im', 's5_log_dt', 's5_b_re', 's5_b_im', 's5_c_re', 's5_c_im')]
            gl, sv["s5"] = _s5_fwd(h, s5p, s5_d_full, bsz)
            sv["gl"] = gl
            zg = _mm("s5_glu", gl, wf['s5_w_glu'][0])
            sv["zg"] = zg
            (y,) = _rowwise("s5_gate", lambda t: t[:, :D_MODEL] * jax.nn.sigmoid(t[:, D_MODEL:]), [zg],
                            out_rows=[(D_MODEL, F32)])
        sv["y1"] = y
        x1, h2 = _rowwise(f"resnorm_a{layer}", lambda a, b, g1, g2: (lambda s: (s, _rms(s, g2)))(a + _rms(b, g1)),
                          [cur, y], [nw[layer, 1], nw[layer, 2]], out_rows=[(D_MODEL, F32), (D_MODEL, BF16)])
        sv["x1"], sv["h2"] = x1, h2
        (mem_n,) = _rowwise(f"mem_norm{layer}", lambda a, g: _rms(a, g), [mem2], [w['mem_norm_w'][layer][None]],
                            out_rows=[(D_MODEL, BF16)])
        sv["mem_n"] = mem_n
        q = _mm(f"xq{layer}", h2, wf['xattn_wq'][layer])
        kv = _mm(f"xkv{layer}", mem_n, wf['xattn_wkv'][layer])
        sv["q"], sv["kv"] = q, kv
        o = _xattn_fwd(f"xattn_f{layer}", q.reshape(bsz, length, D_MODEL), kv.reshape(bsz, MEM_LEN, 2 * D_MODEL))
        o = o.reshape(ntok, D_MODEL)
        sv["o"] = o
        y2 = _mm(f"xo{layer}", o, wf['xattn_wo'][layer])
        sv["y2"] = y2
        x2, h4 = _rowwise(f"resnorm_b{layer}", lambda a, b, g1, g2: (lambda s: (s, _rms(s, g2)))(a + _rms(b, g1)),
                          [x1, y2], [nw[layer, 3], nw[layer, 4]], out_rows=[(D_MODEL, F32), (D_MODEL, BF16)])
        sv["x2"], sv["h4"] = x2, h4
        zf = _mm(f"ffn_in{layer}", h4, wf['ffn_w_in'][layer])
        sv["zf"] = zf
        (act,) = _rowwise(f"swiglu{layer}", lambda t: _silu(t[:, :D_FF]) * t[:, D_FF:], [zf], out_rows=[(D_FF, BF16)], tile=128)
        sv["act"] = act
        y3 = _mm(f"ffn_out{layer}", act, wf['ffn_w_out'][layer])
        sv["y3"] = y3
        saved.append(sv)
        if layer == 0:
            cur, h = _rowwise("resnorm_c0", lambda a, b, g1, g2: (lambda s: (s, _rms(s, g2)))(a + _rms(b, g1)),
                              [x2, y3], [nw[0, 5], nw[1, 0]], out_rows=[(D_MODEL, F32), (D_MODEL, F32)])
    g, sq = _rowwise("loss_head", lambda a, b, t, g1: (lambda e: (e * (1.0 / D_MODEL), jnp.sum(e * e, axis=0, keepdims=True)))(a + _rms(b, g1) - t),
                     [saved[1]["x2"], saved[1]["y3"], tgt], [nw[1, 5]], out_rows=[(D_MODEL, F32)], out_sums=[D_MODEL])
    loss = lax.psum(0.5 * jnp.sum(sq) / D_MODEL, ("x", "y", "c"))

    gbig = {}
    gnw = [[None] * 6 for _ in range(2)]
    gmemnw = [None, None]
    gsmall = {}
    for layer in (1, 0):
        sv = saved[layer]
        dy3, gnw[layer][5] = _rowwise(f"b_norm5_{layer}", lambda gg, yy, g1: _rms_bwd(yy, g1, gg), [g, sv["y3"]], [nw[layer, 5]],
                                      out_rows=[(D_MODEL, BF16)], out_sums=[D_MODEL])
        dact = _mm(f"b_ffn_out_dx{layer}", dy3, wf['ffn_w_out'][layer], tb=True)
        gw_out = _mm(f"b_ffn_out_dw{layer}", sv["act"], dy3, ta=True)

        def swiglu_bwd(t, da):
            a, b = t[:, :D_FF], t[:, D_FF:]
            sg = jax.nn.sigmoid(a)
            return jnp.concatenate([da * b * (sg * (1.0 + a * (1.0 - sg))), da * (a * sg)], axis=1)

        (dzf,) = _rowwise(f"b_swiglu{layer}", swiglu_bwd, [sv["zf"], dact], out_rows=[(2 * D_FF, BF16)], tile=128)
        dh4 = _mm(f"b_ffn_in_dx{layer}", dzf, wf['ffn_w_in'][layer], tb=True)
        gw_in = _mm(f"b_ffn_in_dw{layer}", sv["h4"], dzf, ta=True)
        gbig.setdefault('ffn_w_out', {})[layer] = gw_out
        gbig.setdefault('ffn_w_in', {})[layer] = gw_in

        def resnorm_bwd(gg, dh, xx, yy, g_in, g_res):
            dx, dw_in = _rms_bwd(xx, g_in, dh)
            tot = gg + dx
            dy, dw_res = _rms_bwd(yy, g_res, tot)
            return tot, dy, dw_in, dw_res

        g, dy2, gnw[layer][4], gnw[layer][3] = _rowwise(
            f"b_resnorm_b{layer}", resnorm_bwd, [g, dh4, sv["x2"], sv["y2"]], [nw[layer, 4], nw[layer, 3]],
            out_rows=[(D_MODEL, F32), (D_MODEL, BF16)], out_sums=[D_MODEL, D_MODEL])
        do = _mm(f"b_xo_dx{layer}", dy2, wf['xattn_wo'][layer], tb=True)
        gbig.setdefault('xattn_wo', {})[layer] = _mm(f"b_xo_dw{layer}", sv["o"], dy2, ta=True)
        dq, dk, dv = _xattn_bwd(f"xattn_b{layer}", sv["q"].reshape(bsz, length, D_MODEL),
                                sv["kv"].reshape(bsz, MEM_LEN, 2 * D_MODEL), do.reshape(bsz, length, D_MODEL))
        dq = dq.reshape(ntok, D_MODEL)
        dkv = jnp.concatenate([dk, dv], axis=-1).reshape(bsz * MEM_LEN, 2 * D_MODEL)
        dh2 = _mm(f"b_xq_dx{layer}", dq, wf['xattn_wq'][layer], tb=True)
        gbig.setdefault('xattn_wq', {})[layer] = _mm(f"b_xq_dw{layer}", sv["h2"], dq, ta=True)
        dmem_n = _mm(f"b_xkv_dx{layer}", dkv, wf['xattn_wkv'][layer], tb=True)
        gbig.setdefault('xattn_wkv', {})[layer] = _mm(f"b_xkv_dw{layer}", sv["mem_n"], dkv, ta=True)
        (gmemnw[layer],) = _rowwise(f"b_mem_norm{layer}", lambda dd, mm_, g1: _rms_bwd(mm_, g1, dd)[1], [dmem_n, mem2],
                                    [w['mem_norm_w'][layer][None]], out_sums=[D_MODEL])

        g, dy1, gnw[layer][2], gnw[layer][1] = _rowwise(
            f"b_resnorm_a{layer}", resnorm_bwd, [g, dh2, sv["x1"], sv["y1"]], [nw[layer, 2], nw[layer, 1]],
            out_rows=[(D_MODEL, F32), (D_MODEL, F32 if layer == 1 else BF16)], out_sums=[D_MODEL, D_MODEL])
        if layer == 1:
            def gate_bwd(t, dd):
                a, b = t[:, :D_MODEL], t[:, D_MODEL:]
                sg = jax.nn.sigmoid(b)
                return jnp.concatenate([dd * sg, dd * a * sg * (1.0 - sg)], axis=1)

            (dzg,) = _rowwise("b_s5_gate", gate_bwd, [sv["zg"], dy1], out_rows=[(2 * D_MODEL, BF16)])
            dgl = _mm("b_s5_glu_dx", dzg, wf['s5_w_glu'][0], tb=True)
            gbig['s5_w_glu'] = {0: _mm("b_s5_glu_dw", sv["gl"], dzg, ta=True)}
            dh0, dp, gsmall['s5_d'] = _s5_bwd(dgl, sv["s5"], s5_d_full, bsz)
            for n, t in zip(('s5_lambda_re', 's5_lambda_im', 's5_log_dt', 's5_b_re', 's5_b_im', 's5_c_re', 's5_c_im'), dp):
                gsmall[n] = t[None]
            g, gnw[1][0] = _rowwise("b_norm_in1", lambda gg, dh, xx, g1: (lambda r: (gg + r[0], r[1]))(_rms_bwd(xx, g1, dh)),
                                    [g, dh0, sv["x"]], [nw[1, 0]], out_rows=[(D_MODEL, F32)], out_sums=[D_MODEL])
        else:
            dcore = _mm("b_ab_out_dx", dy1, wf['ab_w_out'][0], tb=True)
            gbig['ab_w_out'] = {0: _mm("b_ab_out_dw", sv["core"], dy1, ta=True)}
            dz, dlb0, gsmall['hgrn_out_norm_w'] = sv["mix_vjp"](dcore.reshape(bsz, length, D_MODEL))
            gsmall['hgrn_out_norm_w'] = gsmall['hgrn_out_norm_w'][None]
            (gsmall['hgrn_lb_logits'],) = lb_vjp(jnp.zeros_like(lb_table).at[0].set(dlb0))
            dz = dz.reshape(ntok, -1)
            dh0 = _mm("b_ab_in_dx", dz, wf['ab_w_in'][0], tb=True)
            gbig['ab_w_in'] = {0: _mm("b_ab_in_dw", sv["h0"], dz, ta=True)}
            grad_x, gnw[0][0] = _rowwise("b_norm_in0", lambda gg, dh, xx, g1: (lambda r: (gg + r[0], r[1]))(_rms_bwd(xx, g1, dh)),
                                         [g, dh0, sv["x"]], [nw[0, 0]], out_rows=[(D_MODEL, F32)], out_sums=[D_MODEL])
    gsmall['norm_w'] = jnp.stack([jnp.concatenate(gnw[l], axis=0) for l in range(2)])
    gsmall['mem_norm_w'] = jnp.concatenate(gmemnw, axis=0)

    packed = _pack_rows([gsmall[n] for n in SMALL])
    theirs = _sibling_swap("small_swap", packed)
    chip_sum = _add2("small_pair_sum", packed, theirs)
    (all_chips,) = _gather_chips("small_gather", [chip_sum[None]], [0])
    small_sum = _sum_slots("small_sum", all_chips.reshape(N_CHIPS, packed.shape[0], PACK_COLS))
    full_shapes = [(2, 6, D_MODEL) if n == 'norm_w' else (1, D_MODEL) if n == 's5_d' else w[n].shape for n in SMALL]
    gs = dict(zip(SMALL, _unpack_rows(small_sum, full_shapes)))
    for n in SHARDED_SMALL:
        gs[n] = lax.dynamic_slice_in_dim(gs[n], chip * 256, 256, axis=gs[n].ndim - 1)

    pos = _pos_vec()
    parts = [jnp.stack([gbig[n][l] for l in sorted(gbig[n])]) for n in BIG_NAMES]
    theirs = _pair_send("grad_pair_send", parts, big_axes)
    pair = [_pair_add("grad_pair_sum_" + n, a, b, ax, pos) for (n, ax), a, b in zip(BIG, parts, theirs)]
    slots = _chip_exchange("grad_chip_exchange", pair, big_axes)
    shards = [_chip_sum("grad_chip_sum_" + n, a, b, ax, pos) for (n, ax), a, b in zip(BIG, pair, slots)]
    gfull = dict(zip(BIG_NAMES, _pair_join("grad_pair_join", shards)))

    grads, deltas, new_m, new_v = {}, {}, {}, {}
    for n in BIG_NAMES:
        grads[n] = gfull[n]
        deltas[n], new_m[n], new_v[n] = _adam("adam_" + n, w[n], gfull[n], mom[n], var[n])
    pk = [_pack_rows([t[n] for n in SMALL]) for t in (w, gs, mom, var)]
    small_out = _adam("adam_small", *pk)
    shard_shapes = [w[n].shape for n in SMALL]
    for dst, packed_out in zip((deltas, new_m, new_v), small_out):
        dst.update(zip(SMALL, _unpack_rows(packed_out, shard_shapes)))
    grads.update(gs)
    return (loss, grad_x.reshape(x.shape), *[grads[n] for n in WEIGHTS], *[deltas[n] for n in WEIGHTS],
            *[new_m[n] for n in WEIGHTS], *[new_v[n] for n in WEIGHTS])
```

```python
import functools
import math

import numpy as np
import jax
import jax.numpy as jnp
from jax import lax
from jax.experimental import pallas as pl
from jax.experimental.pallas import tpu as pltpu

F32 = jnp.float32
BF16 = jnp.bfloat16
MXU_DTYPE = jnp.bfloat16

D_MODEL = 1024
NORM_EPS = 1e-6
A_HEADS, A_DK, A_CHUNK = 4, 128, 32
A_WIDTH = A_HEADS * A_DK
B_HEADS, B_HD = 4, 128
B_WIDTH = B_HEADS * B_HD
B_DILATIONS = ((128, 1), (512, 4), (2048, 16))
ROPE_THETA = 10000.0
C_GROUP, C_GROUPS, C_STATE, C_CHUNK = 16, 64, 64, 128
C_MIN_NEG_RE = -1e-4
MEM_LEN = 256
X_HEADS = 4
X_HD = D_MODEL // X_HEADS
D_FF = 2816
ADAM_LR, ADAM_B1, ADAM_B2, ADAM_EPS, ADAM_WD, ADAM_STEP = 0.001, 0.9, 0.999, 1e-08, 0.01, 10

N_CHIPS = 4
MESH = pl.DeviceIdType.MESH
ANY = pl.BlockSpec(memory_space=pl.ANY)
_RELS = ((1, 0), (0, 1), (1, 1))

WEIGHTS = ['norm_w', 'mem_norm_w', 'ab_w_in', 'ab_w_out', 'hgrn_lb_logits', 'hgrn_out_norm_w', 's5_lambda_re',
           's5_lambda_im', 's5_log_dt', 's5_b_re', 's5_b_im', 's5_c_re', 's5_c_im', 's5_d', 's5_w_glu', 'xattn_wq',
           'xattn_wkv', 'xattn_wo', 'ffn_w_in', 'ffn_w_out']
BIG = (('ab_w_in', 1), ('ab_w_out', 0), ('s5_w_glu', 1), ('xattn_wq', 0), ('xattn_wkv', 1), ('xattn_wo', 0),
       ('ffn_w_in', 1), ('ffn_w_out', 0))
BIG_NAMES = tuple(n for n, _ in BIG)
SMALL = tuple(n for n in WEIGHTS if n not in BIG_NAMES)
SHARDED_SMALL = ('norm_w', 's5_d')
PACK_COLS = 1024


def _pos():
    return lax.axis_index("x"), lax.axis_index("y"), lax.axis_index("c")


def _flip(v, d):
    return 1 - v if d else v


def _divisor(n, want):
    for t in (want, 1024, 512, 256, 128, 64, 32, 16, 8):
        if t <= want and n % t == 0:
            return t
    return n


def _rowwise(name, fn, rows, bcasts=(), out_rows=(), out_sums=(), tile=256):
    n = rows[0].shape[0]
    t = _divisor(n, tile)
    nr, nb, no, ns = len(rows), len(bcasts), len(out_rows), len(out_sums)

    def body(*refs):
        vals = [r[...] for r in refs[:nr + nb]]
        res = fn(*vals)
        if not isinstance(res, (tuple, list)):
            res = (res,)
        outs = refs[nr + nb:]
        for k in range(no):
            outs[k][...] = res[k].astype(outs[k].dtype)
        if ns:
            first = pl.program_id(0) == 0
            for k in range(ns):
                o, val = outs[no + k], res[no + k]

                @pl.when(first)
                def _():
                    o[...] = val

                @pl.when(jnp.logical_not(first))
                def _():
                    o[...] += val

    in_specs = [pl.BlockSpec((t, r.shape[1]), lambda i: (i, 0)) for r in rows]
    in_specs += [pl.BlockSpec(b.shape, lambda i: (0, 0)) for b in bcasts]
    out_specs = [pl.BlockSpec((t, c), lambda i: (i, 0)) for c, _ in out_rows]
    out_specs += [pl.BlockSpec((1, c), lambda i: (0, 0)) for c in out_sums]
    out_shape = [jax.ShapeDtypeStruct((n, c), dt) for c, dt in out_rows]
    out_shape += [jax.ShapeDtypeStruct((1, c), F32) for c in out_sums]
    res = pl.pallas_call(
        body, name=name, grid=(n // t,), in_specs=in_specs, out_specs=out_specs, out_shape=out_shape,
        compiler_params=pltpu.CompilerParams(dimension_semantics=("arbitrary",)),
    )(*rows, *bcasts)
    return res


def _rms(x, w):
    r = lax.rsqrt(jnp.mean(x * x, axis=-1, keepdims=True) + NORM_EPS)
    return x * r * w


def _rms_bwd(x, w, dy):
    r = lax.rsqrt(jnp.mean(x * x, axis=-1, keepdims=True) + NORM_EPS)
    xh = x * r
    dxh = dy * w
    dx = r * (dxh - xh * jnp.mean(dxh * xh, axis=-1, keepdims=True))
    return dx, jnp.sum(dy * xh, axis=0, keepdims=True)


def _silu(z):
    return z * jax.nn.sigmoid(z)


def _mm(name, a, b, ta=False, tb=False, out_dtype=F32, tm=512, tn=512, tk=1024):
    m, k = a.shape[::-1] if ta else a.shape
    k2, n = b.shape[::-1] if tb else b.shape
    assert k == k2, (name, a.shape, b.shape)
    tm, tn = _divisor(m, tm), _divisor(n, tn)
    tk = k if k <= 2816 and not ta else _divisor(k, tk)
    nk = k // tk
    dims = (((0 if ta else 1,), (1 if tb else 0,)), ((), ()))

    def prod(a_ref, b_ref):
        return lax.dot_general(a_ref[...].astype(MXU_DTYPE), b_ref[...].astype(MXU_DTYPE), dims,
                               preferred_element_type=F32)

    def body_one(a_ref, b_ref, o_ref):
        o_ref[...] = prod(a_ref, b_ref).astype(o_ref.dtype)

    def body_acc(a_ref, b_ref, o_ref, acc):
        kk = pl.program_id(2)

        @pl.when(kk == 0)
        def _():
            acc[...] = prod(a_ref, b_ref)

        @pl.when(kk > 0)
        def _():
            acc[...] += prod(a_ref, b_ref)

        @pl.when(kk == nk - 1)
        def _():
            o_ref[...] = acc[...].astype(o_ref.dtype)

    a_spec = pl.BlockSpec((tk, tm), lambda i, j, kk: (kk, i)) if ta else pl.BlockSpec((tm, tk), lambda i, j, kk: (i, kk))
    b_spec = pl.BlockSpec((tn, tk), lambda i, j, kk: (j, kk)) if tb else pl.BlockSpec((tk, tn), lambda i, j, kk: (kk, j))
    return pl.pallas_call(
        body_one if nk == 1 else body_acc, name=name, grid=(m // tm, n // tn, nk),
        in_specs=[a_spec, b_spec], out_specs=pl.BlockSpec((tm, tn), lambda i, j, kk: (i, j)),
        out_shape=jax.ShapeDtypeStruct((m, n), out_dtype),
        scratch_shapes=[] if nk == 1 else [pltpu.VMEM((tm, tn), F32)],
        compiler_params=pltpu.CompilerParams(dimension_semantics=("parallel", "parallel", "arbitrary")),
    )(a, b)


def _xattn_fwd(name, q, kv, tq=512):
    bsz, length, _ = q.shape
    tq = _divisor(length, tq)
    scale = X_HD ** -0.5

    def body(q_ref, k_ref, v_ref, o_ref):
        qv, kk, vv = q_ref[...].astype(MXU_DTYPE), k_ref[...].astype(MXU_DTYPE), v_ref[...].astype(MXU_DTYPE)
        s = lax.dot_general(qv, kk, (((1,), (1,)), ((), ())), preferred_element_type=F32) * scale
        p = jnp.exp(s - jnp.max(s, axis=-1, keepdims=True))
        p = p / jnp.sum(p, axis=-1, keepdims=True)
        o_ref[...] = jnp.dot(p.astype(MXU_DTYPE), vv, preferred_element_type=F32).astype(o_ref.dtype)

    return pl.pallas_call(
        body, name=name, grid=(bsz, X_HEADS, length // tq),
        in_specs=[pl.BlockSpec((None, tq, X_HD), lambda b, h, i: (b, i, h)),
                  pl.BlockSpec((None, MEM_LEN, X_HD), lambda b, h, i: (b, 0, h)),
                  pl.BlockSpec((None, MEM_LEN, X_HD), lambda b, h, i: (b, 0, X_HEADS + h))],
        out_specs=pl.BlockSpec((None, tq, X_HD), lambda b, h, i: (b, i, h)),
        out_shape=jax.ShapeDtypeStruct(q.shape, BF16),
        compiler_params=pltpu.CompilerParams(dimension_semantics=("parallel", "parallel", "arbitrary")),
    )(q, kv, kv)


def _xattn_bwd(name, q, kv, do, tq=512):
    bsz, length, _ = q.shape
    tq = _divisor(length, tq)
    scale = X_HD ** -0.5

    def body(q_ref, k_ref, v_ref, do_ref, dq_ref, dk_ref, dv_ref):
        qv, kk, vv = q_ref[...].astype(MXU_DTYPE), k_ref[...].astype(MXU_DTYPE), v_ref[...].astype(MXU_DTYPE)
        dov = do_ref[...].astype(MXU_DTYPE)
        s = lax.dot_general(qv, kk, (((1,), (1,)), ((), ())), preferred_element_type=F32) * scale
        p = jnp.exp(s - jnp.max(s, axis=-1, keepdims=True))
        p = p / jnp.sum(p, axis=-1, keepdims=True)
        dp = lax.dot_general(dov, vv, (((1,), (1,)), ((), ())), preferred_element_type=F32)
        ds = p * (dp - jnp.sum(dp * p, axis=-1, keepdims=True)) * scale
        dsb = ds.astype(MXU_DTYPE)
        dq_ref[...] = jnp.dot(dsb, kk, preferred_element_type=F32).astype(dq_ref.dtype)
        dk = lax.dot_general(dsb, qv, (((0,), (0,)), ((), ())), preferred_element_type=F32)
        dv = lax.dot_general(p.astype(MXU_DTYPE), dov, (((0,), (0,)), ((), ())), preferred_element_type=F32)
        first = pl.program_id(2) == 0

        @pl.when(first)
        def _():
            dk_ref[...] = dk
            dv_ref[...] = dv

        @pl.when(jnp.logical_not(first))
        def _():
            dk_ref[...] += dk
            dv_ref[...] += dv

    qspec = pl.BlockSpec((None, tq, X_HD), lambda b, h, i: (b, i, h))
    kspec = pl.BlockSpec((None, MEM_LEN, X_HD), lambda b, h, i: (b, 0, h))
    return pl.pallas_call(
        body, name=name, grid=(bsz, X_HEADS, length // tq),
        in_specs=[qspec, kspec, pl.BlockSpec((None, MEM_LEN, X_HD), lambda b, h, i: (b, 0, X_HEADS + h)), qspec],
        out_specs=[qspec, kspec, kspec],
        out_shape=[jax.ShapeDtypeStruct(q.shape, BF16), jax.ShapeDtypeStruct((bsz, MEM_LEN, D_MODEL), F32),
                   jax.ShapeDtypeStruct((bsz, MEM_LEN, D_MODEL), F32)],
        compiler_params=pltpu.CompilerParams(dimension_semantics=("parallel", "parallel", "arbitrary")),
    )(q, kv, kv, do)


def _dma_sems(*counts):
    return [pltpu.SemaphoreType.DMA((max(c, 1),)) for c in counts]


def _gather_chips(name, blocks, axes):
    n = len(blocks)
    shapes = [b.shape for b in blocks]

    def body(*refs):
        ins, outs = refs[:n], refs[n:2 * n]
        lsem, lrsem, ssem, rsem, fssem, frsem = refs[2 * n:]
        x, y, c = _pos()
        me = 2 * x + y

        def region(a, chip, h):
            _, r, cc = shapes[a]
            hr = r // 2
            if axes[a] == 0:
                return outs[a].at[:, pl.ds(chip * r + h * hr, hr), :]
            return outs[a].at[:, pl.ds(h * hr, hr), pl.ds(chip * cc, cc)]

        def whole(a, chip):
            _, r, cc = shapes[a]
            if axes[a] == 0:
                return outs[a].at[:, pl.ds(chip * r, r), :]
            return outs[a].at[:, :, pl.ds(chip * cc, cc)]

        sends = []
        for a in range(n):
            cp = pltpu.make_async_remote_copy(src_ref=ins[a], dst_ref=whole(a, me), send_sem=lsem.at[a], recv_sem=lrsem.at[a],
                                              device_id=(x, y, 1 - c), device_id_type=MESH)
            cp.start()
            sends.append(cp)
        for a in range(n):
            hr = shapes[a][1] // 2
            for k, (dx, dy) in enumerate(_RELS):
                cp = pltpu.make_async_remote_copy(
                    src_ref=ins[a].at[:, pl.ds(c * hr, hr), :], dst_ref=region(a, me, c),
                    send_sem=ssem.at[3 * a + k], recv_sem=rsem.at[3 * a + k],
                    device_id=(_flip(x, dx), _flip(y, dy), c), device_id_type=MESH)
                cp.start()
                sends.append(cp)
        for a in range(n):
            for k, (dx, dy) in enumerate(_RELS):
                px, py = _flip(x, dx), _flip(y, dy)
                got = region(a, 2 * px + py, c)
                pltpu.make_async_remote_copy(
                    src_ref=got, dst_ref=got, send_sem=ssem.at[3 * a + k], recv_sem=rsem.at[3 * a + k],
                    device_id=(px, py, c), device_id_type=MESH).wait_recv()
                cp = pltpu.make_async_remote_copy(
                    src_ref=got, dst_ref=got, send_sem=fssem.at[3 * a + k], recv_sem=frsem.at[3 * a + k],
                    device_id=(x, y, 1 - c), device_id_type=MESH)
                cp.start()
                sends.append(cp)
        for a in range(n):
            for k, (dx, dy) in enumerate(_RELS):
                got = region(a, 2 * _flip(x, dx) + _flip(y, dy), 1 - c)
                pltpu.make_async_remote_copy(
                    src_ref=got, dst_ref=got, send_sem=fssem.at[3 * a + k], recv_sem=frsem.at[3 * a + k],
                    device_id=(x, y, 1 - c), device_id_type=MESH).wait_recv()
        for a in range(n):
            pltpu.make_async_remote_copy(src_ref=ins[a], dst_ref=whole(a, me), send_sem=lsem.at[a], recv_sem=lrsem.at[a],
                                         device_id=(x, y, 1 - c), device_id_type=MESH).wait_recv()
        for cp in sends:
            cp.wait_send()

    out_shape = [jax.ShapeDtypeStruct((l, 4 * r, c) if ax == 0 else (l, r, 4 * c), b.dtype)
                 for (l, r, c), ax, b in zip(shapes, axes, blocks)]
    return pl.pallas_call(
        body, name=name, in_specs=[ANY] * n, out_specs=[ANY] * n, out_shape=out_shape,
        scratch_shapes=_dma_sems(n, n, 3 * n, 3 * n, 3 * n, 3 * n),
    )(*blocks)


def _pos_vec():
    x, y, c = _pos()
    return jnp.stack([c, 2 * x + y]).astype(jnp.int32)


def _pair_send(name, parts, axes):
    n = len(parts)
    shapes = [p.shape for p in parts]
    ncopy = sum(4 if ax == 0 else 1 for ax in axes)

    def body(*refs):
        ins, theirs = refs[:n], refs[n:2 * n]
        ssem, rsem = refs[2 * n:]
        x, y, c = _pos()
        pending, j = [], 0
        for a in range(n):
            _, rf, _ = shapes[a]
            if axes[a] == 0:
                hr = rf // 8
                pieces = [(ins[a].at[:, pl.ds((2 * s + 1 - c) * hr, hr), :], theirs[a].at[:, s]) for s in range(N_CHIPS)]
            else:
                hr = rf // 2
                pieces = [(ins[a].at[:, pl.ds((1 - c) * hr, hr), :], theirs[a])]
            for give, give_dst in pieces:
                rc = pltpu.make_async_remote_copy(src_ref=give, dst_ref=give_dst, send_sem=ssem.at[j],
                                                  recv_sem=rsem.at[j], device_id=(x, y, 1 - c), device_id_type=MESH)
                rc.start()
                pending.append(rc)
                j += 1
        for cp in pending:
            cp.wait()

    def half_shape(s, ax):
        return (s[0], N_CHIPS, s[1] // 8, s[2]) if ax == 0 else (s[0], s[1] // 2, s[2])

    out_shape = [jax.ShapeDtypeStruct(half_shape(s, ax), p.dtype) for s, ax, p in zip(shapes, axes, parts)]
    return pl.pallas_call(
        body, name=name, in_specs=[ANY] * n, out_specs=[ANY] * n, out_shape=out_shape,
        scratch_shapes=_dma_sems(ncopy, ncopy),
    )(*parts)


def _chip_exchange(name, halves, axes):
    n = len(halves)
    shapes = [h.shape for h in halves]

    def body(*refs):
        ins, outs = refs[:n], refs[n:2 * n]
        ssem, rsem = refs[2 * n:]
        x, y, c = _pos()

        def part(a, chip):
            if axes[a] == 0:
                return ins[a].at[:, chip]
            cc = shapes[a][2] // N_CHIPS
            return ins[a].at[:, :, pl.ds(chip * cc, cc)]

        sends = []
        for a in range(n):
            for k, (dx, dy) in enumerate(_RELS):
                px, py = _flip(x, dx), _flip(y, dy)
                rc = pltpu.make_async_remote_copy(
                    src_ref=part(a, 2 * px + py), dst_ref=outs[a].at[:, k], send_sem=ssem.at[3 * a + k],
                    recv_sem=rsem.at[3 * a + k], device_id=(px, py, c), device_id_type=MESH)
                rc.start()
                sends.append(rc)
        for cp in sends:
            cp.wait()

    def slot_shape(s, ax):
        return (s[0], 3, s[2], s[3]) if ax == 0 else (s[0], 3, s[1], s[2] // N_CHIPS)

    out_shape = [jax.ShapeDtypeStruct(slot_shape(s, ax), h.dtype) for s, ax, h in zip(shapes, axes, halves)]
    return pl.pallas_call(
        body, name=name, in_specs=[ANY] * n, out_specs=[ANY] * n, out_shape=out_shape,
        scratch_shapes=_dma_sems(3 * n, 3 * n),
    )(*halves)


def _pair_join(name, shards):
    n = len(shards)

    def body(*refs):
        outs = refs[n:2 * n]
        ssem, rsem = refs[2 * n:]
        x, y, c = _pos()
        pending = []
        for a in range(n):
            hr = shards[a].shape[1] // 2
            mine = outs[a].at[:, pl.ds(c * hr, hr), :]
            rc = pltpu.make_async_remote_copy(src_ref=mine, dst_ref=mine, send_sem=ssem.at[a], recv_sem=rsem.at[a],
                                              device_id=(x, y, 1 - c), device_id_type=MESH)
            rc.start()
            pending.append(rc)
        for a in range(n):
            hr = shards[a].shape[1] // 2
            got = outs[a].at[:, pl.ds((1 - c) * hr, hr), :]
            pltpu.make_async_remote_copy(src_ref=got, dst_ref=got, send_sem=ssem.at[a], recv_sem=rsem.at[a],
                                         device_id=(x, y, 1 - c), device_id_type=MESH).wait_recv()
        for cp in pending:
            cp.wait_send()

    return pl.pallas_call(
        body, name=name, in_specs=[ANY] * n, out_specs=[ANY] * n,
        out_shape=[jax.ShapeDtypeStruct(s.shape, s.dtype) for s in shards],
        input_output_aliases={a: a for a in range(n)}, scratch_shapes=_dma_sems(n, n),
    )(*shards)


def _pair_add(name, part, theirs, axis, pos):
    layers, rf, cf = part.shape

    def body(pos_ref, a_ref, b_ref, o_ref):
        o_ref[...] = a_ref[...] + b_ref[...]

    if axis == 0:
        hr = rf // 8
        grid = (layers, N_CHIPS)
        in_specs = [pl.BlockSpec((None, hr, cf), lambda l, s, p: (l, 2 * s + p[0], 0)),
                    pl.BlockSpec((None, None, hr, cf), lambda l, s, p: (l, s, 0, 0))]
        out_spec = pl.BlockSpec((None, None, hr, cf), lambda l, s, p: (l, s, 0, 0))
    else:
        hr, t = rf // 2, 128
        grid = (layers, hr // t)
        in_specs = [pl.BlockSpec((None, t, cf), lambda l, i, p: (l, p[0] * (hr // t) + i, 0)),
                    pl.BlockSpec((None, t, cf), lambda l, i, p: (l, i, 0))]
        out_spec = pl.BlockSpec((None, t, cf), lambda l, i, p: (l, i, 0))
    return pl.pallas_call(
        body, name=name, out_shape=jax.ShapeDtypeStruct(theirs.shape, F32),
        grid_spec=pltpu.PrefetchScalarGridSpec(num_scalar_prefetch=1, grid=grid, in_specs=in_specs, out_specs=out_spec),
        compiler_params=pltpu.CompilerParams(dimension_semantics=("arbitrary", "arbitrary")),
    )(pos, part, theirs)


def _chip_sum(name, half, slots, axis, pos):
    layers, _, hr, c = slots.shape

    def body(pos_ref, own, s0, s1, s2, o_ref):
        o_ref[...] = ((own[...] + s0[...]) + s1[...]) + s2[...]

    t = hr if axis == 0 else 128
    if axis == 0:
        own_spec = pl.BlockSpec((None, None, t, c), lambda l, i, p: (l, p[1], i, 0))
    else:
        own_spec = pl.BlockSpec((None, t, c), lambda l, i, p: (l, i, p[1]))
    slot_specs = [pl.BlockSpec((None, None, t, c), functools.partial(lambda k, l, i, p: (l, k, i, 0), k)) for k in range(3)]
    return pl.pallas_call(
        body, name=name, out_shape=jax.ShapeDtypeStruct((layers, 2 * hr, c), F32),
        grid_spec=pltpu.PrefetchScalarGridSpec(
            num_scalar_prefetch=1, grid=(layers, hr // t), in_specs=[own_spec] + slot_specs,
            out_specs=pl.BlockSpec((None, t, c), lambda l, i, p: (l, p[0] * (hr // t) + i, 0))),
        compiler_params=pltpu.CompilerParams(dimension_semantics=("arbitrary", "arbitrary")),
    )(pos, half, slots, slots, slots)


def _sibling_swap(name, v):
    def body(v_ref, o_ref, ssem, rsem):
        x, y, c = _pos()
        cp = pltpu.make_async_remote_copy(src_ref=v_ref, dst_ref=o_ref, send_sem=ssem.at[0], recv_sem=rsem.at[0],
                                          device_id=(x, y, 1 - c), device_id_type=MESH)
        cp.start()
        cp.wait()

    return pl.pallas_call(body, name=name, in_specs=[ANY], out_specs=ANY, out_shape=jax.ShapeDtypeStruct(v.shape, v.dtype),
                          scratch_shapes=_dma_sems(1, 1))(v)


def _add2(name, a, b):
    shape = a.shape
    a2, b2 = a.reshape(-1, shape[-1]), b.reshape(-1, shape[-1])
    (o,) = _rowwise(name, lambda u, v: u + v, [a2, b2], out_rows=[(shape[-1], F32)], tile=512)
    return o.reshape(shape)


def _sum_slots(name, slots):
    _, hr, c = slots.shape
    t = _divisor(hr, 256)

    def body(s0, s1, s2, s3, o_ref):
        o_ref[...] = ((s0[...] + s1[...]) + s2[...]) + s3[...]

    return pl.pallas_call(
        body, name=name, grid=(hr // t,),
        in_specs=[pl.BlockSpec((None, t, c), functools.partial(lambda k, i: (k, i, 0), k)) for k in range(N_CHIPS)],
        out_specs=pl.BlockSpec((t, c), lambda i: (i, 0)), out_shape=jax.ShapeDtypeStruct((hr, c), F32),
        compiler_params=pltpu.CompilerParams(dimension_semantics=("arbitrary",)),
    )(slots, slots, slots, slots)


def _adam_tile(w, g, m, v):
    m = ADAM_B1 * m + (1.0 - ADAM_B1) * g
    v = ADAM_B2 * v + (1.0 - ADAM_B2) * (g * g)
    m_hat = m / (1.0 - ADAM_B1 ** ADAM_STEP)
    v_hat = v / (1.0 - ADAM_B2 ** ADAM_STEP)
    delta = -ADAM_LR * (m_hat / (jnp.sqrt(v_hat) + ADAM_EPS) + ADAM_WD * w)
    return delta, m, v


def _adam(name, w, g, m, v):
    shape = w.shape
    c = shape[-1]
    flat = [t.reshape(-1, c) for t in (w, g, m, v)]
    res = _rowwise(name, _adam_tile, flat, out_rows=[(c, F32)] * 3, tile=256)
    return [r.reshape(shape) for r in res]


ATT_T = 128
ATT_NEG = -1e30


def _branch_count(length):
    nblk = length // ATT_T
    d = (np.arange(nblk)[:, None, None] * ATT_T + np.arange(ATT_T)[None, :, None] - np.arange(ATT_T)[None, None, :])
    cnt = np.zeros(d.shape, np.float32)
    for window, dil in B_DILATIONS:
        cnt += ((d >= 0) & (d % dil == 0) & (d <= window)).astype(np.float32)
    return jnp.asarray(cnt)


def _rope_tables(length):
    half = B_HD // 2
    inv_freq = ROPE_THETA ** (-jnp.arange(half, dtype=F32) / half)
    ang = jnp.arange(length, dtype=F32)[:, None] * inv_freq[None, :]
    cos, sin = jnp.cos(ang), jnp.sin(ang)
    return jnp.concatenate([cos, cos], axis=1), jnp.concatenate([-sin, sin], axis=1)


def _swap_halves(t):
    return pltpu.roll(t, B_HD // 2, 1)


def _rope_qkv(name, z, cos, sin, t=256):
    bsz, length, _ = z.shape
    t = _divisor(length, t)

    def body(q_ref, k_ref, v_ref, c_ref, s_ref, qo, ko, vo):
        c, s = c_ref[...], s_ref[...]
        for src, dst in ((q_ref, qo), (k_ref, ko)):
            for h in range(B_HEADS):
                cols = slice(h * B_HD, (h + 1) * B_HD)
                xh = src[:, cols]
                dst[:, cols] = (xh * c + _swap_halves(xh) * s).astype(dst.dtype)
        vo[...] = v_ref[...].astype(vo.dtype)

    col0 = 4 * A_WIDTH // B_WIDTH
    specs = [pl.BlockSpec((None, t, B_WIDTH), functools.partial(lambda k, b, i: (b, i, col0 + k), k)) for k in range(3)]
    tab = pl.BlockSpec((t, B_HD), lambda b, i: (i, 0))
    out = pl.BlockSpec((None, t, B_WIDTH), lambda b, i: (b, i, 0))
    return pl.pallas_call(
        body, name=name, grid=(bsz, length // t), in_specs=specs + [tab, tab], out_specs=[out] * 3,
        out_shape=[jax.ShapeDtypeStruct((bsz, length, B_WIDTH), BF16)] * 3,
        compiler_params=pltpu.CompilerParams(dimension_semantics=("parallel", "parallel")),
    )(z, z, z, cos, sin)


def _dilated_fwd(name, q, k, v, cnt):
    bsz, length, _ = q.shape
    scale = B_HD ** -0.5
    nblk = length // ATT_T

    def body(cnt_ref, q_ref, k_ref, v_ref, o_ref, lse_ref):
        i = pl.program_id(2)
        qb = q_ref[...]

        def step(j, carry):
            m, l, acc = carry
            rows = pl.ds(pl.multiple_of(j * ATT_T, ATT_T), ATT_T)
            s = lax.dot_general(qb, k_ref[rows, :], (((1,), (1,)), ((), ())), preferred_element_type=F32) * scale
            c = cnt_ref[i - j]
            s = jnp.where(c > 0.0, s, ATT_NEG)
            m_new = jnp.maximum(m, jnp.max(s, axis=-1, keepdims=True))
            a = jnp.exp(m - m_new)
            p = c * jnp.exp(s - m_new)
            l = a * l + jnp.sum(p, axis=-1, keepdims=True)
            acc = a * acc + jnp.dot(p.astype(MXU_DTYPE), v_ref[rows, :], preferred_element_type=F32)
            return m_new, l, acc

        init = (jnp.full((ATT_T, 1), ATT_NEG, F32), jnp.zeros((ATT_T, 1), F32), jnp.zeros((ATT_T, B_HD), F32))
        m, l, acc = lax.fori_loop(0, i + 1, step, init)
        o_ref[...] = acc / l
        lse_ref[...] = jnp.broadcast_to(m + jnp.log(l), (ATT_T, B_HD))

    qspec = pl.BlockSpec((None, ATT_T, B_HD), lambda b, h, i: (b, i, h))
    kspec = pl.BlockSpec((None, length, B_HD), lambda b, h, i: (b, 0, h))
    return pl.pallas_call(
        body, name=name, grid=(bsz, B_HEADS, nblk),
        in_specs=[pl.BlockSpec(cnt.shape, lambda b, h, i: (0, 0, 0)), qspec, kspec, kspec],
        out_specs=[qspec, pl.BlockSpec((None, None, ATT_T, B_HD), lambda b, h, i: (b, h, i, 0))],
        out_shape=[jax.ShapeDtypeStruct((bsz, length, B_WIDTH), F32), jax.ShapeDtypeStruct((bsz, B_HEADS, length, B_HD), F32)],
        compiler_params=pltpu.CompilerParams(dimension_semantics=("parallel", "parallel", "arbitrary")),
    )(cnt, q, k, v)


def _dilated_bwd(name, q, k, v, o, lse, do, cnt, cos, sin, off=0):
    bsz, length, _ = q.shape
    scale = B_HD ** -0.5
    nblk = length // ATT_T

    def body(cnt_ref, q_ref, k_ref, v_ref, o_ref, lse_ref, do_ref, c_ref, s_ref, dq_ref, dk_ref, dv_ref, dq_acc, dk_acc, dv_acc):
        dk_acc[...] = jnp.zeros_like(dk_acc)
        dv_acc[...] = jnp.zeros_like(dv_acc)

        def outer(i, _):
            rq = pl.ds(pl.multiple_of(i * ATT_T, ATT_T), ATT_T)
            qi, doi = q_ref[rq, :], do_ref[rq, :]
            lsei = lse_ref[rq, :][:, 0:1]
            di = jnp.sum(doi * o_ref[rq, :], axis=-1, keepdims=True)
            dob = doi.astype(MXU_DTYPE)

            def inner(j, dq):
                rk = pl.ds(pl.multiple_of(j * ATT_T, ATT_T), ATT_T)
                kj, vj = k_ref[rk, :], v_ref[rk, :]
                s = lax.dot_general(qi, kj, (((1,), (1,)), ((), ())), preferred_element_type=F32) * scale
                c = cnt_ref[i - j]
                p = c * jnp.exp(jnp.where(c > 0.0, s, ATT_NEG) - lsei)
                dp = lax.dot_general(dob, vj, (((1,), (1,)), ((), ())), preferred_element_type=F32)
                ds = (p * (dp - di) * scale).astype(MXU_DTYPE)
                dk_acc[rk, :] += lax.dot_general(ds, qi, (((0,), (0,)), ((), ())), preferred_element_type=F32)
                dv_acc[rk, :] += lax.dot_general(p.astype(MXU_DTYPE), dob, (((0,), (0,)), ((), ())), preferred_element_type=F32)
                return dq + jnp.dot(ds, kj, preferred_element_type=F32)

            dq_acc[rq, :] = lax.fori_loop(0, i + 1, inner, jnp.zeros((ATT_T, B_HD), F32))
            return 0

        lax.fori_loop(0, nblk, outer, 0)
        c, s = c_ref[...], s_ref[...]
        for acc, dst in ((dq_acc, dq_ref), (dk_acc, dk_ref)):
            g = acc[...]
            dst[...] = (g * c + _swap_halves(g * s)).astype(dst.dtype)
        dv_ref[...] = dv_acc[...].astype(dv_ref.dtype)

    hspec = pl.BlockSpec((None, length, B_HD), lambda b, h: (b, 0, h))
    ospec = pl.BlockSpec((None, length, B_HD), lambda b, h: (b, 0, off + h))
    tab = pl.BlockSpec((length, B_HD), lambda b, h: (0, 0))
    return pl.pallas_call(
        body, name=name, grid=(bsz, B_HEADS),
        in_specs=[pl.BlockSpec(cnt.shape, lambda b, h: (0, 0, 0)), hspec, hspec, hspec, ospec,
                  pl.BlockSpec((None, None, length, B_HD), lambda b, h: (b, h, 0, 0)), ospec, tab, tab],
        out_specs=[hspec] * 3, out_shape=[jax.ShapeDtypeStruct((bsz, length, B_WIDTH), BF16)] * 3,
        scratch_shapes=[pltpu.VMEM((length, B_HD), F32)] * 3,
        compiler_params=pltpu.CompilerParams(dimension_semantics=("parallel", "parallel")),
    )(cnt, q, k, v, o, lse, do, cos, sin)


def _chunk_cumsum(t, reverse):
    n = t.shape[0]
    row = lax.broadcasted_iota(jnp.int32, t.shape, 0) & (A_CHUNK - 1)
    s = 1
    while s < A_CHUNK:
        if reverse:
            t = t + jnp.where(row < A_CHUNK - s, pltpu.roll(t, n - s, 0), 0.0)
        else:
            t = t + jnp.where(row >= s, pltpu.roll(t, s, 0), 0.0)
        s *= 2
    return t


def _hgrn_gates(fl, lb):
    sg = jax.nn.sigmoid(fl)
    f = lb + (1.0 - lb) * sg
    return sg, f


def _hgrn_chunks(nchunk, qd_s, ki_s, b_s, v_ref, o_s, st_s=None):
    tri = lax.broadcasted_iota(jnp.int32, (A_CHUNK, A_CHUNK), 0) >= lax.broadcasted_iota(jnp.int32, (A_CHUNK, A_CHUNK), 1)

    def step(n, st):
        rows = pl.ds(pl.multiple_of(n * A_CHUNK, A_CHUNK), A_CHUNK)
        if st_s is not None:
            st_s[n] = st
        qd, ki, vc = qd_s[rows, :].astype(MXU_DTYPE), ki_s[rows, :], v_ref[rows, :].astype(MXU_DTYPE)
        dec = jnp.exp(b_s[pl.ds(n * A_CHUNK + A_CHUNK - 1, 1), :])
        a = lax.dot_general(qd, ki.astype(MXU_DTYPE), (((1,), (1,)), ((), ())), preferred_element_type=F32)
        a = jnp.where(tri, a, 0.0).astype(MXU_DTYPE)
        o_s[rows, :] = (jnp.dot(a, vc, preferred_element_type=F32)
                        + lax.dot_general(qd, st.astype(MXU_DTYPE), (((1,), (1,)), ((), ())), preferred_element_type=F32))
        ke = (ki * dec).astype(MXU_DTYPE)
        return st * dec + lax.dot_general(vc, ke, (((0,), (0,)), ((), ())), preferred_element_type=F32)

    lax.fori_loop(0, nchunk, step, jnp.zeros((A_DK, A_DK), F32))


def _hgrn_fwd(name, z, lb, onw):
    bsz, length, _ = z.shape
    nchunk = length // A_CHUNK

    def body(q_ref, f_ref, v_ref, g_ref, lb_ref, w_ref, y_ref, qd_s, ki_s, b_s, o_s):
        _, f = _hgrn_gates(f_ref[...], lb_ref[...])
        b = _chunk_cumsum(jnp.log(f), False)
        b_s[...] = b
        qd_s[...] = q_ref[...] * jnp.exp(b)
        ki_s[...] = (1.0 - f) * jnp.exp(-b)
        _hgrn_chunks(nchunk, qd_s, ki_s, b_s, v_ref, o_s)
        o = o_s[...]
        on = o * lax.rsqrt(jnp.mean(o * o, axis=-1, keepdims=True) + NORM_EPS)
        y_ref[...] = on * w_ref[...] * _silu(g_ref[...])

    cols = [pl.BlockSpec((None, length, A_DK), functools.partial(lambda k, b, h: (b, 0, k * A_HEADS + h), k)) for k in range(4)]
    vec = pl.BlockSpec((1, A_DK), lambda b, h: (0, h))
    return pl.pallas_call(
        body, name=name, grid=(bsz, A_HEADS), in_specs=cols + [vec, vec],
        out_specs=pl.BlockSpec((None, length, A_DK), lambda b, h: (b, 0, h)),
        out_shape=jax.ShapeDtypeStruct((bsz, length, A_WIDTH), F32),
        scratch_shapes=[pltpu.VMEM((length, A_DK), F32)] * 4,
        compiler_params=pltpu.CompilerParams(dimension_semantics=("parallel", "parallel")),
    )(z, z, z, z, lb, onw)


def _hgrn_bwd(name, z, lb, onw, dy):
    bsz, length, _ = z.shape
    nchunk = length // A_CHUNK

    def body(q_ref, f_ref, v_ref, g_ref, lb_ref, w_ref, dy_ref, dq_ref, df_ref, dv_ref, dg_ref, dlb_ref, dw_ref,
             qd_s, ki_s, b_s, o_s, st_s, dqd_s, dki_s, dbl_s):
        lb = lb_ref[...]
        sg, f = _hgrn_gates(f_ref[...], lb)
        b = _chunk_cumsum(jnp.log(f), False)
        b_s[...] = b
        qd_s[...] = q_ref[...] * jnp.exp(b)
        ki_s[...] = (1.0 - f) * jnp.exp(-b)
        _hgrn_chunks(nchunk, qd_s, ki_s, b_s, v_ref, o_s, st_s)
        o, g, w, dyv = o_s[...], g_ref[...], w_ref[...], dy_ref[...]
        r = lax.rsqrt(jnp.mean(o * o, axis=-1, keepdims=True) + NORM_EPS)
        on = o * r
        sgg = jax.nn.sigmoid(g)
        gate = g * sgg
        dg_ref[...] = (dyv * on * w * (sgg * (1.0 + g * (1.0 - sgg)))).astype(dg_ref.dtype)
        dw = jnp.sum(dyv * on * gate, axis=0, keepdims=True)
        don = dyv * w * gate
        o_s[...] = r * (don - on * jnp.mean(don * on, axis=-1, keepdims=True))
        tri = lax.broadcasted_iota(jnp.int32, (A_CHUNK, A_CHUNK), 0) >= lax.broadcasted_iota(jnp.int32, (A_CHUNK, A_CHUNK), 1)
        last = lax.broadcasted_iota(jnp.int32, (A_CHUNK, A_DK), 0) == A_CHUNK - 1

        def back(t, dst):
            n = nchunk - 1 - t
            rows = pl.ds(pl.multiple_of(n * A_CHUNK, A_CHUNK), A_CHUNK)
            qd, ki, vc = qd_s[rows, :].astype(MXU_DTYPE), ki_s[rows, :], v_ref[rows, :].astype(MXU_DTYPE)
            kib = ki.astype(MXU_DTYPE)
            do = o_s[rows, :].astype(MXU_DTYPE)
            st = st_s[n]
            dec = jnp.exp(b_s[pl.ds(n * A_CHUNK + A_CHUNK - 1, 1), :])
            dstb = dst.astype(MXU_DTYPE)
            a = lax.dot_general(qd, kib, (((1,), (1,)), ((), ())), preferred_element_type=F32)
            a = jnp.where(tri, a, 0.0).astype(MXU_DTYPE)
            da = lax.dot_general(do, vc, (((1,), (1,)), ((), ())), preferred_element_type=F32)
            da = jnp.where(tri, da, 0.0).astype(MXU_DTYPE)
            ke = (ki * dec).astype(MXU_DTYPE)
            dv_ref[rows, :] = (lax.dot_general(a, do, (((0,), (0,)), ((), ())), preferred_element_type=F32)
                               + lax.dot_general(ke, dstb, (((1,), (1,)), ((), ())), preferred_element_type=F32)).astype(dv_ref.dtype)
            dqd_s[rows, :] = (jnp.dot(da, kib, preferred_element_type=F32)
                              + jnp.dot(do, st.astype(MXU_DTYPE), preferred_element_type=F32))
            dke = jnp.dot(vc, dstb, preferred_element_type=F32)
            dki_s[rows, :] = lax.dot_general(da, qd, (((0,), (0,)), ((), ())), preferred_element_type=F32) + dke * dec
            ddec = jnp.sum(dst * st, axis=0, keepdims=True) + jnp.sum(dke * ki, axis=0, keepdims=True)
            dbl_s[rows, :] = jnp.where(last, ddec * dec, 0.0)
            return dst * dec + lax.dot_general(do, qd, (((0,), (0,)), ((), ())), preferred_element_type=F32)

        lax.fori_loop(0, nchunk, back, jnp.zeros((A_DK, A_DK), F32))
        dqd, dki, qd, ki = dqd_s[...], dki_s[...], qd_s[...], ki_s[...]
        b = b_s[...]
        dlf = _chunk_cumsum(dqd * qd - dki * ki + dbl_s[...], True)
        dq_ref[...] = (dqd * jnp.exp(b)).astype(dq_ref.dtype)
        dfv = dlf / f - dki * jnp.exp(-b)
        df_ref[...] = (dfv * (1.0 - lb) * sg * (1.0 - sg)).astype(df_ref.dtype)
        dlb = jnp.sum(dfv * (1.0 - sg), axis=0, keepdims=True)
        first = pl.program_id(1) == 0

        @pl.when(first)
        def _():
            dlb_ref[...] = dlb
            dw_ref[...] = dw

        @pl.when(jnp.logical_not(first))
        def _():
            dlb_ref[...] += dlb
            dw_ref[...] += dw

    cols = [pl.BlockSpec((None, length, A_DK), functools.partial(lambda k, h, b: (b, 0, k * A_HEADS + h), k)) for k in range(4)]
    vec = pl.BlockSpec((1, A_DK), lambda h, b: (0, h))
    head = pl.BlockSpec((None, length, A_DK), lambda h, b: (b, 0, h))
    act = jax.ShapeDtypeStruct((bsz, length, A_WIDTH), BF16)
    return pl.pallas_call(
        body, name=name, grid=(A_HEADS, bsz), in_specs=cols + [vec, vec, head],
        out_specs=[head] * 4 + [vec, vec], out_shape=[act] * 4 + [jax.ShapeDtypeStruct((1, A_WIDTH), F32)] * 2,
        scratch_shapes=[pltpu.VMEM((length, A_DK), F32)] * 4 + [pltpu.VMEM((nchunk, A_DK, A_DK), F32)]
        + [pltpu.VMEM((length, A_DK), F32)] * 3,
        compiler_params=pltpu.CompilerParams(dimension_semantics=("parallel", "arbitrary")),
    )(z, z, z, z, lb, onw, dy)


S5_SEG = 8
S5_W = 256
S5_LANES = C_GROUPS * C_STATE


def _seg_permute(t, bsz):
    n, c = t.shape
    return t.reshape(bsz, S5_SEG, n // bsz // S5_SEG, c).transpose(0, 2, 1, 3).reshape(n, c)


def _seg_unpermute(t, bsz):
    n, c = t.shape
    return t.reshape(bsz, n // bsz // S5_SEG, S5_SEG, c).transpose(0, 2, 1, 3).reshape(n, c)


def _s5_weights(lam_re, lam_im, log_dt, b_re, b_im, c_re, c_im):
    lr = jnp.minimum(lam_re, C_MIN_NEG_RE)
    li = lam_im
    dt = jnp.exp(log_dt)[:, None]
    mag = jnp.exp(dt * lr)
    ar, ai = mag * jnp.cos(dt * li), mag * jnp.sin(dt * li)
    den = lr * lr + li * li
    zr = ((ar - 1.0) * lr + ai * li) / den
    zi = (ai * lr - (ar - 1.0) * li) / den
    bbr = zr[..., None] * b_re - zi[..., None] * b_im
    bbi = zr[..., None] * b_im + zi[..., None] * b_re
    eye = jnp.eye(C_GROUPS, dtype=F32)
    nblk = S5_LANES // S5_W
    wb = jnp.einsum('gh,rhpc->gcrhp', eye, jnp.stack([bbr, bbi]))
    wb = wb.reshape(D_MODEL, 2, nblk, S5_W).transpose(0, 2, 1, 3).reshape(D_MODEL, 2 * S5_LANES)
    wc = jnp.einsum('gh,rhcp->rhpgc', eye, jnp.stack([c_re, -c_im]))
    wc = wc.reshape(2, nblk, S5_W, D_MODEL).transpose(1, 0, 2, 3).reshape(2 * S5_LANES, D_MODEL)
    return ar.reshape(1, S5_LANES), ai.reshape(1, S5_LANES), wb, wc


def _s5_scan(name, bu, a_re, a_im, bsz, reverse):
    n, width = bu.shape
    length = n // bsz
    steps = length // S5_SEG
    assert steps & (steps - 1) == 0
    w = S5_W

    def body(bu_ref, ar_ref, ai_ref, x_ref):
        ar = jnp.broadcast_to(ar_ref[...], (S5_SEG, w))
        ai = jnp.broadcast_to(ai_ref[...], (S5_SEG, w))
        if reverse:
            ai = -ai
        zero = jnp.zeros((S5_SEG, w), F32)

        def rows_of(j):
            jj = steps - 1 - j if reverse else j
            return pl.ds(pl.multiple_of(jj * S5_SEG, S5_SEG), S5_SEG)

        def local_step(j, st):
            sr, si = st
            rows = rows_of(j)
            nr = ar * sr - ai * si + bu_ref[rows, 0:w]
            ni = ar * si + ai * sr + bu_ref[rows, w:2 * w]
            x_ref[rows, 0:w] = nr
            x_ref[rows, w:2 * w] = ni
            return nr, ni

        er, ei = lax.fori_loop(0, steps, local_step, (zero, zero))
        pr, pi = ar[0:1], ai[0:1]
        for _ in range(steps.bit_length() - 1):
            pr, pi = pr * pr - pi * pi, 2.0 * pr * pi
        row = lax.broadcasted_iota(jnp.int32, (S5_SEG, w), 0)
        cr, ci = zero, zero
        inr, ini = jnp.zeros((1, w), F32), jnp.zeros((1, w), F32)
        order = list(range(S5_SEG))[::-1] if reverse else list(range(S5_SEG))
        for idx, s in enumerate(order):
            if idx:
                cr = jnp.where(row == s, inr, cr)
                ci = jnp.where(row == s, ini, ci)
            inr, ini = er[s:s + 1] + pr * inr - pi * ini, ei[s:s + 1] + pr * ini + pi * inr

        def carry_step(j, st):
            qr, qi = st
            rows = rows_of(j)
            x_ref[rows, 0:w] += qr * cr - qi * ci
            x_ref[rows, w:2 * w] += qr * ci + qi * cr
            return qr * ar - qi * ai, qr * ai + qi * ar

        lax.fori_loop(0, steps, carry_step, (ar, ai))

    blk = pl.BlockSpec((length, 2 * w), lambda b, j: (b, j))
    aspec = pl.BlockSpec((1, w), lambda b, j: (0, j))
    return pl.pallas_call(
        body, name=name, grid=(bsz, width // (2 * w)), in_specs=[blk, aspec, aspec], out_specs=blk,
        out_shape=jax.ShapeDtypeStruct(bu.shape, F32),
        compiler_params=pltpu.CompilerParams(dimension_semantics=("parallel", "parallel")),
    )(bu, a_re, a_im)


def _s5_da(name, x, g, bsz):
    n, width = x.shape
    length = n // bsz
    steps = length // S5_SEG
    w = S5_W

    def body(x_ref, g_ref, o_ref):
        row = lax.broadcasted_iota(jnp.int32, (S5_SEG, w), 0)
        last = pl.ds((steps - 1) * S5_SEG, S5_SEG)
        xpr = jnp.where(row == 0, 0.0, pltpu.roll(x_ref[last, 0:w], 1, 0))
        xpi = jnp.where(row == 0, 0.0, pltpu.roll(x_ref[last, w:2 * w], 1, 0))
        zero = jnp.zeros((S5_SEG, w), F32)

        def step(j, st):
            pr, pi, accr, acci = st
            rows = pl.ds(pl.multiple_of(j * S5_SEG, S5_SEG), S5_SEG)
            gr, gi = g_ref[rows, 0:w], g_ref[rows, w:2 * w]
            return x_ref[rows, 0:w], x_ref[rows, w:2 * w], accr + gr * pr + gi * pi, acci + gi * pr - gr * pi

        _, _, accr, acci = lax.fori_loop(0, steps, step, (xpr, xpi, zero, zero))
        first = pl.program_id(1) == 0

        @pl.when(first)
        def _():
            o_ref[:, 0:w] = accr
            o_ref[:, w:2 * w] = acci

        @pl.when(jnp.logical_not(first))
        def _():
            o_ref[:, 0:w] += accr
            o_ref[:, w:2 * w] += acci

    blk = pl.BlockSpec((length, 2 * w), lambda j, b: (b, j))
    return pl.pallas_call(
        body, name=name, grid=(width // (2 * w), bsz), in_specs=[blk, blk],
        out_specs=pl.BlockSpec((S5_SEG, 2 * w), lambda j, b: (0, j)), out_shape=jax.ShapeDtypeStruct((S5_SEG, width), F32),
        compiler_params=pltpu.CompilerParams(dimension_semantics=("parallel", "arbitrary")),
    )(x, g)


def _gelu(y):
    return 0.5 * y * (1.0 + lax.erf(y * math.sqrt(0.5)))


def _gelu_grad(y):
    return 0.5 * (1.0 + lax.erf(y * math.sqrt(0.5))) + y * jnp.exp(-0.5 * y * y) * (1.0 / math.sqrt(2.0 * math.pi))


def _s5_fwd(h, params, d_skip, bsz):
    (a_re, a_im, wb, wc), w_vjp = jax.vjp(_s5_weights, *params)
    wb, wc = wb.astype(BF16), wc.astype(BF16)
    hp = _seg_permute(h, bsz)
    bu = _mm("s5_bu", hp, wb)
    xs = _s5_scan("s5_scan_f", bu, a_re, a_im, bsz, False)
    yc = _mm("s5_cx", xs, wc)
    ypre, glp = _rowwise("s5_gelu", lambda yy, uu, dd: (lambda t: (t, _gelu(t)))(yy + dd * uu), [yc, hp], [d_skip],
                         out_rows=[(D_MODEL, F32), (D_MODEL, BF16)])
    return _seg_unpermute(glp, bsz), dict(hp=hp, xs=xs, ypre=ypre, a_re=a_re, a_im=a_im, wb=wb, wc=wc, w_vjp=w_vjp)


def _s5_bwd(dgl, sv, d_skip, bsz):
    dyp, dskip, dd = _rowwise(
        "b_s5_gelu", lambda dg, yy, uu, ds: (lambda t: (t, t * ds, jnp.sum(t * uu, axis=0, keepdims=True)))(dg * _gelu_grad(yy)),
        [_seg_permute(dgl, bsz), sv["ypre"], sv["hp"]], [d_skip], out_rows=[(D_MODEL, BF16), (D_MODEL, F32)],
        out_sums=[D_MODEL])
    dxh = _mm("b_s5_cx_dx", dyp, sv["wc"], tb=True)
    dwc = _mm("b_s5_cx_dw", sv["xs"], dyp, ta=True)
    gs = _s5_scan("s5_scan_b", dxh, sv["a_re"], sv["a_im"], bsz, True)
    da = _s5_da("s5_da", sv["xs"], gs, bsz)
    du = _mm("b_s5_bu_dx", gs, sv["wb"], tb=True)
    dwb = _mm("b_s5_bu_dw", sv["hp"], gs, ta=True)
    da = jnp.sum(da, axis=0).reshape(S5_LANES // S5_W, 2, S5_W)
    dp = sv["w_vjp"]((da[:, 0].reshape(1, S5_LANES), da[:, 1].reshape(1, S5_LANES), dwb, dwc))
    return _seg_unpermute(du + dskip, bsz), dp, dd


def _pack_rows(arrays):
    rows = []
    for a in arrays:
        flat = a.reshape(-1).astype(F32)
        pad = (-flat.shape[0]) % PACK_COLS
        rows.append(jnp.pad(flat, (0, pad)).reshape(-1, PACK_COLS))
    out = jnp.concatenate(rows, axis=0)
    return jnp.pad(out, ((0, (-out.shape[0]) % 16), (0, 0)))


def _unpack_rows(packed, shapes):
    out, r = [], 0
    for s in shapes:
        size = int(np.prod(s))
        nr = -(-size // PACK_COLS)
        out.append(packed[r:r + nr].reshape(-1)[:size].reshape(s))
        r += nr
    return out


def kernel(x, mem, norm_w, mem_norm_w, ab_w_in, ab_w_out, hgrn_lb_logits, hgrn_out_norm_w, s5_lambda_re, s5_lambda_im, s5_log_dt, s5_b_re, s5_b_im, s5_c_re, s5_c_im, s5_d, s5_w_glu, xattn_wq, xattn_wkv, xattn_wo, ffn_w_in, ffn_w_out, loss_target, m_norm_w, m_mem_norm_w, m_ab_w_in, m_ab_w_out, m_hgrn_lb_logits, m_hgrn_out_norm_w, m_s5_lambda_re, m_s5_lambda_im, m_s5_log_dt, m_s5_b_re, m_s5_b_im, m_s5_c_re, m_s5_c_im, m_s5_d, m_s5_w_glu, m_xattn_wq, m_xattn_wkv, m_xattn_wo, m_ffn_w_in, m_ffn_w_out, v_norm_w, v_mem_norm_w, v_ab_w_in, v_ab_w_out, v_hgrn_lb_logits, v_hgrn_out_norm_w, v_s5_lambda_re, v_s5_lambda_im, v_s5_log_dt, v_s5_b_re, v_s5_b_im, v_s5_c_re, v_s5_c_im, v_s5_d, v_s5_w_glu, v_xattn_wq, v_xattn_wkv, v_xattn_wo, v_ffn_w_in, v_ffn_w_out):
    given = dict(locals())
    w = {n: given[n] for n in WEIGHTS}
    mom = {n: given["m_" + n] for n in WEIGHTS}
    var = {n: given["v_" + n] for n in WEIGHTS}
    bsz, length, _ = x.shape
    ntok = bsz * length
    chip = 2 * lax.axis_index("x") + lax.axis_index("y")

    big_axes = [ax for _, ax in BIG]
    full = _gather_chips("gather_weights", [w[n].astype(BF16) for n in BIG_NAMES], big_axes)
    wf = dict(zip(BIG_NAMES, full))
    small_block = jnp.concatenate([w['norm_w'].reshape(12, -1), w['s5_d'].reshape(1, -1), jnp.zeros((3, 256), F32)], axis=0)
    (small_full,) = _gather_chips("gather_norm_w", [small_block[None]], [1])
    nw = small_full[0, :12].reshape(2, 6, 1, D_MODEL)
    s5_d_full = small_full[0, 12:13]

    lb_table, lb_vjp = jax.vjp(lambda t: jnp.cumsum(jax.nn.softmax(t, axis=0), axis=0), w['hgrn_lb_logits'])
    xs = x.reshape(ntok, D_MODEL)
    mem2 = mem.reshape(bsz * MEM_LEN, D_MODEL)
    tgt = loss_target.reshape(ntok, D_MODEL)
    saved = []
    (h,) = _rowwise("norm_in", lambda a, g: _rms(a, g), [xs], [nw[0, 0]], out_rows=[(D_MODEL, BF16)])
    cur = xs
    for layer in range(2):
        sv = {"x": cur}
        if layer == 0:
            z = _mm("ab_in", h, wf['ab_w_in'][0]).reshape(bsz, length, -1)
            sv["h0"] = h
            rope_cos, rope_sin = _rope_tables(length)
            branch_cnt = _branch_count(length)
            oa = _hgrn_fwd("hgrn_f", z, lb_table[0:1], w['hgrn_out_norm_w'])
            qr, kr, vb = _rope_qkv("rope_qkv", z, rope_cos, rope_sin)
            ob, lse = _dilated_fwd("dilated_f", qr, kr, vb, branch_cnt)
            core = jnp.concatenate([oa, ob], axis=-1).reshape(ntok, D_MODEL)
            sv.update(z=z, qr=qr, kr=kr, vb=vb, lse=lse, core=core)
            y = _mm("ab_out", core, wf['ab_w_out'][0])
        else:
            s5p = [w[n][0] for n in ('s5_lambda_re', 's5_lambda_im', 's5_log_dt', 's5_b_re', 's5_b_im', 's5_c_re', 's5_c_im')]
            gl, sv["s5"] = _s5_fwd(h, s5p, s5_d_full, bsz)
            sv["gl"] = gl
            zg = _mm("s5_glu", gl, wf['s5_w_glu'][0])
            sv["zg"] = zg
            (y,) = _rowwise("s5_gate", lambda t: t[:, :D_MODEL] * jax.nn.sigmoid(t[:, D_MODEL:]), [zg],
                            out_rows=[(D_MODEL, F32)])
        sv["y1"] = y
        x1, h2 = _rowwise(f"resnorm_a{layer}", lambda a, b, g1, g2: (lambda s: (s, _rms(s, g2)))(a + _rms(b, g1)),
                          [cur, y], [nw[layer, 1], nw[layer, 2]], out_rows=[(D_MODEL, F32), (D_MODEL, BF16)])
        sv["x1"], sv["h2"] = x1, h2
        (mem_n,) = _rowwise(f"mem_norm{layer}", lambda a, g: _rms(a, g), [mem2], [w['mem_norm_w'][layer][None]],
                            out_rows=[(D_MODEL, BF16)])
        sv["mem_n"] = mem_n
        q = _mm(f"xq{layer}", h2, wf['xattn_wq'][layer])
        kv = _mm(f"xkv{layer}", mem_n, wf['xattn_wkv'][layer])
        sv["q"], sv["kv"] = q, kv
        o = _xattn_fwd(f"xattn_f{layer}", q.reshape(bsz, length, D_MODEL), kv.reshape(bsz, MEM_LEN, 2 * D_MODEL))
        o = o.reshape(ntok, D_MODEL)
        sv["o"] = o
        y2 = _mm(f"xo{layer}", o, wf['xattn_wo'][layer])
        sv["y2"] = y2
        x2, h4 = _rowwise(f"resnorm_b{layer}", lambda a, b, g1, g2: (lambda s: (s, _rms(s, g2)))(a + _rms(b, g1)),
                          [x1, y2], [nw[layer, 3], nw[layer, 4]], out_rows=[(D_MODEL, F32), (D_MODEL, BF16)])
        sv["x2"], sv["h4"] = x2, h4
        zf = _mm(f"ffn_in{layer}", h4, wf['ffn_w_in'][layer])
        sv["zf"] = zf
        (act,) = _rowwise(f"swiglu{layer}", lambda t: _silu(t[:, :D_FF]) * t[:, D_FF:], [zf], out_rows=[(D_FF, BF16)], tile=128)
        sv["act"] = act
        y3 = _mm(f"ffn_out{layer}", act, wf['ffn_w_out'][layer])
        sv["y3"] = y3
        saved.append(sv)
        if layer == 0:
            cur, h = _rowwise("resnorm_c0", lambda a, b, g1, g2: (lambda s: (s, _rms(s, g2)))(a + _rms(b, g1)),
                              [x2, y3], [nw[0, 5], nw[1, 0]], out_rows=[(D_MODEL, F32), (D_MODEL, F32)])
    g, sq = _rowwise("loss_head", lambda a, b, t, g1: (lambda e: (e * (1.0 / D_MODEL), jnp.sum(e * e, axis=0, keepdims=True)))(a + _rms(b, g1) - t),
                     [saved[1]["x2"], saved[1]["y3"], tgt], [nw[1, 5]], out_rows=[(D_MODEL, F32)], out_sums=[D_MODEL])
    loss = lax.psum(0.5 * jnp.sum(sq) / D_MODEL, ("x", "y", "c"))

    gbig = {}
    gnw = [[None] * 6 for _ in range(2)]
    gmemnw = [None, None]
    gsmall = {}
    for layer in (1, 0):
        sv = saved[layer]
        dy3, gnw[layer][5] = _rowwise(f"b_norm5_{layer}", lambda gg, yy, g1: _rms_bwd(yy, g1, gg), [g, sv["y3"]], [nw[layer, 5]],
                                      out_rows=[(D_MODEL, BF16)], out_sums=[D_MODEL])
        dact = _mm(f"b_ffn_out_dx{layer}", dy3, wf['ffn_w_out'][layer], tb=True)
        gw_out = _mm(f"b_ffn_out_dw{layer}", sv["act"], dy3, ta=True)

        def swiglu_bwd(t, da):
            a, b = t[:, :D_FF], t[:, D_FF:]
            sg = jax.nn.sigmoid(a)
            return jnp.concatenate([da * b * (sg * (1.0 + a * (1.0 - sg))), da * (a * sg)], axis=1)

        (dzf,) = _rowwise(f"b_swiglu{layer}", swiglu_bwd, [sv["zf"], dact], out_rows=[(2 * D_FF, BF16)], tile=128)
        dh4 = _mm(f"b_ffn_in_dx{layer}", dzf, wf['ffn_w_in'][layer], tb=True)
        gw_in = _mm(f"b_ffn_in_dw{layer}", sv["h4"], dzf, ta=True)
        gbig.setdefault('ffn_w_out', {})[layer] = gw_out
        gbig.setdefault('ffn_w_in', {})[layer] = gw_in

        def resnorm_bwd(gg, dh, xx, yy, g_in, g_res):
            dx, dw_in = _rms_bwd(xx, g_in, dh)
            tot = gg + dx
            dy, dw_res = _rms_bwd(yy, g_res, tot)
            return tot, dy, dw_in, dw_res

        g, dy2, gnw[layer][4], gnw[layer][3] = _rowwise(
            f"b_resnorm_b{layer}", resnorm_bwd, [g, dh4, sv["x2"], sv["y2"]], [nw[layer, 4], nw[layer, 3]],
            out_rows=[(D_MODEL, F32), (D_MODEL, BF16)], out_sums=[D_MODEL, D_MODEL])
        do = _mm(f"b_xo_dx{layer}", dy2, wf['xattn_wo'][layer], tb=True)
        gbig.setdefault('xattn_wo', {})[layer] = _mm(f"b_xo_dw{layer}", sv["o"], dy2, ta=True)
        dq, dk, dv = _xattn_bwd(f"xattn_b{layer}", sv["q"].reshape(bsz, length, D_MODEL),
                                sv["kv"].reshape(bsz, MEM_LEN, 2 * D_MODEL), do.reshape(bsz, length, D_MODEL))
        dq = dq.reshape(ntok, D_MODEL)
        dkv = jnp.concatenate([dk, dv], axis=-1).reshape(bsz * MEM_LEN, 2 * D_MODEL)
        dh2 = _mm(f"b_xq_dx{layer}", dq, wf['xattn_wq'][layer], tb=True)
        gbig.setdefault('xattn_wq', {})[layer] = _mm(f"b_xq_dw{layer}", sv["h2"], dq, ta=True)
        dmem_n = _mm(f"b_xkv_dx{layer}", dkv, wf['xattn_wkv'][layer], tb=True)
        gbig.setdefault('xattn_wkv', {})[layer] = _mm(f"b_xkv_dw{layer}", sv["mem_n"], dkv, ta=True)
        (gmemnw[layer],) = _rowwise(f"b_mem_norm{layer}", lambda dd, mm_, g1: _rms_bwd(mm_, g1, dd)[1], [dmem_n, mem2],
                                    [w['mem_norm_w'][layer][None]], out_sums=[D_MODEL])

        g, dy1, gnw[layer][2], gnw[layer][1] = _rowwise(
            f"b_resnorm_a{layer}", resnorm_bwd, [g, dh2, sv["x1"], sv["y1"]], [nw[layer, 2], nw[layer, 1]],
            out_rows=[(D_MODEL, F32), (D_MODEL, F32 if layer == 1 else BF16)], out_sums=[D_MODEL, D_MODEL])
        if layer == 1:
            def gate_bwd(t, dd):
                a, b = t[:, :D_MODEL], t[:, D_MODEL:]
                sg = jax.nn.sigmoid(b)
                return jnp.concatenate([dd * sg, dd * a * sg * (1.0 - sg)], axis=1)

            (dzg,) = _rowwise("b_s5_gate", gate_bwd, [sv["zg"], dy1], out_rows=[(2 * D_MODEL, BF16)])
            dgl = _mm("b_s5_glu_dx", dzg, wf['s5_w_glu'][0], tb=True)
            gbig['s5_w_glu'] = {0: _mm("b_s5_glu_dw", sv["gl"], dzg, ta=True)}
            dh0, dp, gsmall['s5_d'] = _s5_bwd(dgl, sv["s5"], s5_d_full, bsz)
            for n, t in zip(('s5_lambda_re', 's5_lambda_im', 's5_log_dt', 's5_b_re', 's5_b_im', 's5_c_re', 's5_c_im'), dp):
                gsmall[n] = t[None]
            g, gnw[1][0] = _rowwise("b_norm_in1", lambda gg, dh, xx, g1: (lambda r: (gg + r[0], r[1]))(_rms_bwd(xx, g1, dh)),
                                    [g, dh0, sv["x"]], [nw[1, 0]], out_rows=[(D_MODEL, F32)], out_sums=[D_MODEL])
        else:
            dcore = _mm("b_ab_out_dx", dy1, wf['ab_w_out'][0], tb=True)
            gbig['ab_w_out'] = {0: _mm("b_ab_out_dw", sv["core"], dy1, ta=True)}
            dcore = dcore.reshape(bsz, length, D_MODEL)
            core3 = sv["core"].reshape(bsz, length, D_MODEL)
            dqa, dfa, dia, dga, dlb0, gsmall['hgrn_out_norm_w'] = _hgrn_bwd("hgrn_b", sv["z"], lb_table[0:1], w['hgrn_out_norm_w'], dcore)
            dqb, dkb, dvb = _dilated_bwd("dilated_b", sv["qr"], sv["kr"], sv["vb"], core3, sv["lse"], dcore, branch_cnt,
                                         rope_cos, rope_sin, off=A_WIDTH // B_HD)
            (gsmall['hgrn_lb_logits'],) = lb_vjp(jnp.zeros_like(lb_table).at[0].set(dlb0[0]))
            dz = jnp.concatenate([dqa, dfa, dia, dga, dqb, dkb, dvb], axis=-1).reshape(ntok, -1)
            dh0 = _mm("b_ab_in_dx", dz, wf['ab_w_in'][0], tb=True)
            gbig['ab_w_in'] = {0: _mm("b_ab_in_dw", sv["h0"], dz, ta=True)}
            grad_x, gnw[0][0] = _rowwise("b_norm_in0", lambda gg, dh, xx, g1: (lambda r: (gg + r[0], r[1]))(_rms_bwd(xx, g1, dh)),
                                         [g, dh0, sv["x"]], [nw[0, 0]], out_rows=[(D_MODEL, F32)], out_sums=[D_MODEL])
    gsmall['norm_w'] = jnp.stack([jnp.concatenate(gnw[l], axis=0) for l in range(2)])
    gsmall['mem_norm_w'] = jnp.concatenate(gmemnw, axis=0)

    packed = _pack_rows([gsmall[n] for n in SMALL])
    theirs = _sibling_swap("small_swap", packed)
    chip_sum = _add2("small_pair_sum", packed, theirs)
    (all_chips,) = _gather_chips("small_gather", [chip_sum[None]], [0])
    small_sum = _sum_slots("small_sum", all_chips.reshape(N_CHIPS, packed.shape[0], PACK_COLS))
    full_shapes = [(2, 6, D_MODEL) if n == 'norm_w' else (1, D_MODEL) if n == 's5_d' else w[n].shape for n in SMALL]
    gs = dict(zip(SMALL, _unpack_rows(small_sum, full_shapes)))
    for n in SHARDED_SMALL:
        gs[n] = lax.dynamic_slice_in_dim(gs[n], chip * 256, 256, axis=gs[n].ndim - 1)

    pos = _pos_vec()
    parts = [jnp.stack([gbig[n][l] for l in sorted(gbig[n])]) for n in BIG_NAMES]
    theirs = _pair_send("grad_pair_send", parts, big_axes)
    pair = [_pair_add("grad_pair_sum_" + n, a, b, ax, pos) for (n, ax), a, b in zip(BIG, parts, theirs)]
    slots = _chip_exchange("grad_chip_exchange", pair, big_axes)
    shards = [_chip_sum("grad_chip_sum_" + n, a, b, ax, pos) for (n, ax), a, b in zip(BIG, pair, slots)]
    gfull = dict(zip(BIG_NAMES, _pair_join("grad_pair_join", shards)))

    grads, deltas, new_m, new_v = {}, {}, {}, {}
    for n in BIG_NAMES:
        grads[n] = gfull[n]
        deltas[n], new_m[n], new_v[n] = _adam("adam_" + n, w[n], gfull[n], mom[n], var[n])
    pk = [_pack_rows([t[n] for n in SMALL]) for t in (w, gs, mom, var)]
    small_out = _adam("adam_small", *pk)
    shard_shapes = [w[n].shape for n in SMALL]
    for dst, packed_out in zip((deltas, new_m, new_v), small_out):
        dst.update(zip(SMALL, _unpack_rows(packed_out, shard_shapes)))
    grads.update(gs)
    return (loss, grad_x.reshape(x.shape), *[grads[n] for n in WEIGHTS], *[deltas[n] for n in WEIGHTS],
            *[new_m[n] for n in WEIGHTS], *[new_v[n] for n in WEIGHTS])
```

```python
import functools
import math

import numpy as np
import jax
import jax.numpy as jnp
from jax import lax
from jax.experimental import pallas as pl
from jax.experimental.pallas import tpu as pltpu

F32 = jnp.float32
BF16 = jnp.bfloat16
MXU_DTYPE = jnp.bfloat16

D_MODEL = 1024
NORM_EPS = 1e-6
A_HEADS, A_DK, A_CHUNK = 4, 128, 32
A_WIDTH = A_HEADS * A_DK
B_HEADS, B_HD = 4, 128
B_WIDTH = B_HEADS * B_HD
B_DILATIONS = ((128, 1), (512, 4), (2048, 16))
ROPE_THETA = 10000.0
C_GROUP, C_GROUPS, C_STATE, C_CHUNK = 16, 64, 64, 128
C_MIN_NEG_RE = -1e-4
MEM_LEN = 256
X_HEADS = 4
X_HD = D_MODEL // X_HEADS
D_FF = 2816
ADAM_LR, ADAM_B1, ADAM_B2, ADAM_EPS, ADAM_WD, ADAM_STEP = 0.001, 0.9, 0.999, 1e-08, 0.01, 10

N_CHIPS = 4
MESH = pl.DeviceIdType.MESH
ANY = pl.BlockSpec(memory_space=pl.ANY)
_RELS = ((1, 0), (0, 1), (1, 1))

WEIGHTS = ['norm_w', 'mem_norm_w', 'ab_w_in', 'ab_w_out', 'hgrn_lb_logits', 'hgrn_out_norm_w', 's5_lambda_re',
           's5_lambda_im', 's5_log_dt', 's5_b_re', 's5_b_im', 's5_c_re', 's5_c_im', 's5_d', 's5_w_glu', 'xattn_wq',
           'xattn_wkv', 'xattn_wo', 'ffn_w_in', 'ffn_w_out']
BIG = (('ab_w_in', 1), ('ab_w_out', 0), ('s5_w_glu', 1), ('xattn_wq', 0), ('xattn_wkv', 1), ('xattn_wo', 0),
       ('ffn_w_in', 1), ('ffn_w_out', 0))
BIG_NAMES = tuple(n for n, _ in BIG)
SMALL = tuple(n for n in WEIGHTS if n not in BIG_NAMES)
SHARDED_SMALL = ('norm_w', 's5_d')
PACK_COLS = 1024


def _pos():
    return lax.axis_index("x"), lax.axis_index("y"), lax.axis_index("c")


def _flip(v, d):
    return 1 - v if d else v


def _divisor(n, want):
    for t in (want, 1024, 512, 256, 128, 64, 32, 16, 8):
        if t <= want and n % t == 0:
            return t
    return n


def _rowwise(name, fn, rows, bcasts=(), out_rows=(), out_sums=(), tile=256):
    n = rows[0].shape[0]
    t = _divisor(n, tile)
    nr, nb, no, ns = len(rows), len(bcasts), len(out_rows), len(out_sums)

    def body(*refs):
        vals = [r[...] for r in refs[:nr + nb]]
        res = fn(*vals)
        if not isinstance(res, (tuple, list)):
            res = (res,)
        outs = refs[nr + nb:]
        for k in range(no):
            outs[k][...] = res[k].astype(outs[k].dtype)
        if ns:
            first = pl.program_id(0) == 0
            for k in range(ns):
                o, val = outs[no + k], res[no + k]

                @pl.when(first)
                def _():
                    o[...] = val

                @pl.when(jnp.logical_not(first))
                def _():
                    o[...] += val

    in_specs = [pl.BlockSpec((t, r.shape[1]), lambda i: (i, 0)) for r in rows]
    in_specs += [pl.BlockSpec(b.shape, lambda i: (0, 0)) for b in bcasts]
    out_specs = [pl.BlockSpec((t, c), lambda i: (i, 0)) for c, _ in out_rows]
    out_specs += [pl.BlockSpec((1, c), lambda i: (0, 0)) for c in out_sums]
    out_shape = [jax.ShapeDtypeStruct((n, c), dt) for c, dt in out_rows]
    out_shape += [jax.ShapeDtypeStruct((1, c), F32) for c in out_sums]
    res = pl.pallas_call(
        body, name=name, grid=(n // t,), in_specs=in_specs, out_specs=out_specs, out_shape=out_shape,
        compiler_params=pltpu.CompilerParams(dimension_semantics=("arbitrary",)),
    )(*rows, *bcasts)
    return res


def _rms(x, w):
    r = lax.rsqrt(jnp.mean(x * x, axis=-1, keepdims=True) + NORM_EPS)
    return x * r * w


def _rms_bwd(x, w, dy):
    r = lax.rsqrt(jnp.mean(x * x, axis=-1, keepdims=True) + NORM_EPS)
    xh = x * r
    dxh = dy * w
    dx = r * (dxh - xh * jnp.mean(dxh * xh, axis=-1, keepdims=True))
    return dx, jnp.sum(dy * xh, axis=0, keepdims=True)


def _silu(z):
    return z * jax.nn.sigmoid(z)


MM_VMEM_BUDGET = 40 * 1024 * 1024


def _mm_tiles(m, n, k, ta, abytes, bbytes, obytes):
    tn = _divisor(n, 512)
    tk = _divisor(k, 1024) if ta else (k if k <= 2816 else next(t for t in (2816, 2048, 1792, 1024, 512) if k % t == 0))
    for tm in (2816, 2048, 1024, 512, 256, 128):
        if m % tm:
            continue
        need = 2 * (tm * tk * abytes + tk * tn * bbytes + tm * tn * obytes) + 2 * tm * tn * 4
        if need <= MM_VMEM_BUDGET:
            return tm, tn, tk
    return _divisor(m, 128), tn, tk


def _mm(name, a, b, ta=False, tb=False, out_dtype=F32):
    m, k = a.shape[::-1] if ta else a.shape
    k2, n = b.shape[::-1] if tb else b.shape
    assert k == k2, (name, a.shape, b.shape)
    tm, tn, tk = _mm_tiles(m, n, k, ta, a.dtype.itemsize, b.dtype.itemsize, jnp.dtype(out_dtype).itemsize)
    nk = k // tk
    dims = (((0 if ta else 1,), (1 if tb else 0,)), ((), ()))

    def prod(a_ref, b_ref):
        return lax.dot_general(a_ref[...].astype(MXU_DTYPE), b_ref[...].astype(MXU_DTYPE), dims,
                               preferred_element_type=F32)

    def body_one(a_ref, b_ref, o_ref):
        o_ref[...] = prod(a_ref, b_ref).astype(o_ref.dtype)

    def body_acc(a_ref, b_ref, o_ref, acc):
        kk = pl.program_id(2)

        @pl.when(kk == 0)
        def _():
            acc[...] = prod(a_ref, b_ref)

        @pl.when(kk > 0)
        def _():
            acc[...] += prod(a_ref, b_ref)

        @pl.when(kk == nk - 1)
        def _():
            o_ref[...] = acc[...].astype(o_ref.dtype)

    a_spec = pl.BlockSpec((tk, tm), lambda i, j, kk: (kk, i)) if ta else pl.BlockSpec((tm, tk), lambda i, j, kk: (i, kk))
    b_spec = pl.BlockSpec((tn, tk), lambda i, j, kk: (j, kk)) if tb else pl.BlockSpec((tk, tn), lambda i, j, kk: (kk, j))
    return pl.pallas_call(
        body_one if nk == 1 else body_acc, name=name, grid=(m // tm, n // tn, nk),
        in_specs=[a_spec, b_spec], out_specs=pl.BlockSpec((tm, tn), lambda i, j, kk: (i, j)),
        out_shape=jax.ShapeDtypeStruct((m, n), out_dtype),
        scratch_shapes=[] if nk == 1 else [pltpu.VMEM((tm, tn), F32)],
        compiler_params=pltpu.CompilerParams(dimension_semantics=("parallel", "parallel", "arbitrary")),
    )(a, b)


def _xattn_fwd(name, q, kv, tq=512):
    bsz, length, _ = q.shape
    tq = _divisor(length, tq)
    scale = X_HD ** -0.5

    def body(q_ref, k_ref, v_ref, o_ref):
        qv, kk, vv = q_ref[...].astype(MXU_DTYPE), k_ref[...].astype(MXU_DTYPE), v_ref[...].astype(MXU_DTYPE)
        s = lax.dot_general(qv, kk, (((1,), (1,)), ((), ())), preferred_element_type=F32) * scale
        p = jnp.exp(s - jnp.max(s, axis=-1, keepdims=True))
        p = p / jnp.sum(p, axis=-1, keepdims=True)
        o_ref[...] = jnp.dot(p.astype(MXU_DTYPE), vv, preferred_element_type=F32).astype(o_ref.dtype)

    return pl.pallas_call(
        body, name=name, grid=(bsz, X_HEADS, length // tq),
        in_specs=[pl.BlockSpec((None, tq, X_HD), lambda b, h, i: (b, i, h)),
                  pl.BlockSpec((None, MEM_LEN, X_HD), lambda b, h, i: (b, 0, h)),
                  pl.BlockSpec((None, MEM_LEN, X_HD), lambda b, h, i: (b, 0, X_HEADS + h))],
        out_specs=pl.BlockSpec((None, tq, X_HD), lambda b, h, i: (b, i, h)),
        out_shape=jax.ShapeDtypeStruct(q.shape, BF16),
        compiler_params=pltpu.CompilerParams(dimension_semantics=("parallel", "parallel", "arbitrary")),
    )(q, kv, kv)


def _xattn_bwd(name, q, kv, do, tq=512):
    bsz, length, _ = q.shape
    tq = _divisor(length, tq)
    scale = X_HD ** -0.5

    def body(q_ref, k_ref, v_ref, do_ref, dq_ref, dk_ref, dv_ref):
        qv, kk, vv = q_ref[...].astype(MXU_DTYPE), k_ref[...].astype(MXU_DTYPE), v_ref[...].astype(MXU_DTYPE)
        dov = do_ref[...].astype(MXU_DTYPE)
        s = lax.dot_general(qv, kk, (((1,), (1,)), ((), ())), preferred_element_type=F32) * scale
        p = jnp.exp(s - jnp.max(s, axis=-1, keepdims=True))
        p = p / jnp.sum(p, axis=-1, keepdims=True)
        dp = lax.dot_general(dov, vv, (((1,), (1,)), ((), ())), preferred_element_type=F32)
        ds = p * (dp - jnp.sum(dp * p, axis=-1, keepdims=True)) * scale
        dsb = ds.astype(MXU_DTYPE)
        dq_ref[...] = jnp.dot(dsb, kk, preferred_element_type=F32).astype(dq_ref.dtype)
        dk = lax.dot_general(dsb, qv, (((0,), (0,)), ((), ())), preferred_element_type=F32)
        dv = lax.dot_general(p.astype(MXU_DTYPE), dov, (((0,), (0,)), ((), ())), preferred_element_type=F32)
        first = pl.program_id(2) == 0

        @pl.when(first)
        def _():
            dk_ref[...] = dk
            dv_ref[...] = dv

        @pl.when(jnp.logical_not(first))
        def _():
            dk_ref[...] += dk
            dv_ref[...] += dv

    qspec = pl.BlockSpec((None, tq, X_HD), lambda b, h, i: (b, i, h))
    kspec = pl.BlockSpec((None, MEM_LEN, X_HD), lambda b, h, i: (b, 0, h))
    return pl.pallas_call(
        body, name=name, grid=(bsz, X_HEADS, length // tq),
        in_specs=[qspec, kspec, pl.BlockSpec((None, MEM_LEN, X_HD), lambda b, h, i: (b, 0, X_HEADS + h)), qspec],
        out_specs=[qspec, kspec, kspec],
        out_shape=[jax.ShapeDtypeStruct(q.shape, BF16), jax.ShapeDtypeStruct((bsz, MEM_LEN, D_MODEL), F32),
                   jax.ShapeDtypeStruct((bsz, MEM_LEN, D_MODEL), F32)],
        compiler_params=pltpu.CompilerParams(dimension_semantics=("parallel", "parallel", "arbitrary")),
    )(q, kv, kv, do)


def _dma_sems(*counts):
    return [pltpu.SemaphoreType.DMA((max(c, 1),)) for c in counts]


def _gather_chips(name, blocks, axes):
    n = len(blocks)
    shapes = [b.shape for b in blocks]

    def body(*refs):
        ins, outs = refs[:n], refs[n:2 * n]
        lsem, lrsem, ssem, rsem, fssem, frsem = refs[2 * n:]
        x, y, c = _pos()
        me = 2 * x + y

        def region(a, chip, h):
            _, r, cc = shapes[a]
            hr = r // 2
            if axes[a] == 0:
                return outs[a].at[:, pl.ds(chip * r + h * hr, hr), :]
            return outs[a].at[:, pl.ds(h * hr, hr), pl.ds(chip * cc, cc)]

        def whole(a, chip):
            _, r, cc = shapes[a]
            if axes[a] == 0:
                return outs[a].at[:, pl.ds(chip * r, r), :]
            return outs[a].at[:, :, pl.ds(chip * cc, cc)]

        sends = []
        for a in range(n):
            cp = pltpu.make_async_remote_copy(src_ref=ins[a], dst_ref=whole(a, me), send_sem=lsem.at[a], recv_sem=lrsem.at[a],
                                              device_id=(x, y, 1 - c), device_id_type=MESH)
            cp.start()
            sends.append(cp)
        for a in range(n):
            hr = shapes[a][1] // 2
            for k, (dx, dy) in enumerate(_RELS):
                cp = pltpu.make_async_remote_copy(
                    src_ref=ins[a].at[:, pl.ds(c * hr, hr), :], dst_ref=region(a, me, c),
                    send_sem=ssem.at[3 * a + k], recv_sem=rsem.at[3 * a + k],
                    device_id=(_flip(x, dx), _flip(y, dy), c), device_id_type=MESH)
                cp.start()
                sends.append(cp)
        for a in range(n):
            for k, (dx, dy) in enumerate(_RELS):
                px, py = _flip(x, dx), _flip(y, dy)
                got = region(a, 2 * px + py, c)
                pltpu.make_async_remote_copy(
                    src_ref=got, dst_ref=got, send_sem=ssem.at[3 * a + k], recv_sem=rsem.at[3 * a + k],
                    device_id=(px, py, c), device_id_type=MESH).wait_recv()
                cp = pltpu.make_async_remote_copy(
                    src_ref=got, dst_ref=got, send_sem=fssem.at[3 * a + k], recv_sem=frsem.at[3 * a + k],
                    device_id=(x, y, 1 - c), device_id_type=MESH)
                cp.start()
                sends.append(cp)
        for a in range(n):
            for k, (dx, dy) in enumerate(_RELS):
                got = region(a, 2 * _flip(x, dx) + _flip(y, dy), 1 - c)
                pltpu.make_async_remote_copy(
                    src_ref=got, dst_ref=got, send_sem=fssem.at[3 * a + k], recv_sem=frsem.at[3 * a + k],
                    device_id=(x, y, 1 - c), device_id_type=MESH).wait_recv()
        for a in range(n):
            pltpu.make_async_remote_copy(src_ref=ins[a], dst_ref=whole(a, me), send_sem=lsem.at[a], recv_sem=lrsem.at[a],
                                         device_id=(x, y, 1 - c), device_id_type=MESH).wait_recv()
        for cp in sends:
            cp.wait_send()

    out_shape = [jax.ShapeDtypeStruct((l, 4 * r, c) if ax == 0 else (l, r, 4 * c), b.dtype)
                 for (l, r, c), ax, b in zip(shapes, axes, blocks)]
    return pl.pallas_call(
        body, name=name, in_specs=[ANY] * n, out_specs=[ANY] * n, out_shape=out_shape,
        scratch_shapes=_dma_sems(n, n, 3 * n, 3 * n, 3 * n, 3 * n),
    )(*blocks)


def _pos_vec():
    x, y, c = _pos()
    return jnp.stack([c, 2 * x + y]).astype(jnp.int32)


def _pair_send(name, parts, axes):
    n = len(parts)
    shapes = [p.shape for p in parts]
    ncopy = sum(4 if ax == 0 else 1 for ax in axes)

    def body(*refs):
        ins, theirs = refs[:n], refs[n:2 * n]
        ssem, rsem = refs[2 * n:]
        x, y, c = _pos()
        pending, j = [], 0
        for a in range(n):
            _, rf, _ = shapes[a]
            if axes[a] == 0:
                hr = rf // 8
                pieces = [(ins[a].at[:, pl.ds((2 * s + 1 - c) * hr, hr), :], theirs[a].at[:, s]) for s in range(N_CHIPS)]
            else:
                hr = rf // 2
                pieces = [(ins[a].at[:, pl.ds((1 - c) * hr, hr), :], theirs[a])]
            for give, give_dst in pieces:
                rc = pltpu.make_async_remote_copy(src_ref=give, dst_ref=give_dst, send_sem=ssem.at[j],
                                                  recv_sem=rsem.at[j], device_id=(x, y, 1 - c), device_id_type=MESH)
                rc.start()
                pending.append(rc)
                j += 1
        for cp in pending:
            cp.wait()

    def half_shape(s, ax):
        return (s[0], N_CHIPS, s[1] // 8, s[2]) if ax == 0 else (s[0], s[1] // 2, s[2])

    out_shape = [jax.ShapeDtypeStruct(half_shape(s, ax), p.dtype) for s, ax, p in zip(shapes, axes, parts)]
    return pl.pallas_call(
        body, name=name, in_specs=[ANY] * n, out_specs=[ANY] * n, out_shape=out_shape,
        scratch_shapes=_dma_sems(ncopy, ncopy),
    )(*parts)


def _chip_exchange(name, halves, axes):
    n = len(halves)
    shapes = [h.shape for h in halves]

    def body(*refs):
        ins, outs = refs[:n], refs[n:2 * n]
        ssem, rsem = refs[2 * n:]
        x, y, c = _pos()

        def part(a, chip):
            if axes[a] == 0:
                return ins[a].at[:, chip]
            cc = shapes[a][2] // N_CHIPS
            return ins[a].at[:, :, pl.ds(chip * cc, cc)]

        sends = []
        for a in range(n):
            for k, (dx, dy) in enumerate(_RELS):
                px, py = _flip(x, dx), _flip(y, dy)
                rc = pltpu.make_async_remote_copy(
                    src_ref=part(a, 2 * px + py), dst_ref=outs[a].at[:, k], send_sem=ssem.at[3 * a + k],
                    recv_sem=rsem.at[3 * a + k], device_id=(px, py, c), device_id_type=MESH)
                rc.start()
                sends.append(rc)
        for cp in sends:
            cp.wait()

    def slot_shape(s, ax):
        return (s[0], 3, s[2], s[3]) if ax == 0 else (s[0], 3, s[1], s[2] // N_CHIPS)

    out_shape = [jax.ShapeDtypeStruct(slot_shape(s, ax), h.dtype) for s, ax, h in zip(shapes, axes, halves)]
    return pl.pallas_call(
        body, name=name, in_specs=[ANY] * n, out_specs=[ANY] * n, out_shape=out_shape,
        scratch_shapes=_dma_sems(3 * n, 3 * n),
    )(*halves)


def _pair_join(name, shards):
    n = len(shards)

    def body(*refs):
        outs = refs[n:2 * n]
        ssem, rsem = refs[2 * n:]
        x, y, c = _pos()
        pending = []
        for a in range(n):
            hr = shards[a].shape[1] // 2
            mine = outs[a].at[:, pl.ds(c * hr, hr), :]
            rc = pltpu.make_async_remote_copy(src_ref=mine, dst_ref=mine, send_sem=ssem.at[a], recv_sem=rsem.at[a],
                                              device_id=(x, y, 1 - c), device_id_type=MESH)
            rc.start()
            pending.append(rc)
        for a in range(n):
            hr = shards[a].shape[1] // 2
            got = outs[a].at[:, pl.ds((1 - c) * hr, hr), :]
            pltpu.make_async_remote_copy(src_ref=got, dst_ref=got, send_sem=ssem.at[a], recv_sem=rsem.at[a],
                                         device_id=(x, y, 1 - c), device_id_type=MESH).wait_recv()
        for cp in pending:
            cp.wait_send()

    return pl.pallas_call(
        body, name=name, in_specs=[ANY] * n, out_specs=[ANY] * n,
        out_shape=[jax.ShapeDtypeStruct(s.shape, s.dtype) for s in shards],
        input_output_aliases={a: a for a in range(n)}, scratch_shapes=_dma_sems(n, n),
    )(*shards)


def _pair_add(name, part, theirs, axis, pos):
    layers, rf, cf = part.shape

    def body(pos_ref, a_ref, b_ref, o_ref):
        o_ref[...] = a_ref[...] + b_ref[...]

    if axis == 0:
        hr = rf // 8
        grid = (layers, N_CHIPS)
        in_specs = [pl.BlockSpec((None, hr, cf), lambda l, s, p: (l, 2 * s + p[0], 0)),
                    pl.BlockSpec((None, None, hr, cf), lambda l, s, p: (l, s, 0, 0))]
        out_spec = pl.BlockSpec((None, None, hr, cf), lambda l, s, p: (l, s, 0, 0))
    else:
        hr, t = rf // 2, 128
        grid = (layers, hr // t)
        in_specs = [pl.BlockSpec((None, t, cf), lambda l, i, p: (l, p[0] * (hr // t) + i, 0)),
                    pl.BlockSpec((None, t, cf), lambda l, i, p: (l, i, 0))]
        out_spec = pl.BlockSpec((None, t, cf), lambda l, i, p: (l, i, 0))
    return pl.pallas_call(
        body, name=name, out_shape=jax.ShapeDtypeStruct(theirs.shape, F32),
        grid_spec=pltpu.PrefetchScalarGridSpec(num_scalar_prefetch=1, grid=grid, in_specs=in_specs, out_specs=out_spec),
        compiler_params=pltpu.CompilerParams(dimension_semantics=("arbitrary", "arbitrary")),
    )(pos, part, theirs)


def _chip_sum(name, half, slots, axis, pos):
    layers, _, hr, c = slots.shape

    def body(pos_ref, own, s0, s1, s2, o_ref):
        o_ref[...] = ((own[...] + s0[...]) + s1[...]) + s2[...]

    t = hr if axis == 0 else 128
    if axis == 0:
        own_spec = pl.BlockSpec((None, None, t, c), lambda l, i, p: (l, p[1], i, 0))
    else:
        own_spec = pl.BlockSpec((None, t, c), lambda l, i, p: (l, i, p[1]))
    slot_specs = [pl.BlockSpec((None, None, t, c), functools.partial(lambda k, l, i, p: (l, k, i, 0), k)) for k in range(3)]
    return pl.pallas_call(
        body, name=name, out_shape=jax.ShapeDtypeStruct((layers, 2 * hr, c), F32),
        grid_spec=pltpu.PrefetchScalarGridSpec(
            num_scalar_prefetch=1, grid=(layers, hr // t), in_specs=[own_spec] + slot_specs,
            out_specs=pl.BlockSpec((None, t, c), lambda l, i, p: (l, p[0] * (hr // t) + i, 0))),
        compiler_params=pltpu.CompilerParams(dimension_semantics=("arbitrary", "arbitrary")),
    )(pos, half, slots, slots, slots)


def _sibling_swap(name, v):
    def body(v_ref, o_ref, ssem, rsem):
        x, y, c = _pos()
        cp = pltpu.make_async_remote_copy(src_ref=v_ref, dst_ref=o_ref, send_sem=ssem.at[0], recv_sem=rsem.at[0],
                                          device_id=(x, y, 1 - c), device_id_type=MESH)
        cp.start()
        cp.wait()

    return pl.pallas_call(body, name=name, in_specs=[ANY], out_specs=ANY, out_shape=jax.ShapeDtypeStruct(v.shape, v.dtype),
                          scratch_shapes=_dma_sems(1, 1))(v)


def _add2(name, a, b):
    shape = a.shape
    a2, b2 = a.reshape(-1, shape[-1]), b.reshape(-1, shape[-1])
    (o,) = _rowwise(name, lambda u, v: u + v, [a2, b2], out_rows=[(shape[-1], F32)], tile=512)
    return o.reshape(shape)


def _sum_slots(name, slots):
    _, hr, c = slots.shape
    t = _divisor(hr, 256)

    def body(s0, s1, s2, s3, o_ref):
        o_ref[...] = ((s0[...] + s1[...]) + s2[...]) + s3[...]

    return pl.pallas_call(
        body, name=name, grid=(hr // t,),
        in_specs=[pl.BlockSpec((None, t, c), functools.partial(lambda k, i: (k, i, 0), k)) for k in range(N_CHIPS)],
        out_specs=pl.BlockSpec((t, c), lambda i: (i, 0)), out_shape=jax.ShapeDtypeStruct((hr, c), F32),
        compiler_params=pltpu.CompilerParams(dimension_semantics=("arbitrary",)),
    )(slots, slots, slots, slots)


def _adam_tile(w, g, m, v):
    m = ADAM_B1 * m + (1.0 - ADAM_B1) * g
    v = ADAM_B2 * v + (1.0 - ADAM_B2) * (g * g)
    m_hat = m / (1.0 - ADAM_B1 ** ADAM_STEP)
    v_hat = v / (1.0 - ADAM_B2 ** ADAM_STEP)
    delta = -ADAM_LR * (m_hat / (jnp.sqrt(v_hat) + ADAM_EPS) + ADAM_WD * w)
    return delta, m, v


def _adam(name, w, g, m, v):
    shape = w.shape
    c = shape[-1]
    flat = [t.reshape(-1, c) for t in (w, g, m, v)]
    res = _rowwise(name, _adam_tile, flat, out_rows=[(c, F32)] * 3, tile=256)
    return [r.reshape(shape) for r in res]


ATT_T = 256
ATT_NEG = -1e30


def _branch_count(length):
    nblk = length // ATT_T
    d = (np.arange(nblk)[:, None, None] * ATT_T + np.arange(ATT_T)[None, :, None] - np.arange(ATT_T)[None, None, :])
    cnt = np.zeros(d.shape, np.float32)
    for window, dil in B_DILATIONS:
        cnt += ((d >= 0) & (d % dil == 0) & (d <= window)).astype(np.float32)
    return jnp.asarray(cnt)


def _rope_tables(length):
    half = B_HD // 2
    inv_freq = ROPE_THETA ** (-jnp.arange(half, dtype=F32) / half)
    ang = jnp.arange(length, dtype=F32)[:, None] * inv_freq[None, :]
    cos, sin = jnp.cos(ang), jnp.sin(ang)
    return jnp.concatenate([cos, cos], axis=1), jnp.concatenate([-sin, sin], axis=1)


def _swap_halves(t):
    return pltpu.roll(t, B_HD // 2, 1)


def _rope_qkv(name, z, cos, sin, t=256):
    bsz, length, _ = z.shape
    t = _divisor(length, t)

    def body(q_ref, k_ref, v_ref, c_ref, s_ref, qo, ko, vo):
        c, s = c_ref[...], s_ref[...]
        for src, dst in ((q_ref, qo), (k_ref, ko)):
            for h in range(B_HEADS):
                cols = slice(h * B_HD, (h + 1) * B_HD)
                xh = src[:, cols]
                dst[:, cols] = (xh * c + _swap_halves(xh) * s).astype(dst.dtype)
        vo[...] = v_ref[...].astype(vo.dtype)

    col0 = 4 * A_WIDTH // B_WIDTH
    specs = [pl.BlockSpec((None, t, B_WIDTH), functools.partial(lambda k, b, i: (b, i, col0 + k), k)) for k in range(3)]
    tab = pl.BlockSpec((t, B_HD), lambda b, i: (i, 0))
    out = pl.BlockSpec((None, t, B_WIDTH), lambda b, i: (b, i, 0))
    return pl.pallas_call(
        body, name=name, grid=(bsz, length // t), in_specs=specs + [tab, tab], out_specs=[out] * 3,
        out_shape=[jax.ShapeDtypeStruct((bsz, length, B_WIDTH), BF16)] * 3,
        compiler_params=pltpu.CompilerParams(dimension_semantics=("parallel", "parallel")),
    )(z, z, z, cos, sin)


def _dilated_fwd(name, q, k, v, cnt):
    bsz, length, _ = q.shape
    scale = B_HD ** -0.5
    nblk = length // ATT_T

    def body(cnt_ref, q_ref, k_ref, v_ref, o_ref, lse_ref):
        i = pl.program_id(2)
        qb = q_ref[...]

        def step(j, carry):
            m, l, acc = carry
            rows = pl.ds(pl.multiple_of(j * ATT_T, ATT_T), ATT_T)
            s = lax.dot_general(qb, k_ref[rows, :], (((1,), (1,)), ((), ())), preferred_element_type=F32) * scale
            c = cnt_ref[i - j]
            s = jnp.where(c > 0.0, s, ATT_NEG)
            m_new = jnp.maximum(m, jnp.max(s, axis=-1, keepdims=True))
            a = jnp.exp(m - m_new)
            p = c * jnp.exp(s - m_new)
            l = a * l + jnp.sum(p, axis=-1, keepdims=True)
            acc = a * acc + jnp.dot(p.astype(MXU_DTYPE), v_ref[rows, :], preferred_element_type=F32)
            return m_new, l, acc

        init = (jnp.full((ATT_T, 1), ATT_NEG, F32), jnp.zeros((ATT_T, 1), F32), jnp.zeros((ATT_T, B_HD), F32))
        m, l, acc = lax.fori_loop(0, i + 1, step, init)
        o_ref[...] = acc / l
        lse_ref[...] = jnp.broadcast_to(m + jnp.log(l), (ATT_T, B_HD))

    qspec = pl.BlockSpec((None, ATT_T, B_HD), lambda b, h, i: (b, i, h))
    kspec = pl.BlockSpec((None, length, B_HD), lambda b, h, i: (b, 0, h))
    return pl.pallas_call(
        body, name=name, grid=(bsz, B_HEADS, nblk),
        in_specs=[pl.BlockSpec(cnt.shape, lambda b, h, i: (0, 0, 0)), qspec, kspec, kspec],
        out_specs=[qspec, pl.BlockSpec((None, None, ATT_T, B_HD), lambda b, h, i: (b, h, i, 0))],
        out_shape=[jax.ShapeDtypeStruct((bsz, length, B_WIDTH), F32), jax.ShapeDtypeStruct((bsz, B_HEADS, length, B_HD), F32)],
        compiler_params=pltpu.CompilerParams(dimension_semantics=("parallel", "parallel", "arbitrary")),
    )(cnt, q, k, v)


def _dilated_bwd(name, q, k, v, o, lse, do, cnt, cos, sin, off=0):
    bsz, length, _ = q.shape
    scale = B_HD ** -0.5
    nblk = length // ATT_T

    def body(cnt_ref, q_ref, k_ref, v_ref, o_ref, lse_ref, do_ref, c_ref, s_ref, dq_ref, dk_ref, dv_ref, dq_acc, dk_acc, dv_acc):
        dk_acc[...] = jnp.zeros_like(dk_acc)
        dv_acc[...] = jnp.zeros_like(dv_acc)

        def outer(i, _):
            rq = pl.ds(pl.multiple_of(i * ATT_T, ATT_T), ATT_T)
            qi, doi = q_ref[rq, :], do_ref[rq, :]
            lsei = lse_ref[rq, :][:, 0:1]
            di = jnp.sum(doi * o_ref[rq, :], axis=-1, keepdims=True)
            dob = doi.astype(MXU_DTYPE)

            def inner(j, dq):
                rk = pl.ds(pl.multiple_of(j * ATT_T, ATT_T), ATT_T)
                kj, vj = k_ref[rk, :], v_ref[rk, :]
                s = lax.dot_general(qi, kj, (((1,), (1,)), ((), ())), preferred_element_type=F32) * scale
                c = cnt_ref[i - j]
                p = c * jnp.exp(jnp.where(c > 0.0, s, ATT_NEG) - lsei)
                dp = lax.dot_general(dob, vj, (((1,), (1,)), ((), ())), preferred_element_type=F32)
                ds = (p * (dp - di) * scale).astype(MXU_DTYPE)
                dk_acc[rk, :] += lax.dot_general(ds, qi, (((0,), (0,)), ((), ())), preferred_element_type=F32)
                dv_acc[rk, :] += lax.dot_general(p.astype(MXU_DTYPE), dob, (((0,), (0,)), ((), ())), preferred_element_type=F32)
                return dq + jnp.dot(ds, kj, preferred_element_type=F32)

            dq_acc[rq, :] = lax.fori_loop(0, i + 1, inner, jnp.zeros((ATT_T, B_HD), F32))
            return 0

        lax.fori_loop(0, nblk, outer, 0)
        c, s = c_ref[...], s_ref[...]
        for acc, dst in ((dq_acc, dq_ref), (dk_acc, dk_ref)):
            g = acc[...]
            dst[...] = (g * c + _swap_halves(g * s)).astype(dst.dtype)
        dv_ref[...] = dv_acc[...].astype(dv_ref.dtype)

    hspec = pl.BlockSpec((None, length, B_HD), lambda b, h: (b, 0, h))
    ospec = pl.BlockSpec((None, length, B_HD), lambda b, h: (b, 0, off + h))
    tab = pl.BlockSpec((length, B_HD), lambda b, h: (0, 0))
    return pl.pallas_call(
        body, name=name, grid=(bsz, B_HEADS),
        in_specs=[pl.BlockSpec(cnt.shape, lambda b, h: (0, 0, 0)), hspec, hspec, hspec, ospec,
                  pl.BlockSpec((None, None, length, B_HD), lambda b, h: (b, h, 0, 0)), ospec, tab, tab],
        out_specs=[hspec] * 3, out_shape=[jax.ShapeDtypeStruct((bsz, length, B_WIDTH), BF16)] * 3,
        scratch_shapes=[pltpu.VMEM((length, B_HD), F32)] * 3,
        compiler_params=pltpu.CompilerParams(dimension_semantics=("parallel", "parallel")),
    )(cnt, q, k, v, o, lse, do, cos, sin)


def _chunk_cumsum(t, reverse):
    n = t.shape[0]
    row = lax.broadcasted_iota(jnp.int32, t.shape, 0) & (A_CHUNK - 1)
    s = 1
    while s < A_CHUNK:
        if reverse:
            t = t + jnp.where(row < A_CHUNK - s, pltpu.roll(t, n - s, 0), 0.0)
        else:
            t = t + jnp.where(row >= s, pltpu.roll(t, s, 0), 0.0)
        s *= 2
    return t


def _hgrn_gates(fl, lb):
    sg = jax.nn.sigmoid(fl)
    f = lb + (1.0 - lb) * sg
    return sg, f


def _hgrn_chunks(nchunk, qd_s, ki_s, b_s, v_ref, o_s, st_s=None):
    tri = lax.broadcasted_iota(jnp.int32, (A_CHUNK, A_CHUNK), 0) >= lax.broadcasted_iota(jnp.int32, (A_CHUNK, A_CHUNK), 1)

    def step(n, st):
        rows = pl.ds(pl.multiple_of(n * A_CHUNK, A_CHUNK), A_CHUNK)
        if st_s is not None:
            st_s[n] = st
        qd, ki, vc = qd_s[rows, :].astype(MXU_DTYPE), ki_s[rows, :], v_ref[rows, :].astype(MXU_DTYPE)
        dec = jnp.exp(b_s[pl.ds(n * A_CHUNK + A_CHUNK - 1, 1), :])
        a = lax.dot_general(qd, ki.astype(MXU_DTYPE), (((1,), (1,)), ((), ())), preferred_element_type=F32)
        a = jnp.where(tri, a, 0.0).astype(MXU_DTYPE)
        o_s[rows, :] = (jnp.dot(a, vc, preferred_element_type=F32)
                        + lax.dot_general(qd, st.astype(MXU_DTYPE), (((1,), (1,)), ((), ())), preferred_element_type=F32))
        ke = (ki * dec).astype(MXU_DTYPE)
        return st * dec + lax.dot_general(vc, ke, (((0,), (0,)), ((), ())), preferred_element_type=F32)

    lax.fori_loop(0, nchunk, step, jnp.zeros((A_DK, A_DK), F32))


def _hgrn_fwd(name, z, lb, onw):
    bsz, length, _ = z.shape
    nchunk = length // A_CHUNK

    def body(q_ref, f_ref, v_ref, g_ref, lb_ref, w_ref, y_ref, qd_s, ki_s, b_s, o_s):
        _, f = _hgrn_gates(f_ref[...], lb_ref[...])
        b = _chunk_cumsum(jnp.log(f), False)
        b_s[...] = b
        qd_s[...] = q_ref[...] * jnp.exp(b)
        ki_s[...] = (1.0 - f) * jnp.exp(-b)
        _hgrn_chunks(nchunk, qd_s, ki_s, b_s, v_ref, o_s)
        o = o_s[...]
        on = o * lax.rsqrt(jnp.mean(o * o, axis=-1, keepdims=True) + NORM_EPS)
        y_ref[...] = on * w_ref[...] * _silu(g_ref[...])

    cols = [pl.BlockSpec((None, length, A_DK), functools.partial(lambda k, b, h: (b, 0, k * A_HEADS + h), k)) for k in range(4)]
    vec = pl.BlockSpec((1, A_DK), lambda b, h: (0, h))
    return pl.pallas_call(
        body, name=name, grid=(bsz, A_HEADS), in_specs=cols + [vec, vec],
        out_specs=pl.BlockSpec((None, length, A_DK), lambda b, h: (b, 0, h)),
        out_shape=jax.ShapeDtypeStruct((bsz, length, A_WIDTH), F32),
        scratch_shapes=[pltpu.VMEM((length, A_DK), F32)] * 4,
        compiler_params=pltpu.CompilerParams(dimension_semantics=("parallel", "parallel")),
    )(z, z, z, z, lb, onw)


def _hgrn_bwd(name, z, lb, onw, dy):
    bsz, length, _ = z.shape
    nchunk = length // A_CHUNK

    def body(q_ref, f_ref, v_ref, g_ref, lb_ref, w_ref, dy_ref, dq_ref, df_ref, dv_ref, dg_ref, dlb_ref, dw_ref,
             qd_s, ki_s, b_s, o_s, st_s, dqd_s, dki_s, dbl_s):
        lb = lb_ref[...]
        sg, f = _hgrn_gates(f_ref[...], lb)
        b = _chunk_cumsum(jnp.log(f), False)
        b_s[...] = b
        qd_s[...] = q_ref[...] * jnp.exp(b)
        ki_s[...] = (1.0 - f) * jnp.exp(-b)
        _hgrn_chunks(nchunk, qd_s, ki_s, b_s, v_ref, o_s, st_s)
        o, g, w, dyv = o_s[...], g_ref[...], w_ref[...], dy_ref[...]
        r = lax.rsqrt(jnp.mean(o * o, axis=-1, keepdims=True) + NORM_EPS)
        on = o * r
        sgg = jax.nn.sigmoid(g)
        gate = g * sgg
        dg_ref[...] = (dyv * on * w * (sgg * (1.0 + g * (1.0 - sgg)))).astype(dg_ref.dtype)
        dw = jnp.sum(dyv * on * gate, axis=0, keepdims=True)
        don = dyv * w * gate
        o_s[...] = r * (don - on * jnp.mean(don * on, axis=-1, keepdims=True))
        tri = lax.broadcasted_iota(jnp.int32, (A_CHUNK, A_CHUNK), 0) >= lax.broadcasted_iota(jnp.int32, (A_CHUNK, A_CHUNK), 1)
        last = lax.broadcasted_iota(jnp.int32, (A_CHUNK, A_DK), 0) == A_CHUNK - 1

        def back(t, dst):
            n = nchunk - 1 - t
            rows = pl.ds(pl.multiple_of(n * A_CHUNK, A_CHUNK), A_CHUNK)
            qd, ki, vc = qd_s[rows, :].astype(MXU_DTYPE), ki_s[rows, :], v_ref[rows, :].astype(MXU_DTYPE)
            kib = ki.astype(MXU_DTYPE)
            do = o_s[rows, :].astype(MXU_DTYPE)
            st = st_s[n]
            dec = jnp.exp(b_s[pl.ds(n * A_CHUNK + A_CHUNK - 1, 1), :])
            dstb = dst.astype(MXU_DTYPE)
            a = lax.dot_general(qd, kib, (((1,), (1,)), ((), ())), preferred_element_type=F32)
            a = jnp.where(tri, a, 0.0).astype(MXU_DTYPE)
            da = lax.dot_general(do, vc, (((1,), (1,)), ((), ())), preferred_element_type=F32)
            da = jnp.where(tri, da, 0.0).astype(MXU_DTYPE)
            ke = (ki * dec).astype(MXU_DTYPE)
            dv_ref[rows, :] = (lax.dot_general(a, do, (((0,), (0,)), ((), ())), preferred_element_type=F32)
                               + lax.dot_general(ke, dstb, (((1,), (1,)), ((), ())), preferred_element_type=F32)).astype(dv_ref.dtype)
            dqd_s[rows, :] = (jnp.dot(da, kib, preferred_element_type=F32)
                              + jnp.dot(do, st.astype(MXU_DTYPE), preferred_element_type=F32))
            dke = jnp.dot(vc, dstb, preferred_element_type=F32)
            dki_s[rows, :] = lax.dot_general(da, qd, (((0,), (0,)), ((), ())), preferred_element_type=F32) + dke * dec
            ddec = jnp.sum(dst * st, axis=0, keepdims=True) + jnp.sum(dke * ki, axis=0, keepdims=True)
            dbl_s[rows, :] = jnp.where(last, ddec * dec, 0.0)
            return dst * dec + lax.dot_general(do, qd, (((0,), (0,)), ((), ())), preferred_element_type=F32)

        lax.fori_loop(0, nchunk, back, jnp.zeros((A_DK, A_DK), F32))
        dqd, dki, qd, ki = dqd_s[...], dki_s[...], qd_s[...], ki_s[...]
        b = b_s[...]
        dlf = _chunk_cumsum(dqd * qd - dki * ki + dbl_s[...], True)
        dq_ref[...] = (dqd * jnp.exp(b)).astype(dq_ref.dtype)
        dfv = dlf / f - dki * jnp.exp(-b)
        df_ref[...] = (dfv * (1.0 - lb) * sg * (1.0 - sg)).astype(df_ref.dtype)
        dlb = jnp.sum(dfv * (1.0 - sg), axis=0, keepdims=True)
        first = pl.program_id(1) == 0

        @pl.when(first)
        def _():
            dlb_ref[...] = dlb
            dw_ref[...] = dw

        @pl.when(jnp.logical_not(first))
        def _():
            dlb_ref[...] += dlb
            dw_ref[...] += dw

    cols = [pl.BlockSpec((None, length, A_DK), functools.partial(lambda k, h, b: (b, 0, k * A_HEADS + h), k)) for k in range(4)]
    vec = pl.BlockSpec((1, A_DK), lambda h, b: (0, h))
    head = pl.BlockSpec((None, length, A_DK), lambda h, b: (b, 0, h))
    act = jax.ShapeDtypeStruct((bsz, length, A_WIDTH), BF16)
    return pl.pallas_call(
        body, name=name, grid=(A_HEADS, bsz), in_specs=cols + [vec, vec, head],
        out_specs=[head] * 4 + [vec, vec], out_shape=[act] * 4 + [jax.ShapeDtypeStruct((1, A_WIDTH), F32)] * 2,
        scratch_shapes=[pltpu.VMEM((length, A_DK), F32)] * 4 + [pltpu.VMEM((nchunk, A_DK, A_DK), F32)]
        + [pltpu.VMEM((length, A_DK), F32)] * 3,
        compiler_params=pltpu.CompilerParams(dimension_semantics=("parallel", "arbitrary")),
    )(z, z, z, z, lb, onw, dy)


S5_SEG = 8
S5_W = 256
S5_LANES = C_GROUPS * C_STATE
S5_NB = 8
S5_CH = D_MODEL // S5_NB
S5_COLS = 2 * S5_LANES // S5_NB


def _bd_mm(name, a, b, tb=False, out_dtype=F32, tm=1024):
    n = a.shape[0]
    nb, ka, kn = (b.shape[0], b.shape[2], b.shape[1]) if tb else b.shape
    tm = _divisor(n, tm)
    dims = (((1,), (1 if tb else 0,)), ((), ()))

    def body(a_ref, b_ref, o_ref):
        o_ref[...] = lax.dot_general(a_ref[...].astype(MXU_DTYPE), b_ref[...].astype(MXU_DTYPE), dims,
                                     preferred_element_type=F32).astype(o_ref.dtype)

    return pl.pallas_call(
        body, name=name, grid=(n // tm, nb),
        in_specs=[pl.BlockSpec((tm, ka), lambda i, j: (i, j)), pl.BlockSpec((None,) + b.shape[1:], lambda i, j: (j, 0, 0))],
        out_specs=pl.BlockSpec((tm, kn), lambda i, j: (i, j)), out_shape=jax.ShapeDtypeStruct((n, nb * kn), out_dtype),
        compiler_params=pltpu.CompilerParams(dimension_semantics=("parallel", "parallel")),
    )(a, b)


def _bd_wgrad(name, a, c, ka, kn, tk=1024):
    n = a.shape[0]
    nb = a.shape[1] // ka
    tk = _divisor(n, tk)

    def body(a_ref, c_ref, o_ref):
        p = lax.dot_general(a_ref[...].astype(MXU_DTYPE), c_ref[...].astype(MXU_DTYPE), (((0,), (0,)), ((), ())),
                            preferred_element_type=F32)
        first = pl.program_id(1) == 0

        @pl.when(first)
        def _():
            o_ref[...] = p

        @pl.when(jnp.logical_not(first))
        def _():
            o_ref[...] += p

    return pl.pallas_call(
        body, name=name, grid=(nb, n // tk),
        in_specs=[pl.BlockSpec((tk, ka), lambda j, k: (k, j)), pl.BlockSpec((tk, kn), lambda j, k: (k, j))],
        out_specs=pl.BlockSpec((None, ka, kn), lambda j, k: (j, 0, 0)), out_shape=jax.ShapeDtypeStruct((nb, ka, kn), F32),
        compiler_params=pltpu.CompilerParams(dimension_semantics=("parallel", "arbitrary")),
    )(a, c)


def _seg_permute(t, bsz):
    n, c = t.shape
    return t.reshape(bsz, S5_SEG, n // bsz // S5_SEG, c).transpose(0, 2, 1, 3).reshape(n, c)


def _seg_unpermute(t, bsz):
    n, c = t.shape
    return t.reshape(bsz, n // bsz // S5_SEG, S5_SEG, c).transpose(0, 2, 1, 3).reshape(n, c)


def _s5_weights(lam_re, lam_im, log_dt, b_re, b_im, c_re, c_im):
    lr = jnp.minimum(lam_re, C_MIN_NEG_RE)
    li = lam_im
    dt = jnp.exp(log_dt)[:, None]
    mag = jnp.exp(dt * lr)
    ar, ai = mag * jnp.cos(dt * li), mag * jnp.sin(dt * li)
    den = lr * lr + li * li
    zr = ((ar - 1.0) * lr + ai * li) / den
    zi = (ai * lr - (ar - 1.0) * li) / den
    bbr = zr[..., None] * b_re - zi[..., None] * b_im
    bbi = zr[..., None] * b_im + zi[..., None] * b_re
    gpb = C_GROUPS // S5_NB
    eye = jnp.eye(gpb, dtype=F32)
    bb = jnp.stack([bbr, bbi]).reshape(2, S5_NB, gpb, C_STATE, C_GROUP)
    wb = jnp.einsum('ij,rbjpc->bicjpr', eye, bb).reshape(S5_NB, S5_CH, -1, S5_W, 2)
    wb = wb.transpose(0, 1, 2, 4, 3).reshape(S5_NB, S5_CH, S5_COLS)
    cc = jnp.stack([c_re, -c_im]).reshape(2, S5_NB, gpb, C_GROUP, C_STATE)
    wc = jnp.einsum('ij,rbjcp->bjpric', eye, cc).reshape(S5_NB, -1, S5_W, 2, S5_CH)
    wc = wc.transpose(0, 1, 3, 2, 4).reshape(S5_NB, S5_COLS, S5_CH)
    return ar.reshape(1, S5_LANES), ai.reshape(1, S5_LANES), wb, wc


def _s5_scan(name, bu, a_re, a_im, bsz, reverse):
    n, width = bu.shape
    length = n // bsz
    steps = length // S5_SEG
    assert steps & (steps - 1) == 0
    w = S5_W

    def body(bu_ref, ar_ref, ai_ref, x_ref):
        ar = jnp.broadcast_to(ar_ref[...], (S5_SEG, w))
        ai = jnp.broadcast_to(ai_ref[...], (S5_SEG, w))
        if reverse:
            ai = -ai
        zero = jnp.zeros((S5_SEG, w), F32)

        def rows_of(j):
            jj = steps - 1 - j if reverse else j
            return pl.ds(pl.multiple_of(jj * S5_SEG, S5_SEG), S5_SEG)

        def local_step(j, st):
            sr, si = st
            rows = rows_of(j)
            nr = ar * sr - ai * si + bu_ref[rows, 0:w]
            ni = ar * si + ai * sr + bu_ref[rows, w:2 * w]
            x_ref[rows, 0:w] = nr
            x_ref[rows, w:2 * w] = ni
            return nr, ni

        er, ei = lax.fori_loop(0, steps, local_step, (zero, zero))
        pr, pi = ar[0:1], ai[0:1]
        for _ in range(steps.bit_length() - 1):
            pr, pi = pr * pr - pi * pi, 2.0 * pr * pi
        row = lax.broadcasted_iota(jnp.int32, (S5_SEG, w), 0)
        cr, ci = zero, zero
        inr, ini = jnp.zeros((1, w), F32), jnp.zeros((1, w), F32)
        order = list(range(S5_SEG))[::-1] if reverse else list(range(S5_SEG))
        for idx, s in enumerate(order):
            if idx:
                cr = jnp.where(row == s, inr, cr)
                ci = jnp.where(row == s, ini, ci)
            inr, ini = er[s:s + 1] + pr * inr - pi * ini, ei[s:s + 1] + pr * ini + pi * inr

        def carry_step(j, st):
            qr, qi = st
            rows = rows_of(j)
            x_ref[rows, 0:w] += qr * cr - qi * ci
            x_ref[rows, w:2 * w] += qr * ci + qi * cr
            return qr * ar - qi * ai, qr * ai + qi * ar

        lax.fori_loop(0, steps, carry_step, (ar, ai))

    blk = pl.BlockSpec((length, 2 * w), lambda b, j: (b, j))
    aspec = pl.BlockSpec((1, w), lambda b, j: (0, j))
    return pl.pallas_call(
        body, name=name, grid=(bsz, width // (2 * w)), in_specs=[blk, aspec, aspec], out_specs=blk,
        out_shape=jax.ShapeDtypeStruct(bu.shape, F32),
        compiler_params=pltpu.CompilerParams(dimension_semantics=("parallel", "parallel")),
    )(bu, a_re, a_im)


def _s5_da(name, x, g, bsz):
    n, width = x.shape
    length = n // bsz
    steps = length // S5_SEG
    w = S5_W

    def body(x_ref, g_ref, o_ref):
        row = lax.broadcasted_iota(jnp.int32, (S5_SEG, w), 0)
        last = pl.ds((steps - 1) * S5_SEG, S5_SEG)
        xpr = jnp.where(row == 0, 0.0, pltpu.roll(x_ref[last, 0:w], 1, 0))
        xpi = jnp.where(row == 0, 0.0, pltpu.roll(x_ref[last, w:2 * w], 1, 0))
        zero = jnp.zeros((S5_SEG, w), F32)

        def step(j, st):
            pr, pi, accr, acci = st
            rows = pl.ds(pl.multiple_of(j * S5_SEG, S5_SEG), S5_SEG)
            gr, gi = g_ref[rows, 0:w], g_ref[rows, w:2 * w]
            return x_ref[rows, 0:w], x_ref[rows, w:2 * w], accr + gr * pr + gi * pi, acci + gi * pr - gr * pi

        _, _, accr, acci = lax.fori_loop(0, steps, step, (xpr, xpi, zero, zero))
        first = pl.program_id(1) == 0

        @pl.when(first)
        def _():
            o_ref[:, 0:w] = accr
            o_ref[:, w:2 * w] = acci

        @pl.when(jnp.logical_not(first))
        def _():
            o_ref[:, 0:w] += accr
            o_ref[:, w:2 * w] += acci

    blk = pl.BlockSpec((length, 2 * w), lambda j, b: (b, j))
    return pl.pallas_call(
        body, name=name, grid=(width // (2 * w), bsz), in_specs=[blk, blk],
        out_specs=pl.BlockSpec((S5_SEG, 2 * w), lambda j, b: (0, j)), out_shape=jax.ShapeDtypeStruct((S5_SEG, width), F32),
        compiler_params=pltpu.CompilerParams(dimension_semantics=("parallel", "arbitrary")),
    )(x, g)


def _gelu(y):
    return 0.5 * y * (1.0 + lax.erf(y * math.sqrt(0.5)))


def _gelu_grad(y):
    return 0.5 * (1.0 + lax.erf(y * math.sqrt(0.5))) + y * jnp.exp(-0.5 * y * y) * (1.0 / math.sqrt(2.0 * math.pi))


def _s5_fwd(h, params, d_skip, bsz):
    (a_re, a_im, wb, wc), w_vjp = jax.vjp(_s5_weights, *params)
    wb, wc = wb.astype(BF16), wc.astype(BF16)
    hp = _seg_permute(h, bsz)
    bu = _bd_mm("s5_bu", hp, wb)
    xs = _s5_scan("s5_scan_f", bu, a_re, a_im, bsz, False)
    yc = _bd_mm("s5_cx", xs, wc)
    ypre, glp = _rowwise("s5_gelu", lambda yy, uu, dd: (lambda t: (t, _gelu(t)))(yy + dd * uu), [yc, hp], [d_skip],
                         out_rows=[(D_MODEL, F32), (D_MODEL, BF16)])
    return _seg_unpermute(glp, bsz), dict(hp=hp, xs=xs, ypre=ypre, a_re=a_re, a_im=a_im, wb=wb, wc=wc, w_vjp=w_vjp)


def _s5_bwd(dgl, sv, d_skip, bsz):
    dyp, dskip, dd = _rowwise(
        "b_s5_gelu", lambda dg, yy, uu, ds: (lambda t: (t, t * ds, jnp.sum(t * uu, axis=0, keepdims=True)))(dg * _gelu_grad(yy)),
        [_seg_permute(dgl, bsz), sv["ypre"], sv["hp"]], [d_skip], out_rows=[(D_MODEL, BF16), (D_MODEL, F32)],
        out_sums=[D_MODEL])
    dxh = _bd_mm("b_s5_cx_dx", dyp, sv["wc"], tb=True)
    dwc = _bd_wgrad("b_s5_cx_dw", sv["xs"], dyp, S5_COLS, S5_CH)
    gs = _s5_scan("s5_scan_b", dxh, sv["a_re"], sv["a_im"], bsz, True)
    da = _s5_da("s5_da", sv["xs"], gs, bsz)
    du = _bd_mm("b_s5_bu_dx", gs, sv["wb"], tb=True)
    dwb = _bd_wgrad("b_s5_bu_dw", sv["hp"], gs, S5_CH, S5_COLS)
    da = jnp.sum(da, axis=0).reshape(S5_LANES // S5_W, 2, S5_W)
    dp = sv["w_vjp"]((da[:, 0].reshape(1, S5_LANES), da[:, 1].reshape(1, S5_LANES), dwb, dwc))
    return _seg_unpermute(du + dskip, bsz), dp, dd


def _pack_rows(arrays):
    rows = []
    for a in arrays:
        flat = a.reshape(-1).astype(F32)
        pad = (-flat.shape[0]) % PACK_COLS
        rows.append(jnp.pad(flat, (0, pad)).reshape(-1, PACK_COLS))
    out = jnp.concatenate(rows, axis=0)
    return jnp.pad(out, ((0, (-out.shape[0]) % 16), (0, 0)))


def _unpack_rows(packed, shapes):
    out, r = [], 0
    for s in shapes:
        size = int(np.prod(s))
        nr = -(-size // PACK_COLS)
        out.append(packed[r:r + nr].reshape(-1)[:size].reshape(s))
        r += nr
    return out


def kernel(x, mem, norm_w, mem_norm_w, ab_w_in, ab_w_out, hgrn_lb_logits, hgrn_out_norm_w, s5_lambda_re, s5_lambda_im, s5_log_dt, s5_b_re, s5_b_im, s5_c_re, s5_c_im, s5_d, s5_w_glu, xattn_wq, xattn_wkv, xattn_wo, ffn_w_in, ffn_w_out, loss_target, m_norm_w, m_mem_norm_w, m_ab_w_in, m_ab_w_out, m_hgrn_lb_logits, m_hgrn_out_norm_w, m_s5_lambda_re, m_s5_lambda_im, m_s5_log_dt, m_s5_b_re, m_s5_b_im, m_s5_c_re, m_s5_c_im, m_s5_d, m_s5_w_glu, m_xattn_wq, m_xattn_wkv, m_xattn_wo, m_ffn_w_in, m_ffn_w_out, v_norm_w, v_mem_norm_w, v_ab_w_in, v_ab_w_out, v_hgrn_lb_logits, v_hgrn_out_norm_w, v_s5_lambda_re, v_s5_lambda_im, v_s5_log_dt, v_s5_b_re, v_s5_b_im, v_s5_c_re, v_s5_c_im, v_s5_d, v_s5_w_glu, v_xattn_wq, v_xattn_wkv, v_xattn_wo, v_ffn_w_in, v_ffn_w_out):
    given = dict(locals())
    w = {n: given[n] for n in WEIGHTS}
    mom = {n: given["m_" + n] for n in WEIGHTS}
    var = {n: given["v_" + n] for n in WEIGHTS}
    bsz, length, _ = x.shape
    ntok = bsz * length
    chip = 2 * lax.axis_index("x") + lax.axis_index("y")

    big_axes = [ax for _, ax in BIG]
    full = _gather_chips("gather_weights", [w[n].astype(BF16) for n in BIG_NAMES], big_axes)
    wf = dict(zip(BIG_NAMES, full))
    small_block = jnp.concatenate([w['norm_w'].reshape(12, -1), w['s5_d'].reshape(1, -1), jnp.zeros((3, 256), F32)], axis=0)
    (small_full,) = _gather_chips("gather_norm_w", [small_block[None]], [1])
    nw = small_full[0, :12].reshape(2, 6, 1, D_MODEL)
    s5_d_full = small_full[0, 12:13]

    lb_table, lb_vjp = jax.vjp(lambda t: jnp.cumsum(jax.nn.softmax(t, axis=0), axis=0), w['hgrn_lb_logits'])
    xs = x.reshape(ntok, D_MODEL)
    mem2 = mem.reshape(bsz * MEM_LEN, D_MODEL)
    tgt = loss_target.reshape(ntok, D_MODEL)
    saved = []
    (h,) = _rowwise("norm_in", lambda a, g: _rms(a, g), [xs], [nw[0, 0]], out_rows=[(D_MODEL, BF16)])
    cur = xs
    for layer in range(2):
        sv = {"x": cur}
        if layer == 0:
            z = _mm("ab_in", h, wf['ab_w_in'][0]).reshape(bsz, length, -1)
            sv["h0"] = h
            rope_cos, rope_sin = _rope_tables(length)
            branch_cnt = _branch_count(length)
            oa = _hgrn_fwd("hgrn_f", z, lb_table[0:1], w['hgrn_out_norm_w'])
            qr, kr, vb = _rope_qkv("rope_qkv", z, rope_cos, rope_sin)
            ob, lse = _dilated_fwd("dilated_f", qr, kr, vb, branch_cnt)
            core = jnp.concatenate([oa, ob], axis=-1).reshape(ntok, D_MODEL)
            sv.update(z=z, qr=qr, kr=kr, vb=vb, lse=lse, core=core)
            y = _mm("ab_out", core, wf['ab_w_out'][0])
        else:
            s5p = [w[n][0] for n in ('s5_lambda_re', 's5_lambda_im', 's5_log_dt', 's5_b_re', 's5_b_im', 's5_c_re', 's5_c_im')]
            gl, sv["s5"] = _s5_fwd(h, s5p, s5_d_full, bsz)
            sv["gl"] = gl
            zg = _mm("s5_glu", gl, wf['s5_w_glu'][0])
            sv["zg"] = zg
            (y,) = _rowwise("s5_gate", lambda t: t[:, :D_MODEL] * jax.nn.sigmoid(t[:, D_MODEL:]), [zg],
                            out_rows=[(D_MODEL, F32)])
        sv["y1"] = y
        x1, h2 = _rowwise(f"resnorm_a{layer}", lambda a, b, g1, g2: (lambda s: (s, _rms(s, g2)))(a + _rms(b, g1)),
                          [cur, y], [nw[layer, 1], nw[layer, 2]], out_rows=[(D_MODEL, F32), (D_MODEL, BF16)])
        sv["x1"], sv["h2"] = x1, h2
        (mem_n,) = _rowwise(f"mem_norm{layer}", lambda a, g: _rms(a, g), [mem2], [w['mem_norm_w'][layer][None]],
                            out_rows=[(D_MODEL, BF16)])
        sv["mem_n"] = mem_n
        q = _mm(f"xq{layer}", h2, wf['xattn_wq'][layer])
        kv = _mm(f"xkv{layer}", mem_n, wf['xattn_wkv'][layer])
        sv["q"], sv["kv"] = q, kv
        o = _xattn_fwd(f"xattn_f{layer}", q.reshape(bsz, length, D_MODEL), kv.reshape(bsz, MEM_LEN, 2 * D_MODEL))
        o = o.reshape(ntok, D_MODEL)
        sv["o"] = o
        y2 = _mm(f"xo{layer}", o, wf['xattn_wo'][layer])
        sv["y2"] = y2
        x2, h4 = _rowwise(f"resnorm_b{layer}", lambda a, b, g1, g2: (lambda s: (s, _rms(s, g2)))(a + _rms(b, g1)),
                          [x1, y2], [nw[layer, 3], nw[layer, 4]], out_rows=[(D_MODEL, F32), (D_MODEL, BF16)])
        sv["x2"], sv["h4"] = x2, h4
        zf = _mm(f"ffn_in{layer}", h4, wf['ffn_w_in'][layer])
        sv["zf"] = zf
        (act,) = _rowwise(f"swiglu{layer}", lambda t: _silu(t[:, :D_FF]) * t[:, D_FF:], [zf], out_rows=[(D_FF, BF16)], tile=128)
        sv["act"] = act
        y3 = _mm(f"ffn_out{layer}", act, wf['ffn_w_out'][layer])
        sv["y3"] = y3
        saved.append(sv)
        if layer == 0:
            cur, h = _rowwise("resnorm_c0", lambda a, b, g1, g2: (lambda s: (s, _rms(s, g2)))(a + _rms(b, g1)),
                              [x2, y3], [nw[0, 5], nw[1, 0]], out_rows=[(D_MODEL, F32), (D_MODEL, F32)])
    g, sq = _rowwise("loss_head", lambda a, b, t, g1: (lambda e: (e * (1.0 / D_MODEL), jnp.sum(e * e, axis=0, keepdims=True)))(a + _rms(b, g1) - t),
                     [saved[1]["x2"], saved[1]["y3"], tgt], [nw[1, 5]], out_rows=[(D_MODEL, F32)], out_sums=[D_MODEL])
    loss = lax.psum(0.5 * jnp.sum(sq) / D_MODEL, ("x", "y", "c"))

    gbig = {}
    gnw = [[None] * 6 for _ in range(2)]
    gmemnw = [None, None]
    gsmall = {}
    for layer in (1, 0):
        sv = saved[layer]
        dy3, gnw[layer][5] = _rowwise(f"b_norm5_{layer}", lambda gg, yy, g1: _rms_bwd(yy, g1, gg), [g, sv["y3"]], [nw[layer, 5]],
                                      out_rows=[(D_MODEL, BF16)], out_sums=[D_MODEL])
        dact = _mm(f"b_ffn_out_dx{layer}", dy3, wf['ffn_w_out'][layer], tb=True)
        gw_out = _mm(f"b_ffn_out_dw{layer}", sv["act"], dy3, ta=True)

        def swiglu_bwd(t, da):
            a, b = t[:, :D_FF], t[:, D_FF:]
            sg = jax.nn.sigmoid(a)
            return jnp.concatenate([da * b * (sg * (1.0 + a * (1.0 - sg))), da * (a * sg)], axis=1)

        (dzf,) = _rowwise(f"b_swiglu{layer}", swiglu_bwd, [sv["zf"], dact], out_rows=[(2 * D_FF, BF16)], tile=128)
        dh4 = _mm(f"b_ffn_in_dx{layer}", dzf, wf['ffn_w_in'][layer], tb=True)
        gw_in = _mm(f"b_ffn_in_dw{layer}", sv["h4"], dzf, ta=True)
        gbig.setdefault('ffn_w_out', {})[layer] = gw_out
        gbig.setdefault('ffn_w_in', {})[layer] = gw_in

        def resnorm_bwd(gg, dh, xx, yy, g_in, g_res):
            dx, dw_in = _rms_bwd(xx, g_in, dh)
            tot = gg + dx
            dy, dw_res = _rms_bwd(yy, g_res, tot)
            return tot, dy, dw_in, dw_res

        g, dy2, gnw[layer][4], gnw[layer][3] = _rowwise(
            f"b_resnorm_b{layer}", resnorm_bwd, [g, dh4, sv["x2"], sv["y2"]], [nw[layer, 4], nw[layer, 3]],
            out_rows=[(D_MODEL, F32), (D_MODEL, BF16)], out_sums=[D_MODEL, D_MODEL])
        do = _mm(f"b_xo_dx{layer}", dy2, wf['xattn_wo'][layer], tb=True)
        gbig.setdefault('xattn_wo', {})[layer] = _mm(f"b_xo_dw{layer}", sv["o"], dy2, ta=True)
        dq, dk, dv = _xattn_bwd(f"xattn_b{layer}", sv["q"].reshape(bsz, length, D_MODEL),
                                sv["kv"].reshape(bsz, MEM_LEN, 2 * D_MODEL), do.reshape(bsz, length, D_MODEL))
        dq = dq.reshape(ntok, D_MODEL)
        dkv = jnp.concatenate([dk, dv], axis=-1).reshape(bsz * MEM_LEN, 2 * D_MODEL)
        dh2 = _mm(f"b_xq_dx{layer}", dq, wf['xattn_wq'][layer], tb=True)
        gbig.setdefault('xattn_wq', {})[layer] = _mm(f"b_xq_dw{layer}", sv["h2"], dq, ta=True)
        dmem_n = _mm(f"b_xkv_dx{layer}", dkv, wf['xattn_wkv'][layer], tb=True)
        gbig.setdefault('xattn_wkv', {})[layer] = _mm(f"b_xkv_dw{layer}", sv["mem_n"], dkv, ta=True)
        (gmemnw[layer],) = _rowwise(f"b_mem_norm{layer}", lambda dd, mm_, g1: _rms_bwd(mm_, g1, dd)[1], [dmem_n, mem2],
                                    [w['mem_norm_w'][layer][None]], out_sums=[D_MODEL])

        g, dy1, gnw[layer][2], gnw[layer][1] = _rowwise(
            f"b_resnorm_a{layer}", resnorm_bwd, [g, dh2, sv["x1"], sv["y1"]], [nw[layer, 2], nw[layer, 1]],
            out_rows=[(D_MODEL, F32), (D_MODEL, F32 if layer == 1 else BF16)], out_sums=[D_MODEL, D_MODEL])
        if layer == 1:
            def gate_bwd(t, dd):
                a, b = t[:, :D_MODEL], t[:, D_MODEL:]
                sg = jax.nn.sigmoid(b)
                return jnp.concatenate([dd * sg, dd * a * sg * (1.0 - sg)], axis=1)

            (dzg,) = _rowwise("b_s5_gate", gate_bwd, [sv["zg"], dy1], out_rows=[(2 * D_MODEL, BF16)])
            dgl = _mm("b_s5_glu_dx", dzg, wf['s5_w_glu'][0], tb=True)
            gbig['s5_w_glu'] = {0: _mm("b_s5_glu_dw", sv["gl"], dzg, ta=True)}
            dh0, dp, gsmall['s5_d'] = _s5_bwd(dgl, sv["s5"], s5_d_full, bsz)
            for n, t in zip(('s5_lambda_re', 's5_lambda_im', 's5_log_dt', 's5_b_re', 's5_b_im', 's5_c_re', 's5_c_im'), dp):
                gsmall[n] = t[None]
            g, gnw[1][0] = _rowwise("b_norm_in1", lambda gg, dh, xx, g1: (lambda r: (gg + r[0], r[1]))(_rms_bwd(xx, g1, dh)),
                                    [g, dh0, sv["x"]], [nw[1, 0]], out_rows=[(D_MODEL, F32)], out_sums=[D_MODEL])
        else:
            dcore = _mm("b_ab_out_dx", dy1, wf['ab_w_out'][0], tb=True)
            gbig['ab_w_out'] = {0: _mm("b_ab_out_dw", sv["core"], dy1, ta=True)}
            dcore = dcore.reshape(bsz, length, D_MODEL)
            core3 = sv["core"].reshape(bsz, length, D_MODEL)
            dqa, dfa, dia, dga, dlb0, gsmall['hgrn_out_norm_w'] = _hgrn_bwd("hgrn_b", sv["z"], lb_table[0:1], w['hgrn_out_norm_w'], dcore)
            dqb, dkb, dvb = _dilated_bwd("dilated_b", sv["qr"], sv["kr"], sv["vb"], core3, sv["lse"], dcore, branch_cnt,
                                         rope_cos, rope_sin, off=A_WIDTH // B_HD)
            (gsmall['hgrn_lb_logits'],) = lb_vjp(jnp.zeros_like(lb_table).at[0].set(dlb0[0]))
            dz = jnp.concatenate([dqa, dfa, dia, dga, dqb, dkb, dvb], axis=-1).reshape(ntok, -1)
            dh0 = _mm("b_ab_in_dx", dz, wf['ab_w_in'][0], tb=True)
            gbig['ab_w_in'] = {0: _mm("b_ab_in_dw", sv["h0"], dz, ta=True)}
            grad_x, gnw[0][0] = _rowwise("b_norm_in0", lambda gg, dh, xx, g1: (lambda r: (gg + r[0], r[1]))(_rms_bwd(xx, g1, dh)),
                                         [g, dh0, sv["x"]], [nw[0, 0]], out_rows=[(D_MODEL, F32)], out_sums=[D_MODEL])
    gsmall['norm_w'] = jnp.stack([jnp.concatenate(gnw[l], axis=0) for l in range(2)])
    gsmall['mem_norm_w'] = jnp.concatenate(gmemnw, axis=0)

    packed = _pack_rows([gsmall[n] for n in SMALL])
    theirs = _sibling_swap("small_swap", packed)
    chip_sum = _add2("small_pair_sum", packed, theirs)
    (all_chips,) = _gather_chips("small_gather", [chip_sum[None]], [0])
    small_sum = _sum_slots("small_sum", all_chips.reshape(N_CHIPS, packed.shape[0], PACK_COLS))
    full_shapes = [(2, 6, D_MODEL) if n == 'norm_w' else (1, D_MODEL) if n == 's5_d' else w[n].shape for n in SMALL]
    gs = dict(zip(SMALL, _unpack_rows(small_sum, full_shapes)))
    for n in SHARDED_SMALL:
        gs[n] = lax.dynamic_slice_in_dim(gs[n], chip * 256, 256, axis=gs[n].ndim - 1)

    pos = _pos_vec()
    parts = [jnp.stack([gbig[n][l] for l in sorted(gbig[n])]) for n in BIG_NAMES]
    theirs = _pair_send("grad_pair_send", parts, big_axes)
    pair = [_pair_add("grad_pair_sum_" + n, a, b, ax, pos) for (n, ax), a, b in zip(BIG, parts, theirs)]
    slots = _chip_exchange("grad_chip_exchange", pair, big_axes)
    shards = [_chip_sum("grad_chip_sum_" + n, a, b, ax, pos) for (n, ax), a, b in zip(BIG, pair, slots)]
    gfull = dict(zip(BIG_NAMES, _pair_join("grad_pair_join", shards)))

    grads, deltas, new_m, new_v = {}, {}, {}, {}
    for n in BIG_NAMES:
        grads[n] = gfull[n]
        deltas[n], new_m[n], new_v[n] = _adam("adam_" + n, w[n], gfull[n], mom[n], var[n])
    pk = [_pack_rows([t[n] for n in SMALL]) for t in (w, gs, mom, var)]
    small_out = _adam("adam_small", *pk)
    shard_shapes = [w[n].shape for n in SMALL]
    for dst, packed_out in zip((deltas, new_m, new_v), small_out):
        dst.update(zip(SMALL, _unpack_rows(packed_out, shard_shapes)))
    grads.update(gs)
    return (loss, grad_x.reshape(x.shape), *[grads[n] for n in WEIGHTS], *[deltas[n] for n in WEIGHTS],
            *[new_m[n] for n in WEIGHTS], *[new_v[n] for n in WEIGHTS])
```

```python
import functools
import math

import numpy as np
import jax
import jax.numpy as jnp
from jax import lax
from jax.experimental import pallas as pl
from jax.experimental.pallas import tpu as pltpu

F32 = jnp.float32
BF16 = jnp.bfloat16
MXU_DTYPE = jnp.bfloat16

D_MODEL = 1024
NORM_EPS = 1e-6
A_HEADS, A_DK, A_CHUNK = 4, 128, 32
A_WIDTH = A_HEADS * A_DK
B_HEADS, B_HD = 4, 128
B_WIDTH = B_HEADS * B_HD
B_DILATIONS = ((128, 1), (512, 4), (2048, 16))
ROPE_THETA = 10000.0
C_GROUP, C_GROUPS, C_STATE, C_CHUNK = 16, 64, 64, 128
C_MIN_NEG_RE = -1e-4
MEM_LEN = 256
X_HEADS = 4
X_HD = D_MODEL // X_HEADS
D_FF = 2816
ADAM_LR, ADAM_B1, ADAM_B2, ADAM_EPS, ADAM_WD, ADAM_STEP = 0.001, 0.9, 0.999, 1e-08, 0.01, 10

N_CHIPS = 4
MESH = pl.DeviceIdType.MESH
ANY = pl.BlockSpec(memory_space=pl.ANY)
_RELS = ((1, 0), (0, 1), (1, 1))

WEIGHTS = ['norm_w', 'mem_norm_w', 'ab_w_in', 'ab_w_out', 'hgrn_lb_logits', 'hgrn_out_norm_w', 's5_lambda_re',
           's5_lambda_im', 's5_log_dt', 's5_b_re', 's5_b_im', 's5_c_re', 's5_c_im', 's5_d', 's5_w_glu', 'xattn_wq',
           'xattn_wkv', 'xattn_wo', 'ffn_w_in', 'ffn_w_out']
BIG = (('ab_w_in', 1), ('ab_w_out', 0), ('s5_w_glu', 1), ('xattn_wq', 0), ('xattn_wkv', 1), ('xattn_wo', 0),
       ('ffn_w_in', 1), ('ffn_w_out', 0))
BIG_NAMES = tuple(n for n, _ in BIG)
SMALL = tuple(n for n in WEIGHTS if n not in BIG_NAMES)
SHARDED_SMALL = ('norm_w', 's5_d')
PACK_COLS = 1024


def _pos():
    return lax.axis_index("x"), lax.axis_index("y"), lax.axis_index("c")


def _flip(v, d):
    return 1 - v if d else v


def _divisor(n, want):
    for t in (want, 1024, 512, 256, 128, 64, 32, 16, 8):
        if t <= want and n % t == 0:
            return t
    return n


def _rowwise(name, fn, rows, bcasts=(), out_rows=(), out_sums=(), tile=256):
    n = rows[0].shape[0]
    t = _divisor(n, tile)
    nr, nb, no, ns = len(rows), len(bcasts), len(out_rows), len(out_sums)

    def body(*refs):
        vals = [r[...] for r in refs[:nr + nb]]
        res = fn(*vals)
        if not isinstance(res, (tuple, list)):
            res = (res,)
        outs = refs[nr + nb:]
        for k in range(no):
            outs[k][...] = res[k].astype(outs[k].dtype)
        if ns:
            first = pl.program_id(0) == 0
            for k in range(ns):
                o, val = outs[no + k], res[no + k]

                @pl.when(first)
                def _():
                    o[...] = val

                @pl.when(jnp.logical_not(first))
                def _():
                    o[...] += val

    in_specs = [pl.BlockSpec((t, r.shape[1]), lambda i: (i, 0)) for r in rows]
    in_specs += [pl.BlockSpec(b.shape, lambda i: (0, 0)) for b in bcasts]
    out_specs = [pl.BlockSpec((t, c), lambda i: (i, 0)) for c, _ in out_rows]
    out_specs += [pl.BlockSpec((1, c), lambda i: (0, 0)) for c in out_sums]
    out_shape = [jax.ShapeDtypeStruct((n, c), dt) for c, dt in out_rows]
    out_shape += [jax.ShapeDtypeStruct((1, c), F32) for c in out_sums]
    res = pl.pallas_call(
        body, name=name, grid=(n // t,), in_specs=in_specs, out_specs=out_specs, out_shape=out_shape,
        compiler_params=pltpu.CompilerParams(dimension_semantics=("arbitrary",)),
    )(*rows, *bcasts)
    return res


def _rms(x, w):
    r = lax.rsqrt(jnp.mean(x * x, axis=-1, keepdims=True) + NORM_EPS)
    return x * r * w


def _rms_bwd(x, w, dy):
    r = lax.rsqrt(jnp.mean(x * x, axis=-1, keepdims=True) + NORM_EPS)
    xh = x * r
    dxh = dy * w
    dx = r * (dxh - xh * jnp.mean(dxh * xh, axis=-1, keepdims=True))
    return dx, jnp.sum(dy * xh, axis=0, keepdims=True)


def _silu(z):
    return z * jax.nn.sigmoid(z)


MM_VMEM_BUDGET = 40 * 1024 * 1024


def _mm_tiles(m, n, k, ta, abytes, bbytes, obytes):
    tn = _divisor(n, 512)
    tk = _divisor(k, 1024) if ta else (k if k <= 2816 else next(t for t in (2816, 2048, 1792, 1024, 512) if k % t == 0))
    for tm in (2816, 2048, 1024, 512, 256, 128):
        if m % tm:
            continue
        need = 2 * (tm * tk * abytes + tk * tn * bbytes + tm * tn * obytes) + 2 * tm * tn * 4
        if need <= MM_VMEM_BUDGET:
            return tm, tn, tk
    return _divisor(m, 128), tn, tk


def _mm(name, a, b, ta=False, tb=False, out_dtype=F32):
    m, k = a.shape[::-1] if ta else a.shape
    k2, n = b.shape[::-1] if tb else b.shape
    assert k == k2, (name, a.shape, b.shape)
    tm, tn, tk = _mm_tiles(m, n, k, ta, a.dtype.itemsize, b.dtype.itemsize, jnp.dtype(out_dtype).itemsize)
    nk = k // tk
    dims = (((0 if ta else 1,), (1 if tb else 0,)), ((), ()))

    def prod(a_ref, b_ref):
        return lax.dot_general(a_ref[...].astype(MXU_DTYPE), b_ref[...].astype(MXU_DTYPE), dims,
                               preferred_element_type=F32)

    def body_one(a_ref, b_ref, o_ref):
        o_ref[...] = prod(a_ref, b_ref).astype(o_ref.dtype)

    def body_acc(a_ref, b_ref, o_ref, acc):
        kk = pl.program_id(2)

        @pl.when(kk == 0)
        def _():
            acc[...] = prod(a_ref, b_ref)

        @pl.when(kk > 0)
        def _():
            acc[...] += prod(a_ref, b_ref)

        @pl.when(kk == nk - 1)
        def _():
            o_ref[...] = acc[...].astype(o_ref.dtype)

    a_spec = pl.BlockSpec((tk, tm), lambda i, j, kk: (kk, i)) if ta else pl.BlockSpec((tm, tk), lambda i, j, kk: (i, kk))
    b_spec = pl.BlockSpec((tn, tk), lambda i, j, kk: (j, kk)) if tb else pl.BlockSpec((tk, tn), lambda i, j, kk: (kk, j))
    return pl.pallas_call(
        body_one if nk == 1 else body_acc, name=name, grid=(m // tm, n // tn, nk),
        in_specs=[a_spec, b_spec], out_specs=pl.BlockSpec((tm, tn), lambda i, j, kk: (i, j)),
        out_shape=jax.ShapeDtypeStruct((m, n), out_dtype),
        scratch_shapes=[] if nk == 1 else [pltpu.VMEM((tm, tn), F32)],
        compiler_params=pltpu.CompilerParams(dimension_semantics=("parallel", "parallel", "arbitrary")),
    )(a, b)


def _xattn_fwd(name, q, kv, tq=512):
    bsz, length, _ = q.shape
    tq = _divisor(length, tq)
    scale = X_HD ** -0.5

    def body(q_ref, k_ref, v_ref, o_ref):
        qv, kk, vv = q_ref[...].astype(MXU_DTYPE), k_ref[...].astype(MXU_DTYPE), v_ref[...].astype(MXU_DTYPE)
        s = lax.dot_general(qv, kk, (((1,), (1,)), ((), ())), preferred_element_type=F32) * scale
        p = jnp.exp(s - jnp.max(s, axis=-1, keepdims=True))
        p = p / jnp.sum(p, axis=-1, keepdims=True)
        o_ref[...] = jnp.dot(p.astype(MXU_DTYPE), vv, preferred_element_type=F32).astype(o_ref.dtype)

    return pl.pallas_call(
        body, name=name, grid=(bsz, X_HEADS, length // tq),
        in_specs=[pl.BlockSpec((None, tq, X_HD), lambda b, h, i: (b, i, h)),
                  pl.BlockSpec((None, MEM_LEN, X_HD), lambda b, h, i: (b, 0, h)),
                  pl.BlockSpec((None, MEM_LEN, X_HD), lambda b, h, i: (b, 0, X_HEADS + h))],
        out_specs=pl.BlockSpec((None, tq, X_HD), lambda b, h, i: (b, i, h)),
        out_shape=jax.ShapeDtypeStruct(q.shape, BF16),
        compiler_params=pltpu.CompilerParams(dimension_semantics=("parallel", "parallel", "arbitrary")),
    )(q, kv, kv)


def _xattn_bwd(name, q, kv, do, tq=512):
    bsz, length, _ = q.shape
    tq = _divisor(length, tq)
    scale = X_HD ** -0.5

    def body(q_ref, k_ref, v_ref, do_ref, dq_ref, dk_ref, dv_ref):
        qv, kk, vv = q_ref[...].astype(MXU_DTYPE), k_ref[...].astype(MXU_DTYPE), v_ref[...].astype(MXU_DTYPE)
        dov = do_ref[...].astype(MXU_DTYPE)
        s = lax.dot_general(qv, kk, (((1,), (1,)), ((), ())), preferred_element_type=F32) * scale
        p = jnp.exp(s - jnp.max(s, axis=-1, keepdims=True))
        p = p / jnp.sum(p, axis=-1, keepdims=True)
        dp = lax.dot_general(dov, vv, (((1,), (1,)), ((), ())), preferred_element_type=F32)
        ds = p * (dp - jnp.sum(dp * p, axis=-1, keepdims=True)) * scale
        dsb = ds.astype(MXU_DTYPE)
        dq_ref[...] = jnp.dot(dsb, kk, preferred_element_type=F32).astype(dq_ref.dtype)
        dk = lax.dot_general(dsb, qv, (((0,), (0,)), ((), ())), preferred_element_type=F32)
        dv = lax.dot_general(p.astype(MXU_DTYPE), dov, (((0,), (0,)), ((), ())), preferred_element_type=F32)
        first = pl.program_id(2) == 0

        @pl.when(first)
        def _():
            dk_ref[...] = dk
            dv_ref[...] = dv

        @pl.when(jnp.logical_not(first))
        def _():
            dk_ref[...] += dk
            dv_ref[...] += dv

    qspec = pl.BlockSpec((None, tq, X_HD), lambda b, h, i: (b, i, h))
    kspec = pl.BlockSpec((None, MEM_LEN, X_HD), lambda b, h, i: (b, 0, h))
    return pl.pallas_call(
        body, name=name, grid=(bsz, X_HEADS, length // tq),
        in_specs=[qspec, kspec, pl.BlockSpec((None, MEM_LEN, X_HD), lambda b, h, i: (b, 0, X_HEADS + h)), qspec],
        out_specs=[qspec, kspec, kspec],
        out_shape=[jax.ShapeDtypeStruct(q.shape, BF16), jax.ShapeDtypeStruct((bsz, MEM_LEN, D_MODEL), F32),
                   jax.ShapeDtypeStruct((bsz, MEM_LEN, D_MODEL), F32)],
        compiler_params=pltpu.CompilerParams(dimension_semantics=("parallel", "parallel", "arbitrary")),
    )(q, kv, kv, do)


def _dma_sems(*counts):
    return [pltpu.SemaphoreType.DMA((max(c, 1),)) for c in counts]


def _gather_chips(name, blocks, axes):
    n = len(blocks)
    shapes = [b.shape for b in blocks]

    def body(*refs):
        ins, outs = refs[:n], refs[n:2 * n]
        lsem, lrsem, ssem, rsem, fssem, frsem = refs[2 * n:]
        x, y, c = _pos()
        me = 2 * x + y

        def region(a, chip, h):
            _, r, cc = shapes[a]
            hr = r // 2
            if axes[a] == 0:
                return outs[a].at[:, pl.ds(chip * r + h * hr, hr), :]
            return outs[a].at[:, pl.ds(h * hr, hr), pl.ds(chip * cc, cc)]

        def whole(a, chip):
            _, r, cc = shapes[a]
            if axes[a] == 0:
                return outs[a].at[:, pl.ds(chip * r, r), :]
            return outs[a].at[:, :, pl.ds(chip * cc, cc)]

        sends = []
        for a in range(n):
            cp = pltpu.make_async_remote_copy(src_ref=ins[a], dst_ref=whole(a, me), send_sem=lsem.at[a], recv_sem=lrsem.at[a],
                                              device_id=(x, y, 1 - c), device_id_type=MESH)
            cp.start()
            sends.append(cp)
        for a in range(n):
            hr = shapes[a][1] // 2
            for k, (dx, dy) in enumerate(_RELS):
                cp = pltpu.make_async_remote_copy(
                    src_ref=ins[a].at[:, pl.ds(c * hr, hr), :], dst_ref=region(a, me, c),
                    send_sem=ssem.at[3 * a + k], recv_sem=rsem.at[3 * a + k],
                    device_id=(_flip(x, dx), _flip(y, dy), c), device_id_type=MESH)
                cp.start()
                sends.append(cp)
        for a in range(n):
            for k, (dx, dy) in enumerate(_RELS):
                px, py = _flip(x, dx), _flip(y, dy)
                got = region(a, 2 * px + py, c)
                pltpu.make_async_remote_copy(
                    src_ref=got, dst_ref=got, send_sem=ssem.at[3 * a + k], recv_sem=rsem.at[3 * a + k],
                    device_id=(px, py, c), device_id_type=MESH).wait_recv()
                cp = pltpu.make_async_remote_copy(
                    src_ref=got, dst_ref=got, send_sem=fssem.at[3 * a + k], recv_sem=frsem.at[3 * a + k],
                    device_id=(x, y, 1 - c), device_id_type=MESH)
                cp.start()
                sends.append(cp)
        for a in range(n):
            for k, (dx, dy) in enumerate(_RELS):
                got = region(a, 2 * _flip(x, dx) + _flip(y, dy), 1 - c)
                pltpu.make_async_remote_copy(
                    src_ref=got, dst_ref=got, send_sem=fssem.at[3 * a + k], recv_sem=frsem.at[3 * a + k],
                    device_id=(x, y, 1 - c), device_id_type=MESH).wait_recv()
        for a in range(n):
            pltpu.make_async_remote_copy(src_ref=ins[a], dst_ref=whole(a, me), send_sem=lsem.at[a], recv_sem=lrsem.at[a],
                                         device_id=(x, y, 1 - c), device_id_type=MESH).wait_recv()
        for cp in sends:
            cp.wait_send()

    out_shape = [jax.ShapeDtypeStruct((l, 4 * r, c) if ax == 0 else (l, r, 4 * c), b.dtype)
                 for (l, r, c), ax, b in zip(shapes, axes, blocks)]
    return pl.pallas_call(
        body, name=name, in_specs=[ANY] * n, out_specs=[ANY] * n, out_shape=out_shape,
        scratch_shapes=_dma_sems(n, n, 3 * n, 3 * n, 3 * n, 3 * n),
    )(*blocks)


def _pos_vec():
    x, y, c = _pos()
    return jnp.stack([c, 2 * x + y]).astype(jnp.int32)


def _pair_send(name, parts, axes):
    n = len(parts)
    shapes = [p.shape for p in parts]
    ncopy = sum(4 if ax == 0 else 1 for ax in axes)

    def body(*refs):
        ins, theirs = refs[:n], refs[n:2 * n]
        ssem, rsem = refs[2 * n:]
        x, y, c = _pos()
        pending, j = [], 0
        for a in range(n):
            _, rf, _ = shapes[a]
            if axes[a] == 0:
                hr = rf // 8
                pieces = [(ins[a].at[:, pl.ds((2 * s + 1 - c) * hr, hr), :], theirs[a].at[:, s]) for s in range(N_CHIPS)]
            else:
                hr = rf // 2
                pieces = [(ins[a].at[:, pl.ds((1 - c) * hr, hr), :], theirs[a])]
            for give, give_dst in pieces:
                rc = pltpu.make_async_remote_copy(src_ref=give, dst_ref=give_dst, send_sem=ssem.at[j],
                                                  recv_sem=rsem.at[j], device_id=(x, y, 1 - c), device_id_type=MESH)
                rc.start()
                pending.append(rc)
                j += 1
        for cp in pending:
            cp.wait()

    def half_shape(s, ax):
        return (s[0], N_CHIPS, s[1] // 8, s[2]) if ax == 0 else (s[0], s[1] // 2, s[2])

    out_shape = [jax.ShapeDtypeStruct(half_shape(s, ax), p.dtype) for s, ax, p in zip(shapes, axes, parts)]
    return pl.pallas_call(
        body, name=name, in_specs=[ANY] * n, out_specs=[ANY] * n, out_shape=out_shape,
        scratch_shapes=_dma_sems(ncopy, ncopy),
    )(*parts)


def _chip_exchange(name, halves, axes):
    n = len(halves)
    shapes = [h.shape for h in halves]

    def body(*refs):
        ins, outs = refs[:n], refs[n:2 * n]
        ssem, rsem = refs[2 * n:]
        x, y, c = _pos()

        def part(a, chip):
            if axes[a] == 0:
                return ins[a].at[:, chip]
            cc = shapes[a][2] // N_CHIPS
            return ins[a].at[:, :, pl.ds(chip * cc, cc)]

        sends = []
        for a in range(n):
            for k, (dx, dy) in enumerate(_RELS):
                px, py = _flip(x, dx), _flip(y, dy)
                rc = pltpu.make_async_remote_copy(
                    src_ref=part(a, 2 * px + py), dst_ref=outs[a].at[:, k], send_sem=ssem.at[3 * a + k],
                    recv_sem=rsem.at[3 * a + k], device_id=(px, py, c), device_id_type=MESH)
                rc.start()
                sends.append(rc)
        for cp in sends:
            cp.wait()

    def slot_shape(s, ax):
        return (s[0], 3, s[2], s[3]) if ax == 0 else (s[0], 3, s[1], s[2] // N_CHIPS)

    out_shape = [jax.ShapeDtypeStruct(slot_shape(s, ax), h.dtype) for s, ax, h in zip(shapes, axes, halves)]
    return pl.pallas_call(
        body, name=name, in_specs=[ANY] * n, out_specs=[ANY] * n, out_shape=out_shape,
        scratch_shapes=_dma_sems(3 * n, 3 * n),
    )(*halves)


def _pair_join(name, shards):
    n = len(shards)

    def body(*refs):
        outs = refs[n:2 * n]
        ssem, rsem = refs[2 * n:]
        x, y, c = _pos()
        pending = []
        for a in range(n):
            hr = shards[a].shape[1] // 2
            mine = outs[a].at[:, pl.ds(c * hr, hr), :]
            rc = pltpu.make_async_remote_copy(src_ref=mine, dst_ref=mine, send_sem=ssem.at[a], recv_sem=rsem.at[a],
                                              device_id=(x, y, 1 - c), device_id_type=MESH)
            rc.start()
            pending.append(rc)
        for a in range(n):
            hr = shards[a].shape[1] // 2
            got = outs[a].at[:, pl.ds((1 - c) * hr, hr), :]
            pltpu.make_async_remote_copy(src_ref=got, dst_ref=got, send_sem=ssem.at[a], recv_sem=rsem.at[a],
                                         device_id=(x, y, 1 - c), device_id_type=MESH).wait_recv()
        for cp in pending:
            cp.wait_send()

    return pl.pallas_call(
        body, name=name, in_specs=[ANY] * n, out_specs=[ANY] * n,
        out_shape=[jax.ShapeDtypeStruct(s.shape, s.dtype) for s in shards],
        input_output_aliases={a: a for a in range(n)}, scratch_shapes=_dma_sems(n, n),
    )(*shards)


def _pair_add(name, part, theirs, axis, pos):
    layers, rf, cf = part.shape

    def body(pos_ref, a_ref, b_ref, o_ref):
        o_ref[...] = (a_ref[...] + b_ref[...]).astype(o_ref.dtype)

    if axis == 0:
        hr = rf // 8
        grid = (layers, N_CHIPS)
        in_specs = [pl.BlockSpec((None, hr, cf), lambda l, s, p: (l, 2 * s + p[0], 0)),
                    pl.BlockSpec((None, None, hr, cf), lambda l, s, p: (l, s, 0, 0))]
        out_spec = pl.BlockSpec((None, None, hr, cf), lambda l, s, p: (l, s, 0, 0))
    else:
        hr, t = rf // 2, 128
        grid = (layers, hr // t)
        in_specs = [pl.BlockSpec((None, t, cf), lambda l, i, p: (l, p[0] * (hr // t) + i, 0)),
                    pl.BlockSpec((None, t, cf), lambda l, i, p: (l, i, 0))]
        out_spec = pl.BlockSpec((None, t, cf), lambda l, i, p: (l, i, 0))
    return pl.pallas_call(
        body, name=name, out_shape=jax.ShapeDtypeStruct(theirs.shape, BF16),
        grid_spec=pltpu.PrefetchScalarGridSpec(num_scalar_prefetch=1, grid=grid, in_specs=in_specs, out_specs=out_spec),
        compiler_params=pltpu.CompilerParams(dimension_semantics=("arbitrary", "arbitrary")),
    )(pos, part, theirs)


def _chip_sum(name, part, theirs, slots, axis, pos):
    layers, _, hr, c = slots.shape

    def body(pos_ref, mine, sib, s0, s1, s2, o_ref):
        o_ref[...] = (((mine[...] + sib[...]) + s0[...].astype(F32)) + s1[...].astype(F32)) + s2[...].astype(F32)

    t = hr if axis == 0 else 128
    if axis == 0:
        own_specs = [pl.BlockSpec((None, t, c), lambda l, i, p: (l, 2 * p[1] + p[0], 0)),
                     pl.BlockSpec((None, None, t, c), lambda l, i, p: (l, p[1], 0, 0))]
    else:
        own_specs = [pl.BlockSpec((None, t, c), lambda l, i, p: (l, p[0] * (hr // t) + i, p[1])),
                     pl.BlockSpec((None, t, c), lambda l, i, p: (l, i, p[1]))]
    slot_specs = [pl.BlockSpec((None, None, t, c), functools.partial(lambda k, l, i, p: (l, k, i, 0), k)) for k in range(3)]
    return pl.pallas_call(
        body, name=name, out_shape=jax.ShapeDtypeStruct((layers, 2 * hr, c), F32),
        grid_spec=pltpu.PrefetchScalarGridSpec(
            num_scalar_prefetch=1, grid=(layers, hr // t), in_specs=own_specs + slot_specs,
            out_specs=pl.BlockSpec((None, t, c), lambda l, i, p: (l, p[0] * (hr // t) + i, 0))),
        compiler_params=pltpu.CompilerParams(dimension_semantics=("arbitrary", "arbitrary")),
    )(pos, part, theirs, slots, slots, slots)


def _sibling_swap(name, v):
    def body(v_ref, o_ref, ssem, rsem):
        x, y, c = _pos()
        cp = pltpu.make_async_remote_copy(src_ref=v_ref, dst_ref=o_ref, send_sem=ssem.at[0], recv_sem=rsem.at[0],
                                          device_id=(x, y, 1 - c), device_id_type=MESH)
        cp.start()
        cp.wait()

    return pl.pallas_call(body, name=name, in_specs=[ANY], out_specs=ANY, out_shape=jax.ShapeDtypeStruct(v.shape, v.dtype),
                          scratch_shapes=_dma_sems(1, 1))(v)


def _add2(name, a, b):
    shape = a.shape
    a2, b2 = a.reshape(-1, shape[-1]), b.reshape(-1, shape[-1])
    (o,) = _rowwise(name, lambda u, v: u + v, [a2, b2], out_rows=[(shape[-1], F32)], tile=512)
    return o.reshape(shape)


def _sum_slots(name, slots):
    _, hr, c = slots.shape
    t = _divisor(hr, 256)

    def body(s0, s1, s2, s3, o_ref):
        o_ref[...] = ((s0[...] + s1[...]) + s2[...]) + s3[...]

    return pl.pallas_call(
        body, name=name, grid=(hr // t,),
        in_specs=[pl.BlockSpec((None, t, c), functools.partial(lambda k, i: (k, i, 0), k)) for k in range(N_CHIPS)],
        out_specs=pl.BlockSpec((t, c), lambda i: (i, 0)), out_shape=jax.ShapeDtypeStruct((hr, c), F32),
        compiler_params=pltpu.CompilerParams(dimension_semantics=("arbitrary",)),
    )(slots, slots, slots, slots)


def _adam_tile(w, g, m, v):
    m = ADAM_B1 * m + (1.0 - ADAM_B1) * g
    v = ADAM_B2 * v + (1.0 - ADAM_B2) * (g * g)
    m_hat = m / (1.0 - ADAM_B1 ** ADAM_STEP)
    v_hat = v / (1.0 - ADAM_B2 ** ADAM_STEP)
    delta = -ADAM_LR * (m_hat / (jnp.sqrt(v_hat) + ADAM_EPS) + ADAM_WD * w)
    return delta, m, v


def _adam(name, w, g, m, v):
    shape = w.shape
    c = shape[-1]
    flat = [t.reshape(-1, c) for t in (w, g, m, v)]
    res = _rowwise(name, _adam_tile, flat, out_rows=[(c, F32)] * 3, tile=256)
    return [r.reshape(shape) for r in res]


ATT_T = 256
ATT_NEG = -1e30


def _branch_count(length):
    nblk = length // ATT_T
    d = (np.arange(nblk)[:, None, None] * ATT_T + np.arange(ATT_T)[None, :, None] - np.arange(ATT_T)[None, None, :])
    cnt = np.zeros(d.shape, np.float32)
    for window, dil in B_DILATIONS:
        cnt += ((d >= 0) & (d % dil == 0) & (d <= window)).astype(np.float32)
    return jnp.asarray(cnt)


def _rope_tables(length):
    half = B_HD // 2
    inv_freq = ROPE_THETA ** (-jnp.arange(half, dtype=F32) / half)
    ang = jnp.arange(length, dtype=F32)[:, None] * inv_freq[None, :]
    cos, sin = jnp.cos(ang), jnp.sin(ang)
    return jnp.concatenate([cos, cos], axis=1), jnp.concatenate([-sin, sin], axis=1)


def _swap_halves(t):
    return pltpu.roll(t, B_HD // 2, 1)


def _rope_qkv(name, z, cos, sin, t=256):
    bsz, length, _ = z.shape
    t = _divisor(length, t)

    def body(q_ref, k_ref, v_ref, c_ref, s_ref, qo, ko, vo):
        c, s = c_ref[...], s_ref[...]
        for src, dst in ((q_ref, qo), (k_ref, ko)):
            for h in range(B_HEADS):
                cols = slice(h * B_HD, (h + 1) * B_HD)
                xh = src[:, cols]
                dst[:, cols] = (xh * c + _swap_halves(xh) * s).astype(dst.dtype)
        vo[...] = v_ref[...].astype(vo.dtype)

    col0 = 4 * A_WIDTH // B_WIDTH
    specs = [pl.BlockSpec((None, t, B_WIDTH), functools.partial(lambda k, b, i: (b, i, col0 + k), k)) for k in range(3)]
    tab = pl.BlockSpec((t, B_HD), lambda b, i: (i, 0))
    out = pl.BlockSpec((None, t, B_WIDTH), lambda b, i: (b, i, 0))
    return pl.pallas_call(
        body, name=name, grid=(bsz, length // t), in_specs=specs + [tab, tab], out_specs=[out] * 3,
        out_shape=[jax.ShapeDtypeStruct((bsz, length, B_WIDTH), BF16)] * 3,
        compiler_params=pltpu.CompilerParams(dimension_semantics=("parallel", "parallel")),
    )(z, z, z, cos, sin)


def _dilated_fwd(name, q, k, v, cnt):
    bsz, length, _ = q.shape
    scale = B_HD ** -0.5
    nblk = length // ATT_T

    def body(cnt_ref, q_ref, k_ref, v_ref, o_ref, lse_ref):
        i = pl.program_id(2)
        qb = q_ref[...]

        def step(j, carry):
            m, l, acc = carry
            rows = pl.ds(pl.multiple_of(j * ATT_T, ATT_T), ATT_T)
            s = lax.dot_general(qb, k_ref[rows, :], (((1,), (1,)), ((), ())), preferred_element_type=F32) * scale
            c = cnt_ref[i - j]
            s = jnp.where(c > 0.0, s, ATT_NEG)
            m_new = jnp.maximum(m, jnp.max(s, axis=-1, keepdims=True))
            a = jnp.exp(m - m_new)
            p = c * jnp.exp(s - m_new)
            l = a * l + jnp.sum(p, axis=-1, keepdims=True)
            acc = a * acc + jnp.dot(p.astype(MXU_DTYPE), v_ref[rows, :], preferred_element_type=F32)
            return m_new, l, acc

        init = (jnp.full((ATT_T, 1), ATT_NEG, F32), jnp.zeros((ATT_T, 1), F32), jnp.zeros((ATT_T, B_HD), F32))
        m, l, acc = lax.fori_loop(0, i + 1, step, init)
        o_ref[...] = acc / l
        lse_ref[...] = jnp.broadcast_to(m + jnp.log(l), (ATT_T, B_HD))

    qspec = pl.BlockSpec((None, ATT_T, B_HD), lambda b, h, i: (b, i, h))
    kspec = pl.BlockSpec((None, length, B_HD), lambda b, h, i: (b, 0, h))
    return pl.pallas_call(
        body, name=name, grid=(bsz, B_HEADS, nblk),
        in_specs=[pl.BlockSpec(cnt.shape, lambda b, h, i: (0, 0, 0)), qspec, kspec, kspec],
        out_specs=[qspec, pl.BlockSpec((None, None, ATT_T, B_HD), lambda b, h, i: (b, h, i, 0))],
        out_shape=[jax.ShapeDtypeStruct((bsz, length, B_WIDTH), F32), jax.ShapeDtypeStruct((bsz, B_HEADS, length, B_HD), F32)],
        compiler_params=pltpu.CompilerParams(dimension_semantics=("parallel", "parallel", "arbitrary")),
    )(cnt, q, k, v)


def _dilated_bwd(name, q, k, v, o, lse, do, cnt, cos, sin, off=0):
    bsz, length, _ = q.shape
    scale = B_HD ** -0.5
    nblk = length // ATT_T

    def body(cnt_ref, q_ref, k_ref, v_ref, o_ref, lse_ref, do_ref, c_ref, s_ref, dq_ref, dk_ref, dv_ref, dq_acc, dk_acc, dv_acc):
        dk_acc[...] = jnp.zeros_like(dk_acc)
        dv_acc[...] = jnp.zeros_like(dv_acc)

        def outer(i, _):
            rq = pl.ds(pl.multiple_of(i * ATT_T, ATT_T), ATT_T)
            qi, doi = q_ref[rq, :], do_ref[rq, :]
            lsei = lse_ref[rq, :][:, 0:1]
            di = jnp.sum(doi * o_ref[rq, :], axis=-1, keepdims=True)
            dob = doi.astype(MXU_DTYPE)

            def inner(j, dq):
                rk = pl.ds(pl.multiple_of(j * ATT_T, ATT_T), ATT_T)
                kj, vj = k_ref[rk, :], v_ref[rk, :]
                s = lax.dot_general(qi, kj, (((1,), (1,)), ((), ())), preferred_element_type=F32) * scale
                c = cnt_ref[i - j]
                p = c * jnp.exp(jnp.where(c > 0.0, s, ATT_NEG) - lsei)
                dp = lax.dot_general(dob, vj, (((1,), (1,)), ((), ())), preferred_element_type=F32)
                ds = (p * (dp - di) * scale).astype(MXU_DTYPE)
                dk_acc[rk, :] += lax.dot_general(ds, qi, (((0,), (0,)), ((), ())), preferred_element_type=F32)
                dv_acc[rk, :] += lax.dot_general(p.astype(MXU_DTYPE), dob, (((0,), (0,)), ((), ())), preferred_element_type=F32)
                return dq + jnp.dot(ds, kj, preferred_element_type=F32)

            dq_acc[rq, :] = lax.fori_loop(0, i + 1, inner, jnp.zeros((ATT_T, B_HD), F32))
            return 0

        lax.fori_loop(0, nblk, outer, 0)
        c, s = c_ref[...], s_ref[...]
        for acc, dst in ((dq_acc, dq_ref), (dk_acc, dk_ref)):
            g = acc[...]
            dst[...] = (g * c + _swap_halves(g * s)).astype(dst.dtype)
        dv_ref[...] = dv_acc[...].astype(dv_ref.dtype)

    hspec = pl.BlockSpec((None, length, B_HD), lambda b, h: (b, 0, h))
    ospec = pl.BlockSpec((None, length, B_HD), lambda b, h: (b, 0, off + h))
    tab = pl.BlockSpec((length, B_HD), lambda b, h: (0, 0))
    return pl.pallas_call(
        body, name=name, grid=(bsz, B_HEADS),
        in_specs=[pl.BlockSpec(cnt.shape, lambda b, h: (0, 0, 0)), hspec, hspec, hspec, ospec,
                  pl.BlockSpec((None, None, length, B_HD), lambda b, h: (b, h, 0, 0)), ospec, tab, tab],
        out_specs=[hspec] * 3, out_shape=[jax.ShapeDtypeStruct((bsz, length, B_WIDTH), BF16)] * 3,
        scratch_shapes=[pltpu.VMEM((length, B_HD), F32)] * 3,
        compiler_params=pltpu.CompilerParams(dimension_semantics=("parallel", "parallel")),
    )(cnt, q, k, v, o, lse, do, cos, sin)


def _chunk_cumsum(t, reverse):
    n = t.shape[0]
    row = lax.broadcasted_iota(jnp.int32, t.shape, 0) & (A_CHUNK - 1)
    s = 1
    while s < A_CHUNK:
        if reverse:
            t = t + jnp.where(row < A_CHUNK - s, pltpu.roll(t, n - s, 0), 0.0)
        else:
            t = t + jnp.where(row >= s, pltpu.roll(t, s, 0), 0.0)
        s *= 2
    return t


def _hgrn_gates(fl, lb):
    sg = jax.nn.sigmoid(fl)
    f = lb + (1.0 - lb) * sg
    return sg, f


def _hgrn_chunks(nchunk, qd_s, ki_s, b_s, v_ref, o_s, st_s=None):
    tri = lax.broadcasted_iota(jnp.int32, (A_CHUNK, A_CHUNK), 0) >= lax.broadcasted_iota(jnp.int32, (A_CHUNK, A_CHUNK), 1)

    def step(n, st):
        rows = pl.ds(pl.multiple_of(n * A_CHUNK, A_CHUNK), A_CHUNK)
        if st_s is not None:
            st_s[n] = st
        qd, ki, vc = qd_s[rows, :].astype(MXU_DTYPE), ki_s[rows, :], v_ref[rows, :].astype(MXU_DTYPE)
        dec = jnp.exp(b_s[pl.ds(n * A_CHUNK + A_CHUNK - 1, 1), :])
        a = lax.dot_general(qd, ki.astype(MXU_DTYPE), (((1,), (1,)), ((), ())), preferred_element_type=F32)
        a = jnp.where(tri, a, 0.0).astype(MXU_DTYPE)
        o_s[rows, :] = (jnp.dot(a, vc, preferred_element_type=F32)
                        + lax.dot_general(qd, st.astype(MXU_DTYPE), (((1,), (1,)), ((), ())), preferred_element_type=F32))
        ke = (ki * dec).astype(MXU_DTYPE)
        return st * dec + lax.dot_general(vc, ke, (((0,), (0,)), ((), ())), preferred_element_type=F32)

    lax.fori_loop(0, nchunk, step, jnp.zeros((A_DK, A_DK), F32), unroll=4)


def _hgrn_fwd(name, z, lb, onw):
    bsz, length, _ = z.shape
    nchunk = length // A_CHUNK

    def body(q_ref, f_ref, v_ref, g_ref, lb_ref, w_ref, y_ref, qd_s, ki_s, b_s, o_s):
        _, f = _hgrn_gates(f_ref[...], lb_ref[...])
        b = _chunk_cumsum(jnp.log(f), False)
        b_s[...] = b
        qd_s[...] = q_ref[...] * jnp.exp(b)
        ki_s[...] = (1.0 - f) * jnp.exp(-b)
        _hgrn_chunks(nchunk, qd_s, ki_s, b_s, v_ref, o_s)
        o = o_s[...]
        on = o * lax.rsqrt(jnp.mean(o * o, axis=-1, keepdims=True) + NORM_EPS)
        y_ref[...] = on * w_ref[...] * _silu(g_ref[...])

    cols = [pl.BlockSpec((None, length, A_DK), functools.partial(lambda k, b, h: (b, 0, k * A_HEADS + h), k)) for k in range(4)]
    vec = pl.BlockSpec((1, A_DK), lambda b, h: (0, h))
    return pl.pallas_call(
        body, name=name, grid=(bsz, A_HEADS), in_specs=cols + [vec, vec],
        out_specs=pl.BlockSpec((None, length, A_DK), lambda b, h: (b, 0, h)),
        out_shape=jax.ShapeDtypeStruct((bsz, length, A_WIDTH), F32),
        scratch_shapes=[pltpu.VMEM((length, A_DK), F32)] * 4,
        compiler_params=pltpu.CompilerParams(dimension_semantics=("parallel", "parallel")),
    )(z, z, z, z, lb, onw)


def _hgrn_bwd(name, z, lb, onw, dy):
    bsz, length, _ = z.shape
    nchunk = length // A_CHUNK

    def body(q_ref, f_ref, v_ref, g_ref, lb_ref, w_ref, dy_ref, dq_ref, df_ref, dv_ref, dg_ref, dlb_ref, dw_ref,
             qd_s, ki_s, b_s, o_s, st_s, dqd_s, dki_s, dbl_s):
        lb = lb_ref[...]
        sg, f = _hgrn_gates(f_ref[...], lb)
        b = _chunk_cumsum(jnp.log(f), False)
        b_s[...] = b
        qd_s[...] = q_ref[...] * jnp.exp(b)
        ki_s[...] = (1.0 - f) * jnp.exp(-b)
        _hgrn_chunks(nchunk, qd_s, ki_s, b_s, v_ref, o_s, st_s)
        o, g, w, dyv = o_s[...], g_ref[...], w_ref[...], dy_ref[...]
        r = lax.rsqrt(jnp.mean(o * o, axis=-1, keepdims=True) + NORM_EPS)
        on = o * r
        sgg = jax.nn.sigmoid(g)
        gate = g * sgg
        dg_ref[...] = (dyv * on * w * (sgg * (1.0 + g * (1.0 - sgg)))).astype(dg_ref.dtype)
        dw = jnp.sum(dyv * on * gate, axis=0, keepdims=True)
        don = dyv * w * gate
        o_s[...] = r * (don - on * jnp.mean(don * on, axis=-1, keepdims=True))
        tri = lax.broadcasted_iota(jnp.int32, (A_CHUNK, A_CHUNK), 0) >= lax.broadcasted_iota(jnp.int32, (A_CHUNK, A_CHUNK), 1)
        last = lax.broadcasted_iota(jnp.int32, (A_CHUNK, A_DK), 0) == A_CHUNK - 1

        def back(t, dst):
            n = nchunk - 1 - t
            rows = pl.ds(pl.multiple_of(n * A_CHUNK, A_CHUNK), A_CHUNK)
            qd, ki, vc = qd_s[rows, :].astype(MXU_DTYPE), ki_s[rows, :], v_ref[rows, :].astype(MXU_DTYPE)
            kib = ki.astype(MXU_DTYPE)
            do = o_s[rows, :].astype(MXU_DTYPE)
            st = st_s[n]
            dec = jnp.exp(b_s[pl.ds(n * A_CHUNK + A_CHUNK - 1, 1), :])
            dstb = dst.astype(MXU_DTYPE)
            a = lax.dot_general(qd, kib, (((1,), (1,)), ((), ())), preferred_element_type=F32)
            a = jnp.where(tri, a, 0.0).astype(MXU_DTYPE)
            da = lax.dot_general(do, vc, (((1,), (1,)), ((), ())), preferred_element_type=F32)
            da = jnp.where(tri, da, 0.0).astype(MXU_DTYPE)
            ke = (ki * dec).astype(MXU_DTYPE)
            dv_ref[rows, :] = (lax.dot_general(a, do, (((0,), (0,)), ((), ())), preferred_element_type=F32)
                               + lax.dot_general(ke, dstb, (((1,), (1,)), ((), ())), preferred_element_type=F32)).astype(dv_ref.dtype)
            dqd_s[rows, :] = (jnp.dot(da, kib, preferred_element_type=F32)
                              + jnp.dot(do, st.astype(MXU_DTYPE), preferred_element_type=F32))
            dke = jnp.dot(vc, dstb, preferred_element_type=F32)
            dki_s[rows, :] = lax.dot_general(da, qd, (((0,), (0,)), ((), ())), preferred_element_type=F32) + dke * dec
            ddec = jnp.sum(dst * st, axis=0, keepdims=True) + jnp.sum(dke * ki, axis=0, keepdims=True)
            dbl_s[rows, :] = jnp.where(last, ddec * dec, 0.0)
            return dst * dec + lax.dot_general(do, qd, (((0,), (0,)), ((), ())), preferred_element_type=F32)

        lax.fori_loop(0, nchunk, back, jnp.zeros((A_DK, A_DK), F32), unroll=2)
        dqd, dki, qd, ki = dqd_s[...], dki_s[...], qd_s[...], ki_s[...]
        b = b_s[...]
        dlf = _chunk_cumsum(dqd * qd - dki * ki + dbl_s[...], True)
        dq_ref[...] = (dqd * jnp.exp(b)).astype(dq_ref.dtype)
        dfv = dlf / f - dki * jnp.exp(-b)
        df_ref[...] = (dfv * (1.0 - lb) * sg * (1.0 - sg)).astype(df_ref.dtype)
        dlb = jnp.sum(dfv * (1.0 - sg), axis=0, keepdims=True)
        first = pl.program_id(1) == 0

        @pl.when(first)
        def _():
            dlb_ref[...] = dlb
            dw_ref[...] = dw

        @pl.when(jnp.logical_not(first))
        def _():
            dlb_ref[...] += dlb
            dw_ref[...] += dw

    cols = [pl.BlockSpec((None, length, A_DK), functools.partial(lambda k, h, b: (b, 0, k * A_HEADS + h), k)) for k in range(4)]
    vec = pl.BlockSpec((1, A_DK), lambda h, b: (0, h))
    head = pl.BlockSpec((None, length, A_DK), lambda h, b: (b, 0, h))
    act = jax.ShapeDtypeStruct((bsz, length, A_WIDTH), BF16)
    return pl.pallas_call(
        body, name=name, grid=(A_HEADS, bsz), in_specs=cols + [vec, vec, head],
        out_specs=[head] * 4 + [vec, vec], out_shape=[act] * 4 + [jax.ShapeDtypeStruct((1, A_WIDTH), F32)] * 2,
        scratch_shapes=[pltpu.VMEM((length, A_DK), F32)] * 4 + [pltpu.VMEM((nchunk, A_DK, A_DK), F32)]
        + [pltpu.VMEM((length, A_DK), F32)] * 3,
        compiler_params=pltpu.CompilerParams(dimension_semantics=("parallel", "arbitrary")),
    )(z, z, z, z, lb, onw, dy)


S5_SEG = 16
S5_W = 256
S5_LANES = C_GROUPS * C_STATE
S5_NB = 8
S5_CH = D_MODEL // S5_NB
S5_COLS = 2 * S5_LANES // S5_NB


def _bd_mm(name, a, b, tb=False, out_dtype=F32, tm=1024):
    n = a.shape[0]
    nb, ka, kn = (b.shape[0], b.shape[2], b.shape[1]) if tb else b.shape
    tm = _divisor(n, tm)
    dims = (((1,), (1 if tb else 0,)), ((), ()))

    def body(a_ref, b_ref, o_ref):
        o_ref[...] = lax.dot_general(a_ref[...].astype(MXU_DTYPE), b_ref[...].astype(MXU_DTYPE), dims,
                                     preferred_element_type=F32).astype(o_ref.dtype)

    return pl.pallas_call(
        body, name=name, grid=(n // tm, nb),
        in_specs=[pl.BlockSpec((tm, ka), lambda i, j: (i, j)), pl.BlockSpec((None,) + b.shape[1:], lambda i, j: (j, 0, 0))],
        out_specs=pl.BlockSpec((tm, kn), lambda i, j: (i, j)), out_shape=jax.ShapeDtypeStruct((n, nb * kn), out_dtype),
        compiler_params=pltpu.CompilerParams(dimension_semantics=("parallel", "parallel")),
    )(a, b)


def _bd_wgrad(name, a, c, ka, kn, tk=1024):
    n = a.shape[0]
    nb = a.shape[1] // ka
    tk = _divisor(n, tk)

    def body(a_ref, c_ref, o_ref):
        p = lax.dot_general(a_ref[...].astype(MXU_DTYPE), c_ref[...].astype(MXU_DTYPE), (((0,), (0,)), ((), ())),
                            preferred_element_type=F32)
        first = pl.program_id(1) == 0

        @pl.when(first)
        def _():
            o_ref[...] = p

        @pl.when(jnp.logical_not(first))
        def _():
            o_ref[...] += p

    return pl.pallas_call(
        body, name=name, grid=(nb, n // tk),
        in_specs=[pl.BlockSpec((tk, ka), lambda j, k: (k, j)), pl.BlockSpec((tk, kn), lambda j, k: (k, j))],
        out_specs=pl.BlockSpec((None, ka, kn), lambda j, k: (j, 0, 0)), out_shape=jax.ShapeDtypeStruct((nb, ka, kn), F32),
        compiler_params=pltpu.CompilerParams(dimension_semantics=("parallel", "arbitrary")),
    )(a, c)


def _seg_permute(t, bsz):
    n, c = t.shape
    return t.reshape(bsz, S5_SEG, n // bsz // S5_SEG, c).transpose(0, 2, 1, 3).reshape(n, c)


def _seg_unpermute(t, bsz):
    n, c = t.shape
    return t.reshape(bsz, n // bsz // S5_SEG, S5_SEG, c).transpose(0, 2, 1, 3).reshape(n, c)


def _s5_weights(lam_re, lam_im, log_dt, b_re, b_im, c_re, c_im):
    lr = jnp.minimum(lam_re, C_MIN_NEG_RE)
    li = lam_im
    dt = jnp.exp(log_dt)[:, None]
    mag = jnp.exp(dt * lr)
    ar, ai = mag * jnp.cos(dt * li), mag * jnp.sin(dt * li)
    den = lr * lr + li * li
    zr = ((ar - 1.0) * lr + ai * li) / den
    zi = (ai * lr - (ar - 1.0) * li) / den
    bbr = zr[..., None] * b_re - zi[..., None] * b_im
    bbi = zr[..., None] * b_im + zi[..., None] * b_re
    gpb = C_GROUPS // S5_NB
    eye = jnp.eye(gpb, dtype=F32)
    bb = jnp.stack([bbr, bbi]).reshape(2, S5_NB, gpb, C_STATE, C_GROUP)
    wb = jnp.einsum('ij,rbjpc->bicjpr', eye, bb).reshape(S5_NB, S5_CH, -1, S5_W, 2)
    wb = wb.transpose(0, 1, 2, 4, 3).reshape(S5_NB, S5_CH, S5_COLS)
    cc = jnp.stack([c_re, -c_im]).reshape(2, S5_NB, gpb, C_GROUP, C_STATE)
    wc = jnp.einsum('ij,rbjcp->bjpric', eye, cc).reshape(S5_NB, -1, S5_W, 2, S5_CH)
    wc = wc.transpose(0, 1, 3, 2, 4).reshape(S5_NB, S5_COLS, S5_CH)
    return ar.reshape(1, S5_LANES), ai.reshape(1, S5_LANES), wb, wc


def _s5_scan(name, bu, a_re, a_im, bsz, reverse):
    n, width = bu.shape
    length = n // bsz
    steps = length // S5_SEG
    assert steps & (steps - 1) == 0
    w = S5_W

    def body(bu_ref, ar_ref, ai_ref, x_ref):
        ar = jnp.broadcast_to(ar_ref[...], (S5_SEG, w))
        ai = jnp.broadcast_to(ai_ref[...], (S5_SEG, w))
        if reverse:
            ai = -ai
        zero = jnp.zeros((S5_SEG, w), F32)

        def rows_of(j):
            jj = steps - 1 - j if reverse else j
            return pl.ds(pl.multiple_of(jj * S5_SEG, S5_SEG), S5_SEG)

        def local_step(j, st):
            sr, si = st
            rows = rows_of(j)
            nr = ar * sr - ai * si + bu_ref[rows, 0:w]
            ni = ar * si + ai * sr + bu_ref[rows, w:2 * w]
            x_ref[rows, 0:w] = nr
            x_ref[rows, w:2 * w] = ni
            return nr, ni

        er, ei = lax.fori_loop(0, steps, local_step, (zero, zero), unroll=4)
        pr, pi = ar[0:1], ai[0:1]
        for _ in range(steps.bit_length() - 1):
            pr, pi = pr * pr - pi * pi, 2.0 * pr * pi
        row = lax.broadcasted_iota(jnp.int32, (S5_SEG, w), 0)
        cr, ci = zero, zero
        inr, ini = jnp.zeros((1, w), F32), jnp.zeros((1, w), F32)
        order = list(range(S5_SEG))[::-1] if reverse else list(range(S5_SEG))
        for idx, s in enumerate(order):
            if idx:
                cr = jnp.where(row == s, inr, cr)
                ci = jnp.where(row == s, ini, ci)
            inr, ini = er[s:s + 1] + pr * inr - pi * ini, ei[s:s + 1] + pr * ini + pi * inr

        def carry_step(j, st):
            qr, qi = st
            rows = rows_of(j)
            x_ref[rows, 0:w] += qr * cr - qi * ci
            x_ref[rows, w:2 * w] += qr * ci + qi * cr
            return qr * ar - qi * ai, qr * ai + qi * ar

        lax.fori_loop(0, steps, carry_step, (ar, ai), unroll=4)

    blk = pl.BlockSpec((length, 2 * w), lambda b, j: (b, j))
    aspec = pl.BlockSpec((1, w), lambda b, j: (0, j))
    return pl.pallas_call(
        body, name=name, grid=(bsz, width // (2 * w)), in_specs=[blk, aspec, aspec], out_specs=blk,
        out_shape=jax.ShapeDtypeStruct(bu.shape, F32),
        compiler_params=pltpu.CompilerParams(dimension_semantics=("parallel", "parallel")),
    )(bu, a_re, a_im)


def _s5_da(name, x, g, bsz):
    n, width = x.shape
    length = n // bsz
    steps = length // S5_SEG
    w = S5_W

    def body(x_ref, g_ref, o_ref):
        row = lax.broadcasted_iota(jnp.int32, (S5_SEG, w), 0)
        last = pl.ds((steps - 1) * S5_SEG, S5_SEG)
        xpr = jnp.where(row == 0, 0.0, pltpu.roll(x_ref[last, 0:w], 1, 0))
        xpi = jnp.where(row == 0, 0.0, pltpu.roll(x_ref[last, w:2 * w], 1, 0))
        zero = jnp.zeros((S5_SEG, w), F32)

        def step(j, st):
            pr, pi, accr, acci = st
            rows = pl.ds(pl.multiple_of(j * S5_SEG, S5_SEG), S5_SEG)
            gr, gi = g_ref[rows, 0:w], g_ref[rows, w:2 * w]
            return x_ref[rows, 0:w], x_ref[rows, w:2 * w], accr + gr * pr + gi * pi, acci + gi * pr - gr * pi

        _, _, accr, acci = lax.fori_loop(0, steps, step, (xpr, xpi, zero, zero), unroll=4)
        first = pl.program_id(1) == 0

        @pl.when(first)
        def _():
            o_ref[:, 0:w] = accr
            o_ref[:, w:2 * w] = acci

        @pl.when(jnp.logical_not(first))
        def _():
            o_ref[:, 0:w] += accr
            o_ref[:, w:2 * w] += acci

    blk = pl.BlockSpec((length, 2 * w), lambda j, b: (b, j))
    return pl.pallas_call(
        body, name=name, grid=(width // (2 * w), bsz), in_specs=[blk, blk],
        out_specs=pl.BlockSpec((S5_SEG, 2 * w), lambda j, b: (0, j)), out_shape=jax.ShapeDtypeStruct((S5_SEG, width), F32),
        compiler_params=pltpu.CompilerParams(dimension_semantics=("parallel", "arbitrary")),
    )(x, g)


def _gelu(y):
    return 0.5 * y * (1.0 + lax.erf(y * math.sqrt(0.5)))


def _gelu_grad(y):
    return 0.5 * (1.0 + lax.erf(y * math.sqrt(0.5))) + y * jnp.exp(-0.5 * y * y) * (1.0 / math.sqrt(2.0 * math.pi))


def _s5_fwd(h, params, d_skip, bsz):
    (a_re, a_im, wb, wc), w_vjp = jax.vjp(_s5_weights, *params)
    wb, wc = wb.astype(BF16), wc.astype(BF16)
    hp = _seg_permute(h, bsz)
    bu = _bd_mm("s5_bu", hp, wb)
    xs = _s5_scan("s5_scan_f", bu, a_re, a_im, bsz, False)
    yc = _bd_mm("s5_cx", xs, wc)
    ypre, glp = _rowwise("s5_gelu", lambda yy, uu, dd: (lambda t: (t, _gelu(t)))(yy + dd * uu), [yc, hp], [d_skip],
                         out_rows=[(D_MODEL, F32), (D_MODEL, BF16)])
    return _seg_unpermute(glp, bsz), dict(hp=hp, xs=xs, ypre=ypre, a_re=a_re, a_im=a_im, wb=wb, wc=wc, w_vjp=w_vjp)


def _s5_bwd(dgl, sv, d_skip, bsz):
    dyp, dskip, dd = _rowwise(
        "b_s5_gelu", lambda dg, yy, uu, ds: (lambda t: (t, t * ds, jnp.sum(t * uu, axis=0, keepdims=True)))(dg * _gelu_grad(yy)),
        [_seg_permute(dgl, bsz), sv["ypre"], sv["hp"]], [d_skip], out_rows=[(D_MODEL, BF16), (D_MODEL, F32)],
        out_sums=[D_MODEL])
    dxh = _bd_mm("b_s5_cx_dx", dyp, sv["wc"], tb=True)
    dwc = _bd_wgrad("b_s5_cx_dw", sv["xs"], dyp, S5_COLS, S5_CH)
    gs = _s5_scan("s5_scan_b", dxh, sv["a_re"], sv["a_im"], bsz, True)
    da = _s5_da("s5_da", sv["xs"], gs, bsz)
    du = _bd_mm("b_s5_bu_dx", gs, sv["wb"], tb=True)
    dwb = _bd_wgrad("b_s5_bu_dw", sv["hp"], gs, S5_CH, S5_COLS)
    da = jnp.sum(da, axis=0).reshape(S5_LANES // S5_W, 2, S5_W)
    dp = sv["w_vjp"]((da[:, 0].reshape(1, S5_LANES), da[:, 1].reshape(1, S5_LANES), dwb, dwc))
    return _seg_unpermute(du + dskip, bsz), dp, dd


def _pack_rows(arrays):
    rows = []
    for a in arrays:
        flat = a.reshape(-1).astype(F32)
        pad = (-flat.shape[0]) % PACK_COLS
        rows.append(jnp.pad(flat, (0, pad)).reshape(-1, PACK_COLS))
    out = jnp.concatenate(rows, axis=0)
    return jnp.pad(out, ((0, (-out.shape[0]) % 16), (0, 0)))


def _unpack_rows(packed, shapes):
    out, r = [], 0
    for s in shapes:
        size = int(np.prod(s))
        nr = -(-size // PACK_COLS)
        out.append(packed[r:r + nr].reshape(-1)[:size].reshape(s))
        r += nr
    return out


def kernel(x, mem, norm_w, mem_norm_w, ab_w_in, ab_w_out, hgrn_lb_logits, hgrn_out_norm_w, s5_lambda_re, s5_lambda_im, s5_log_dt, s5_b_re, s5_b_im, s5_c_re, s5_c_im, s5_d, s5_w_glu, xattn_wq, xattn_wkv, xattn_wo, ffn_w_in, ffn_w_out, loss_target, m_norm_w, m_mem_norm_w, m_ab_w_in, m_ab_w_out, m_hgrn_lb_logits, m_hgrn_out_norm_w, m_s5_lambda_re, m_s5_lambda_im, m_s5_log_dt, m_s5_b_re, m_s5_b_im, m_s5_c_re, m_s5_c_im, m_s5_d, m_s5_w_glu, m_xattn_wq, m_xattn_wkv, m_xattn_wo, m_ffn_w_in, m_ffn_w_out, v_norm_w, v_mem_norm_w, v_ab_w_in, v_ab_w_out, v_hgrn_lb_logits, v_hgrn_out_norm_w, v_s5_lambda_re, v_s5_lambda_im, v_s5_log_dt, v_s5_b_re, v_s5_b_im, v_s5_c_re, v_s5_c_im, v_s5_d, v_s5_w_glu, v_xattn_wq, v_xattn_wkv, v_xattn_wo, v_ffn_w_in, v_ffn_w_out):
    given = dict(locals())
    w = {n: given[n] for n in WEIGHTS}
    mom = {n: given["m_" + n] for n in WEIGHTS}
    var = {n: given["v_" + n] for n in WEIGHTS}
    bsz, length, _ = x.shape
    ntok = bsz * length
    chip = 2 * lax.axis_index("x") + lax.axis_index("y")

    big_axes = [ax for _, ax in BIG]
    full = _gather_chips("gather_weights", [w[n].astype(BF16) for n in BIG_NAMES], big_axes)
    wf = dict(zip(BIG_NAMES, full))
    small_block = jnp.concatenate([w['norm_w'].reshape(12, -1), w['s5_d'].reshape(1, -1), jnp.zeros((3, 256), F32)], axis=0)
    (small_full,) = _gather_chips("gather_norm_w", [small_block[None]], [1])
    nw = small_full[0, :12].reshape(2, 6, 1, D_MODEL)
    s5_d_full = small_full[0, 12:13]

    lb_table, lb_vjp = jax.vjp(lambda t: jnp.cumsum(jax.nn.softmax(t, axis=0), axis=0), w['hgrn_lb_logits'])
    xs = x.reshape(ntok, D_MODEL)
    mem2 = mem.reshape(bsz * MEM_LEN, D_MODEL)
    tgt = loss_target.reshape(ntok, D_MODEL)
    saved = []
    (h,) = _rowwise("norm_in", lambda a, g: _rms(a, g), [xs], [nw[0, 0]], out_rows=[(D_MODEL, BF16)])
    cur = xs
    for layer in range(2):
        sv = {"x": cur}
        if layer == 0:
            z = _mm("ab_in", h, wf['ab_w_in'][0]).reshape(bsz, length, -1)
            sv["h0"] = h
            rope_cos, rope_sin = _rope_tables(length)
            branch_cnt = _branch_count(length)
            oa = _hgrn_fwd("hgrn_f", z, lb_table[0:1], w['hgrn_out_norm_w'])
            qr, kr, vb = _rope_qkv("rope_qkv", z, rope_cos, rope_sin)
            ob, lse = _dilated_fwd("dilated_f", qr, kr, vb, branch_cnt)
            core = jnp.concatenate([oa, ob], axis=-1).reshape(ntok, D_MODEL)
            sv.update(z=z, qr=qr, kr=kr, vb=vb, lse=lse, core=core)
            y = _mm("ab_out", core, wf['ab_w_out'][0])
        else:
            s5p = [w[n][0] for n in ('s5_lambda_re', 's5_lambda_im', 's5_log_dt', 's5_b_re', 's5_b_im', 's5_c_re', 's5_c_im')]
            gl, sv["s5"] = _s5_fwd(h, s5p, s5_d_full, bsz)
            sv["gl"] = gl
            zg = _mm("s5_glu", gl, wf['s5_w_glu'][0])
            sv["zg"] = zg
            (y,) = _rowwise("s5_gate", lambda t: t[:, :D_MODEL] * jax.nn.sigmoid(t[:, D_MODEL:]), [zg],
                            out_rows=[(D_MODEL, F32)])
        sv["y1"] = y
        x1, h2 = _rowwise(f"resnorm_a{layer}", lambda a, b, g1, g2: (lambda s: (s, _rms(s, g2)))(a + _rms(b, g1)),
                          [cur, y], [nw[layer, 1], nw[layer, 2]], out_rows=[(D_MODEL, F32), (D_MODEL, BF16)])
        sv["x1"], sv["h2"] = x1, h2
        (mem_n,) = _rowwise(f"mem_norm{layer}", lambda a, g: _rms(a, g), [mem2], [w['mem_norm_w'][layer][None]],
                            out_rows=[(D_MODEL, BF16)])
        sv["mem_n"] = mem_n
        q = _mm(f"xq{layer}", h2, wf['xattn_wq'][layer])
        kv = _mm(f"xkv{layer}", mem_n, wf['xattn_wkv'][layer])
        sv["q"], sv["kv"] = q, kv
        o = _xattn_fwd(f"xattn_f{layer}", q.reshape(bsz, length, D_MODEL), kv.reshape(bsz, MEM_LEN, 2 * D_MODEL))
        o = o.reshape(ntok, D_MODEL)
        sv["o"] = o
        y2 = _mm(f"xo{layer}", o, wf['xattn_wo'][layer])
        sv["y2"] = y2
        x2, h4 = _rowwise(f"resnorm_b{layer}", lambda a, b, g1, g2: (lambda s: (s, _rms(s, g2)))(a + _rms(b, g1)),
                          [x1, y2], [nw[layer, 3], nw[layer, 4]], out_rows=[(D_MODEL, F32), (D_MODEL, BF16)])
        sv["x2"], sv["h4"] = x2, h4
        zf = _mm(f"ffn_in{layer}", h4, wf['ffn_w_in'][layer])
        sv["zf"] = zf
        (act,) = _rowwise(f"swiglu{layer}", lambda t: _silu(t[:, :D_FF]) * t[:, D_FF:], [zf], out_rows=[(D_FF, BF16)], tile=128)
        sv["act"] = act
        y3 = _mm(f"ffn_out{layer}", act, wf['ffn_w_out'][layer])
        sv["y3"] = y3
        saved.append(sv)
        if layer == 0:
            cur, h = _rowwise("resnorm_c0", lambda a, b, g1, g2: (lambda s: (s, _rms(s, g2)))(a + _rms(b, g1)),
                              [x2, y3], [nw[0, 5], nw[1, 0]], out_rows=[(D_MODEL, F32), (D_MODEL, F32)])
    g, sq = _rowwise("loss_head", lambda a, b, t, g1: (lambda e: (e * (1.0 / D_MODEL), jnp.sum(e * e, axis=0, keepdims=True)))(a + _rms(b, g1) - t),
                     [saved[1]["x2"], saved[1]["y3"], tgt], [nw[1, 5]], out_rows=[(D_MODEL, F32)], out_sums=[D_MODEL])
    loss = lax.psum(0.5 * jnp.sum(sq) / D_MODEL, ("x", "y", "c"))

    gbig = {}
    gnw = [[None] * 6 for _ in range(2)]
    gmemnw = [None, None]
    gsmall = {}
    for layer in (1, 0):
        sv = saved[layer]
        dy3, gnw[layer][5] = _rowwise(f"b_norm5_{layer}", lambda gg, yy, g1: _rms_bwd(yy, g1, gg), [g, sv["y3"]], [nw[layer, 5]],
                                      out_rows=[(D_MODEL, BF16)], out_sums=[D_MODEL])
        dact = _mm(f"b_ffn_out_dx{layer}", dy3, wf['ffn_w_out'][layer], tb=True)
        gw_out = _mm(f"b_ffn_out_dw{layer}", sv["act"], dy3, ta=True)

        def swiglu_bwd(t, da):
            a, b = t[:, :D_FF], t[:, D_FF:]
            sg = jax.nn.sigmoid(a)
            return jnp.concatenate([da * b * (sg * (1.0 + a * (1.0 - sg))), da * (a * sg)], axis=1)

        (dzf,) = _rowwise(f"b_swiglu{layer}", swiglu_bwd, [sv["zf"], dact], out_rows=[(2 * D_FF, BF16)], tile=128)
        dh4 = _mm(f"b_ffn_in_dx{layer}", dzf, wf['ffn_w_in'][layer], tb=True)
        gw_in = _mm(f"b_ffn_in_dw{layer}", sv["h4"], dzf, ta=True)
        gbig.setdefault('ffn_w_out', {})[layer] = gw_out
        gbig.setdefault('ffn_w_in', {})[layer] = gw_in

        def resnorm_bwd(gg, dh, xx, yy, g_in, g_res):
            dx, dw_in = _rms_bwd(xx, g_in, dh)
            tot = gg + dx
            dy, dw_res = _rms_bwd(yy, g_res, tot)
            return tot, dy, dw_in, dw_res

        g, dy2, gnw[layer][4], gnw[layer][3] = _rowwise(
            f"b_resnorm_b{layer}", resnorm_bwd, [g, dh4, sv["x2"], sv["y2"]], [nw[layer, 4], nw[layer, 3]],
            out_rows=[(D_MODEL, F32), (D_MODEL, BF16)], out_sums=[D_MODEL, D_MODEL])
        do = _mm(f"b_xo_dx{layer}", dy2, wf['xattn_wo'][layer], tb=True)
        gbig.setdefault('xattn_wo', {})[layer] = _mm(f"b_xo_dw{layer}", sv["o"], dy2, ta=True)
        dq, dk, dv = _xattn_bwd(f"xattn_b{layer}", sv["q"].reshape(bsz, length, D_MODEL),
                                sv["kv"].reshape(bsz, MEM_LEN, 2 * D_MODEL), do.reshape(bsz, length, D_MODEL))
        dq = dq.reshape(ntok, D_MODEL)
        dkv = jnp.concatenate([dk, dv], axis=-1).reshape(bsz * MEM_LEN, 2 * D_MODEL)
        dh2 = _mm(f"b_xq_dx{layer}", dq, wf['xattn_wq'][layer], tb=True)
        gbig.setdefault('xattn_wq', {})[layer] = _mm(f"b_xq_dw{layer}", sv["h2"], dq, ta=True)
        dmem_n = _mm(f"b_xkv_dx{layer}", dkv, wf['xattn_wkv'][layer], tb=True)
        gbig.setdefault('xattn_wkv', {})[layer] = _mm(f"b_xkv_dw{layer}", sv["mem_n"], dkv, ta=True)
        (gmemnw[layer],) = _rowwise(f"b_mem_norm{layer}", lambda dd, mm_, g1: _rms_bwd(mm_, g1, dd)[1], [dmem_n, mem2],
                                    [w['mem_norm_w'][layer][None]], out_sums=[D_MODEL])

        g, dy1, gnw[layer][2], gnw[layer][1] = _rowwise(
            f"b_resnorm_a{layer}", resnorm_bwd, [g, dh2, sv["x1"], sv["y1"]], [nw[layer, 2], nw[layer, 1]],
            out_rows=[(D_MODEL, F32), (D_MODEL, F32 if layer == 1 else BF16)], out_sums=[D_MODEL, D_MODEL])
        if layer == 1:
            def gate_bwd(t, dd):
                a, b = t[:, :D_MODEL], t[:, D_MODEL:]
                sg = jax.nn.sigmoid(b)
                return jnp.concatenate([dd * sg, dd * a * sg * (1.0 - sg)], axis=1)

            (dzg,) = _rowwise("b_s5_gate", gate_bwd, [sv["zg"], dy1], out_rows=[(2 * D_MODEL, BF16)])
            dgl = _mm("b_s5_glu_dx", dzg, wf['s5_w_glu'][0], tb=True)
            gbig['s5_w_glu'] = {0: _mm("b_s5_glu_dw", sv["gl"], dzg, ta=True)}
            dh0, dp, gsmall['s5_d'] = _s5_bwd(dgl, sv["s5"], s5_d_full, bsz)
            for n, t in zip(('s5_lambda_re', 's5_lambda_im', 's5_log_dt', 's5_b_re', 's5_b_im', 's5_c_re', 's5_c_im'), dp):
                gsmall[n] = t[None]
            g, gnw[1][0] = _rowwise("b_norm_in1", lambda gg, dh, xx, g1: (lambda r: (gg + r[0], r[1]))(_rms_bwd(xx, g1, dh)),
                                    [g, dh0, sv["x"]], [nw[1, 0]], out_rows=[(D_MODEL, F32)], out_sums=[D_MODEL])
        else:
            dcore = _mm("b_ab_out_dx", dy1, wf['ab_w_out'][0], tb=True)
            gbig['ab_w_out'] = {0: _mm("b_ab_out_dw", sv["core"], dy1, ta=True)}
            dcore = dcore.reshape(bsz, length, D_MODEL)
            core3 = sv["core"].reshape(bsz, length, D_MODEL)
            dqa, dfa, dia, dga, dlb0, gsmall['hgrn_out_norm_w'] = _hgrn_bwd("hgrn_b", sv["z"], lb_table[0:1], w['hgrn_out_norm_w'], dcore)
            dqb, dkb, dvb = _dilated_bwd("dilated_b", sv["qr"], sv["kr"], sv["vb"], core3, sv["lse"], dcore, branch_cnt,
                                         rope_cos, rope_sin, off=A_WIDTH // B_HD)
            (gsmall['hgrn_lb_logits'],) = lb_vjp(jnp.zeros_like(lb_table).at[0].set(dlb0[0]))
            dz = jnp.concatenate([dqa, dfa, dia, dga, dqb, dkb, dvb], axis=-1).reshape(ntok, -1)
            dh0 = _mm("b_ab_in_dx", dz, wf['ab_w_in'][0], tb=True)
            gbig['ab_w_in'] = {0: _mm("b_ab_in_dw", sv["h0"], dz, ta=True)}
            grad_x, gnw[0][0] = _rowwise("b_norm_in0", lambda gg, dh, xx, g1: (lambda r: (gg + r[0], r[1]))(_rms_bwd(xx, g1, dh)),
                                         [g, dh0, sv["x"]], [nw[0, 0]], out_rows=[(D_MODEL, F32)], out_sums=[D_MODEL])
    gsmall['norm_w'] = jnp.stack([jnp.concatenate(gnw[l], axis=0) for l in range(2)])
    gsmall['mem_norm_w'] = jnp.concatenate(gmemnw, axis=0)

    packed = _pack_rows([gsmall[n] for n in SMALL])
    theirs = _sibling_swap("small_swap", packed)
    chip_sum = _add2("small_pair_sum", packed, theirs)
    (all_chips,) = _gather_chips("small_gather", [chip_sum[None]], [0])
    small_sum = _sum_slots("small_sum", all_chips.reshape(N_CHIPS, packed.shape[0], PACK_COLS))
    full_shapes = [(2, 6, D_MODEL) if n == 'norm_w' else (1, D_MODEL) if n == 's5_d' else w[n].shape for n in SMALL]
    gs = dict(zip(SMALL, _unpack_rows(small_sum, full_shapes)))
    for n in SHARDED_SMALL:
        gs[n] = lax.dynamic_slice_in_dim(gs[n], chip * 256, 256, axis=gs[n].ndim - 1)

    pos = _pos_vec()
    parts = [jnp.stack([gbig[n][l] for l in sorted(gbig[n])]) for n in BIG_NAMES]
    theirs = _pair_send("grad_pair_send", parts, big_axes)
    pair = [_pair_add("grad_pair_sum_" + n, a, b, ax, pos) for (n, ax), a, b in zip(BIG, parts, theirs)]
    slots = _chip_exchange("grad_chip_exchange", pair, big_axes)
    shards = [_chip_sum("grad_chip_sum_" + n, a, b, s, ax, pos) for (n, ax), a, b, s in zip(BIG, parts, theirs, slots)]
    gfull = dict(zip(BIG_NAMES, _pair_join("grad_pair_join", shards)))

    grads, deltas, new_m, new_v = {}, {}, {}, {}
    for n in BIG_NAMES:
        grads[n] = gfull[n]
        deltas[n], new_m[n], new_v[n] = _adam("adam_" + n, w[n], gfull[n], mom[n], var[n])
    pk = [_pack_rows([t[n] for n in SMALL]) for t in (w, gs, mom, var)]
    small_out = _adam("adam_small", *pk)
    shard_shapes = [w[n].shape for n in SMALL]
    for dst, packed_out in zip((deltas, new_m, new_v), small_out):
        dst.update(zip(SMALL, _unpack_rows(packed_out, shard_shapes)))
    grads.update(gs)
    return (loss, grad_x.reshape(x.shape), *[grads[n] for n in WEIGHTS], *[deltas[n] for n in WEIGHTS],
            *[new_m[n] for n in WEIGHTS], *[new_v[n] for n in WEIGHTS])
```

```python
import functools
import math

import numpy as np
import jax
import jax.numpy as jnp
from jax import lax
from jax.experimental import pallas as pl
from jax.experimental.pallas import tpu as pltpu

F32 = jnp.float32
BF16 = jnp.bfloat16
MXU_DTYPE = jnp.bfloat16

D_MODEL = 1024
NORM_EPS = 1e-6
A_HEADS, A_DK, A_CHUNK = 4, 128, 32
A_WIDTH = A_HEADS * A_DK
B_HEADS, B_HD = 4, 128
B_WIDTH = B_HEADS * B_HD
B_DILATIONS = ((128, 1), (512, 4), (2048, 16))
ROPE_THETA = 10000.0
C_GROUP, C_GROUPS, C_STATE, C_CHUNK = 16, 64, 64, 128
C_MIN_NEG_RE = -1e-4
MEM_LEN = 256
X_HEADS = 4
X_HD = D_MODEL // X_HEADS
D_FF = 2816
ADAM_LR, ADAM_B1, ADAM_B2, ADAM_EPS, ADAM_WD, ADAM_STEP = 0.001, 0.9, 0.999, 1e-08, 0.01, 10

N_CHIPS = 4
MESH = pl.DeviceIdType.MESH
ANY = pl.BlockSpec(memory_space=pl.ANY)
_RELS = ((1, 0), (0, 1), (1, 1))

WEIGHTS = ['norm_w', 'mem_norm_w', 'ab_w_in', 'ab_w_out', 'hgrn_lb_logits', 'hgrn_out_norm_w', 's5_lambda_re',
           's5_lambda_im', 's5_log_dt', 's5_b_re', 's5_b_im', 's5_c_re', 's5_c_im', 's5_d', 's5_w_glu', 'xattn_wq',
           'xattn_wkv', 'xattn_wo', 'ffn_w_in', 'ffn_w_out']
BIG = (('ab_w_in', 1), ('ab_w_out', 0), ('s5_w_glu', 1), ('xattn_wq', 0), ('xattn_wkv', 1), ('xattn_wo', 0),
       ('ffn_w_in', 1), ('ffn_w_out', 0))
BIG_NAMES = tuple(n for n, _ in BIG)
SMALL = tuple(n for n in WEIGHTS if n not in BIG_NAMES)
SHARDED_SMALL = ('norm_w', 's5_d')
PACK_COLS = 1024


def _pos():
    return lax.axis_index("x"), lax.axis_index("y"), lax.axis_index("c")


def _flip(v, d):
    return 1 - v if d else v


def _divisor(n, want):
    for t in (want, 1024, 512, 256, 128, 64, 32, 16, 8):
        if t <= want and n % t == 0:
            return t
    return n


def _rowwise(name, fn, rows, bcasts=(), out_rows=(), out_sums=(), tile=256):
    n = rows[0].shape[0]
    t = _divisor(n, tile)
    nr, nb, no, ns = len(rows), len(bcasts), len(out_rows), len(out_sums)

    def body(*refs):
        vals = [r[...] for r in refs[:nr + nb]]
        res = fn(*vals)
        if not isinstance(res, (tuple, list)):
            res = (res,)
        outs = refs[nr + nb:]
        for k in range(no):
            outs[k][...] = res[k].astype(outs[k].dtype)
        if ns:
            first = pl.program_id(0) == 0
            for k in range(ns):
                o, val = outs[no + k], res[no + k]

                @pl.when(first)
                def _():
                    o[...] = val

                @pl.when(jnp.logical_not(first))
                def _():
                    o[...] += val

    in_specs = [pl.BlockSpec((t, r.shape[1]), lambda i: (i, 0)) for r in rows]
    in_specs += [pl.BlockSpec(b.shape, lambda i: (0, 0)) for b in bcasts]
    out_specs = [pl.BlockSpec((t, c), lambda i: (i, 0)) for c, _ in out_rows]
    out_specs += [pl.BlockSpec((1, c), lambda i: (0, 0)) for c in out_sums]
    out_shape = [jax.ShapeDtypeStruct((n, c), dt) for c, dt in out_rows]
    out_shape += [jax.ShapeDtypeStruct((1, c), F32) for c in out_sums]
    res = pl.pallas_call(
        body, name=name, grid=(n // t,), in_specs=in_specs, out_specs=out_specs, out_shape=out_shape,
        compiler_params=pltpu.CompilerParams(dimension_semantics=("arbitrary",)),
    )(*rows, *bcasts)
    return res


def _rms(x, w):
    r = lax.rsqrt(jnp.mean(x * x, axis=-1, keepdims=True) + NORM_EPS)
    return x * r * w


def _rms_bwd(x, w, dy):
    r = lax.rsqrt(jnp.mean(x * x, axis=-1, keepdims=True) + NORM_EPS)
    xh = x * r
    dxh = dy * w
    dx = r * (dxh - xh * jnp.mean(dxh * xh, axis=-1, keepdims=True))
    return dx, jnp.sum(dy * xh, axis=0, keepdims=True)


def _silu(z):
    return z * jax.nn.sigmoid(z)


MM_VMEM_BUDGET = 40 * 1024 * 1024


def _mm_tiles(m, n, k, ta, abytes, bbytes, obytes):
    tn = _divisor(n, 512)
    tk = _divisor(k, 1024) if ta else (k if k <= 2816 else next(t for t in (2816, 2048, 1792, 1024, 512) if k % t == 0))
    for tm in (2816, 2048, 1024, 512, 256, 128):
        if m % tm:
            continue
        need = 2 * (tm * tk * abytes + tk * tn * bbytes + tm * tn * obytes) + 2 * tm * tn * 4
        if need <= MM_VMEM_BUDGET:
            return tm, tn, tk
    return _divisor(m, 128), tn, tk


def _mm(name, a, b, ta=False, tb=False, out_dtype=F32):
    m, k = a.shape[::-1] if ta else a.shape
    k2, n = b.shape[::-1] if tb else b.shape
    assert k == k2, (name, a.shape, b.shape)
    tm, tn, tk = _mm_tiles(m, n, k, ta, a.dtype.itemsize, b.dtype.itemsize, jnp.dtype(out_dtype).itemsize)
    nk = k // tk
    dims = (((0 if ta else 1,), (1 if tb else 0,)), ((), ()))

    def prod(a_ref, b_ref):
        return lax.dot_general(a_ref[...].astype(MXU_DTYPE), b_ref[...].astype(MXU_DTYPE), dims,
                               preferred_element_type=F32)

    def body_one(a_ref, b_ref, o_ref):
        o_ref[...] = prod(a_ref, b_ref).astype(o_ref.dtype)

    def body_acc(a_ref, b_ref, o_ref, acc):
        kk = pl.program_id(2)

        @pl.when(kk == 0)
        def _():
            acc[...] = prod(a_ref, b_ref)

        @pl.when(kk > 0)
        def _():
            acc[...] += prod(a_ref, b_ref)

        @pl.when(kk == nk - 1)
        def _():
            o_ref[...] = acc[...].astype(o_ref.dtype)

    a_spec = pl.BlockSpec((tk, tm), lambda i, j, kk: (kk, i)) if ta else pl.BlockSpec((tm, tk), lambda i, j, kk: (i, kk))
    b_spec = pl.BlockSpec((tn, tk), lambda i, j, kk: (j, kk)) if tb else pl.BlockSpec((tk, tn), lambda i, j, kk: (kk, j))
    return pl.pallas_call(
        body_one if nk == 1 else body_acc, name=name, grid=(m // tm, n // tn, nk),
        in_specs=[a_spec, b_spec], out_specs=pl.BlockSpec((tm, tn), lambda i, j, kk: (i, j)),
        out_shape=jax.ShapeDtypeStruct((m, n), out_dtype),
        scratch_shapes=[] if nk == 1 else [pltpu.VMEM((tm, tn), F32)],
        compiler_params=pltpu.CompilerParams(dimension_semantics=("parallel", "parallel", "arbitrary")),
    )(a, b)


def _mm_swiglu(name, h, w, tm=2048, tn=256):
    n, k = h.shape
    f = w.shape[1] // 2
    tm, nj = _divisor(n, tm), f // tn

    def body(h_ref, wa_ref, wb_ref, act_ref, za_ref, zb_ref):
        hv = h_ref[...].astype(MXU_DTYPE)
        za = jnp.dot(hv, wa_ref[...].astype(MXU_DTYPE), preferred_element_type=F32)
        zb = jnp.dot(hv, wb_ref[...].astype(MXU_DTYPE), preferred_element_type=F32)
        act_ref[...] = (_silu(za) * zb).astype(act_ref.dtype)
        za_ref[...] = za.astype(za_ref.dtype)
        zb_ref[...] = zb.astype(zb_ref.dtype)

    out = pl.BlockSpec((tm, tn), lambda i, j: (i, j))
    return pl.pallas_call(
        body, name=name, grid=(n // tm, nj),
        in_specs=[pl.BlockSpec((tm, k), lambda i, j: (i, 0)), pl.BlockSpec((k, tn), lambda i, j: (0, j)),
                  pl.BlockSpec((k, tn), lambda i, j: (0, j + nj))],
        out_specs=[out] * 3, out_shape=[jax.ShapeDtypeStruct((n, f), BF16)] * 3,
        compiler_params=pltpu.CompilerParams(dimension_semantics=("parallel", "parallel")),
    )(h, w, w)


def _xattn_fwd(name, q, kv, tq=512):
    bsz, length, _ = q.shape
    tq = _divisor(length, tq)
    scale = X_HD ** -0.5

    def body(q_ref, k_ref, v_ref, o_ref):
        qv, kk, vv = q_ref[...].astype(MXU_DTYPE), k_ref[...].astype(MXU_DTYPE), v_ref[...].astype(MXU_DTYPE)
        s = lax.dot_general(qv, kk, (((1,), (1,)), ((), ())), preferred_element_type=F32) * scale
        p = jnp.exp(s - jnp.max(s, axis=-1, keepdims=True))
        p = p / jnp.sum(p, axis=-1, keepdims=True)
        o_ref[...] = jnp.dot(p.astype(MXU_DTYPE), vv, preferred_element_type=F32).astype(o_ref.dtype)

    return pl.pallas_call(
        body, name=name, grid=(bsz, X_HEADS, length // tq),
        in_specs=[pl.BlockSpec((None, tq, X_HD), lambda b, h, i: (b, i, h)),
                  pl.BlockSpec((None, MEM_LEN, X_HD), lambda b, h, i: (b, 0, h)),
                  pl.BlockSpec((None, MEM_LEN, X_HD), lambda b, h, i: (b, 0, X_HEADS + h))],
        out_specs=pl.BlockSpec((None, tq, X_HD), lambda b, h, i: (b, i, h)),
        out_shape=jax.ShapeDtypeStruct(q.shape, BF16),
        compiler_params=pltpu.CompilerParams(dimension_semantics=("parallel", "parallel", "arbitrary")),
    )(q, kv, kv)


def _xattn_bwd(name, q, kv, do, tq=512):
    bsz, length, _ = q.shape
    tq = _divisor(length, tq)
    scale = X_HD ** -0.5

    def body(q_ref, k_ref, v_ref, do_ref, dq_ref, dk_ref, dv_ref):
        qv, kk, vv = q_ref[...].astype(MXU_DTYPE), k_ref[...].astype(MXU_DTYPE), v_ref[...].astype(MXU_DTYPE)
        dov = do_ref[...].astype(MXU_DTYPE)
        s = lax.dot_general(qv, kk, (((1,), (1,)), ((), ())), preferred_element_type=F32) * scale
        p = jnp.exp(s - jnp.max(s, axis=-1, keepdims=True))
        p = p / jnp.sum(p, axis=-1, keepdims=True)
        dp = lax.dot_general(dov, vv, (((1,), (1,)), ((), ())), preferred_element_type=F32)
        ds = p * (dp - jnp.sum(dp * p, axis=-1, keepdims=True)) * scale
        dsb = ds.astype(MXU_DTYPE)
        dq_ref[...] = jnp.dot(dsb, kk, preferred_element_type=F32).astype(dq_ref.dtype)
        dk = lax.dot_general(dsb, qv, (((0,), (0,)), ((), ())), preferred_element_type=F32)
        dv = lax.dot_general(p.astype(MXU_DTYPE), dov, (((0,), (0,)), ((), ())), preferred_element_type=F32)
        first = pl.program_id(2) == 0

        @pl.when(first)
        def _():
            dk_ref[...] = dk
            dv_ref[...] = dv

        @pl.when(jnp.logical_not(first))
        def _():
            dk_ref[...] += dk
            dv_ref[...] += dv

    qspec = pl.BlockSpec((None, tq, X_HD), lambda b, h, i: (b, i, h))
    kspec = pl.BlockSpec((None, MEM_LEN, X_HD), lambda b, h, i: (b, 0, h))
    return pl.pallas_call(
        body, name=name, grid=(bsz, X_HEADS, length // tq),
        in_specs=[qspec, kspec, pl.BlockSpec((None, MEM_LEN, X_HD), lambda b, h, i: (b, 0, X_HEADS + h)), qspec],
        out_specs=[qspec, kspec, kspec],
        out_shape=[jax.ShapeDtypeStruct(q.shape, BF16), jax.ShapeDtypeStruct((bsz, MEM_LEN, D_MODEL), F32),
                   jax.ShapeDtypeStruct((bsz, MEM_LEN, D_MODEL), F32)],
        compiler_params=pltpu.CompilerParams(dimension_semantics=("parallel", "parallel", "arbitrary")),
    )(q, kv, kv, do)


def _dma_sems(*counts):
    return [pltpu.SemaphoreType.DMA((max(c, 1),)) for c in counts]


def _gather_chips(name, blocks, axes):
    n = len(blocks)
    shapes = [b.shape for b in blocks]

    def body(*refs):
        ins, outs = refs[:n], refs[n:2 * n]
        lsem, lrsem, ssem, rsem, fssem, frsem = refs[2 * n:]
        x, y, c = _pos()
        me = 2 * x + y

        def region(a, chip, h):
            _, r, cc = shapes[a]
            hr = r // 2
            if axes[a] == 0:
                return outs[a].at[:, pl.ds(chip * r + h * hr, hr), :]
            return outs[a].at[:, pl.ds(h * hr, hr), pl.ds(chip * cc, cc)]

        def whole(a, chip):
            _, r, cc = shapes[a]
            if axes[a] == 0:
                return outs[a].at[:, pl.ds(chip * r, r), :]
            return outs[a].at[:, :, pl.ds(chip * cc, cc)]

        sends = []
        for a in range(n):
            cp = pltpu.make_async_remote_copy(src_ref=ins[a], dst_ref=whole(a, me), send_sem=lsem.at[a], recv_sem=lrsem.at[a],
                                              device_id=(x, y, 1 - c), device_id_type=MESH)
            cp.start()
            sends.append(cp)
        for a in range(n):
            hr = shapes[a][1] // 2
            for k, (dx, dy) in enumerate(_RELS):
                cp = pltpu.make_async_remote_copy(
                    src_ref=ins[a].at[:, pl.ds(c * hr, hr), :], dst_ref=region(a, me, c),
                    send_sem=ssem.at[3 * a + k], recv_sem=rsem.at[3 * a + k],
                    device_id=(_flip(x, dx), _flip(y, dy), c), device_id_type=MESH)
                cp.start()
                sends.append(cp)
        for a in range(n):
            for k, (dx, dy) in enumerate(_RELS):
                px, py = _flip(x, dx), _flip(y, dy)
                got = region(a, 2 * px + py, c)
                pltpu.make_async_remote_copy(
                    src_ref=got, dst_ref=got, send_sem=ssem.at[3 * a + k], recv_sem=rsem.at[3 * a + k],
                    device_id=(px, py, c), device_id_type=MESH).wait_recv()
                cp = pltpu.make_async_remote_copy(
                    src_ref=got, dst_ref=got, send_sem=fssem.at[3 * a + k], recv_sem=frsem.at[3 * a + k],
                    device_id=(x, y, 1 - c), device_id_type=MESH)
                cp.start()
                sends.append(cp)
        for a in range(n):
            for k, (dx, dy) in enumerate(_RELS):
                got = region(a, 2 * _flip(x, dx) + _flip(y, dy), 1 - c)
                pltpu.make_async_remote_copy(
                    src_ref=got, dst_ref=got, send_sem=fssem.at[3 * a + k], recv_sem=frsem.at[3 * a + k],
                    device_id=(x, y, 1 - c), device_id_type=MESH).wait_recv()
        for a in range(n):
            pltpu.make_async_remote_copy(src_ref=ins[a], dst_ref=whole(a, me), send_sem=lsem.at[a], recv_sem=lrsem.at[a],
                                         device_id=(x, y, 1 - c), device_id_type=MESH).wait_recv()
        for cp in sends:
            cp.wait_send()

    out_shape = [jax.ShapeDtypeStruct((l, 4 * r, c) if ax == 0 else (l, r, 4 * c), b.dtype)
                 for (l, r, c), ax, b in zip(shapes, axes, blocks)]
    return pl.pallas_call(
        body, name=name, in_specs=[ANY] * n, out_specs=[ANY] * n, out_shape=out_shape,
        scratch_shapes=_dma_sems(n, n, 3 * n, 3 * n, 3 * n, 3 * n),
    )(*blocks)


def _pos_vec():
    x, y, c = _pos()
    return jnp.stack([c, 2 * x + y]).astype(jnp.int32)


def _pair_send(name, parts, axes):
    n = len(parts)
    shapes = [p.shape for p in parts]
    ncopy = sum(4 if ax == 0 else 1 for ax in axes)

    def body(*refs):
        ins, theirs = refs[:n], refs[n:2 * n]
        ssem, rsem = refs[2 * n:]
        x, y, c = _pos()
        pending, j = [], 0
        for a in range(n):
            _, rf, _ = shapes[a]
            if axes[a] == 0:
                hr = rf // 8
                pieces = [(ins[a].at[:, pl.ds((2 * s + 1 - c) * hr, hr), :], theirs[a].at[:, s]) for s in range(N_CHIPS)]
            else:
                hr = rf // 2
                pieces = [(ins[a].at[:, pl.ds((1 - c) * hr, hr), :], theirs[a])]
            for give, give_dst in pieces:
                rc = pltpu.make_async_remote_copy(src_ref=give, dst_ref=give_dst, send_sem=ssem.at[j],
                                                  recv_sem=rsem.at[j], device_id=(x, y, 1 - c), device_id_type=MESH)
                rc.start()
                pending.append(rc)
                j += 1
        for cp in pending:
            cp.wait()

    def half_shape(s, ax):
        return (s[0], N_CHIPS, s[1] // 8, s[2]) if ax == 0 else (s[0], s[1] // 2, s[2])

    out_shape = [jax.ShapeDtypeStruct(half_shape(s, ax), p.dtype) for s, ax, p in zip(shapes, axes, parts)]
    return pl.pallas_call(
        body, name=name, in_specs=[ANY] * n, out_specs=[ANY] * n, out_shape=out_shape,
        scratch_shapes=_dma_sems(ncopy, ncopy),
    )(*parts)


def _chip_exchange(name, halves, axes):
    n = len(halves)
    shapes = [h.shape for h in halves]

    def body(*refs):
        ins, outs = refs[:n], refs[n:2 * n]
        ssem, rsem = refs[2 * n:]
        x, y, c = _pos()

        def part(a, chip):
            if axes[a] == 0:
                return ins[a].at[:, chip]
            cc = shapes[a][2] // N_CHIPS
            return ins[a].at[:, :, pl.ds(chip * cc, cc)]

        sends = []
        for a in range(n):
            for k, (dx, dy) in enumerate(_RELS):
                px, py = _flip(x, dx), _flip(y, dy)
                rc = pltpu.make_async_remote_copy(
                    src_ref=part(a, 2 * px + py), dst_ref=outs[a].at[:, k], send_sem=ssem.at[3 * a + k],
                    recv_sem=rsem.at[3 * a + k], device_id=(px, py, c), device_id_type=MESH)
                rc.start()
                sends.append(rc)
        for cp in sends:
            cp.wait()

    def slot_shape(s, ax):
        return (s[0], 3, s[2], s[3]) if ax == 0 else (s[0], 3, s[1], s[2] // N_CHIPS)

    out_shape = [jax.ShapeDtypeStruct(slot_shape(s, ax), h.dtype) for s, ax, h in zip(shapes, axes, halves)]
    return pl.pallas_call(
        body, name=name, in_specs=[ANY] * n, out_specs=[ANY] * n, out_shape=out_shape,
        scratch_shapes=_dma_sems(3 * n, 3 * n),
    )(*halves)


def _pair_join(name, shards):
    n = len(shards)

    def body(*refs):
        outs = refs[n:2 * n]
        ssem, rsem = refs[2 * n:]
        x, y, c = _pos()
        pending = []
        for a in range(n):
            hr = shards[a].shape[1] // 2
            mine = outs[a].at[:, pl.ds(c * hr, hr), :]
            rc = pltpu.make_async_remote_copy(src_ref=mine, dst_ref=mine, send_sem=ssem.at[a], recv_sem=rsem.at[a],
                                              device_id=(x, y, 1 - c), device_id_type=MESH)
            rc.start()
            pending.append(rc)
        for a in range(n):
            hr = shards[a].shape[1] // 2
            got = outs[a].at[:, pl.ds((1 - c) * hr, hr), :]
            pltpu.make_async_remote_copy(src_ref=got, dst_ref=got, send_sem=ssem.at[a], recv_sem=rsem.at[a],
                                         device_id=(x, y, 1 - c), device_id_type=MESH).wait_recv()
        for cp in pending:
            cp.wait_send()

    return pl.pallas_call(
        body, name=name, in_specs=[ANY] * n, out_specs=[ANY] * n,
        out_shape=[jax.ShapeDtypeStruct(s.shape, s.dtype) for s in shards],
        input_output_aliases={a: a for a in range(n)}, scratch_shapes=_dma_sems(n, n),
    )(*shards)


def _pair_add(name, part, theirs, axis, pos):
    layers, rf, cf = part.shape

    def body(pos_ref, a_ref, b_ref, o_ref):
        o_ref[...] = (a_ref[...] + b_ref[...]).astype(o_ref.dtype)

    if axis == 0:
        hr = rf // 8
        grid = (layers, N_CHIPS)
        in_specs = [pl.BlockSpec((None, hr, cf), lambda l, s, p: (l, 2 * s + p[0], 0)),
                    pl.BlockSpec((None, None, hr, cf), lambda l, s, p: (l, s, 0, 0))]
        out_spec = pl.BlockSpec((None, None, hr, cf), lambda l, s, p: (l, s, 0, 0))
    else:
        hr, t = rf // 2, 128
        grid = (layers, hr // t)
        in_specs = [pl.BlockSpec((None, t, cf), lambda l, i, p: (l, p[0] * (hr // t) + i, 0)),
                    pl.BlockSpec((None, t, cf), lambda l, i, p: (l, i, 0))]
        out_spec = pl.BlockSpec((None, t, cf), lambda l, i, p: (l, i, 0))
    return pl.pallas_call(
        body, name=name, out_shape=jax.ShapeDtypeStruct(theirs.shape, BF16),
        grid_spec=pltpu.PrefetchScalarGridSpec(num_scalar_prefetch=1, grid=grid, in_specs=in_specs, out_specs=out_spec),
        compiler_params=pltpu.CompilerParams(dimension_semantics=("arbitrary", "arbitrary")),
    )(pos, part, theirs)


def _chip_sum(name, part, theirs, slots, axis, pos):
    layers, _, hr, c = slots.shape

    def body(pos_ref, mine, sib, s0, s1, s2, o_ref):
        o_ref[...] = (((mine[...] + sib[...]) + s0[...].astype(F32)) + s1[...].astype(F32)) + s2[...].astype(F32)

    t = hr if axis == 0 else 128
    if axis == 0:
        own_specs = [pl.BlockSpec((None, t, c), lambda l, i, p: (l, 2 * p[1] + p[0], 0)),
                     pl.BlockSpec((None, None, t, c), lambda l, i, p: (l, p[1], 0, 0))]
    else:
        own_specs = [pl.BlockSpec((None, t, c), lambda l, i, p: (l, p[0] * (hr // t) + i, p[1])),
                     pl.BlockSpec((None, t, c), lambda l, i, p: (l, i, p[1]))]
    slot_specs = [pl.BlockSpec((None, None, t, c), functools.partial(lambda k, l, i, p: (l, k, i, 0), k)) for k in range(3)]
    return pl.pallas_call(
        body, name=name, out_shape=jax.ShapeDtypeStruct((layers, 2 * hr, c), F32),
        grid_spec=pltpu.PrefetchScalarGridSpec(
            num_scalar_prefetch=1, grid=(layers, hr // t), in_specs=own_specs + slot_specs,
            out_specs=pl.BlockSpec((None, t, c), lambda l, i, p: (l, p[0] * (hr // t) + i, 0))),
        compiler_params=pltpu.CompilerParams(dimension_semantics=("arbitrary", "arbitrary")),
    )(pos, part, theirs, slots, slots, slots)


def _sibling_swap(name, v):
    def body(v_ref, o_ref, ssem, rsem):
        x, y, c = _pos()
        cp = pltpu.make_async_remote_copy(src_ref=v_ref, dst_ref=o_ref, send_sem=ssem.at[0], recv_sem=rsem.at[0],
                                          device_id=(x, y, 1 - c), device_id_type=MESH)
        cp.start()
        cp.wait()

    return pl.pallas_call(body, name=name, in_specs=[ANY], out_specs=ANY, out_shape=jax.ShapeDtypeStruct(v.shape, v.dtype),
                          scratch_shapes=_dma_sems(1, 1))(v)


def _add2(name, a, b):
    shape = a.shape
    a2, b2 = a.reshape(-1, shape[-1]), b.reshape(-1, shape[-1])
    (o,) = _rowwise(name, lambda u, v: u + v, [a2, b2], out_rows=[(shape[-1], F32)], tile=512)
    return o.reshape(shape)


def _sum_slots(name, slots):
    _, hr, c = slots.shape
    t = _divisor(hr, 256)

    def body(s0, s1, s2, s3, o_ref):
        o_ref[...] = ((s0[...] + s1[...]) + s2[...]) + s3[...]

    return pl.pallas_call(
        body, name=name, grid=(hr // t,),
        in_specs=[pl.BlockSpec((None, t, c), functools.partial(lambda k, i: (k, i, 0), k)) for k in range(N_CHIPS)],
        out_specs=pl.BlockSpec((t, c), lambda i: (i, 0)), out_shape=jax.ShapeDtypeStruct((hr, c), F32),
        compiler_params=pltpu.CompilerParams(dimension_semantics=("arbitrary",)),
    )(slots, slots, slots, slots)


def _adam_tile(w, g, m, v):
    m = ADAM_B1 * m + (1.0 - ADAM_B1) * g
    v = ADAM_B2 * v + (1.0 - ADAM_B2) * (g * g)
    m_hat = m / (1.0 - ADAM_B1 ** ADAM_STEP)
    v_hat = v / (1.0 - ADAM_B2 ** ADAM_STEP)
    delta = -ADAM_LR * (m_hat / (jnp.sqrt(v_hat) + ADAM_EPS) + ADAM_WD * w)
    return delta, m, v


def _adam(name, w, g, m, v):
    shape = w.shape
    c = shape[-1]
    flat = [t.reshape(-1, c) for t in (w, g, m, v)]
    res = _rowwise(name, _adam_tile, flat, out_rows=[(c, F32)] * 3, tile=256)
    return [r.reshape(shape) for r in res]


ATT_T = 256
ATT_NEG = -1e30


def _branch_count(length):
    nblk = length // ATT_T
    d = (np.arange(nblk)[:, None, None] * ATT_T + np.arange(ATT_T)[None, :, None] - np.arange(ATT_T)[None, None, :])
    cnt = np.zeros(d.shape, np.float32)
    for window, dil in B_DILATIONS:
        cnt += ((d >= 0) & (d % dil == 0) & (d <= window)).astype(np.float32)
    return jnp.asarray(cnt)


def _rope_tables(length):
    half = B_HD // 2
    inv_freq = ROPE_THETA ** (-jnp.arange(half, dtype=F32) / half)
    ang = jnp.arange(length, dtype=F32)[:, None] * inv_freq[None, :]
    cos, sin = jnp.cos(ang), jnp.sin(ang)
    return jnp.concatenate([cos, cos], axis=1), jnp.concatenate([-sin, sin], axis=1)


def _swap_halves(t):
    return pltpu.roll(t, B_HD // 2, 1)


def _rope_qkv(name, z, cos, sin, t=256):
    bsz, length, _ = z.shape
    t = _divisor(length, t)

    def body(q_ref, k_ref, v_ref, c_ref, s_ref, qo, ko, vo):
        c, s = c_ref[...], s_ref[...]
        for src, dst in ((q_ref, qo), (k_ref, ko)):
            for h in range(B_HEADS):
                cols = slice(h * B_HD, (h + 1) * B_HD)
                xh = src[:, cols]
                dst[:, cols] = (xh * c + _swap_halves(xh) * s).astype(dst.dtype)
        vo[...] = v_ref[...].astype(vo.dtype)

    col0 = 4 * A_WIDTH // B_WIDTH
    specs = [pl.BlockSpec((None, t, B_WIDTH), functools.partial(lambda k, b, i: (b, i, col0 + k), k)) for k in range(3)]
    tab = pl.BlockSpec((t, B_HD), lambda b, i: (i, 0))
    out = pl.BlockSpec((None, t, B_WIDTH), lambda b, i: (b, i, 0))
    return pl.pallas_call(
        body, name=name, grid=(bsz, length // t), in_specs=specs + [tab, tab], out_specs=[out] * 3,
        out_shape=[jax.ShapeDtypeStruct((bsz, length, B_WIDTH), BF16)] * 3,
        compiler_params=pltpu.CompilerParams(dimension_semantics=("parallel", "parallel")),
    )(z, z, z, cos, sin)


def _dilated_fwd(name, q, k, v, cnt):
    bsz, length, _ = q.shape
    scale = B_HD ** -0.5
    nblk = length // ATT_T

    def body(cnt_ref, q_ref, k_ref, v_ref, o_ref, lse_ref):
        i = pl.program_id(2)
        qb = q_ref[...]

        def step(j, carry):
            m, l, acc = carry
            rows = pl.ds(pl.multiple_of(j * ATT_T, ATT_T), ATT_T)
            s = lax.dot_general(qb, k_ref[rows, :], (((1,), (1,)), ((), ())), preferred_element_type=F32) * scale
            c = cnt_ref[i - j]
            s = jnp.where(c > 0.0, s, ATT_NEG)
            m_new = jnp.maximum(m, jnp.max(s, axis=-1, keepdims=True))
            a = jnp.exp(m - m_new)
            p = c * jnp.exp(s - m_new)
            l = a * l + jnp.sum(p, axis=-1, keepdims=True)
            acc = a * acc + jnp.dot(p.astype(MXU_DTYPE), v_ref[rows, :], preferred_element_type=F32)
            return m_new, l, acc

        init = (jnp.full((ATT_T, 1), ATT_NEG, F32), jnp.zeros((ATT_T, 1), F32), jnp.zeros((ATT_T, B_HD), F32))
        m, l, acc = lax.fori_loop(0, i + 1, step, init)
        o_ref[...] = acc / l
        lse_ref[...] = jnp.broadcast_to(m + jnp.log(l), (ATT_T, B_HD))

    qspec = pl.BlockSpec((None, ATT_T, B_HD), lambda b, h, i: (b, i, h))
    kspec = pl.BlockSpec((None, length, B_HD), lambda b, h, i: (b, 0, h))
    return pl.pallas_call(
        body, name=name, grid=(bsz, B_HEADS, nblk),
        in_specs=[pl.BlockSpec(cnt.shape, lambda b, h, i: (0, 0, 0)), qspec, kspec, kspec],
        out_specs=[qspec, pl.BlockSpec((None, None, ATT_T, B_HD), lambda b, h, i: (b, h, i, 0))],
        out_shape=[jax.ShapeDtypeStruct((bsz, length, B_WIDTH), F32), jax.ShapeDtypeStruct((bsz, B_HEADS, length, B_HD), F32)],
        compiler_params=pltpu.CompilerParams(dimension_semantics=("parallel", "parallel", "arbitrary")),
    )(cnt, q, k, v)


def _dilated_bwd(name, q, k, v, o, lse, do, cnt, cos, sin, off=0):
    bsz, length, _ = q.shape
    scale = B_HD ** -0.5
    nblk = length // ATT_T

    def body(cnt_ref, q_ref, k_ref, v_ref, o_ref, lse_ref, do_ref, c_ref, s_ref, dq_ref, dk_ref, dv_ref, dq_acc, dk_acc, dv_acc):
        dk_acc[...] = jnp.zeros_like(dk_acc)
        dv_acc[...] = jnp.zeros_like(dv_acc)

        def outer(i, _):
            rq = pl.ds(pl.multiple_of(i * ATT_T, ATT_T), ATT_T)
            qi, doi = q_ref[rq, :], do_ref[rq, :]
            lsei = lse_ref[rq, :][:, 0:1]
            di = jnp.sum(doi * o_ref[rq, :], axis=-1, keepdims=True)
            dob = doi.astype(MXU_DTYPE)

            def inner(j, dq):
                rk = pl.ds(pl.multiple_of(j * ATT_T, ATT_T), ATT_T)
                kj, vj = k_ref[rk, :], v_ref[rk, :]
                s = lax.dot_general(qi, kj, (((1,), (1,)), ((), ())), preferred_element_type=F32) * scale
                c = cnt_ref[i - j]
                p = c * jnp.exp(jnp.where(c > 0.0, s, ATT_NEG) - lsei)
                dp = lax.dot_general(dob, vj, (((1,), (1,)), ((), ())), preferred_element_type=F32)
                ds = (p * (dp - di) * scale).astype(MXU_DTYPE)
                dk_acc[rk, :] += lax.dot_general(ds, qi, (((0,), (0,)), ((), ())), preferred_element_type=F32)
                dv_acc[rk, :] += lax.dot_general(p.astype(MXU_DTYPE), dob, (((0,), (0,)), ((), ())), preferred_element_type=F32)
                return dq + jnp.dot(ds, kj, preferred_element_type=F32)

            dq_acc[rq, :] = lax.fori_loop(0, i + 1, inner, jnp.zeros((ATT_T, B_HD), F32))
            return 0

        lax.fori_loop(0, nblk, outer, 0)
        c, s = c_ref[...], s_ref[...]
        for acc, dst in ((dq_acc, dq_ref), (dk_acc, dk_ref)):
            g = acc[...]
            dst[...] = (g * c + _swap_halves(g * s)).astype(dst.dtype)
        dv_ref[...] = dv_acc[...].astype(dv_ref.dtype)

    hspec = pl.BlockSpec((None, length, B_HD), lambda b, h: (b, 0, h))
    ospec = pl.BlockSpec((None, length, B_HD), lambda b, h: (b, 0, off + h))
    tab = pl.BlockSpec((length, B_HD), lambda b, h: (0, 0))
    return pl.pallas_call(
        body, name=name, grid=(bsz, B_HEADS),
        in_specs=[pl.BlockSpec(cnt.shape, lambda b, h: (0, 0, 0)), hspec, hspec, hspec, ospec,
                  pl.BlockSpec((None, None, length, B_HD), lambda b, h: (b, h, 0, 0)), ospec, tab, tab],
        out_specs=[hspec] * 3, out_shape=[jax.ShapeDtypeStruct((bsz, length, B_WIDTH), BF16)] * 3,
        scratch_shapes=[pltpu.VMEM((length, B_HD), F32)] * 3,
        compiler_params=pltpu.CompilerParams(dimension_semantics=("parallel", "parallel")),
    )(cnt, q, k, v, o, lse, do, cos, sin)


def _chunk_cumsum(t, reverse):
    n = t.shape[0]
    row = lax.broadcasted_iota(jnp.int32, t.shape, 0) & (A_CHUNK - 1)
    s = 1
    while s < A_CHUNK:
        if reverse:
            t = t + jnp.where(row < A_CHUNK - s, pltpu.roll(t, n - s, 0), 0.0)
        else:
            t = t + jnp.where(row >= s, pltpu.roll(t, s, 0), 0.0)
        s *= 2
    return t


def _hgrn_gates(fl, lb):
    sg = jax.nn.sigmoid(fl)
    f = lb + (1.0 - lb) * sg
    return sg, f


def _hgrn_chunks(nchunk, qd_s, ki_s, b_s, v_ref, o_s, st_s=None):
    tri = lax.broadcasted_iota(jnp.int32, (A_CHUNK, A_CHUNK), 0) >= lax.broadcasted_iota(jnp.int32, (A_CHUNK, A_CHUNK), 1)

    def step(n, st):
        rows = pl.ds(pl.multiple_of(n * A_CHUNK, A_CHUNK), A_CHUNK)
        if st_s is not None:
            st_s[n] = st
        qd, ki, vc = qd_s[rows, :].astype(MXU_DTYPE), ki_s[rows, :], v_ref[rows, :].astype(MXU_DTYPE)
        dec = jnp.exp(b_s[pl.ds(n * A_CHUNK + A_CHUNK - 1, 1), :])
        a = lax.dot_general(qd, ki.astype(MXU_DTYPE), (((1,), (1,)), ((), ())), preferred_element_type=F32)
        a = jnp.where(tri, a, 0.0).astype(MXU_DTYPE)
        o_s[rows, :] = (jnp.dot(a, vc, preferred_element_type=F32)
                        + lax.dot_general(qd, st.astype(MXU_DTYPE), (((1,), (1,)), ((), ())), preferred_element_type=F32))
        ke = (ki * dec).astype(MXU_DTYPE)
        return st * dec + lax.dot_general(vc, ke, (((0,), (0,)), ((), ())), preferred_element_type=F32)

    lax.fori_loop(0, nchunk, step, jnp.zeros((A_DK, A_DK), F32), unroll=4)


def _bmm(a, b, ca, cb):
    return lax.dot_general(a, b, (((ca,), (cb,)), ((0,), (0,))), preferred_element_type=F32)


def _hgrn_forward_chunks(nchunk, q, f, b, v_ref, st_s, dec_s):
    shape = (nchunk, A_CHUNK, A_DK)
    b3 = b.reshape(shape)
    dec = jnp.exp(b3[:, A_CHUNK - 1:A_CHUNK, :])
    dec_s[...] = dec
    qd = (q * jnp.exp(b)).reshape(shape)
    ki = ((1.0 - f) * jnp.exp(-b)).reshape(shape)
    qdb, kib, keb = qd.astype(MXU_DTYPE), ki.astype(MXU_DTYPE), (ki * dec).astype(MXU_DTYPE)
    v3 = v_ref[...].reshape(shape).astype(MXU_DTYPE)
    tri = (lax.broadcasted_iota(jnp.int32, (1, A_CHUNK, A_CHUNK), 1) >= lax.broadcasted_iota(jnp.int32, (1, A_CHUNK, A_CHUNK), 2))
    a = jnp.where(tri, _bmm(qdb, kib, 2, 2), 0.0).astype(MXU_DTYPE)
    st_s[...] = _bmm(v3, keb, 1, 1)

    def rec(n, st):
        u = st_s[n]
        st_s[n] = st
        return st * dec_s[n] + u

    lax.fori_loop(0, nchunk, rec, jnp.zeros((A_DK, A_DK), F32))
    o = _bmm(a, v3, 2, 1) + _bmm(qdb, st_s[...].astype(MXU_DTYPE), 2, 2)
    return dict(dec=dec, qd=qd, ki=ki, qdb=qdb, kib=kib, keb=keb, v3=v3, a=a, tri=tri), o


def _hgrn_fwd(name, z, lb, onw):
    bsz, length, _ = z.shape
    nchunk = length // A_CHUNK

    def body(q_ref, f_ref, v_ref, g_ref, lb_ref, w_ref, y_ref, st_s, dec_s):
        _, f = _hgrn_gates(f_ref[...], lb_ref[...])
        b = _chunk_cumsum(jnp.log(f), False)
        _, o = _hgrn_forward_chunks(nchunk, q_ref[...], f, b, v_ref, st_s, dec_s)
        o = o.reshape(length, A_DK)
        on = o * lax.rsqrt(jnp.mean(o * o, axis=-1, keepdims=True) + NORM_EPS)
        y_ref[...] = on * w_ref[...] * _silu(g_ref[...])

    cols = [pl.BlockSpec((None, length, A_DK), functools.partial(lambda k, b, h: (b, 0, k * A_HEADS + h), k)) for k in range(4)]
    vec = pl.BlockSpec((1, A_DK), lambda b, h: (0, h))
    return pl.pallas_call(
        body, name=name, grid=(bsz, A_HEADS), in_specs=cols + [vec, vec],
        out_specs=pl.BlockSpec((None, length, A_DK), lambda b, h: (b, 0, h)),
        out_shape=jax.ShapeDtypeStruct((bsz, length, A_WIDTH), F32),
        scratch_shapes=[pltpu.VMEM((nchunk, A_DK, A_DK), F32), pltpu.VMEM((nchunk, 1, A_DK), F32)],
        compiler_params=pltpu.CompilerParams(dimension_semantics=("parallel", "parallel")),
    )(z, z, z, z, lb, onw)


def _hgrn_bwd(name, z, lb, onw, dy):
    bsz, length, _ = z.shape
    nchunk = length // A_CHUNK
    shape = (nchunk, A_CHUNK, A_DK)

    def body(q_ref, f_ref, v_ref, g_ref, lb_ref, w_ref, dy_ref, dq_ref, df_ref, dv_ref, dg_ref, dlb_ref, dw_ref,
             st_s, dst_s, dec_s):
        lb = lb_ref[...]
        sg, f = _hgrn_gates(f_ref[...], lb)
        b = _chunk_cumsum(jnp.log(f), False)
        t, o = _hgrn_forward_chunks(nchunk, q_ref[...], f, b, v_ref, st_s, dec_s)
        o, g, w, dyv = o.reshape(length, A_DK), g_ref[...], w_ref[...], dy_ref[...]
        r = lax.rsqrt(jnp.mean(o * o, axis=-1, keepdims=True) + NORM_EPS)
        on = o * r
        sgg = jax.nn.sigmoid(g)
        gate = g * sgg
        dg_ref[...] = (dyv * on * w * (sgg * (1.0 + g * (1.0 - sgg)))).astype(dg_ref.dtype)
        dw = jnp.sum(dyv * on * gate, axis=0, keepdims=True)
        don = dyv * w * gate
        do = (r * (don - on * jnp.mean(don * on, axis=-1, keepdims=True))).reshape(shape).astype(MXU_DTYPE)
        da = jnp.where(t["tri"], _bmm(do, t["v3"], 2, 2), 0.0).astype(MXU_DTYPE)
        dst_s[...] = _bmm(do, t["qdb"], 1, 1)

        def rec(i, dst):
            n = nchunk - 1 - i
            u = dst_s[n]
            dst_s[n] = dst
            return dst * dec_s[n] + u

        lax.fori_loop(0, nchunk, rec, jnp.zeros((A_DK, A_DK), F32))
        dst, st = dst_s[...], st_s[...]
        dstb = dst.astype(MXU_DTYPE)
        dec, ki, qd = t["dec"], t["ki"], t["qd"]
        dv_ref[...] = (_bmm(t["a"], do, 1, 1) + _bmm(t["keb"], dstb, 2, 2)).reshape(length, A_DK).astype(dv_ref.dtype)
        dqd = _bmm(da, t["kib"], 2, 1) + _bmm(do, st.astype(MXU_DTYPE), 2, 1)
        dke = _bmm(t["v3"], dstb, 2, 1)
        dki = _bmm(da, t["qdb"], 1, 1) + dke * dec
        ddec = jnp.sum(dst * st, axis=1, keepdims=True) + jnp.sum(dke * ki, axis=1, keepdims=True)
        last = lax.broadcasted_iota(jnp.int32, (1, A_CHUNK, A_DK), 1) == A_CHUNK - 1
        db = (dqd * qd - dki * ki + jnp.where(last, ddec * dec, 0.0)).reshape(length, A_DK)
        dlf = _chunk_cumsum(db, True)
        dq_ref[...] = (dqd.reshape(length, A_DK) * jnp.exp(b)).astype(dq_ref.dtype)
        dfv = dlf / f - dki.reshape(length, A_DK) * jnp.exp(-b)
        df_ref[...] = (dfv * (1.0 - lb) * sg * (1.0 - sg)).astype(df_ref.dtype)
        dlb = jnp.sum(dfv * (1.0 - sg), axis=0, keepdims=True)
        first = pl.program_id(1) == 0

        @pl.when(first)
        def _():
            dlb_ref[...] = dlb
            dw_ref[...] = dw

        @pl.when(jnp.logical_not(first))
        def _():
            dlb_ref[...] += dlb
            dw_ref[...] += dw

    cols = [pl.BlockSpec((None, length, A_DK), functools.partial(lambda k, h, b: (b, 0, k * A_HEADS + h), k)) for k in range(4)]
    vec = pl.BlockSpec((1, A_DK), lambda h, b: (0, h))
    head = pl.BlockSpec((None, length, A_DK), lambda h, b: (b, 0, h))
    act = jax.ShapeDtypeStruct((bsz, length, A_WIDTH), BF16)
    return pl.pallas_call(
        body, name=name, grid=(A_HEADS, bsz), in_specs=cols + [vec, vec, head],
        out_specs=[head] * 4 + [vec, vec], out_shape=[act] * 4 + [jax.ShapeDtypeStruct((1, A_WIDTH), F32)] * 2,
        scratch_shapes=[pltpu.VMEM((nchunk, A_DK, A_DK), F32)] * 2 + [pltpu.VMEM((nchunk, 1, A_DK), F32)],
        compiler_params=pltpu.CompilerParams(dimension_semantics=("parallel", "arbitrary")),
    )(z, z, z, z, lb, onw, dy)


def _hgrn_fwd_loop(name, z, lb, onw):
    bsz, length, _ = z.shape
    nchunk = length // A_CHUNK

    def body(q_ref, f_ref, v_ref, g_ref, lb_ref, w_ref, y_ref, qd_s, ki_s, b_s, o_s):
        _, f = _hgrn_gates(f_ref[...], lb_ref[...])
        b = _chunk_cumsum(jnp.log(f), False)
        b_s[...] = b
        qd_s[...] = q_ref[...] * jnp.exp(b)
        ki_s[...] = (1.0 - f) * jnp.exp(-b)
        _hgrn_chunks(nchunk, qd_s, ki_s, b_s, v_ref, o_s)
        o = o_s[...]
        on = o * lax.rsqrt(jnp.mean(o * o, axis=-1, keepdims=True) + NORM_EPS)
        y_ref[...] = on * w_ref[...] * _silu(g_ref[...])

    cols = [pl.BlockSpec((None, length, A_DK), functools.partial(lambda k, b, h: (b, 0, k * A_HEADS + h), k)) for k in range(4)]
    vec = pl.BlockSpec((1, A_DK), lambda b, h: (0, h))
    return pl.pallas_call(
        body, name=name, grid=(bsz, A_HEADS), in_specs=cols + [vec, vec],
        out_specs=pl.BlockSpec((None, length, A_DK), lambda b, h: (b, 0, h)),
        out_shape=jax.ShapeDtypeStruct((bsz, length, A_WIDTH), F32),
        scratch_shapes=[pltpu.VMEM((length, A_DK), F32)] * 4,
        compiler_params=pltpu.CompilerParams(dimension_semantics=("parallel", "parallel")),
    )(z, z, z, z, lb, onw)


def _hgrn_bwd_loop(name, z, lb, onw, dy):
    bsz, length, _ = z.shape
    nchunk = length // A_CHUNK

    def body(q_ref, f_ref, v_ref, g_ref, lb_ref, w_ref, dy_ref, dq_ref, df_ref, dv_ref, dg_ref, dlb_ref, dw_ref,
             qd_s, ki_s, b_s, o_s, st_s, dqd_s, dki_s, dbl_s):
        lb = lb_ref[...]
        sg, f = _hgrn_gates(f_ref[...], lb)
        b = _chunk_cumsum(jnp.log(f), False)
        b_s[...] = b
        qd_s[...] = q_ref[...] * jnp.exp(b)
        ki_s[...] = (1.0 - f) * jnp.exp(-b)
        _hgrn_chunks(nchunk, qd_s, ki_s, b_s, v_ref, o_s, st_s)
        o, g, w, dyv = o_s[...], g_ref[...], w_ref[...], dy_ref[...]
        r = lax.rsqrt(jnp.mean(o * o, axis=-1, keepdims=True) + NORM_EPS)
        on = o * r
        sgg = jax.nn.sigmoid(g)
        gate = g * sgg
        dg_ref[...] = (dyv * on * w * (sgg * (1.0 + g * (1.0 - sgg)))).astype(dg_ref.dtype)
        dw = jnp.sum(dyv * on * gate, axis=0, keepdims=True)
        don = dyv * w * gate
        o_s[...] = r * (don - on * jnp.mean(don * on, axis=-1, keepdims=True))
        tri = lax.broadcasted_iota(jnp.int32, (A_CHUNK, A_CHUNK), 0) >= lax.broadcasted_iota(jnp.int32, (A_CHUNK, A_CHUNK), 1)
        last = lax.broadcasted_iota(jnp.int32, (A_CHUNK, A_DK), 0) == A_CHUNK - 1

        def back(t, dst):
            n = nchunk - 1 - t
            rows = pl.ds(pl.multiple_of(n * A_CHUNK, A_CHUNK), A_CHUNK)
            qd, ki, vc = qd_s[rows, :].astype(MXU_DTYPE), ki_s[rows, :], v_ref[rows, :].astype(MXU_DTYPE)
            kib = ki.astype(MXU_DTYPE)
            do = o_s[rows, :].astype(MXU_DTYPE)
            st = st_s[n]
            dec = jnp.exp(b_s[pl.ds(n * A_CHUNK + A_CHUNK - 1, 1), :])
            dstb = dst.astype(MXU_DTYPE)
            a = lax.dot_general(qd, kib, (((1,), (1,)), ((), ())), preferred_element_type=F32)
            a = jnp.where(tri, a, 0.0).astype(MXU_DTYPE)
            da = lax.dot_general(do, vc, (((1,), (1,)), ((), ())), preferred_element_type=F32)
            da = jnp.where(tri, da, 0.0).astype(MXU_DTYPE)
            ke = (ki * dec).astype(MXU_DTYPE)
            dv_ref[rows, :] = (lax.dot_general(a, do, (((0,), (0,)), ((), ())), preferred_element_type=F32)
                               + lax.dot_general(ke, dstb, (((1,), (1,)), ((), ())), preferred_element_type=F32)).astype(dv_ref.dtype)
            dqd_s[rows, :] = (jnp.dot(da, kib, preferred_element_type=F32)
                              + jnp.dot(do, st.astype(MXU_DTYPE), preferred_element_type=F32))
            dke = jnp.dot(vc, dstb, preferred_element_type=F32)
            dki_s[rows, :] = lax.dot_general(da, qd, (((0,), (0,)), ((), ())), preferred_element_type=F32) + dke * dec
            ddec = jnp.sum(dst * st, axis=0, keepdims=True) + jnp.sum(dke * ki, axis=0, keepdims=True)
            dbl_s[rows, :] = jnp.where(last, ddec * dec, 0.0)
            return dst * dec + lax.dot_general(do, qd, (((0,), (0,)), ((), ())), preferred_element_type=F32)

        lax.fori_loop(0, nchunk, back, jnp.zeros((A_DK, A_DK), F32), unroll=2)
        dqd, dki, qd, ki = dqd_s[...], dki_s[...], qd_s[...], ki_s[...]
        b = b_s[...]
        dlf = _chunk_cumsum(dqd * qd - dki * ki + dbl_s[...], True)
        dq_ref[...] = (dqd * jnp.exp(b)).astype(dq_ref.dtype)
        dfv = dlf / f - dki * jnp.exp(-b)
        df_ref[...] = (dfv * (1.0 - lb) * sg * (1.0 - sg)).astype(df_ref.dtype)
        dlb = jnp.sum(dfv * (1.0 - sg), axis=0, keepdims=True)
        first = pl.program_id(1) == 0

        @pl.when(first)
        def _():
            dlb_ref[...] = dlb
            dw_ref[...] = dw

        @pl.when(jnp.logical_not(first))
        def _():
            dlb_ref[...] += dlb
            dw_ref[...] += dw

    cols = [pl.BlockSpec((None, length, A_DK), functools.partial(lambda k, h, b: (b, 0, k * A_HEADS + h), k)) for k in range(4)]
    vec = pl.BlockSpec((1, A_DK), lambda h, b: (0, h))
    head = pl.BlockSpec((None, length, A_DK), lambda h, b: (b, 0, h))
    act = jax.ShapeDtypeStruct((bsz, length, A_WIDTH), BF16)
    return pl.pallas_call(
        body, name=name, grid=(A_HEADS, bsz), in_specs=cols + [vec, vec, head],
        out_specs=[head] * 4 + [vec, vec], out_shape=[act] * 4 + [jax.ShapeDtypeStruct((1, A_WIDTH), F32)] * 2,
        scratch_shapes=[pltpu.VMEM((length, A_DK), F32)] * 4 + [pltpu.VMEM((nchunk, A_DK, A_DK), F32)]
        + [pltpu.VMEM((length, A_DK), F32)] * 3,
        compiler_params=pltpu.CompilerParams(dimension_semantics=("parallel", "arbitrary")),
    )(z, z, z, z, lb, onw, dy)


S5_SEG = 16
S5_W = 256
S5_LANES = C_GROUPS * C_STATE
S5_NB = 8
S5_CH = D_MODEL // S5_NB
S5_COLS = 2 * S5_LANES // S5_NB


def _bd_mm(name, a, b, tb=False, out_dtype=F32, tm=1024):
    n = a.shape[0]
    nb, ka, kn = (b.shape[0], b.shape[2], b.shape[1]) if tb else b.shape
    tm = _divisor(n, tm)
    dims = (((1,), (1 if tb else 0,)), ((), ()))

    def body(a_ref, b_ref, o_ref):
        o_ref[...] = lax.dot_general(a_ref[...].astype(MXU_DTYPE), b_ref[...].astype(MXU_DTYPE), dims,
                                     preferred_element_type=F32).astype(o_ref.dtype)

    return pl.pallas_call(
        body, name=name, grid=(n // tm, nb),
        in_specs=[pl.BlockSpec((tm, ka), lambda i, j: (i, j)), pl.BlockSpec((None,) + b.shape[1:], lambda i, j: (j, 0, 0))],
        out_specs=pl.BlockSpec((tm, kn), lambda i, j: (i, j)), out_shape=jax.ShapeDtypeStruct((n, nb * kn), out_dtype),
        compiler_params=pltpu.CompilerParams(dimension_semantics=("parallel", "parallel")),
    )(a, b)


def _bd_wgrad(name, a, c, ka, kn, tk=1024):
    n = a.shape[0]
    nb = a.shape[1] // ka
    tk = _divisor(n, tk)

    def body(a_ref, c_ref, o_ref):
        p = lax.dot_general(a_ref[...].astype(MXU_DTYPE), c_ref[...].astype(MXU_DTYPE), (((0,), (0,)), ((), ())),
                            preferred_element_type=F32)
        first = pl.program_id(1) == 0

        @pl.when(first)
        def _():
            o_ref[...] = p

        @pl.when(jnp.logical_not(first))
        def _():
            o_ref[...] += p

    return pl.pallas_call(
        body, name=name, grid=(nb, n // tk),
        in_specs=[pl.BlockSpec((tk, ka), lambda j, k: (k, j)), pl.BlockSpec((tk, kn), lambda j, k: (k, j))],
        out_specs=pl.BlockSpec((None, ka, kn), lambda j, k: (j, 0, 0)), out_shape=jax.ShapeDtypeStruct((nb, ka, kn), F32),
        compiler_params=pltpu.CompilerParams(dimension_semantics=("parallel", "arbitrary")),
    )(a, c)


def _seg_permute(t, bsz):
    n, c = t.shape
    return t.reshape(bsz, S5_SEG, n // bsz // S5_SEG, c).transpose(0, 2, 1, 3).reshape(n, c)


def _seg_unpermute(t, bsz):
    n, c = t.shape
    return t.reshape(bsz, n // bsz // S5_SEG, S5_SEG, c).transpose(0, 2, 1, 3).reshape(n, c)


def _s5_weights(lam_re, lam_im, log_dt, b_re, b_im, c_re, c_im):
    lr = jnp.minimum(lam_re, C_MIN_NEG_RE)
    li = lam_im
    dt = jnp.exp(log_dt)[:, None]
    mag = jnp.exp(dt * lr)
    ar, ai = mag * jnp.cos(dt * li), mag * jnp.sin(dt * li)
    den = lr * lr + li * li
    zr = ((ar - 1.0) * lr + ai * li) / den
    zi = (ai * lr - (ar - 1.0) * li) / den
    bbr = zr[..., None] * b_re - zi[..., None] * b_im
    bbi = zr[..., None] * b_im + zi[..., None] * b_re
    gpb = C_GROUPS // S5_NB
    eye = jnp.eye(gpb, dtype=F32)
    bb = jnp.stack([bbr, bbi]).reshape(2, S5_NB, gpb, C_STATE, C_GROUP)
    wb = jnp.einsum('ij,rbjpc->bicjpr', eye, bb).reshape(S5_NB, S5_CH, -1, S5_W, 2)
    wb = wb.transpose(0, 1, 2, 4, 3).reshape(S5_NB, S5_CH, S5_COLS)
    cc = jnp.stack([c_re, -c_im]).reshape(2, S5_NB, gpb, C_GROUP, C_STATE)
    wc = jnp.einsum('ij,rbjcp->bjpric', eye, cc).reshape(S5_NB, -1, S5_W, 2, S5_CH)
    wc = wc.transpose(0, 1, 3, 2, 4).reshape(S5_NB, S5_COLS, S5_CH)
    return ar.reshape(1, S5_LANES), ai.reshape(1, S5_LANES), wb, wc


def _s5_scan(name, bu, a_re, a_im, bsz, reverse):
    n, width = bu.shape
    length = n // bsz
    steps = length // S5_SEG
    assert steps & (steps - 1) == 0
    w = S5_W

    def body(bu_ref, ar_ref, ai_ref, x_ref):
        ar = jnp.broadcast_to(ar_ref[...], (S5_SEG, w))
        ai = jnp.broadcast_to(ai_ref[...], (S5_SEG, w))
        if reverse:
            ai = -ai
        zero = jnp.zeros((S5_SEG, w), F32)

        def rows_of(j):
            jj = steps - 1 - j if reverse else j
            return pl.ds(pl.multiple_of(jj * S5_SEG, S5_SEG), S5_SEG)

        def local_step(j, st):
            sr, si = st
            rows = rows_of(j)
            nr = ar * sr - ai * si + bu_ref[rows, 0:w]
            ni = ar * si + ai * sr + bu_ref[rows, w:2 * w]
            x_ref[rows, 0:w] = nr
            x_ref[rows, w:2 * w] = ni
            return nr, ni

        er, ei = lax.fori_loop(0, steps, local_step, (zero, zero), unroll=4)
        pr, pi = ar[0:1], ai[0:1]
        for _ in range(steps.bit_length() - 1):
            pr, pi = pr * pr - pi * pi, 2.0 * pr * pi
        row = lax.broadcasted_iota(jnp.int32, (S5_SEG, w), 0)
        cr, ci = zero, zero
        inr, ini = jnp.zeros((1, w), F32), jnp.zeros((1, w), F32)
        order = list(range(S5_SEG))[::-1] if reverse else list(range(S5_SEG))
        for idx, s in enumerate(order):
            if idx:
                cr = jnp.where(row == s, inr, cr)
                ci = jnp.where(row == s, ini, ci)
            inr, ini = er[s:s + 1] + pr * inr - pi * ini, ei[s:s + 1] + pr * ini + pi * inr

        def carry_step(j, st):
            qr, qi = st
            rows = rows_of(j)
            x_ref[rows, 0:w] += qr * cr - qi * ci
            x_ref[rows, w:2 * w] += qr * ci + qi * cr
            return qr * ar - qi * ai, qr * ai + qi * ar

        lax.fori_loop(0, steps, carry_step, (ar, ai), unroll=4)

    blk = pl.BlockSpec((length, 2 * w), lambda b, j: (b, j))
    aspec = pl.BlockSpec((1, w), lambda b, j: (0, j))
    return pl.pallas_call(
        body, name=name, grid=(bsz, width // (2 * w)), in_specs=[blk, aspec, aspec], out_specs=blk,
        out_shape=jax.ShapeDtypeStruct(bu.shape, F32),
        compiler_params=pltpu.CompilerParams(dimension_semantics=("parallel", "parallel")),
    )(bu, a_re, a_im)


def _s5_da(name, x, g, bsz):
    n, width = x.shape
    length = n // bsz
    steps = length // S5_SEG
    w = S5_W

    def body(x_ref, g_ref, o_ref):
        row = lax.broadcasted_iota(jnp.int32, (S5_SEG, w), 0)
        last = pl.ds((steps - 1) * S5_SEG, S5_SEG)
        xpr = jnp.where(row == 0, 0.0, pltpu.roll(x_ref[last, 0:w], 1, 0))
        xpi = jnp.where(row == 0, 0.0, pltpu.roll(x_ref[last, w:2 * w], 1, 0))
        zero = jnp.zeros((S5_SEG, w), F32)

        def step(j, st):
            pr, pi, accr, acci = st
            rows = pl.ds(pl.multiple_of(j * S5_SEG, S5_SEG), S5_SEG)
            gr, gi = g_ref[rows, 0:w], g_ref[rows, w:2 * w]
            return x_ref[rows, 0:w], x_ref[rows, w:2 * w], accr + gr * pr + gi * pi, acci + gi * pr - gr * pi

        _, _, accr, acci = lax.fori_loop(0, steps, step, (xpr, xpi, zero, zero), unroll=4)
        first = pl.program_id(1) == 0

        @pl.when(first)
        def _():
            o_ref[:, 0:w] = accr
            o_ref[:, w:2 * w] = acci

        @pl.when(jnp.logical_not(first))
        def _():
            o_ref[:, 0:w] += accr
            o_ref[:, w:2 * w] += acci

    blk = pl.BlockSpec((length, 2 * w), lambda j, b: (b, j))
    return pl.pallas_call(
        body, name=name, grid=(width // (2 * w), bsz), in_specs=[blk, blk],
        out_specs=pl.BlockSpec((S5_SEG, 2 * w), lambda j, b: (0, j)), out_shape=jax.ShapeDtypeStruct((S5_SEG, width), F32),
        compiler_params=pltpu.CompilerParams(dimension_semantics=("parallel", "arbitrary")),
    )(x, g)


def _gelu(y):
    return 0.5 * y * (1.0 + lax.erf(y * math.sqrt(0.5)))


def _gelu_grad(y):
    return 0.5 * (1.0 + lax.erf(y * math.sqrt(0.5))) + y * jnp.exp(-0.5 * y * y) * (1.0 / math.sqrt(2.0 * math.pi))


def _s5_fwd(h, params, d_skip, bsz):
    (a_re, a_im, wb, wc), w_vjp = jax.vjp(_s5_weights, *params)
    wb, wc = wb.astype(BF16), wc.astype(BF16)
    hp = _seg_permute(h, bsz)
    bu = _bd_mm("s5_bu", hp, wb)
    xs = _s5_scan("s5_scan_f", bu, a_re, a_im, bsz, False)
    yc = _bd_mm("s5_cx", xs, wc)
    ypre, glp = _rowwise("s5_gelu", lambda yy, uu, dd: (lambda t: (t, _gelu(t)))(yy + dd * uu), [yc, hp], [d_skip],
                         out_rows=[(D_MODEL, F32), (D_MODEL, BF16)])
    return _seg_unpermute(glp, bsz), dict(hp=hp, xs=xs, ypre=ypre, a_re=a_re, a_im=a_im, wb=wb, wc=wc, w_vjp=w_vjp)


def _s5_bwd(dgl, sv, d_skip, bsz):
    dyp, dskip, dd = _rowwise(
        "b_s5_gelu", lambda dg, yy, uu, ds: (lambda t: (t, t * ds, jnp.sum(t * uu, axis=0, keepdims=True)))(dg * _gelu_grad(yy)),
        [_seg_permute(dgl, bsz), sv["ypre"], sv["hp"]], [d_skip], out_rows=[(D_MODEL, BF16), (D_MODEL, F32)],
        out_sums=[D_MODEL])
    dxh = _bd_mm("b_s5_cx_dx", dyp, sv["wc"], tb=True)
    dwc = _bd_wgrad("b_s5_cx_dw", sv["xs"], dyp, S5_COLS, S5_CH)
    gs = _s5_scan("s5_scan_b", dxh, sv["a_re"], sv["a_im"], bsz, True)
    da = _s5_da("s5_da", sv["xs"], gs, bsz)
    du = _bd_mm("b_s5_bu_dx", gs, sv["wb"], tb=True)
    dwb = _bd_wgrad("b_s5_bu_dw", sv["hp"], gs, S5_CH, S5_COLS)
    da = jnp.sum(da, axis=0).reshape(S5_LANES // S5_W, 2, S5_W)
    dp = sv["w_vjp"]((da[:, 0].reshape(1, S5_LANES), da[:, 1].reshape(1, S5_LANES), dwb, dwc))
    return _seg_unpermute(du + dskip, bsz), dp, dd


def _pack_rows(arrays):
    rows = []
    for a in arrays:
        flat = a.reshape(-1).astype(F32)
        pad = (-flat.shape[0]) % PACK_COLS
        rows.append(jnp.pad(flat, (0, pad)).reshape(-1, PACK_COLS))
    out = jnp.concatenate(rows, axis=0)
    return jnp.pad(out, ((0, (-out.shape[0]) % 16), (0, 0)))


def _unpack_rows(packed, shapes):
    out, r = [], 0
    for s in shapes:
        size = int(np.prod(s))
        nr = -(-size // PACK_COLS)
        out.append(packed[r:r + nr].reshape(-1)[:size].reshape(s))
        r += nr
    return out


def kernel(x, mem, norm_w, mem_norm_w, ab_w_in, ab_w_out, hgrn_lb_logits, hgrn_out_norm_w, s5_lambda_re, s5_lambda_im, s5_log_dt, s5_b_re, s5_b_im, s5_c_re, s5_c_im, s5_d, s5_w_glu, xattn_wq, xattn_wkv, xattn_wo, ffn_w_in, ffn_w_out, loss_target, m_norm_w, m_mem_norm_w, m_ab_w_in, m_ab_w_out, m_hgrn_lb_logits, m_hgrn_out_norm_w, m_s5_lambda_re, m_s5_lambda_im, m_s5_log_dt, m_s5_b_re, m_s5_b_im, m_s5_c_re, m_s5_c_im, m_s5_d, m_s5_w_glu, m_xattn_wq, m_xattn_wkv, m_xattn_wo, m_ffn_w_in, m_ffn_w_out, v_norm_w, v_mem_norm_w, v_ab_w_in, v_ab_w_out, v_hgrn_lb_logits, v_hgrn_out_norm_w, v_s5_lambda_re, v_s5_lambda_im, v_s5_log_dt, v_s5_b_re, v_s5_b_im, v_s5_c_re, v_s5_c_im, v_s5_d, v_s5_w_glu, v_xattn_wq, v_xattn_wkv, v_xattn_wo, v_ffn_w_in, v_ffn_w_out):
    given = dict(locals())
    w = {n: given[n] for n in WEIGHTS}
    mom = {n: given["m_" + n] for n in WEIGHTS}
    var = {n: given["v_" + n] for n in WEIGHTS}
    bsz, length, _ = x.shape
    ntok = bsz * length
    chip = 2 * lax.axis_index("x") + lax.axis_index("y")

    big_axes = [ax for _, ax in BIG]
    full = _gather_chips("gather_weights", [w[n].astype(BF16) for n in BIG_NAMES], big_axes)
    wf = dict(zip(BIG_NAMES, full))
    small_block = jnp.concatenate([w['norm_w'].reshape(12, -1), w['s5_d'].reshape(1, -1), jnp.zeros((3, 256), F32)], axis=0)
    (small_full,) = _gather_chips("gather_norm_w", [small_block[None]], [1])
    nw = small_full[0, :12].reshape(2, 6, 1, D_MODEL)
    s5_d_full = small_full[0, 12:13]

    lb_table, lb_vjp = jax.vjp(lambda t: jnp.cumsum(jax.nn.softmax(t, axis=0), axis=0), w['hgrn_lb_logits'])
    xs = x.reshape(ntok, D_MODEL)
    mem2 = mem.reshape(bsz * MEM_LEN, D_MODEL)
    tgt = loss_target.reshape(ntok, D_MODEL)
    saved = []
    (h,) = _rowwise("norm_in", lambda a, g: _rms(a, g), [xs], [nw[0, 0]], out_rows=[(D_MODEL, BF16)])
    cur = xs
    for layer in range(2):
        sv = {"x": cur}
        if layer == 0:
            z = _mm("ab_in", h, wf['ab_w_in'][0]).reshape(bsz, length, -1)
            sv["h0"] = h
            rope_cos, rope_sin = _rope_tables(length)
            branch_cnt = _branch_count(length)
            oa = _hgrn_fwd("hgrn_f", z, lb_table[0:1], w['hgrn_out_norm_w'])
            qr, kr, vb = _rope_qkv("rope_qkv", z, rope_cos, rope_sin)
            ob, lse = _dilated_fwd("dilated_f", qr, kr, vb, branch_cnt)
            core = jnp.concatenate([oa, ob], axis=-1).reshape(ntok, D_MODEL)
            sv.update(z=z, qr=qr, kr=kr, vb=vb, lse=lse, core=core)
            y = _mm("ab_out", core, wf['ab_w_out'][0])
        else:
            s5p = [w[n][0] for n in ('s5_lambda_re', 's5_lambda_im', 's5_log_dt', 's5_b_re', 's5_b_im', 's5_c_re', 's5_c_im')]
            gl, sv["s5"] = _s5_fwd(h, s5p, s5_d_full, bsz)
            sv["gl"] = gl
            zg = _mm("s5_glu", gl, wf['s5_w_glu'][0])
            sv["zg"] = zg
            (y,) = _rowwise("s5_gate", lambda t: t[:, :D_MODEL] * jax.nn.sigmoid(t[:, D_MODEL:]), [zg],
                            out_rows=[(D_MODEL, F32)])
        sv["y1"] = y
        x1, h2 = _rowwise(f"resnorm_a{layer}", lambda a, b, g1, g2: (lambda s: (s, _rms(s, g2)))(a + _rms(b, g1)),
                          [cur, y], [nw[layer, 1], nw[layer, 2]], out_rows=[(D_MODEL, F32), (D_MODEL, BF16)])
        sv["x1"], sv["h2"] = x1, h2
        (mem_n,) = _rowwise(f"mem_norm{layer}", lambda a, g: _rms(a, g), [mem2], [w['mem_norm_w'][layer][None]],
                            out_rows=[(D_MODEL, BF16)])
        sv["mem_n"] = mem_n
        q = _mm(f"xq{layer}", h2, wf['xattn_wq'][layer])
        kv = _mm(f"xkv{layer}", mem_n, wf['xattn_wkv'][layer])
        sv["q"], sv["kv"] = q, kv
        o = _xattn_fwd(f"xattn_f{layer}", q.reshape(bsz, length, D_MODEL), kv.reshape(bsz, MEM_LEN, 2 * D_MODEL))
        o = o.reshape(ntok, D_MODEL)
        sv["o"] = o
        y2 = _mm(f"xo{layer}", o, wf['xattn_wo'][layer])
        sv["y2"] = y2
        x2, h4 = _rowwise(f"resnorm_b{layer}", lambda a, b, g1, g2: (lambda s: (s, _rms(s, g2)))(a + _rms(b, g1)),
                          [x1, y2], [nw[layer, 3], nw[layer, 4]], out_rows=[(D_MODEL, F32), (D_MODEL, BF16)])
        sv["x2"], sv["h4"] = x2, h4
        act, sv["za"], sv["zb"] = _mm_swiglu(f"ffn_in{layer}", h4, wf['ffn_w_in'][layer])
        sv["act"] = act
        y3 = _mm(f"ffn_out{layer}", act, wf['ffn_w_out'][layer])
        sv["y3"] = y3
        saved.append(sv)
        if layer == 0:
            cur, h = _rowwise("resnorm_c0", lambda a, b, g1, g2: (lambda s: (s, _rms(s, g2)))(a + _rms(b, g1)),
                              [x2, y3], [nw[0, 5], nw[1, 0]], out_rows=[(D_MODEL, F32), (D_MODEL, F32)])
    g, sq = _rowwise("loss_head", lambda a, b, t, g1: (lambda e: (e * (1.0 / D_MODEL), jnp.sum(e * e, axis=0, keepdims=True)))(a + _rms(b, g1) - t),
                     [saved[1]["x2"], saved[1]["y3"], tgt], [nw[1, 5]], out_rows=[(D_MODEL, F32)], out_sums=[D_MODEL])
    loss = lax.psum(0.5 * jnp.sum(sq) / D_MODEL, ("x", "y", "c"))

    gbig = {}
    gnw = [[None] * 6 for _ in range(2)]
    gmemnw = [None, None]
    gsmall = {}
    for layer in (1, 0):
        sv = saved[layer]
        dy3, gnw[layer][5] = _rowwise(f"b_norm5_{layer}", lambda gg, yy, g1: _rms_bwd(yy, g1, gg), [g, sv["y3"]], [nw[layer, 5]],
                                      out_rows=[(D_MODEL, BF16)], out_sums=[D_MODEL])
        dact = _mm(f"b_ffn_out_dx{layer}", dy3, wf['ffn_w_out'][layer], tb=True)
        gw_out = _mm(f"b_ffn_out_dw{layer}", sv["act"], dy3, ta=True)

        def swiglu_bwd(a, b, da):
            a, b = a.astype(F32), b.astype(F32)
            sg = jax.nn.sigmoid(a)
            return jnp.concatenate([da * b * (sg * (1.0 + a * (1.0 - sg))), da * (a * sg)], axis=1)

        (dzf,) = _rowwise(f"b_swiglu{layer}", swiglu_bwd, [sv["za"], sv["zb"], dact], out_rows=[(2 * D_FF, BF16)], tile=256)
        dh4 = _mm(f"b_ffn_in_dx{layer}", dzf, wf['ffn_w_in'][layer], tb=True)
        gw_in = _mm(f"b_ffn_in_dw{layer}", sv["h4"], dzf, ta=True)
        gbig.setdefault('ffn_w_out', {})[layer] = gw_out
        gbig.setdefault('ffn_w_in', {})[layer] = gw_in

        def resnorm_bwd(gg, dh, xx, yy, g_in, g_res):
            dx, dw_in = _rms_bwd(xx, g_in, dh)
            tot = gg + dx
            dy, dw_res = _rms_bwd(yy, g_res, tot)
            return tot, dy, dw_in, dw_res

        g, dy2, gnw[layer][4], gnw[layer][3] = _rowwise(
            f"b_resnorm_b{layer}", resnorm_bwd, [g, dh4, sv["x2"], sv["y2"]], [nw[layer, 4], nw[layer, 3]],
            out_rows=[(D_MODEL, F32), (D_MODEL, BF16)], out_sums=[D_MODEL, D_MODEL])
        do = _mm(f"b_xo_dx{layer}", dy2, wf['xattn_wo'][layer], tb=True)
        gbig.setdefault('xattn_wo', {})[layer] = _mm(f"b_xo_dw{layer}", sv["o"], dy2, ta=True)
        dq, dk, dv = _xattn_bwd(f"xattn_b{layer}", sv["q"].reshape(bsz, length, D_MODEL),
                                sv["kv"].reshape(bsz, MEM_LEN, 2 * D_MODEL), do.reshape(bsz, length, D_MODEL))
        dq = dq.reshape(ntok, D_MODEL)
        dkv = jnp.concatenate([dk, dv], axis=-1).reshape(bsz * MEM_LEN, 2 * D_MODEL)
        dh2 = _mm(f"b_xq_dx{layer}", dq, wf['xattn_wq'][layer], tb=True)
        gbig.setdefault('xattn_wq', {})[layer] = _mm(f"b_xq_dw{layer}", sv["h2"], dq, ta=True)
        dmem_n = _mm(f"b_xkv_dx{layer}", dkv, wf['xattn_wkv'][layer], tb=True)
        gbig.setdefault('xattn_wkv', {})[layer] = _mm(f"b_xkv_dw{layer}", sv["mem_n"], dkv, ta=True)
        (gmemnw[layer],) = _rowwise(f"b_mem_norm{layer}", lambda dd, mm_, g1: _rms_bwd(mm_, g1, dd)[1], [dmem_n, mem2],
                                    [w['mem_norm_w'][layer][None]], out_sums=[D_MODEL])

        g, dy1, gnw[layer][2], gnw[layer][1] = _rowwise(
            f"b_resnorm_a{layer}", resnorm_bwd, [g, dh2, sv["x1"], sv["y1"]], [nw[layer, 2], nw[layer, 1]],
            out_rows=[(D_MODEL, F32), (D_MODEL, F32 if layer == 1 else BF16)], out_sums=[D_MODEL, D_MODEL])
        if layer == 1:
            def gate_bwd(t, dd):
                a, b = t[:, :D_MODEL], t[:, D_MODEL:]
                sg = jax.nn.sigmoid(b)
                return jnp.concatenate([dd * sg, dd * a * sg * (1.0 - sg)], axis=1)

            (dzg,) = _rowwise("b_s5_gate", gate_bwd, [sv["zg"], dy1], out_rows=[(2 * D_MODEL, BF16)])
            dgl = _mm("b_s5_glu_dx", dzg, wf['s5_w_glu'][0], tb=True)
            gbig['s5_w_glu'] = {0: _mm("b_s5_glu_dw", sv["gl"], dzg, ta=True)}
            dh0, dp, gsmall['s5_d'] = _s5_bwd(dgl, sv["s5"], s5_d_full, bsz)
            for n, t in zip(('s5_lambda_re', 's5_lambda_im', 's5_log_dt', 's5_b_re', 's5_b_im', 's5_c_re', 's5_c_im'), dp):
                gsmall[n] = t[None]
            g, gnw[1][0] = _rowwise("b_norm_in1", lambda gg, dh, xx, g1: (lambda r: (gg + r[0], r[1]))(_rms_bwd(xx, g1, dh)),
                                    [g, dh0, sv["x"]], [nw[1, 0]], out_rows=[(D_MODEL, F32)], out_sums=[D_MODEL])
        else:
            dcore = _mm("b_ab_out_dx", dy1, wf['ab_w_out'][0], tb=True)
            gbig['ab_w_out'] = {0: _mm("b_ab_out_dw", sv["core"], dy1, ta=True)}
            dcore = dcore.reshape(bsz, length, D_MODEL)
            core3 = sv["core"].reshape(bsz, length, D_MODEL)
            dqa, dfa, dia, dga, dlb0, gsmall['hgrn_out_norm_w'] = _hgrn_bwd("hgrn_b", sv["z"], lb_table[0:1], w['hgrn_out_norm_w'], dcore)
            dqb, dkb, dvb = _dilated_bwd("dilated_b", sv["qr"], sv["kr"], sv["vb"], core3, sv["lse"], dcore, branch_cnt,
                                         rope_cos, rope_sin, off=A_WIDTH // B_HD)
            (gsmall['hgrn_lb_logits'],) = lb_vjp(jnp.zeros_like(lb_table).at[0].set(dlb0[0]))
            dz = jnp.concatenate([dqa, dfa, dia, dga, dqb, dkb, dvb], axis=-1).reshape(ntok, -1)
            dh0 = _mm("b_ab_in_dx", dz, wf['ab_w_in'][0], tb=True)
            gbig['ab_w_in'] = {0: _mm("b_ab_in_dw", sv["h0"], dz, ta=True)}
            grad_x, gnw[0][0] = _rowwise("b_norm_in0", lambda gg, dh, xx, g1: (lambda r: (gg + r[0], r[1]))(_rms_bwd(xx, g1, dh)),
                                         [g, dh0, sv["x"]], [nw[0, 0]], out_rows=[(D_MODEL, F32)], out_sums=[D_MODEL])
    gsmall['norm_w'] = jnp.stack([jnp.concatenate(gnw[l], axis=0) for l in range(2)])
    gsmall['mem_norm_w'] = jnp.concatenate(gmemnw, axis=0)

    packed = _pack_rows([gsmall[n] for n in SMALL])
    theirs = _sibling_swap("small_swap", packed)
    chip_sum = _add2("small_pair_sum", packed, theirs)
    (all_chips,) = _gather_chips("small_gather", [chip_sum[None]], [0])
    small_sum = _sum_slots("small_sum", all_chips.reshape(N_CHIPS, packed.shape[0], PACK_COLS))
    full_shapes = [(2, 6, D_MODEL) if n == 'norm_w' else (1, D_MODEL) if n == 's5_d' else w[n].shape for n in SMALL]
    gs = dict(zip(SMALL, _unpack_rows(small_sum, full_shapes)))
    for n in SHARDED_SMALL:
        gs[n] = lax.dynamic_slice_in_dim(gs[n], chip * 256, 256, axis=gs[n].ndim - 1)

    pos = _pos_vec()
    parts = [jnp.stack([gbig[n][l] for l in sorted(gbig[n])]) for n in BIG_NAMES]
    theirs = _pair_send("grad_pair_send", parts, big_axes)
    pair = [_pair_add("grad_pair_sum_" + n, a, b, ax, pos) for (n, ax), a, b in zip(BIG, parts, theirs)]
    slots = _chip_exchange("grad_chip_exchange", pair, big_axes)
    shards = [_chip_sum("grad_chip_sum_" + n, a, b, s, ax, pos) for (n, ax), a, b, s in zip(BIG, parts, theirs, slots)]
    gfull = dict(zip(BIG_NAMES, _pair_join("grad_pair_join", shards)))

    grads, deltas, new_m, new_v = {}, {}, {}, {}
    for n in BIG_NAMES:
        grads[n] = gfull[n]
        deltas[n], new_m[n], new_v[n] = _adam("adam_" + n, w[n], gfull[n], mom[n], var[n])
    pk = [_pack_rows([t[n] for n in SMALL]) for t in (w, gs, mom, var)]
    small_out = _adam("adam_small", *pk)
    shard_shapes = [w[n].shape for n in SMALL]
    for dst, packed_out in zip((deltas, new_m, new_v), small_out):
        dst.update(zip(SMALL, _unpack_rows(packed_out, shard_shapes)))
    grads.update(gs)
    return (loss, grad_x.reshape(x.shape), *[grads[n] for n in WEIGHTS], *[deltas[n] for n in WEIGHTS],
            *[new_m[n] for n in WEIGHTS], *[new_v[n] for n in WEIGHTS])
```

```python
import functools
import math

import numpy as np
import jax
import jax.numpy as jnp
from jax import lax
from jax.experimental import pallas as pl
from jax.experimental.pallas import tpu as pltpu

F32 = jnp.float32
BF16 = jnp.bfloat16
MXU_DTYPE = jnp.bfloat16

D_MODEL = 1024
NORM_EPS = 1e-6
A_HEADS, A_DK, A_CHUNK = 4, 128, 32
A_WIDTH = A_HEADS * A_DK
B_HEADS, B_HD = 4, 128
B_WIDTH = B_HEADS * B_HD
B_DILATIONS = ((128, 1), (512, 4), (2048, 16))
ROPE_THETA = 10000.0
C_GROUP, C_GROUPS, C_STATE, C_CHUNK = 16, 64, 64, 128
C_MIN_NEG_RE = -1e-4
MEM_LEN = 256
X_HEADS = 4
X_HD = D_MODEL // X_HEADS
D_FF = 2816
ADAM_LR, ADAM_B1, ADAM_B2, ADAM_EPS, ADAM_WD, ADAM_STEP = 0.001, 0.9, 0.999, 1e-08, 0.01, 10

N_CHIPS = 4
MESH = pl.DeviceIdType.MESH
ANY = pl.BlockSpec(memory_space=pl.ANY)
_RELS = ((1, 0), (0, 1), (1, 1))

WEIGHTS = ['norm_w', 'mem_norm_w', 'ab_w_in', 'ab_w_out', 'hgrn_lb_logits', 'hgrn_out_norm_w', 's5_lambda_re',
           's5_lambda_im', 's5_log_dt', 's5_b_re', 's5_b_im', 's5_c_re', 's5_c_im', 's5_d', 's5_w_glu', 'xattn_wq',
           'xattn_wkv', 'xattn_wo', 'ffn_w_in', 'ffn_w_out']
BIG = (('ab_w_in', 1), ('ab_w_out', 0), ('s5_w_glu', 1), ('xattn_wq', 0), ('xattn_wkv', 1), ('xattn_wo', 0),
       ('ffn_w_in', 1), ('ffn_w_out', 0))
BIG_NAMES = tuple(n for n, _ in BIG)
SMALL = tuple(n for n in WEIGHTS if n not in BIG_NAMES)
SHARDED_SMALL = ('norm_w', 's5_d')
PACK_COLS = 1024


def _pos():
    return lax.axis_index("x"), lax.axis_index("y"), lax.axis_index("c")


def _flip(v, d):
    return 1 - v if d else v


def _divisor(n, want):
    for t in (want, 1024, 512, 256, 128, 64, 32, 16, 8):
        if t <= want and n % t == 0:
            return t
    return n


def _rowwise(name, fn, rows, bcasts=(), out_rows=(), out_sums=(), tile=256):
    n = rows[0].shape[0]
    t = _divisor(n, tile)
    nr, nb, no, ns = len(rows), len(bcasts), len(out_rows), len(out_sums)

    def body(*refs):
        vals = [r[...] for r in refs[:nr + nb]]
        res = fn(*vals)
        if not isinstance(res, (tuple, list)):
            res = (res,)
        outs = refs[nr + nb:]
        for k in range(no):
            outs[k][...] = res[k].astype(outs[k].dtype)
        if ns:
            first = pl.program_id(0) == 0
            for k in range(ns):
                o, val = outs[no + k], res[no + k]

                @pl.when(first)
                def _():
                    o[...] = val

                @pl.when(jnp.logical_not(first))
                def _():
                    o[...] += val

    in_specs = [pl.BlockSpec((t, r.shape[1]), lambda i: (i, 0)) for r in rows]
    in_specs += [pl.BlockSpec(b.shape, lambda i: (0, 0)) for b in bcasts]
    out_specs = [pl.BlockSpec((t, c), lambda i: (i, 0)) for c, _ in out_rows]
    out_specs += [pl.BlockSpec((1, c), lambda i: (0, 0)) for c in out_sums]
    out_shape = [jax.ShapeDtypeStruct((n, c), dt) for c, dt in out_rows]
    out_shape += [jax.ShapeDtypeStruct((1, c), F32) for c in out_sums]
    res = pl.pallas_call(
        body, name=name, grid=(n // t,), in_specs=in_specs, out_specs=out_specs, out_shape=out_shape,
        compiler_params=pltpu.CompilerParams(dimension_semantics=("arbitrary",)),
    )(*rows, *bcasts)
    return res


def _rms(x, w):
    r = lax.rsqrt(jnp.mean(x * x, axis=-1, keepdims=True) + NORM_EPS)
    return x * r * w


def _rms_bwd(x, w, dy):
    r = lax.rsqrt(jnp.mean(x * x, axis=-1, keepdims=True) + NORM_EPS)
    xh = x * r
    dxh = dy * w
    dx = r * (dxh - xh * jnp.mean(dxh * xh, axis=-1, keepdims=True))
    return dx, jnp.sum(dy * xh, axis=0, keepdims=True)


def _silu(z):
    return z * jax.nn.sigmoid(z)


MM_VMEM_BUDGET = 40 * 1024 * 1024


def _mm_tiles(m, n, k, ta, abytes, bbytes, obytes):
    tn = _divisor(n, 512)
    tk = _divisor(k, 1024) if ta else (k if k <= 2816 else next(t for t in (2816, 2048, 1792, 1024, 512) if k % t == 0))
    for tm in (2816, 2048, 1024, 512, 256, 128):
        if m % tm:
            continue
        need = 2 * (tm * tk * abytes + tk * tn * bbytes + tm * tn * obytes) + 2 * tm * tn * 4
        if need <= MM_VMEM_BUDGET:
            return tm, tn, tk
    return _divisor(m, 128), tn, tk


def _mm(name, a, b, ta=False, tb=False, out_dtype=F32):
    m, k = a.shape[::-1] if ta else a.shape
    k2, n = b.shape[::-1] if tb else b.shape
    assert k == k2, (name, a.shape, b.shape)
    tm, tn, tk = _mm_tiles(m, n, k, ta, a.dtype.itemsize, b.dtype.itemsize, jnp.dtype(out_dtype).itemsize)
    nk = k // tk
    dims = (((0 if ta else 1,), (1 if tb else 0,)), ((), ()))

    def prod(a_ref, b_ref):
        return lax.dot_general(a_ref[...].astype(MXU_DTYPE), b_ref[...].astype(MXU_DTYPE), dims,
                               preferred_element_type=F32)

    def body_one(a_ref, b_ref, o_ref):
        o_ref[...] = prod(a_ref, b_ref).astype(o_ref.dtype)

    def body_acc(a_ref, b_ref, o_ref, acc):
        kk = pl.program_id(2)

        @pl.when(kk == 0)
        def _():
            acc[...] = prod(a_ref, b_ref)

        @pl.when(kk > 0)
        def _():
            acc[...] += prod(a_ref, b_ref)

        @pl.when(kk == nk - 1)
        def _():
            o_ref[...] = acc[...].astype(o_ref.dtype)

    a_spec = pl.BlockSpec((tk, tm), lambda i, j, kk: (kk, i)) if ta else pl.BlockSpec((tm, tk), lambda i, j, kk: (i, kk))
    b_spec = pl.BlockSpec((tn, tk), lambda i, j, kk: (j, kk)) if tb else pl.BlockSpec((tk, tn), lambda i, j, kk: (kk, j))
    return pl.pallas_call(
        body_one if nk == 1 else body_acc, name=name, grid=(m // tm, n // tn, nk),
        in_specs=[a_spec, b_spec], out_specs=pl.BlockSpec((tm, tn), lambda i, j, kk: (i, j)),
        out_shape=jax.ShapeDtypeStruct((m, n), out_dtype),
        scratch_shapes=[] if nk == 1 else [pltpu.VMEM((tm, tn), F32)],
        compiler_params=pltpu.CompilerParams(dimension_semantics=("parallel", "parallel", "arbitrary")),
    )(a, b)


def _mm_gated(name, h, w, gate, out_dtype, tm=2048, tn=256):
    n, k = h.shape
    f = w.shape[1] // 2
    tm, nj = _divisor(n, tm), f // tn

    def body(h_ref, wa_ref, wb_ref, act_ref, za_ref, zb_ref):
        hv = h_ref[...].astype(MXU_DTYPE)
        za = jnp.dot(hv, wa_ref[...].astype(MXU_DTYPE), preferred_element_type=F32)
        zb = jnp.dot(hv, wb_ref[...].astype(MXU_DTYPE), preferred_element_type=F32)
        act_ref[...] = gate(za, zb).astype(act_ref.dtype)
        za_ref[...] = za.astype(za_ref.dtype)
        zb_ref[...] = zb.astype(zb_ref.dtype)

    out = pl.BlockSpec((tm, tn), lambda i, j: (i, j))
    return pl.pallas_call(
        body, name=name, grid=(n // tm, nj),
        in_specs=[pl.BlockSpec((tm, k), lambda i, j: (i, 0)), pl.BlockSpec((k, tn), lambda i, j: (0, j)),
                  pl.BlockSpec((k, tn), lambda i, j: (0, j + nj))],
        out_specs=[out] * 3,
        out_shape=[jax.ShapeDtypeStruct((n, f), out_dtype), jax.ShapeDtypeStruct((n, f), BF16), jax.ShapeDtypeStruct((n, f), BF16)],
        compiler_params=pltpu.CompilerParams(dimension_semantics=("parallel", "parallel")),
    )(h, w, w)


def _xattn_fwd(name, q, kv, tq=512):
    bsz, length, _ = q.shape
    tq = _divisor(length, tq)
    scale = X_HD ** -0.5

    def body(q_ref, k_ref, v_ref, o_ref):
        qv, kk, vv = q_ref[...].astype(MXU_DTYPE), k_ref[...].astype(MXU_DTYPE), v_ref[...].astype(MXU_DTYPE)
        s = lax.dot_general(qv, kk, (((1,), (1,)), ((), ())), preferred_element_type=F32) * scale
        p = jnp.exp(s - jnp.max(s, axis=-1, keepdims=True))
        p = p / jnp.sum(p, axis=-1, keepdims=True)
        o_ref[...] = jnp.dot(p.astype(MXU_DTYPE), vv, preferred_element_type=F32).astype(o_ref.dtype)

    return pl.pallas_call(
        body, name=name, grid=(bsz, X_HEADS, length // tq),
        in_specs=[pl.BlockSpec((None, tq, X_HD), lambda b, h, i: (b, i, h)),
                  pl.BlockSpec((None, MEM_LEN, X_HD), lambda b, h, i: (b, 0, h)),
                  pl.BlockSpec((None, MEM_LEN, X_HD), lambda b, h, i: (b, 0, X_HEADS + h))],
        out_specs=pl.BlockSpec((None, tq, X_HD), lambda b, h, i: (b, i, h)),
        out_shape=jax.ShapeDtypeStruct(q.shape, BF16),
        compiler_params=pltpu.CompilerParams(dimension_semantics=("parallel", "parallel", "arbitrary")),
    )(q, kv, kv)


def _xattn_bwd(name, q, kv, do, tq=512):
    bsz, length, _ = q.shape
    tq = _divisor(length, tq)
    scale = X_HD ** -0.5

    def body(q_ref, k_ref, v_ref, do_ref, dq_ref, dk_ref, dv_ref):
        qv, kk, vv = q_ref[...].astype(MXU_DTYPE), k_ref[...].astype(MXU_DTYPE), v_ref[...].astype(MXU_DTYPE)
        dov = do_ref[...].astype(MXU_DTYPE)
        s = lax.dot_general(qv, kk, (((1,), (1,)), ((), ())), preferred_element_type=F32) * scale
        p = jnp.exp(s - jnp.max(s, axis=-1, keepdims=True))
        p = p / jnp.sum(p, axis=-1, keepdims=True)
        dp = lax.dot_general(dov, vv, (((1,), (1,)), ((), ())), preferred_element_type=F32)
        ds = p * (dp - jnp.sum(dp * p, axis=-1, keepdims=True)) * scale
        dsb = ds.astype(MXU_DTYPE)
        dq_ref[...] = jnp.dot(dsb, kk, preferred_element_type=F32).astype(dq_ref.dtype)
        dk = lax.dot_general(dsb, qv, (((0,), (0,)), ((), ())), preferred_element_type=F32)
        dv = lax.dot_general(p.astype(MXU_DTYPE), dov, (((0,), (0,)), ((), ())), preferred_element_type=F32)
        first = pl.program_id(2) == 0

        @pl.when(first)
        def _():
            dk_ref[...] = dk
            dv_ref[...] = dv

        @pl.when(jnp.logical_not(first))
        def _():
            dk_ref[...] += dk
            dv_ref[...] += dv

    qspec = pl.BlockSpec((None, tq, X_HD), lambda b, h, i: (b, i, h))
    kspec = pl.BlockSpec((None, MEM_LEN, X_HD), lambda b, h, i: (b, 0, h))
    return pl.pallas_call(
        body, name=name, grid=(bsz, X_HEADS, length // tq),
        in_specs=[qspec, kspec, pl.BlockSpec((None, MEM_LEN, X_HD), lambda b, h, i: (b, 0, X_HEADS + h)), qspec],
        out_specs=[qspec, kspec, kspec],
        out_shape=[jax.ShapeDtypeStruct(q.shape, BF16), jax.ShapeDtypeStruct((bsz, MEM_LEN, D_MODEL), F32),
                   jax.ShapeDtypeStruct((bsz, MEM_LEN, D_MODEL), F32)],
        compiler_params=pltpu.CompilerParams(dimension_semantics=("parallel", "parallel", "arbitrary")),
    )(q, kv, kv, do)


def _dma_sems(*counts):
    return [pltpu.SemaphoreType.DMA((max(c, 1),)) for c in counts]


def _gather_chips(name, blocks, axes):
    n = len(blocks)
    shapes = [b.shape for b in blocks]

    def body(*refs):
        ins, outs = refs[:n], refs[n:2 * n]
        lsem, lrsem, ssem, rsem, fssem, frsem = refs[2 * n:]
        x, y, c = _pos()
        me = 2 * x + y

        def region(a, chip, h):
            _, r, cc = shapes[a]
            hr = r // 2
            if axes[a] == 0:
                return outs[a].at[:, pl.ds(chip * r + h * hr, hr), :]
            return outs[a].at[:, pl.ds(h * hr, hr), pl.ds(chip * cc, cc)]

        def whole(a, chip):
            _, r, cc = shapes[a]
            if axes[a] == 0:
                return outs[a].at[:, pl.ds(chip * r, r), :]
            return outs[a].at[:, :, pl.ds(chip * cc, cc)]

        sends = []
        for a in range(n):
            cp = pltpu.make_async_remote_copy(src_ref=ins[a], dst_ref=whole(a, me), send_sem=lsem.at[a], recv_sem=lrsem.at[a],
                                              device_id=(x, y, 1 - c), device_id_type=MESH)
            cp.start()
            sends.append(cp)
        for a in range(n):
            hr = shapes[a][1] // 2
            for k, (dx, dy) in enumerate(_RELS):
                cp = pltpu.make_async_remote_copy(
                    src_ref=ins[a].at[:, pl.ds(c * hr, hr), :], dst_ref=region(a, me, c),
                    send_sem=ssem.at[3 * a + k], recv_sem=rsem.at[3 * a + k],
                    device_id=(_flip(x, dx), _flip(y, dy), c), device_id_type=MESH)
                cp.start()
                sends.append(cp)
        for a in range(n):
            for k, (dx, dy) in enumerate(_RELS):
                px, py = _flip(x, dx), _flip(y, dy)
                got = region(a, 2 * px + py, c)
                pltpu.make_async_remote_copy(
                    src_ref=got, dst_ref=got, send_sem=ssem.at[3 * a + k], recv_sem=rsem.at[3 * a + k],
                    device_id=(px, py, c), device_id_type=MESH).wait_recv()
                cp = pltpu.make_async_remote_copy(
                    src_ref=got, dst_ref=got, send_sem=fssem.at[3 * a + k], recv_sem=frsem.at[3 * a + k],
                    device_id=(x, y, 1 - c), device_id_type=MESH)
                cp.start()
                sends.append(cp)
        for a in range(n):
            for k, (dx, dy) in enumerate(_RELS):
                got = region(a, 2 * _flip(x, dx) + _flip(y, dy), 1 - c)
                pltpu.make_async_remote_copy(
                    src_ref=got, dst_ref=got, send_sem=fssem.at[3 * a + k], recv_sem=frsem.at[3 * a + k],
                    device_id=(x, y, 1 - c), device_id_type=MESH).wait_recv()
        for a in range(n):
            pltpu.make_async_remote_copy(src_ref=ins[a], dst_ref=whole(a, me), send_sem=lsem.at[a], recv_sem=lrsem.at[a],
                                         device_id=(x, y, 1 - c), device_id_type=MESH).wait_recv()
        for cp in sends:
            cp.wait_send()

    out_shape = [jax.ShapeDtypeStruct((l, 4 * r, c) if ax == 0 else (l, r, 4 * c), b.dtype)
                 for (l, r, c), ax, b in zip(shapes, axes, blocks)]
    return pl.pallas_call(
        body, name=name, in_specs=[ANY] * n, out_specs=[ANY] * n, out_shape=out_shape,
        scratch_shapes=_dma_sems(n, n, 3 * n, 3 * n, 3 * n, 3 * n),
    )(*blocks)


def _pos_vec():
    x, y, c = _pos()
    return jnp.stack([c, 2 * x + y]).astype(jnp.int32)


def _pair_send(name, parts, axes):
    n = len(parts)
    shapes = [p.shape for p in parts]
    ncopy = sum(4 if ax == 0 else 1 for ax in axes)

    def body(*refs):
        ins, theirs = refs[:n], refs[n:2 * n]
        ssem, rsem = refs[2 * n:]
        x, y, c = _pos()
        pending, j = [], 0
        for a in range(n):
            _, rf, _ = shapes[a]
            if axes[a] == 0:
                hr = rf // 8
                pieces = [(ins[a].at[:, pl.ds((2 * s + 1 - c) * hr, hr), :], theirs[a].at[:, s]) for s in range(N_CHIPS)]
            else:
                hr = rf // 2
                pieces = [(ins[a].at[:, pl.ds((1 - c) * hr, hr), :], theirs[a])]
            for give, give_dst in pieces:
                rc = pltpu.make_async_remote_copy(src_ref=give, dst_ref=give_dst, send_sem=ssem.at[j],
                                                  recv_sem=rsem.at[j], device_id=(x, y, 1 - c), device_id_type=MESH)
                rc.start()
                pending.append(rc)
                j += 1
        for cp in pending:
            cp.wait()

    def half_shape(s, ax):
        return (s[0], N_CHIPS, s[1] // 8, s[2]) if ax == 0 else (s[0], s[1] // 2, s[2])

    out_shape = [jax.ShapeDtypeStruct(half_shape(s, ax), p.dtype) for s, ax, p in zip(shapes, axes, parts)]
    return pl.pallas_call(
        body, name=name, in_specs=[ANY] * n, out_specs=[ANY] * n, out_shape=out_shape,
        scratch_shapes=_dma_sems(ncopy, ncopy),
    )(*parts)


def _chip_exchange(name, halves, axes):
    n = len(halves)
    shapes = [h.shape for h in halves]

    def body(*refs):
        ins, outs = refs[:n], refs[n:2 * n]
        ssem, rsem = refs[2 * n:]
        x, y, c = _pos()

        def part(a, chip):
            if axes[a] == 0:
                return ins[a].at[:, chip]
            cc = shapes[a][2] // N_CHIPS
            return ins[a].at[:, :, pl.ds(chip * cc, cc)]

        sends = []
        for a in range(n):
            for k, (dx, dy) in enumerate(_RELS):
                px, py = _flip(x, dx), _flip(y, dy)
                rc = pltpu.make_async_remote_copy(
                    src_ref=part(a, 2 * px + py), dst_ref=outs[a].at[:, k], send_sem=ssem.at[3 * a + k],
                    recv_sem=rsem.at[3 * a + k], device_id=(px, py, c), device_id_type=MESH)
                rc.start()
                sends.append(rc)
        for cp in sends:
            cp.wait()

    def slot_shape(s, ax):
        return (s[0], 3, s[2], s[3]) if ax == 0 else (s[0], 3, s[1], s[2] // N_CHIPS)

    out_shape = [jax.ShapeDtypeStruct(slot_shape(s, ax), h.dtype) for s, ax, h in zip(shapes, axes, halves)]
    return pl.pallas_call(
        body, name=name, in_specs=[ANY] * n, out_specs=[ANY] * n, out_shape=out_shape,
        scratch_shapes=_dma_sems(3 * n, 3 * n),
    )(*halves)


def _pair_join(name, shards):
    n = len(shards)

    def body(*refs):
        outs = refs[n:2 * n]
        ssem, rsem = refs[2 * n:]
        x, y, c = _pos()
        pending = []
        for a in range(n):
            hr = shards[a].shape[1] // 2
            mine = outs[a].at[:, pl.ds(c * hr, hr), :]
            rc = pltpu.make_async_remote_copy(src_ref=mine, dst_ref=mine, send_sem=ssem.at[a], recv_sem=rsem.at[a],
                                              device_id=(x, y, 1 - c), device_id_type=MESH)
            rc.start()
            pending.append(rc)
        for a in range(n):
            hr = shards[a].shape[1] // 2
            got = outs[a].at[:, pl.ds((1 - c) * hr, hr), :]
            pltpu.make_async_remote_copy(src_ref=got, dst_ref=got, send_sem=ssem.at[a], recv_sem=rsem.at[a],
                                         device_id=(x, y, 1 - c), device_id_type=MESH).wait_recv()
        for cp in pending:
            cp.wait_send()

    return pl.pallas_call(
        body, name=name, in_specs=[ANY] * n, out_specs=[ANY] * n,
        out_shape=[jax.ShapeDtypeStruct(s.shape, s.dtype) for s in shards],
        input_output_aliases={a: a for a in range(n)}, scratch_shapes=_dma_sems(n, n),
    )(*shards)


def _pair_add(name, part, theirs, axis, pos):
    layers, rf, cf = part.shape

    def body(pos_ref, a_ref, b_ref, o_ref):
        o_ref[...] = (a_ref[...] + b_ref[...]).astype(o_ref.dtype)

    if axis == 0:
        hr = rf // 8
        grid = (layers, N_CHIPS)
        in_specs = [pl.BlockSpec((None, hr, cf), lambda l, s, p: (l, 2 * s + p[0], 0)),
                    pl.BlockSpec((None, None, hr, cf), lambda l, s, p: (l, s, 0, 0))]
        out_spec = pl.BlockSpec((None, None, hr, cf), lambda l, s, p: (l, s, 0, 0))
    else:
        hr, t = rf // 2, 128
        grid = (layers, hr // t)
        in_specs = [pl.BlockSpec((None, t, cf), lambda l, i, p: (l, p[0] * (hr // t) + i, 0)),
                    pl.BlockSpec((None, t, cf), lambda l, i, p: (l, i, 0))]
        out_spec = pl.BlockSpec((None, t, cf), lambda l, i, p: (l, i, 0))
    return pl.pallas_call(
        body, name=name, out_shape=jax.ShapeDtypeStruct(theirs.shape, BF16),
        grid_spec=pltpu.PrefetchScalarGridSpec(num_scalar_prefetch=1, grid=grid, in_specs=in_specs, out_specs=out_spec),
        compiler_params=pltpu.CompilerParams(dimension_semantics=("arbitrary", "arbitrary")),
    )(pos, part, theirs)


def _chip_sum(name, part, theirs, slots, axis, pos):
    layers, _, hr, c = slots.shape

    def body(pos_ref, mine, sib, s0, s1, s2, o_ref):
        o_ref[...] = (((mine[...] + sib[...]) + s0[...].astype(F32)) + s1[...].astype(F32)) + s2[...].astype(F32)

    t = hr if axis == 0 else 128
    if axis == 0:
        own_specs = [pl.BlockSpec((None, t, c), lambda l, i, p: (l, 2 * p[1] + p[0], 0)),
                     pl.BlockSpec((None, None, t, c), lambda l, i, p: (l, p[1], 0, 0))]
    else:
        own_specs = [pl.BlockSpec((None, t, c), lambda l, i, p: (l, p[0] * (hr // t) + i, p[1])),
                     pl.BlockSpec((None, t, c), lambda l, i, p: (l, i, p[1]))]
    slot_specs = [pl.BlockSpec((None, None, t, c), functools.partial(lambda k, l, i, p: (l, k, i, 0), k)) for k in range(3)]
    return pl.pallas_call(
        body, name=name, out_shape=jax.ShapeDtypeStruct((layers, 2 * hr, c), F32),
        grid_spec=pltpu.PrefetchScalarGridSpec(
            num_scalar_prefetch=1, grid=(layers, hr // t), in_specs=own_specs + slot_specs,
            out_specs=pl.BlockSpec((None, t, c), lambda l, i, p: (l, p[0] * (hr // t) + i, 0))),
        compiler_params=pltpu.CompilerParams(dimension_semantics=("arbitrary", "arbitrary")),
    )(pos, part, theirs, slots, slots, slots)


def _sibling_swap(name, v):
    def body(v_ref, o_ref, ssem, rsem):
        x, y, c = _pos()
        cp = pltpu.make_async_remote_copy(src_ref=v_ref, dst_ref=o_ref, send_sem=ssem.at[0], recv_sem=rsem.at[0],
                                          device_id=(x, y, 1 - c), device_id_type=MESH)
        cp.start()
        cp.wait()

    return pl.pallas_call(body, name=name, in_specs=[ANY], out_specs=ANY, out_shape=jax.ShapeDtypeStruct(v.shape, v.dtype),
                          scratch_shapes=_dma_sems(1, 1))(v)


def _add2(name, a, b):
    shape = a.shape
    a2, b2 = a.reshape(-1, shape[-1]), b.reshape(-1, shape[-1])
    (o,) = _rowwise(name, lambda u, v: u + v, [a2, b2], out_rows=[(shape[-1], F32)], tile=512)
    return o.reshape(shape)


def _sum_slots(name, slots):
    _, hr, c = slots.shape
    t = _divisor(hr, 256)

    def body(s0, s1, s2, s3, o_ref):
        o_ref[...] = ((s0[...] + s1[...]) + s2[...]) + s3[...]

    return pl.pallas_call(
        body, name=name, grid=(hr // t,),
        in_specs=[pl.BlockSpec((None, t, c), functools.partial(lambda k, i: (k, i, 0), k)) for k in range(N_CHIPS)],
        out_specs=pl.BlockSpec((t, c), lambda i: (i, 0)), out_shape=jax.ShapeDtypeStruct((hr, c), F32),
        compiler_params=pltpu.CompilerParams(dimension_semantics=("arbitrary",)),
    )(slots, slots, slots, slots)


def _adam_tile(w, g, m, v):
    m = ADAM_B1 * m + (1.0 - ADAM_B1) * g
    v = ADAM_B2 * v + (1.0 - ADAM_B2) * (g * g)
    m_hat = m / (1.0 - ADAM_B1 ** ADAM_STEP)
    v_hat = v / (1.0 - ADAM_B2 ** ADAM_STEP)
    delta = -ADAM_LR * (m_hat / (jnp.sqrt(v_hat) + ADAM_EPS) + ADAM_WD * w)
    return delta, m, v


def _adam(name, w, g, m, v):
    shape = w.shape
    c = shape[-1]
    flat = [t.reshape(-1, c) for t in (w, g, m, v)]
    res = _rowwise(name, _adam_tile, flat, out_rows=[(c, F32)] * 3, tile=256)
    return [r.reshape(shape) for r in res]


ATT_T = 256
ATT_NEG = -1e30


def _branch_count(length):
    nblk = length // ATT_T
    d = (np.arange(nblk)[:, None, None] * ATT_T + np.arange(ATT_T)[None, :, None] - np.arange(ATT_T)[None, None, :])
    cnt = np.zeros(d.shape, np.float32)
    for window, dil in B_DILATIONS:
        cnt += ((d >= 0) & (d % dil == 0) & (d <= window)).astype(np.float32)
    return jnp.asarray(cnt)


def _rope_tables(length):
    half = B_HD // 2
    inv_freq = ROPE_THETA ** (-jnp.arange(half, dtype=F32) / half)
    ang = jnp.arange(length, dtype=F32)[:, None] * inv_freq[None, :]
    cos, sin = jnp.cos(ang), jnp.sin(ang)
    return jnp.concatenate([cos, cos], axis=1), jnp.concatenate([-sin, sin], axis=1)


def _swap_halves(t):
    return pltpu.roll(t, B_HD // 2, 1)


def _rope_qkv(name, z, cos, sin, t=256):
    bsz, length, _ = z.shape
    t = _divisor(length, t)

    def body(q_ref, k_ref, v_ref, c_ref, s_ref, qo, ko, vo):
        c, s = c_ref[...], s_ref[...]
        for src, dst in ((q_ref, qo), (k_ref, ko)):
            for h in range(B_HEADS):
                cols = slice(h * B_HD, (h + 1) * B_HD)
                xh = src[:, cols]
                dst[:, cols] = (xh * c + _swap_halves(xh) * s).astype(dst.dtype)
        vo[...] = v_ref[...].astype(vo.dtype)

    col0 = 4 * A_WIDTH // B_WIDTH
    specs = [pl.BlockSpec((None, t, B_WIDTH), functools.partial(lambda k, b, i: (b, i, col0 + k), k)) for k in range(3)]
    tab = pl.BlockSpec((t, B_HD), lambda b, i: (i, 0))
    out = pl.BlockSpec((None, t, B_WIDTH), lambda b, i: (b, i, 0))
    return pl.pallas_call(
        body, name=name, grid=(bsz, length // t), in_specs=specs + [tab, tab], out_specs=[out] * 3,
        out_shape=[jax.ShapeDtypeStruct((bsz, length, B_WIDTH), BF16)] * 3,
        compiler_params=pltpu.CompilerParams(dimension_semantics=("parallel", "parallel")),
    )(z, z, z, cos, sin)


def _dilated_fwd(name, q, k, v, cnt):
    bsz, length, _ = q.shape
    scale = B_HD ** -0.5
    nblk = length // ATT_T

    def body(cnt_ref, q_ref, k_ref, v_ref, o_ref, lse_ref):
        i = pl.program_id(2)
        qb = q_ref[...]

        def step(j, carry):
            m, l, acc = carry
            rows = pl.ds(pl.multiple_of(j * ATT_T, ATT_T), ATT_T)
            s = lax.dot_general(qb, k_ref[rows, :], (((1,), (1,)), ((), ())), preferred_element_type=F32) * scale
            c = cnt_ref[i - j]
            s = jnp.where(c > 0.0, s, ATT_NEG)
            m_new = jnp.maximum(m, jnp.max(s, axis=-1, keepdims=True))
            a = jnp.exp(m - m_new)
            p = c * jnp.exp(s - m_new)
            l = a * l + jnp.sum(p, axis=-1, keepdims=True)
            acc = a * acc + jnp.dot(p.astype(MXU_DTYPE), v_ref[rows, :], preferred_element_type=F32)
            return m_new, l, acc

        init = (jnp.full((ATT_T, 1), ATT_NEG, F32), jnp.zeros((ATT_T, 1), F32), jnp.zeros((ATT_T, B_HD), F32))
        m, l, acc = lax.fori_loop(0, i + 1, step, init)
        o_ref[...] = acc / l
        lse_ref[...] = jnp.broadcast_to(m + jnp.log(l), (ATT_T, B_HD))

    qspec = pl.BlockSpec((None, ATT_T, B_HD), lambda b, h, i: (b, i, h))
    kspec = pl.BlockSpec((None, length, B_HD), lambda b, h, i: (b, 0, h))
    return pl.pallas_call(
        body, name=name, grid=(bsz, B_HEADS, nblk),
        in_specs=[pl.BlockSpec(cnt.shape, lambda b, h, i: (0, 0, 0)), qspec, kspec, kspec],
        out_specs=[qspec, pl.BlockSpec((None, None, ATT_T, B_HD), lambda b, h, i: (b, h, i, 0))],
        out_shape=[jax.ShapeDtypeStruct((bsz, length, B_WIDTH), F32), jax.ShapeDtypeStruct((bsz, B_HEADS, length, B_HD), F32)],
        compiler_params=pltpu.CompilerParams(dimension_semantics=("parallel", "parallel", "arbitrary")),
    )(cnt, q, k, v)


def _dilated_bwd(name, q, k, v, o, lse, do, cnt, cos, sin, off=0):
    bsz, length, _ = q.shape
    scale = B_HD ** -0.5
    nblk = length // ATT_T

    def body(cnt_ref, q_ref, k_ref, v_ref, o_ref, lse_ref, do_ref, c_ref, s_ref, dq_ref, dk_ref, dv_ref, dq_acc, dk_acc, dv_acc):
        dk_acc[...] = jnp.zeros_like(dk_acc)
        dv_acc[...] = jnp.zeros_like(dv_acc)

        def outer(i, _):
            rq = pl.ds(pl.multiple_of(i * ATT_T, ATT_T), ATT_T)
            qi, doi = q_ref[rq, :], do_ref[rq, :]
            lsei = lse_ref[rq, :][:, 0:1]
            di = jnp.sum(doi * o_ref[rq, :], axis=-1, keepdims=True)
            dob = doi.astype(MXU_DTYPE)

            def inner(j, dq):
                rk = pl.ds(pl.multiple_of(j * ATT_T, ATT_T), ATT_T)
                kj, vj = k_ref[rk, :], v_ref[rk, :]
                s = lax.dot_general(qi, kj, (((1,), (1,)), ((), ())), preferred_element_type=F32) * scale
                c = cnt_ref[i - j]
                p = c * jnp.exp(jnp.where(c > 0.0, s, ATT_NEG) - lsei)
                dp = lax.dot_general(dob, vj, (((1,), (1,)), ((), ())), preferred_element_type=F32)
                ds = (p * (dp - di) * scale).astype(MXU_DTYPE)
                dk_acc[rk, :] += lax.dot_general(ds, qi, (((0,), (0,)), ((), ())), preferred_element_type=F32)
                dv_acc[rk, :] += lax.dot_general(p.astype(MXU_DTYPE), dob, (((0,), (0,)), ((), ())), preferred_element_type=F32)
                return dq + jnp.dot(ds, kj, preferred_element_type=F32)

            dq_acc[rq, :] = lax.fori_loop(0, i + 1, inner, jnp.zeros((ATT_T, B_HD), F32))
            return 0

        lax.fori_loop(0, nblk, outer, 0)
        c, s = c_ref[...], s_ref[...]
        for acc, dst in ((dq_acc, dq_ref), (dk_acc, dk_ref)):
            g = acc[...]
            dst[...] = (g * c + _swap_halves(g * s)).astype(dst.dtype)
        dv_ref[...] = dv_acc[...].astype(dv_ref.dtype)

    hspec = pl.BlockSpec((None, length, B_HD), lambda b, h: (b, 0, h))
    ospec = pl.BlockSpec((None, length, B_HD), lambda b, h: (b, 0, off + h))
    tab = pl.BlockSpec((length, B_HD), lambda b, h: (0, 0))
    return pl.pallas_call(
        body, name=name, grid=(bsz, B_HEADS),
        in_specs=[pl.BlockSpec(cnt.shape, lambda b, h: (0, 0, 0)), hspec, hspec, hspec, ospec,
                  pl.BlockSpec((None, None, length, B_HD), lambda b, h: (b, h, 0, 0)), ospec, tab, tab],
        out_specs=[hspec] * 3, out_shape=[jax.ShapeDtypeStruct((bsz, length, B_WIDTH), BF16)] * 3,
        scratch_shapes=[pltpu.VMEM((length, B_HD), F32)] * 3,
        compiler_params=pltpu.CompilerParams(dimension_semantics=("parallel", "parallel")),
    )(cnt, q, k, v, o, lse, do, cos, sin)


def _chunk_cumsum(t, reverse):
    n = t.shape[0]
    row = lax.broadcasted_iota(jnp.int32, t.shape, 0) & (A_CHUNK - 1)
    s = 1
    while s < A_CHUNK:
        if reverse:
            t = t + jnp.where(row < A_CHUNK - s, pltpu.roll(t, n - s, 0), 0.0)
        else:
            t = t + jnp.where(row >= s, pltpu.roll(t, s, 0), 0.0)
        s *= 2
    return t


def _hgrn_gates(fl, lb):
    sg = jax.nn.sigmoid(fl)
    f = lb + (1.0 - lb) * sg
    return sg, f


def _hgrn_chunks(nchunk, qd_s, ki_s, b_s, v_ref, o_s, st_s=None):
    tri = lax.broadcasted_iota(jnp.int32, (A_CHUNK, A_CHUNK), 0) >= lax.broadcasted_iota(jnp.int32, (A_CHUNK, A_CHUNK), 1)

    def step(n, st):
        rows = pl.ds(pl.multiple_of(n * A_CHUNK, A_CHUNK), A_CHUNK)
        if st_s is not None:
            st_s[n] = st
        qd, ki, vc = qd_s[rows, :].astype(MXU_DTYPE), ki_s[rows, :], v_ref[rows, :].astype(MXU_DTYPE)
        dec = jnp.exp(b_s[pl.ds(n * A_CHUNK + A_CHUNK - 1, 1), :])
        a = lax.dot_general(qd, ki.astype(MXU_DTYPE), (((1,), (1,)), ((), ())), preferred_element_type=F32)
        a = jnp.where(tri, a, 0.0).astype(MXU_DTYPE)
        o_s[rows, :] = (jnp.dot(a, vc, preferred_element_type=F32)
                        + lax.dot_general(qd, st.astype(MXU_DTYPE), (((1,), (1,)), ((), ())), preferred_element_type=F32))
        ke = (ki * dec).astype(MXU_DTYPE)
        return st * dec + lax.dot_general(vc, ke, (((0,), (0,)), ((), ())), preferred_element_type=F32)

    lax.fori_loop(0, nchunk, step, jnp.zeros((A_DK, A_DK), F32), unroll=4)


def _bmm(a, b, ca, cb):
    return lax.dot_general(a, b, (((ca,), (cb,)), ((0,), (0,))), preferred_element_type=F32)


def _hgrn_forward_chunks(nchunk, q, f, b, v_ref, st_s, dec_s):
    shape = (nchunk, A_CHUNK, A_DK)
    b3 = b.reshape(shape)
    dec = jnp.exp(b3[:, A_CHUNK - 1:A_CHUNK, :])
    dec_s[...] = dec
    qd = (q * jnp.exp(b)).reshape(shape)
    ki = ((1.0 - f) * jnp.exp(-b)).reshape(shape)
    qdb, kib, keb = qd.astype(MXU_DTYPE), ki.astype(MXU_DTYPE), (ki * dec).astype(MXU_DTYPE)
    v3 = v_ref[...].reshape(shape).astype(MXU_DTYPE)
    tri = (lax.broadcasted_iota(jnp.int32, (1, A_CHUNK, A_CHUNK), 1) >= lax.broadcasted_iota(jnp.int32, (1, A_CHUNK, A_CHUNK), 2))
    a = jnp.where(tri, _bmm(qdb, kib, 2, 2), 0.0).astype(MXU_DTYPE)
    st_s[...] = _bmm(v3, keb, 1, 1)

    def rec(n, st):
        u = st_s[n]
        st_s[n] = st
        return st * dec_s[n] + u

    lax.fori_loop(0, nchunk, rec, jnp.zeros((A_DK, A_DK), F32))
    o = _bmm(a, v3, 2, 1) + _bmm(qdb, st_s[...].astype(MXU_DTYPE), 2, 2)
    return dict(dec=dec, qd=qd, ki=ki, qdb=qdb, kib=kib, keb=keb, v3=v3, a=a, tri=tri), o


def _hgrn_fwd(name, z, lb, onw):
    bsz, length, _ = z.shape
    nchunk = length // A_CHUNK

    def body(q_ref, f_ref, v_ref, g_ref, lb_ref, w_ref, y_ref, st_s, dec_s):
        _, f = _hgrn_gates(f_ref[...], lb_ref[...])
        b = _chunk_cumsum(jnp.log(f), False)
        _, o = _hgrn_forward_chunks(nchunk, q_ref[...], f, b, v_ref, st_s, dec_s)
        o = o.reshape(length, A_DK)
        on = o * lax.rsqrt(jnp.mean(o * o, axis=-1, keepdims=True) + NORM_EPS)
        y_ref[...] = on * w_ref[...] * _silu(g_ref[...])

    cols = [pl.BlockSpec((None, length, A_DK), functools.partial(lambda k, b, h: (b, 0, k * A_HEADS + h), k)) for k in range(4)]
    vec = pl.BlockSpec((1, A_DK), lambda b, h: (0, h))
    return pl.pallas_call(
        body, name=name, grid=(bsz, A_HEADS), in_specs=cols + [vec, vec],
        out_specs=pl.BlockSpec((None, length, A_DK), lambda b, h: (b, 0, h)),
        out_shape=jax.ShapeDtypeStruct((bsz, length, A_WIDTH), F32),
        scratch_shapes=[pltpu.VMEM((nchunk, A_DK, A_DK), F32), pltpu.VMEM((nchunk, 1, A_DK), F32)],
        compiler_params=pltpu.CompilerParams(dimension_semantics=("parallel", "parallel")),
    )(z, z, z, z, lb, onw)


def _hgrn_bwd(name, z, lb, onw, dy):
    bsz, length, _ = z.shape
    nchunk = length // A_CHUNK
    shape = (nchunk, A_CHUNK, A_DK)

    def body(q_ref, f_ref, v_ref, g_ref, lb_ref, w_ref, dy_ref, dq_ref, df_ref, dv_ref, dg_ref, dlb_ref, dw_ref,
             st_s, dst_s, dec_s):
        lb = lb_ref[...]
        sg, f = _hgrn_gates(f_ref[...], lb)
        b = _chunk_cumsum(jnp.log(f), False)
        t, o = _hgrn_forward_chunks(nchunk, q_ref[...], f, b, v_ref, st_s, dec_s)
        o, g, w, dyv = o.reshape(length, A_DK), g_ref[...], w_ref[...], dy_ref[...]
        r = lax.rsqrt(jnp.mean(o * o, axis=-1, keepdims=True) + NORM_EPS)
        on = o * r
        sgg = jax.nn.sigmoid(g)
        gate = g * sgg
        dg_ref[...] = (dyv * on * w * (sgg * (1.0 + g * (1.0 - sgg)))).astype(dg_ref.dtype)
        dw = jnp.sum(dyv * on * gate, axis=0, keepdims=True)
        don = dyv * w * gate
        do = (r * (don - on * jnp.mean(don * on, axis=-1, keepdims=True))).reshape(shape).astype(MXU_DTYPE)
        da = jnp.where(t["tri"], _bmm(do, t["v3"], 2, 2), 0.0).astype(MXU_DTYPE)
        dst_s[...] = _bmm(do, t["qdb"], 1, 1)

        def rec(i, dst):
            n = nchunk - 1 - i
            u = dst_s[n]
            dst_s[n] = dst
            return dst * dec_s[n] + u

        lax.fori_loop(0, nchunk, rec, jnp.zeros((A_DK, A_DK), F32))
        dst, st = dst_s[...], st_s[...]
        dstb = dst.astype(MXU_DTYPE)
        dec, ki, qd = t["dec"], t["ki"], t["qd"]
        dv_ref[...] = (_bmm(t["a"], do, 1, 1) + _bmm(t["keb"], dstb, 2, 2)).reshape(length, A_DK).astype(dv_ref.dtype)
        dqd = _bmm(da, t["kib"], 2, 1) + _bmm(do, st.astype(MXU_DTYPE), 2, 1)
        dke = _bmm(t["v3"], dstb, 2, 1)
        dki = _bmm(da, t["qdb"], 1, 1) + dke * dec
        ddec = jnp.sum(dst * st, axis=1, keepdims=True) + jnp.sum(dke * ki, axis=1, keepdims=True)
        last = lax.broadcasted_iota(jnp.int32, (1, A_CHUNK, A_DK), 1) == A_CHUNK - 1
        db = (dqd * qd - dki * ki + jnp.where(last, ddec * dec, 0.0)).reshape(length, A_DK)
        dlf = _chunk_cumsum(db, True)
        dq_ref[...] = (dqd.reshape(length, A_DK) * jnp.exp(b)).astype(dq_ref.dtype)
        dfv = dlf / f - dki.reshape(length, A_DK) * jnp.exp(-b)
        df_ref[...] = (dfv * (1.0 - lb) * sg * (1.0 - sg)).astype(df_ref.dtype)
        dlb = jnp.sum(dfv * (1.0 - sg), axis=0, keepdims=True)
        first = pl.program_id(1) == 0

        @pl.when(first)
        def _():
            dlb_ref[...] = dlb
            dw_ref[...] = dw

        @pl.when(jnp.logical_not(first))
        def _():
            dlb_ref[...] += dlb
            dw_ref[...] += dw

    cols = [pl.BlockSpec((None, length, A_DK), functools.partial(lambda k, h, b: (b, 0, k * A_HEADS + h), k)) for k in range(4)]
    vec = pl.BlockSpec((1, A_DK), lambda h, b: (0, h))
    head = pl.BlockSpec((None, length, A_DK), lambda h, b: (b, 0, h))
    act = jax.ShapeDtypeStruct((bsz, length, A_WIDTH), BF16)
    return pl.pallas_call(
        body, name=name, grid=(A_HEADS, bsz), in_specs=cols + [vec, vec, head],
        out_specs=[head] * 4 + [vec, vec], out_shape=[act] * 4 + [jax.ShapeDtypeStruct((1, A_WIDTH), F32)] * 2,
        scratch_shapes=[pltpu.VMEM((nchunk, A_DK, A_DK), F32)] * 2 + [pltpu.VMEM((nchunk, 1, A_DK), F32)],
        compiler_params=pltpu.CompilerParams(dimension_semantics=("parallel", "arbitrary")),
    )(z, z, z, z, lb, onw, dy)


def _hgrn_fwd_loop(name, z, lb, onw):
    bsz, length, _ = z.shape
    nchunk = length // A_CHUNK

    def body(q_ref, f_ref, v_ref, g_ref, lb_ref, w_ref, y_ref, qd_s, ki_s, b_s, o_s):
        _, f = _hgrn_gates(f_ref[...], lb_ref[...])
        b = _chunk_cumsum(jnp.log(f), False)
        b_s[...] = b
        qd_s[...] = q_ref[...] * jnp.exp(b)
        ki_s[...] = (1.0 - f) * jnp.exp(-b)
        _hgrn_chunks(nchunk, qd_s, ki_s, b_s, v_ref, o_s)
        o = o_s[...]
        on = o * lax.rsqrt(jnp.mean(o * o, axis=-1, keepdims=True) + NORM_EPS)
        y_ref[...] = on * w_ref[...] * _silu(g_ref[...])

    cols = [pl.BlockSpec((None, length, A_DK), functools.partial(lambda k, b, h: (b, 0, k * A_HEADS + h), k)) for k in range(4)]
    vec = pl.BlockSpec((1, A_DK), lambda b, h: (0, h))
    return pl.pallas_call(
        body, name=name, grid=(bsz, A_HEADS), in_specs=cols + [vec, vec],
        out_specs=pl.BlockSpec((None, length, A_DK), lambda b, h: (b, 0, h)),
        out_shape=jax.ShapeDtypeStruct((bsz, length, A_WIDTH), F32),
        scratch_shapes=[pltpu.VMEM((length, A_DK), F32)] * 4,
        compiler_params=pltpu.CompilerParams(dimension_semantics=("parallel", "parallel")),
    )(z, z, z, z, lb, onw)


def _hgrn_bwd_loop(name, z, lb, onw, dy):
    bsz, length, _ = z.shape
    nchunk = length // A_CHUNK

    def body(q_ref, f_ref, v_ref, g_ref, lb_ref, w_ref, dy_ref, dq_ref, df_ref, dv_ref, dg_ref, dlb_ref, dw_ref,
             qd_s, ki_s, b_s, o_s, st_s, dqd_s, dki_s, dbl_s):
        lb = lb_ref[...]
        sg, f = _hgrn_gates(f_ref[...], lb)
        b = _chunk_cumsum(jnp.log(f), False)
        b_s[...] = b
        qd_s[...] = q_ref[...] * jnp.exp(b)
        ki_s[...] = (1.0 - f) * jnp.exp(-b)
        _hgrn_chunks(nchunk, qd_s, ki_s, b_s, v_ref, o_s, st_s)
        o, g, w, dyv = o_s[...], g_ref[...], w_ref[...], dy_ref[...]
        r = lax.rsqrt(jnp.mean(o * o, axis=-1, keepdims=True) + NORM_EPS)
        on = o * r
        sgg = jax.nn.sigmoid(g)
        gate = g * sgg
        dg_ref[...] = (dyv * on * w * (sgg * (1.0 + g * (1.0 - sgg)))).astype(dg_ref.dtype)
        dw = jnp.sum(dyv * on * gate, axis=0, keepdims=True)
        don = dyv * w * gate
        o_s[...] = r * (don - on * jnp.mean(don * on, axis=-1, keepdims=True))
        tri = lax.broadcasted_iota(jnp.int32, (A_CHUNK, A_CHUNK), 0) >= lax.broadcasted_iota(jnp.int32, (A_CHUNK, A_CHUNK), 1)
        last = lax.broadcasted_iota(jnp.int32, (A_CHUNK, A_DK), 0) == A_CHUNK - 1

        def back(t, dst):
            n = nchunk - 1 - t
            rows = pl.ds(pl.multiple_of(n * A_CHUNK, A_CHUNK), A_CHUNK)
            qd, ki, vc = qd_s[rows, :].astype(MXU_DTYPE), ki_s[rows, :], v_ref[rows, :].astype(MXU_DTYPE)
            kib = ki.astype(MXU_DTYPE)
            do = o_s[rows, :].astype(MXU_DTYPE)
            st = st_s[n]
            dec = jnp.exp(b_s[pl.ds(n * A_CHUNK + A_CHUNK - 1, 1), :])
            dstb = dst.astype(MXU_DTYPE)
            a = lax.dot_general(qd, kib, (((1,), (1,)), ((), ())), preferred_element_type=F32)
            a = jnp.where(tri, a, 0.0).astype(MXU_DTYPE)
            da = lax.dot_general(do, vc, (((1,), (1,)), ((), ())), preferred_element_type=F32)
            da = jnp.where(tri, da, 0.0).astype(MXU_DTYPE)
            ke = (ki * dec).astype(MXU_DTYPE)
            dv_ref[rows, :] = (lax.dot_general(a, do, (((0,), (0,)), ((), ())), preferred_element_type=F32)
                               + lax.dot_general(ke, dstb, (((1,), (1,)), ((), ())), preferred_element_type=F32)).astype(dv_ref.dtype)
            dqd_s[rows, :] = (jnp.dot(da, kib, preferred_element_type=F32)
                              + jnp.dot(do, st.astype(MXU_DTYPE), preferred_element_type=F32))
            dke = jnp.dot(vc, dstb, preferred_element_type=F32)
            dki_s[rows, :] = lax.dot_general(da, qd, (((0,), (0,)), ((), ())), preferred_element_type=F32) + dke * dec
            ddec = jnp.sum(dst * st, axis=0, keepdims=True) + jnp.sum(dke * ki, axis=0, keepdims=True)
            dbl_s[rows, :] = jnp.where(last, ddec * dec, 0.0)
            return dst * dec + lax.dot_general(do, qd, (((0,), (0,)), ((), ())), preferred_element_type=F32)

        lax.fori_loop(0, nchunk, back, jnp.zeros((A_DK, A_DK), F32), unroll=2)
        dqd, dki, qd, ki = dqd_s[...], dki_s[...], qd_s[...], ki_s[...]
        b = b_s[...]
        dlf = _chunk_cumsum(dqd * qd - dki * ki + dbl_s[...], True)
        dq_ref[...] = (dqd * jnp.exp(b)).astype(dq_ref.dtype)
        dfv = dlf / f - dki * jnp.exp(-b)
        df_ref[...] = (dfv * (1.0 - lb) * sg * (1.0 - sg)).astype(df_ref.dtype)
        dlb = jnp.sum(dfv * (1.0 - sg), axis=0, keepdims=True)
        first = pl.program_id(1) == 0

        @pl.when(first)
        def _():
            dlb_ref[...] = dlb
            dw_ref[...] = dw

        @pl.when(jnp.logical_not(first))
        def _():
            dlb_ref[...] += dlb
            dw_ref[...] += dw

    cols = [pl.BlockSpec((None, length, A_DK), functools.partial(lambda k, h, b: (b, 0, k * A_HEADS + h), k)) for k in range(4)]
    vec = pl.BlockSpec((1, A_DK), lambda h, b: (0, h))
    head = pl.BlockSpec((None, length, A_DK), lambda h, b: (b, 0, h))
    act = jax.ShapeDtypeStruct((bsz, length, A_WIDTH), BF16)
    return pl.pallas_call(
        body, name=name, grid=(A_HEADS, bsz), in_specs=cols + [vec, vec, head],
        out_specs=[head] * 4 + [vec, vec], out_shape=[act] * 4 + [jax.ShapeDtypeStruct((1, A_WIDTH), F32)] * 2,
        scratch_shapes=[pltpu.VMEM((length, A_DK), F32)] * 4 + [pltpu.VMEM((nchunk, A_DK, A_DK), F32)]
        + [pltpu.VMEM((length, A_DK), F32)] * 3,
        compiler_params=pltpu.CompilerParams(dimension_semantics=("parallel", "arbitrary")),
    )(z, z, z, z, lb, onw, dy)


S5_SEG = 16
S5_W = 256
S5_LANES = C_GROUPS * C_STATE
S5_NB = 8
S5_CH = D_MODEL // S5_NB
S5_COLS = 2 * S5_LANES // S5_NB


def _bd_mm(name, a, b, tb=False, out_dtype=F32, tm=1024):
    n = a.shape[0]
    nb, ka, kn = (b.shape[0], b.shape[2], b.shape[1]) if tb else b.shape
    tm = _divisor(n, tm)
    dims = (((1,), (1 if tb else 0,)), ((), ()))

    def body(a_ref, b_ref, o_ref):
        o_ref[...] = lax.dot_general(a_ref[...].astype(MXU_DTYPE), b_ref[...].astype(MXU_DTYPE), dims,
                                     preferred_element_type=F32).astype(o_ref.dtype)

    return pl.pallas_call(
        body, name=name, grid=(n // tm, nb),
        in_specs=[pl.BlockSpec((tm, ka), lambda i, j: (i, j)), pl.BlockSpec((None,) + b.shape[1:], lambda i, j: (j, 0, 0))],
        out_specs=pl.BlockSpec((tm, kn), lambda i, j: (i, j)), out_shape=jax.ShapeDtypeStruct((n, nb * kn), out_dtype),
        compiler_params=pltpu.CompilerParams(dimension_semantics=("parallel", "parallel")),
    )(a, b)


def _bd_wgrad(name, a, c, ka, kn, tk=1024):
    n = a.shape[0]
    nb = a.shape[1] // ka
    tk = _divisor(n, tk)

    def body(a_ref, c_ref, o_ref):
        p = lax.dot_general(a_ref[...].astype(MXU_DTYPE), c_ref[...].astype(MXU_DTYPE), (((0,), (0,)), ((), ())),
                            preferred_element_type=F32)
        first = pl.program_id(1) == 0

        @pl.when(first)
        def _():
            o_ref[...] = p

        @pl.when(jnp.logical_not(first))
        def _():
            o_ref[...] += p

    return pl.pallas_call(
        body, name=name, grid=(nb, n // tk),
        in_specs=[pl.BlockSpec((tk, ka), lambda j, k: (k, j)), pl.BlockSpec((tk, kn), lambda j, k: (k, j))],
        out_specs=pl.BlockSpec((None, ka, kn), lambda j, k: (j, 0, 0)), out_shape=jax.ShapeDtypeStruct((nb, ka, kn), F32),
        compiler_params=pltpu.CompilerParams(dimension_semantics=("parallel", "arbitrary")),
    )(a, c)


def _seg_permute(t, bsz):
    n, c = t.shape
    return t.reshape(bsz, S5_SEG, n // bsz // S5_SEG, c).transpose(0, 2, 1, 3).reshape(n, c)


def _seg_unpermute(t, bsz):
    n, c = t.shape
    return t.reshape(bsz, n // bsz // S5_SEG, S5_SEG, c).transpose(0, 2, 1, 3).reshape(n, c)


def _s5_weights(lam_re, lam_im, log_dt, b_re, b_im, c_re, c_im):
    lr = jnp.minimum(lam_re, C_MIN_NEG_RE)
    li = lam_im
    dt = jnp.exp(log_dt)[:, None]
    mag = jnp.exp(dt * lr)
    ar, ai = mag * jnp.cos(dt * li), mag * jnp.sin(dt * li)
    den = lr * lr + li * li
    zr = ((ar - 1.0) * lr + ai * li) / den
    zi = (ai * lr - (ar - 1.0) * li) / den
    bbr = zr[..., None] * b_re - zi[..., None] * b_im
    bbi = zr[..., None] * b_im + zi[..., None] * b_re
    gpb = C_GROUPS // S5_NB
    eye = jnp.eye(gpb, dtype=F32)
    bb = jnp.stack([bbr, bbi]).reshape(2, S5_NB, gpb, C_STATE, C_GROUP)
    wb = jnp.einsum('ij,rbjpc->bicjpr', eye, bb).reshape(S5_NB, S5_CH, -1, S5_W, 2)
    wb = wb.transpose(0, 1, 2, 4, 3).reshape(S5_NB, S5_CH, S5_COLS)
    cc = jnp.stack([c_re, -c_im]).reshape(2, S5_NB, gpb, C_GROUP, C_STATE)
    wc = jnp.einsum('ij,rbjcp->bjpric', eye, cc).reshape(S5_NB, -1, S5_W, 2, S5_CH)
    wc = wc.transpose(0, 1, 3, 2, 4).reshape(S5_NB, S5_COLS, S5_CH)
    return ar.reshape(1, S5_LANES), ai.reshape(1, S5_LANES), wb, wc


def _s5_scan(name, bu, a_re, a_im, bsz, reverse):
    n, width = bu.shape
    length = n // bsz
    steps = length // S5_SEG
    assert steps & (steps - 1) == 0
    w = S5_W

    def body(bu_ref, ar_ref, ai_ref, x_ref):
        ar = jnp.broadcast_to(ar_ref[...], (S5_SEG, w))
        ai = jnp.broadcast_to(ai_ref[...], (S5_SEG, w))
        if reverse:
            ai = -ai
        zero = jnp.zeros((S5_SEG, w), F32)

        def rows_of(j):
            jj = steps - 1 - j if reverse else j
            return pl.ds(pl.multiple_of(jj * S5_SEG, S5_SEG), S5_SEG)

        def local_step(j, st):
            sr, si = st
            rows = rows_of(j)
            nr = ar * sr - ai * si + bu_ref[rows, 0:w]
            ni = ar * si + ai * sr + bu_ref[rows, w:2 * w]
            x_ref[rows, 0:w] = nr
            x_ref[rows, w:2 * w] = ni
            return nr, ni

        er, ei = lax.fori_loop(0, steps, local_step, (zero, zero), unroll=4)
        pr, pi = ar[0:1], ai[0:1]
        for _ in range(steps.bit_length() - 1):
            pr, pi = pr * pr - pi * pi, 2.0 * pr * pi
        row = lax.broadcasted_iota(jnp.int32, (S5_SEG, w), 0)
        cr, ci = zero, zero
        inr, ini = jnp.zeros((1, w), F32), jnp.zeros((1, w), F32)
        order = list(range(S5_SEG))[::-1] if reverse else list(range(S5_SEG))
        for idx, s in enumerate(order):
            if idx:
                cr = jnp.where(row == s, inr, cr)
                ci = jnp.where(row == s, ini, ci)
            inr, ini = er[s:s + 1] + pr * inr - pi * ini, ei[s:s + 1] + pr * ini + pi * inr

        def carry_step(j, st):
            qr, qi = st
            rows = rows_of(j)
            x_ref[rows, 0:w] += qr * cr - qi * ci
            x_ref[rows, w:2 * w] += qr * ci + qi * cr
            return qr * ar - qi * ai, qr * ai + qi * ar

        lax.fori_loop(0, steps, carry_step, (ar, ai), unroll=4)

    blk = pl.BlockSpec((length, 2 * w), lambda b, j: (b, j))
    aspec = pl.BlockSpec((1, w), lambda b, j: (0, j))
    return pl.pallas_call(
        body, name=name, grid=(bsz, width // (2 * w)), in_specs=[blk, aspec, aspec], out_specs=blk,
        out_shape=jax.ShapeDtypeStruct(bu.shape, F32),
        compiler_params=pltpu.CompilerParams(dimension_semantics=("parallel", "parallel")),
    )(bu, a_re, a_im)


def _s5_da(name, x, g, bsz):
    n, width = x.shape
    length = n // bsz
    steps = length // S5_SEG
    w = S5_W

    def body(x_ref, g_ref, o_ref):
        row = lax.broadcasted_iota(jnp.int32, (S5_SEG, w), 0)
        last = pl.ds((steps - 1) * S5_SEG, S5_SEG)
        xpr = jnp.where(row == 0, 0.0, pltpu.roll(x_ref[last, 0:w], 1, 0))
        xpi = jnp.where(row == 0, 0.0, pltpu.roll(x_ref[last, w:2 * w], 1, 0))
        zero = jnp.zeros((S5_SEG, w), F32)

        def step(j, st):
            pr, pi, accr, acci = st
            rows = pl.ds(pl.multiple_of(j * S5_SEG, S5_SEG), S5_SEG)
            gr, gi = g_ref[rows, 0:w], g_ref[rows, w:2 * w]
            return x_ref[rows, 0:w], x_ref[rows, w:2 * w], accr + gr * pr + gi * pi, acci + gi * pr - gr * pi

        _, _, accr, acci = lax.fori_loop(0, steps, step, (xpr, xpi, zero, zero), unroll=4)
        first = pl.program_id(1) == 0

        @pl.when(first)
        def _():
            o_ref[:, 0:w] = accr
            o_ref[:, w:2 * w] = acci

        @pl.when(jnp.logical_not(first))
        def _():
            o_ref[:, 0:w] += accr
            o_ref[:, w:2 * w] += acci

    blk = pl.BlockSpec((length, 2 * w), lambda j, b: (b, j))
    return pl.pallas_call(
        body, name=name, grid=(width // (2 * w), bsz), in_specs=[blk, blk],
        out_specs=pl.BlockSpec((S5_SEG, 2 * w), lambda j, b: (0, j)), out_shape=jax.ShapeDtypeStruct((S5_SEG, width), F32),
        compiler_params=pltpu.CompilerParams(dimension_semantics=("parallel", "arbitrary")),
    )(x, g)


def _scan_in_place(ref, c0, ar1, ai1, steps, reverse):
    w = S5_W
    ar = jnp.broadcast_to(ar1, (S5_SEG, w))
    ai = jnp.broadcast_to(-ai1 if reverse else ai1, (S5_SEG, w))
    zero = jnp.zeros((S5_SEG, w), F32)
    re, im = pl.ds(c0, w), pl.ds(c0 + w, w)

    def rows_of(j):
        jj = steps - 1 - j if reverse else j
        return pl.ds(pl.multiple_of(jj * S5_SEG, S5_SEG), S5_SEG)

    def local_step(j, st):
        sr, si = st
        rows = rows_of(j)
        nr = ar * sr - ai * si + ref[rows, re]
        ni = ar * si + ai * sr + ref[rows, im]
        ref[rows, re] = nr
        ref[rows, im] = ni
        return nr, ni

    er, ei = lax.fori_loop(0, steps, local_step, (zero, zero), unroll=4)
    pr, pi = ar[0:1], ai[0:1]
    for _ in range(steps.bit_length() - 1):
        pr, pi = pr * pr - pi * pi, 2.0 * pr * pi
    row = lax.broadcasted_iota(jnp.int32, (S5_SEG, w), 0)
    cr, ci = zero, zero
    inr, ini = jnp.zeros((1, w), F32), jnp.zeros((1, w), F32)
    order = list(range(S5_SEG))[::-1] if reverse else list(range(S5_SEG))
    for idx, s in enumerate(order):
        if idx:
            cr = jnp.where(row == s, inr, cr)
            ci = jnp.where(row == s, ini, ci)
        inr, ini = er[s:s + 1] + pr * inr - pi * ini, ei[s:s + 1] + pr * ini + pi * inr

    def carry_step(j, st):
        qr, qi = st
        rows = rows_of(j)
        ref[rows, re] += qr * cr - qi * ci
        ref[rows, im] += qr * ci + qi * cr
        return qr * ar - qi * ai, qr * ai + qi * ar

    lax.fori_loop(0, steps, carry_step, (ar, ai), unroll=4)


def _da_partial(x_ref, g_ref, c0, steps):
    w = S5_W
    re, im = pl.ds(c0, w), pl.ds(c0 + w, w)
    row = lax.broadcasted_iota(jnp.int32, (S5_SEG, w), 0)
    last = pl.ds((steps - 1) * S5_SEG, S5_SEG)
    xpr = jnp.where(row == 0, 0.0, pltpu.roll(x_ref[last, re], 1, 0))
    xpi = jnp.where(row == 0, 0.0, pltpu.roll(x_ref[last, im], 1, 0))
    zero = jnp.zeros((S5_SEG, w), F32)

    def step(j, st):
        pr, pi, accr, acci = st
        rows = pl.ds(pl.multiple_of(j * S5_SEG, S5_SEG), S5_SEG)
        gr, gi = g_ref[rows, re], g_ref[rows, im]
        return x_ref[rows, re], x_ref[rows, im], accr + gr * pr + gi * pi, acci + gi * pr - gr * pi

    _, _, accr, acci = lax.fori_loop(0, steps, step, (xpr, xpi, zero, zero), unroll=4)
    return accr, acci


S5_VMEM_LIMIT = 56 * 1024 * 1024


def _s5_states(name, hp, wb, wc, a_re, a_im, bsz):
    n = hp.shape[0]
    length = n // bsz
    steps = length // S5_SEG
    assert steps & (steps - 1) == 0
    nsub = S5_COLS // (2 * S5_W)

    def body(h_ref, wb_ref, wc_ref, ar_ref, ai_ref, x_ref, y_ref):
        x_ref[...] = jnp.dot(h_ref[...].astype(MXU_DTYPE), wb_ref[...], preferred_element_type=F32)
        for sub in range(nsub):
            lanes = slice(sub * S5_W, (sub + 1) * S5_W)
            _scan_in_place(x_ref, sub * 2 * S5_W, ar_ref[:, lanes], ai_ref[:, lanes], steps, False)
        y_ref[...] = jnp.dot(x_ref[...].astype(MXU_DTYPE), wc_ref[...], preferred_element_type=F32)

    chan = pl.BlockSpec((length, S5_CH), lambda b, j: (b, j))
    avec = pl.BlockSpec((1, nsub * S5_W), lambda b, j: (0, j))
    return pl.pallas_call(
        body, name=name, grid=(bsz, S5_NB),
        in_specs=[chan, pl.BlockSpec((None, S5_CH, S5_COLS), lambda b, j: (j, 0, 0)),
                  pl.BlockSpec((None, S5_COLS, S5_CH), lambda b, j: (j, 0, 0)), avec, avec],
        out_specs=[pl.BlockSpec((length, S5_COLS), lambda b, j: (b, j)), chan],
        out_shape=[jax.ShapeDtypeStruct((n, S5_NB * S5_COLS), F32), jax.ShapeDtypeStruct((n, D_MODEL), F32)],
        compiler_params=pltpu.CompilerParams(dimension_semantics=("parallel", "parallel"), vmem_limit_bytes=S5_VMEM_LIMIT),
    )(hp, wb, wc, a_re, a_im)


def _s5_states_bwd(name, dyp, xs, hp, wb, wc, a_re, a_im, bsz):
    n = hp.shape[0]
    length = n // bsz
    steps = length // S5_SEG
    nsub = S5_COLS // (2 * S5_W)

    def body(dy_ref, x_ref, h_ref, wb_ref, wc_ref, ar_ref, ai_ref, du_ref, dwb_ref, dwc_ref, da_ref, g_s):
        dy = dy_ref[...]
        g_s[...] = lax.dot_general(dy, wc_ref[...], (((1,), (1,)), ((), ())), preferred_element_type=F32)
        das = []
        for sub in range(nsub):
            lanes = slice(sub * S5_W, (sub + 1) * S5_W)
            _scan_in_place(g_s, sub * 2 * S5_W, ar_ref[:, lanes], ai_ref[:, lanes], steps, True)
            das += list(_da_partial(x_ref, g_s, sub * 2 * S5_W, steps))
        gb = g_s[...].astype(MXU_DTYPE)
        du_ref[...] = lax.dot_general(gb, wb_ref[...], (((1,), (1,)), ((), ())), preferred_element_type=F32)
        dwb = lax.dot_general(h_ref[...].astype(MXU_DTYPE), gb, (((0,), (0,)), ((), ())), preferred_element_type=F32)
        dwc = lax.dot_general(x_ref[...].astype(MXU_DTYPE), dy, (((0,), (0,)), ((), ())), preferred_element_type=F32)
        first = pl.program_id(1) == 0

        @pl.when(first)
        def _():
            dwb_ref[...] = dwb
            dwc_ref[...] = dwc
            for k, t in enumerate(das):
                da_ref[:, k * S5_W:(k + 1) * S5_W] = t

        @pl.when(jnp.logical_not(first))
        def _():
            dwb_ref[...] += dwb
            dwc_ref[...] += dwc
            for k, t in enumerate(das):
                da_ref[:, k * S5_W:(k + 1) * S5_W] += t

    chan = pl.BlockSpec((length, S5_CH), lambda j, b: (b, j))
    avec = pl.BlockSpec((1, nsub * S5_W), lambda j, b: (0, j))
    wbs = pl.BlockSpec((None, S5_CH, S5_COLS), lambda j, b: (j, 0, 0))
    wcs = pl.BlockSpec((None, S5_COLS, S5_CH), lambda j, b: (j, 0, 0))
    return pl.pallas_call(
        body, name=name, grid=(S5_NB, bsz),
        in_specs=[chan, pl.BlockSpec((length, S5_COLS), lambda j, b: (b, j)), chan, wbs, wcs, avec, avec],
        out_specs=[chan, wbs, wcs, pl.BlockSpec((S5_SEG, S5_COLS), lambda j, b: (0, j))],
        out_shape=[jax.ShapeDtypeStruct((n, D_MODEL), F32), jax.ShapeDtypeStruct(wb.shape, F32),
                   jax.ShapeDtypeStruct(wc.shape, F32), jax.ShapeDtypeStruct((S5_SEG, S5_NB * S5_COLS), F32)],
        scratch_shapes=[pltpu.VMEM((length, S5_COLS), F32)],
        compiler_params=pltpu.CompilerParams(dimension_semantics=("parallel", "arbitrary"), vmem_limit_bytes=S5_VMEM_LIMIT),
    )(dyp, xs, hp, wb, wc, a_re, a_im)


def _gelu(y):
    return 0.5 * y * (1.0 + lax.erf(y * math.sqrt(0.5)))


def _gelu_grad(y):
    return 0.5 * (1.0 + lax.erf(y * math.sqrt(0.5))) + y * jnp.exp(-0.5 * y * y) * (1.0 / math.sqrt(2.0 * math.pi))


def _s5_fwd(h, params, d_skip, bsz):
    (a_re, a_im, wb, wc), w_vjp = jax.vjp(_s5_weights, *params)
    wb, wc = wb.astype(BF16), wc.astype(BF16)
    hp = _seg_permute(h, bsz)
    xs, yc = _s5_states("s5_states_f", hp, wb, wc, a_re, a_im, bsz)
    ypre, glp = _rowwise("s5_gelu", lambda yy, uu, dd: (lambda t: (t, _gelu(t)))(yy + dd * uu), [yc, hp], [d_skip],
                         out_rows=[(D_MODEL, F32), (D_MODEL, BF16)])
    return _seg_unpermute(glp, bsz), dict(hp=hp, xs=xs, ypre=ypre, a_re=a_re, a_im=a_im, wb=wb, wc=wc, w_vjp=w_vjp)


def _s5_bwd(dgl, sv, d_skip, bsz):
    dyp, dskip, dd = _rowwise(
        "b_s5_gelu", lambda dg, yy, uu, ds: (lambda t: (t, t * ds, jnp.sum(t * uu, axis=0, keepdims=True)))(dg * _gelu_grad(yy)),
        [_seg_permute(dgl, bsz), sv["ypre"], sv["hp"]], [d_skip], out_rows=[(D_MODEL, BF16), (D_MODEL, F32)],
        out_sums=[D_MODEL])
    du, dwb, dwc, da = _s5_states_bwd("s5_states_b", dyp, sv["xs"], sv["hp"], sv["wb"], sv["wc"], sv["a_re"], sv["a_im"], bsz)
    da = jnp.sum(da, axis=0).reshape(S5_LANES // S5_W, 2, S5_W)
    dp = sv["w_vjp"]((da[:, 0].reshape(1, S5_LANES), da[:, 1].reshape(1, S5_LANES), dwb, dwc))
    return _seg_unpermute(du + dskip, bsz), dp, dd


def _pack_rows(arrays):
    rows = []
    for a in arrays:
        flat = a.reshape(-1).astype(F32)
        pad = (-flat.shape[0]) % PACK_COLS
        rows.append(jnp.pad(flat, (0, pad)).reshape(-1, PACK_COLS))
    out = jnp.concatenate(rows, axis=0)
    return jnp.pad(out, ((0, (-out.shape[0]) % 16), (0, 0)))


def _unpack_rows(packed, shapes):
    out, r = [], 0
    for s in shapes:
        size = int(np.prod(s))
        nr = -(-size // PACK_COLS)
        out.append(packed[r:r + nr].reshape(-1)[:size].reshape(s))
        r += nr
    return out


def kernel(x, mem, norm_w, mem_norm_w, ab_w_in, ab_w_out, hgrn_lb_logits, hgrn_out_norm_w, s5_lambda_re, s5_lambda_im, s5_log_dt, s5_b_re, s5_b_im, s5_c_re, s5_c_im, s5_d, s5_w_glu, xattn_wq, xattn_wkv, xattn_wo, ffn_w_in, ffn_w_out, loss_target, m_norm_w, m_mem_norm_w, m_ab_w_in, m_ab_w_out, m_hgrn_lb_logits, m_hgrn_out_norm_w, m_s5_lambda_re, m_s5_lambda_im, m_s5_log_dt, m_s5_b_re, m_s5_b_im, m_s5_c_re, m_s5_c_im, m_s5_d, m_s5_w_glu, m_xattn_wq, m_xattn_wkv, m_xattn_wo, m_ffn_w_in, m_ffn_w_out, v_norm_w, v_mem_norm_w, v_ab_w_in, v_ab_w_out, v_hgrn_lb_logits, v_hgrn_out_norm_w, v_s5_lambda_re, v_s5_lambda_im, v_s5_log_dt, v_s5_b_re, v_s5_b_im, v_s5_c_re, v_s5_c_im, v_s5_d, v_s5_w_glu, v_xattn_wq, v_xattn_wkv, v_xattn_wo, v_ffn_w_in, v_ffn_w_out):
    given = dict(locals())
    w = {n: given[n] for n in WEIGHTS}
    mom = {n: given["m_" + n] for n in WEIGHTS}
    var = {n: given["v_" + n] for n in WEIGHTS}
    bsz, length, _ = x.shape
    ntok = bsz * length
    chip = 2 * lax.axis_index("x") + lax.axis_index("y")

    big_axes = [ax for _, ax in BIG]
    full = _gather_chips("gather_weights", [w[n].astype(BF16) for n in BIG_NAMES], big_axes)
    wf = dict(zip(BIG_NAMES, full))
    small_block = jnp.concatenate([w['norm_w'].reshape(12, -1), w['s5_d'].reshape(1, -1), jnp.zeros((3, 256), F32)], axis=0)
    (small_full,) = _gather_chips("gather_norm_w", [small_block[None]], [1])
    nw = small_full[0, :12].reshape(2, 6, 1, D_MODEL)
    s5_d_full = small_full[0, 12:13]

    lb_table, lb_vjp = jax.vjp(lambda t: jnp.cumsum(jax.nn.softmax(t, axis=0), axis=0), w['hgrn_lb_logits'])
    xs = x.reshape(ntok, D_MODEL)
    mem2 = mem.reshape(bsz * MEM_LEN, D_MODEL)
    tgt = loss_target.reshape(ntok, D_MODEL)
    saved = []
    (h,) = _rowwise("norm_in", lambda a, g: _rms(a, g), [xs], [nw[0, 0]], out_rows=[(D_MODEL, BF16)])
    cur = xs
    for layer in range(2):
        sv = {"x": cur}
        if layer == 0:
            z = _mm("ab_in", h, wf['ab_w_in'][0]).reshape(bsz, length, -1)
            sv["h0"] = h
            rope_cos, rope_sin = _rope_tables(length)
            branch_cnt = _branch_count(length)
            oa = _hgrn_fwd("hgrn_f", z, lb_table[0:1], w['hgrn_out_norm_w'])
            qr, kr, vb = _rope_qkv("rope_qkv", z, rope_cos, rope_sin)
            ob, lse = _dilated_fwd("dilated_f", qr, kr, vb, branch_cnt)
            core = jnp.concatenate([oa, ob], axis=-1).reshape(ntok, D_MODEL)
            sv.update(z=z, qr=qr, kr=kr, vb=vb, lse=lse, core=core)
            y = _mm("ab_out", core, wf['ab_w_out'][0])
        else:
            s5p = [w[n][0] for n in ('s5_lambda_re', 's5_lambda_im', 's5_log_dt', 's5_b_re', 's5_b_im', 's5_c_re', 's5_c_im')]
            gl, sv["s5"] = _s5_fwd(h, s5p, s5_d_full, bsz)
            sv["gl"] = gl
            y, sv["zga"], sv["zgb"] = _mm_gated("s5_glu", gl, wf['s5_w_glu'][0], lambda a, b: a * jax.nn.sigmoid(b), F32)
        sv["y1"] = y
        x1, h2 = _rowwise(f"resnorm_a{layer}", lambda a, b, g1, g2: (lambda s: (s, _rms(s, g2)))(a + _rms(b, g1)),
                          [cur, y], [nw[layer, 1], nw[layer, 2]], out_rows=[(D_MODEL, F32), (D_MODEL, BF16)])
        sv["x1"], sv["h2"] = x1, h2
        (mem_n,) = _rowwise(f"mem_norm{layer}", lambda a, g: _rms(a, g), [mem2], [w['mem_norm_w'][layer][None]],
                            out_rows=[(D_MODEL, BF16)])
        sv["mem_n"] = mem_n
        q = _mm(f"xq{layer}", h2, wf['xattn_wq'][layer])
        kv = _mm(f"xkv{layer}", mem_n, wf['xattn_wkv'][layer])
        sv["q"], sv["kv"] = q, kv
        o = _xattn_fwd(f"xattn_f{layer}", q.reshape(bsz, length, D_MODEL), kv.reshape(bsz, MEM_LEN, 2 * D_MODEL))
        o = o.reshape(ntok, D_MODEL)
        sv["o"] = o
        y2 = _mm(f"xo{layer}", o, wf['xattn_wo'][layer])
        sv["y2"] = y2
        x2, h4 = _rowwise(f"resnorm_b{layer}", lambda a, b, g1, g2: (lambda s: (s, _rms(s, g2)))(a + _rms(b, g1)),
                          [x1, y2], [nw[layer, 3], nw[layer, 4]], out_rows=[(D_MODEL, F32), (D_MODEL, BF16)])
        sv["x2"], sv["h4"] = x2, h4
        act, sv["za"], sv["zb"] = _mm_gated(f"ffn_in{layer}", h4, wf['ffn_w_in'][layer], lambda a, b: _silu(a) * b, BF16)
        sv["act"] = act
        y3 = _mm(f"ffn_out{layer}", act, wf['ffn_w_out'][layer])
        sv["y3"] = y3
        saved.append(sv)
        if layer == 0:
            cur, h = _rowwise("resnorm_c0", lambda a, b, g1, g2: (lambda s: (s, _rms(s, g2)))(a + _rms(b, g1)),
                              [x2, y3], [nw[0, 5], nw[1, 0]], out_rows=[(D_MODEL, F32), (D_MODEL, F32)])
    g, sq = _rowwise("loss_head", lambda a, b, t, g1: (lambda e: (e * (1.0 / D_MODEL), jnp.sum(e * e, axis=0, keepdims=True)))(a + _rms(b, g1) - t),
                     [saved[1]["x2"], saved[1]["y3"], tgt], [nw[1, 5]], out_rows=[(D_MODEL, F32)], out_sums=[D_MODEL])
    loss = lax.psum(0.5 * jnp.sum(sq) / D_MODEL, ("x", "y", "c"))

    gbig = {}
    gnw = [[None] * 6 for _ in range(2)]
    gmemnw = [None, None]
    gsmall = {}
    for layer in (1, 0):
        sv = saved[layer]
        dy3, gnw[layer][5] = _rowwise(f"b_norm5_{layer}", lambda gg, yy, g1: _rms_bwd(yy, g1, gg), [g, sv["y3"]], [nw[layer, 5]],
                                      out_rows=[(D_MODEL, BF16)], out_sums=[D_MODEL])
        dact = _mm(f"b_ffn_out_dx{layer}", dy3, wf['ffn_w_out'][layer], tb=True)
        gw_out = _mm(f"b_ffn_out_dw{layer}", sv["act"], dy3, ta=True)

        def swiglu_bwd(a, b, da):
            a, b = a.astype(F32), b.astype(F32)
            sg = jax.nn.sigmoid(a)
            return jnp.concatenate([da * b * (sg * (1.0 + a * (1.0 - sg))), da * (a * sg)], axis=1)

        (dzf,) = _rowwise(f"b_swiglu{layer}", swiglu_bwd, [sv["za"], sv["zb"], dact], out_rows=[(2 * D_FF, BF16)], tile=256)
        dh4 = _mm(f"b_ffn_in_dx{layer}", dzf, wf['ffn_w_in'][layer], tb=True)
        gw_in = _mm(f"b_ffn_in_dw{layer}", sv["h4"], dzf, ta=True)
        gbig.setdefault('ffn_w_out', {})[layer] = gw_out
        gbig.setdefault('ffn_w_in', {})[layer] = gw_in

        def resnorm_bwd(gg, dh, xx, yy, g_in, g_res):
            dx, dw_in = _rms_bwd(xx, g_in, dh)
            tot = gg + dx
            dy, dw_res = _rms_bwd(yy, g_res, tot)
            return tot, dy, dw_in, dw_res

        g, dy2, gnw[layer][4], gnw[layer][3] = _rowwise(
            f"b_resnorm_b{layer}", resnorm_bwd, [g, dh4, sv["x2"], sv["y2"]], [nw[layer, 4], nw[layer, 3]],
            out_rows=[(D_MODEL, F32), (D_MODEL, BF16)], out_sums=[D_MODEL, D_MODEL])
        do = _mm(f"b_xo_dx{layer}", dy2, wf['xattn_wo'][layer], tb=True)
        gbig.setdefault('xattn_wo', {})[layer] = _mm(f"b_xo_dw{layer}", sv["o"], dy2, ta=True)
        dq, dk, dv = _xattn_bwd(f"xattn_b{layer}", sv["q"].reshape(bsz, length, D_MODEL),
                                sv["kv"].reshape(bsz, MEM_LEN, 2 * D_MODEL), do.reshape(bsz, length, D_MODEL))
        dq = dq.reshape(ntok, D_MODEL)
        dkv = jnp.concatenate([dk, dv], axis=-1).reshape(bsz * MEM_LEN, 2 * D_MODEL)
        dh2 = _mm(f"b_xq_dx{layer}", dq, wf['xattn_wq'][layer], tb=True)
        gbig.setdefault('xattn_wq', {})[layer] = _mm(f"b_xq_dw{layer}", sv["h2"], dq, ta=True)
        dmem_n = _mm(f"b_xkv_dx{layer}", dkv, wf['xattn_wkv'][layer], tb=True)
        gbig.setdefault('xattn_wkv', {})[layer] = _mm(f"b_xkv_dw{layer}", sv["mem_n"], dkv, ta=True)
        (gmemnw[layer],) = _rowwise(f"b_mem_norm{layer}", lambda dd, mm_, g1: _rms_bwd(mm_, g1, dd)[1], [dmem_n, mem2],
                                    [w['mem_norm_w'][layer][None]], out_sums=[D_MODEL])

        g, dy1, gnw[layer][2], gnw[layer][1] = _rowwise(
            f"b_resnorm_a{layer}", resnorm_bwd, [g, dh2, sv["x1"], sv["y1"]], [nw[layer, 2], nw[layer, 1]],
            out_rows=[(D_MODEL, F32), (D_MODEL, F32 if layer == 1 else BF16)], out_sums=[D_MODEL, D_MODEL])
        if layer == 1:
            def gate_bwd(a, b, dd):
                sg = jax.nn.sigmoid(b.astype(F32))
                return jnp.concatenate([dd * sg, dd * a.astype(F32) * sg * (1.0 - sg)], axis=1)

            (dzg,) = _rowwise("b_s5_gate", gate_bwd, [sv["zga"], sv["zgb"], dy1], out_rows=[(2 * D_MODEL, BF16)])
            dgl = _mm("b_s5_glu_dx", dzg, wf['s5_w_glu'][0], tb=True)
            gbig['s5_w_glu'] = {0: _mm("b_s5_glu_dw", sv["gl"], dzg, ta=True)}
            dh0, dp, gsmall['s5_d'] = _s5_bwd(dgl, sv["s5"], s5_d_full, bsz)
            for n, t in zip(('s5_lambda_re', 's5_lambda_im', 's5_log_dt', 's5_b_re', 's5_b_im', 's5_c_re', 's5_c_im'), dp):
                gsmall[n] = t[None]
            g, gnw[1][0] = _rowwise("b_norm_in1", lambda gg, dh, xx, g1: (lambda r: (gg + r[0], r[1]))(_rms_bwd(xx, g1, dh)),
                                    [g, dh0, sv["x"]], [nw[1, 0]], out_rows=[(D_MODEL, F32)], out_sums=[D_MODEL])
        else:
            dcore = _mm("b_ab_out_dx", dy1, wf['ab_w_out'][0], tb=True)
            gbig['ab_w_out'] = {0: _mm("b_ab_out_dw", sv["core"], dy1, ta=True)}
            dcore = dcore.reshape(bsz, length, D_MODEL)
            core3 = sv["core"].reshape(bsz, length, D_MODEL)
            dqa, dfa, dia, dga, dlb0, gsmall['hgrn_out_norm_w'] = _hgrn_bwd("hgrn_b", sv["z"], lb_table[0:1], w['hgrn_out_norm_w'], dcore)
            dqb, dkb, dvb = _dilated_bwd("dilated_b", sv["qr"], sv["kr"], sv["vb"], core3, sv["lse"], dcore, branch_cnt,
                                         rope_cos, rope_sin, off=A_WIDTH // B_HD)
            (gsmall['hgrn_lb_logits'],) = lb_vjp(jnp.zeros_like(lb_table).at[0].set(dlb0[0]))
            dz = jnp.concatenate([dqa, dfa, dia, dga, dqb, dkb, dvb], axis=-1).reshape(ntok, -1)
            dh0 = _mm("b_ab_in_dx", dz, wf['ab_w_in'][0], tb=True)
            gbig['ab_w_in'] = {0: _mm("b_ab_in_dw", sv["h0"], dz, ta=True)}
            grad_x, gnw[0][0] = _rowwise("b_norm_in0", lambda gg, dh, xx, g1: (lambda r: (gg + r[0], r[1]))(_rms_bwd(xx, g1, dh)),
                                         [g, dh0, sv["x"]], [nw[0, 0]], out_rows=[(D_MODEL, F32)], out_sums=[D_MODEL])
    gsmall['norm_w'] = jnp.stack([jnp.concatenate(gnw[l], axis=0) for l in range(2)])
    gsmall['mem_norm_w'] = jnp.concatenate(gmemnw, axis=0)

    packed = _pack_rows([gsmall[n] for n in SMALL])
    theirs = _sibling_swap("small_swap", packed)
    chip_sum = _add2("small_pair_sum", packed, theirs)
    (all_chips,) = _gather_chips("small_gather", [chip_sum[None]], [0])
    small_sum = _sum_slots("small_sum", all_chips.reshape(N_CHIPS, packed.shape[0], PACK_COLS))
    full_shapes = [(2, 6, D_MODEL) if n == 'norm_w' else (1, D_MODEL) if n == 's5_d' else w[n].shape for n in SMALL]
    gs = dict(zip(SMALL, _unpack_rows(small_sum, full_shapes)))
    for n in SHARDED_SMALL:
        gs[n] = lax.dynamic_slice_in_dim(gs[n], chip * 256, 256, axis=gs[n].ndim - 1)

    pos = _pos_vec()
    parts = [jnp.stack([gbig[n][l] for l in sorted(gbig[n])]) for n in BIG_NAMES]
    theirs = _pair_send("grad_pair_send", parts, big_axes)
    pair = [_pair_add("grad_pair_sum_" + n, a, b, ax, pos) for (n, ax), a, b in zip(BIG, parts, theirs)]
    slots = _chip_exchange("grad_chip_exchange", pair, big_axes)
    shards = [_chip_sum("grad_chip_sum_" + n, a, b, s, ax, pos) for (n, ax), a, b, s in zip(BIG, parts, theirs, slots)]
    gfull = dict(zip(BIG_NAMES, _pair_join("grad_pair_join", shards)))

    grads, deltas, new_m, new_v = {}, {}, {}, {}
    for n in BIG_NAMES:
        grads[n] = gfull[n]
        deltas[n], new_m[n], new_v[n] = _adam("adam_" + n, w[n], gfull[n], mom[n], var[n])
    pk = [_pack_rows([t[n] for n in SMALL]) for t in (w, gs, mom, var)]
    small_out = _adam("adam_small", *pk)
    shard_shapes = [w[n].shape for n in SMALL]
    for dst, packed_out in zip((deltas, new_m, new_v), small_out):
        dst.update(zip(SMALL, _unpack_rows(packed_out, shard_shapes)))
    grads.update(gs)
    return (loss, grad_x.reshape(x.shape), *[grads[n] for n in WEIGHTS], *[deltas[n] for n in WEIGHTS],
            *[new_m[n] for n in WEIGHTS], *[new_v[n] for n in WEIGHTS])
```

```python
import functools
import math

import numpy as np
import jax
import jax.numpy as jnp
from jax import lax
from jax.experimental import pallas as pl
from jax.experimental.pallas import tpu as pltpu

F32 = jnp.float32
BF16 = jnp.bfloat16
MXU_DTYPE = jnp.bfloat16

D_MODEL = 1024
NORM_EPS = 1e-6
A_HEADS, A_DK, A_CHUNK = 4, 128, 32
A_WIDTH = A_HEADS * A_DK
B_HEADS, B_HD = 4, 128
B_WIDTH = B_HEADS * B_HD
B_DILATIONS = ((128, 1), (512, 4), (2048, 16))
ROPE_THETA = 10000.0
C_GROUP, C_GROUPS, C_STATE, C_CHUNK = 16, 64, 64, 128
C_MIN_NEG_RE = -1e-4
MEM_LEN = 256
X_HEADS = 4
X_HD = D_MODEL // X_HEADS
D_FF = 2816
ADAM_LR, ADAM_B1, ADAM_B2, ADAM_EPS, ADAM_WD, ADAM_STEP = 0.001, 0.9, 0.999, 1e-08, 0.01, 10

N_CHIPS = 4
MESH = pl.DeviceIdType.MESH
ANY = pl.BlockSpec(memory_space=pl.ANY)
_RELS = ((1, 0), (0, 1), (1, 1))

WEIGHTS = ['norm_w', 'mem_norm_w', 'ab_w_in', 'ab_w_out', 'hgrn_lb_logits', 'hgrn_out_norm_w', 's5_lambda_re',
           's5_lambda_im', 's5_log_dt', 's5_b_re', 's5_b_im', 's5_c_re', 's5_c_im', 's5_d', 's5_w_glu', 'xattn_wq',
           'xattn_wkv', 'xattn_wo', 'ffn_w_in', 'ffn_w_out']
BIG = (('ab_w_in', 1), ('ab_w_out', 0), ('s5_w_glu', 1), ('xattn_wq', 0), ('xattn_wkv', 1), ('xattn_wo', 0),
       ('ffn_w_in', 1), ('ffn_w_out', 0))
BIG_NAMES = tuple(n for n, _ in BIG)
SMALL = tuple(n for n in WEIGHTS if n not in BIG_NAMES)
SHARDED_SMALL = ('norm_w', 's5_d')
PACK_COLS = 1024


def _pos():
    return lax.axis_index("x"), lax.axis_index("y"), lax.axis_index("c")


def _flip(v, d):
    return 1 - v if d else v


def _divisor(n, want):
    for t in (want, 1024, 512, 256, 128, 64, 32, 16, 8):
        if t <= want and n % t == 0:
            return t
    return n


def _rowwise(name, fn, rows, bcasts=(), out_rows=(), out_sums=(), tile=512):
    n = rows[0].shape[0]
    t = _divisor(n, tile)
    nr, nb, no, ns = len(rows), len(bcasts), len(out_rows), len(out_sums)

    def body(*refs):
        vals = [r[...] for r in refs[:nr + nb]]
        res = fn(*vals)
        if not isinstance(res, (tuple, list)):
            res = (res,)
        outs = refs[nr + nb:]
        for k in range(no):
            outs[k][...] = res[k].astype(outs[k].dtype)
        if ns:
            first = pl.program_id(0) == 0
            for k in range(ns):
                o, val = outs[no + k], res[no + k]

                @pl.when(first)
                def _():
                    o[...] = val

                @pl.when(jnp.logical_not(first))
                def _():
                    o[...] += val

    in_specs = [pl.BlockSpec((t, r.shape[1]), lambda i: (i, 0)) for r in rows]
    in_specs += [pl.BlockSpec(b.shape, lambda i: (0, 0)) for b in bcasts]
    out_specs = [pl.BlockSpec((t, c), lambda i: (i, 0)) for c, _ in out_rows]
    out_specs += [pl.BlockSpec((1, c), lambda i: (0, 0)) for c in out_sums]
    out_shape = [jax.ShapeDtypeStruct((n, c), dt) for c, dt in out_rows]
    out_shape += [jax.ShapeDtypeStruct((1, c), F32) for c in out_sums]
    res = pl.pallas_call(
        body, name=name, grid=(n // t,), in_specs=in_specs, out_specs=out_specs, out_shape=out_shape,
        compiler_params=pltpu.CompilerParams(dimension_semantics=("arbitrary",)),
    )(*rows, *bcasts)
    return res


def _rms(x, w):
    r = lax.rsqrt(jnp.mean(x * x, axis=-1, keepdims=True) + NORM_EPS)
    return x * r * w


def _rms_bwd(x, w, dy):
    r = lax.rsqrt(jnp.mean(x * x, axis=-1, keepdims=True) + NORM_EPS)
    xh = x * r
    dxh = dy * w
    dx = r * (dxh - xh * jnp.mean(dxh * xh, axis=-1, keepdims=True))
    return dx, jnp.sum(dy * xh, axis=0, keepdims=True)


def _silu(z):
    return z * jax.nn.sigmoid(z)


MM_VMEM_BUDGET = 44 * 1024 * 1024


def _mm_tiles(m, n, k, ta, abytes, bbytes, obytes):
    tn = next(t for t in (1792, 1408, 1024, 512, 256, 128) if n % t == 0) if ta else _divisor(n, 512)
    tk = _divisor(k, 1024) if ta else (k if k <= 2816 else next(t for t in (2816, 2048, 1792, 1024, 512) if k % t == 0))
    for tm in (2816, 2048, 1408, 1024, 512, 256, 128):
        if m % tm:
            continue
        need = 2 * (tm * tk * abytes + tk * tn * bbytes + tm * tn * obytes) + 2 * tm * tn * 4
        if need <= MM_VMEM_BUDGET:
            return tm, tn, tk
    return _divisor(m, 128), tn, tk


def _mm(name, a, b, ta=False, tb=False, out_dtype=F32):
    m, k = a.shape[::-1] if ta else a.shape
    k2, n = b.shape[::-1] if tb else b.shape
    assert k == k2, (name, a.shape, b.shape)
    tm, tn, tk = _mm_tiles(m, n, k, ta, a.dtype.itemsize, b.dtype.itemsize, jnp.dtype(out_dtype).itemsize)
    nk = k // tk
    dims = (((0 if ta else 1,), (1 if tb else 0,)), ((), ()))

    def prod(a_ref, b_ref):
        return lax.dot_general(a_ref[...].astype(MXU_DTYPE), b_ref[...].astype(MXU_DTYPE), dims,
                               preferred_element_type=F32)

    def body_one(a_ref, b_ref, o_ref):
        o_ref[...] = prod(a_ref, b_ref).astype(o_ref.dtype)

    def body_acc(a_ref, b_ref, o_ref, acc):
        kk = pl.program_id(2)

        @pl.when(kk == 0)
        def _():
            acc[...] = prod(a_ref, b_ref)

        @pl.when(kk > 0)
        def _():
            acc[...] += prod(a_ref, b_ref)

        @pl.when(kk == nk - 1)
        def _():
            o_ref[...] = acc[...].astype(o_ref.dtype)

    a_spec = pl.BlockSpec((tk, tm), lambda i, j, kk: (kk, i)) if ta else pl.BlockSpec((tm, tk), lambda i, j, kk: (i, kk))
    b_spec = pl.BlockSpec((tn, tk), lambda i, j, kk: (j, kk)) if tb else pl.BlockSpec((tk, tn), lambda i, j, kk: (kk, j))
    return pl.pallas_call(
        body_one if nk == 1 else body_acc, name=name, grid=(m // tm, n // tn, nk),
        in_specs=[a_spec, b_spec], out_specs=pl.BlockSpec((tm, tn), lambda i, j, kk: (i, j)),
        out_shape=jax.ShapeDtypeStruct((m, n), out_dtype),
        scratch_shapes=[] if nk == 1 else [pltpu.VMEM((tm, tn), F32)],
        compiler_params=pltpu.CompilerParams(dimension_semantics=("parallel", "parallel", "arbitrary")),
    )(a, b)


def _mm_gated(name, h, w, gate, out_dtype, tm=2048, tn=256):
    n, k = h.shape
    f = w.shape[1] // 2
    tm, nj = _divisor(n, tm), f // tn

    def body(h_ref, wa_ref, wb_ref, act_ref, za_ref, zb_ref):
        hv = h_ref[...].astype(MXU_DTYPE)
        za = jnp.dot(hv, wa_ref[...].astype(MXU_DTYPE), preferred_element_type=F32)
        zb = jnp.dot(hv, wb_ref[...].astype(MXU_DTYPE), preferred_element_type=F32)
        act_ref[...] = gate(za, zb).astype(act_ref.dtype)
        za_ref[...] = za.astype(za_ref.dtype)
        zb_ref[...] = zb.astype(zb_ref.dtype)

    out = pl.BlockSpec((tm, tn), lambda i, j: (i, j))
    return pl.pallas_call(
        body, name=name, grid=(n // tm, nj),
        in_specs=[pl.BlockSpec((tm, k), lambda i, j: (i, 0)), pl.BlockSpec((k, tn), lambda i, j: (0, j)),
                  pl.BlockSpec((k, tn), lambda i, j: (0, j + nj))],
        out_specs=[out] * 3,
        out_shape=[jax.ShapeDtypeStruct((n, f), out_dtype), jax.ShapeDtypeStruct((n, f), BF16), jax.ShapeDtypeStruct((n, f), BF16)],
        compiler_params=pltpu.CompilerParams(dimension_semantics=("parallel", "parallel")),
    )(h, w, w)


def _xattn_fwd(name, q, kv, tq=512):
    bsz, length, _ = q.shape
    tq = _divisor(length, tq)
    scale = X_HD ** -0.5

    def body(q_ref, k_ref, v_ref, o_ref):
        qv, kk, vv = q_ref[...].astype(MXU_DTYPE), k_ref[...].astype(MXU_DTYPE), v_ref[...].astype(MXU_DTYPE)
        s = lax.dot_general(qv, kk, (((1,), (1,)), ((), ())), preferred_element_type=F32) * scale
        p = jnp.exp(s - jnp.max(s, axis=-1, keepdims=True))
        p = p / jnp.sum(p, axis=-1, keepdims=True)
        o_ref[...] = jnp.dot(p.astype(MXU_DTYPE), vv, preferred_element_type=F32).astype(o_ref.dtype)

    return pl.pallas_call(
        body, name=name, grid=(bsz, X_HEADS, length // tq),
        in_specs=[pl.BlockSpec((None, tq, X_HD), lambda b, h, i: (b, i, h)),
                  pl.BlockSpec((None, MEM_LEN, X_HD), lambda b, h, i: (b, 0, h)),
                  pl.BlockSpec((None, MEM_LEN, X_HD), lambda b, h, i: (b, 0, X_HEADS + h))],
        out_specs=pl.BlockSpec((None, tq, X_HD), lambda b, h, i: (b, i, h)),
        out_shape=jax.ShapeDtypeStruct(q.shape, BF16),
        compiler_params=pltpu.CompilerParams(dimension_semantics=("parallel", "parallel", "arbitrary")),
    )(q, kv, kv)


def _xattn_bwd(name, q, kv, do, tq=512):
    bsz, length, _ = q.shape
    tq = _divisor(length, tq)
    scale = X_HD ** -0.5

    def body(q_ref, k_ref, v_ref, do_ref, dq_ref, dk_ref, dv_ref):
        qv, kk, vv = q_ref[...].astype(MXU_DTYPE), k_ref[...].astype(MXU_DTYPE), v_ref[...].astype(MXU_DTYPE)
        dov = do_ref[...].astype(MXU_DTYPE)
        s = lax.dot_general(qv, kk, (((1,), (1,)), ((), ())), preferred_element_type=F32) * scale
        p = jnp.exp(s - jnp.max(s, axis=-1, keepdims=True))
        p = p / jnp.sum(p, axis=-1, keepdims=True)
        dp = lax.dot_general(dov, vv, (((1,), (1,)), ((), ())), preferred_element_type=F32)
        ds = p * (dp - jnp.sum(dp * p, axis=-1, keepdims=True)) * scale
        dsb = ds.astype(MXU_DTYPE)
        dq_ref[...] = jnp.dot(dsb, kk, preferred_element_type=F32).astype(dq_ref.dtype)
        dk = lax.dot_general(dsb, qv, (((0,), (0,)), ((), ())), preferred_element_type=F32)
        dv = lax.dot_general(p.astype(MXU_DTYPE), dov, (((0,), (0,)), ((), ())), preferred_element_type=F32)
        first = pl.program_id(2) == 0

        @pl.when(first)
        def _():
            dk_ref[...] = dk
            dv_ref[...] = dv

        @pl.when(jnp.logical_not(first))
        def _():
            dk_ref[...] += dk
            dv_ref[...] += dv

    qspec = pl.BlockSpec((None, tq, X_HD), lambda b, h, i: (b, i, h))
    kspec = pl.BlockSpec((None, MEM_LEN, X_HD), lambda b, h, i: (b, 0, h))
    return pl.pallas_call(
        body, name=name, grid=(bsz, X_HEADS, length // tq),
        in_specs=[qspec, kspec, pl.BlockSpec((None, MEM_LEN, X_HD), lambda b, h, i: (b, 0, X_HEADS + h)), qspec],
        out_specs=[qspec, kspec, kspec],
        out_shape=[jax.ShapeDtypeStruct(q.shape, BF16), jax.ShapeDtypeStruct((bsz, MEM_LEN, D_MODEL), F32),
                   jax.ShapeDtypeStruct((bsz, MEM_LEN, D_MODEL), F32)],
        compiler_params=pltpu.CompilerParams(dimension_semantics=("parallel", "parallel", "arbitrary")),
    )(q, kv, kv, do)


def _dma_sems(*counts):
    return [pltpu.SemaphoreType.DMA((max(c, 1),)) for c in counts]


def _gather_chips(name, blocks, axes):
    n = len(blocks)
    shapes = [b.shape for b in blocks]

    def body(*refs):
        ins, outs = refs[:n], refs[n:2 * n]
        lsem, lrsem, ssem, rsem, fssem, frsem = refs[2 * n:]
        x, y, c = _pos()
        me = 2 * x + y

        def region(a, chip, h):
            _, r, cc = shapes[a]
            hr = r // 2
            if axes[a] == 0:
                return outs[a].at[:, pl.ds(chip * r + h * hr, hr), :]
            return outs[a].at[:, pl.ds(h * hr, hr), pl.ds(chip * cc, cc)]

        def whole(a, chip):
            _, r, cc = shapes[a]
            if axes[a] == 0:
                return outs[a].at[:, pl.ds(chip * r, r), :]
            return outs[a].at[:, :, pl.ds(chip * cc, cc)]

        sends = []
        for a in range(n):
            cp = pltpu.make_async_remote_copy(src_ref=ins[a], dst_ref=whole(a, me), send_sem=lsem.at[a], recv_sem=lrsem.at[a],
                                              device_id=(x, y, 1 - c), device_id_type=MESH)
            cp.start()
            sends.append(cp)
        for a in range(n):
            hr = shapes[a][1] // 2
            for k, (dx, dy) in enumerate(_RELS):
                cp = pltpu.make_async_remote_copy(
                    src_ref=ins[a].at[:, pl.ds(c * hr, hr), :], dst_ref=region(a, me, c),
                    send_sem=ssem.at[3 * a + k], recv_sem=rsem.at[3 * a + k],
                    device_id=(_flip(x, dx), _flip(y, dy), c), device_id_type=MESH)
                cp.start()
                sends.append(cp)
        for a in range(n):
            for k, (dx, dy) in enumerate(_RELS):
                px, py = _flip(x, dx), _flip(y, dy)
                got = region(a, 2 * px + py, c)
                pltpu.make_async_remote_copy(
                    src_ref=got, dst_ref=got, send_sem=ssem.at[3 * a + k], recv_sem=rsem.at[3 * a + k],
                    device_id=(px, py, c), device_id_type=MESH).wait_recv()
                cp = pltpu.make_async_remote_copy(
                    src_ref=got, dst_ref=got, send_sem=fssem.at[3 * a + k], recv_sem=frsem.at[3 * a + k],
                    device_id=(x, y, 1 - c), device_id_type=MESH)
                cp.start()
                sends.append(cp)
        for a in range(n):
            for k, (dx, dy) in enumerate(_RELS):
                got = region(a, 2 * _flip(x, dx) + _flip(y, dy), 1 - c)
                pltpu.make_async_remote_copy(
                    src_ref=got, dst_ref=got, send_sem=fssem.at[3 * a + k], recv_sem=frsem.at[3 * a + k],
                    device_id=(x, y, 1 - c), device_id_type=MESH).wait_recv()
        for a in range(n):
            pltpu.make_async_remote_copy(src_ref=ins[a], dst_ref=whole(a, me), send_sem=lsem.at[a], recv_sem=lrsem.at[a],
                                         device_id=(x, y, 1 - c), device_id_type=MESH).wait_recv()
        for cp in sends:
            cp.wait_send()

    out_shape = [jax.ShapeDtypeStruct((l, 4 * r, c) if ax == 0 else (l, r, 4 * c), b.dtype)
                 for (l, r, c), ax, b in zip(shapes, axes, blocks)]
    return pl.pallas_call(
        body, name=name, in_specs=[ANY] * n, out_specs=[ANY] * n, out_shape=out_shape,
        scratch_shapes=_dma_sems(n, n, 3 * n, 3 * n, 3 * n, 3 * n),
    )(*blocks)


def _pos_vec():
    x, y, c = _pos()
    return jnp.stack([c, 2 * x + y]).astype(jnp.int32)


def _pair_send(name, parts, axes):
    n = len(parts)
    shapes = [p.shape for p in parts]
    ncopy = sum(4 if ax == 0 else 1 for ax in axes)

    def body(*refs):
        ins, theirs = refs[:n], refs[n:2 * n]
        ssem, rsem = refs[2 * n:]
        x, y, c = _pos()
        pending, j = [], 0
        for a in range(n):
            _, rf, _ = shapes[a]
            if axes[a] == 0:
                hr = rf // 8
                pieces = [(ins[a].at[:, pl.ds((2 * s + 1 - c) * hr, hr), :], theirs[a].at[:, s]) for s in range(N_CHIPS)]
            else:
                hr = rf // 2
                pieces = [(ins[a].at[:, pl.ds((1 - c) * hr, hr), :], theirs[a])]
            for give, give_dst in pieces:
                rc = pltpu.make_async_remote_copy(src_ref=give, dst_ref=give_dst, send_sem=ssem.at[j],
                                                  recv_sem=rsem.at[j], device_id=(x, y, 1 - c), device_id_type=MESH)
                rc.start()
                pending.append(rc)
                j += 1
        for cp in pending:
            cp.wait()

    def half_shape(s, ax):
        return (s[0], N_CHIPS, s[1] // 8, s[2]) if ax == 0 else (s[0], s[1] // 2, s[2])

    out_shape = [jax.ShapeDtypeStruct(half_shape(s, ax), p.dtype) for s, ax, p in zip(shapes, axes, parts)]
    return pl.pallas_call(
        body, name=name, in_specs=[ANY] * n, out_specs=[ANY] * n, out_shape=out_shape,
        scratch_shapes=_dma_sems(ncopy, ncopy),
    )(*parts)


def _chip_exchange(name, halves, axes):
    n = len(halves)
    shapes = [h.shape for h in halves]

    def body(*refs):
        ins, outs = refs[:n], refs[n:2 * n]
        ssem, rsem = refs[2 * n:]
        x, y, c = _pos()

        def part(a, chip):
            if axes[a] == 0:
                return ins[a].at[:, chip]
            cc = shapes[a][2] // N_CHIPS
            return ins[a].at[:, :, pl.ds(chip * cc, cc)]

        sends = []
        for a in range(n):
            for k, (dx, dy) in enumerate(_RELS):
                px, py = _flip(x, dx), _flip(y, dy)
                rc = pltpu.make_async_remote_copy(
                    src_ref=part(a, 2 * px + py), dst_ref=outs[a].at[:, k], send_sem=ssem.at[3 * a + k],
                    recv_sem=rsem.at[3 * a + k], device_id=(px, py, c), device_id_type=MESH)
                rc.start()
                sends.append(rc)
        for cp in sends:
            cp.wait()

    def slot_shape(s, ax):
        return (s[0], 3, s[2], s[3]) if ax == 0 else (s[0], 3, s[1], s[2] // N_CHIPS)

    out_shape = [jax.ShapeDtypeStruct(slot_shape(s, ax), h.dtype) for s, ax, h in zip(shapes, axes, halves)]
    return pl.pallas_call(
        body, name=name, in_specs=[ANY] * n, out_specs=[ANY] * n, out_shape=out_shape,
        scratch_shapes=_dma_sems(3 * n, 3 * n),
    )(*halves)


def _pair_join(name, shards):
    n = len(shards)

    def body(*refs):
        outs = refs[n:2 * n]
        ssem, rsem = refs[2 * n:]
        x, y, c = _pos()
        pending = []
        for a in range(n):
            hr = shards[a].shape[1] // 2
            mine = outs[a].at[:, pl.ds(c * hr, hr), :]
            rc = pltpu.make_async_remote_copy(src_ref=mine, dst_ref=mine, send_sem=ssem.at[a], recv_sem=rsem.at[a],
                                              device_id=(x, y, 1 - c), device_id_type=MESH)
            rc.start()
            pending.append(rc)
        for a in range(n):
            hr = shards[a].shape[1] // 2
            got = outs[a].at[:, pl.ds((1 - c) * hr, hr), :]
            pltpu.make_async_remote_copy(src_ref=got, dst_ref=got, send_sem=ssem.at[a], recv_sem=rsem.at[a],
                                         device_id=(x, y, 1 - c), device_id_type=MESH).wait_recv()
        for cp in pending:
            cp.wait_send()

    return pl.pallas_call(
        body, name=name, in_specs=[ANY] * n, out_specs=[ANY] * n,
        out_shape=[jax.ShapeDtypeStruct(s.shape, s.dtype) for s in shards],
        input_output_aliases={a: a for a in range(n)}, scratch_shapes=_dma_sems(n, n),
    )(*shards)


def _pair_add(name, part, theirs, axis, pos):
    layers, rf, cf = part.shape

    def body(pos_ref, a_ref, b_ref, o_ref):
        o_ref[...] = (a_ref[...] + b_ref[...]).astype(o_ref.dtype)

    if axis == 0:
        hr = rf // 8
        grid = (layers, N_CHIPS)
        in_specs = [pl.BlockSpec((None, hr, cf), lambda l, s, p: (l, 2 * s + p[0], 0)),
                    pl.BlockSpec((None, None, hr, cf), lambda l, s, p: (l, s, 0, 0))]
        out_spec = pl.BlockSpec((None, None, hr, cf), lambda l, s, p: (l, s, 0, 0))
    else:
        hr, t = rf // 2, 128
        grid = (layers, hr // t)
        in_specs = [pl.BlockSpec((None, t, cf), lambda l, i, p: (l, p[0] * (hr // t) + i, 0)),
                    pl.BlockSpec((None, t, cf), lambda l, i, p: (l, i, 0))]
        out_spec = pl.BlockSpec((None, t, cf), lambda l, i, p: (l, i, 0))
    return pl.pallas_call(
        body, name=name, out_shape=jax.ShapeDtypeStruct(theirs.shape, BF16),
        grid_spec=pltpu.PrefetchScalarGridSpec(num_scalar_prefetch=1, grid=grid, in_specs=in_specs, out_specs=out_spec),
        compiler_params=pltpu.CompilerParams(dimension_semantics=("arbitrary", "arbitrary")),
    )(pos, part, theirs)


def _chip_sum(name, part, theirs, slots, axis, pos):
    layers, _, hr, c = slots.shape

    def body(pos_ref, mine, sib, s0, s1, s2, o_ref):
        o_ref[...] = (((mine[...] + sib[...]) + s0[...].astype(F32)) + s1[...].astype(F32)) + s2[...].astype(F32)

    t = hr if axis == 0 else 128
    if axis == 0:
        own_specs = [pl.BlockSpec((None, t, c), lambda l, i, p: (l, 2 * p[1] + p[0], 0)),
                     pl.BlockSpec((None, None, t, c), lambda l, i, p: (l, p[1], 0, 0))]
    else:
        own_specs = [pl.BlockSpec((None, t, c), lambda l, i, p: (l, p[0] * (hr // t) + i, p[1])),
                     pl.BlockSpec((None, t, c), lambda l, i, p: (l, i, p[1]))]
    slot_specs = [pl.BlockSpec((None, None, t, c), functools.partial(lambda k, l, i, p: (l, k, i, 0), k)) for k in range(3)]
    return pl.pallas_call(
        body, name=name, out_shape=jax.ShapeDtypeStruct((layers, 2 * hr, c), F32),
        grid_spec=pltpu.PrefetchScalarGridSpec(
            num_scalar_prefetch=1, grid=(layers, hr // t), in_specs=own_specs + slot_specs,
            out_specs=pl.BlockSpec((None, t, c), lambda l, i, p: (l, p[0] * (hr // t) + i, 0))),
        compiler_params=pltpu.CompilerParams(dimension_semantics=("arbitrary", "arbitrary")),
    )(pos, part, theirs, slots, slots, slots)


def _sibling_swap(name, v):
    def body(v_ref, o_ref, ssem, rsem):
        x, y, c = _pos()
        cp = pltpu.make_async_remote_copy(src_ref=v_ref, dst_ref=o_ref, send_sem=ssem.at[0], recv_sem=rsem.at[0],
                                          device_id=(x, y, 1 - c), device_id_type=MESH)
        cp.start()
        cp.wait()

    return pl.pallas_call(body, name=name, in_specs=[ANY], out_specs=ANY, out_shape=jax.ShapeDtypeStruct(v.shape, v.dtype),
                          scratch_shapes=_dma_sems(1, 1))(v)


def _add2(name, a, b):
    shape = a.shape
    a2, b2 = a.reshape(-1, shape[-1]), b.reshape(-1, shape[-1])
    (o,) = _rowwise(name, lambda u, v: u + v, [a2, b2], out_rows=[(shape[-1], F32)], tile=512)
    return o.reshape(shape)


def _sum_slots(name, slots):
    _, hr, c = slots.shape
    t = _divisor(hr, 256)

    def body(s0, s1, s2, s3, o_ref):
        o_ref[...] = ((s0[...] + s1[...]) + s2[...]) + s3[...]

    return pl.pallas_call(
        body, name=name, grid=(hr // t,),
        in_specs=[pl.BlockSpec((None, t, c), functools.partial(lambda k, i: (k, i, 0), k)) for k in range(N_CHIPS)],
        out_specs=pl.BlockSpec((t, c), lambda i: (i, 0)), out_shape=jax.ShapeDtypeStruct((hr, c), F32),
        compiler_params=pltpu.CompilerParams(dimension_semantics=("arbitrary",)),
    )(slots, slots, slots, slots)


def _adam_tile(w, g, m, v):
    m = ADAM_B1 * m + (1.0 - ADAM_B1) * g
    v = ADAM_B2 * v + (1.0 - ADAM_B2) * (g * g)
    m_hat = m / (1.0 - ADAM_B1 ** ADAM_STEP)
    v_hat = v / (1.0 - ADAM_B2 ** ADAM_STEP)
    delta = -ADAM_LR * (m_hat / (jnp.sqrt(v_hat) + ADAM_EPS) + ADAM_WD * w)
    return delta, m, v


def _adam(name, w, g, m, v):
    shape = w.shape
    c = shape[-1]
    flat = [t.reshape(-1, c) for t in (w, g, m, v)]
    res = _rowwise(name, _adam_tile, flat, out_rows=[(c, F32)] * 3, tile=256)
    return [r.reshape(shape) for r in res]


ATT_T = 256
ATT_NEG = -1e30


def _branch_count(length):
    nblk = length // ATT_T
    d = (np.arange(nblk)[:, None, None] * ATT_T + np.arange(ATT_T)[None, :, None] - np.arange(ATT_T)[None, None, :])
    cnt = np.zeros(d.shape, np.float32)
    for window, dil in B_DILATIONS:
        cnt += ((d >= 0) & (d % dil == 0) & (d <= window)).astype(np.float32)
    return jnp.asarray(cnt)


def _rope_tables(length):
    half = B_HD // 2
    inv_freq = ROPE_THETA ** (-jnp.arange(half, dtype=F32) / half)
    ang = jnp.arange(length, dtype=F32)[:, None] * inv_freq[None, :]
    cos, sin = jnp.cos(ang), jnp.sin(ang)
    return jnp.concatenate([cos, cos], axis=1), jnp.concatenate([-sin, sin], axis=1)


def _swap_halves(t):
    return pltpu.roll(t, B_HD // 2, 1)


def _rope_qkv(name, z, cos, sin, t=256):
    bsz, length, _ = z.shape
    t = _divisor(length, t)

    def body(q_ref, k_ref, v_ref, c_ref, s_ref, qo, ko, vo):
        c, s = c_ref[...], s_ref[...]
        for src, dst in ((q_ref, qo), (k_ref, ko)):
            for h in range(B_HEADS):
                cols = slice(h * B_HD, (h + 1) * B_HD)
                xh = src[:, cols]
                dst[:, cols] = (xh * c + _swap_halves(xh) * s).astype(dst.dtype)
        vo[...] = v_ref[...].astype(vo.dtype)

    col0 = 4 * A_WIDTH // B_WIDTH
    specs = [pl.BlockSpec((None, t, B_WIDTH), functools.partial(lambda k, b, i: (b, i, col0 + k), k)) for k in range(3)]
    tab = pl.BlockSpec((t, B_HD), lambda b, i: (i, 0))
    out = pl.BlockSpec((None, t, B_WIDTH), lambda b, i: (b, i, 0))
    return pl.pallas_call(
        body, name=name, grid=(bsz, length // t), in_specs=specs + [tab, tab], out_specs=[out] * 3,
        out_shape=[jax.ShapeDtypeStruct((bsz, length, B_WIDTH), BF16)] * 3,
        compiler_params=pltpu.CompilerParams(dimension_semantics=("parallel", "parallel")),
    )(z, z, z, cos, sin)


def _dilated_fwd(name, q, k, v, cnt):
    bsz, length, _ = q.shape
    scale = B_HD ** -0.5
    nblk = length // ATT_T

    def body(cnt_ref, q_ref, k_ref, v_ref, o_ref, lse_ref):
        i = pl.program_id(2)
        qb = q_ref[...]

        def step(j, carry):
            m, l, acc = carry
            rows = pl.ds(pl.multiple_of(j * ATT_T, ATT_T), ATT_T)
            s = lax.dot_general(qb, k_ref[rows, :], (((1,), (1,)), ((), ())), preferred_element_type=F32) * scale
            c = cnt_ref[i - j]
            s = jnp.where(c > 0.0, s, ATT_NEG)
            m_new = jnp.maximum(m, jnp.max(s, axis=-1, keepdims=True))
            a = jnp.exp(m - m_new)
            p = c * jnp.exp(s - m_new)
            l = a * l + jnp.sum(p, axis=-1, keepdims=True)
            acc = a * acc + jnp.dot(p.astype(MXU_DTYPE), v_ref[rows, :], preferred_element_type=F32)
            return m_new, l, acc

        init = (jnp.full((ATT_T, 1), ATT_NEG, F32), jnp.zeros((ATT_T, 1), F32), jnp.zeros((ATT_T, B_HD), F32))
        m, l, acc = lax.fori_loop(0, i + 1, step, init)
        o_ref[...] = acc / l
        lse_ref[...] = jnp.broadcast_to(m + jnp.log(l), (ATT_T, B_HD))

    qspec = pl.BlockSpec((None, ATT_T, B_HD), lambda b, h, i: (b, i, h))
    kspec = pl.BlockSpec((None, length, B_HD), lambda b, h, i: (b, 0, h))
    return pl.pallas_call(
        body, name=name, grid=(bsz, B_HEADS, nblk),
        in_specs=[pl.BlockSpec(cnt.shape, lambda b, h, i: (0, 0, 0)), qspec, kspec, kspec],
        out_specs=[qspec, pl.BlockSpec((None, None, ATT_T, B_HD), lambda b, h, i: (b, h, i, 0))],
        out_shape=[jax.ShapeDtypeStruct((bsz, length, B_WIDTH), F32), jax.ShapeDtypeStruct((bsz, B_HEADS, length, B_HD), F32)],
        compiler_params=pltpu.CompilerParams(dimension_semantics=("parallel", "parallel", "arbitrary")),
    )(cnt, q, k, v)


def _dilated_bwd(name, q, k, v, o, lse, do, cnt, cos, sin, off=0):
    bsz, length, _ = q.shape
    scale = B_HD ** -0.5
    nblk = length // ATT_T

    def body(cnt_ref, q_ref, k_ref, v_ref, o_ref, lse_ref, do_ref, c_ref, s_ref, dq_ref, dk_ref, dv_ref, dq_acc, dk_acc, dv_acc):
        dk_acc[...] = jnp.zeros_like(dk_acc)
        dv_acc[...] = jnp.zeros_like(dv_acc)

        def outer(i, _):
            rq = pl.ds(pl.multiple_of(i * ATT_T, ATT_T), ATT_T)
            qi, doi = q_ref[rq, :], do_ref[rq, :]
            lsei = lse_ref[rq, :][:, 0:1]
            di = jnp.sum(doi * o_ref[rq, :], axis=-1, keepdims=True)
            dob = doi.astype(MXU_DTYPE)

            def inner(j, dq):
                rk = pl.ds(pl.multiple_of(j * ATT_T, ATT_T), ATT_T)
                kj, vj = k_ref[rk, :], v_ref[rk, :]
                s = lax.dot_general(qi, kj, (((1,), (1,)), ((), ())), preferred_element_type=F32) * scale
                c = cnt_ref[i - j]
                p = c * jnp.exp(jnp.where(c > 0.0, s, ATT_NEG) - lsei)
                dp = lax.dot_general(dob, vj, (((1,), (1,)), ((), ())), preferred_element_type=F32)
                ds = (p * (dp - di) * scale).astype(MXU_DTYPE)
                dk_acc[rk, :] += lax.dot_general(ds, qi, (((0,), (0,)), ((), ())), preferred_element_type=F32)
                dv_acc[rk, :] += lax.dot_general(p.astype(MXU_DTYPE), dob, (((0,), (0,)), ((), ())), preferred_element_type=F32)
                return dq + jnp.dot(ds, kj, preferred_element_type=F32)

            dq_acc[rq, :] = lax.fori_loop(0, i + 1, inner, jnp.zeros((ATT_T, B_HD), F32))
            return 0

        lax.fori_loop(0, nblk, outer, 0)
        c, s = c_ref[...], s_ref[...]
        for acc, dst in ((dq_acc, dq_ref), (dk_acc, dk_ref)):
            g = acc[...]
            dst[...] = (g * c + _swap_halves(g * s)).astype(dst.dtype)
        dv_ref[...] = dv_acc[...].astype(dv_ref.dtype)

    hspec = pl.BlockSpec((None, length, B_HD), lambda b, h: (b, 0, h))
    ospec = pl.BlockSpec((None, length, B_HD), lambda b, h: (b, 0, off + h))
    tab = pl.BlockSpec((length, B_HD), lambda b, h: (0, 0))
    return pl.pallas_call(
        body, name=name, grid=(bsz, B_HEADS),
        in_specs=[pl.BlockSpec(cnt.shape, lambda b, h: (0, 0, 0)), hspec, hspec, hspec, ospec,
                  pl.BlockSpec((None, None, length, B_HD), lambda b, h: (b, h, 0, 0)), ospec, tab, tab],
        out_specs=[hspec] * 3, out_shape=[jax.ShapeDtypeStruct((bsz, length, B_WIDTH), BF16)] * 3,
        scratch_shapes=[pltpu.VMEM((length, B_HD), F32)] * 3,
        compiler_params=pltpu.CompilerParams(dimension_semantics=("parallel", "parallel")),
    )(cnt, q, k, v, o, lse, do, cos, sin)


def _chunk_cumsum(t, reverse):
    n = t.shape[0]
    row = lax.broadcasted_iota(jnp.int32, t.shape, 0) & (A_CHUNK - 1)
    s = 1
    while s < A_CHUNK:
        if reverse:
            t = t + jnp.where(row < A_CHUNK - s, pltpu.roll(t, n - s, 0), 0.0)
        else:
            t = t + jnp.where(row >= s, pltpu.roll(t, s, 0), 0.0)
        s *= 2
    return t


def _hgrn_gates(fl, lb):
    sg = jax.nn.sigmoid(fl)
    f = lb + (1.0 - lb) * sg
    return sg, f


def _hgrn_chunks(nchunk, qd_s, ki_s, b_s, v_ref, o_s, st_s=None):
    tri = lax.broadcasted_iota(jnp.int32, (A_CHUNK, A_CHUNK), 0) >= lax.broadcasted_iota(jnp.int32, (A_CHUNK, A_CHUNK), 1)

    def step(n, st):
        rows = pl.ds(pl.multiple_of(n * A_CHUNK, A_CHUNK), A_CHUNK)
        if st_s is not None:
            st_s[n] = st
        qd, ki, vc = qd_s[rows, :].astype(MXU_DTYPE), ki_s[rows, :], v_ref[rows, :].astype(MXU_DTYPE)
        dec = jnp.exp(b_s[pl.ds(n * A_CHUNK + A_CHUNK - 1, 1), :])
        a = lax.dot_general(qd, ki.astype(MXU_DTYPE), (((1,), (1,)), ((), ())), preferred_element_type=F32)
        a = jnp.where(tri, a, 0.0).astype(MXU_DTYPE)
        o_s[rows, :] = (jnp.dot(a, vc, preferred_element_type=F32)
                        + lax.dot_general(qd, st.astype(MXU_DTYPE), (((1,), (1,)), ((), ())), preferred_element_type=F32))
        ke = (ki * dec).astype(MXU_DTYPE)
        return st * dec + lax.dot_general(vc, ke, (((0,), (0,)), ((), ())), preferred_element_type=F32)

    lax.fori_loop(0, nchunk, step, jnp.zeros((A_DK, A_DK), F32), unroll=4)


def _bmm(a, b, ca, cb):
    return lax.dot_general(a, b, (((ca,), (cb,)), ((0,), (0,))), preferred_element_type=F32)


def _hgrn_forward_chunks(nchunk, q, f, b, v_ref, st_s, dec_s):
    shape = (nchunk, A_CHUNK, A_DK)
    b3 = b.reshape(shape)
    dec = jnp.exp(b3[:, A_CHUNK - 1:A_CHUNK, :])
    dec_s[...] = dec
    qd = (q * jnp.exp(b)).reshape(shape)
    ki = ((1.0 - f) * jnp.exp(-b)).reshape(shape)
    qdb, kib, keb = qd.astype(MXU_DTYPE), ki.astype(MXU_DTYPE), (ki * dec).astype(MXU_DTYPE)
    v3 = v_ref[...].reshape(shape).astype(MXU_DTYPE)
    tri = (lax.broadcasted_iota(jnp.int32, (1, A_CHUNK, A_CHUNK), 1) >= lax.broadcasted_iota(jnp.int32, (1, A_CHUNK, A_CHUNK), 2))
    a = jnp.where(tri, _bmm(qdb, kib, 2, 2), 0.0).astype(MXU_DTYPE)
    st_s[...] = _bmm(v3, keb, 1, 1)

    def rec(n, st):
        u = st_s[n]
        st_s[n] = st
        return st * dec_s[n] + u

    lax.fori_loop(0, nchunk, rec, jnp.zeros((A_DK, A_DK), F32))
    o = _bmm(a, v3, 2, 1) + _bmm(qdb, st_s[...].astype(MXU_DTYPE), 2, 2)
    return dict(dec=dec, qd=qd, ki=ki, qdb=qdb, kib=kib, keb=keb, v3=v3, a=a, tri=tri), o


def _hgrn_fwd(name, z, lb, onw):
    bsz, length, _ = z.shape
    nchunk = length // A_CHUNK

    def body(q_ref, f_ref, v_ref, g_ref, lb_ref, w_ref, y_ref, st_s, dec_s):
        _, f = _hgrn_gates(f_ref[...], lb_ref[...])
        b = _chunk_cumsum(jnp.log(f), False)
        _, o = _hgrn_forward_chunks(nchunk, q_ref[...], f, b, v_ref, st_s, dec_s)
        o = o.reshape(length, A_DK)
        on = o * lax.rsqrt(jnp.mean(o * o, axis=-1, keepdims=True) + NORM_EPS)
        y_ref[...] = on * w_ref[...] * _silu(g_ref[...])

    cols = [pl.BlockSpec((None, length, A_DK), functools.partial(lambda k, b, h: (b, 0, k * A_HEADS + h), k)) for k in range(4)]
    vec = pl.BlockSpec((1, A_DK), lambda b, h: (0, h))
    return pl.pallas_call(
        body, name=name, grid=(bsz, A_HEADS), in_specs=cols + [vec, vec],
        out_specs=pl.BlockSpec((None, length, A_DK), lambda b, h: (b, 0, h)),
        out_shape=jax.ShapeDtypeStruct((bsz, length, A_WIDTH), F32),
        scratch_shapes=[pltpu.VMEM((nchunk, A_DK, A_DK), F32), pltpu.VMEM((nchunk, 1, A_DK), F32)],
        compiler_params=pltpu.CompilerParams(dimension_semantics=("parallel", "parallel")),
    )(z, z, z, z, lb, onw)


def _hgrn_bwd(name, z, lb, onw, dy):
    bsz, length, _ = z.shape
    nchunk = length // A_CHUNK
    shape = (nchunk, A_CHUNK, A_DK)

    def body(q_ref, f_ref, v_ref, g_ref, lb_ref, w_ref, dy_ref, dq_ref, df_ref, dv_ref, dg_ref, dlb_ref, dw_ref,
             st_s, dst_s, dec_s):
        lb = lb_ref[...]
        sg, f = _hgrn_gates(f_ref[...], lb)
        b = _chunk_cumsum(jnp.log(f), False)
        t, o = _hgrn_forward_chunks(nchunk, q_ref[...], f, b, v_ref, st_s, dec_s)
        o, g, w, dyv = o.reshape(length, A_DK), g_ref[...], w_ref[...], dy_ref[...]
        r = lax.rsqrt(jnp.mean(o * o, axis=-1, keepdims=True) + NORM_EPS)
        on = o * r
        sgg = jax.nn.sigmoid(g)
        gate = g * sgg
        dg_ref[...] = (dyv * on * w * (sgg * (1.0 + g * (1.0 - sgg)))).astype(dg_ref.dtype)
        dw = jnp.sum(dyv * on * gate, axis=0, keepdims=True)
        don = dyv * w * gate
        do = (r * (don - on * jnp.mean(don * on, axis=-1, keepdims=True))).reshape(shape).astype(MXU_DTYPE)
        da = jnp.where(t["tri"], _bmm(do, t["v3"], 2, 2), 0.0).astype(MXU_DTYPE)
        dst_s[...] = _bmm(do, t["qdb"], 1, 1)

        def rec(i, dst):
            n = nchunk - 1 - i
            u = dst_s[n]
            dst_s[n] = dst
            return dst * dec_s[n] + u

        lax.fori_loop(0, nchunk, rec, jnp.zeros((A_DK, A_DK), F32))
        dst, st = dst_s[...], st_s[...]
        dstb = dst.astype(MXU_DTYPE)
        dec, ki, qd = t["dec"], t["ki"], t["qd"]
        dv_ref[...] = (_bmm(t["a"], do, 1, 1) + _bmm(t["keb"], dstb, 2, 2)).reshape(length, A_DK).astype(dv_ref.dtype)
        dqd = _bmm(da, t["kib"], 2, 1) + _bmm(do, st.astype(MXU_DTYPE), 2, 1)
        dke = _bmm(t["v3"], dstb, 2, 1)
        dki = _bmm(da, t["qdb"], 1, 1) + dke * dec
        ddec = jnp.sum(dst * st, axis=1, keepdims=True) + jnp.sum(dke * ki, axis=1, keepdims=True)
        last = lax.broadcasted_iota(jnp.int32, (1, A_CHUNK, A_DK), 1) == A_CHUNK - 1
        db = (dqd * qd - dki * ki + jnp.where(last, ddec * dec, 0.0)).reshape(length, A_DK)
        dlf = _chunk_cumsum(db, True)
        dq_ref[...] = (dqd.reshape(length, A_DK) * jnp.exp(b)).astype(dq_ref.dtype)
        dfv = dlf / f - dki.reshape(length, A_DK) * jnp.exp(-b)
        df_ref[...] = (dfv * (1.0 - lb) * sg * (1.0 - sg)).astype(df_ref.dtype)
        dlb = jnp.sum(dfv * (1.0 - sg), axis=0, keepdims=True)
        first = pl.program_id(1) == 0

        @pl.when(first)
        def _():
            dlb_ref[...] = dlb
            dw_ref[...] = dw

        @pl.when(jnp.logical_not(first))
        def _():
            dlb_ref[...] += dlb
            dw_ref[...] += dw

    cols = [pl.BlockSpec((None, length, A_DK), functools.partial(lambda k, h, b: (b, 0, k * A_HEADS + h), k)) for k in range(4)]
    vec = pl.BlockSpec((1, A_DK), lambda h, b: (0, h))
    head = pl.BlockSpec((None, length, A_DK), lambda h, b: (b, 0, h))
    act = jax.ShapeDtypeStruct((bsz, length, A_WIDTH), BF16)
    return pl.pallas_call(
        body, name=name, grid=(A_HEADS, bsz), in_specs=cols + [vec, vec, head],
        out_specs=[head] * 4 + [vec, vec], out_shape=[act] * 4 + [jax.ShapeDtypeStruct((1, A_WIDTH), F32)] * 2,
        scratch_shapes=[pltpu.VMEM((nchunk, A_DK, A_DK), F32)] * 2 + [pltpu.VMEM((nchunk, 1, A_DK), F32)],
        compiler_params=pltpu.CompilerParams(dimension_semantics=("parallel", "arbitrary")),
    )(z, z, z, z, lb, onw, dy)


def _hgrn_fwd_loop(name, z, lb, onw):
    bsz, length, _ = z.shape
    nchunk = length // A_CHUNK

    def body(q_ref, f_ref, v_ref, g_ref, lb_ref, w_ref, y_ref, qd_s, ki_s, b_s, o_s):
        _, f = _hgrn_gates(f_ref[...], lb_ref[...])
        b = _chunk_cumsum(jnp.log(f), False)
        b_s[...] = b
        qd_s[...] = q_ref[...] * jnp.exp(b)
        ki_s[...] = (1.0 - f) * jnp.exp(-b)
        _hgrn_chunks(nchunk, qd_s, ki_s, b_s, v_ref, o_s)
        o = o_s[...]
        on = o * lax.rsqrt(jnp.mean(o * o, axis=-1, keepdims=True) + NORM_EPS)
        y_ref[...] = on * w_ref[...] * _silu(g_ref[...])

    cols = [pl.BlockSpec((None, length, A_DK), functools.partial(lambda k, b, h: (b, 0, k * A_HEADS + h), k)) for k in range(4)]
    vec = pl.BlockSpec((1, A_DK), lambda b, h: (0, h))
    return pl.pallas_call(
        body, name=name, grid=(bsz, A_HEADS), in_specs=cols + [vec, vec],
        out_specs=pl.BlockSpec((None, length, A_DK), lambda b, h: (b, 0, h)),
        out_shape=jax.ShapeDtypeStruct((bsz, length, A_WIDTH), F32),
        scratch_shapes=[pltpu.VMEM((length, A_DK), F32)] * 4,
        compiler_params=pltpu.CompilerParams(dimension_semantics=("parallel", "parallel")),
    )(z, z, z, z, lb, onw)


def _hgrn_bwd_loop(name, z, lb, onw, dy):
    bsz, length, _ = z.shape
    nchunk = length // A_CHUNK

    def body(q_ref, f_ref, v_ref, g_ref, lb_ref, w_ref, dy_ref, dq_ref, df_ref, dv_ref, dg_ref, dlb_ref, dw_ref,
             qd_s, ki_s, b_s, o_s, st_s, dqd_s, dki_s, dbl_s):
        lb = lb_ref[...]
        sg, f = _hgrn_gates(f_ref[...], lb)
        b = _chunk_cumsum(jnp.log(f), False)
        b_s[...] = b
        qd_s[...] = q_ref[...] * jnp.exp(b)
        ki_s[...] = (1.0 - f) * jnp.exp(-b)
        _hgrn_chunks(nchunk, qd_s, ki_s, b_s, v_ref, o_s, st_s)
        o, g, w, dyv = o_s[...], g_ref[...], w_ref[...], dy_ref[...]
        r = lax.rsqrt(jnp.mean(o * o, axis=-1, keepdims=True) + NORM_EPS)
        on = o * r
        sgg = jax.nn.sigmoid(g)
        gate = g * sgg
        dg_ref[...] = (dyv * on * w * (sgg * (1.0 + g * (1.0 - sgg)))).astype(dg_ref.dtype)
        dw = jnp.sum(dyv * on * gate, axis=0, keepdims=True)
        don = dyv * w * gate
        o_s[...] = r * (don - on * jnp.mean(don * on, axis=-1, keepdims=True))
        tri = lax.broadcasted_iota(jnp.int32, (A_CHUNK, A_CHUNK), 0) >= lax.broadcasted_iota(jnp.int32, (A_CHUNK, A_CHUNK), 1)
        last = lax.broadcasted_iota(jnp.int32, (A_CHUNK, A_DK), 0) == A_CHUNK - 1

        def back(t, dst):
            n = nchunk - 1 - t
            rows = pl.ds(pl.multiple_of(n * A_CHUNK, A_CHUNK), A_CHUNK)
            qd, ki, vc = qd_s[rows, :].astype(MXU_DTYPE), ki_s[rows, :], v_ref[rows, :].astype(MXU_DTYPE)
            kib = ki.astype(MXU_DTYPE)
            do = o_s[rows, :].astype(MXU_DTYPE)
            st = st_s[n]
            dec = jnp.exp(b_s[pl.ds(n * A_CHUNK + A_CHUNK - 1, 1), :])
            dstb = dst.astype(MXU_DTYPE)
            a = lax.dot_general(qd, kib, (((1,), (1,)), ((), ())), preferred_element_type=F32)
            a = jnp.where(tri, a, 0.0).astype(MXU_DTYPE)
            da = lax.dot_general(do, vc, (((1,), (1,)), ((), ())), preferred_element_type=F32)
            da = jnp.where(tri, da, 0.0).astype(MXU_DTYPE)
            ke = (ki * dec).astype(MXU_DTYPE)
            dv_ref[rows, :] = (lax.dot_general(a, do, (((0,), (0,)), ((), ())), preferred_element_type=F32)
                               + lax.dot_general(ke, dstb, (((1,), (1,)), ((), ())), preferred_element_type=F32)).astype(dv_ref.dtype)
            dqd_s[rows, :] = (jnp.dot(da, kib, preferred_element_type=F32)
                              + jnp.dot(do, st.astype(MXU_DTYPE), preferred_element_type=F32))
            dke = jnp.dot(vc, dstb, preferred_element_type=F32)
            dki_s[rows, :] = lax.dot_general(da, qd, (((0,), (0,)), ((), ())), preferred_element_type=F32) + dke * dec
            ddec = jnp.sum(dst * st, axis=0, keepdims=True) + jnp.sum(dke * ki, axis=0, keepdims=True)
            dbl_s[rows, :] = jnp.where(last, ddec * dec, 0.0)
            return dst * dec + lax.dot_general(do, qd, (((0,), (0,)), ((), ())), preferred_element_type=F32)

        lax.fori_loop(0, nchunk, back, jnp.zeros((A_DK, A_DK), F32), unroll=2)
        dqd, dki, qd, ki = dqd_s[...], dki_s[...], qd_s[...], ki_s[...]
        b = b_s[...]
        dlf = _chunk_cumsum(dqd * qd - dki * ki + dbl_s[...], True)
        dq_ref[...] = (dqd * jnp.exp(b)).astype(dq_ref.dtype)
        dfv = dlf / f - dki * jnp.exp(-b)
        df_ref[...] = (dfv * (1.0 - lb) * sg * (1.0 - sg)).astype(df_ref.dtype)
        dlb = jnp.sum(dfv * (1.0 - sg), axis=0, keepdims=True)
        first = pl.program_id(1) == 0

        @pl.when(first)
        def _():
            dlb_ref[...] = dlb
            dw_ref[...] = dw

        @pl.when(jnp.logical_not(first))
        def _():
            dlb_ref[...] += dlb
            dw_ref[...] += dw

    cols = [pl.BlockSpec((None, length, A_DK), functools.partial(lambda k, h, b: (b, 0, k * A_HEADS + h), k)) for k in range(4)]
    vec = pl.BlockSpec((1, A_DK), lambda h, b: (0, h))
    head = pl.BlockSpec((None, length, A_DK), lambda h, b: (b, 0, h))
    act = jax.ShapeDtypeStruct((bsz, length, A_WIDTH), BF16)
    return pl.pallas_call(
        body, name=name, grid=(A_HEADS, bsz), in_specs=cols + [vec, vec, head],
        out_specs=[head] * 4 + [vec, vec], out_shape=[act] * 4 + [jax.ShapeDtypeStruct((1, A_WIDTH), F32)] * 2,
        scratch_shapes=[pltpu.VMEM((length, A_DK), F32)] * 4 + [pltpu.VMEM((nchunk, A_DK, A_DK), F32)]
        + [pltpu.VMEM((length, A_DK), F32)] * 3,
        compiler_params=pltpu.CompilerParams(dimension_semantics=("parallel", "arbitrary")),
    )(z, z, z, z, lb, onw, dy)


S5_SEG = 16
S5_W = 512
S5_LANES = C_GROUPS * C_STATE
S5_NB = 8
S5_CH = D_MODEL // S5_NB
S5_COLS = 2 * S5_LANES // S5_NB


def _bd_mm(name, a, b, tb=False, out_dtype=F32, tm=1024):
    n = a.shape[0]
    nb, ka, kn = (b.shape[0], b.shape[2], b.shape[1]) if tb else b.shape
    tm = _divisor(n, tm)
    dims = (((1,), (1 if tb else 0,)), ((), ()))

    def body(a_ref, b_ref, o_ref):
        o_ref[...] = lax.dot_general(a_ref[...].astype(MXU_DTYPE), b_ref[...].astype(MXU_DTYPE), dims,
                                     preferred_element_type=F32).astype(o_ref.dtype)

    return pl.pallas_call(
        body, name=name, grid=(n // tm, nb),
        in_specs=[pl.BlockSpec((tm, ka), lambda i, j: (i, j)), pl.BlockSpec((None,) + b.shape[1:], lambda i, j: (j, 0, 0))],
        out_specs=pl.BlockSpec((tm, kn), lambda i, j: (i, j)), out_shape=jax.ShapeDtypeStruct((n, nb * kn), out_dtype),
        compiler_params=pltpu.CompilerParams(dimension_semantics=("parallel", "parallel")),
    )(a, b)


def _bd_wgrad(name, a, c, ka, kn, tk=1024):
    n = a.shape[0]
    nb = a.shape[1] // ka
    tk = _divisor(n, tk)

    def body(a_ref, c_ref, o_ref):
        p = lax.dot_general(a_ref[...].astype(MXU_DTYPE), c_ref[...].astype(MXU_DTYPE), (((0,), (0,)), ((), ())),
                            preferred_element_type=F32)
        first = pl.program_id(1) == 0

        @pl.when(first)
        def _():
            o_ref[...] = p

        @pl.when(jnp.logical_not(first))
        def _():
            o_ref[...] += p

    return pl.pallas_call(
        body, name=name, grid=(nb, n // tk),
        in_specs=[pl.BlockSpec((tk, ka), lambda j, k: (k, j)), pl.BlockSpec((tk, kn), lambda j, k: (k, j))],
        out_specs=pl.BlockSpec((None, ka, kn), lambda j, k: (j, 0, 0)), out_shape=jax.ShapeDtypeStruct((nb, ka, kn), F32),
        compiler_params=pltpu.CompilerParams(dimension_semantics=("parallel", "arbitrary")),
    )(a, c)


def _seg_permute(t, bsz):
    n, c = t.shape
    return t.reshape(bsz, S5_SEG, n // bsz // S5_SEG, c).transpose(0, 2, 1, 3).reshape(n, c)


def _seg_unpermute(t, bsz):
    n, c = t.shape
    return t.reshape(bsz, n // bsz // S5_SEG, S5_SEG, c).transpose(0, 2, 1, 3).reshape(n, c)


def _s5_weights(lam_re, lam_im, log_dt, b_re, b_im, c_re, c_im):
    lr = jnp.minimum(lam_re, C_MIN_NEG_RE)
    li = lam_im
    dt = jnp.exp(log_dt)[:, None]
    mag = jnp.exp(dt * lr)
    ar, ai = mag * jnp.cos(dt * li), mag * jnp.sin(dt * li)
    den = lr * lr + li * li
    zr = ((ar - 1.0) * lr + ai * li) / den
    zi = (ai * lr - (ar - 1.0) * li) / den
    bbr = zr[..., None] * b_re - zi[..., None] * b_im
    bbi = zr[..., None] * b_im + zi[..., None] * b_re
    gpb = C_GROUPS // S5_NB
    eye = jnp.eye(gpb, dtype=F32)
    bb = jnp.stack([bbr, bbi]).reshape(2, S5_NB, gpb, C_STATE, C_GROUP)
    wb = jnp.einsum('ij,rbjpc->bicjpr', eye, bb).reshape(S5_NB, S5_CH, -1, S5_W, 2)
    wb = wb.transpose(0, 1, 2, 4, 3).reshape(S5_NB, S5_CH, S5_COLS)
    cc = jnp.stack([c_re, -c_im]).reshape(2, S5_NB, gpb, C_GROUP, C_STATE)
    wc = jnp.einsum('ij,rbjcp->bjpric', eye, cc).reshape(S5_NB, -1, S5_W, 2, S5_CH)
    wc = wc.transpose(0, 1, 3, 2, 4).reshape(S5_NB, S5_COLS, S5_CH)
    return ar.reshape(1, S5_LANES), ai.reshape(1, S5_LANES), wb, wc


def _s5_scan(name, bu, a_re, a_im, bsz, reverse):
    n, width = bu.shape
    length = n // bsz
    steps = length // S5_SEG
    assert steps & (steps - 1) == 0
    w = S5_W

    def body(bu_ref, ar_ref, ai_ref, x_ref):
        ar = jnp.broadcast_to(ar_ref[...], (S5_SEG, w))
        ai = jnp.broadcast_to(ai_ref[...], (S5_SEG, w))
        if reverse:
            ai = -ai
        zero = jnp.zeros((S5_SEG, w), F32)

        def rows_of(j):
            jj = steps - 1 - j if reverse else j
            return pl.ds(pl.multiple_of(jj * S5_SEG, S5_SEG), S5_SEG)

        def local_step(j, st):
            sr, si = st
            rows = rows_of(j)
            nr = ar * sr - ai * si + bu_ref[rows, 0:w]
            ni = ar * si + ai * sr + bu_ref[rows, w:2 * w]
            x_ref[rows, 0:w] = nr
            x_ref[rows, w:2 * w] = ni
            return nr, ni

        er, ei = lax.fori_loop(0, steps, local_step, (zero, zero), unroll=4)
        pr, pi = ar[0:1], ai[0:1]
        for _ in range(steps.bit_length() - 1):
            pr, pi = pr * pr - pi * pi, 2.0 * pr * pi
        row = lax.broadcasted_iota(jnp.int32, (S5_SEG, w), 0)
        cr, ci = zero, zero
        inr, ini = jnp.zeros((1, w), F32), jnp.zeros((1, w), F32)
        order = list(range(S5_SEG))[::-1] if reverse else list(range(S5_SEG))
        for idx, s in enumerate(order):
            if idx:
                cr = jnp.where(row == s, inr, cr)
                ci = jnp.where(row == s, ini, ci)
            inr, ini = er[s:s + 1] + pr * inr - pi * ini, ei[s:s + 1] + pr * ini + pi * inr

        def carry_step(j, st):
            qr, qi = st
            rows = rows_of(j)
            x_ref[rows, 0:w] += qr * cr - qi * ci
            x_ref[rows, w:2 * w] += qr * ci + qi * cr
            return qr * ar - qi * ai, qr * ai + qi * ar

        lax.fori_loop(0, steps, carry_step, (ar, ai), unroll=4)

    blk = pl.BlockSpec((length, 2 * w), lambda b, j: (b, j))
    aspec = pl.BlockSpec((1, w), lambda b, j: (0, j))
    return pl.pallas_call(
        body, name=name, grid=(bsz, width // (2 * w)), in_specs=[blk, aspec, aspec], out_specs=blk,
        out_shape=jax.ShapeDtypeStruct(bu.shape, F32),
        compiler_params=pltpu.CompilerParams(dimension_semantics=("parallel", "parallel")),
    )(bu, a_re, a_im)


def _s5_da(name, x, g, bsz):
    n, width = x.shape
    length = n // bsz
    steps = length // S5_SEG
    w = S5_W

    def body(x_ref, g_ref, o_ref):
        row = lax.broadcasted_iota(jnp.int32, (S5_SEG, w), 0)
        last = pl.ds((steps - 1) * S5_SEG, S5_SEG)
        xpr = jnp.where(row == 0, 0.0, pltpu.roll(x_ref[last, 0:w], 1, 0))
        xpi = jnp.where(row == 0, 0.0, pltpu.roll(x_ref[last, w:2 * w], 1, 0))
        zero = jnp.zeros((S5_SEG, w), F32)

        def step(j, st):
            pr, pi, accr, acci = st
            rows = pl.ds(pl.multiple_of(j * S5_SEG, S5_SEG), S5_SEG)
            gr, gi = g_ref[rows, 0:w], g_ref[rows, w:2 * w]
            return x_ref[rows, 0:w], x_ref[rows, w:2 * w], accr + gr * pr + gi * pi, acci + gi * pr - gr * pi

        _, _, accr, acci = lax.fori_loop(0, steps, step, (xpr, xpi, zero, zero), unroll=4)
        first = pl.program_id(1) == 0

        @pl.when(first)
        def _():
            o_ref[:, 0:w] = accr
            o_ref[:, w:2 * w] = acci

        @pl.when(jnp.logical_not(first))
        def _():
            o_ref[:, 0:w] += accr
            o_ref[:, w:2 * w] += acci

    blk = pl.BlockSpec((length, 2 * w), lambda j, b: (b, j))
    return pl.pallas_call(
        body, name=name, grid=(width // (2 * w), bsz), in_specs=[blk, blk],
        out_specs=pl.BlockSpec((S5_SEG, 2 * w), lambda j, b: (0, j)), out_shape=jax.ShapeDtypeStruct((S5_SEG, width), F32),
        compiler_params=pltpu.CompilerParams(dimension_semantics=("parallel", "arbitrary")),
    )(x, g)


def _scan_in_place(ref, c0, ar1, ai1, steps, reverse):
    w = S5_W
    ar = jnp.broadcast_to(ar1, (S5_SEG, w))
    ai = jnp.broadcast_to(-ai1 if reverse else ai1, (S5_SEG, w))
    zero = jnp.zeros((S5_SEG, w), F32)
    re, im = pl.ds(c0, w), pl.ds(c0 + w, w)

    def rows_of(j):
        jj = steps - 1 - j if reverse else j
        return pl.ds(pl.multiple_of(jj * S5_SEG, S5_SEG), S5_SEG)

    def local_step(j, st):
        sr, si = st
        rows = rows_of(j)
        nr = ar * sr - ai * si + ref[rows, re]
        ni = ar * si + ai * sr + ref[rows, im]
        ref[rows, re] = nr
        ref[rows, im] = ni
        return nr, ni

    er, ei = lax.fori_loop(0, steps, local_step, (zero, zero), unroll=4)
    pr, pi = ar[0:1], ai[0:1]
    for _ in range(steps.bit_length() - 1):
        pr, pi = pr * pr - pi * pi, 2.0 * pr * pi
    row = lax.broadcasted_iota(jnp.int32, (S5_SEG, w), 0)
    cr, ci = zero, zero
    inr, ini = jnp.zeros((1, w), F32), jnp.zeros((1, w), F32)
    order = list(range(S5_SEG))[::-1] if reverse else list(range(S5_SEG))
    for idx, s in enumerate(order):
        if idx:
            cr = jnp.where(row == s, inr, cr)
            ci = jnp.where(row == s, ini, ci)
        inr, ini = er[s:s + 1] + pr * inr - pi * ini, ei[s:s + 1] + pr * ini + pi * inr

    def carry_step(j, st):
        qr, qi = st
        rows = rows_of(j)
        ref[rows, re] += qr * cr - qi * ci
        ref[rows, im] += qr * ci + qi * cr
        return qr * ar - qi * ai, qr * ai + qi * ar

    lax.fori_loop(0, steps, carry_step, (ar, ai), unroll=4)


def _da_partial(x_ref, g_ref, c0, steps):
    w = S5_W
    re, im = pl.ds(c0, w), pl.ds(c0 + w, w)
    row = lax.broadcasted_iota(jnp.int32, (S5_SEG, w), 0)
    last = pl.ds((steps - 1) * S5_SEG, S5_SEG)
    xpr = jnp.where(row == 0, 0.0, pltpu.roll(x_ref[last, re], 1, 0))
    xpi = jnp.where(row == 0, 0.0, pltpu.roll(x_ref[last, im], 1, 0))
    zero = jnp.zeros((S5_SEG, w), F32)

    def step(j, st):
        pr, pi, accr, acci = st
        rows = pl.ds(pl.multiple_of(j * S5_SEG, S5_SEG), S5_SEG)
        gr, gi = g_ref[rows, re], g_ref[rows, im]
        return x_ref[rows, re], x_ref[rows, im], accr + gr * pr + gi * pi, acci + gi * pr - gr * pi

    _, _, accr, acci = lax.fori_loop(0, steps, step, (xpr, xpi, zero, zero), unroll=4)
    return accr, acci


S5_VMEM_LIMIT = 56 * 1024 * 1024


def _s5_states(name, hp, wb, wc, a_re, a_im, bsz):
    n = hp.shape[0]
    length = n // bsz
    steps = length // S5_SEG
    assert steps & (steps - 1) == 0
    nsub = S5_COLS // (2 * S5_W)

    def body(h_ref, wb_ref, wc_ref, ar_ref, ai_ref, x_ref, y_ref):
        x_ref[...] = jnp.dot(h_ref[...].astype(MXU_DTYPE), wb_ref[...], preferred_element_type=F32)
        for sub in range(nsub):
            lanes = slice(sub * S5_W, (sub + 1) * S5_W)
            _scan_in_place(x_ref, sub * 2 * S5_W, ar_ref[:, lanes], ai_ref[:, lanes], steps, False)
        y_ref[...] = jnp.dot(x_ref[...].astype(MXU_DTYPE), wc_ref[...], preferred_element_type=F32)

    chan = pl.BlockSpec((length, S5_CH), lambda b, j: (b, j))
    avec = pl.BlockSpec((1, nsub * S5_W), lambda b, j: (0, j))
    return pl.pallas_call(
        body, name=name, grid=(bsz, S5_NB),
        in_specs=[chan, pl.BlockSpec((None, S5_CH, S5_COLS), lambda b, j: (j, 0, 0)),
                  pl.BlockSpec((None, S5_COLS, S5_CH), lambda b, j: (j, 0, 0)), avec, avec],
        out_specs=[pl.BlockSpec((length, S5_COLS), lambda b, j: (b, j)), chan],
        out_shape=[jax.ShapeDtypeStruct((n, S5_NB * S5_COLS), F32), jax.ShapeDtypeStruct((n, D_MODEL), F32)],
        compiler_params=pltpu.CompilerParams(dimension_semantics=("parallel", "parallel"), vmem_limit_bytes=S5_VMEM_LIMIT),
    )(hp, wb, wc, a_re, a_im)


def _s5_states_bwd(name, dyp, xs, hp, wb, wc, a_re, a_im, bsz):
    n = hp.shape[0]
    length = n // bsz
    steps = length // S5_SEG
    nsub = S5_COLS // (2 * S5_W)

    def body(dy_ref, x_ref, h_ref, wb_ref, wc_ref, ar_ref, ai_ref, du_ref, dwb_ref, dwc_ref, da_ref, g_s):
        dy = dy_ref[...]
        g_s[...] = lax.dot_general(dy, wc_ref[...], (((1,), (1,)), ((), ())), preferred_element_type=F32)
        das = []
        for sub in range(nsub):
            lanes = slice(sub * S5_W, (sub + 1) * S5_W)
            _scan_in_place(g_s, sub * 2 * S5_W, ar_ref[:, lanes], ai_ref[:, lanes], steps, True)
            das += list(_da_partial(x_ref, g_s, sub * 2 * S5_W, steps))
        gb = g_s[...].astype(MXU_DTYPE)
        du_ref[...] = lax.dot_general(gb, wb_ref[...], (((1,), (1,)), ((), ())), preferred_element_type=F32)
        dwb = lax.dot_general(h_ref[...].astype(MXU_DTYPE), gb, (((0,), (0,)), ((), ())), preferred_element_type=F32)
        dwc = lax.dot_general(x_ref[...].astype(MXU_DTYPE), dy, (((0,), (0,)), ((), ())), preferred_element_type=F32)
        first = pl.program_id(1) == 0

        @pl.when(first)
        def _():
            dwb_ref[...] = dwb
            dwc_ref[...] = dwc
            for k, t in enumerate(das):
                da_ref[:, k * S5_W:(k + 1) * S5_W] = t

        @pl.when(jnp.logical_not(first))
        def _():
            dwb_ref[...] += dwb
            dwc_ref[...] += dwc
            for k, t in enumerate(das):
                da_ref[:, k * S5_W:(k + 1) * S5_W] += t

    chan = pl.BlockSpec((length, S5_CH), lambda j, b: (b, j))
    avec = pl.BlockSpec((1, nsub * S5_W), lambda j, b: (0, j))
    wbs = pl.BlockSpec((None, S5_CH, S5_COLS), lambda j, b: (j, 0, 0))
    wcs = pl.BlockSpec((None, S5_COLS, S5_CH), lambda j, b: (j, 0, 0))
    return pl.pallas_call(
        body, name=name, grid=(S5_NB, bsz),
        in_specs=[chan, pl.BlockSpec((length, S5_COLS), lambda j, b: (b, j)), chan, wbs, wcs, avec, avec],
        out_specs=[chan, wbs, wcs, pl.BlockSpec((S5_SEG, S5_COLS), lambda j, b: (0, j))],
        out_shape=[jax.ShapeDtypeStruct((n, D_MODEL), F32), jax.ShapeDtypeStruct(wb.shape, F32),
                   jax.ShapeDtypeStruct(wc.shape, F32), jax.ShapeDtypeStruct((S5_SEG, S5_NB * S5_COLS), F32)],
        scratch_shapes=[pltpu.VMEM((length, S5_COLS), F32)],
        compiler_params=pltpu.CompilerParams(dimension_semantics=("parallel", "arbitrary"), vmem_limit_bytes=S5_VMEM_LIMIT),
    )(dyp, xs, hp, wb, wc, a_re, a_im)


def _gelu(y):
    return 0.5 * y * (1.0 + lax.erf(y * math.sqrt(0.5)))


def _gelu_grad(y):
    return 0.5 * (1.0 + lax.erf(y * math.sqrt(0.5))) + y * jnp.exp(-0.5 * y * y) * (1.0 / math.sqrt(2.0 * math.pi))


def _s5_fwd(h, params, d_skip, bsz):
    (a_re, a_im, wb, wc), w_vjp = jax.vjp(_s5_weights, *params)
    wb, wc = wb.astype(BF16), wc.astype(BF16)
    hp = _seg_permute(h, bsz)
    xs, yc = _s5_states("s5_states_f", hp, wb, wc, a_re, a_im, bsz)
    ypre, glp = _rowwise("s5_gelu", lambda yy, uu, dd: (lambda t: (t, _gelu(t)))(yy + dd * uu), [yc, hp], [d_skip],
                         out_rows=[(D_MODEL, F32), (D_MODEL, BF16)])
    return _seg_unpermute(glp, bsz), dict(hp=hp, xs=xs, ypre=ypre, a_re=a_re, a_im=a_im, wb=wb, wc=wc, w_vjp=w_vjp)


def _s5_bwd(dgl, sv, d_skip, bsz):
    dyp, dskip, dd = _rowwise(
        "b_s5_gelu", lambda dg, yy, uu, ds: (lambda t: (t, t * ds, jnp.sum(t * uu, axis=0, keepdims=True)))(dg * _gelu_grad(yy)),
        [_seg_permute(dgl, bsz), sv["ypre"], sv["hp"]], [d_skip], out_rows=[(D_MODEL, BF16), (D_MODEL, F32)],
        out_sums=[D_MODEL])
    du, dwb, dwc, da = _s5_states_bwd("s5_states_b", dyp, sv["xs"], sv["hp"], sv["wb"], sv["wc"], sv["a_re"], sv["a_im"], bsz)
    da = jnp.sum(da, axis=0).reshape(S5_LANES // S5_W, 2, S5_W)
    dp = sv["w_vjp"]((da[:, 0].reshape(1, S5_LANES), da[:, 1].reshape(1, S5_LANES), dwb, dwc))
    return _seg_unpermute(du + dskip, bsz), dp, dd


def _pack_rows(arrays):
    rows = []
    for a in arrays:
        flat = a.reshape(-1).astype(F32)
        pad = (-flat.shape[0]) % PACK_COLS
        rows.append(jnp.pad(flat, (0, pad)).reshape(-1, PACK_COLS))
    out = jnp.concatenate(rows, axis=0)
    return jnp.pad(out, ((0, (-out.shape[0]) % 16), (0, 0)))


def _unpack_rows(packed, shapes):
    out, r = [], 0
    for s in shapes:
        size = int(np.prod(s))
        nr = -(-size // PACK_COLS)
        out.append(packed[r:r + nr].reshape(-1)[:size].reshape(s))
        r += nr
    return out


def kernel(x, mem, norm_w, mem_norm_w, ab_w_in, ab_w_out, hgrn_lb_logits, hgrn_out_norm_w, s5_lambda_re, s5_lambda_im, s5_log_dt, s5_b_re, s5_b_im, s5_c_re, s5_c_im, s5_d, s5_w_glu, xattn_wq, xattn_wkv, xattn_wo, ffn_w_in, ffn_w_out, loss_target, m_norm_w, m_mem_norm_w, m_ab_w_in, m_ab_w_out, m_hgrn_lb_logits, m_hgrn_out_norm_w, m_s5_lambda_re, m_s5_lambda_im, m_s5_log_dt, m_s5_b_re, m_s5_b_im, m_s5_c_re, m_s5_c_im, m_s5_d, m_s5_w_glu, m_xattn_wq, m_xattn_wkv, m_xattn_wo, m_ffn_w_in, m_ffn_w_out, v_norm_w, v_mem_norm_w, v_ab_w_in, v_ab_w_out, v_hgrn_lb_logits, v_hgrn_out_norm_w, v_s5_lambda_re, v_s5_lambda_im, v_s5_log_dt, v_s5_b_re, v_s5_b_im, v_s5_c_re, v_s5_c_im, v_s5_d, v_s5_w_glu, v_xattn_wq, v_xattn_wkv, v_xattn_wo, v_ffn_w_in, v_ffn_w_out):
    given = dict(locals())
    w = {n: given[n] for n in WEIGHTS}
    mom = {n: given["m_" + n] for n in WEIGHTS}
    var = {n: given["v_" + n] for n in WEIGHTS}
    bsz, length, _ = x.shape
    ntok = bsz * length
    chip = 2 * lax.axis_index("x") + lax.axis_index("y")

    big_axes = [ax for _, ax in BIG]
    full = _gather_chips("gather_weights", [w[n].astype(BF16) for n in BIG_NAMES], big_axes)
    wf = dict(zip(BIG_NAMES, full))
    small_block = jnp.concatenate([w['norm_w'].reshape(12, -1), w['s5_d'].reshape(1, -1), jnp.zeros((3, 256), F32)], axis=0)
    (small_full,) = _gather_chips("gather_norm_w", [small_block[None]], [1])
    nw = small_full[0, :12].reshape(2, 6, 1, D_MODEL)
    s5_d_full = small_full[0, 12:13]

    lb_table, lb_vjp = jax.vjp(lambda t: jnp.cumsum(jax.nn.softmax(t, axis=0), axis=0), w['hgrn_lb_logits'])
    xs = x.reshape(ntok, D_MODEL)
    mem2 = mem.reshape(bsz * MEM_LEN, D_MODEL)
    tgt = loss_target.reshape(ntok, D_MODEL)
    saved = []
    (h,) = _rowwise("norm_in", lambda a, g: _rms(a, g), [xs], [nw[0, 0]], out_rows=[(D_MODEL, BF16)])
    cur = xs
    for layer in range(2):
        sv = {"x": cur}
        if layer == 0:
            z = _mm("ab_in", h, wf['ab_w_in'][0]).reshape(bsz, length, -1)
            sv["h0"] = h
            rope_cos, rope_sin = _rope_tables(length)
            branch_cnt = _branch_count(length)
            oa = _hgrn_fwd("hgrn_f", z, lb_table[0:1], w['hgrn_out_norm_w'])
            qr, kr, vb = _rope_qkv("rope_qkv", z, rope_cos, rope_sin)
            ob, lse = _dilated_fwd("dilated_f", qr, kr, vb, branch_cnt)
            core = jnp.concatenate([oa, ob], axis=-1).reshape(ntok, D_MODEL)
            sv.update(z=z, qr=qr, kr=kr, vb=vb, lse=lse, core=core)
            y = _mm("ab_out", core, wf['ab_w_out'][0])
        else:
            s5p = [w[n][0] for n in ('s5_lambda_re', 's5_lambda_im', 's5_log_dt', 's5_b_re', 's5_b_im', 's5_c_re', 's5_c_im')]
            gl, sv["s5"] = _s5_fwd(h, s5p, s5_d_full, bsz)
            sv["gl"] = gl
            y, sv["zga"], sv["zgb"] = _mm_gated("s5_glu", gl, wf['s5_w_glu'][0], lambda a, b: a * jax.nn.sigmoid(b), F32)
        sv["y1"] = y
        x1, h2 = _rowwise(f"resnorm_a{layer}", lambda a, b, g1, g2: (lambda s: (s, _rms(s, g2)))(a + _rms(b, g1)),
                          [cur, y], [nw[layer, 1], nw[layer, 2]], out_rows=[(D_MODEL, F32), (D_MODEL, BF16)])
        sv["x1"], sv["h2"] = x1, h2
        (mem_n,) = _rowwise(f"mem_norm{layer}", lambda a, g: _rms(a, g), [mem2], [w['mem_norm_w'][layer][None]],
                            out_rows=[(D_MODEL, BF16)])
        sv["mem_n"] = mem_n
        q = _mm(f"xq{layer}", h2, wf['xattn_wq'][layer])
        kv = _mm(f"xkv{layer}", mem_n, wf['xattn_wkv'][layer])
        sv["q"], sv["kv"] = q, kv
        o = _xattn_fwd(f"xattn_f{layer}", q.reshape(bsz, length, D_MODEL), kv.reshape(bsz, MEM_LEN, 2 * D_MODEL))
        o = o.reshape(ntok, D_MODEL)
        sv["o"] = o
        y2 = _mm(f"xo{layer}", o, wf['xattn_wo'][layer])
        sv["y2"] = y2
        x2, h4 = _rowwise(f"resnorm_b{layer}", lambda a, b, g1, g2: (lambda s: (s, _rms(s, g2)))(a + _rms(b, g1)),
                          [x1, y2], [nw[layer, 3], nw[layer, 4]], out_rows=[(D_MODEL, F32), (D_MODEL, BF16)])
        sv["x2"], sv["h4"] = x2, h4
        act, sv["za"], sv["zb"] = _mm_gated(f"ffn_in{layer}", h4, wf['ffn_w_in'][layer], lambda a, b: _silu(a) * b, BF16)
        sv["act"] = act
        y3 = _mm(f"ffn_out{layer}", act, wf['ffn_w_out'][layer])
        sv["y3"] = y3
        saved.append(sv)
        if layer == 0:
            cur, h = _rowwise("resnorm_c0", lambda a, b, g1, g2: (lambda s: (s, _rms(s, g2)))(a + _rms(b, g1)),
                              [x2, y3], [nw[0, 5], nw[1, 0]], out_rows=[(D_MODEL, F32), (D_MODEL, F32)])
    g, sq = _rowwise("loss_head", lambda a, b, t, g1: (lambda e: (e * (1.0 / D_MODEL), jnp.sum(e * e, axis=0, keepdims=True)))(a + _rms(b, g1) - t),
                     [saved[1]["x2"], saved[1]["y3"], tgt], [nw[1, 5]], out_rows=[(D_MODEL, F32)], out_sums=[D_MODEL])
    loss = lax.psum(0.5 * jnp.sum(sq) / D_MODEL, ("x", "y", "c"))

    gbig = {}
    gnw = [[None] * 6 for _ in range(2)]
    gmemnw = [None, None]
    gsmall = {}
    for layer in (1, 0):
        sv = saved[layer]
        dy3, gnw[layer][5] = _rowwise(f"b_norm5_{layer}", lambda gg, yy, g1: _rms_bwd(yy, g1, gg), [g, sv["y3"]], [nw[layer, 5]],
                                      out_rows=[(D_MODEL, BF16)], out_sums=[D_MODEL])
        dact = _mm(f"b_ffn_out_dx{layer}", dy3, wf['ffn_w_out'][layer], tb=True)
        gw_out = _mm(f"b_ffn_out_dw{layer}", sv["act"], dy3, ta=True)

        def swiglu_bwd(a, b, da):
            a, b = a.astype(F32), b.astype(F32)
            sg = jax.nn.sigmoid(a)
            return jnp.concatenate([da * b * (sg * (1.0 + a * (1.0 - sg))), da * (a * sg)], axis=1)

        (dzf,) = _rowwise(f"b_swiglu{layer}", swiglu_bwd, [sv["za"], sv["zb"], dact], out_rows=[(2 * D_FF, BF16)], tile=256)
        dh4 = _mm(f"b_ffn_in_dx{layer}", dzf, wf['ffn_w_in'][layer], tb=True)
        gw_in = _mm(f"b_ffn_in_dw{layer}", sv["h4"], dzf, ta=True)
        gbig.setdefault('ffn_w_out', {})[layer] = gw_out
        gbig.setdefault('ffn_w_in', {})[layer] = gw_in

        def resnorm_bwd(gg, dh, xx, yy, g_in, g_res):
            dx, dw_in = _rms_bwd(xx, g_in, dh)
            tot = gg + dx
            dy, dw_res = _rms_bwd(yy, g_res, tot)
            return tot, dy, dw_in, dw_res

        g, dy2, gnw[layer][4], gnw[layer][3] = _rowwise(
            f"b_resnorm_b{layer}", resnorm_bwd, [g, dh4, sv["x2"], sv["y2"]], [nw[layer, 4], nw[layer, 3]],
            out_rows=[(D_MODEL, F32), (D_MODEL, BF16)], out_sums=[D_MODEL, D_MODEL])
        do = _mm(f"b_xo_dx{layer}", dy2, wf['xattn_wo'][layer], tb=True)
        gbig.setdefault('xattn_wo', {})[layer] = _mm(f"b_xo_dw{layer}", sv["o"], dy2, ta=True)
        dq, dk, dv = _xattn_bwd(f"xattn_b{layer}", sv["q"].reshape(bsz, length, D_MODEL),
                                sv["kv"].reshape(bsz, MEM_LEN, 2 * D_MODEL), do.reshape(bsz, length, D_MODEL))
        dq = dq.reshape(ntok, D_MODEL)
        dkv = jnp.concatenate([dk, dv], axis=-1).reshape(bsz * MEM_LEN, 2 * D_MODEL)
        dh2 = _mm(f"b_xq_dx{layer}", dq, wf['xattn_wq'][layer], tb=True)
        gbig.setdefault('xattn_wq', {})[layer] = _mm(f"b_xq_dw{layer}", sv["h2"], dq, ta=True)
        dmem_n = _mm(f"b_xkv_dx{layer}", dkv, wf['xattn_wkv'][layer], tb=True)
        gbig.setdefault('xattn_wkv', {})[layer] = _mm(f"b_xkv_dw{layer}", sv["mem_n"], dkv, ta=True)
        (gmemnw[layer],) = _rowwise(f"b_mem_norm{layer}", lambda dd, mm_, g1: _rms_bwd(mm_, g1, dd)[1], [dmem_n, mem2],
                                    [w['mem_norm_w'][layer][None]], out_sums=[D_MODEL])

        g, dy1, gnw[layer][2], gnw[layer][1] = _rowwise(
            f"b_resnorm_a{layer}", resnorm_bwd, [g, dh2, sv["x1"], sv["y1"]], [nw[layer, 2], nw[layer, 1]],
            out_rows=[(D_MODEL, F32), (D_MODEL, F32 if layer == 1 else BF16)], out_sums=[D_MODEL, D_MODEL])
        if layer == 1:
            def gate_bwd(a, b, dd):
                sg = jax.nn.sigmoid(b.astype(F32))
                return jnp.concatenate([dd * sg, dd * a.astype(F32) * sg * (1.0 - sg)], axis=1)

            (dzg,) = _rowwise("b_s5_gate", gate_bwd, [sv["zga"], sv["zgb"], dy1], out_rows=[(2 * D_MODEL, BF16)])
            dgl = _mm("b_s5_glu_dx", dzg, wf['s5_w_glu'][0], tb=True)
            gbig['s5_w_glu'] = {0: _mm("b_s5_glu_dw", sv["gl"], dzg, ta=True)}
            dh0, dp, gsmall['s5_d'] = _s5_bwd(dgl, sv["s5"], s5_d_full, bsz)
            for n, t in zip(('s5_lambda_re', 's5_lambda_im', 's5_log_dt', 's5_b_re', 's5_b_im', 's5_c_re', 's5_c_im'), dp):
                gsmall[n] = t[None]
            g, gnw[1][0] = _rowwise("b_norm_in1", lambda gg, dh, xx, g1: (lambda r: (gg + r[0], r[1]))(_rms_bwd(xx, g1, dh)),
                                    [g, dh0, sv["x"]], [nw[1, 0]], out_rows=[(D_MODEL, F32)], out_sums=[D_MODEL])
        else:
            dcore = _mm("b_ab_out_dx", dy1, wf['ab_w_out'][0], tb=True)
            gbig['ab_w_out'] = {0: _mm("b_ab_out_dw", sv["core"], dy1, ta=True)}
            dcore = dcore.reshape(bsz, length, D_MODEL)
            core3 = sv["core"].reshape(bsz, length, D_MODEL)
            dqa, dfa, dia, dga, dlb0, gsmall['hgrn_out_norm_w'] = _hgrn_bwd("hgrn_b", sv["z"], lb_table[0:1], w['hgrn_out_norm_w'], dcore)
            dqb, dkb, dvb = _dilated_bwd("dilated_b", sv["qr"], sv["kr"], sv["vb"], core3, sv["lse"], dcore, branch_cnt,
                                         rope_cos, rope_sin, off=A_WIDTH // B_HD)
            (gsmall['hgrn_lb_logits'],) = lb_vjp(jnp.zeros_like(lb_table).at[0].set(dlb0[0]))
            dz = jnp.concatenate([dqa, dfa, dia, dga, dqb, dkb, dvb], axis=-1).reshape(ntok, -1)
            dh0 = _mm("b_ab_in_dx", dz, wf['ab_w_in'][0], tb=True)
            gbig['ab_w_in'] = {0: _mm("b_ab_in_dw", sv["h0"], dz, ta=True)}
            grad_x, gnw[0][0] = _rowwise("b_norm_in0", lambda gg, dh, xx, g1: (lambda r: (gg + r[0], r[1]))(_rms_bwd(xx, g1, dh)),
                                         [g, dh0, sv["x"]], [nw[0, 0]], out_rows=[(D_MODEL, F32)], out_sums=[D_MODEL])
    gsmall['norm_w'] = jnp.stack([jnp.concatenate(gnw[l], axis=0) for l in range(2)])
    gsmall['mem_norm_w'] = jnp.concatenate(gmemnw, axis=0)

    packed = _pack_rows([gsmall[n] for n in SMALL])
    theirs = _sibling_swap("small_swap", packed)
    chip_sum = _add2("small_pair_sum", packed, theirs)
    (all_chips,) = _gather_chips("small_gather", [chip_sum[None]], [0])
    small_sum = _sum_slots("small_sum", all_chips.reshape(N_CHIPS, packed.shape[0], PACK_COLS))
    full_shapes = [(2, 6, D_MODEL) if n == 'norm_w' else (1, D_MODEL) if n == 's5_d' else w[n].shape for n in SMALL]
    gs = dict(zip(SMALL, _unpack_rows(small_sum, full_shapes)))
    for n in SHARDED_SMALL:
        gs[n] = lax.dynamic_slice_in_dim(gs[n], chip * 256, 256, axis=gs[n].ndim - 1)

    pos = _pos_vec()
    parts = [jnp.stack([gbig[n][l] for l in sorted(gbig[n])]) for n in BIG_NAMES]
    theirs = _pair_send("grad_pair_send", parts, big_axes)
    pair = [_pair_add("grad_pair_sum_" + n, a, b, ax, pos) for (n, ax), a, b in zip(BIG, parts, theirs)]
    slots = _chip_exchange("grad_chip_exchange", pair, big_axes)
    shards = [_chip_sum("grad_chip_sum_" + n, a, b, s, ax, pos) for (n, ax), a, b, s in zip(BIG, parts, theirs, slots)]
    gfull = dict(zip(BIG_NAMES, _pair_join("grad_pair_join", shards)))

    grads, deltas, new_m, new_v = {}, {}, {}, {}
    for n in BIG_NAMES:
        grads[n] = gfull[n]
        deltas[n], new_m[n], new_v[n] = _adam("adam_" + n, w[n], gfull[n], mom[n], var[n])
    pk = [_pack_rows([t[n] for n in SMALL]) for t in (w, gs, mom, var)]
    small_out = _adam("adam_small", *pk)
    shard_shapes = [w[n].shape for n in SMALL]
    for dst, packed_out in zip((deltas, new_m, new_v), small_out):
        dst.update(zip(SMALL, _unpack_rows(packed_out, shard_shapes)))
    grads.update(gs)
    return (loss, grad_x.reshape(x.shape), *[grads[n] for n in WEIGHTS], *[deltas[n] for n in WEIGHTS],
            *[new_m[n] for n in WEIGHTS], *[new_v[n] for n in WEIGHTS])
```

```python
import functools
import math

import numpy as np
import jax
import jax.numpy as jnp
from jax import lax
from jax.experimental import pallas as pl
from jax.experimental.pallas import tpu as pltpu

F32 = jnp.float32
BF16 = jnp.bfloat16
MXU_DTYPE = jnp.bfloat16

D_MODEL = 1024
NORM_EPS = 1e-6
A_HEADS, A_DK, A_CHUNK = 4, 128, 32
A_WIDTH = A_HEADS * A_DK
B_HEADS, B_HD = 4, 128
B_WIDTH = B_HEADS * B_HD
B_DILATIONS = ((128, 1), (512, 4), (2048, 16))
ROPE_THETA = 10000.0
C_GROUP, C_GROUPS, C_STATE, C_CHUNK = 16, 64, 64, 128
C_MIN_NEG_RE = -1e-4
MEM_LEN = 256
X_HEADS = 4
X_HD = D_MODEL // X_HEADS
D_FF = 2816
ADAM_LR, ADAM_B1, ADAM_B2, ADAM_EPS, ADAM_WD, ADAM_STEP = 0.001, 0.9, 0.999, 1e-08, 0.01, 10

N_CHIPS = 4
MESH = pl.DeviceIdType.MESH
ANY = pl.BlockSpec(memory_space=pl.ANY)
_RELS = ((1, 0), (0, 1), (1, 1))

WEIGHTS = ['norm_w', 'mem_norm_w', 'ab_w_in', 'ab_w_out', 'hgrn_lb_logits', 'hgrn_out_norm_w', 's5_lambda_re',
           's5_lambda_im', 's5_log_dt', 's5_b_re', 's5_b_im', 's5_c_re', 's5_c_im', 's5_d', 's5_w_glu', 'xattn_wq',
           'xattn_wkv', 'xattn_wo', 'ffn_w_in', 'ffn_w_out']
BIG = (('ab_w_in', 1), ('ab_w_out', 0), ('s5_w_glu', 1), ('xattn_wq', 0), ('xattn_wkv', 1), ('xattn_wo', 0),
       ('ffn_w_in', 1), ('ffn_w_out', 0))
BIG_NAMES = tuple(n for n, _ in BIG)
SMALL = tuple(n for n in WEIGHTS if n not in BIG_NAMES)
SHARDED_SMALL = ('norm_w', 's5_d')
PACK_COLS = 1024


def _pos():
    return lax.axis_index("x"), lax.axis_index("y"), lax.axis_index("c")


def _flip(v, d):
    return 1 - v if d else v


def _divisor(n, want):
    for t in (want, 1024, 512, 256, 128, 64, 32, 16, 8):
        if t <= want and n % t == 0:
            return t
    return n


def _rowwise(name, fn, rows, bcasts=(), out_rows=(), out_sums=(), tile=512):
    n = rows[0].shape[0]
    t = _divisor(n, tile)
    nr, nb, no, ns = len(rows), len(bcasts), len(out_rows), len(out_sums)

    def body(*refs):
        vals = [r[...] for r in refs[:nr + nb]]
        res = fn(*vals)
        if not isinstance(res, (tuple, list)):
            res = (res,)
        outs = refs[nr + nb:]
        for k in range(no):
            outs[k][...] = res[k].astype(outs[k].dtype)
        if ns:
            first = pl.program_id(0) == 0
            for k in range(ns):
                o, val = outs[no + k], res[no + k]

                @pl.when(first)
                def _():
                    o[...] = val

                @pl.when(jnp.logical_not(first))
                def _():
                    o[...] += val

    in_specs = [pl.BlockSpec((t, r.shape[1]), lambda i: (i, 0)) for r in rows]
    in_specs += [pl.BlockSpec(b.shape, lambda i: (0, 0)) for b in bcasts]
    out_specs = [pl.BlockSpec((t, c), lambda i: (i, 0)) for c, _ in out_rows]
    out_specs += [pl.BlockSpec((1, c), lambda i: (0, 0)) for c in out_sums]
    out_shape = [jax.ShapeDtypeStruct((n, c), dt) for c, dt in out_rows]
    out_shape += [jax.ShapeDtypeStruct((1, c), F32) for c in out_sums]
    res = pl.pallas_call(
        body, name=name, grid=(n // t,), in_specs=in_specs, out_specs=out_specs, out_shape=out_shape,
        compiler_params=pltpu.CompilerParams(dimension_semantics=("arbitrary",)),
    )(*rows, *bcasts)
    return res


def _rms(x, w):
    r = lax.rsqrt(jnp.mean(x * x, axis=-1, keepdims=True) + NORM_EPS)
    return x * r * w


def _rms_bwd(x, w, dy):
    r = lax.rsqrt(jnp.mean(x * x, axis=-1, keepdims=True) + NORM_EPS)
    xh = x * r
    dxh = dy * w
    dx = r * (dxh - xh * jnp.mean(dxh * xh, axis=-1, keepdims=True))
    return dx, jnp.sum(dy * xh, axis=0, keepdims=True)


def _silu(z):
    return z * jax.nn.sigmoid(z)


MM_VMEM_BUDGET = 44 * 1024 * 1024


def _mm_tiles(m, n, k, ta, abytes, bbytes, obytes):
    tn = next(t for t in (1792, 1408, 1024, 512, 256, 128) if n % t == 0) if ta else _divisor(n, 512)
    tk = _divisor(k, 1024) if ta else (k if k <= 2816 else next(t for t in (2816, 2048, 1792, 1024, 512) if k % t == 0))
    for tm in (2816, 2048, 1408, 1024, 512, 256, 128):
        if m % tm:
            continue
        need = 2 * (tm * tk * abytes + tk * tn * bbytes + tm * tn * obytes) + 2 * tm * tn * 4
        if need <= MM_VMEM_BUDGET:
            return tm, tn, tk
    return _divisor(m, 128), tn, tk


def _mm(name, a, b, ta=False, tb=False, out_dtype=F32):
    m, k = a.shape[::-1] if ta else a.shape
    k2, n = b.shape[::-1] if tb else b.shape
    assert k == k2, (name, a.shape, b.shape)
    tm, tn, tk = _mm_tiles(m, n, k, ta, a.dtype.itemsize, b.dtype.itemsize, jnp.dtype(out_dtype).itemsize)
    nk = k // tk
    dims = (((0 if ta else 1,), (1 if tb else 0,)), ((), ()))

    def prod(a_ref, b_ref):
        return lax.dot_general(a_ref[...].astype(MXU_DTYPE), b_ref[...].astype(MXU_DTYPE), dims,
                               preferred_element_type=F32)

    def body_one(a_ref, b_ref, o_ref):
        o_ref[...] = prod(a_ref, b_ref).astype(o_ref.dtype)

    def body_acc(a_ref, b_ref, o_ref, acc):
        kk = pl.program_id(2)

        @pl.when(kk == 0)
        def _():
            acc[...] = prod(a_ref, b_ref)

        @pl.when(kk > 0)
        def _():
            acc[...] += prod(a_ref, b_ref)

        @pl.when(kk == nk - 1)
        def _():
            o_ref[...] = acc[...].astype(o_ref.dtype)

    a_spec = pl.BlockSpec((tk, tm), lambda i, j, kk: (kk, i)) if ta else pl.BlockSpec((tm, tk), lambda i, j, kk: (i, kk))
    b_spec = pl.BlockSpec((tn, tk), lambda i, j, kk: (j, kk)) if tb else pl.BlockSpec((tk, tn), lambda i, j, kk: (kk, j))
    return pl.pallas_call(
        body_one if nk == 1 else body_acc, name=name, grid=(m // tm, n // tn, nk),
        in_specs=[a_spec, b_spec], out_specs=pl.BlockSpec((tm, tn), lambda i, j, kk: (i, j)),
        out_shape=jax.ShapeDtypeStruct((m, n), out_dtype),
        scratch_shapes=[] if nk == 1 else [pltpu.VMEM((tm, tn), F32)],
        compiler_params=pltpu.CompilerParams(dimension_semantics=("parallel", "parallel", "arbitrary")),
    )(a, b)


def _mm_gated(name, h, w, gate, out_dtype, tm=2048, tn=256):
    n, k = h.shape
    f = w.shape[1] // 2
    tm, nj = _divisor(n, tm), f // tn

    def body(h_ref, wa_ref, wb_ref, act_ref, za_ref, zb_ref):
        hv = h_ref[...].astype(MXU_DTYPE)
        za = jnp.dot(hv, wa_ref[...].astype(MXU_DTYPE), preferred_element_type=F32)
        zb = jnp.dot(hv, wb_ref[...].astype(MXU_DTYPE), preferred_element_type=F32)
        act_ref[...] = gate(za, zb).astype(act_ref.dtype)
        za_ref[...] = za.astype(za_ref.dtype)
        zb_ref[...] = zb.astype(zb_ref.dtype)

    out = pl.BlockSpec((tm, tn), lambda i, j: (i, j))
    return pl.pallas_call(
        body, name=name, grid=(n // tm, nj),
        in_specs=[pl.BlockSpec((tm, k), lambda i, j: (i, 0)), pl.BlockSpec((k, tn), lambda i, j: (0, j)),
                  pl.BlockSpec((k, tn), lambda i, j: (0, j + nj))],
        out_specs=[out] * 3,
        out_shape=[jax.ShapeDtypeStruct((n, f), out_dtype), jax.ShapeDtypeStruct((n, f), BF16), jax.ShapeDtypeStruct((n, f), BF16)],
        compiler_params=pltpu.CompilerParams(dimension_semantics=("parallel", "parallel")),
    )(h, w, w)


def _xattn_fwd(name, q, kv, tq=512):
    bsz, length, _ = q.shape
    tq = _divisor(length, tq)
    scale = X_HD ** -0.5

    def body(q_ref, k_ref, v_ref, o_ref):
        qv, kk, vv = q_ref[...].astype(MXU_DTYPE), k_ref[...].astype(MXU_DTYPE), v_ref[...].astype(MXU_DTYPE)
        s = lax.dot_general(qv, kk, (((1,), (1,)), ((), ())), preferred_element_type=F32) * scale
        p = jnp.exp(s - jnp.max(s, axis=-1, keepdims=True))
        p = p / jnp.sum(p, axis=-1, keepdims=True)
        o_ref[...] = jnp.dot(p.astype(MXU_DTYPE), vv, preferred_element_type=F32).astype(o_ref.dtype)

    return pl.pallas_call(
        body, name=name, grid=(bsz, X_HEADS, length // tq),
        in_specs=[pl.BlockSpec((None, tq, X_HD), lambda b, h, i: (b, i, h)),
                  pl.BlockSpec((None, MEM_LEN, X_HD), lambda b, h, i: (b, 0, h)),
                  pl.BlockSpec((None, MEM_LEN, X_HD), lambda b, h, i: (b, 0, X_HEADS + h))],
        out_specs=pl.BlockSpec((None, tq, X_HD), lambda b, h, i: (b, i, h)),
        out_shape=jax.ShapeDtypeStruct(q.shape, BF16),
        compiler_params=pltpu.CompilerParams(dimension_semantics=("parallel", "parallel", "arbitrary")),
    )(q, kv, kv)


def _xattn_bwd(name, q, kv, do, tq=512):
    bsz, length, _ = q.shape
    tq = _divisor(length, tq)
    scale = X_HD ** -0.5

    def body(q_ref, k_ref, v_ref, do_ref, dq_ref, dk_ref, dv_ref):
        qv, kk, vv = q_ref[...].astype(MXU_DTYPE), k_ref[...].astype(MXU_DTYPE), v_ref[...].astype(MXU_DTYPE)
        dov = do_ref[...].astype(MXU_DTYPE)
        s = lax.dot_general(qv, kk, (((1,), (1,)), ((), ())), preferred_element_type=F32) * scale
        p = jnp.exp(s - jnp.max(s, axis=-1, keepdims=True))
        p = p / jnp.sum(p, axis=-1, keepdims=True)
        dp = lax.dot_general(dov, vv, (((1,), (1,)), ((), ())), preferred_element_type=F32)
        ds = p * (dp - jnp.sum(dp * p, axis=-1, keepdims=True)) * scale
        dsb = ds.astype(MXU_DTYPE)
        dq_ref[...] = jnp.dot(dsb, kk, preferred_element_type=F32).astype(dq_ref.dtype)
        dk = lax.dot_general(dsb, qv, (((0,), (0,)), ((), ())), preferred_element_type=F32)
        dv = lax.dot_general(p.astype(MXU_DTYPE), dov, (((0,), (0,)), ((), ())), preferred_element_type=F32)
        first = pl.program_id(2) == 0

        @pl.when(first)
        def _():
            dk_ref[...] = dk
            dv_ref[...] = dv

        @pl.when(jnp.logical_not(first))
        def _():
            dk_ref[...] += dk
            dv_ref[...] += dv

    qspec = pl.BlockSpec((None, tq, X_HD), lambda b, h, i: (b, i, h))
    kspec = pl.BlockSpec((None, MEM_LEN, X_HD), lambda b, h, i: (b, 0, h))
    return pl.pallas_call(
        body, name=name, grid=(bsz, X_HEADS, length // tq),
        in_specs=[qspec, kspec, pl.BlockSpec((None, MEM_LEN, X_HD), lambda b, h, i: (b, 0, X_HEADS + h)), qspec],
        out_specs=[qspec, kspec, kspec],
        out_shape=[jax.ShapeDtypeStruct(q.shape, BF16), jax.ShapeDtypeStruct((bsz, MEM_LEN, D_MODEL), F32),
                   jax.ShapeDtypeStruct((bsz, MEM_LEN, D_MODEL), F32)],
        compiler_params=pltpu.CompilerParams(dimension_semantics=("parallel", "parallel", "arbitrary")),
    )(q, kv, kv, do)


def _dma_sems(*counts):
    return [pltpu.SemaphoreType.DMA((max(c, 1),)) for c in counts]


def _gather_chips(name, blocks, axes):
    n = len(blocks)
    shapes = [b.shape for b in blocks]

    def body(*refs):
        ins, outs = refs[:n], refs[n:2 * n]
        lsem, lrsem, ssem, rsem, fssem, frsem = refs[2 * n:]
        x, y, c = _pos()
        me = 2 * x + y

        def region(a, chip, h):
            _, r, cc = shapes[a]
            hr = r // 2
            if axes[a] == 0:
                return outs[a].at[:, pl.ds(chip * r + h * hr, hr), :]
            return outs[a].at[:, pl.ds(h * hr, hr), pl.ds(chip * cc, cc)]

        def whole(a, chip):
            _, r, cc = shapes[a]
            if axes[a] == 0:
                return outs[a].at[:, pl.ds(chip * r, r), :]
            return outs[a].at[:, :, pl.ds(chip * cc, cc)]

        sends = []
        for a in range(n):
            cp = pltpu.make_async_remote_copy(src_ref=ins[a], dst_ref=whole(a, me), send_sem=lsem.at[a], recv_sem=lrsem.at[a],
                                              device_id=(x, y, 1 - c), device_id_type=MESH)
            cp.start()
            sends.append(cp)
        for a in range(n):
            hr = shapes[a][1] // 2
            for k, (dx, dy) in enumerate(_RELS):
                cp = pltpu.make_async_remote_copy(
                    src_ref=ins[a].at[:, pl.ds(c * hr, hr), :], dst_ref=region(a, me, c),
                    send_sem=ssem.at[3 * a + k], recv_sem=rsem.at[3 * a + k],
                    device_id=(_flip(x, dx), _flip(y, dy), c), device_id_type=MESH)
                cp.start()
                sends.append(cp)
        for a in range(n):
            for k, (dx, dy) in enumerate(_RELS):
                px, py = _flip(x, dx), _flip(y, dy)
                got = region(a, 2 * px + py, c)
                pltpu.make_async_remote_copy(
                    src_ref=got, dst_ref=got, send_sem=ssem.at[3 * a + k], recv_sem=rsem.at[3 * a + k],
                    device_id=(px, py, c), device_id_type=MESH).wait_recv()
                cp = pltpu.make_async_remote_copy(
                    src_ref=got, dst_ref=got, send_sem=fssem.at[3 * a + k], recv_sem=frsem.at[3 * a + k],
                    device_id=(x, y, 1 - c), device_id_type=MESH)
                cp.start()
                sends.append(cp)
        for a in range(n):
            for k, (dx, dy) in enumerate(_RELS):
                got = region(a, 2 * _flip(x, dx) + _flip(y, dy), 1 - c)
                pltpu.make_async_remote_copy(
                    src_ref=got, dst_ref=got, send_sem=fssem.at[3 * a + k], recv_sem=frsem.at[3 * a + k],
                    device_id=(x, y, 1 - c), device_id_type=MESH).wait_recv()
        for a in range(n):
            pltpu.make_async_remote_copy(src_ref=ins[a], dst_ref=whole(a, me), send_sem=lsem.at[a], recv_sem=lrsem.at[a],
                                         device_id=(x, y, 1 - c), device_id_type=MESH).wait_recv()
        for cp in sends:
            cp.wait_send()

    out_shape = [jax.ShapeDtypeStruct((l, 4 * r, c) if ax == 0 else (l, r, 4 * c), b.dtype)
                 for (l, r, c), ax, b in zip(shapes, axes, blocks)]
    return pl.pallas_call(
        body, name=name, in_specs=[ANY] * n, out_specs=[ANY] * n, out_shape=out_shape,
        scratch_shapes=_dma_sems(n, n, 3 * n, 3 * n, 3 * n, 3 * n),
    )(*blocks)


def _pos_vec():
    x, y, c = _pos()
    return jnp.stack([c, 2 * x + y]).astype(jnp.int32)


def _pair_send(name, parts, axes):
    n = len(parts)
    shapes = [p.shape for p in parts]
    ncopy = sum(4 if ax == 0 else 1 for ax in axes)

    def body(*refs):
        ins, theirs = refs[:n], refs[n:2 * n]
        ssem, rsem = refs[2 * n:]
        x, y, c = _pos()
        pending, j = [], 0
        for a in range(n):
            _, rf, _ = shapes[a]
            if axes[a] == 0:
                hr = rf // 8
                pieces = [(ins[a].at[:, pl.ds((2 * s + 1 - c) * hr, hr), :], theirs[a].at[:, s]) for s in range(N_CHIPS)]
            else:
                hr = rf // 2
                pieces = [(ins[a].at[:, pl.ds((1 - c) * hr, hr), :], theirs[a])]
            for give, give_dst in pieces:
                rc = pltpu.make_async_remote_copy(src_ref=give, dst_ref=give_dst, send_sem=ssem.at[j],
                                                  recv_sem=rsem.at[j], device_id=(x, y, 1 - c), device_id_type=MESH)
                rc.start()
                pending.append(rc)
                j += 1
        for cp in pending:
            cp.wait()

    def half_shape(s, ax):
        return (s[0], N_CHIPS, s[1] // 8, s[2]) if ax == 0 else (s[0], s[1] // 2, s[2])

    out_shape = [jax.ShapeDtypeStruct(half_shape(s, ax), p.dtype) for s, ax, p in zip(shapes, axes, parts)]
    return pl.pallas_call(
        body, name=name, in_specs=[ANY] * n, out_specs=[ANY] * n, out_shape=out_shape,
        scratch_shapes=_dma_sems(ncopy, ncopy),
    )(*parts)


def _chip_exchange(name, halves, axes):
    n = len(halves)
    shapes = [h.shape for h in halves]

    def body(*refs):
        ins, outs = refs[:n], refs[n:2 * n]
        ssem, rsem = refs[2 * n:]
        x, y, c = _pos()

        def part(a, chip):
            if axes[a] == 0:
                return ins[a].at[:, chip]
            cc = shapes[a][2] // N_CHIPS
            return ins[a].at[:, :, pl.ds(chip * cc, cc)]

        sends = []
        for a in range(n):
            for k, (dx, dy) in enumerate(_RELS):
                px, py = _flip(x, dx), _flip(y, dy)
                rc = pltpu.make_async_remote_copy(
                    src_ref=part(a, 2 * px + py), dst_ref=outs[a].at[:, k], send_sem=ssem.at[3 * a + k],
                    recv_sem=rsem.at[3 * a + k], device_id=(px, py, c), device_id_type=MESH)
                rc.start()
                sends.append(rc)
        for cp in sends:
            cp.wait()

    def slot_shape(s, ax):
        return (s[0], 3, s[2], s[3]) if ax == 0 else (s[0], 3, s[1], s[2] // N_CHIPS)

    out_shape = [jax.ShapeDtypeStruct(slot_shape(s, ax), h.dtype) for s, ax, h in zip(shapes, axes, halves)]
    return pl.pallas_call(
        body, name=name, in_specs=[ANY] * n, out_specs=[ANY] * n, out_shape=out_shape,
        scratch_shapes=_dma_sems(3 * n, 3 * n),
    )(*halves)


def _pair_join(name, shards):
    n = len(shards)

    def body(*refs):
        outs = refs[n:2 * n]
        ssem, rsem = refs[2 * n:]
        x, y, c = _pos()
        pending = []
        for a in range(n):
            hr = shards[a].shape[1] // 2
            mine = outs[a].at[:, pl.ds(c * hr, hr), :]
            rc = pltpu.make_async_remote_copy(src_ref=mine, dst_ref=mine, send_sem=ssem.at[a], recv_sem=rsem.at[a],
                                              device_id=(x, y, 1 - c), device_id_type=MESH)
            rc.start()
            pending.append(rc)
        for a in range(n):
            hr = shards[a].shape[1] // 2
            got = outs[a].at[:, pl.ds((1 - c) * hr, hr), :]
            pltpu.make_async_remote_copy(src_ref=got, dst_ref=got, send_sem=ssem.at[a], recv_sem=rsem.at[a],
                                         device_id=(x, y, 1 - c), device_id_type=MESH).wait_recv()
        for cp in pending:
            cp.wait_send()

    return pl.pallas_call(
        body, name=name, in_specs=[ANY] * n, out_specs=[ANY] * n,
        out_shape=[jax.ShapeDtypeStruct(s.shape, s.dtype) for s in shards],
        input_output_aliases={a: a for a in range(n)}, scratch_shapes=_dma_sems(n, n),
    )(*shards)


def _pair_add(name, part, theirs, axis, pos):
    layers, rf, cf = part.shape

    def body(pos_ref, a_ref, b_ref, o_ref):
        o_ref[...] = (a_ref[...] + b_ref[...]).astype(o_ref.dtype)

    if axis == 0:
        hr = rf // 8
        grid = (layers, N_CHIPS)
        in_specs = [pl.BlockSpec((None, hr, cf), lambda l, s, p: (l, 2 * s + p[0], 0)),
                    pl.BlockSpec((None, None, hr, cf), lambda l, s, p: (l, s, 0, 0))]
        out_spec = pl.BlockSpec((None, None, hr, cf), lambda l, s, p: (l, s, 0, 0))
    else:
        hr, t = rf // 2, 128
        grid = (layers, hr // t)
        in_specs = [pl.BlockSpec((None, t, cf), lambda l, i, p: (l, p[0] * (hr // t) + i, 0)),
                    pl.BlockSpec((None, t, cf), lambda l, i, p: (l, i, 0))]
        out_spec = pl.BlockSpec((None, t, cf), lambda l, i, p: (l, i, 0))
    return pl.pallas_call(
        body, name=name, out_shape=jax.ShapeDtypeStruct(theirs.shape, BF16),
        grid_spec=pltpu.PrefetchScalarGridSpec(num_scalar_prefetch=1, grid=grid, in_specs=in_specs, out_specs=out_spec),
        compiler_params=pltpu.CompilerParams(dimension_semantics=("arbitrary", "arbitrary")),
    )(pos, part, theirs)


def _chip_sum(name, part, theirs, slots, axis, pos):
    layers, _, hr, c = slots.shape

    def body(pos_ref, mine, sib, s0, s1, s2, o_ref):
        o_ref[...] = (((mine[...] + sib[...]) + s0[...].astype(F32)) + s1[...].astype(F32)) + s2[...].astype(F32)

    t = hr if axis == 0 else 128
    if axis == 0:
        own_specs = [pl.BlockSpec((None, t, c), lambda l, i, p: (l, 2 * p[1] + p[0], 0)),
                     pl.BlockSpec((None, None, t, c), lambda l, i, p: (l, p[1], 0, 0))]
    else:
        own_specs = [pl.BlockSpec((None, t, c), lambda l, i, p: (l, p[0] * (hr // t) + i, p[1])),
                     pl.BlockSpec((None, t, c), lambda l, i, p: (l, i, p[1]))]
    slot_specs = [pl.BlockSpec((None, None, t, c), functools.partial(lambda k, l, i, p: (l, k, i, 0), k)) for k in range(3)]
    return pl.pallas_call(
        body, name=name, out_shape=jax.ShapeDtypeStruct((layers, 2 * hr, c), F32),
        grid_spec=pltpu.PrefetchScalarGridSpec(
            num_scalar_prefetch=1, grid=(layers, hr // t), in_specs=own_specs + slot_specs,
            out_specs=pl.BlockSpec((None, t, c), lambda l, i, p: (l, p[0] * (hr // t) + i, 0))),
        compiler_params=pltpu.CompilerParams(dimension_semantics=("arbitrary", "arbitrary")),
    )(pos, part, theirs, slots, slots, slots)


def _sibling_swap(name, v):
    def body(v_ref, o_ref, ssem, rsem):
        x, y, c = _pos()
        cp = pltpu.make_async_remote_copy(src_ref=v_ref, dst_ref=o_ref, send_sem=ssem.at[0], recv_sem=rsem.at[0],
                                          device_id=(x, y, 1 - c), device_id_type=MESH)
        cp.start()
        cp.wait()

    return pl.pallas_call(body, name=name, in_specs=[ANY], out_specs=ANY, out_shape=jax.ShapeDtypeStruct(v.shape, v.dtype),
                          scratch_shapes=_dma_sems(1, 1))(v)


def _add2(name, a, b):
    shape = a.shape
    a2, b2 = a.reshape(-1, shape[-1]), b.reshape(-1, shape[-1])
    (o,) = _rowwise(name, lambda u, v: u + v, [a2, b2], out_rows=[(shape[-1], F32)], tile=512)
    return o.reshape(shape)


def _sum_slots(name, slots):
    _, hr, c = slots.shape
    t = _divisor(hr, 256)

    def body(s0, s1, s2, s3, o_ref):
        o_ref[...] = ((s0[...] + s1[...]) + s2[...]) + s3[...]

    return pl.pallas_call(
        body, name=name, grid=(hr // t,),
        in_specs=[pl.BlockSpec((None, t, c), functools.partial(lambda k, i: (k, i, 0), k)) for k in range(N_CHIPS)],
        out_specs=pl.BlockSpec((t, c), lambda i: (i, 0)), out_shape=jax.ShapeDtypeStruct((hr, c), F32),
        compiler_params=pltpu.CompilerParams(dimension_semantics=("arbitrary",)),
    )(slots, slots, slots, slots)


def _adam_tile(w, g, m, v):
    m = ADAM_B1 * m + (1.0 - ADAM_B1) * g
    v = ADAM_B2 * v + (1.0 - ADAM_B2) * (g * g)
    m_hat = m / (1.0 - ADAM_B1 ** ADAM_STEP)
    v_hat = v / (1.0 - ADAM_B2 ** ADAM_STEP)
    delta = -ADAM_LR * (m_hat / (jnp.sqrt(v_hat) + ADAM_EPS) + ADAM_WD * w)
    return delta, m, v


def _adam(name, w, g, m, v):
    shape = w.shape
    c = shape[-1]
    flat = [t.reshape(-1, c) for t in (w, g, m, v)]
    res = _rowwise(name, _adam_tile, flat, out_rows=[(c, F32)] * 3, tile=256)
    return [r.reshape(shape) for r in res]


ATT_T = 256
ATT_NEG = -1e30


def _branch_bias(length):
    nblk = length // ATT_T
    d = (np.arange(nblk)[:, None, None] * ATT_T + np.arange(ATT_T)[None, :, None] - np.arange(ATT_T)[None, None, :])
    cnt = np.zeros(d.shape, np.float32)
    for window, dil in B_DILATIONS:
        cnt += ((d >= 0) & (d % dil == 0) & (d <= window)).astype(np.float32)
    return jnp.asarray(np.where(cnt > 0, np.log(np.maximum(cnt, 1.0)), ATT_NEG).astype(np.float32))


def _rope_tables(length):
    half = B_HD // 2
    inv_freq = ROPE_THETA ** (-jnp.arange(half, dtype=F32) / half)
    ang = jnp.arange(length, dtype=F32)[:, None] * inv_freq[None, :]
    cos, sin = jnp.cos(ang), jnp.sin(ang)
    return jnp.concatenate([cos, cos], axis=1), jnp.concatenate([-sin, sin], axis=1)


def _swap_halves(t):
    return pltpu.roll(t, B_HD // 2, 1)


def _rope_qkv(name, z, cos, sin, t=256):
    bsz, length, _ = z.shape
    t = _divisor(length, t)

    def body(q_ref, k_ref, v_ref, c_ref, s_ref, qo, ko, vo):
        c, s = c_ref[...], s_ref[...]
        for src, dst in ((q_ref, qo), (k_ref, ko)):
            for h in range(B_HEADS):
                cols = slice(h * B_HD, (h + 1) * B_HD)
                xh = src[:, cols]
                dst[:, cols] = (xh * c + _swap_halves(xh) * s).astype(dst.dtype)
        vo[...] = v_ref[...].astype(vo.dtype)

    col0 = 4 * A_WIDTH // B_WIDTH
    specs = [pl.BlockSpec((None, t, B_WIDTH), functools.partial(lambda k, b, i: (b, i, col0 + k), k)) for k in range(3)]
    tab = pl.BlockSpec((t, B_HD), lambda b, i: (i, 0))
    out = pl.BlockSpec((None, t, B_WIDTH), lambda b, i: (b, i, 0))
    return pl.pallas_call(
        body, name=name, grid=(bsz, length // t), in_specs=specs + [tab, tab], out_specs=[out] * 3,
        out_shape=[jax.ShapeDtypeStruct((bsz, length, B_WIDTH), BF16)] * 3,
        compiler_params=pltpu.CompilerParams(dimension_semantics=("parallel", "parallel")),
    )(z, z, z, cos, sin)


def _dilated_fwd(name, q, k, v, cnt):
    bsz, length, _ = q.shape
    scale = B_HD ** -0.5
    nblk = length // ATT_T

    def body(cnt_ref, q_ref, k_ref, v_ref, o_ref, lse_ref):
        i = pl.program_id(2)
        qb = q_ref[...]

        def step(j, carry):
            m, l, acc = carry
            rows = pl.ds(pl.multiple_of(j * ATT_T, ATT_T), ATT_T)
            s = lax.dot_general(qb, k_ref[rows, :], (((1,), (1,)), ((), ())), preferred_element_type=F32) * scale
            s = s + cnt_ref[i - j]
            m_new = jnp.maximum(m, jnp.max(s, axis=-1, keepdims=True))
            a = jnp.exp(m - m_new)
            p = jnp.exp(s - m_new)
            l = a * l + jnp.sum(p, axis=-1, keepdims=True)
            acc = a * acc + jnp.dot(p.astype(MXU_DTYPE), v_ref[rows, :], preferred_element_type=F32)
            return m_new, l, acc

        init = (jnp.full((ATT_T, 1), ATT_NEG, F32), jnp.zeros((ATT_T, 1), F32), jnp.zeros((ATT_T, B_HD), F32))
        m, l, acc = lax.fori_loop(0, i + 1, step, init)
        o_ref[...] = acc / l
        lse_ref[...] = jnp.broadcast_to(m + jnp.log(l), (ATT_T, B_HD))

    qspec = pl.BlockSpec((None, ATT_T, B_HD), lambda b, h, i: (b, i, h))
    kspec = pl.BlockSpec((None, length, B_HD), lambda b, h, i: (b, 0, h))
    return pl.pallas_call(
        body, name=name, grid=(bsz, B_HEADS, nblk),
        in_specs=[pl.BlockSpec(cnt.shape, lambda b, h, i: (0, 0, 0)), qspec, kspec, kspec],
        out_specs=[qspec, pl.BlockSpec((None, None, ATT_T, B_HD), lambda b, h, i: (b, h, i, 0))],
        out_shape=[jax.ShapeDtypeStruct((bsz, length, B_WIDTH), F32), jax.ShapeDtypeStruct((bsz, B_HEADS, length, B_HD), F32)],
        compiler_params=pltpu.CompilerParams(dimension_semantics=("parallel", "parallel", "arbitrary")),
    )(cnt, q, k, v)


def _dilated_bwd(name, q, k, v, o, lse, do, cnt, cos, sin, off=0):
    bsz, length, _ = q.shape
    scale = B_HD ** -0.5
    nblk = length // ATT_T

    def body(cnt_ref, q_ref, k_ref, v_ref, o_ref, lse_ref, do_ref, c_ref, s_ref, dq_ref, dk_ref, dv_ref, dq_acc, dk_acc, dv_acc):
        dk_acc[...] = jnp.zeros_like(dk_acc)
        dv_acc[...] = jnp.zeros_like(dv_acc)

        def outer(i, _):
            rq = pl.ds(pl.multiple_of(i * ATT_T, ATT_T), ATT_T)
            qi, doi = q_ref[rq, :], do_ref[rq, :]
            lsei = lse_ref[rq, :][:, 0:1]
            di = jnp.sum(doi * o_ref[rq, :], axis=-1, keepdims=True)
            dob = doi.astype(MXU_DTYPE)

            def inner(j, dq):
                rk = pl.ds(pl.multiple_of(j * ATT_T, ATT_T), ATT_T)
                kj, vj = k_ref[rk, :], v_ref[rk, :]
                s = lax.dot_general(qi, kj, (((1,), (1,)), ((), ())), preferred_element_type=F32) * scale
                p = jnp.exp(s + cnt_ref[i - j] - lsei)
                dp = lax.dot_general(dob, vj, (((1,), (1,)), ((), ())), preferred_element_type=F32)
                ds = (p * (dp - di) * scale).astype(MXU_DTYPE)
                dk_acc[rk, :] += lax.dot_general(ds, qi, (((0,), (0,)), ((), ())), preferred_element_type=F32)
                dv_acc[rk, :] += lax.dot_general(p.astype(MXU_DTYPE), dob, (((0,), (0,)), ((), ())), preferred_element_type=F32)
                return dq + jnp.dot(ds, kj, preferred_element_type=F32)

            dq_acc[rq, :] = lax.fori_loop(0, i + 1, inner, jnp.zeros((ATT_T, B_HD), F32))
            return 0

        lax.fori_loop(0, nblk, outer, 0)
        c, s = c_ref[...], s_ref[...]
        for acc, dst in ((dq_acc, dq_ref), (dk_acc, dk_ref)):
            g = acc[...]
            dst[...] = (g * c + _swap_halves(g * s)).astype(dst.dtype)
        dv_ref[...] = dv_acc[...].astype(dv_ref.dtype)

    hspec = pl.BlockSpec((None, length, B_HD), lambda b, h: (b, 0, h))
    ospec = pl.BlockSpec((None, length, B_HD), lambda b, h: (b, 0, off + h))
    tab = pl.BlockSpec((length, B_HD), lambda b, h: (0, 0))
    return pl.pallas_call(
        body, name=name, grid=(bsz, B_HEADS),
        in_specs=[pl.BlockSpec(cnt.shape, lambda b, h: (0, 0, 0)), hspec, hspec, hspec, ospec,
                  pl.BlockSpec((None, None, length, B_HD), lambda b, h: (b, h, 0, 0)), ospec, tab, tab],
        out_specs=[hspec] * 3, out_shape=[jax.ShapeDtypeStruct((bsz, length, B_WIDTH), BF16)] * 3,
        scratch_shapes=[pltpu.VMEM((length, B_HD), F32)] * 3,
        compiler_params=pltpu.CompilerParams(dimension_semantics=("parallel", "parallel")),
    )(cnt, q, k, v, o, lse, do, cos, sin)


def _chunk_cumsum(t, reverse):
    n = t.shape[0]
    row = lax.broadcasted_iota(jnp.int32, t.shape, 0) & (A_CHUNK - 1)
    s = 1
    while s < A_CHUNK:
        if reverse:
            t = t + jnp.where(row < A_CHUNK - s, pltpu.roll(t, n - s, 0), 0.0)
        else:
            t = t + jnp.where(row >= s, pltpu.roll(t, s, 0), 0.0)
        s *= 2
    return t


def _hgrn_gates(fl, lb):
    sg = jax.nn.sigmoid(fl)
    f = lb + (1.0 - lb) * sg
    return sg, f


def _bmm(a, b, ca, cb):
    return lax.dot_general(a, b, (((ca,), (cb,)), ((0,), (0,))), preferred_element_type=F32)


def _hgrn_forward_chunks(nchunk, q, f, b, v_ref, st_s, dec_s):
    shape = (nchunk, A_CHUNK, A_DK)
    b3 = b.reshape(shape)
    dec = jnp.exp(b3[:, A_CHUNK - 1:A_CHUNK, :])
    dec_s[...] = dec
    qd = (q * jnp.exp(b)).reshape(shape)
    ki = ((1.0 - f) * jnp.exp(-b)).reshape(shape)
    qdb, kib, keb = qd.astype(MXU_DTYPE), ki.astype(MXU_DTYPE), (ki * dec).astype(MXU_DTYPE)
    v3 = v_ref[...].reshape(shape).astype(MXU_DTYPE)
    tri = (lax.broadcasted_iota(jnp.int32, (1, A_CHUNK, A_CHUNK), 1) >= lax.broadcasted_iota(jnp.int32, (1, A_CHUNK, A_CHUNK), 2))
    a = jnp.where(tri, _bmm(qdb, kib, 2, 2), 0.0).astype(MXU_DTYPE)
    st_s[...] = _bmm(v3, keb, 1, 1)

    def rec(n, st):
        u = st_s[n]
        st_s[n] = st
        return st * dec_s[n] + u

    lax.fori_loop(0, nchunk, rec, jnp.zeros((A_DK, A_DK), F32))
    o = _bmm(a, v3, 2, 1) + _bmm(qdb, st_s[...].astype(MXU_DTYPE), 2, 2)
    return dict(dec=dec, qd=qd, ki=ki, qdb=qdb, kib=kib, keb=keb, v3=v3, a=a, tri=tri), o


def _hgrn_fwd(name, z, lb, onw):
    bsz, length, _ = z.shape
    nchunk = length // A_CHUNK

    def body(q_ref, f_ref, v_ref, g_ref, lb_ref, w_ref, y_ref, st_s, dec_s):
        _, f = _hgrn_gates(f_ref[...], lb_ref[...])
        b = _chunk_cumsum(jnp.log(f), False)
        _, o = _hgrn_forward_chunks(nchunk, q_ref[...], f, b, v_ref, st_s, dec_s)
        o = o.reshape(length, A_DK)
        on = o * lax.rsqrt(jnp.mean(o * o, axis=-1, keepdims=True) + NORM_EPS)
        y_ref[...] = on * w_ref[...] * _silu(g_ref[...])

    cols = [pl.BlockSpec((None, length, A_DK), functools.partial(lambda k, b, h: (b, 0, k * A_HEADS + h), k)) for k in range(4)]
    vec = pl.BlockSpec((1, A_DK), lambda b, h: (0, h))
    return pl.pallas_call(
        body, name=name, grid=(bsz, A_HEADS), in_specs=cols + [vec, vec],
        out_specs=pl.BlockSpec((None, length, A_DK), lambda b, h: (b, 0, h)),
        out_shape=jax.ShapeDtypeStruct((bsz, length, A_WIDTH), F32),
        scratch_shapes=[pltpu.VMEM((nchunk, A_DK, A_DK), F32), pltpu.VMEM((nchunk, 1, A_DK), F32)],
        compiler_params=pltpu.CompilerParams(dimension_semantics=("parallel", "parallel")),
    )(z, z, z, z, lb, onw)


def _hgrn_bwd(name, z, lb, onw, dy):
    bsz, length, _ = z.shape
    nchunk = length // A_CHUNK
    shape = (nchunk, A_CHUNK, A_DK)

    def body(q_ref, f_ref, v_ref, g_ref, lb_ref, w_ref, dy_ref, dq_ref, df_ref, dv_ref, dg_ref, dlb_ref, dw_ref,
             st_s, dst_s, dec_s):
        lb = lb_ref[...]
        sg, f = _hgrn_gates(f_ref[...], lb)
        b = _chunk_cumsum(jnp.log(f), False)
        t, o = _hgrn_forward_chunks(nchunk, q_ref[...], f, b, v_ref, st_s, dec_s)
        o, g, w, dyv = o.reshape(length, A_DK), g_ref[...], w_ref[...], dy_ref[...]
        r = lax.rsqrt(jnp.mean(o * o, axis=-1, keepdims=True) + NORM_EPS)
        on = o * r
        sgg = jax.nn.sigmoid(g)
        gate = g * sgg
        dg_ref[...] = (dyv * on * w * (sgg * (1.0 + g * (1.0 - sgg)))).astype(dg_ref.dtype)
        dw = jnp.sum(dyv * on * gate, axis=0, keepdims=True)
        don = dyv * w * gate
        do = (r * (don - on * jnp.mean(don * on, axis=-1, keepdims=True))).reshape(shape).astype(MXU_DTYPE)
        da = jnp.where(t["tri"], _bmm(do, t["v3"], 2, 2), 0.0).astype(MXU_DTYPE)
        dst_s[...] = _bmm(do, t["qdb"], 1, 1)

        def rec(i, dst):
            n = nchunk - 1 - i
            u = dst_s[n]
            dst_s[n] = dst
            return dst * dec_s[n] + u

        lax.fori_loop(0, nchunk, rec, jnp.zeros((A_DK, A_DK), F32))
        dst, st = dst_s[...], st_s[...]
        dstb = dst.astype(MXU_DTYPE)
        dec, ki, qd = t["dec"], t["ki"], t["qd"]
        dv_ref[...] = (_bmm(t["a"], do, 1, 1) + _bmm(t["keb"], dstb, 2, 2)).reshape(length, A_DK).astype(dv_ref.dtype)
        dqd = _bmm(da, t["kib"], 2, 1) + _bmm(do, st.astype(MXU_DTYPE), 2, 1)
        dke = _bmm(t["v3"], dstb, 2, 1)
        dki = _bmm(da, t["qdb"], 1, 1) + dke * dec
        ddec = jnp.sum(dst * st, axis=1, keepdims=True) + jnp.sum(dke * ki, axis=1, keepdims=True)
        last = lax.broadcasted_iota(jnp.int32, (1, A_CHUNK, A_DK), 1) == A_CHUNK - 1
        db = (dqd * qd - dki * ki + jnp.where(last, ddec * dec, 0.0)).reshape(length, A_DK)
        dlf = _chunk_cumsum(db, True)
        dq_ref[...] = (dqd.reshape(length, A_DK) * jnp.exp(b)).astype(dq_ref.dtype)
        dfv = dlf / f - dki.reshape(length, A_DK) * jnp.exp(-b)
        df_ref[...] = (dfv * (1.0 - lb) * sg * (1.0 - sg)).astype(df_ref.dtype)
        dlb = jnp.sum(dfv * (1.0 - sg), axis=0, keepdims=True)
        first = pl.program_id(1) == 0

        @pl.when(first)
        def _():
            dlb_ref[...] = dlb
            dw_ref[...] = dw

        @pl.when(jnp.logical_not(first))
        def _():
            dlb_ref[...] += dlb
            dw_ref[...] += dw

    cols = [pl.BlockSpec((None, length, A_DK), functools.partial(lambda k, h, b: (b, 0, k * A_HEADS + h), k)) for k in range(4)]
    vec = pl.BlockSpec((1, A_DK), lambda h, b: (0, h))
    head = pl.BlockSpec((None, length, A_DK), lambda h, b: (b, 0, h))
    act = jax.ShapeDtypeStruct((bsz, length, A_WIDTH), BF16)
    return pl.pallas_call(
        body, name=name, grid=(A_HEADS, bsz), in_specs=cols + [vec, vec, head],
        out_specs=[head] * 4 + [vec, vec], out_shape=[act] * 4 + [jax.ShapeDtypeStruct((1, A_WIDTH), F32)] * 2,
        scratch_shapes=[pltpu.VMEM((nchunk, A_DK, A_DK), F32)] * 2 + [pltpu.VMEM((nchunk, 1, A_DK), F32)],
        compiler_params=pltpu.CompilerParams(dimension_semantics=("parallel", "arbitrary")),
    )(z, z, z, z, lb, onw, dy)


S5_SEG = 16
S5_W = 512
S5_LANES = C_GROUPS * C_STATE
S5_NB = 8
S5_CH = D_MODEL // S5_NB
S5_COLS = 2 * S5_LANES // S5_NB


def _seg_permute(t, bsz):
    n, c = t.shape
    return t.reshape(bsz, S5_SEG, n // bsz // S5_SEG, c).transpose(0, 2, 1, 3).reshape(n, c)


def _seg_unpermute(t, bsz):
    n, c = t.shape
    return t.reshape(bsz, n // bsz // S5_SEG, S5_SEG, c).transpose(0, 2, 1, 3).reshape(n, c)


def _s5_weights(lam_re, lam_im, log_dt, b_re, b_im, c_re, c_im):
    lr = jnp.minimum(lam_re, C_MIN_NEG_RE)
    li = lam_im
    dt = jnp.exp(log_dt)[:, None]
    mag = jnp.exp(dt * lr)
    ar, ai = mag * jnp.cos(dt * li), mag * jnp.sin(dt * li)
    den = lr * lr + li * li
    zr = ((ar - 1.0) * lr + ai * li) / den
    zi = (ai * lr - (ar - 1.0) * li) / den
    bbr = zr[..., None] * b_re - zi[..., None] * b_im
    bbi = zr[..., None] * b_im + zi[..., None] * b_re
    gpb = C_GROUPS // S5_NB
    eye = jnp.eye(gpb, dtype=F32)
    bb = jnp.stack([bbr, bbi]).reshape(2, S5_NB, gpb, C_STATE, C_GROUP)
    wb = jnp.einsum('ij,rbjpc->bicjpr', eye, bb).reshape(S5_NB, S5_CH, -1, S5_W, 2)
    wb = wb.transpose(0, 1, 2, 4, 3).reshape(S5_NB, S5_CH, S5_COLS)
    cc = jnp.stack([c_re, -c_im]).reshape(2, S5_NB, gpb, C_GROUP, C_STATE)
    wc = jnp.einsum('ij,rbjcp->bjpric', eye, cc).reshape(S5_NB, -1, S5_W, 2, S5_CH)
    wc = wc.transpose(0, 1, 3, 2, 4).reshape(S5_NB, S5_COLS, S5_CH)
    return ar.reshape(1, S5_LANES), ai.reshape(1, S5_LANES), wb, wc


def _scan_in_place(ref, c0, ar1, ai1, steps, reverse):
    w = S5_W
    ar = jnp.broadcast_to(ar1, (S5_SEG, w))
    ai = jnp.broadcast_to(-ai1 if reverse else ai1, (S5_SEG, w))
    zero = jnp.zeros((S5_SEG, w), F32)
    re, im = pl.ds(c0, w), pl.ds(c0 + w, w)

    def rows_of(j):
        jj = steps - 1 - j if reverse else j
        return pl.ds(pl.multiple_of(jj * S5_SEG, S5_SEG), S5_SEG)

    def local_step(j, st):
        sr, si = st
        rows = rows_of(j)
        nr = ar * sr - ai * si + ref[rows, re]
        ni = ar * si + ai * sr + ref[rows, im]
        ref[rows, re] = nr
        ref[rows, im] = ni
        return nr, ni

    er, ei = lax.fori_loop(0, steps, local_step, (zero, zero), unroll=4)
    pr, pi = ar[0:1], ai[0:1]
    for _ in range(steps.bit_length() - 1):
        pr, pi = pr * pr - pi * pi, 2.0 * pr * pi
    row = lax.broadcasted_iota(jnp.int32, (S5_SEG, w), 0)
    cr, ci = zero, zero
    inr, ini = jnp.zeros((1, w), F32), jnp.zeros((1, w), F32)
    order = list(range(S5_SEG))[::-1] if reverse else list(range(S5_SEG))
    for idx, s in enumerate(order):
        if idx:
            cr = jnp.where(row == s, inr, cr)
            ci = jnp.where(row == s, ini, ci)
        inr, ini = er[s:s + 1] + pr * inr - pi * ini, ei[s:s + 1] + pr * ini + pi * inr

    def carry_step(j, st):
        qr, qi = st
        rows = rows_of(j)
        ref[rows, re] += qr * cr - qi * ci
        ref[rows, im] += qr * ci + qi * cr
        return qr * ar - qi * ai, qr * ai + qi * ar

    lax.fori_loop(0, steps, carry_step, (ar, ai), unroll=4)


def _da_partial(x_ref, g_ref, c0, steps):
    w = S5_W
    re, im = pl.ds(c0, w), pl.ds(c0 + w, w)
    row = lax.broadcasted_iota(jnp.int32, (S5_SEG, w), 0)
    last = pl.ds((steps - 1) * S5_SEG, S5_SEG)
    xpr = jnp.where(row == 0, 0.0, pltpu.roll(x_ref[last, re], 1, 0))
    xpi = jnp.where(row == 0, 0.0, pltpu.roll(x_ref[last, im], 1, 0))
    zero = jnp.zeros((S5_SEG, w), F32)

    def step(j, st):
        pr, pi, accr, acci = st
        rows = pl.ds(pl.multiple_of(j * S5_SEG, S5_SEG), S5_SEG)
        gr, gi = g_ref[rows, re], g_ref[rows, im]
        return x_ref[rows, re], x_ref[rows, im], accr + gr * pr + gi * pi, acci + gi * pr - gr * pi

    _, _, accr, acci = lax.fori_loop(0, steps, step, (xpr, xpi, zero, zero), unroll=4)
    return accr, acci


S5_VMEM_LIMIT = 56 * 1024 * 1024


def _s5_states(name, hp, wb, wc, a_re, a_im, bsz):
    n = hp.shape[0]
    length = n // bsz
    steps = length // S5_SEG
    assert steps & (steps - 1) == 0
    nsub = S5_COLS // (2 * S5_W)

    def body(h_ref, wb_ref, wc_ref, ar_ref, ai_ref, x_ref, y_ref):
        x_ref[...] = jnp.dot(h_ref[...].astype(MXU_DTYPE), wb_ref[...], preferred_element_type=F32)
        for sub in range(nsub):
            lanes = slice(sub * S5_W, (sub + 1) * S5_W)
            _scan_in_place(x_ref, sub * 2 * S5_W, ar_ref[:, lanes], ai_ref[:, lanes], steps, False)
        y_ref[...] = jnp.dot(x_ref[...].astype(MXU_DTYPE), wc_ref[...], preferred_element_type=F32)

    chan = pl.BlockSpec((length, S5_CH), lambda b, j: (b, j))
    avec = pl.BlockSpec((1, nsub * S5_W), lambda b, j: (0, j))
    return pl.pallas_call(
        body, name=name, grid=(bsz, S5_NB),
        in_specs=[chan, pl.BlockSpec((None, S5_CH, S5_COLS), lambda b, j: (j, 0, 0)),
                  pl.BlockSpec((None, S5_COLS, S5_CH), lambda b, j: (j, 0, 0)), avec, avec],
        out_specs=[pl.BlockSpec((length, S5_COLS), lambda b, j: (b, j)), chan],
        out_shape=[jax.ShapeDtypeStruct((n, S5_NB * S5_COLS), F32), jax.ShapeDtypeStruct((n, D_MODEL), F32)],
        compiler_params=pltpu.CompilerParams(dimension_semantics=("parallel", "parallel"), vmem_limit_bytes=S5_VMEM_LIMIT),
    )(hp, wb, wc, a_re, a_im)


def _s5_states_bwd(name, dyp, xs, hp, wb, wc, a_re, a_im, bsz):
    n = hp.shape[0]
    length = n // bsz
    steps = length // S5_SEG
    nsub = S5_COLS // (2 * S5_W)

    def body(dy_ref, x_ref, h_ref, wb_ref, wc_ref, ar_ref, ai_ref, du_ref, dwb_ref, dwc_ref, da_ref, g_s):
        dy = dy_ref[...]
        g_s[...] = lax.dot_general(dy, wc_ref[...], (((1,), (1,)), ((), ())), preferred_element_type=F32)
        das = []
        for sub in range(nsub):
            lanes = slice(sub * S5_W, (sub + 1) * S5_W)
            _scan_in_place(g_s, sub * 2 * S5_W, ar_ref[:, lanes], ai_ref[:, lanes], steps, True)
            das += list(_da_partial(x_ref, g_s, sub * 2 * S5_W, steps))
        gb = g_s[...].astype(MXU_DTYPE)
        du_ref[...] = lax.dot_general(gb, wb_ref[...], (((1,), (1,)), ((), ())), preferred_element_type=F32)
        dwb = lax.dot_general(h_ref[...].astype(MXU_DTYPE), gb, (((0,), (0,)), ((), ())), preferred_element_type=F32)
        dwc = lax.dot_general(x_ref[...].astype(MXU_DTYPE), dy, (((0,), (0,)), ((), ())), preferred_element_type=F32)
        first = pl.program_id(1) == 0

        @pl.when(first)
        def _():
            dwb_ref[...] = dwb
            dwc_ref[...] = dwc
            for k, t in enumerate(das):
                da_ref[:, k * S5_W:(k + 1) * S5_W] = t

        @pl.when(jnp.logical_not(first))
        def _():
            dwb_ref[...] += dwb
            dwc_ref[...] += dwc
            for k, t in enumerate(das):
                da_ref[:, k * S5_W:(k + 1) * S5_W] += t

    chan = pl.BlockSpec((length, S5_CH), lambda j, b: (b, j))
    avec = pl.BlockSpec((1, nsub * S5_W), lambda j, b: (0, j))
    wbs = pl.BlockSpec((None, S5_CH, S5_COLS), lambda j, b: (j, 0, 0))
    wcs = pl.BlockSpec((None, S5_COLS, S5_CH), lambda j, b: (j, 0, 0))
    return pl.pallas_call(
        body, name=name, grid=(S5_NB, bsz),
        in_specs=[chan, pl.BlockSpec((length, S5_COLS), lambda j, b: (b, j)), chan, wbs, wcs, avec, avec],
        out_specs=[chan, wbs, wcs, pl.BlockSpec((S5_SEG, S5_COLS), lambda j, b: (0, j))],
        out_shape=[jax.ShapeDtypeStruct((n, D_MODEL), F32), jax.ShapeDtypeStruct(wb.shape, F32),
                   jax.ShapeDtypeStruct(wc.shape, F32), jax.ShapeDtypeStruct((S5_SEG, S5_NB * S5_COLS), F32)],
        scratch_shapes=[pltpu.VMEM((length, S5_COLS), F32)],
        compiler_params=pltpu.CompilerParams(dimension_semantics=("parallel", "arbitrary"), vmem_limit_bytes=S5_VMEM_LIMIT),
    )(dyp, xs, hp, wb, wc, a_re, a_im)


def _gelu(y):
    return 0.5 * y * (1.0 + lax.erf(y * math.sqrt(0.5)))


def _gelu_grad(y):
    return 0.5 * (1.0 + lax.erf(y * math.sqrt(0.5))) + y * jnp.exp(-0.5 * y * y) * (1.0 / math.sqrt(2.0 * math.pi))


def _s5_fwd(h, params, d_skip, bsz):
    (a_re, a_im, wb, wc), w_vjp = jax.vjp(_s5_weights, *params)
    wb, wc = wb.astype(BF16), wc.astype(BF16)
    hp = _seg_permute(h, bsz)
    xs, yc = _s5_states("s5_states_f", hp, wb, wc, a_re, a_im, bsz)
    ypre, glp = _rowwise("s5_gelu", lambda yy, uu, dd: (lambda t: (t, _gelu(t)))(yy + dd * uu), [yc, hp], [d_skip],
                         out_rows=[(D_MODEL, F32), (D_MODEL, BF16)])
    return _seg_unpermute(glp, bsz), dict(hp=hp, xs=xs, ypre=ypre, a_re=a_re, a_im=a_im, wb=wb, wc=wc, w_vjp=w_vjp)


def _s5_bwd(dgl, sv, d_skip, bsz):
    dyp, dskip, dd = _rowwise(
        "b_s5_gelu", lambda dg, yy, uu, ds: (lambda t: (t, t * ds, jnp.sum(t * uu, axis=0, keepdims=True)))(dg * _gelu_grad(yy)),
        [_seg_permute(dgl, bsz), sv["ypre"], sv["hp"]], [d_skip], out_rows=[(D_MODEL, BF16), (D_MODEL, F32)],
        out_sums=[D_MODEL])
    du, dwb, dwc, da = _s5_states_bwd("s5_states_b", dyp, sv["xs"], sv["hp"], sv["wb"], sv["wc"], sv["a_re"], sv["a_im"], bsz)
    da = jnp.sum(da, axis=0).reshape(S5_LANES // S5_W, 2, S5_W)
    dp = sv["w_vjp"]((da[:, 0].reshape(1, S5_LANES), da[:, 1].reshape(1, S5_LANES), dwb, dwc))
    return _seg_unpermute(du + dskip, bsz), dp, dd


def _pack_rows(arrays):
    rows = []
    for a in arrays:
        flat = a.reshape(-1).astype(F32)
        pad = (-flat.shape[0]) % PACK_COLS
        rows.append(jnp.pad(flat, (0, pad)).reshape(-1, PACK_COLS))
    out = jnp.concatenate(rows, axis=0)
    return jnp.pad(out, ((0, (-out.shape[0]) % 16), (0, 0)))


def _unpack_rows(packed, shapes):
    out, r = [], 0
    for s in shapes:
        size = int(np.prod(s))
        nr = -(-size // PACK_COLS)
        out.append(packed[r:r + nr].reshape(-1)[:size].reshape(s))
        r += nr
    return out


def kernel(x, mem, norm_w, mem_norm_w, ab_w_in, ab_w_out, hgrn_lb_logits, hgrn_out_norm_w, s5_lambda_re, s5_lambda_im, s5_log_dt, s5_b_re, s5_b_im, s5_c_re, s5_c_im, s5_d, s5_w_glu, xattn_wq, xattn_wkv, xattn_wo, ffn_w_in, ffn_w_out, loss_target, m_norm_w, m_mem_norm_w, m_ab_w_in, m_ab_w_out, m_hgrn_lb_logits, m_hgrn_out_norm_w, m_s5_lambda_re, m_s5_lambda_im, m_s5_log_dt, m_s5_b_re, m_s5_b_im, m_s5_c_re, m_s5_c_im, m_s5_d, m_s5_w_glu, m_xattn_wq, m_xattn_wkv, m_xattn_wo, m_ffn_w_in, m_ffn_w_out, v_norm_w, v_mem_norm_w, v_ab_w_in, v_ab_w_out, v_hgrn_lb_logits, v_hgrn_out_norm_w, v_s5_lambda_re, v_s5_lambda_im, v_s5_log_dt, v_s5_b_re, v_s5_b_im, v_s5_c_re, v_s5_c_im, v_s5_d, v_s5_w_glu, v_xattn_wq, v_xattn_wkv, v_xattn_wo, v_ffn_w_in, v_ffn_w_out):
    given = dict(locals())
    w = {n: given[n] for n in WEIGHTS}
    mom = {n: given["m_" + n] for n in WEIGHTS}
    var = {n: given["v_" + n] for n in WEIGHTS}
    bsz, length, _ = x.shape
    ntok = bsz * length
    chip = 2 * lax.axis_index("x") + lax.axis_index("y")

    big_axes = [ax for _, ax in BIG]
    full = _gather_chips("gather_weights", [w[n].astype(BF16) for n in BIG_NAMES], big_axes)
    wf = dict(zip(BIG_NAMES, full))
    small_block = jnp.concatenate([w['norm_w'].reshape(12, -1), w['s5_d'].reshape(1, -1), jnp.zeros((3, 256), F32)], axis=0)
    (small_full,) = _gather_chips("gather_norm_w", [small_block[None]], [1])
    nw = small_full[0, :12].reshape(2, 6, 1, D_MODEL)
    s5_d_full = small_full[0, 12:13]

    lb_table, lb_vjp = jax.vjp(lambda t: jnp.cumsum(jax.nn.softmax(t, axis=0), axis=0), w['hgrn_lb_logits'])
    xs = x.reshape(ntok, D_MODEL)
    mem2 = mem.reshape(bsz * MEM_LEN, D_MODEL)
    tgt = loss_target.reshape(ntok, D_MODEL)
    saved = []
    (h,) = _rowwise("norm_in", lambda a, g: _rms(a, g), [xs], [nw[0, 0]], out_rows=[(D_MODEL, BF16)])
    cur = xs
    for layer in range(2):
        sv = {"x": cur}
        if layer == 0:
            z = _mm("ab_in", h, wf['ab_w_in'][0]).reshape(bsz, length, -1)
            sv["h0"] = h
            rope_cos, rope_sin = _rope_tables(length)
            branch_cnt = _branch_bias(length)
            oa = _hgrn_fwd("hgrn_f", z, lb_table[0:1], w['hgrn_out_norm_w'])
            qr, kr, vb = _rope_qkv("rope_qkv", z, rope_cos, rope_sin)
            ob, lse = _dilated_fwd("dilated_f", qr, kr, vb, branch_cnt)
            core = jnp.concatenate([oa, ob], axis=-1).reshape(ntok, D_MODEL)
            sv.update(z=z, qr=qr, kr=kr, vb=vb, lse=lse, core=core)
            y = _mm("ab_out", core, wf['ab_w_out'][0])
        else:
            s5p = [w[n][0] for n in ('s5_lambda_re', 's5_lambda_im', 's5_log_dt', 's5_b_re', 's5_b_im', 's5_c_re', 's5_c_im')]
            gl, sv["s5"] = _s5_fwd(h, s5p, s5_d_full, bsz)
            sv["gl"] = gl
            y, sv["zga"], sv["zgb"] = _mm_gated("s5_glu", gl, wf['s5_w_glu'][0], lambda a, b: a * jax.nn.sigmoid(b), F32)
        sv["y1"] = y
        x1, h2 = _rowwise(f"resnorm_a{layer}", lambda a, b, g1, g2: (lambda s: (s, _rms(s, g2)))(a + _rms(b, g1)),
                          [cur, y], [nw[layer, 1], nw[layer, 2]], out_rows=[(D_MODEL, F32), (D_MODEL, BF16)])
        sv["x1"], sv["h2"] = x1, h2
        (mem_n,) = _rowwise(f"mem_norm{layer}", lambda a, g: _rms(a, g), [mem2], [w['mem_norm_w'][layer][None]],
                            out_rows=[(D_MODEL, BF16)])
        sv["mem_n"] = mem_n
        q = _mm(f"xq{layer}", h2, wf['xattn_wq'][layer])
        kv = _mm(f"xkv{layer}", mem_n, wf['xattn_wkv'][layer])
        sv["q"], sv["kv"] = q, kv
        o = _xattn_fwd(f"xattn_f{layer}", q.reshape(bsz, length, D_MODEL), kv.reshape(bsz, MEM_LEN, 2 * D_MODEL))
        o = o.reshape(ntok, D_MODEL)
        sv["o"] = o
        y2 = _mm(f"xo{layer}", o, wf['xattn_wo'][layer])
        sv["y2"] = y2
        x2, h4 = _rowwise(f"resnorm_b{layer}", lambda a, b, g1, g2: (lambda s: (s, _rms(s, g2)))(a + _rms(b, g1)),
                          [x1, y2], [nw[layer, 3], nw[layer, 4]], out_rows=[(D_MODEL, F32), (D_MODEL, BF16)])
        sv["x2"], sv["h4"] = x2, h4
        act, sv["za"], sv["zb"] = _mm_gated(f"ffn_in{layer}", h4, wf['ffn_w_in'][layer], lambda a, b: _silu(a) * b, BF16)
        sv["act"] = act
        y3 = _mm(f"ffn_out{layer}", act, wf['ffn_w_out'][layer])
        sv["y3"] = y3
        saved.append(sv)
        if layer == 0:
            cur, h = _rowwise("resnorm_c0", lambda a, b, g1, g2: (lambda s: (s, _rms(s, g2)))(a + _rms(b, g1)),
                              [x2, y3], [nw[0, 5], nw[1, 0]], out_rows=[(D_MODEL, F32), (D_MODEL, F32)])
    g, sq = _rowwise("loss_head", lambda a, b, t, g1: (lambda e: (e * (1.0 / D_MODEL), jnp.sum(e * e, axis=0, keepdims=True)))(a + _rms(b, g1) - t),
                     [saved[1]["x2"], saved[1]["y3"], tgt], [nw[1, 5]], out_rows=[(D_MODEL, F32)], out_sums=[D_MODEL])
    loss = lax.psum(0.5 * jnp.sum(sq) / D_MODEL, ("x", "y", "c"))

    gbig = {}
    gnw = [[None] * 6 for _ in range(2)]
    gmemnw = [None, None]
    gsmall = {}
    for layer in (1, 0):
        sv = saved[layer]
        dy3, gnw[layer][5] = _rowwise(f"b_norm5_{layer}", lambda gg, yy, g1: _rms_bwd(yy, g1, gg), [g, sv["y3"]], [nw[layer, 5]],
                                      out_rows=[(D_MODEL, BF16)], out_sums=[D_MODEL])
        dact = _mm(f"b_ffn_out_dx{layer}", dy3, wf['ffn_w_out'][layer], tb=True, out_dtype=BF16)
        gw_out = _mm(f"b_ffn_out_dw{layer}", sv["act"], dy3, ta=True)

        def swiglu_bwd(a, b, da):
            a, b = a.astype(F32), b.astype(F32)
            sg = jax.nn.sigmoid(a)
            return jnp.concatenate([da * b * (sg * (1.0 + a * (1.0 - sg))), da * (a * sg)], axis=1)

        (dzf,) = _rowwise(f"b_swiglu{layer}", swiglu_bwd, [sv["za"], sv["zb"], dact], out_rows=[(2 * D_FF, BF16)], tile=256)
        dh4 = _mm(f"b_ffn_in_dx{layer}", dzf, wf['ffn_w_in'][layer], tb=True)
        gw_in = _mm(f"b_ffn_in_dw{layer}", sv["h4"], dzf, ta=True)
        gbig.setdefault('ffn_w_out', {})[layer] = gw_out
        gbig.setdefault('ffn_w_in', {})[layer] = gw_in

        def resnorm_bwd(gg, dh, xx, yy, g_in, g_res):
            dx, dw_in = _rms_bwd(xx, g_in, dh)
            tot = gg + dx
            dy, dw_res = _rms_bwd(yy, g_res, tot)
            return tot, dy, dw_in, dw_res

        g, dy2, gnw[layer][4], gnw[layer][3] = _rowwise(
            f"b_resnorm_b{layer}", resnorm_bwd, [g, dh4, sv["x2"], sv["y2"]], [nw[layer, 4], nw[layer, 3]],
            out_rows=[(D_MODEL, F32), (D_MODEL, BF16)], out_sums=[D_MODEL, D_MODEL])
        do = _mm(f"b_xo_dx{layer}", dy2, wf['xattn_wo'][layer], tb=True)
        gbig.setdefault('xattn_wo', {})[layer] = _mm(f"b_xo_dw{layer}", sv["o"], dy2, ta=True)
        dq, dk, dv = _xattn_bwd(f"xattn_b{layer}", sv["q"].reshape(bsz, length, D_MODEL),
                                sv["kv"].reshape(bsz, MEM_LEN, 2 * D_MODEL), do.reshape(bsz, length, D_MODEL))
        dq = dq.reshape(ntok, D_MODEL)
        dkv = jnp.concatenate([dk, dv], axis=-1).reshape(bsz * MEM_LEN, 2 * D_MODEL)
        dh2 = _mm(f"b_xq_dx{layer}", dq, wf['xattn_wq'][layer], tb=True)
        gbig.setdefault('xattn_wq', {})[layer] = _mm(f"b_xq_dw{layer}", sv["h2"], dq, ta=True)
        dmem_n = _mm(f"b_xkv_dx{layer}", dkv, wf['xattn_wkv'][layer], tb=True)
        gbig.setdefault('xattn_wkv', {})[layer] = _mm(f"b_xkv_dw{layer}", sv["mem_n"], dkv, ta=True)
        (gmemnw[layer],) = _rowwise(f"b_mem_norm{layer}", lambda dd, mm_, g1: _rms_bwd(mm_, g1, dd)[1], [dmem_n, mem2],
                                    [w['mem_norm_w'][layer][None]], out_sums=[D_MODEL])

        g, dy1, gnw[layer][2], gnw[layer][1] = _rowwise(
            f"b_resnorm_a{layer}", resnorm_bwd, [g, dh2, sv["x1"], sv["y1"]], [nw[layer, 2], nw[layer, 1]],
            out_rows=[(D_MODEL, F32), (D_MODEL, F32 if layer == 1 else BF16)], out_sums=[D_MODEL, D_MODEL])
        if layer == 1:
            def gate_bwd(a, b, dd):
                sg = jax.nn.sigmoid(b.astype(F32))
                return jnp.concatenate([dd * sg, dd * a.astype(F32) * sg * (1.0 - sg)], axis=1)

            (dzg,) = _rowwise("b_s5_gate", gate_bwd, [sv["zga"], sv["zgb"], dy1], out_rows=[(2 * D_MODEL, BF16)])
            dgl = _mm("b_s5_glu_dx", dzg, wf['s5_w_glu'][0], tb=True)
            gbig['s5_w_glu'] = {0: _mm("b_s5_glu_dw", sv["gl"], dzg, ta=True)}
            dh0, dp, gsmall['s5_d'] = _s5_bwd(dgl, sv["s5"], s5_d_full, bsz)
            for n, t in zip(('s5_lambda_re', 's5_lambda_im', 's5_log_dt', 's5_b_re', 's5_b_im', 's5_c_re', 's5_c_im'), dp):
                gsmall[n] = t[None]
            g, gnw[1][0] = _rowwise("b_norm_in1", lambda gg, dh, xx, g1: (lambda r: (gg + r[0], r[1]))(_rms_bwd(xx, g1, dh)),
                                    [g, dh0, sv["x"]], [nw[1, 0]], out_rows=[(D_MODEL, F32)], out_sums=[D_MODEL])
        else:
            dcore = _mm("b_ab_out_dx", dy1, wf['ab_w_out'][0], tb=True)
            gbig['ab_w_out'] = {0: _mm("b_ab_out_dw", sv["core"], dy1, ta=True)}
            dcore = dcore.reshape(bsz, length, D_MODEL)
            core3 = sv["core"].reshape(bsz, length, D_MODEL)
            dqa, dfa, dia, dga, dlb0, gsmall['hgrn_out_norm_w'] = _hgrn_bwd("hgrn_b", sv["z"], lb_table[0:1], w['hgrn_out_norm_w'], dcore)
            dqb, dkb, dvb = _dilated_bwd("dilated_b", sv["qr"], sv["kr"], sv["vb"], core3, sv["lse"], dcore, branch_cnt,
                                         rope_cos, rope_sin, off=A_WIDTH // B_HD)
            (gsmall['hgrn_lb_logits'],) = lb_vjp(jnp.zeros_like(lb_table).at[0].set(dlb0[0]))
            dz = jnp.concatenate([dqa, dfa, dia, dga, dqb, dkb, dvb], axis=-1).reshape(ntok, -1)
            dh0 = _mm("b_ab_in_dx", dz, wf['ab_w_in'][0], tb=True)
            gbig['ab_w_in'] = {0: _mm("b_ab_in_dw", sv["h0"], dz, ta=True)}
            grad_x, gnw[0][0] = _rowwise("b_norm_in0", lambda gg, dh, xx, g1: (lambda r: (gg + r[0], r[1]))(_rms_bwd(xx, g1, dh)),
                                         [g, dh0, sv["x"]], [nw[0, 0]], out_rows=[(D_MODEL, F32)], out_sums=[D_MODEL])
    gsmall['norm_w'] = jnp.stack([jnp.concatenate(gnw[l], axis=0) for l in range(2)])
    gsmall['mem_norm_w'] = jnp.concatenate(gmemnw, axis=0)

    packed = _pack_rows([gsmall[n] for n in SMALL])
    theirs = _sibling_swap("small_swap", packed)
    chip_sum = _add2("small_pair_sum", packed, theirs)
    (all_chips,) = _gather_chips("small_gather", [chip_sum[None]], [0])
    small_sum = _sum_slots("small_sum", all_chips.reshape(N_CHIPS, packed.shape[0], PACK_COLS))
    full_shapes = [(2, 6, D_MODEL) if n == 'norm_w' else (1, D_MODEL) if n == 's5_d' else w[n].shape for n in SMALL]
    gs = dict(zip(SMALL, _unpack_rows(small_sum, full_shapes)))
    for n in SHARDED_SMALL:
        gs[n] = lax.dynamic_slice_in_dim(gs[n], chip * 256, 256, axis=gs[n].ndim - 1)

    pos = _pos_vec()
    parts = [jnp.stack([gbig[n][l] for l in sorted(gbig[n])]) for n in BIG_NAMES]
    theirs = _pair_send("grad_pair_send", parts, big_axes)
    pair = [_pair_add("grad_pair_sum_" + n, a, b, ax, pos) for (n, ax), a, b in zip(BIG, parts, theirs)]
    slots = _chip_exchange("grad_chip_exchange", pair, big_axes)
    shards = [_chip_sum("grad_chip_sum_" + n, a, b, s, ax, pos) for (n, ax), a, b, s in zip(BIG, parts, theirs, slots)]
    gfull = dict(zip(BIG_NAMES, _pair_join("grad_pair_join", shards)))

    grads, deltas, new_m, new_v = {}, {}, {}, {}
    for n in BIG_NAMES:
        grads[n] = gfull[n]
        deltas[n], new_m[n], new_v[n] = _adam("adam_" + n, w[n], gfull[n], mom[n], var[n])
    pk = [_pack_rows([t[n] for n in SMALL]) for t in (w, gs, mom, var)]
    small_out = _adam("adam_small", *pk)
    shard_shapes = [w[n].shape for n in SMALL]
    for dst, packed_out in zip((deltas, new_m, new_v), small_out):
        dst.update(zip(SMALL, _unpack_rows(packed_out, shard_shapes)))
    grads.update(gs)
    return (loss, grad_x.reshape(x.shape), *[grads[n] for n in WEIGHTS], *[deltas[n] for n in WEIGHTS],
            *[new_m[n] for n in WEIGHTS], *[new_v[n] for n in WEIGHTS])
```

```python
import functools
import math

import numpy as np
import jax
import jax.numpy as jnp
from jax import lax
from jax.experimental import pallas as pl
from jax.experimental.pallas import tpu as pltpu

F32 = jnp.float32
BF16 = jnp.bfloat16
MXU_DTYPE = jnp.bfloat16

D_MODEL = 1024
NORM_EPS = 1e-6
A_HEADS, A_DK, A_CHUNK = 4, 128, 32
A_WIDTH = A_HEADS * A_DK
B_HEADS, B_HD = 4, 128
B_WIDTH = B_HEADS * B_HD
B_DILATIONS = ((128, 1), (512, 4), (2048, 16))
ROPE_THETA = 10000.0
C_GROUP, C_GROUPS, C_STATE, C_CHUNK = 16, 64, 64, 128
C_MIN_NEG_RE = -1e-4
MEM_LEN = 256
X_HEADS = 4
X_HD = D_MODEL // X_HEADS
D_FF = 2816
ADAM_LR, ADAM_B1, ADAM_B2, ADAM_EPS, ADAM_WD, ADAM_STEP = 0.001, 0.9, 0.999, 1e-08, 0.01, 10

N_CHIPS = 4
MESH = pl.DeviceIdType.MESH
ANY = pl.BlockSpec(memory_space=pl.ANY)
_RELS = ((1, 0), (0, 1), (1, 1))

WEIGHTS = ['norm_w', 'mem_norm_w', 'ab_w_in', 'ab_w_out', 'hgrn_lb_logits', 'hgrn_out_norm_w', 's5_lambda_re',
           's5_lambda_im', 's5_log_dt', 's5_b_re', 's5_b_im', 's5_c_re', 's5_c_im', 's5_d', 's5_w_glu', 'xattn_wq',
           'xattn_wkv', 'xattn_wo', 'ffn_w_in', 'ffn_w_out']
BIG = (('ab_w_in', 1), ('ab_w_out', 0), ('s5_w_glu', 1), ('xattn_wq', 0), ('xattn_wkv', 1), ('xattn_wo', 0),
       ('ffn_w_in', 1), ('ffn_w_out', 0))
BIG_NAMES = tuple(n for n, _ in BIG)
SMALL = tuple(n for n in WEIGHTS if n not in BIG_NAMES)
SHARDED_SMALL = ('norm_w', 's5_d')
PACK_COLS = 1024


def _pos():
    return lax.axis_index("x"), lax.axis_index("y"), lax.axis_index("c")


def _flip(v, d):
    return 1 - v if d else v


def _divisor(n, want):
    for t in (want, 1024, 512, 256, 128, 64, 32, 16, 8):
        if t <= want and n % t == 0:
            return t
    return n


def _rowwise(name, fn, rows, bcasts=(), out_rows=(), out_sums=(), tile=512):
    n = rows[0].shape[0]
    t = _divisor(n, tile)
    nr, nb, no, ns = len(rows), len(bcasts), len(out_rows), len(out_sums)

    def body(*refs):
        vals = [r[...] for r in refs[:nr + nb]]
        res = fn(*vals)
        if not isinstance(res, (tuple, list)):
            res = (res,)
        outs = refs[nr + nb:]
        for k in range(no):
            outs[k][...] = res[k].astype(outs[k].dtype)
        if ns:
            first = pl.program_id(0) == 0
            for k in range(ns):
                o, val = outs[no + k], res[no + k]

                @pl.when(first)
                def _():
                    o[...] = val

                @pl.when(jnp.logical_not(first))
                def _():
                    o[...] += val

    in_specs = [pl.BlockSpec((t, r.shape[1]), lambda i: (i, 0)) for r in rows]
    in_specs += [pl.BlockSpec(b.shape, lambda i: (0, 0)) for b in bcasts]
    out_specs = [pl.BlockSpec((t, c), lambda i: (i, 0)) for c, _ in out_rows]
    out_specs += [pl.BlockSpec((1, c), lambda i: (0, 0)) for c in out_sums]
    out_shape = [jax.ShapeDtypeStruct((n, c), dt) for c, dt in out_rows]
    out_shape += [jax.ShapeDtypeStruct((1, c), F32) for c in out_sums]
    res = pl.pallas_call(
        body, name=name, grid=(n // t,), in_specs=in_specs, out_specs=out_specs, out_shape=out_shape,
        compiler_params=pltpu.CompilerParams(dimension_semantics=("arbitrary",)),
    )(*rows, *bcasts)
    return res


def _rms(x, w):
    r = lax.rsqrt(jnp.mean(x * x, axis=-1, keepdims=True) + NORM_EPS)
    return x * r * w


def _rms_bwd(x, w, dy):
    r = lax.rsqrt(jnp.mean(x * x, axis=-1, keepdims=True) + NORM_EPS)
    xh = x * r
    dxh = dy * w
    dx = r * (dxh - xh * jnp.mean(dxh * xh, axis=-1, keepdims=True))
    return dx, jnp.sum(dy * xh, axis=0, keepdims=True)


def _silu(z):
    return z * jax.nn.sigmoid(z)


MM_VMEM_BUDGET = 44 * 1024 * 1024


def _mm_tiles(m, n, k, ta, abytes, bbytes, obytes):
    tn = next(t for t in (1792, 1408, 1024, 512, 256, 128) if n % t == 0) if ta else _divisor(n, 512)
    tk = _divisor(k, 1024) if ta else (k if k <= 2816 else next(t for t in (2816, 2048, 1792, 1024, 512) if k % t == 0))
    for tm in (2816, 2048, 1408, 1024, 512, 256, 128):
        if m % tm:
            continue
        need = 2 * (tm * tk * abytes + tk * tn * bbytes + tm * tn * obytes) + 2 * tm * tn * 4
        if need <= MM_VMEM_BUDGET:
            return tm, tn, tk
    return _divisor(m, 128), tn, tk


def _mm(name, a, b, ta=False, tb=False, out_dtype=F32):
    m, k = a.shape[::-1] if ta else a.shape
    k2, n = b.shape[::-1] if tb else b.shape
    assert k == k2, (name, a.shape, b.shape)
    tm, tn, tk = _mm_tiles(m, n, k, ta, a.dtype.itemsize, b.dtype.itemsize, jnp.dtype(out_dtype).itemsize)
    nk = k // tk
    dims = (((0 if ta else 1,), (1 if tb else 0,)), ((), ()))

    def prod(a_ref, b_ref):
        return lax.dot_general(a_ref[...].astype(MXU_DTYPE), b_ref[...].astype(MXU_DTYPE), dims,
                               preferred_element_type=F32)

    def body_one(a_ref, b_ref, o_ref):
        o_ref[...] = prod(a_ref, b_ref).astype(o_ref.dtype)

    def body_acc(a_ref, b_ref, o_ref, acc):
        kk = pl.program_id(2)

        @pl.when(kk == 0)
        def _():
            acc[...] = prod(a_ref, b_ref)

        @pl.when(kk > 0)
        def _():
            acc[...] += prod(a_ref, b_ref)

        @pl.when(kk == nk - 1)
        def _():
            o_ref[...] = acc[...].astype(o_ref.dtype)

    a_spec = pl.BlockSpec((tk, tm), lambda i, j, kk: (kk, i)) if ta else pl.BlockSpec((tm, tk), lambda i, j, kk: (i, kk))
    b_spec = pl.BlockSpec((tn, tk), lambda i, j, kk: (j, kk)) if tb else pl.BlockSpec((tk, tn), lambda i, j, kk: (kk, j))
    return pl.pallas_call(
        body_one if nk == 1 else body_acc, name=name, grid=(m // tm, n // tn, nk),
        in_specs=[a_spec, b_spec], out_specs=pl.BlockSpec((tm, tn), lambda i, j, kk: (i, j)),
        out_shape=jax.ShapeDtypeStruct((m, n), out_dtype),
        scratch_shapes=[] if nk == 1 else [pltpu.VMEM((tm, tn), F32)],
        compiler_params=pltpu.CompilerParams(dimension_semantics=("parallel", "parallel", "arbitrary")),
    )(a, b)


def _mm_gated(name, h, w, gate, out_dtype, tm=2048, tn=256):
    n, k = h.shape
    f = w.shape[1] // 2
    tm, nj = _divisor(n, tm), f // tn

    def body(h_ref, wa_ref, wb_ref, act_ref, za_ref, zb_ref):
        hv = h_ref[...].astype(MXU_DTYPE)
        za = jnp.dot(hv, wa_ref[...].astype(MXU_DTYPE), preferred_element_type=F32)
        zb = jnp.dot(hv, wb_ref[...].astype(MXU_DTYPE), preferred_element_type=F32)
        act_ref[...] = gate(za, zb).astype(act_ref.dtype)
        za_ref[...] = za.astype(za_ref.dtype)
        zb_ref[...] = zb.astype(zb_ref.dtype)

    out = pl.BlockSpec((tm, tn), lambda i, j: (i, j))
    return pl.pallas_call(
        body, name=name, grid=(n // tm, nj),
        in_specs=[pl.BlockSpec((tm, k), lambda i, j: (i, 0)), pl.BlockSpec((k, tn), lambda i, j: (0, j)),
                  pl.BlockSpec((k, tn), lambda i, j: (0, j + nj))],
        out_specs=[out] * 3,
        out_shape=[jax.ShapeDtypeStruct((n, f), out_dtype), jax.ShapeDtypeStruct((n, f), BF16), jax.ShapeDtypeStruct((n, f), BF16)],
        compiler_params=pltpu.CompilerParams(dimension_semantics=("parallel", "parallel")),
    )(h, w, w)


def _xattn_fwd(name, q, kv, tq=512):
    bsz, length, _ = q.shape
    tq = _divisor(length, tq)
    scale = X_HD ** -0.5

    def body(q_ref, k_ref, v_ref, o_ref):
        qv, kk, vv = q_ref[...].astype(MXU_DTYPE), k_ref[...].astype(MXU_DTYPE), v_ref[...].astype(MXU_DTYPE)
        s = lax.dot_general(qv, kk, (((1,), (1,)), ((), ())), preferred_element_type=F32) * scale
        p = jnp.exp(s - jnp.max(s, axis=-1, keepdims=True))
        p = p / jnp.sum(p, axis=-1, keepdims=True)
        o_ref[...] = jnp.dot(p.astype(MXU_DTYPE), vv, preferred_element_type=F32).astype(o_ref.dtype)

    return pl.pallas_call(
        body, name=name, grid=(bsz, X_HEADS, length // tq),
        in_specs=[pl.BlockSpec((None, tq, X_HD), lambda b, h, i: (b, i, h)),
                  pl.BlockSpec((None, MEM_LEN, X_HD), lambda b, h, i: (b, 0, h)),
                  pl.BlockSpec((None, MEM_LEN, X_HD), lambda b, h, i: (b, 0, X_HEADS + h))],
        out_specs=pl.BlockSpec((None, tq, X_HD), lambda b, h, i: (b, i, h)),
        out_shape=jax.ShapeDtypeStruct(q.shape, BF16),
        compiler_params=pltpu.CompilerParams(dimension_semantics=("parallel", "parallel", "arbitrary")),
    )(q, kv, kv)


def _xattn_bwd(name, q, kv, do, tq=512):
    bsz, length, _ = q.shape
    tq = _divisor(length, tq)
    scale = X_HD ** -0.5

    def body(q_ref, k_ref, v_ref, do_ref, dq_ref, dk_ref, dv_ref):
        qv, kk, vv = q_ref[...].astype(MXU_DTYPE), k_ref[...].astype(MXU_DTYPE), v_ref[...].astype(MXU_DTYPE)
        dov = do_ref[...].astype(MXU_DTYPE)
        s = lax.dot_general(qv, kk, (((1,), (1,)), ((), ())), preferred_element_type=F32) * scale
        p = jnp.exp(s - jnp.max(s, axis=-1, keepdims=True))
        p = p / jnp.sum(p, axis=-1, keepdims=True)
        dp = lax.dot_general(dov, vv, (((1,), (1,)), ((), ())), preferred_element_type=F32)
        ds = p * (dp - jnp.sum(dp * p, axis=-1, keepdims=True)) * scale
        dsb = ds.astype(MXU_DTYPE)
        dq_ref[...] = jnp.dot(dsb, kk, preferred_element_type=F32).astype(dq_ref.dtype)
        dk = lax.dot_general(dsb, qv, (((0,), (0,)), ((), ())), preferred_element_type=F32)
        dv = lax.dot_general(p.astype(MXU_DTYPE), dov, (((0,), (0,)), ((), ())), preferred_element_type=F32)
        first = pl.program_id(2) == 0

        @pl.when(first)
        def _():
            dk_ref[...] = dk
            dv_ref[...] = dv

        @pl.when(jnp.logical_not(first))
        def _():
            dk_ref[...] += dk
            dv_ref[...] += dv

    qspec = pl.BlockSpec((None, tq, X_HD), lambda b, h, i: (b, i, h))
    kspec = pl.BlockSpec((None, MEM_LEN, X_HD), lambda b, h, i: (b, 0, h))
    return pl.pallas_call(
        body, name=name, grid=(bsz, X_HEADS, length // tq),
        in_specs=[qspec, kspec, pl.BlockSpec((None, MEM_LEN, X_HD), lambda b, h, i: (b, 0, X_HEADS + h)), qspec],
        out_specs=[qspec, kspec, kspec],
        out_shape=[jax.ShapeDtypeStruct(q.shape, BF16), jax.ShapeDtypeStruct((bsz, MEM_LEN, D_MODEL), F32),
                   jax.ShapeDtypeStruct((bsz, MEM_LEN, D_MODEL), F32)],
        compiler_params=pltpu.CompilerParams(dimension_semantics=("parallel", "parallel", "arbitrary")),
    )(q, kv, kv, do)


def _dma_sems(*counts):
    return [pltpu.SemaphoreType.DMA((max(c, 1),)) for c in counts]


def _gather_chips(name, blocks, axes):
    n = len(blocks)
    shapes = [b.shape for b in blocks]

    def body(*refs):
        ins, outs = refs[:n], refs[n:2 * n]
        lsem, lrsem, ssem, rsem, fssem, frsem = refs[2 * n:]
        x, y, c = _pos()
        me = 2 * x + y

        def region(a, chip, h):
            _, r, cc = shapes[a]
            hr = r // 2
            if axes[a] == 0:
                return outs[a].at[:, pl.ds(chip * r + h * hr, hr), :]
            return outs[a].at[:, pl.ds(h * hr, hr), pl.ds(chip * cc, cc)]

        def whole(a, chip):
            _, r, cc = shapes[a]
            if axes[a] == 0:
                return outs[a].at[:, pl.ds(chip * r, r), :]
            return outs[a].at[:, :, pl.ds(chip * cc, cc)]

        sends = []
        for a in range(n):
            cp = pltpu.make_async_remote_copy(src_ref=ins[a], dst_ref=whole(a, me), send_sem=lsem.at[a], recv_sem=lrsem.at[a],
                                              device_id=(x, y, 1 - c), device_id_type=MESH)
            cp.start()
            sends.append(cp)
        for a in range(n):
            hr = shapes[a][1] // 2
            for k, (dx, dy) in enumerate(_RELS):
                cp = pltpu.make_async_remote_copy(
                    src_ref=ins[a].at[:, pl.ds(c * hr, hr), :], dst_ref=region(a, me, c),
                    send_sem=ssem.at[3 * a + k], recv_sem=rsem.at[3 * a + k],
                    device_id=(_flip(x, dx), _flip(y, dy), c), device_id_type=MESH)
                cp.start()
                sends.append(cp)
        for a in range(n):
            for k, (dx, dy) in enumerate(_RELS):
                px, py = _flip(x, dx), _flip(y, dy)
                got = region(a, 2 * px + py, c)
                pltpu.make_async_remote_copy(
                    src_ref=got, dst_ref=got, send_sem=ssem.at[3 * a + k], recv_sem=rsem.at[3 * a + k],
                    device_id=(px, py, c), device_id_type=MESH).wait_recv()
                cp = pltpu.make_async_remote_copy(
                    src_ref=got, dst_ref=got, send_sem=fssem.at[3 * a + k], recv_sem=frsem.at[3 * a + k],
                    device_id=(x, y, 1 - c), device_id_type=MESH)
                cp.start()
                sends.append(cp)
        for a in range(n):
            for k, (dx, dy) in enumerate(_RELS):
                got = region(a, 2 * _flip(x, dx) + _flip(y, dy), 1 - c)
                pltpu.make_async_remote_copy(
                    src_ref=got, dst_ref=got, send_sem=fssem.at[3 * a + k], recv_sem=frsem.at[3 * a + k],
                    device_id=(x, y, 1 - c), device_id_type=MESH).wait_recv()
        for a in range(n):
            pltpu.make_async_remote_copy(src_ref=ins[a], dst_ref=whole(a, me), send_sem=lsem.at[a], recv_sem=lrsem.at[a],
                                         device_id=(x, y, 1 - c), device_id_type=MESH).wait_recv()
        for cp in sends:
            cp.wait_send()

    out_shape = [jax.ShapeDtypeStruct((l, 4 * r, c) if ax == 0 else (l, r, 4 * c), b.dtype)
                 for (l, r, c), ax, b in zip(shapes, axes, blocks)]
    return pl.pallas_call(
        body, name=name, in_specs=[ANY] * n, out_specs=[ANY] * n, out_shape=out_shape,
        scratch_shapes=_dma_sems(n, n, 3 * n, 3 * n, 3 * n, 3 * n),
    )(*blocks)


def _pos_vec():
    x, y, c = _pos()
    return jnp.stack([c, 2 * x + y]).astype(jnp.int32)


def _pair_send(name, parts, axes):
    n = len(parts)
    shapes = [p.shape for p in parts]
    ncopy = sum(4 if ax == 0 else 1 for ax in axes)

    def body(*refs):
        ins, theirs = refs[:n], refs[n:2 * n]
        ssem, rsem = refs[2 * n:]
        x, y, c = _pos()
        pending, j = [], 0
        for a in range(n):
            _, rf, _ = shapes[a]
            if axes[a] == 0:
                hr = rf // 8
                pieces = [(ins[a].at[:, pl.ds((2 * s + 1 - c) * hr, hr), :], theirs[a].at[:, s]) for s in range(N_CHIPS)]
            else:
                hr = rf // 2
                pieces = [(ins[a].at[:, pl.ds((1 - c) * hr, hr), :], theirs[a])]
            for give, give_dst in pieces:
                rc = pltpu.make_async_remote_copy(src_ref=give, dst_ref=give_dst, send_sem=ssem.at[j],
                                                  recv_sem=rsem.at[j], device_id=(x, y, 1 - c), device_id_type=MESH)
                rc.start()
                pending.append(rc)
                j += 1
        for cp in pending:
            cp.wait()

    def half_shape(s, ax):
        return (s[0], N_CHIPS, s[1] // 8, s[2]) if ax == 0 else (s[0], s[1] // 2, s[2])

    out_shape = [jax.ShapeDtypeStruct(half_shape(s, ax), p.dtype) for s, ax, p in zip(shapes, axes, parts)]
    return pl.pallas_call(
        body, name=name, in_specs=[ANY] * n, out_specs=[ANY] * n, out_shape=out_shape,
        scratch_shapes=_dma_sems(ncopy, ncopy),
    )(*parts)


def _chip_exchange(name, halves, axes):
    n = len(halves)
    shapes = [h.shape for h in halves]

    def body(*refs):
        ins, outs = refs[:n], refs[n:2 * n]
        ssem, rsem = refs[2 * n:]
        x, y, c = _pos()

        def part(a, chip):
            if axes[a] == 0:
                return ins[a].at[:, chip]
            cc = shapes[a][2] // N_CHIPS
            return ins[a].at[:, :, pl.ds(chip * cc, cc)]

        sends = []
        for a in range(n):
            for k, (dx, dy) in enumerate(_RELS):
                px, py = _flip(x, dx), _flip(y, dy)
                rc = pltpu.make_async_remote_copy(
                    src_ref=part(a, 2 * px + py), dst_ref=outs[a].at[:, k], send_sem=ssem.at[3 * a + k],
                    recv_sem=rsem.at[3 * a + k], device_id=(px, py, c), device_id_type=MESH)
                rc.start()
                sends.append(rc)
        for cp in sends:
            cp.wait()

    def slot_shape(s, ax):
        return (s[0], 3, s[2], s[3]) if ax == 0 else (s[0], 3, s[1], s[2] // N_CHIPS)

    out_shape = [jax.ShapeDtypeStruct(slot_shape(s, ax), h.dtype) for s, ax, h in zip(shapes, axes, halves)]
    return pl.pallas_call(
        body, name=name, in_specs=[ANY] * n, out_specs=[ANY] * n, out_shape=out_shape,
        scratch_shapes=_dma_sems(3 * n, 3 * n),
    )(*halves)


def _pair_join(name, shards):
    n = len(shards)

    def body(*refs):
        outs = refs[n:2 * n]
        ssem, rsem = refs[2 * n:]
        x, y, c = _pos()
        pending = []
        for a in range(n):
            hr = shards[a].shape[1] // 2
            mine = outs[a].at[:, pl.ds(c * hr, hr), :]
            rc = pltpu.make_async_remote_copy(src_ref=mine, dst_ref=mine, send_sem=ssem.at[a], recv_sem=rsem.at[a],
                                              device_id=(x, y, 1 - c), device_id_type=MESH)
            rc.start()
            pending.append(rc)
        for a in range(n):
            hr = shards[a].shape[1] // 2
            got = outs[a].at[:, pl.ds((1 - c) * hr, hr), :]
            pltpu.make_async_remote_copy(src_ref=got, dst_ref=got, send_sem=ssem.at[a], recv_sem=rsem.at[a],
                                         device_id=(x, y, 1 - c), device_id_type=MESH).wait_recv()
        for cp in pending:
            cp.wait_send()

    return pl.pallas_call(
        body, name=name, in_specs=[ANY] * n, out_specs=[ANY] * n,
        out_shape=[jax.ShapeDtypeStruct(s.shape, s.dtype) for s in shards],
        input_output_aliases={a: a for a in range(n)}, scratch_shapes=_dma_sems(n, n),
    )(*shards)


def _pair_add(name, part, theirs, axis, pos):
    layers, rf, cf = part.shape

    def body(pos_ref, a_ref, b_ref, o_ref):
        o_ref[...] = (a_ref[...] + b_ref[...]).astype(o_ref.dtype)

    if axis == 0:
        hr = rf // 8
        grid = (layers, N_CHIPS)
        in_specs = [pl.BlockSpec((None, hr, cf), lambda l, s, p: (l, 2 * s + p[0], 0)),
                    pl.BlockSpec((None, None, hr, cf), lambda l, s, p: (l, s, 0, 0))]
        out_spec = pl.BlockSpec((None, None, hr, cf), lambda l, s, p: (l, s, 0, 0))
    else:
        hr, t = rf // 2, 128
        grid = (layers, hr // t)
        in_specs = [pl.BlockSpec((None, t, cf), lambda l, i, p: (l, p[0] * (hr // t) + i, 0)),
                    pl.BlockSpec((None, t, cf), lambda l, i, p: (l, i, 0))]
        out_spec = pl.BlockSpec((None, t, cf), lambda l, i, p: (l, i, 0))
    return pl.pallas_call(
        body, name=name, out_shape=jax.ShapeDtypeStruct(theirs.shape, BF16),
        grid_spec=pltpu.PrefetchScalarGridSpec(num_scalar_prefetch=1, grid=grid, in_specs=in_specs, out_specs=out_spec),
        compiler_params=pltpu.CompilerParams(dimension_semantics=("arbitrary", "arbitrary")),
    )(pos, part, theirs)


def _chip_sum(name, part, theirs, slots, axis, pos):
    layers, _, hr, c = slots.shape

    def body(pos_ref, mine, sib, s0, s1, s2, o_ref):
        o_ref[...] = (((mine[...] + sib[...]) + s0[...].astype(F32)) + s1[...].astype(F32)) + s2[...].astype(F32)

    t = hr if axis == 0 else 128
    if axis == 0:
        own_specs = [pl.BlockSpec((None, t, c), lambda l, i, p: (l, 2 * p[1] + p[0], 0)),
                     pl.BlockSpec((None, None, t, c), lambda l, i, p: (l, p[1], 0, 0))]
    else:
        own_specs = [pl.BlockSpec((None, t, c), lambda l, i, p: (l, p[0] * (hr // t) + i, p[1])),
                     pl.BlockSpec((None, t, c), lambda l, i, p: (l, i, p[1]))]
    slot_specs = [pl.BlockSpec((None, None, t, c), functools.partial(lambda k, l, i, p: (l, k, i, 0), k)) for k in range(3)]
    return pl.pallas_call(
        body, name=name, out_shape=jax.ShapeDtypeStruct((layers, 2 * hr, c), F32),
        grid_spec=pltpu.PrefetchScalarGridSpec(
            num_scalar_prefetch=1, grid=(layers, hr // t), in_specs=own_specs + slot_specs,
            out_specs=pl.BlockSpec((None, t, c), lambda l, i, p: (l, p[0] * (hr // t) + i, 0))),
        compiler_params=pltpu.CompilerParams(dimension_semantics=("arbitrary", "arbitrary")),
    )(pos, part, theirs, slots, slots, slots)


def _sibling_swap(name, v):
    def body(v_ref, o_ref, ssem, rsem):
        x, y, c = _pos()
        cp = pltpu.make_async_remote_copy(src_ref=v_ref, dst_ref=o_ref, send_sem=ssem.at[0], recv_sem=rsem.at[0],
                                          device_id=(x, y, 1 - c), device_id_type=MESH)
        cp.start()
        cp.wait()

    return pl.pallas_call(body, name=name, in_specs=[ANY], out_specs=ANY, out_shape=jax.ShapeDtypeStruct(v.shape, v.dtype),
                          scratch_shapes=_dma_sems(1, 1))(v)


def _add2(name, a, b):
    shape = a.shape
    a2, b2 = a.reshape(-1, shape[-1]), b.reshape(-1, shape[-1])
    (o,) = _rowwise(name, lambda u, v: u + v, [a2, b2], out_rows=[(shape[-1], F32)], tile=512)
    return o.reshape(shape)


def _sum_slots(name, slots):
    _, hr, c = slots.shape
    t = _divisor(hr, 256)

    def body(s0, s1, s2, s3, o_ref):
        o_ref[...] = ((s0[...] + s1[...]) + s2[...]) + s3[...]

    return pl.pallas_call(
        body, name=name, grid=(hr // t,),
        in_specs=[pl.BlockSpec((None, t, c), functools.partial(lambda k, i: (k, i, 0), k)) for k in range(N_CHIPS)],
        out_specs=pl.BlockSpec((t, c), lambda i: (i, 0)), out_shape=jax.ShapeDtypeStruct((hr, c), F32),
        compiler_params=pltpu.CompilerParams(dimension_semantics=("arbitrary",)),
    )(slots, slots, slots, slots)


def _adam_tile(w, g, m, v):
    m = ADAM_B1 * m + (1.0 - ADAM_B1) * g
    v = ADAM_B2 * v + (1.0 - ADAM_B2) * (g * g)
    m_hat = m / (1.0 - ADAM_B1 ** ADAM_STEP)
    v_hat = v / (1.0 - ADAM_B2 ** ADAM_STEP)
    delta = -ADAM_LR * (m_hat / (jnp.sqrt(v_hat) + ADAM_EPS) + ADAM_WD * w)
    return delta, m, v


def _adam(name, w, g, m, v):
    shape = w.shape
    c = shape[-1]
    flat = [t.reshape(-1, c) for t in (w, g, m, v)]
    res = _rowwise(name, _adam_tile, flat, out_rows=[(c, F32)] * 3, tile=256)
    return [r.reshape(shape) for r in res]


ATT_T = 256
ATT_NEG = -1e30


def _branch_bias(length):
    nblk = length // ATT_T
    d = (np.arange(nblk)[:, None, None] * ATT_T + np.arange(ATT_T)[None, :, None] - np.arange(ATT_T)[None, None, :])
    cnt = np.zeros(d.shape, np.float32)
    for window, dil in B_DILATIONS:
        cnt += ((d >= 0) & (d % dil == 0) & (d <= window)).astype(np.float32)
    bias = np.where(cnt > 0, np.log(np.maximum(cnt, 1.0)), ATT_NEG).astype(np.float32)
    return jnp.asarray(np.concatenate([bias, np.full((1, ATT_T, ATT_T), ATT_NEG, np.float32)]))


def _key_block_pair(i, jj, nblk):
    j0, j1 = 2 * jj, 2 * jj + 1
    j1c = jnp.minimum(j1, nblk - 1)
    rows = [pl.ds(pl.multiple_of(j * ATT_T, ATT_T), ATT_T) for j in (j0, j1c)]
    return rows, [i - j0, jnp.where(j1 <= i, i - j1, nblk)]


def _rope_tables(length):
    half = B_HD // 2
    inv_freq = ROPE_THETA ** (-jnp.arange(half, dtype=F32) / half)
    ang = jnp.arange(length, dtype=F32)[:, None] * inv_freq[None, :]
    cos, sin = jnp.cos(ang), jnp.sin(ang)
    return jnp.concatenate([cos, cos], axis=1), jnp.concatenate([-sin, sin], axis=1)


def _swap_halves(t):
    return pltpu.roll(t, B_HD // 2, 1)


def _rope_qkv(name, z, cos, sin, t=256):
    bsz, length, _ = z.shape
    t = _divisor(length, t)

    def body(q_ref, k_ref, v_ref, c_ref, s_ref, qo, ko, vo):
        c, s = c_ref[...], s_ref[...]
        for src, dst in ((q_ref, qo), (k_ref, ko)):
            for h in range(B_HEADS):
                cols = slice(h * B_HD, (h + 1) * B_HD)
                xh = src[:, cols]
                dst[:, cols] = (xh * c + _swap_halves(xh) * s).astype(dst.dtype)
        vo[...] = v_ref[...].astype(vo.dtype)

    col0 = 4 * A_WIDTH // B_WIDTH
    specs = [pl.BlockSpec((None, t, B_WIDTH), functools.partial(lambda k, b, i: (b, i, col0 + k), k)) for k in range(3)]
    tab = pl.BlockSpec((t, B_HD), lambda b, i: (i, 0))
    out = pl.BlockSpec((None, t, B_WIDTH), lambda b, i: (b, i, 0))
    return pl.pallas_call(
        body, name=name, grid=(bsz, length // t), in_specs=specs + [tab, tab], out_specs=[out] * 3,
        out_shape=[jax.ShapeDtypeStruct((bsz, length, B_WIDTH), BF16)] * 3,
        compiler_params=pltpu.CompilerParams(dimension_semantics=("parallel", "parallel")),
    )(z, z, z, cos, sin)


def _dilated_fwd(name, q, k, v, cnt):
    bsz, length, _ = q.shape
    scale = B_HD ** -0.5
    nblk = length // ATT_T

    def body(cnt_ref, q_ref, k_ref, v_ref, o_ref, lse_ref):
        i = pl.program_id(2)
        qb = q_ref[...]

        def step(jj, carry):
            m, l, acc = carry
            rows, bias = _key_block_pair(i, jj, nblk)
            s = [lax.dot_general(qb, k_ref[r, :], (((1,), (1,)), ((), ())), preferred_element_type=F32) * scale
                 + cnt_ref[b] for r, b in zip(rows, bias)]
            m_new = jnp.maximum(m, jnp.maximum(jnp.max(s[0], axis=-1, keepdims=True), jnp.max(s[1], axis=-1, keepdims=True)))
            a = jnp.exp(m - m_new)
            p = [jnp.exp(t - m_new) for t in s]
            l = a * l + jnp.sum(p[0], axis=-1, keepdims=True) + jnp.sum(p[1], axis=-1, keepdims=True)
            acc = (a * acc + jnp.dot(p[0].astype(MXU_DTYPE), v_ref[rows[0], :], preferred_element_type=F32)
                   + jnp.dot(p[1].astype(MXU_DTYPE), v_ref[rows[1], :], preferred_element_type=F32))
            return m_new, l, acc

        init = (jnp.full((ATT_T, 1), ATT_NEG, F32), jnp.zeros((ATT_T, 1), F32), jnp.zeros((ATT_T, B_HD), F32))
        m, l, acc = lax.fori_loop(0, (i + 2) // 2, step, init)
        o_ref[...] = acc / l
        lse_ref[...] = jnp.broadcast_to(m + jnp.log(l), (ATT_T, B_HD))

    qspec = pl.BlockSpec((None, ATT_T, B_HD), lambda b, h, i: (b, i, h))
    kspec = pl.BlockSpec((None, length, B_HD), lambda b, h, i: (b, 0, h))
    return pl.pallas_call(
        body, name=name, grid=(bsz, B_HEADS, nblk),
        in_specs=[pl.BlockSpec(cnt.shape, lambda b, h, i: (0, 0, 0)), qspec, kspec, kspec],
        out_specs=[qspec, pl.BlockSpec((None, None, ATT_T, B_HD), lambda b, h, i: (b, h, i, 0))],
        out_shape=[jax.ShapeDtypeStruct((bsz, length, B_WIDTH), F32), jax.ShapeDtypeStruct((bsz, B_HEADS, length, B_HD), F32)],
        compiler_params=pltpu.CompilerParams(dimension_semantics=("parallel", "parallel", "arbitrary")),
    )(cnt, q, k, v)


def _dilated_bwd(name, q, k, v, o, lse, do, cnt, cos, sin, off=0):
    bsz, length, _ = q.shape
    scale = B_HD ** -0.5
    nblk = length // ATT_T

    def body(cnt_ref, q_ref, k_ref, v_ref, o_ref, lse_ref, do_ref, c_ref, s_ref, dq_ref, dk_ref, dv_ref, dq_acc, dk_acc, dv_acc):
        dk_acc[...] = jnp.zeros_like(dk_acc)
        dv_acc[...] = jnp.zeros_like(dv_acc)

        def outer(i, _):
            rq = pl.ds(pl.multiple_of(i * ATT_T, ATT_T), ATT_T)
            qi, doi = q_ref[rq, :], do_ref[rq, :]
            lsei = lse_ref[rq, :][:, 0:1]
            di = jnp.sum(doi * o_ref[rq, :], axis=-1, keepdims=True)
            dob = doi.astype(MXU_DTYPE)

            def inner(jj, dq):
                rows, bias = _key_block_pair(i, jj, nblk)
                for rk, b in zip(rows, bias):
                    kj, vj = k_ref[rk, :], v_ref[rk, :]
                    s = lax.dot_general(qi, kj, (((1,), (1,)), ((), ())), preferred_element_type=F32) * scale
                    p = jnp.exp(s + cnt_ref[b] - lsei)
                    dp = lax.dot_general(dob, vj, (((1,), (1,)), ((), ())), preferred_element_type=F32)
                    ds = (p * (dp - di) * scale).astype(MXU_DTYPE)
                    dk_acc[rk, :] += lax.dot_general(ds, qi, (((0,), (0,)), ((), ())), preferred_element_type=F32)
                    dv_acc[rk, :] += lax.dot_general(p.astype(MXU_DTYPE), dob, (((0,), (0,)), ((), ())), preferred_element_type=F32)
                    dq = dq + jnp.dot(ds, kj, preferred_element_type=F32)
                return dq

            dq_acc[rq, :] = lax.fori_loop(0, (i + 2) // 2, inner, jnp.zeros((ATT_T, B_HD), F32))
            return 0

        lax.fori_loop(0, nblk, outer, 0)
        c, s = c_ref[...], s_ref[...]
        for acc, dst in ((dq_acc, dq_ref), (dk_acc, dk_ref)):
            g = acc[...]
            dst[...] = (g * c + _swap_halves(g * s)).astype(dst.dtype)
        dv_ref[...] = dv_acc[...].astype(dv_ref.dtype)

    hspec = pl.BlockSpec((None, length, B_HD), lambda b, h: (b, 0, h))
    ospec = pl.BlockSpec((None, length, B_HD), lambda b, h: (b, 0, off + h))
    tab = pl.BlockSpec((length, B_HD), lambda b, h: (0, 0))
    return pl.pallas_call(
        body, name=name, grid=(bsz, B_HEADS),
        in_specs=[pl.BlockSpec(cnt.shape, lambda b, h: (0, 0, 0)), hspec, hspec, hspec, ospec,
                  pl.BlockSpec((None, None, length, B_HD), lambda b, h: (b, h, 0, 0)), ospec, tab, tab],
        out_specs=[hspec] * 3, out_shape=[jax.ShapeDtypeStruct((bsz, length, B_WIDTH), BF16)] * 3,
        scratch_shapes=[pltpu.VMEM((length, B_HD), F32)] * 3,
        compiler_params=pltpu.CompilerParams(dimension_semantics=("parallel", "parallel")),
    )(cnt, q, k, v, o, lse, do, cos, sin)


def _chunk_cumsum(t, reverse):
    n = t.shape[0]
    row = lax.broadcasted_iota(jnp.int32, t.shape, 0) & (A_CHUNK - 1)
    s = 1
    while s < A_CHUNK:
        if reverse:
            t = t + jnp.where(row < A_CHUNK - s, pltpu.roll(t, n - s, 0), 0.0)
        else:
            t = t + jnp.where(row >= s, pltpu.roll(t, s, 0), 0.0)
        s *= 2
    return t


def _hgrn_gates(fl, lb):
    sg = jax.nn.sigmoid(fl)
    f = lb + (1.0 - lb) * sg
    return sg, f


def _bmm(a, b, ca, cb):
    return lax.dot_general(a, b, (((ca,), (cb,)), ((0,), (0,))), preferred_element_type=F32)


def _hgrn_forward_chunks(nchunk, q, f, b, v_ref, st_s, dec_s):
    shape = (nchunk, A_CHUNK, A_DK)
    b3 = b.reshape(shape)
    dec = jnp.exp(b3[:, A_CHUNK - 1:A_CHUNK, :])
    dec_s[...] = dec
    qd = (q * jnp.exp(b)).reshape(shape)
    ki = ((1.0 - f) * jnp.exp(-b)).reshape(shape)
    qdb, kib, keb = qd.astype(MXU_DTYPE), ki.astype(MXU_DTYPE), (ki * dec).astype(MXU_DTYPE)
    v3 = v_ref[...].reshape(shape).astype(MXU_DTYPE)
    tri = (lax.broadcasted_iota(jnp.int32, (1, A_CHUNK, A_CHUNK), 1) >= lax.broadcasted_iota(jnp.int32, (1, A_CHUNK, A_CHUNK), 2))
    a = jnp.where(tri, _bmm(qdb, kib, 2, 2), 0.0).astype(MXU_DTYPE)
    st_s[...] = _bmm(v3, keb, 1, 1)

    def rec(n, st):
        u = st_s[n]
        st_s[n] = st
        return st * dec_s[n] + u

    lax.fori_loop(0, nchunk, rec, jnp.zeros((A_DK, A_DK), F32))
    o = _bmm(a, v3, 2, 1) + _bmm(qdb, st_s[...].astype(MXU_DTYPE), 2, 2)
    return dict(dec=dec, qd=qd, ki=ki, qdb=qdb, kib=kib, keb=keb, v3=v3, a=a, tri=tri), o


def _hgrn_fwd(name, z, lb, onw):
    bsz, length, _ = z.shape
    nchunk = length // A_CHUNK

    def body(q_ref, f_ref, v_ref, g_ref, lb_ref, w_ref, y_ref, st_s, dec_s):
        _, f = _hgrn_gates(f_ref[...], lb_ref[...])
        b = _chunk_cumsum(jnp.log(f), False)
        _, o = _hgrn_forward_chunks(nchunk, q_ref[...], f, b, v_ref, st_s, dec_s)
        o = o.reshape(length, A_DK)
        on = o * lax.rsqrt(jnp.mean(o * o, axis=-1, keepdims=True) + NORM_EPS)
        y_ref[...] = on * w_ref[...] * _silu(g_ref[...])

    cols = [pl.BlockSpec((None, length, A_DK), functools.partial(lambda k, b, h: (b, 0, k * A_HEADS + h), k)) for k in range(4)]
    vec = pl.BlockSpec((1, A_DK), lambda b, h: (0, h))
    return pl.pallas_call(
        body, name=name, grid=(bsz, A_HEADS), in_specs=cols + [vec, vec],
        out_specs=pl.BlockSpec((None, length, A_DK), lambda b, h: (b, 0, h)),
        out_shape=jax.ShapeDtypeStruct((bsz, length, A_WIDTH), F32),
        scratch_shapes=[pltpu.VMEM((nchunk, A_DK, A_DK), F32), pltpu.VMEM((nchunk, 1, A_DK), F32)],
        compiler_params=pltpu.CompilerParams(dimension_semantics=("parallel", "parallel")),
    )(z, z, z, z, lb, onw)


def _hgrn_bwd(name, z, lb, onw, dy):
    bsz, length, _ = z.shape
    nchunk = length // A_CHUNK
    shape = (nchunk, A_CHUNK, A_DK)

    def body(q_ref, f_ref, v_ref, g_ref, lb_ref, w_ref, dy_ref, dq_ref, df_ref, dv_ref, dg_ref, dlb_ref, dw_ref,
             st_s, dst_s, dec_s):
        lb = lb_ref[...]
        sg, f = _hgrn_gates(f_ref[...], lb)
        b = _chunk_cumsum(jnp.log(f), False)
        t, o = _hgrn_forward_chunks(nchunk, q_ref[...], f, b, v_ref, st_s, dec_s)
        o, g, w, dyv = o.reshape(length, A_DK), g_ref[...], w_ref[...], dy_ref[...]
        r = lax.rsqrt(jnp.mean(o * o, axis=-1, keepdims=True) + NORM_EPS)
        on = o * r
        sgg = jax.nn.sigmoid(g)
        gate = g * sgg
        dg_ref[...] = (dyv * on * w * (sgg * (1.0 + g * (1.0 - sgg)))).astype(dg_ref.dtype)
        dw = jnp.sum(dyv * on * gate, axis=0, keepdims=True)
        don = dyv * w * gate
        do = (r * (don - on * jnp.mean(don * on, axis=-1, keepdims=True))).reshape(shape).astype(MXU_DTYPE)
        da = jnp.where(t["tri"], _bmm(do, t["v3"], 2, 2), 0.0).astype(MXU_DTYPE)
        dst_s[...] = _bmm(do, t["qdb"], 1, 1)

        def rec(i, dst):
            n = nchunk - 1 - i
            u = dst_s[n]
            dst_s[n] = dst
            return dst * dec_s[n] + u

        lax.fori_loop(0, nchunk, rec, jnp.zeros((A_DK, A_DK), F32))
        dst, st = dst_s[...], st_s[...]
        dstb = dst.astype(MXU_DTYPE)
        dec, ki, qd = t["dec"], t["ki"], t["qd"]
        dv_ref[...] = (_bmm(t["a"], do, 1, 1) + _bmm(t["keb"], dstb, 2, 2)).reshape(length, A_DK).astype(dv_ref.dtype)
        dqd = _bmm(da, t["kib"], 2, 1) + _bmm(do, st.astype(MXU_DTYPE), 2, 1)
        dke = _bmm(t["v3"], dstb, 2, 1)
        dki = _bmm(da, t["qdb"], 1, 1) + dke * dec
        ddec = jnp.sum(dst * st, axis=1, keepdims=True) + jnp.sum(dke * ki, axis=1, keepdims=True)
        last = lax.broadcasted_iota(jnp.int32, (1, A_CHUNK, A_DK), 1) == A_CHUNK - 1
        db = (dqd * qd - dki * ki + jnp.where(last, ddec * dec, 0.0)).reshape(length, A_DK)
        dlf = _chunk_cumsum(db, True)
        dq_ref[...] = (dqd.reshape(length, A_DK) * jnp.exp(b)).astype(dq_ref.dtype)
        dfv = dlf / f - dki.reshape(length, A_DK) * jnp.exp(-b)
        df_ref[...] = (dfv * (1.0 - lb) * sg * (1.0 - sg)).astype(df_ref.dtype)
        dlb = jnp.sum(dfv * (1.0 - sg), axis=0, keepdims=True)
        first = pl.program_id(1) == 0

        @pl.when(first)
        def _():
            dlb_ref[...] = dlb
            dw_ref[...] = dw

        @pl.when(jnp.logical_not(first))
        def _():
            dlb_ref[...] += dlb
            dw_ref[...] += dw

    cols = [pl.BlockSpec((None, length, A_DK), functools.partial(lambda k, h, b: (b, 0, k * A_HEADS + h), k)) for k in range(4)]
    vec = pl.BlockSpec((1, A_DK), lambda h, b: (0, h))
    head = pl.BlockSpec((None, length, A_DK), lambda h, b: (b, 0, h))
    act = jax.ShapeDtypeStruct((bsz, length, A_WIDTH), BF16)
    return pl.pallas_call(
        body, name=name, grid=(A_HEADS, bsz), in_specs=cols + [vec, vec, head],
        out_specs=[head] * 4 + [vec, vec], out_shape=[act] * 4 + [jax.ShapeDtypeStruct((1, A_WIDTH), F32)] * 2,
        scratch_shapes=[pltpu.VMEM((nchunk, A_DK, A_DK), F32)] * 2 + [pltpu.VMEM((nchunk, 1, A_DK), F32)],
        compiler_params=pltpu.CompilerParams(dimension_semantics=("parallel", "arbitrary")),
    )(z, z, z, z, lb, onw, dy)


S5_SEG = 16
S5_W = 512
S5_LANES = C_GROUPS * C_STATE
S5_NB = 8
S5_CH = D_MODEL // S5_NB
S5_COLS = 2 * S5_LANES // S5_NB


def _seg_permute(t, bsz):
    n, c = t.shape
    return t.reshape(bsz, S5_SEG, n // bsz // S5_SEG, c).transpose(0, 2, 1, 3).reshape(n, c)


def _seg_unpermute(t, bsz):
    n, c = t.shape
    return t.reshape(bsz, n // bsz // S5_SEG, S5_SEG, c).transpose(0, 2, 1, 3).reshape(n, c)


def _s5_weights(lam_re, lam_im, log_dt, b_re, b_im, c_re, c_im):
    lr = jnp.minimum(lam_re, C_MIN_NEG_RE)
    li = lam_im
    dt = jnp.exp(log_dt)[:, None]
    mag = jnp.exp(dt * lr)
    ar, ai = mag * jnp.cos(dt * li), mag * jnp.sin(dt * li)
    den = lr * lr + li * li
    zr = ((ar - 1.0) * lr + ai * li) / den
    zi = (ai * lr - (ar - 1.0) * li) / den
    bbr = zr[..., None] * b_re - zi[..., None] * b_im
    bbi = zr[..., None] * b_im + zi[..., None] * b_re
    gpb = C_GROUPS // S5_NB
    eye = jnp.eye(gpb, dtype=F32)
    bb = jnp.stack([bbr, bbi]).reshape(2, S5_NB, gpb, C_STATE, C_GROUP)
    wb = jnp.einsum('ij,rbjpc->bicjpr', eye, bb).reshape(S5_NB, S5_CH, -1, S5_W, 2)
    wb = wb.transpose(0, 1, 2, 4, 3).reshape(S5_NB, S5_CH, S5_COLS)
    cc = jnp.stack([c_re, -c_im]).reshape(2, S5_NB, gpb, C_GROUP, C_STATE)
    wc = jnp.einsum('ij,rbjcp->bjpric', eye, cc).reshape(S5_NB, -1, S5_W, 2, S5_CH)
    wc = wc.transpose(0, 1, 3, 2, 4).reshape(S5_NB, S5_COLS, S5_CH)
    return ar.reshape(1, S5_LANES), ai.reshape(1, S5_LANES), wb, wc


def _scan_in_place(ref, c0, ar1, ai1, steps, reverse):
    w = S5_W
    ar = jnp.broadcast_to(ar1, (S5_SEG, w))
    ai = jnp.broadcast_to(-ai1 if reverse else ai1, (S5_SEG, w))
    zero = jnp.zeros((S5_SEG, w), F32)
    re, im = pl.ds(c0, w), pl.ds(c0 + w, w)

    def rows_of(j):
        jj = steps - 1 - j if reverse else j
        return pl.ds(pl.multiple_of(jj * S5_SEG, S5_SEG), S5_SEG)

    def local_step(j, st):
        sr, si = st
        rows = rows_of(j)
        nr = ar * sr - ai * si + ref[rows, re]
        ni = ar * si + ai * sr + ref[rows, im]
        ref[rows, re] = nr
        ref[rows, im] = ni
        return nr, ni

    er, ei = lax.fori_loop(0, steps, local_step, (zero, zero), unroll=4)
    pr, pi = ar[0:1], ai[0:1]
    for _ in range(steps.bit_length() - 1):
        pr, pi = pr * pr - pi * pi, 2.0 * pr * pi
    row = lax.broadcasted_iota(jnp.int32, (S5_SEG, w), 0)
    cr, ci = zero, zero
    inr, ini = jnp.zeros((1, w), F32), jnp.zeros((1, w), F32)
    order = list(range(S5_SEG))[::-1] if reverse else list(range(S5_SEG))
    for idx, s in enumerate(order):
        if idx:
            cr = jnp.where(row == s, inr, cr)
            ci = jnp.where(row == s, ini, ci)
        inr, ini = er[s:s + 1] + pr * inr - pi * ini, ei[s:s + 1] + pr * ini + pi * inr

    def carry_step(j, st):
        qr, qi = st
        rows = rows_of(j)
        ref[rows, re] += qr * cr - qi * ci
        ref[rows, im] += qr * ci + qi * cr
        return qr * ar - qi * ai, qr * ai + qi * ar

    lax.fori_loop(0, steps, carry_step, (ar, ai), unroll=4)


def _da_partial(x_ref, g_ref, c0, steps):
    w = S5_W
    re, im = pl.ds(c0, w), pl.ds(c0 + w, w)
    row = lax.broadcasted_iota(jnp.int32, (S5_SEG, w), 0)
    last = pl.ds((steps - 1) * S5_SEG, S5_SEG)
    xpr = jnp.where(row == 0, 0.0, pltpu.roll(x_ref[last, re], 1, 0))
    xpi = jnp.where(row == 0, 0.0, pltpu.roll(x_ref[last, im], 1, 0))
    zero = jnp.zeros((S5_SEG, w), F32)

    def step(j, st):
        pr, pi, accr, acci = st
        rows = pl.ds(pl.multiple_of(j * S5_SEG, S5_SEG), S5_SEG)
        gr, gi = g_ref[rows, re], g_ref[rows, im]
        return x_ref[rows, re], x_ref[rows, im], accr + gr * pr + gi * pi, acci + gi * pr - gr * pi

    _, _, accr, acci = lax.fori_loop(0, steps, step, (xpr, xpi, zero, zero), unroll=4)
    return accr, acci


S5_VMEM_LIMIT = 56 * 1024 * 1024


def _s5_states(name, hp, wb, wc, a_re, a_im, bsz):
    n = hp.shape[0]
    length = n // bsz
    steps = length // S5_SEG
    assert steps & (steps - 1) == 0
    nsub = S5_COLS // (2 * S5_W)

    def body(h_ref, wb_ref, wc_ref, ar_ref, ai_ref, x_ref, y_ref):
        x_ref[...] = jnp.dot(h_ref[...].astype(MXU_DTYPE), wb_ref[...], preferred_element_type=F32)
        for sub in range(nsub):
            lanes = slice(sub * S5_W, (sub + 1) * S5_W)
            _scan_in_place(x_ref, sub * 2 * S5_W, ar_ref[:, lanes], ai_ref[:, lanes], steps, False)
        y_ref[...] = jnp.dot(x_ref[...].astype(MXU_DTYPE), wc_ref[...], preferred_element_type=F32)

    chan = pl.BlockSpec((length, S5_CH), lambda b, j: (b, j))
    avec = pl.BlockSpec((1, nsub * S5_W), lambda b, j: (0, j))
    return pl.pallas_call(
        body, name=name, grid=(bsz, S5_NB),
        in_specs=[chan, pl.BlockSpec((None, S5_CH, S5_COLS), lambda b, j: (j, 0, 0)),
                  pl.BlockSpec((None, S5_COLS, S5_CH), lambda b, j: (j, 0, 0)), avec, avec],
        out_specs=[pl.BlockSpec((length, S5_COLS), lambda b, j: (b, j)), chan],
        out_shape=[jax.ShapeDtypeStruct((n, S5_NB * S5_COLS), F32), jax.ShapeDtypeStruct((n, D_MODEL), F32)],
        compiler_params=pltpu.CompilerParams(dimension_semantics=("parallel", "parallel"), vmem_limit_bytes=S5_VMEM_LIMIT),
    )(hp, wb, wc, a_re, a_im)


def _s5_states_bwd(name, dyp, xs, hp, wb, wc, a_re, a_im, bsz):
    n = hp.shape[0]
    length = n // bsz
    steps = length // S5_SEG
    nsub = S5_COLS // (2 * S5_W)

    def body(dy_ref, x_ref, h_ref, wb_ref, wc_ref, ar_ref, ai_ref, du_ref, dwb_ref, dwc_ref, da_ref, g_s):
        dy = dy_ref[...]
        g_s[...] = lax.dot_general(dy, wc_ref[...], (((1,), (1,)), ((), ())), preferred_element_type=F32)
        das = []
        for sub in range(nsub):
            lanes = slice(sub * S5_W, (sub + 1) * S5_W)
            _scan_in_place(g_s, sub * 2 * S5_W, ar_ref[:, lanes], ai_ref[:, lanes], steps, True)
            das += list(_da_partial(x_ref, g_s, sub * 2 * S5_W, steps))
        gb = g_s[...].astype(MXU_DTYPE)
        du_ref[...] = lax.dot_general(gb, wb_ref[...], (((1,), (1,)), ((), ())), preferred_element_type=F32)
        dwb = lax.dot_general(h_ref[...].astype(MXU_DTYPE), gb, (((0,), (0,)), ((), ())), preferred_element_type=F32)
        dwc = lax.dot_general(x_ref[...].astype(MXU_DTYPE), dy, (((0,), (0,)), ((), ())), preferred_element_type=F32)
        first = pl.program_id(1) == 0

        @pl.when(first)
        def _():
            dwb_ref[...] = dwb
            dwc_ref[...] = dwc
            for k, t in enumerate(das):
                da_ref[:, k * S5_W:(k + 1) * S5_W] = t

        @pl.when(jnp.logical_not(first))
        def _():
            dwb_ref[...] += dwb
            dwc_ref[...] += dwc
            for k, t in enumerate(das):
                da_ref[:, k * S5_W:(k + 1) * S5_W] += t

    chan = pl.BlockSpec((length, S5_CH), lambda j, b: (b, j))
    avec = pl.BlockSpec((1, nsub * S5_W), lambda j, b: (0, j))
    wbs = pl.BlockSpec((None, S5_CH, S5_COLS), lambda j, b: (j, 0, 0))
    wcs = pl.BlockSpec((None, S5_COLS, S5_CH), lambda j, b: (j, 0, 0))
    return pl.pallas_call(
        body, name=name, grid=(S5_NB, bsz),
        in_specs=[chan, pl.BlockSpec((length, S5_COLS), lambda j, b: (b, j)), chan, wbs, wcs, avec, avec],
        out_specs=[chan, wbs, wcs, pl.BlockSpec((S5_SEG, S5_COLS), lambda j, b: (0, j))],
        out_shape=[jax.ShapeDtypeStruct((n, D_MODEL), F32), jax.ShapeDtypeStruct(wb.shape, F32),
                   jax.ShapeDtypeStruct(wc.shape, F32), jax.ShapeDtypeStruct((S5_SEG, S5_NB * S5_COLS), F32)],
        scratch_shapes=[pltpu.VMEM((length, S5_COLS), F32)],
        compiler_params=pltpu.CompilerParams(dimension_semantics=("parallel", "arbitrary"), vmem_limit_bytes=S5_VMEM_LIMIT),
    )(dyp, xs, hp, wb, wc, a_re, a_im)


def _gelu(y):
    return 0.5 * y * (1.0 + lax.erf(y * math.sqrt(0.5)))


def _gelu_grad(y):
    return 0.5 * (1.0 + lax.erf(y * math.sqrt(0.5))) + y * jnp.exp(-0.5 * y * y) * (1.0 / math.sqrt(2.0 * math.pi))


def _s5_fwd(h, params, d_skip, bsz):
    (a_re, a_im, wb, wc), w_vjp = jax.vjp(_s5_weights, *params)
    wb, wc = wb.astype(BF16), wc.astype(BF16)
    hp = _seg_permute(h, bsz)
    xs, yc = _s5_states("s5_states_f", hp, wb, wc, a_re, a_im, bsz)
    ypre, glp = _rowwise("s5_gelu", lambda yy, uu, dd: (lambda t: (t, _gelu(t)))(yy + dd * uu), [yc, hp], [d_skip],
                         out_rows=[(D_MODEL, F32), (D_MODEL, BF16)])
    return _seg_unpermute(glp, bsz), dict(hp=hp, xs=xs, ypre=ypre, a_re=a_re, a_im=a_im, wb=wb, wc=wc, w_vjp=w_vjp)


def _s5_bwd(dgl, sv, d_skip, bsz):
    dyp, dskip, dd = _rowwise(
        "b_s5_gelu", lambda dg, yy, uu, ds: (lambda t: (t, t * ds, jnp.sum(t * uu, axis=0, keepdims=True)))(dg * _gelu_grad(yy)),
        [_seg_permute(dgl, bsz), sv["ypre"], sv["hp"]], [d_skip], out_rows=[(D_MODEL, BF16), (D_MODEL, F32)],
        out_sums=[D_MODEL])
    du, dwb, dwc, da = _s5_states_bwd("s5_states_b", dyp, sv["xs"], sv["hp"], sv["wb"], sv["wc"], sv["a_re"], sv["a_im"], bsz)
    da = jnp.sum(da, axis=0).reshape(S5_LANES // S5_W, 2, S5_W)
    dp = sv["w_vjp"]((da[:, 0].reshape(1, S5_LANES), da[:, 1].reshape(1, S5_LANES), dwb, dwc))
    return _seg_unpermute(du + dskip, bsz), dp, dd


def _pack_rows(arrays):
    rows = []
    for a in arrays:
        flat = a.reshape(-1).astype(F32)
        pad = (-flat.shape[0]) % PACK_COLS
        rows.append(jnp.pad(flat, (0, pad)).reshape(-1, PACK_COLS))
    out = jnp.concatenate(rows, axis=0)
    return jnp.pad(out, ((0, (-out.shape[0]) % 16), (0, 0)))


def _unpack_rows(packed, shapes):
    out, r = [], 0
    for s in shapes:
        size = int(np.prod(s))
        nr = -(-size // PACK_COLS)
        out.append(packed[r:r + nr].reshape(-1)[:size].reshape(s))
        r += nr
    return out


def kernel(x, mem, norm_w, mem_norm_w, ab_w_in, ab_w_out, hgrn_lb_logits, hgrn_out_norm_w, s5_lambda_re, s5_lambda_im, s5_log_dt, s5_b_re, s5_b_im, s5_c_re, s5_c_im, s5_d, s5_w_glu, xattn_wq, xattn_wkv, xattn_wo, ffn_w_in, ffn_w_out, loss_target, m_norm_w, m_mem_norm_w, m_ab_w_in, m_ab_w_out, m_hgrn_lb_logits, m_hgrn_out_norm_w, m_s5_lambda_re, m_s5_lambda_im, m_s5_log_dt, m_s5_b_re, m_s5_b_im, m_s5_c_re, m_s5_c_im, m_s5_d, m_s5_w_glu, m_xattn_wq, m_xattn_wkv, m_xattn_wo, m_ffn_w_in, m_ffn_w_out, v_norm_w, v_mem_norm_w, v_ab_w_in, v_ab_w_out, v_hgrn_lb_logits, v_hgrn_out_norm_w, v_s5_lambda_re, v_s5_lambda_im, v_s5_log_dt, v_s5_b_re, v_s5_b_im, v_s5_c_re, v_s5_c_im, v_s5_d, v_s5_w_glu, v_xattn_wq, v_xattn_wkv, v_xattn_wo, v_ffn_w_in, v_ffn_w_out):
    given = dict(locals())
    w = {n: given[n] for n in WEIGHTS}
    mom = {n: given["m_" + n] for n in WEIGHTS}
    var = {n: given["v_" + n] for n in WEIGHTS}
    bsz, length, _ = x.shape
    ntok = bsz * length
    chip = 2 * lax.axis_index("x") + lax.axis_index("y")

    big_axes = [ax for _, ax in BIG]
    full = _gather_chips("gather_weights", [w[n].astype(BF16) for n in BIG_NAMES], big_axes)
    wf = dict(zip(BIG_NAMES, full))
    small_block = jnp.concatenate([w['norm_w'].reshape(12, -1), w['s5_d'].reshape(1, -1), jnp.zeros((3, 256), F32)], axis=0)
    (small_full,) = _gather_chips("gather_norm_w", [small_block[None]], [1])
    nw = small_full[0, :12].reshape(2, 6, 1, D_MODEL)
    s5_d_full = small_full[0, 12:13]

    lb_table, lb_vjp = jax.vjp(lambda t: jnp.cumsum(jax.nn.softmax(t, axis=0), axis=0), w['hgrn_lb_logits'])
    xs = x.reshape(ntok, D_MODEL)
    mem2 = mem.reshape(bsz * MEM_LEN, D_MODEL)
    tgt = loss_target.reshape(ntok, D_MODEL)
    saved = []
    (h,) = _rowwise("norm_in", lambda a, g: _rms(a, g), [xs], [nw[0, 0]], out_rows=[(D_MODEL, BF16)])
    cur = xs
    for layer in range(2):
        sv = {"x": cur}
        if layer == 0:
            z = _mm("ab_in", h, wf['ab_w_in'][0]).reshape(bsz, length, -1)
            sv["h0"] = h
            rope_cos, rope_sin = _rope_tables(length)
            branch_cnt = _branch_bias(length)
            oa = _hgrn_fwd("hgrn_f", z, lb_table[0:1], w['hgrn_out_norm_w'])
            qr, kr, vb = _rope_qkv("rope_qkv", z, rope_cos, rope_sin)
            ob, lse = _dilated_fwd("dilated_f", qr, kr, vb, branch_cnt)
            core = jnp.concatenate([oa, ob], axis=-1).reshape(ntok, D_MODEL)
            sv.update(z=z, qr=qr, kr=kr, vb=vb, lse=lse, core=core)
            y = _mm("ab_out", core, wf['ab_w_out'][0])
        else:
            s5p = [w[n][0] for n in ('s5_lambda_re', 's5_lambda_im', 's5_log_dt', 's5_b_re', 's5_b_im', 's5_c_re', 's5_c_im')]
            gl, sv["s5"] = _s5_fwd(h, s5p, s5_d_full, bsz)
            sv["gl"] = gl
            y, sv["zga"], sv["zgb"] = _mm_gated("s5_glu", gl, wf['s5_w_glu'][0], lambda a, b: a * jax.nn.sigmoid(b), F32)
        sv["y1"] = y
        x1, h2 = _rowwise(f"resnorm_a{layer}", lambda a, b, g1, g2: (lambda s: (s, _rms(s, g2)))(a + _rms(b, g1)),
                          [cur, y], [nw[layer, 1], nw[layer, 2]], out_rows=[(D_MODEL, F32), (D_MODEL, BF16)])
        sv["x1"], sv["h2"] = x1, h2
        (mem_n,) = _rowwise(f"mem_norm{layer}", lambda a, g: _rms(a, g), [mem2], [w['mem_norm_w'][layer][None]],
                            out_rows=[(D_MODEL, BF16)])
        sv["mem_n"] = mem_n
        q = _mm(f"xq{layer}", h2, wf['xattn_wq'][layer])
        kv = _mm(f"xkv{layer}", mem_n, wf['xattn_wkv'][layer])
        sv["q"], sv["kv"] = q, kv
        o = _xattn_fwd(f"xattn_f{layer}", q.reshape(bsz, length, D_MODEL), kv.reshape(bsz, MEM_LEN, 2 * D_MODEL))
        o = o.reshape(ntok, D_MODEL)
        sv["o"] = o
        y2 = _mm(f"xo{layer}", o, wf['xattn_wo'][layer])
        sv["y2"] = y2
        x2, h4 = _rowwise(f"resnorm_b{layer}", lambda a, b, g1, g2: (lambda s: (s, _rms(s, g2)))(a + _rms(b, g1)),
                          [x1, y2], [nw[layer, 3], nw[layer, 4]], out_rows=[(D_MODEL, F32), (D_MODEL, BF16)])
        sv["x2"], sv["h4"] = x2, h4
        act, sv["za"], sv["zb"] = _mm_gated(f"ffn_in{layer}", h4, wf['ffn_w_in'][layer], lambda a, b: _silu(a) * b, BF16)
        sv["act"] = act
        y3 = _mm(f"ffn_out{layer}", act, wf['ffn_w_out'][layer])
        sv["y3"] = y3
        saved.append(sv)
        if layer == 0:
            cur, h = _rowwise("resnorm_c0", lambda a, b, g1, g2: (lambda s: (s, _rms(s, g2)))(a + _rms(b, g1)),
                              [x2, y3], [nw[0, 5], nw[1, 0]], out_rows=[(D_MODEL, F32), (D_MODEL, F32)])
    g, sq = _rowwise("loss_head", lambda a, b, t, g1: (lambda e: (e * (1.0 / D_MODEL), jnp.sum(e * e, axis=0, keepdims=True)))(a + _rms(b, g1) - t),
                     [saved[1]["x2"], saved[1]["y3"], tgt], [nw[1, 5]], out_rows=[(D_MODEL, F32)], out_sums=[D_MODEL])
    loss = lax.psum(0.5 * jnp.sum(sq) / D_MODEL, ("x", "y", "c"))

    gbig = {}
    gnw = [[None] * 6 for _ in range(2)]
    gmemnw = [None, None]
    gsmall = {}
    for layer in (1, 0):
        sv = saved[layer]
        dy3, gnw[layer][5] = _rowwise(f"b_norm5_{layer}", lambda gg, yy, g1: _rms_bwd(yy, g1, gg), [g, sv["y3"]], [nw[layer, 5]],
                                      out_rows=[(D_MODEL, BF16)], out_sums=[D_MODEL])
        dact = _mm(f"b_ffn_out_dx{layer}", dy3, wf['ffn_w_out'][layer], tb=True, out_dtype=BF16)
        gw_out = _mm(f"b_ffn_out_dw{layer}", sv["act"], dy3, ta=True)

        def swiglu_bwd(a, b, da):
            a, b = a.astype(F32), b.astype(F32)
            sg = jax.nn.sigmoid(a)
            return jnp.concatenate([da * b * (sg * (1.0 + a * (1.0 - sg))), da * (a * sg)], axis=1)

        (dzf,) = _rowwise(f"b_swiglu{layer}", swiglu_bwd, [sv["za"], sv["zb"], dact], out_rows=[(2 * D_FF, BF16)], tile=256)
        dh4 = _mm(f"b_ffn_in_dx{layer}", dzf, wf['ffn_w_in'][layer], tb=True)
        gw_in = _mm(f"b_ffn_in_dw{layer}", sv["h4"], dzf, ta=True)
        gbig.setdefault('ffn_w_out', {})[layer] = gw_out
        gbig.setdefault('ffn_w_in', {})[layer] = gw_in

        def resnorm_bwd(gg, dh, xx, yy, g_in, g_res):
            dx, dw_in = _rms_bwd(xx, g_in, dh)
            tot = gg + dx
            dy, dw_res = _rms_bwd(yy, g_res, tot)
            return tot, dy, dw_in, dw_res

        g, dy2, gnw[layer][4], gnw[layer][3] = _rowwise(
            f"b_resnorm_b{layer}", resnorm_bwd, [g, dh4, sv["x2"], sv["y2"]], [nw[layer, 4], nw[layer, 3]],
            out_rows=[(D_MODEL, F32), (D_MODEL, BF16)], out_sums=[D_MODEL, D_MODEL])
        do = _mm(f"b_xo_dx{layer}", dy2, wf['xattn_wo'][layer], tb=True)
        gbig.setdefault('xattn_wo', {})[layer] = _mm(f"b_xo_dw{layer}", sv["o"], dy2, ta=True)
        dq, dk, dv = _xattn_bwd(f"xattn_b{layer}", sv["q"].reshape(bsz, length, D_MODEL),
                                sv["kv"].reshape(bsz, MEM_LEN, 2 * D_MODEL), do.reshape(bsz, length, D_MODEL))
        dq = dq.reshape(ntok, D_MODEL)
        dkv = jnp.concatenate([dk, dv], axis=-1).reshape(bsz * MEM_LEN, 2 * D_MODEL)
        dh2 = _mm(f"b_xq_dx{layer}", dq, wf['xattn_wq'][layer], tb=True)
        gbig.setdefault('xattn_wq', {})[layer] = _mm(f"b_xq_dw{layer}", sv["h2"], dq, ta=True)
        dmem_n = _mm(f"b_xkv_dx{layer}", dkv, wf['xattn_wkv'][layer], tb=True)
        gbig.setdefault('xattn_wkv', {})[layer] = _mm(f"b_xkv_dw{layer}", sv["mem_n"], dkv, ta=True)
        (gmemnw[layer],) = _rowwise(f"b_mem_norm{layer}", lambda dd, mm_, g1: _rms_bwd(mm_, g1, dd)[1], [dmem_n, mem2],
                                    [w['mem_norm_w'][layer][None]], out_sums=[D_MODEL])

        g, dy1, gnw[layer][2], gnw[layer][1] = _rowwise(
            f"b_resnorm_a{layer}", resnorm_bwd, [g, dh2, sv["x1"], sv["y1"]], [nw[layer, 2], nw[layer, 1]],
            out_rows=[(D_MODEL, F32), (D_MODEL, F32 if layer == 1 else BF16)], out_sums=[D_MODEL, D_MODEL])
        if layer == 1:
            def gate_bwd(a, b, dd):
                sg = jax.nn.sigmoid(b.astype(F32))
                return jnp.concatenate([dd * sg, dd * a.astype(F32) * sg * (1.0 - sg)], axis=1)

            (dzg,) = _rowwise("b_s5_gate", gate_bwd, [sv["zga"], sv["zgb"], dy1], out_rows=[(2 * D_MODEL, BF16)])
            dgl = _mm("b_s5_glu_dx", dzg, wf['s5_w_glu'][0], tb=True)
            gbig['s5_w_glu'] = {0: _mm("b_s5_glu_dw", sv["gl"], dzg, ta=True)}
            dh0, dp, gsmall['s5_d'] = _s5_bwd(dgl, sv["s5"], s5_d_full, bsz)
            for n, t in zip(('s5_lambda_re', 's5_lambda_im', 's5_log_dt', 's5_b_re', 's5_b_im', 's5_c_re', 's5_c_im'), dp):
                gsmall[n] = t[None]
            g, gnw[1][0] = _rowwise("b_norm_in1", lambda gg, dh, xx, g1: (lambda r: (gg + r[0], r[1]))(_rms_bwd(xx, g1, dh)),
                                    [g, dh0, sv["x"]], [nw[1, 0]], out_rows=[(D_MODEL, F32)], out_sums=[D_MODEL])
        else:
            dcore = _mm("b_ab_out_dx", dy1, wf['ab_w_out'][0], tb=True)
            gbig['ab_w_out'] = {0: _mm("b_ab_out_dw", sv["core"], dy1, ta=True)}
            dcore = dcore.reshape(bsz, length, D_MODEL)
            core3 = sv["core"].reshape(bsz, length, D_MODEL)
            dqa, dfa, dia, dga, dlb0, gsmall['hgrn_out_norm_w'] = _hgrn_bwd("hgrn_b", sv["z"], lb_table[0:1], w['hgrn_out_norm_w'], dcore)
            dqb, dkb, dvb = _dilated_bwd("dilated_b", sv["qr"], sv["kr"], sv["vb"], core3, sv["lse"], dcore, branch_cnt,
                                         rope_cos, rope_sin, off=A_WIDTH // B_HD)
            (gsmall['hgrn_lb_logits'],) = lb_vjp(jnp.zeros_like(lb_table).at[0].set(dlb0[0]))
            dz = jnp.concatenate([dqa, dfa, dia, dga, dqb, dkb, dvb], axis=-1).reshape(ntok, -1)
            dh0 = _mm("b_ab_in_dx", dz, wf['ab_w_in'][0], tb=True)
            gbig['ab_w_in'] = {0: _mm("b_ab_in_dw", sv["h0"], dz, ta=True)}
            grad_x, gnw[0][0] = _rowwise("b_norm_in0", lambda gg, dh, xx, g1: (lambda r: (gg + r[0], r[1]))(_rms_bwd(xx, g1, dh)),
                                         [g, dh0, sv["x"]], [nw[0, 0]], out_rows=[(D_MODEL, F32)], out_sums=[D_MODEL])
    gsmall['norm_w'] = jnp.stack([jnp.concatenate(gnw[l], axis=0) for l in range(2)])
    gsmall['mem_norm_w'] = jnp.concatenate(gmemnw, axis=0)

    packed = _pack_rows([gsmall[n] for n in SMALL])
    theirs = _sibling_swap("small_swap", packed)
    chip_sum = _add2("small_pair_sum", packed, theirs)
    (all_chips,) = _gather_chips("small_gather", [chip_sum[None]], [0])
    small_sum = _sum_slots("small_sum", all_chips.reshape(N_CHIPS, packed.shape[0], PACK_COLS))
    full_shapes = [(2, 6, D_MODEL) if n == 'norm_w' else (1, D_MODEL) if n == 's5_d' else w[n].shape for n in SMALL]
    gs = dict(zip(SMALL, _unpack_rows(small_sum, full_shapes)))
    for n in SHARDED_SMALL:
        gs[n] = lax.dynamic_slice_in_dim(gs[n], chip * 256, 256, axis=gs[n].ndim - 1)

    pos = _pos_vec()
    parts = [jnp.stack([gbig[n][l] for l in sorted(gbig[n])]) for n in BIG_NAMES]
    theirs = _pair_send("grad_pair_send", parts, big_axes)
    pair = [_pair_add("grad_pair_sum_" + n, a, b, ax, pos) for (n, ax), a, b in zip(BIG, parts, theirs)]
    slots = _chip_exchange("grad_chip_exchange", pair, big_axes)
    shards = [_chip_sum("grad_chip_sum_" + n, a, b, s, ax, pos) for (n, ax), a, b, s in zip(BIG, parts, theirs, slots)]
    gfull = dict(zip(BIG_NAMES, _pair_join("grad_pair_join", shards)))

    grads, deltas, new_m, new_v = {}, {}, {}, {}
    for n in BIG_NAMES:
        grads[n] = gfull[n]
        deltas[n], new_m[n], new_v[n] = _adam("adam_" + n, w[n], gfull[n], mom[n], var[n])
    pk = [_pack_rows([t[n] for n in SMALL]) for t in (w, gs, mom, var)]
    small_out = _adam("adam_small", *pk)
    shard_shapes = [w[n].shape for n in SMALL]
    for dst, packed_out in zip((deltas, new_m, new_v), small_out):
        dst.update(zip(SMALL, _unpack_rows(packed_out, shard_shapes)))
    grads.update(gs)
    return (loss, grad_x.reshape(x.shape), *[grads[n] for n in WEIGHTS], *[deltas[n] for n in WEIGHTS],
            *[new_m[n] for n in WEIGHTS], *[new_v[n] for n in WEIGHTS])
```

```python
import functools
import math

import numpy as np
import jax
import jax.numpy as jnp
from jax import lax
from jax.experimental import pallas as pl
from jax.experimental.pallas import tpu as pltpu

F32 = jnp.float32
BF16 = jnp.bfloat16
MXU_DTYPE = jnp.bfloat16

D_MODEL = 1024
NORM_EPS = 1e-6
A_HEADS, A_DK, A_CHUNK = 4, 128, 32
A_WIDTH = A_HEADS * A_DK
B_HEADS, B_HD = 4, 128
B_WIDTH = B_HEADS * B_HD
B_DILATIONS = ((128, 1), (512, 4), (2048, 16))
ROPE_THETA = 10000.0
C_GROUP, C_GROUPS, C_STATE, C_CHUNK = 16, 64, 64, 128
C_MIN_NEG_RE = -1e-4
MEM_LEN = 256
X_HEADS = 4
X_HD = D_MODEL // X_HEADS
D_FF = 2816
ADAM_LR, ADAM_B1, ADAM_B2, ADAM_EPS, ADAM_WD, ADAM_STEP = 0.001, 0.9, 0.999, 1e-08, 0.01, 10

N_CHIPS = 4
MESH = pl.DeviceIdType.MESH
ANY = pl.BlockSpec(memory_space=pl.ANY)
_RELS = ((1, 0), (0, 1), (1, 1))

WEIGHTS = ['norm_w', 'mem_norm_w', 'ab_w_in', 'ab_w_out', 'hgrn_lb_logits', 'hgrn_out_norm_w', 's5_lambda_re',
           's5_lambda_im', 's5_log_dt', 's5_b_re', 's5_b_im', 's5_c_re', 's5_c_im', 's5_d', 's5_w_glu', 'xattn_wq',
           'xattn_wkv', 'xattn_wo', 'ffn_w_in', 'ffn_w_out']
BIG = (('ab_w_in', 1), ('ab_w_out', 0), ('s5_w_glu', 1), ('xattn_wq', 0), ('xattn_wkv', 1), ('xattn_wo', 0),
       ('ffn_w_in', 1), ('ffn_w_out', 0))
BIG_NAMES = tuple(n for n, _ in BIG)
SMALL = tuple(n for n in WEIGHTS if n not in BIG_NAMES)
SHARDED_SMALL = ('norm_w', 's5_d')
PACK_COLS = 1024


def _pos():
    return lax.axis_index("x"), lax.axis_index("y"), lax.axis_index("c")


def _flip(v, d):
    return 1 - v if d else v


def _divisor(n, want):
    for t in (want, 1024, 512, 256, 128, 64, 32, 16, 8):
        if t <= want and n % t == 0:
            return t
    return n


def _rowwise(name, fn, rows, bcasts=(), out_rows=(), out_sums=(), tile=512):
    n = rows[0].shape[0]
    t = _divisor(n, tile)
    nr, nb, no, ns = len(rows), len(bcasts), len(out_rows), len(out_sums)

    def body(*refs):
        vals = [r[...] for r in refs[:nr + nb]]
        res = fn(*vals)
        if not isinstance(res, (tuple, list)):
            res = (res,)
        outs = refs[nr + nb:]
        for k in range(no):
            outs[k][...] = res[k].astype(outs[k].dtype)
        if ns:
            first = pl.program_id(0) == 0
            for k in range(ns):
                o, val = outs[no + k], res[no + k]

                @pl.when(first)
                def _():
                    o[...] = val

                @pl.when(jnp.logical_not(first))
                def _():
                    o[...] += val

    in_specs = [pl.BlockSpec((t, r.shape[1]), lambda i: (i, 0)) for r in rows]
    in_specs += [pl.BlockSpec(b.shape, lambda i: (0, 0)) for b in bcasts]
    out_specs = [pl.BlockSpec((t, c), lambda i: (i, 0)) for c, _ in out_rows]
    out_specs += [pl.BlockSpec((1, c), lambda i: (0, 0)) for c in out_sums]
    out_shape = [jax.ShapeDtypeStruct((n, c), dt) for c, dt in out_rows]
    out_shape += [jax.ShapeDtypeStruct((1, c), F32) for c in out_sums]
    res = pl.pallas_call(
        body, name=name, grid=(n // t,), in_specs=in_specs, out_specs=out_specs, out_shape=out_shape,
        compiler_params=pltpu.CompilerParams(dimension_semantics=("arbitrary",)),
    )(*rows, *bcasts)
    return res


def _rms(x, w):
    r = lax.rsqrt(jnp.mean(x * x, axis=-1, keepdims=True) + NORM_EPS)
    return x * r * w


def _rms_bwd(x, w, dy):
    r = lax.rsqrt(jnp.mean(x * x, axis=-1, keepdims=True) + NORM_EPS)
    xh = x * r
    dxh = dy * w
    dx = r * (dxh - xh * jnp.mean(dxh * xh, axis=-1, keepdims=True))
    return dx, jnp.sum(dy * xh, axis=0, keepdims=True)


def _silu(z):
    return z * jax.nn.sigmoid(z)


MM_VMEM_BUDGET = 44 * 1024 * 1024


def _mm_tiles(m, n, k, ta, abytes, bbytes, obytes):
    tn = next(t for t in (1792, 1408, 1024, 512, 256, 128) if n % t == 0) if ta else _divisor(n, 512)
    tk = _divisor(k, 1024) if ta else (k if k <= 2816 else next(t for t in (2816, 2048, 1792, 1024, 512) if k % t == 0))
    for tm in (2816, 2048, 1408, 1024, 512, 256, 128):
        if m % tm:
            continue
        need = 2 * (tm * tk * abytes + tk * tn * bbytes + tm * tn * obytes) + 2 * tm * tn * 4
        if need <= MM_VMEM_BUDGET:
            return tm, tn, tk
    return _divisor(m, 128), tn, tk


def _mm(name, a, b, ta=False, tb=False, out_dtype=F32):
    m, k = a.shape[::-1] if ta else a.shape
    k2, n = b.shape[::-1] if tb else b.shape
    assert k == k2, (name, a.shape, b.shape)
    tm, tn, tk = _mm_tiles(m, n, k, ta, a.dtype.itemsize, b.dtype.itemsize, jnp.dtype(out_dtype).itemsize)
    nk = k // tk
    dims = (((0 if ta else 1,), (1 if tb else 0,)), ((), ()))

    def prod(a_ref, b_ref):
        return lax.dot_general(a_ref[...].astype(MXU_DTYPE), b_ref[...].astype(MXU_DTYPE), dims,
                               preferred_element_type=F32)

    def body_one(a_ref, b_ref, o_ref):
        o_ref[...] = prod(a_ref, b_ref).astype(o_ref.dtype)

    def body_acc(a_ref, b_ref, o_ref, acc):
        kk = pl.program_id(2)

        @pl.when(kk == 0)
        def _():
            acc[...] = prod(a_ref, b_ref)

        @pl.when(kk > 0)
        def _():
            acc[...] += prod(a_ref, b_ref)

        @pl.when(kk == nk - 1)
        def _():
            o_ref[...] = acc[...].astype(o_ref.dtype)

    a_spec = pl.BlockSpec((tk, tm), lambda i, j, kk: (kk, i)) if ta else pl.BlockSpec((tm, tk), lambda i, j, kk: (i, kk))
    b_spec = pl.BlockSpec((tn, tk), lambda i, j, kk: (j, kk)) if tb else pl.BlockSpec((tk, tn), lambda i, j, kk: (kk, j))
    return pl.pallas_call(
        body_one if nk == 1 else body_acc, name=name, grid=(m // tm, n // tn, nk),
        in_specs=[a_spec, b_spec], out_specs=pl.BlockSpec((tm, tn), lambda i, j, kk: (i, j)),
        out_shape=jax.ShapeDtypeStruct((m, n), out_dtype),
        scratch_shapes=[] if nk == 1 else [pltpu.VMEM((tm, tn), F32)],
        compiler_params=pltpu.CompilerParams(dimension_semantics=("parallel", "parallel", "arbitrary")),
    )(a, b)


def _mm_gated(name, h, w, gate, out_dtype, tm=2048, tn=256):
    n, k = h.shape
    f = w.shape[1] // 2
    tm, nj = _divisor(n, tm), f // tn

    def body(h_ref, wa_ref, wb_ref, act_ref, za_ref, zb_ref):
        hv = h_ref[...].astype(MXU_DTYPE)
        za = jnp.dot(hv, wa_ref[...].astype(MXU_DTYPE), preferred_element_type=F32)
        zb = jnp.dot(hv, wb_ref[...].astype(MXU_DTYPE), preferred_element_type=F32)
        act_ref[...] = gate(za, zb).astype(act_ref.dtype)
        za_ref[...] = za.astype(za_ref.dtype)
        zb_ref[...] = zb.astype(zb_ref.dtype)

    out = pl.BlockSpec((tm, tn), lambda i, j: (i, j))
    return pl.pallas_call(
        body, name=name, grid=(n // tm, nj),
        in_specs=[pl.BlockSpec((tm, k), lambda i, j: (i, 0)), pl.BlockSpec((k, tn), lambda i, j: (0, j)),
                  pl.BlockSpec((k, tn), lambda i, j: (0, j + nj))],
        out_specs=[out] * 3,
        out_shape=[jax.ShapeDtypeStruct((n, f), out_dtype), jax.ShapeDtypeStruct((n, f), BF16), jax.ShapeDtypeStruct((n, f), BF16)],
        compiler_params=pltpu.CompilerParams(dimension_semantics=("parallel", "parallel")),
    )(h, w, w)


def _xattn_fwd(name, q, kv, tq=1024):
    bsz, length, _ = q.shape
    tq = _divisor(length, tq)
    scale = X_HD ** -0.5

    def body(q_ref, k_ref, v_ref, o_ref):
        qv, kk, vv = q_ref[...].astype(MXU_DTYPE), k_ref[...].astype(MXU_DTYPE), v_ref[...].astype(MXU_DTYPE)
        s = lax.dot_general(qv, kk, (((1,), (1,)), ((), ())), preferred_element_type=F32) * scale
        p = jnp.exp(s - jnp.max(s, axis=-1, keepdims=True))
        p = p / jnp.sum(p, axis=-1, keepdims=True)
        o_ref[...] = jnp.dot(p.astype(MXU_DTYPE), vv, preferred_element_type=F32).astype(o_ref.dtype)

    return pl.pallas_call(
        body, name=name, grid=(bsz, X_HEADS, length // tq),
        in_specs=[pl.BlockSpec((None, tq, X_HD), lambda b, h, i: (b, i, h)),
                  pl.BlockSpec((None, MEM_LEN, X_HD), lambda b, h, i: (b, 0, h)),
                  pl.BlockSpec((None, MEM_LEN, X_HD), lambda b, h, i: (b, 0, X_HEADS + h))],
        out_specs=pl.BlockSpec((None, tq, X_HD), lambda b, h, i: (b, i, h)),
        out_shape=jax.ShapeDtypeStruct(q.shape, BF16),
        compiler_params=pltpu.CompilerParams(dimension_semantics=("parallel", "parallel", "arbitrary")),
    )(q, kv, kv)


def _xattn_bwd(name, q, kv, do, tq=1024):
    bsz, length, _ = q.shape
    tq = _divisor(length, tq)
    scale = X_HD ** -0.5

    def body(q_ref, k_ref, v_ref, do_ref, dq_ref, dk_ref, dv_ref):
        qv, kk, vv = q_ref[...].astype(MXU_DTYPE), k_ref[...].astype(MXU_DTYPE), v_ref[...].astype(MXU_DTYPE)
        dov = do_ref[...].astype(MXU_DTYPE)
        s = lax.dot_general(qv, kk, (((1,), (1,)), ((), ())), preferred_element_type=F32) * scale
        p = jnp.exp(s - jnp.max(s, axis=-1, keepdims=True))
        p = p / jnp.sum(p, axis=-1, keepdims=True)
        dp = lax.dot_general(dov, vv, (((1,), (1,)), ((), ())), preferred_element_type=F32)
        ds = p * (dp - jnp.sum(dp * p, axis=-1, keepdims=True)) * scale
        dsb = ds.astype(MXU_DTYPE)
        dq_ref[...] = jnp.dot(dsb, kk, preferred_element_type=F32).astype(dq_ref.dtype)
        dk = lax.dot_general(dsb, qv, (((0,), (0,)), ((), ())), preferred_element_type=F32)
        dv = lax.dot_general(p.astype(MXU_DTYPE), dov, (((0,), (0,)), ((), ())), preferred_element_type=F32)
        first = pl.program_id(2) == 0

        @pl.when(first)
        def _():
            dk_ref[...] = dk
            dv_ref[...] = dv

        @pl.when(jnp.logical_not(first))
        def _():
            dk_ref[...] += dk
            dv_ref[...] += dv

    qspec = pl.BlockSpec((None, tq, X_HD), lambda b, h, i: (b, i, h))
    kspec = pl.BlockSpec((None, MEM_LEN, X_HD), lambda b, h, i: (b, 0, h))
    return pl.pallas_call(
        body, name=name, grid=(bsz, X_HEADS, length // tq),
        in_specs=[qspec, kspec, pl.BlockSpec((None, MEM_LEN, X_HD), lambda b, h, i: (b, 0, X_HEADS + h)), qspec],
        out_specs=[qspec, kspec, kspec],
        out_shape=[jax.ShapeDtypeStruct(q.shape, BF16), jax.ShapeDtypeStruct((bsz, MEM_LEN, D_MODEL), F32),
                   jax.ShapeDtypeStruct((bsz, MEM_LEN, D_MODEL), F32)],
        compiler_params=pltpu.CompilerParams(dimension_semantics=("parallel", "parallel", "arbitrary")),
    )(q, kv, kv, do)


def _dma_sems(*counts):
    return [pltpu.SemaphoreType.DMA((max(c, 1),)) for c in counts]


def _gather_chips(name, blocks, axes):
    n = len(blocks)
    shapes = [b.shape for b in blocks]

    def body(*refs):
        ins, outs = refs[:n], refs[n:2 * n]
        lsem, lrsem, ssem, rsem, fssem, frsem = refs[2 * n:]
        x, y, c = _pos()
        me = 2 * x + y

        def region(a, chip, h):
            _, r, cc = shapes[a]
            hr = r // 2
            if axes[a] == 0:
                return outs[a].at[:, pl.ds(chip * r + h * hr, hr), :]
            return outs[a].at[:, pl.ds(h * hr, hr), pl.ds(chip * cc, cc)]

        def whole(a, chip):
            _, r, cc = shapes[a]
            if axes[a] == 0:
                return outs[a].at[:, pl.ds(chip * r, r), :]
            return outs[a].at[:, :, pl.ds(chip * cc, cc)]

        sends = []
        for a in range(n):
            cp = pltpu.make_async_remote_copy(src_ref=ins[a], dst_ref=whole(a, me), send_sem=lsem.at[a], recv_sem=lrsem.at[a],
                                              device_id=(x, y, 1 - c), device_id_type=MESH)
            cp.start()
            sends.append(cp)
        for a in range(n):
            hr = shapes[a][1] // 2
            for k, (dx, dy) in enumerate(_RELS):
                cp = pltpu.make_async_remote_copy(
                    src_ref=ins[a].at[:, pl.ds(c * hr, hr), :], dst_ref=region(a, me, c),
                    send_sem=ssem.at[3 * a + k], recv_sem=rsem.at[3 * a + k],
                    device_id=(_flip(x, dx), _flip(y, dy), c), device_id_type=MESH)
                cp.start()
                sends.append(cp)
        for a in range(n):
            for k, (dx, dy) in enumerate(_RELS):
                px, py = _flip(x, dx), _flip(y, dy)
                got = region(a, 2 * px + py, c)
                pltpu.make_async_remote_copy(
                    src_ref=got, dst_ref=got, send_sem=ssem.at[3 * a + k], recv_sem=rsem.at[3 * a + k],
                    device_id=(px, py, c), device_id_type=MESH).wait_recv()
                cp = pltpu.make_async_remote_copy(
                    src_ref=got, dst_ref=got, send_sem=fssem.at[3 * a + k], recv_sem=frsem.at[3 * a + k],
                    device_id=(x, y, 1 - c), device_id_type=MESH)
                cp.start()
                sends.append(cp)
        for a in range(n):
            for k, (dx, dy) in enumerate(_RELS):
                got = region(a, 2 * _flip(x, dx) + _flip(y, dy), 1 - c)
                pltpu.make_async_remote_copy(
                    src_ref=got, dst_ref=got, send_sem=fssem.at[3 * a + k], recv_sem=frsem.at[3 * a + k],
                    device_id=(x, y, 1 - c), device_id_type=MESH).wait_recv()
        for a in range(n):
            pltpu.make_async_remote_copy(src_ref=ins[a], dst_ref=whole(a, me), send_sem=lsem.at[a], recv_sem=lrsem.at[a],
                                         device_id=(x, y, 1 - c), device_id_type=MESH).wait_recv()
        for cp in sends:
            cp.wait_send()

    out_shape = [jax.ShapeDtypeStruct((l, 4 * r, c) if ax == 0 else (l, r, 4 * c), b.dtype)
                 for (l, r, c), ax, b in zip(shapes, axes, blocks)]
    return pl.pallas_call(
        body, name=name, in_specs=[ANY] * n, out_specs=[ANY] * n, out_shape=out_shape,
        scratch_shapes=_dma_sems(n, n, 3 * n, 3 * n, 3 * n, 3 * n),
    )(*blocks)


def _pos_vec():
    x, y, c = _pos()
    return jnp.stack([c, 2 * x + y]).astype(jnp.int32)


def _pair_send(name, parts, axes):
    n = len(parts)
    shapes = [p.shape for p in parts]
    ncopy = sum(4 if ax == 0 else 1 for ax in axes)

    def body(*refs):
        ins, theirs = refs[:n], refs[n:2 * n]
        ssem, rsem = refs[2 * n:]
        x, y, c = _pos()
        pending, j = [], 0
        for a in range(n):
            _, rf, _ = shapes[a]
            if axes[a] == 0:
                hr = rf // 8
                pieces = [(ins[a].at[:, pl.ds((2 * s + 1 - c) * hr, hr), :], theirs[a].at[:, s]) for s in range(N_CHIPS)]
            else:
                hr = rf // 2
                pieces = [(ins[a].at[:, pl.ds((1 - c) * hr, hr), :], theirs[a])]
            for give, give_dst in pieces:
                rc = pltpu.make_async_remote_copy(src_ref=give, dst_ref=give_dst, send_sem=ssem.at[j],
                                                  recv_sem=rsem.at[j], device_id=(x, y, 1 - c), device_id_type=MESH)
                rc.start()
                pending.append(rc)
                j += 1
        for cp in pending:
            cp.wait()

    def half_shape(s, ax):
        return (s[0], N_CHIPS, s[1] // 8, s[2]) if ax == 0 else (s[0], s[1] // 2, s[2])

    out_shape = [jax.ShapeDtypeStruct(half_shape(s, ax), p.dtype) for s, ax, p in zip(shapes, axes, parts)]
    return pl.pallas_call(
        body, name=name, in_specs=[ANY] * n, out_specs=[ANY] * n, out_shape=out_shape,
        scratch_shapes=_dma_sems(ncopy, ncopy),
    )(*parts)


def _chip_exchange(name, halves, axes):
    n = len(halves)
    shapes = [h.shape for h in halves]

    def body(*refs):
        ins, outs = refs[:n], refs[n:2 * n]
        ssem, rsem = refs[2 * n:]
        x, y, c = _pos()

        def part(a, chip):
            if axes[a] == 0:
                return ins[a].at[:, chip]
            cc = shapes[a][2] // N_CHIPS
            return ins[a].at[:, :, pl.ds(chip * cc, cc)]

        sends = []
        for a in range(n):
            for k, (dx, dy) in enumerate(_RELS):
                px, py = _flip(x, dx), _flip(y, dy)
                rc = pltpu.make_async_remote_copy(
                    src_ref=part(a, 2 * px + py), dst_ref=outs[a].at[:, k], send_sem=ssem.at[3 * a + k],
                    recv_sem=rsem.at[3 * a + k], device_id=(px, py, c), device_id_type=MESH)
                rc.start()
                sends.append(rc)
        for cp in sends:
            cp.wait()

    def slot_shape(s, ax):
        return (s[0], 3, s[2], s[3]) if ax == 0 else (s[0], 3, s[1], s[2] // N_CHIPS)

    out_shape = [jax.ShapeDtypeStruct(slot_shape(s, ax), h.dtype) for s, ax, h in zip(shapes, axes, halves)]
    return pl.pallas_call(
        body, name=name, in_specs=[ANY] * n, out_specs=[ANY] * n, out_shape=out_shape,
        scratch_shapes=_dma_sems(3 * n, 3 * n),
    )(*halves)


def _pair_join(name, shards):
    n = len(shards)

    def body(*refs):
        outs = refs[n:2 * n]
        ssem, rsem = refs[2 * n:]
        x, y, c = _pos()
        pending = []
        for a in range(n):
            hr = shards[a].shape[1] // 2
            mine = outs[a].at[:, pl.ds(c * hr, hr), :]
            rc = pltpu.make_async_remote_copy(src_ref=mine, dst_ref=mine, send_sem=ssem.at[a], recv_sem=rsem.at[a],
                                              device_id=(x, y, 1 - c), device_id_type=MESH)
            rc.start()
            pending.append(rc)
        for a in range(n):
            hr = shards[a].shape[1] // 2
            got = outs[a].at[:, pl.ds((1 - c) * hr, hr), :]
            pltpu.make_async_remote_copy(src_ref=got, dst_ref=got, send_sem=ssem.at[a], recv_sem=rsem.at[a],
                                         device_id=(x, y, 1 - c), device_id_type=MESH).wait_recv()
        for cp in pending:
            cp.wait_send()

    return pl.pallas_call(
        body, name=name, in_specs=[ANY] * n, out_specs=[ANY] * n,
        out_shape=[jax.ShapeDtypeStruct(s.shape, s.dtype) for s in shards],
        input_output_aliases={a: a for a in range(n)}, scratch_shapes=_dma_sems(n, n),
    )(*shards)


def _pair_add(name, part, theirs, axis, pos):
    layers, rf, cf = part.shape

    def body(pos_ref, a_ref, b_ref, o_ref):
        o_ref[...] = (a_ref[...] + b_ref[...]).astype(o_ref.dtype)

    if axis == 0:
        hr = rf // 8
        grid = (layers, N_CHIPS)
        in_specs = [pl.BlockSpec((None, hr, cf), lambda l, s, p: (l, 2 * s + p[0], 0)),
                    pl.BlockSpec((None, None, hr, cf), lambda l, s, p: (l, s, 0, 0))]
        out_spec = pl.BlockSpec((None, None, hr, cf), lambda l, s, p: (l, s, 0, 0))
    else:
        hr, t = rf // 2, 128
        grid = (layers, hr // t)
        in_specs = [pl.BlockSpec((None, t, cf), lambda l, i, p: (l, p[0] * (hr // t) + i, 0)),
                    pl.BlockSpec((None, t, cf), lambda l, i, p: (l, i, 0))]
        out_spec = pl.BlockSpec((None, t, cf), lambda l, i, p: (l, i, 0))
    return pl.pallas_call(
        body, name=name, out_shape=jax.ShapeDtypeStruct(theirs.shape, BF16),
        grid_spec=pltpu.PrefetchScalarGridSpec(num_scalar_prefetch=1, grid=grid, in_specs=in_specs, out_specs=out_spec),
        compiler_params=pltpu.CompilerParams(dimension_semantics=("arbitrary", "arbitrary")),
    )(pos, part, theirs)


def _chip_sum(name, part, theirs, slots, axis, pos):
    layers, _, hr, c = slots.shape

    def body(pos_ref, mine, sib, s0, s1, s2, o_ref):
        o_ref[...] = (((mine[...] + sib[...]) + s0[...].astype(F32)) + s1[...].astype(F32)) + s2[...].astype(F32)

    t = hr if axis == 0 else 128
    if axis == 0:
        own_specs = [pl.BlockSpec((None, t, c), lambda l, i, p: (l, 2 * p[1] + p[0], 0)),
                     pl.BlockSpec((None, None, t, c), lambda l, i, p: (l, p[1], 0, 0))]
    else:
        own_specs = [pl.BlockSpec((None, t, c), lambda l, i, p: (l, p[0] * (hr // t) + i, p[1])),
                     pl.BlockSpec((None, t, c), lambda l, i, p: (l, i, p[1]))]
    slot_specs = [pl.BlockSpec((None, None, t, c), functools.partial(lambda k, l, i, p: (l, k, i, 0), k)) for k in range(3)]
    return pl.pallas_call(
        body, name=name, out_shape=jax.ShapeDtypeStruct((layers, 2 * hr, c), F32),
        grid_spec=pltpu.PrefetchScalarGridSpec(
            num_scalar_prefetch=1, grid=(layers, hr // t), in_specs=own_specs + slot_specs,
            out_specs=pl.BlockSpec((None, t, c), lambda l, i, p: (l, p[0] * (hr // t) + i, 0))),
        compiler_params=pltpu.CompilerParams(dimension_semantics=("arbitrary", "arbitrary")),
    )(pos, part, theirs, slots, slots, slots)


def _sibling_swap(name, v):
    def body(v_ref, o_ref, ssem, rsem):
        x, y, c = _pos()
        cp = pltpu.make_async_remote_copy(src_ref=v_ref, dst_ref=o_ref, send_sem=ssem.at[0], recv_sem=rsem.at[0],
                                          device_id=(x, y, 1 - c), device_id_type=MESH)
        cp.start()
        cp.wait()

    return pl.pallas_call(body, name=name, in_specs=[ANY], out_specs=ANY, out_shape=jax.ShapeDtypeStruct(v.shape, v.dtype),
                          scratch_shapes=_dma_sems(1, 1))(v)


def _add2(name, a, b):
    shape = a.shape
    a2, b2 = a.reshape(-1, shape[-1]), b.reshape(-1, shape[-1])
    (o,) = _rowwise(name, lambda u, v: u + v, [a2, b2], out_rows=[(shape[-1], F32)], tile=512)
    return o.reshape(shape)


def _sum_slots(name, slots):
    _, hr, c = slots.shape
    t = _divisor(hr, 256)

    def body(s0, s1, s2, s3, o_ref):
        o_ref[...] = ((s0[...] + s1[...]) + s2[...]) + s3[...]

    return pl.pallas_call(
        body, name=name, grid=(hr // t,),
        in_specs=[pl.BlockSpec((None, t, c), functools.partial(lambda k, i: (k, i, 0), k)) for k in range(N_CHIPS)],
        out_specs=pl.BlockSpec((t, c), lambda i: (i, 0)), out_shape=jax.ShapeDtypeStruct((hr, c), F32),
        compiler_params=pltpu.CompilerParams(dimension_semantics=("arbitrary",)),
    )(slots, slots, slots, slots)


def _adam_tile(w, g, m, v):
    m = ADAM_B1 * m + (1.0 - ADAM_B1) * g
    v = ADAM_B2 * v + (1.0 - ADAM_B2) * (g * g)
    m_hat = m / (1.0 - ADAM_B1 ** ADAM_STEP)
    v_hat = v / (1.0 - ADAM_B2 ** ADAM_STEP)
    delta = -ADAM_LR * (m_hat / (jnp.sqrt(v_hat) + ADAM_EPS) + ADAM_WD * w)
    return delta, m, v


def _adam(name, w, g, m, v):
    shape = w.shape
    c = shape[-1]
    flat = [t.reshape(-1, c) for t in (w, g, m, v)]
    res = _rowwise(name, _adam_tile, flat, out_rows=[(c, F32)] * 3, tile=256)
    return [r.reshape(shape) for r in res]


ATT_T = 256
ATT_NEG = -1e30


def _branch_bias(length):
    nblk = length // ATT_T
    d = (np.arange(nblk)[:, None, None] * ATT_T + np.arange(ATT_T)[None, :, None] - np.arange(ATT_T)[None, None, :])
    cnt = np.zeros(d.shape, np.float32)
    for window, dil in B_DILATIONS:
        cnt += ((d >= 0) & (d % dil == 0) & (d <= window)).astype(np.float32)
    bias = np.where(cnt > 0, np.log(np.maximum(cnt, 1.0)), ATT_NEG).astype(np.float32)
    return jnp.asarray(np.concatenate([bias, np.full((1, ATT_T, ATT_T), ATT_NEG, np.float32)]))


def _key_block_pair(i, jj, nblk):
    j0, j1 = 2 * jj, 2 * jj + 1
    j1c = jnp.minimum(j1, nblk - 1)
    rows = [pl.ds(pl.multiple_of(j * ATT_T, ATT_T), ATT_T) for j in (j0, j1c)]
    return rows, [i - j0, jnp.where(j1 <= i, i - j1, nblk)]


def _rope_tables(length):
    half = B_HD // 2
    inv_freq = ROPE_THETA ** (-jnp.arange(half, dtype=F32) / half)
    ang = jnp.arange(length, dtype=F32)[:, None] * inv_freq[None, :]
    cos, sin = jnp.cos(ang), jnp.sin(ang)
    return jnp.concatenate([cos, cos], axis=1), jnp.concatenate([-sin, sin], axis=1)


def _swap_halves(t):
    return pltpu.roll(t, B_HD // 2, 1)


def _rope_qkv(name, z, cos, sin, t=256):
    bsz, length, _ = z.shape
    t = _divisor(length, t)

    def body(q_ref, k_ref, v_ref, c_ref, s_ref, qo, ko, vo):
        c, s = c_ref[...], s_ref[...]
        for src, dst in ((q_ref, qo), (k_ref, ko)):
            for h in range(B_HEADS):
                cols = slice(h * B_HD, (h + 1) * B_HD)
                xh = src[:, cols]
                dst[:, cols] = (xh * c + _swap_halves(xh) * s).astype(dst.dtype)
        vo[...] = v_ref[...].astype(vo.dtype)

    col0 = 4 * A_WIDTH // B_WIDTH
    specs = [pl.BlockSpec((None, t, B_WIDTH), functools.partial(lambda k, b, i: (b, i, col0 + k), k)) for k in range(3)]
    tab = pl.BlockSpec((t, B_HD), lambda b, i: (i, 0))
    out = pl.BlockSpec((None, t, B_WIDTH), lambda b, i: (b, i, 0))
    return pl.pallas_call(
        body, name=name, grid=(bsz, length // t), in_specs=specs + [tab, tab], out_specs=[out] * 3,
        out_shape=[jax.ShapeDtypeStruct((bsz, length, B_WIDTH), BF16)] * 3,
        compiler_params=pltpu.CompilerParams(dimension_semantics=("parallel", "parallel")),
    )(z, z, z, cos, sin)


def _dilated_fwd(name, q, k, v, cnt):
    bsz, length, _ = q.shape
    scale = B_HD ** -0.5
    nblk = length // ATT_T

    def body(cnt_ref, q_ref, k_ref, v_ref, o_ref, lse_ref):
        i = pl.program_id(2)
        qb = q_ref[...]

        def step(jj, carry):
            m, l, acc = carry
            rows, bias = _key_block_pair(i, jj, nblk)
            s = [lax.dot_general(qb, k_ref[r, :], (((1,), (1,)), ((), ())), preferred_element_type=F32) * scale
                 + cnt_ref[b] for r, b in zip(rows, bias)]
            m_new = jnp.maximum(m, jnp.maximum(jnp.max(s[0], axis=-1, keepdims=True), jnp.max(s[1], axis=-1, keepdims=True)))
            a = jnp.exp(m - m_new)
            p = [jnp.exp(t - m_new) for t in s]
            l = a * l + jnp.sum(p[0], axis=-1, keepdims=True) + jnp.sum(p[1], axis=-1, keepdims=True)
            acc = (a * acc + jnp.dot(p[0].astype(MXU_DTYPE), v_ref[rows[0], :], preferred_element_type=F32)
                   + jnp.dot(p[1].astype(MXU_DTYPE), v_ref[rows[1], :], preferred_element_type=F32))
            return m_new, l, acc

        init = (jnp.full((ATT_T, 1), ATT_NEG, F32), jnp.zeros((ATT_T, 1), F32), jnp.zeros((ATT_T, B_HD), F32))
        m, l, acc = lax.fori_loop(0, (i + 2) // 2, step, init)
        o_ref[...] = acc / l
        lse_ref[...] = jnp.broadcast_to(m + jnp.log(l), (ATT_T, B_HD))

    qspec = pl.BlockSpec((None, ATT_T, B_HD), lambda b, h, i: (b, i, h))
    kspec = pl.BlockSpec((None, length, B_HD), lambda b, h, i: (b, 0, h))
    return pl.pallas_call(
        body, name=name, grid=(bsz, B_HEADS, nblk),
        in_specs=[pl.BlockSpec(cnt.shape, lambda b, h, i: (0, 0, 0)), qspec, kspec, kspec],
        out_specs=[qspec, pl.BlockSpec((None, None, ATT_T, B_HD), lambda b, h, i: (b, h, i, 0))],
        out_shape=[jax.ShapeDtypeStruct((bsz, length, B_WIDTH), F32), jax.ShapeDtypeStruct((bsz, B_HEADS, length, B_HD), F32)],
        compiler_params=pltpu.CompilerParams(dimension_semantics=("parallel", "parallel", "arbitrary")),
    )(cnt, q, k, v)


def _dilated_bwd(name, q, k, v, o, lse, do, cnt, cos, sin, off=0):
    bsz, length, _ = q.shape
    scale = B_HD ** -0.5
    nblk = length // ATT_T

    def body(cnt_ref, q_ref, k_ref, v_ref, o_ref, lse_ref, do_ref, c_ref, s_ref, dq_ref, dk_ref, dv_ref, dq_acc, dk_acc, dv_acc):
        dk_acc[...] = jnp.zeros_like(dk_acc)
        dv_acc[...] = jnp.zeros_like(dv_acc)

        def outer(i, _):
            rq = pl.ds(pl.multiple_of(i * ATT_T, ATT_T), ATT_T)
            qi, doi = q_ref[rq, :], do_ref[rq, :]
            lsei = lse_ref[rq, :][:, 0:1]
            di = jnp.sum(doi * o_ref[rq, :], axis=-1, keepdims=True)
            dob = doi.astype(MXU_DTYPE)

            def inner(jj, dq):
                rows, bias = _key_block_pair(i, jj, nblk)
                for rk, b in zip(rows, bias):
                    kj, vj = k_ref[rk, :], v_ref[rk, :]
                    s = lax.dot_general(qi, kj, (((1,), (1,)), ((), ())), preferred_element_type=F32) * scale
                    p = jnp.exp(s + cnt_ref[b] - lsei)
                    dp = lax.dot_general(dob, vj, (((1,), (1,)), ((), ())), preferred_element_type=F32)
                    ds = (p * (dp - di) * scale).astype(MXU_DTYPE)
                    dk_acc[rk, :] += lax.dot_general(ds, qi, (((0,), (0,)), ((), ())), preferred_element_type=F32)
                    dv_acc[rk, :] += lax.dot_general(p.astype(MXU_DTYPE), dob, (((0,), (0,)), ((), ())), preferred_element_type=F32)
                    dq = dq + jnp.dot(ds, kj, preferred_element_type=F32)
                return dq

            dq_acc[rq, :] = lax.fori_loop(0, (i + 2) // 2, inner, jnp.zeros((ATT_T, B_HD), F32))
            return 0

        lax.fori_loop(0, nblk, outer, 0)
        c, s = c_ref[...], s_ref[...]
        for acc, dst in ((dq_acc, dq_ref), (dk_acc, dk_ref)):
            g = acc[...]
            dst[...] = (g * c + _swap_halves(g * s)).astype(dst.dtype)
        dv_ref[...] = dv_acc[...].astype(dv_ref.dtype)

    hspec = pl.BlockSpec((None, length, B_HD), lambda b, h: (b, 0, h))
    ospec = pl.BlockSpec((None, length, B_HD), lambda b, h: (b, 0, off + h))
    tab = pl.BlockSpec((length, B_HD), lambda b, h: (0, 0))
    return pl.pallas_call(
        body, name=name, grid=(bsz, B_HEADS),
        in_specs=[pl.BlockSpec(cnt.shape, lambda b, h: (0, 0, 0)), hspec, hspec, hspec, ospec,
                  pl.BlockSpec((None, None, length, B_HD), lambda b, h: (b, h, 0, 0)), ospec, tab, tab],
        out_specs=[hspec] * 3, out_shape=[jax.ShapeDtypeStruct((bsz, length, B_WIDTH), BF16)] * 3,
        scratch_shapes=[pltpu.VMEM((length, B_HD), F32)] * 3,
        compiler_params=pltpu.CompilerParams(dimension_semantics=("parallel", "parallel")),
    )(cnt, q, k, v, o, lse, do, cos, sin)


def _chunk_cumsum(t, reverse):
    n = t.shape[0]
    row = lax.broadcasted_iota(jnp.int32, t.shape, 0) & (A_CHUNK - 1)
    s = 1
    while s < A_CHUNK:
        if reverse:
            t = t + jnp.where(row < A_CHUNK - s, pltpu.roll(t, n - s, 0), 0.0)
        else:
            t = t + jnp.where(row >= s, pltpu.roll(t, s, 0), 0.0)
        s *= 2
    return t


def _hgrn_gates(fl, lb):
    sg = jax.nn.sigmoid(fl)
    f = lb + (1.0 - lb) * sg
    return sg, f


def _bmm(a, b, ca, cb):
    return lax.dot_general(a, b, (((ca,), (cb,)), ((0,), (0,))), preferred_element_type=F32)


def _hgrn_forward_chunks(nchunk, q, f, b, v_ref, st_s, dec_s):
    shape = (nchunk, A_CHUNK, A_DK)
    b3 = b.reshape(shape)
    dec = jnp.exp(b3[:, A_CHUNK - 1:A_CHUNK, :])
    dec_s[...] = dec
    qd = (q * jnp.exp(b)).reshape(shape)
    ki = ((1.0 - f) * jnp.exp(-b)).reshape(shape)
    qdb, kib, keb = qd.astype(MXU_DTYPE), ki.astype(MXU_DTYPE), (ki * dec).astype(MXU_DTYPE)
    v3 = v_ref[...].reshape(shape).astype(MXU_DTYPE)
    tri = (lax.broadcasted_iota(jnp.int32, (1, A_CHUNK, A_CHUNK), 1) >= lax.broadcasted_iota(jnp.int32, (1, A_CHUNK, A_CHUNK), 2))
    a = jnp.where(tri, _bmm(qdb, kib, 2, 2), 0.0).astype(MXU_DTYPE)
    st_s[...] = _bmm(v3, keb, 1, 1)

    def rec(n, st):
        u = st_s[n]
        st_s[n] = st
        return st * dec_s[n] + u

    lax.fori_loop(0, nchunk, rec, jnp.zeros((A_DK, A_DK), F32))
    o = _bmm(a, v3, 2, 1) + _bmm(qdb, st_s[...].astype(MXU_DTYPE), 2, 2)
    return dict(dec=dec, qd=qd, ki=ki, qdb=qdb, kib=kib, keb=keb, v3=v3, a=a, tri=tri), o


def _hgrn_fwd(name, z, lb, onw):
    bsz, length, _ = z.shape
    nchunk = length // A_CHUNK

    def body(q_ref, f_ref, v_ref, g_ref, lb_ref, w_ref, y_ref, st_s, dec_s):
        _, f = _hgrn_gates(f_ref[...], lb_ref[...])
        b = _chunk_cumsum(jnp.log(f), False)
        _, o = _hgrn_forward_chunks(nchunk, q_ref[...], f, b, v_ref, st_s, dec_s)
        o = o.reshape(length, A_DK)
        on = o * lax.rsqrt(jnp.mean(o * o, axis=-1, keepdims=True) + NORM_EPS)
        y_ref[...] = on * w_ref[...] * _silu(g_ref[...])

    cols = [pl.BlockSpec((None, length, A_DK), functools.partial(lambda k, b, h: (b, 0, k * A_HEADS + h), k)) for k in range(4)]
    vec = pl.BlockSpec((1, A_DK), lambda b, h: (0, h))
    return pl.pallas_call(
        body, name=name, grid=(bsz, A_HEADS), in_specs=cols + [vec, vec],
        out_specs=pl.BlockSpec((None, length, A_DK), lambda b, h: (b, 0, h)),
        out_shape=jax.ShapeDtypeStruct((bsz, length, A_WIDTH), F32),
        scratch_shapes=[pltpu.VMEM((nchunk, A_DK, A_DK), F32), pltpu.VMEM((nchunk, 1, A_DK), F32)],
        compiler_params=pltpu.CompilerParams(dimension_semantics=("parallel", "parallel")),
    )(z, z, z, z, lb, onw)


def _hgrn_bwd(name, z, lb, onw, dy):
    bsz, length, _ = z.shape
    nchunk = length // A_CHUNK
    shape = (nchunk, A_CHUNK, A_DK)

    def body(q_ref, f_ref, v_ref, g_ref, lb_ref, w_ref, dy_ref, dq_ref, df_ref, dv_ref, dg_ref, dlb_ref, dw_ref,
             st_s, dst_s, dec_s):
        lb = lb_ref[...]
        sg, f = _hgrn_gates(f_ref[...], lb)
        b = _chunk_cumsum(jnp.log(f), False)
        t, o = _hgrn_forward_chunks(nchunk, q_ref[...], f, b, v_ref, st_s, dec_s)
        o, g, w, dyv = o.reshape(length, A_DK), g_ref[...], w_ref[...], dy_ref[...]
        r = lax.rsqrt(jnp.mean(o * o, axis=-1, keepdims=True) + NORM_EPS)
        on = o * r
        sgg = jax.nn.sigmoid(g)
        gate = g * sgg
        dg_ref[...] = (dyv * on * w * (sgg * (1.0 + g * (1.0 - sgg)))).astype(dg_ref.dtype)
        dw = jnp.sum(dyv * on * gate, axis=0, keepdims=True)
        don = dyv * w * gate
        do = (r * (don - on * jnp.mean(don * on, axis=-1, keepdims=True))).reshape(shape).astype(MXU_DTYPE)
        da = jnp.where(t["tri"], _bmm(do, t["v3"], 2, 2), 0.0).astype(MXU_DTYPE)
        dst_s[...] = _bmm(do, t["qdb"], 1, 1)

        def rec(i, dst):
            n = nchunk - 1 - i
            u = dst_s[n]
            dst_s[n] = dst
            return dst * dec_s[n] + u

        lax.fori_loop(0, nchunk, rec, jnp.zeros((A_DK, A_DK), F32))
        dst, st = dst_s[...], st_s[...]
        dstb = dst.astype(MXU_DTYPE)
        dec, ki, qd = t["dec"], t["ki"], t["qd"]
        dv_ref[...] = (_bmm(t["a"], do, 1, 1) + _bmm(t["keb"], dstb, 2, 2)).reshape(length, A_DK).astype(dv_ref.dtype)
        dqd = _bmm(da, t["kib"], 2, 1) + _bmm(do, st.astype(MXU_DTYPE), 2, 1)
        dke = _bmm(t["v3"], dstb, 2, 1)
        dki = _bmm(da, t["qdb"], 1, 1) + dke * dec
        ddec = jnp.sum(dst * st, axis=1, keepdims=True) + jnp.sum(dke * ki, axis=1, keepdims=True)
        last = lax.broadcasted_iota(jnp.int32, (1, A_CHUNK, A_DK), 1) == A_CHUNK - 1
        db = (dqd * qd - dki * ki + jnp.where(last, ddec * dec, 0.0)).reshape(length, A_DK)
        dlf = _chunk_cumsum(db, True)
        dq_ref[...] = (dqd.reshape(length, A_DK) * jnp.exp(b)).astype(dq_ref.dtype)
        dfv = dlf / f - dki.reshape(length, A_DK) * jnp.exp(-b)
        df_ref[...] = (dfv * (1.0 - lb) * sg * (1.0 - sg)).astype(df_ref.dtype)
        dlb = jnp.sum(dfv * (1.0 - sg), axis=0, keepdims=True)
        first = pl.program_id(1) == 0

        @pl.when(first)
        def _():
            dlb_ref[...] = dlb
            dw_ref[...] = dw

        @pl.when(jnp.logical_not(first))
        def _():
            dlb_ref[...] += dlb
            dw_ref[...] += dw

    cols = [pl.BlockSpec((None, length, A_DK), functools.partial(lambda k, h, b: (b, 0, k * A_HEADS + h), k)) for k in range(4)]
    vec = pl.BlockSpec((1, A_DK), lambda h, b: (0, h))
    head = pl.BlockSpec((None, length, A_DK), lambda h, b: (b, 0, h))
    act = jax.ShapeDtypeStruct((bsz, length, A_WIDTH), BF16)
    return pl.pallas_call(
        body, name=name, grid=(A_HEADS, bsz), in_specs=cols + [vec, vec, head],
        out_specs=[head] * 4 + [vec, vec], out_shape=[act] * 4 + [jax.ShapeDtypeStruct((1, A_WIDTH), F32)] * 2,
        scratch_shapes=[pltpu.VMEM((nchunk, A_DK, A_DK), F32)] * 2 + [pltpu.VMEM((nchunk, 1, A_DK), F32)],
        compiler_params=pltpu.CompilerParams(dimension_semantics=("parallel", "arbitrary")),
    )(z, z, z, z, lb, onw, dy)


S5_SEG = 16
S5_W = 512
S5_LANES = C_GROUPS * C_STATE
S5_NB = 8
S5_CH = D_MODEL // S5_NB
S5_COLS = 2 * S5_LANES // S5_NB


def _seg_permute(t, bsz):
    n, c = t.shape
    return t.reshape(bsz, S5_SEG, n // bsz // S5_SEG, c).transpose(0, 2, 1, 3).reshape(n, c)


def _seg_unpermute(t, bsz):
    n, c = t.shape
    return t.reshape(bsz, n // bsz // S5_SEG, S5_SEG, c).transpose(0, 2, 1, 3).reshape(n, c)


def _s5_weights(lam_re, lam_im, log_dt, b_re, b_im, c_re, c_im):
    lr = jnp.minimum(lam_re, C_MIN_NEG_RE)
    li = lam_im
    dt = jnp.exp(log_dt)[:, None]
    mag = jnp.exp(dt * lr)
    ar, ai = mag * jnp.cos(dt * li), mag * jnp.sin(dt * li)
    den = lr * lr + li * li
    zr = ((ar - 1.0) * lr + ai * li) / den
    zi = (ai * lr - (ar - 1.0) * li) / den
    bbr = zr[..., None] * b_re - zi[..., None] * b_im
    bbi = zr[..., None] * b_im + zi[..., None] * b_re
    gpb = C_GROUPS // S5_NB
    eye = jnp.eye(gpb, dtype=F32)
    bb = jnp.stack([bbr, bbi]).reshape(2, S5_NB, gpb, C_STATE, C_GROUP)
    wb = jnp.einsum('ij,rbjpc->bicjpr', eye, bb).reshape(S5_NB, S5_CH, -1, S5_W, 2)
    wb = wb.transpose(0, 1, 2, 4, 3).reshape(S5_NB, S5_CH, S5_COLS)
    cc = jnp.stack([c_re, -c_im]).reshape(2, S5_NB, gpb, C_GROUP, C_STATE)
    wc = jnp.einsum('ij,rbjcp->bjpric', eye, cc).reshape(S5_NB, -1, S5_W, 2, S5_CH)
    wc = wc.transpose(0, 1, 3, 2, 4).reshape(S5_NB, S5_COLS, S5_CH)
    return ar.reshape(1, S5_LANES), ai.reshape(1, S5_LANES), wb, wc


def _scan_in_place(ref, c0, ar1, ai1, steps, reverse):
    w = S5_W
    ar = jnp.broadcast_to(ar1, (S5_SEG, w))
    ai = jnp.broadcast_to(-ai1 if reverse else ai1, (S5_SEG, w))
    zero = jnp.zeros((S5_SEG, w), F32)
    re, im = pl.ds(c0, w), pl.ds(c0 + w, w)

    def rows_of(j):
        jj = steps - 1 - j if reverse else j
        return pl.ds(pl.multiple_of(jj * S5_SEG, S5_SEG), S5_SEG)

    def local_step(j, st):
        sr, si = st
        rows = rows_of(j)
        nr = ar * sr - ai * si + ref[rows, re]
        ni = ar * si + ai * sr + ref[rows, im]
        ref[rows, re] = nr
        ref[rows, im] = ni
        return nr, ni

    er, ei = lax.fori_loop(0, steps, local_step, (zero, zero), unroll=4)
    pr, pi = ar[0:1], ai[0:1]
    for _ in range(steps.bit_length() - 1):
        pr, pi = pr * pr - pi * pi, 2.0 * pr * pi
    row = lax.broadcasted_iota(jnp.int32, (S5_SEG, w), 0)
    cr, ci = zero, zero
    inr, ini = jnp.zeros((1, w), F32), jnp.zeros((1, w), F32)
    order = list(range(S5_SEG))[::-1] if reverse else list(range(S5_SEG))
    for idx, s in enumerate(order):
        if idx:
            cr = jnp.where(row == s, inr, cr)
            ci = jnp.where(row == s, ini, ci)
        inr, ini = er[s:s + 1] + pr * inr - pi * ini, ei[s:s + 1] + pr * ini + pi * inr

    def carry_step(j, st):
        qr, qi = st
        rows = rows_of(j)
        ref[rows, re] += qr * cr - qi * ci
        ref[rows, im] += qr * ci + qi * cr
        return qr * ar - qi * ai, qr * ai + qi * ar

    lax.fori_loop(0, steps, carry_step, (ar, ai), unroll=4)


def _da_partial(x_ref, g_ref, c0, steps):
    w = S5_W
    re, im = pl.ds(c0, w), pl.ds(c0 + w, w)
    row = lax.broadcasted_iota(jnp.int32, (S5_SEG, w), 0)
    last = pl.ds((steps - 1) * S5_SEG, S5_SEG)
    xpr = jnp.where(row == 0, 0.0, pltpu.roll(x_ref[last, re], 1, 0))
    xpi = jnp.where(row == 0, 0.0, pltpu.roll(x_ref[last, im], 1, 0))
    zero = jnp.zeros((S5_SEG, w), F32)

    def step(j, st):
        pr, pi, accr, acci = st
        rows = pl.ds(pl.multiple_of(j * S5_SEG, S5_SEG), S5_SEG)
        gr, gi = g_ref[rows, re], g_ref[rows, im]
        return x_ref[rows, re], x_ref[rows, im], accr + gr * pr + gi * pi, acci + gi * pr - gr * pi

    _, _, accr, acci = lax.fori_loop(0, steps, step, (xpr, xpi, zero, zero), unroll=4)
    return accr, acci


S5_VMEM_LIMIT = 56 * 1024 * 1024


def _s5_states(name, hp, wb, wc, a_re, a_im, bsz):
    n = hp.shape[0]
    length = n // bsz
    steps = length // S5_SEG
    assert steps & (steps - 1) == 0
    nsub = S5_COLS // (2 * S5_W)

    def body(h_ref, wb_ref, wc_ref, ar_ref, ai_ref, x_ref, y_ref):
        x_ref[...] = jnp.dot(h_ref[...].astype(MXU_DTYPE), wb_ref[...], preferred_element_type=F32)
        for sub in range(nsub):
            lanes = slice(sub * S5_W, (sub + 1) * S5_W)
            _scan_in_place(x_ref, sub * 2 * S5_W, ar_ref[:, lanes], ai_ref[:, lanes], steps, False)
        y_ref[...] = jnp.dot(x_ref[...].astype(MXU_DTYPE), wc_ref[...], preferred_element_type=F32)

    chan = pl.BlockSpec((length, S5_CH), lambda b, j: (b, j))
    avec = pl.BlockSpec((1, nsub * S5_W), lambda b, j: (0, j))
    return pl.pallas_call(
        body, name=name, grid=(bsz, S5_NB),
        in_specs=[chan, pl.BlockSpec((None, S5_CH, S5_COLS), lambda b, j: (j, 0, 0)),
                  pl.BlockSpec((None, S5_COLS, S5_CH), lambda b, j: (j, 0, 0)), avec, avec],
        out_specs=[pl.BlockSpec((length, S5_COLS), lambda b, j: (b, j)), chan],
        out_shape=[jax.ShapeDtypeStruct((n, S5_NB * S5_COLS), F32), jax.ShapeDtypeStruct((n, D_MODEL), F32)],
        compiler_params=pltpu.CompilerParams(dimension_semantics=("parallel", "parallel"), vmem_limit_bytes=S5_VMEM_LIMIT),
    )(hp, wb, wc, a_re, a_im)


def _s5_states_bwd(name, dyp, xs, hp, wb, wc, a_re, a_im, bsz):
    n = hp.shape[0]
    length = n // bsz
    steps = length // S5_SEG
    nsub = S5_COLS // (2 * S5_W)

    def body(dy_ref, x_ref, h_ref, wb_ref, wc_ref, ar_ref, ai_ref, du_ref, dwb_ref, dwc_ref, da_ref, g_s):
        dy = dy_ref[...]
        g_s[...] = lax.dot_general(dy, wc_ref[...], (((1,), (1,)), ((), ())), preferred_element_type=F32)
        das = []
        for sub in range(nsub):
            lanes = slice(sub * S5_W, (sub + 1) * S5_W)
            _scan_in_place(g_s, sub * 2 * S5_W, ar_ref[:, lanes], ai_ref[:, lanes], steps, True)
            das += list(_da_partial(x_ref, g_s, sub * 2 * S5_W, steps))
        gb = g_s[...].astype(MXU_DTYPE)
        du_ref[...] = lax.dot_general(gb, wb_ref[...], (((1,), (1,)), ((), ())), preferred_element_type=F32)
        dwb = lax.dot_general(h_ref[...].astype(MXU_DTYPE), gb, (((0,), (0,)), ((), ())), preferred_element_type=F32)
        dwc = lax.dot_general(x_ref[...].astype(MXU_DTYPE), dy, (((0,), (0,)), ((), ())), preferred_element_type=F32)
        first = pl.program_id(1) == 0

        @pl.when(first)
        def _():
            dwb_ref[...] = dwb
            dwc_ref[...] = dwc
            for k, t in enumerate(das):
                da_ref[:, k * S5_W:(k + 1) * S5_W] = t

        @pl.when(jnp.logical_not(first))
        def _():
            dwb_ref[...] += dwb
            dwc_ref[...] += dwc
            for k, t in enumerate(das):
                da_ref[:, k * S5_W:(k + 1) * S5_W] += t

    chan = pl.BlockSpec((length, S5_CH), lambda j, b: (b, j))
    avec = pl.BlockSpec((1, nsub * S5_W), lambda j, b: (0, j))
    wbs = pl.BlockSpec((None, S5_CH, S5_COLS), lambda j, b: (j, 0, 0))
    wcs = pl.BlockSpec((None, S5_COLS, S5_CH), lambda j, b: (j, 0, 0))
    return pl.pallas_call(
        body, name=name, grid=(S5_NB, bsz),
        in_specs=[chan, pl.BlockSpec((length, S5_COLS), lambda j, b: (b, j)), chan, wbs, wcs, avec, avec],
        out_specs=[chan, wbs, wcs, pl.BlockSpec((S5_SEG, S5_COLS), lambda j, b: (0, j))],
        out_shape=[jax.ShapeDtypeStruct((n, D_MODEL), F32), jax.ShapeDtypeStruct(wb.shape, F32),
                   jax.ShapeDtypeStruct(wc.shape, F32), jax.ShapeDtypeStruct((S5_SEG, S5_NB * S5_COLS), F32)],
        scratch_shapes=[pltpu.VMEM((length, S5_COLS), F32)],
        compiler_params=pltpu.CompilerParams(dimension_semantics=("parallel", "arbitrary"), vmem_limit_bytes=S5_VMEM_LIMIT),
    )(dyp, xs, hp, wb, wc, a_re, a_im)


def _gelu(y):
    return 0.5 * y * (1.0 + lax.erf(y * math.sqrt(0.5)))


def _gelu_grad(y):
    return 0.5 * (1.0 + lax.erf(y * math.sqrt(0.5))) + y * jnp.exp(-0.5 * y * y) * (1.0 / math.sqrt(2.0 * math.pi))


def _s5_fwd(h, params, d_skip, bsz):
    (a_re, a_im, wb, wc), w_vjp = jax.vjp(_s5_weights, *params)
    wb, wc = wb.astype(BF16), wc.astype(BF16)
    hp = _seg_permute(h, bsz)
    xs, yc = _s5_states("s5_states_f", hp, wb, wc, a_re, a_im, bsz)
    ypre, glp = _rowwise("s5_gelu", lambda yy, uu, dd: (lambda t: (t, _gelu(t)))(yy + dd * uu), [yc, hp], [d_skip],
                         out_rows=[(D_MODEL, F32), (D_MODEL, BF16)])
    return _seg_unpermute(glp, bsz), dict(hp=hp, xs=xs, ypre=ypre, a_re=a_re, a_im=a_im, wb=wb, wc=wc, w_vjp=w_vjp)


def _s5_bwd(dgl, sv, d_skip, bsz):
    dyp, dskip, dd = _rowwise(
        "b_s5_gelu", lambda dg, yy, uu, ds: (lambda t: (t, t * ds, jnp.sum(t * uu, axis=0, keepdims=True)))(dg * _gelu_grad(yy)),
        [_seg_permute(dgl, bsz), sv["ypre"], sv["hp"]], [d_skip], out_rows=[(D_MODEL, BF16), (D_MODEL, F32)],
        out_sums=[D_MODEL])
    du, dwb, dwc, da = _s5_states_bwd("s5_states_b", dyp, sv["xs"], sv["hp"], sv["wb"], sv["wc"], sv["a_re"], sv["a_im"], bsz)
    da = jnp.sum(da, axis=0).reshape(S5_LANES // S5_W, 2, S5_W)
    dp = sv["w_vjp"]((da[:, 0].reshape(1, S5_LANES), da[:, 1].reshape(1, S5_LANES), dwb, dwc))
    return _seg_unpermute(du + dskip, bsz), dp, dd


def _pack_rows(arrays):
    rows = []
    for a in arrays:
        flat = a.reshape(-1).astype(F32)
        pad = (-flat.shape[0]) % PACK_COLS
        rows.append(jnp.pad(flat, (0, pad)).reshape(-1, PACK_COLS))
    out = jnp.concatenate(rows, axis=0)
    return jnp.pad(out, ((0, (-out.shape[0]) % 16), (0, 0)))


def _unpack_rows(packed, shapes):
    out, r = [], 0
    for s in shapes:
        size = int(np.prod(s))
        nr = -(-size // PACK_COLS)
        out.append(packed[r:r + nr].reshape(-1)[:size].reshape(s))
        r += nr
    return out


def kernel(x, mem, norm_w, mem_norm_w, ab_w_in, ab_w_out, hgrn_lb_logits, hgrn_out_norm_w, s5_lambda_re, s5_lambda_im, s5_log_dt, s5_b_re, s5_b_im, s5_c_re, s5_c_im, s5_d, s5_w_glu, xattn_wq, xattn_wkv, xattn_wo, ffn_w_in, ffn_w_out, loss_target, m_norm_w, m_mem_norm_w, m_ab_w_in, m_ab_w_out, m_hgrn_lb_logits, m_hgrn_out_norm_w, m_s5_lambda_re, m_s5_lambda_im, m_s5_log_dt, m_s5_b_re, m_s5_b_im, m_s5_c_re, m_s5_c_im, m_s5_d, m_s5_w_glu, m_xattn_wq, m_xattn_wkv, m_xattn_wo, m_ffn_w_in, m_ffn_w_out, v_norm_w, v_mem_norm_w, v_ab_w_in, v_ab_w_out, v_hgrn_lb_logits, v_hgrn_out_norm_w, v_s5_lambda_re, v_s5_lambda_im, v_s5_log_dt, v_s5_b_re, v_s5_b_im, v_s5_c_re, v_s5_c_im, v_s5_d, v_s5_w_glu, v_xattn_wq, v_xattn_wkv, v_xattn_wo, v_ffn_w_in, v_ffn_w_out):
    given = dict(locals())
    w = {n: given[n] for n in WEIGHTS}
    mom = {n: given["m_" + n] for n in WEIGHTS}
    var = {n: given["v_" + n] for n in WEIGHTS}
    bsz, length, _ = x.shape
    ntok = bsz * length
    chip = 2 * lax.axis_index("x") + lax.axis_index("y")

    big_axes = [ax for _, ax in BIG]
    full = _gather_chips("gather_weights", [w[n].astype(BF16) for n in BIG_NAMES], big_axes)
    wf = dict(zip(BIG_NAMES, full))
    small_block = jnp.concatenate([w['norm_w'].reshape(12, -1), w['s5_d'].reshape(1, -1), jnp.zeros((3, 256), F32)], axis=0)
    (small_full,) = _gather_chips("gather_norm_w", [small_block[None]], [1])
    nw = small_full[0, :12].reshape(2, 6, 1, D_MODEL)
    s5_d_full = small_full[0, 12:13]

    lb_table, lb_vjp = jax.vjp(lambda t: jnp.cumsum(jax.nn.softmax(t, axis=0), axis=0), w['hgrn_lb_logits'])
    xs = x.reshape(ntok, D_MODEL)
    mem2 = mem.reshape(bsz * MEM_LEN, D_MODEL)
    tgt = loss_target.reshape(ntok, D_MODEL)
    saved = []
    (h,) = _rowwise("norm_in", lambda a, g: _rms(a, g), [xs], [nw[0, 0]], out_rows=[(D_MODEL, BF16)])
    cur = xs
    for layer in range(2):
        sv = {"x": cur}
        if layer == 0:
            z = _mm("ab_in", h, wf['ab_w_in'][0]).reshape(bsz, length, -1)
            sv["h0"] = h
            rope_cos, rope_sin = _rope_tables(length)
            branch_cnt = _branch_bias(length)
            oa = _hgrn_fwd("hgrn_f", z, lb_table[0:1], w['hgrn_out_norm_w'])
            qr, kr, vb = _rope_qkv("rope_qkv", z, rope_cos, rope_sin)
            ob, lse = _dilated_fwd("dilated_f", qr, kr, vb, branch_cnt)
            core = jnp.concatenate([oa, ob], axis=-1).reshape(ntok, D_MODEL)
            sv.update(z=z, qr=qr, kr=kr, vb=vb, lse=lse, core=core)
            y = _mm("ab_out", core, wf['ab_w_out'][0])
        else:
            s5p = [w[n][0] for n in ('s5_lambda_re', 's5_lambda_im', 's5_log_dt', 's5_b_re', 's5_b_im', 's5_c_re', 's5_c_im')]
            gl, sv["s5"] = _s5_fwd(h, s5p, s5_d_full, bsz)
            sv["gl"] = gl
            y, sv["zga"], sv["zgb"] = _mm_gated("s5_glu", gl, wf['s5_w_glu'][0], lambda a, b: a * jax.nn.sigmoid(b), F32)
        sv["y1"] = y
        x1, h2 = _rowwise(f"resnorm_a{layer}", lambda a, b, g1, g2: (lambda s: (s, _rms(s, g2)))(a + _rms(b, g1)),
                          [cur, y], [nw[layer, 1], nw[layer, 2]], out_rows=[(D_MODEL, F32), (D_MODEL, BF16)])
        sv["x1"], sv["h2"] = x1, h2
        (mem_n,) = _rowwise(f"mem_norm{layer}", lambda a, g: _rms(a, g), [mem2], [w['mem_norm_w'][layer][None]],
                            out_rows=[(D_MODEL, BF16)])
        sv["mem_n"] = mem_n
        q = _mm(f"xq{layer}", h2, wf['xattn_wq'][layer])
        kv = _mm(f"xkv{layer}", mem_n, wf['xattn_wkv'][layer])
        sv["q"], sv["kv"] = q, kv
        o = _xattn_fwd(f"xattn_f{layer}", q.reshape(bsz, length, D_MODEL), kv.reshape(bsz, MEM_LEN, 2 * D_MODEL))
        o = o.reshape(ntok, D_MODEL)
        sv["o"] = o
        y2 = _mm(f"xo{layer}", o, wf['xattn_wo'][layer])
        sv["y2"] = y2
        x2, h4 = _rowwise(f"resnorm_b{layer}", lambda a, b, g1, g2: (lambda s: (s, _rms(s, g2)))(a + _rms(b, g1)),
                          [x1, y2], [nw[layer, 3], nw[layer, 4]], out_rows=[(D_MODEL, F32), (D_MODEL, BF16)])
        sv["x2"], sv["h4"] = x2, h4
        act, sv["za"], sv["zb"] = _mm_gated(f"ffn_in{layer}", h4, wf['ffn_w_in'][layer], lambda a, b: _silu(a) * b, BF16)
        sv["act"] = act
        y3 = _mm(f"ffn_out{layer}", act, wf['ffn_w_out'][layer])
        sv["y3"] = y3
        saved.append(sv)
        if layer == 0:
            cur, h = _rowwise("resnorm_c0", lambda a, b, g1, g2: (lambda s: (s, _rms(s, g2)))(a + _rms(b, g1)),
                              [x2, y3], [nw[0, 5], nw[1, 0]], out_rows=[(D_MODEL, F32), (D_MODEL, F32)])
    def loss_head(a, b, t, g1):
        e = a + _rms(b, g1) - t
        gg = e * (1.0 / D_MODEL)
        dy, dw = _rms_bwd(b, g1, gg)
        return gg, dy, jnp.sum(e * e, axis=0, keepdims=True), dw

    g, dy3_ahead, sq, dw5_ahead = _rowwise("loss_head", loss_head, [saved[1]["x2"], saved[1]["y3"], tgt], [nw[1, 5]],
                                           out_rows=[(D_MODEL, F32), (D_MODEL, BF16)], out_sums=[D_MODEL, D_MODEL])
    loss = lax.psum(0.5 * jnp.sum(sq) / D_MODEL, ("x", "y", "c"))

    gbig = {}
    gnw = [[None] * 6 for _ in range(2)]
    gmemnw = [None, None]
    gsmall = {}
    for layer in (1, 0):
        sv = saved[layer]
        dy3, gnw[layer][5] = dy3_ahead, dw5_ahead
        dact = _mm(f"b_ffn_out_dx{layer}", dy3, wf['ffn_w_out'][layer], tb=True, out_dtype=BF16)
        gw_out = _mm(f"b_ffn_out_dw{layer}", sv["act"], dy3, ta=True)

        def swiglu_bwd(a, b, da):
            a, b = a.astype(F32), b.astype(F32)
            sg = jax.nn.sigmoid(a)
            return jnp.concatenate([da * b * (sg * (1.0 + a * (1.0 - sg))), da * (a * sg)], axis=1)

        (dzf,) = _rowwise(f"b_swiglu{layer}", swiglu_bwd, [sv["za"], sv["zb"], dact], out_rows=[(2 * D_FF, BF16)], tile=256)
        dh4 = _mm(f"b_ffn_in_dx{layer}", dzf, wf['ffn_w_in'][layer], tb=True)
        gw_in = _mm(f"b_ffn_in_dw{layer}", sv["h4"], dzf, ta=True)
        gbig.setdefault('ffn_w_out', {})[layer] = gw_out
        gbig.setdefault('ffn_w_in', {})[layer] = gw_in

        def resnorm_bwd(gg, dh, xx, yy, g_in, g_res):
            dx, dw_in = _rms_bwd(xx, g_in, dh)
            tot = gg + dx
            dy, dw_res = _rms_bwd(yy, g_res, tot)
            return tot, dy, dw_in, dw_res

        g, dy2, gnw[layer][4], gnw[layer][3] = _rowwise(
            f"b_resnorm_b{layer}", resnorm_bwd, [g, dh4, sv["x2"], sv["y2"]], [nw[layer, 4], nw[layer, 3]],
            out_rows=[(D_MODEL, F32), (D_MODEL, BF16)], out_sums=[D_MODEL, D_MODEL])
        do = _mm(f"b_xo_dx{layer}", dy2, wf['xattn_wo'][layer], tb=True)
        gbig.setdefault('xattn_wo', {})[layer] = _mm(f"b_xo_dw{layer}", sv["o"], dy2, ta=True)
        dq, dk, dv = _xattn_bwd(f"xattn_b{layer}", sv["q"].reshape(bsz, length, D_MODEL),
                                sv["kv"].reshape(bsz, MEM_LEN, 2 * D_MODEL), do.reshape(bsz, length, D_MODEL))
        dq = dq.reshape(ntok, D_MODEL)
        dkv = jnp.concatenate([dk, dv], axis=-1).reshape(bsz * MEM_LEN, 2 * D_MODEL)
        dh2 = _mm(f"b_xq_dx{layer}", dq, wf['xattn_wq'][layer], tb=True)
        gbig.setdefault('xattn_wq', {})[layer] = _mm(f"b_xq_dw{layer}", sv["h2"], dq, ta=True)
        dmem_n = _mm(f"b_xkv_dx{layer}", dkv, wf['xattn_wkv'][layer], tb=True)
        gbig.setdefault('xattn_wkv', {})[layer] = _mm(f"b_xkv_dw{layer}", sv["mem_n"], dkv, ta=True)
        (gmemnw[layer],) = _rowwise(f"b_mem_norm{layer}", lambda dd, mm_, g1: _rms_bwd(mm_, g1, dd)[1], [dmem_n, mem2],
                                    [w['mem_norm_w'][layer][None]], out_sums=[D_MODEL])

        g, dy1, gnw[layer][2], gnw[layer][1] = _rowwise(
            f"b_resnorm_a{layer}", resnorm_bwd, [g, dh2, sv["x1"], sv["y1"]], [nw[layer, 2], nw[layer, 1]],
            out_rows=[(D_MODEL, F32), (D_MODEL, F32 if layer == 1 else BF16)], out_sums=[D_MODEL, D_MODEL])
        if layer == 1:
            def gate_bwd(a, b, dd):
                sg = jax.nn.sigmoid(b.astype(F32))
                return jnp.concatenate([dd * sg, dd * a.astype(F32) * sg * (1.0 - sg)], axis=1)

            (dzg,) = _rowwise("b_s5_gate", gate_bwd, [sv["zga"], sv["zgb"], dy1], out_rows=[(2 * D_MODEL, BF16)])
            dgl = _mm("b_s5_glu_dx", dzg, wf['s5_w_glu'][0], tb=True)
            gbig['s5_w_glu'] = {0: _mm("b_s5_glu_dw", sv["gl"], dzg, ta=True)}
            dh0, dp, gsmall['s5_d'] = _s5_bwd(dgl, sv["s5"], s5_d_full, bsz)
            for n, t in zip(('s5_lambda_re', 's5_lambda_im', 's5_log_dt', 's5_b_re', 's5_b_im', 's5_c_re', 's5_c_im'), dp):
                gsmall[n] = t[None]
            g, dy3_ahead, gnw[1][0], dw5_ahead = _rowwise(
                "b_norm_in1", resnorm_bwd, [g, dh0, sv["x"], saved[0]["y3"]], [nw[1, 0], nw[0, 5]],
                out_rows=[(D_MODEL, F32), (D_MODEL, BF16)], out_sums=[D_MODEL, D_MODEL])
        else:
            dcore = _mm("b_ab_out_dx", dy1, wf['ab_w_out'][0], tb=True)
            gbig['ab_w_out'] = {0: _mm("b_ab_out_dw", sv["core"], dy1, ta=True)}
            dcore = dcore.reshape(bsz, length, D_MODEL)
            core3 = sv["core"].reshape(bsz, length, D_MODEL)
            dqa, dfa, dia, dga, dlb0, gsmall['hgrn_out_norm_w'] = _hgrn_bwd("hgrn_b", sv["z"], lb_table[0:1], w['hgrn_out_norm_w'], dcore)
            dqb, dkb, dvb = _dilated_bwd("dilated_b", sv["qr"], sv["kr"], sv["vb"], core3, sv["lse"], dcore, branch_cnt,
                                         rope_cos, rope_sin, off=A_WIDTH // B_HD)
            (gsmall['hgrn_lb_logits'],) = lb_vjp(jnp.zeros_like(lb_table).at[0].set(dlb0[0]))
            dz = jnp.concatenate([dqa, dfa, dia, dga, dqb, dkb, dvb], axis=-1).reshape(ntok, -1)
            dh0 = _mm("b_ab_in_dx", dz, wf['ab_w_in'][0], tb=True)
            gbig['ab_w_in'] = {0: _mm("b_ab_in_dw", sv["h0"], dz, ta=True)}
            grad_x, gnw[0][0] = _rowwise("b_norm_in0", lambda gg, dh, xx, g1: (lambda r: (gg + r[0], r[1]))(_rms_bwd(xx, g1, dh)),
                                         [g, dh0, sv["x"]], [nw[0, 0]], out_rows=[(D_MODEL, F32)], out_sums=[D_MODEL])
    gsmall['norm_w'] = jnp.stack([jnp.concatenate(gnw[l], axis=0) for l in range(2)])
    gsmall['mem_norm_w'] = jnp.concatenate(gmemnw, axis=0)

    packed = _pack_rows([gsmall[n] for n in SMALL])
    theirs = _sibling_swap("small_swap", packed)
    chip_sum = _add2("small_pair_sum", packed, theirs)
    (all_chips,) = _gather_chips("small_gather", [chip_sum[None]], [0])
    small_sum = _sum_slots("small_sum", all_chips.reshape(N_CHIPS, packed.shape[0], PACK_COLS))
    full_shapes = [(2, 6, D_MODEL) if n == 'norm_w' else (1, D_MODEL) if n == 's5_d' else w[n].shape for n in SMALL]
    gs = dict(zip(SMALL, _unpack_rows(small_sum, full_shapes)))
    for n in SHARDED_SMALL:
        gs[n] = lax.dynamic_slice_in_dim(gs[n], chip * 256, 256, axis=gs[n].ndim - 1)

    pos = _pos_vec()
    parts = [jnp.stack([gbig[n][l] for l in sorted(gbig[n])]) for n in BIG_NAMES]
    theirs = _pair_send("grad_pair_send", parts, big_axes)
    pair = [_pair_add("grad_pair_sum_" + n, a, b, ax, pos) for (n, ax), a, b in zip(BIG, parts, theirs)]
    slots = _chip_exchange("grad_chip_exchange", pair, big_axes)
    shards = [_chip_sum("grad_chip_sum_" + n, a, b, s, ax, pos) for (n, ax), a, b, s in zip(BIG, parts, theirs, slots)]
    gfull = dict(zip(BIG_NAMES, _pair_join("grad_pair_join", shards)))

    grads, deltas, new_m, new_v = {}, {}, {}, {}
    for n in BIG_NAMES:
        grads[n] = gfull[n]
        deltas[n], new_m[n], new_v[n] = _adam("adam_" + n, w[n], gfull[n], mom[n], var[n])
    pk = [_pack_rows([t[n] for n in SMALL]) for t in (w, gs, mom, var)]
    small_out = _adam("adam_small", *pk)
    shard_shapes = [w[n].shape for n in SMALL]
    for dst, packed_out in zip((deltas, new_m, new_v), small_out):
        dst.update(zip(SMALL, _unpack_rows(packed_out, shard_shapes)))
    grads.update(gs)
    return (loss, grad_x.reshape(x.shape), *[grads[n] for n in WEIGHTS], *[deltas[n] for n in WEIGHTS],
            *[new_m[n] for n in WEIGHTS], *[new_v[n] for n in WEIGHTS])
```

```python
import functools
import math

import numpy as np
import jax
import jax.numpy as jnp
from jax import lax
from jax.experimental import pallas as pl
from jax.experimental.pallas import tpu as pltpu

F32 = jnp.float32
BF16 = jnp.bfloat16
MXU_DTYPE = jnp.bfloat16

D_MODEL = 1024
NORM_EPS = 1e-6
A_HEADS, A_DK, A_CHUNK = 4, 128, 32
A_WIDTH = A_HEADS * A_DK
B_HEADS, B_HD = 4, 128
B_WIDTH = B_HEADS * B_HD
B_DILATIONS = ((128, 1), (512, 4), (2048, 16))
ROPE_THETA = 10000.0
C_GROUP, C_GROUPS, C_STATE, C_CHUNK = 16, 64, 64, 128
C_MIN_NEG_RE = -1e-4
MEM_LEN = 256
X_HEADS = 4
X_HD = D_MODEL // X_HEADS
D_FF = 2816
ADAM_LR, ADAM_B1, ADAM_B2, ADAM_EPS, ADAM_WD, ADAM_STEP = 0.001, 0.9, 0.999, 1e-08, 0.01, 10

N_CHIPS = 4
MESH = pl.DeviceIdType.MESH
ANY = pl.BlockSpec(memory_space=pl.ANY)
_RELS = ((1, 0), (0, 1), (1, 1))

WEIGHTS = ['norm_w', 'mem_norm_w', 'ab_w_in', 'ab_w_out', 'hgrn_lb_logits', 'hgrn_out_norm_w', 's5_lambda_re',
           's5_lambda_im', 's5_log_dt', 's5_b_re', 's5_b_im', 's5_c_re', 's5_c_im', 's5_d', 's5_w_glu', 'xattn_wq',
           'xattn_wkv', 'xattn_wo', 'ffn_w_in', 'ffn_w_out']
BIG = (('ab_w_in', 1), ('ab_w_out', 0), ('s5_w_glu', 1), ('xattn_wq', 0), ('xattn_wkv', 1), ('xattn_wo', 0),
       ('ffn_w_in', 1), ('ffn_w_out', 0))
BIG_NAMES = tuple(n for n, _ in BIG)
SMALL = tuple(n for n in WEIGHTS if n not in BIG_NAMES)
SHARDED_SMALL = ('norm_w', 's5_d')
PACK_COLS = 1024


def _pos():
    return lax.axis_index("x"), lax.axis_index("y"), lax.axis_index("c")


def _flip(v, d):
    return 1 - v if d else v


def _divisor(n, want):
    for t in (want, 1024, 512, 256, 128, 64, 32, 16, 8):
        if t <= want and n % t == 0:
            return t
    return n


def _rowwise(name, fn, rows, bcasts=(), out_rows=(), out_sums=(), tile=512):
    n = rows[0].shape[0]
    t = _divisor(n, tile)
    nr, nb, no, ns = len(rows), len(bcasts), len(out_rows), len(out_sums)

    def body(*refs):
        vals = [r[...] for r in refs[:nr + nb]]
        res = fn(*vals)
        if not isinstance(res, (tuple, list)):
            res = (res,)
        outs = refs[nr + nb:]
        for k in range(no):
            outs[k][...] = res[k].astype(outs[k].dtype)
        if ns:
            first = pl.program_id(0) == 0
            for k in range(ns):
                o, val = outs[no + k], res[no + k]

                @pl.when(first)
                def _():
                    o[...] = val

                @pl.when(jnp.logical_not(first))
                def _():
                    o[...] += val

    in_specs = [pl.BlockSpec((t, r.shape[1]), lambda i: (i, 0)) for r in rows]
    in_specs += [pl.BlockSpec(b.shape, lambda i: (0, 0)) for b in bcasts]
    out_specs = [pl.BlockSpec((t, c), lambda i: (i, 0)) for c, _ in out_rows]
    out_specs += [pl.BlockSpec((1, c), lambda i: (0, 0)) for c in out_sums]
    out_shape = [jax.ShapeDtypeStruct((n, c), dt) for c, dt in out_rows]
    out_shape += [jax.ShapeDtypeStruct((1, c), F32) for c in out_sums]
    res = pl.pallas_call(
        body, name=name, grid=(n // t,), in_specs=in_specs, out_specs=out_specs, out_shape=out_shape,
        compiler_params=pltpu.CompilerParams(dimension_semantics=("arbitrary",)),
    )(*rows, *bcasts)
    return res


def _rms(x, w):
    r = lax.rsqrt(jnp.mean(x * x, axis=-1, keepdims=True) + NORM_EPS)
    return x * r * w


def _rms_bwd(x, w, dy):
    r = lax.rsqrt(jnp.mean(x * x, axis=-1, keepdims=True) + NORM_EPS)
    xh = x * r
    dxh = dy * w
    dx = r * (dxh - xh * jnp.mean(dxh * xh, axis=-1, keepdims=True))
    return dx, jnp.sum(dy * xh, axis=0, keepdims=True)


def _silu(z):
    return z * jax.nn.sigmoid(z)


MM_VMEM_BUDGET = 44 * 1024 * 1024


def _mm_tiles(m, n, k, ta, abytes, bbytes, obytes):
    tn = next(t for t in (1792, 1408, 1024, 512, 256, 128) if n % t == 0) if ta else _divisor(n, 512)
    tk = _divisor(k, 1024) if ta else (k if k <= 2816 else next(t for t in (2816, 2048, 1792, 1024, 512) if k % t == 0))
    for tm in (2816, 2048, 1408, 1024, 512, 256, 128):
        if m % tm:
            continue
        need = 2 * (tm * tk * abytes + tk * tn * bbytes + tm * tn * obytes) + 2 * tm * tn * 4
        if need <= MM_VMEM_BUDGET:
            return tm, tn, tk
    return _divisor(m, 128), tn, tk


def _mm(name, a, b, ta=False, tb=False, out_dtype=F32):
    m, k = a.shape[::-1] if ta else a.shape
    k2, n = b.shape[::-1] if tb else b.shape
    assert k == k2, (name, a.shape, b.shape)
    tm, tn, tk = _mm_tiles(m, n, k, ta, a.dtype.itemsize, b.dtype.itemsize, jnp.dtype(out_dtype).itemsize)
    nk = k // tk
    dims = (((0 if ta else 1,), (1 if tb else 0,)), ((), ()))

    def prod(a_ref, b_ref):
        return lax.dot_general(a_ref[...].astype(MXU_DTYPE), b_ref[...].astype(MXU_DTYPE), dims,
                               preferred_element_type=F32)

    def body_one(a_ref, b_ref, o_ref):
        o_ref[...] = prod(a_ref, b_ref).astype(o_ref.dtype)

    def body_acc(a_ref, b_ref, o_ref, acc):
        kk = pl.program_id(2)

        @pl.when(kk == 0)
        def _():
            acc[...] = prod(a_ref, b_ref)

        @pl.when(kk > 0)
        def _():
            acc[...] += prod(a_ref, b_ref)

        @pl.when(kk == nk - 1)
        def _():
            o_ref[...] = acc[...].astype(o_ref.dtype)

    a_spec = pl.BlockSpec((tk, tm), lambda i, j, kk: (kk, i)) if ta else pl.BlockSpec((tm, tk), lambda i, j, kk: (i, kk))
    b_spec = pl.BlockSpec((tn, tk), lambda i, j, kk: (j, kk)) if tb else pl.BlockSpec((tk, tn), lambda i, j, kk: (kk, j))
    return pl.pallas_call(
        body_one if nk == 1 else body_acc, name=name, grid=(m // tm, n // tn, nk),
        in_specs=[a_spec, b_spec], out_specs=pl.BlockSpec((tm, tn), lambda i, j, kk: (i, j)),
        out_shape=jax.ShapeDtypeStruct((m, n), out_dtype),
        scratch_shapes=[] if nk == 1 else [pltpu.VMEM((tm, tn), F32)],
        compiler_params=pltpu.CompilerParams(dimension_semantics=("parallel", "parallel", "arbitrary")),
    )(a, b)


def _mm_gated(name, h, w, gate, out_dtype, tm=2048, tn=256):
    n, k = h.shape
    f = w.shape[1] // 2
    tm, nj = _divisor(n, tm), f // tn

    def body(h_ref, wa_ref, wb_ref, act_ref, za_ref, zb_ref):
        hv = h_ref[...].astype(MXU_DTYPE)
        za = jnp.dot(hv, wa_ref[...].astype(MXU_DTYPE), preferred_element_type=F32)
        zb = jnp.dot(hv, wb_ref[...].astype(MXU_DTYPE), preferred_element_type=F32)
        act_ref[...] = gate(za, zb).astype(act_ref.dtype)
        za_ref[...] = za.astype(za_ref.dtype)
        zb_ref[...] = zb.astype(zb_ref.dtype)

    out = pl.BlockSpec((tm, tn), lambda i, j: (i, j))
    return pl.pallas_call(
        body, name=name, grid=(n // tm, nj),
        in_specs=[pl.BlockSpec((tm, k), lambda i, j: (i, 0)), pl.BlockSpec((k, tn), lambda i, j: (0, j)),
                  pl.BlockSpec((k, tn), lambda i, j: (0, j + nj))],
        out_specs=[out] * 3,
        out_shape=[jax.ShapeDtypeStruct((n, f), out_dtype), jax.ShapeDtypeStruct((n, f), BF16), jax.ShapeDtypeStruct((n, f), BF16)],
        compiler_params=pltpu.CompilerParams(dimension_semantics=("parallel", "parallel")),
    )(h, w, w)


def _xattn_fwd(name, q, kv, tq=1024):
    bsz, length, _ = q.shape
    tq = _divisor(length, tq)
    scale = X_HD ** -0.5

    def body(q_ref, k_ref, v_ref, o_ref):
        qv, kk, vv = q_ref[...].astype(MXU_DTYPE), k_ref[...].astype(MXU_DTYPE), v_ref[...].astype(MXU_DTYPE)
        s = lax.dot_general(qv, kk, (((1,), (1,)), ((), ())), preferred_element_type=F32) * scale
        p = jnp.exp(s - jnp.max(s, axis=-1, keepdims=True))
        p = p / jnp.sum(p, axis=-1, keepdims=True)
        o_ref[...] = jnp.dot(p.astype(MXU_DTYPE), vv, preferred_element_type=F32).astype(o_ref.dtype)

    return pl.pallas_call(
        body, name=name, grid=(bsz, X_HEADS, length // tq),
        in_specs=[pl.BlockSpec((None, tq, X_HD), lambda b, h, i: (b, i, h)),
                  pl.BlockSpec((None, MEM_LEN, X_HD), lambda b, h, i: (b, 0, h)),
                  pl.BlockSpec((None, MEM_LEN, X_HD), lambda b, h, i: (b, 0, X_HEADS + h))],
        out_specs=pl.BlockSpec((None, tq, X_HD), lambda b, h, i: (b, i, h)),
        out_shape=jax.ShapeDtypeStruct(q.shape, BF16),
        compiler_params=pltpu.CompilerParams(dimension_semantics=("parallel", "parallel", "arbitrary")),
    )(q, kv, kv)


def _xattn_bwd(name, q, kv, do, tq=1024):
    bsz, length, _ = q.shape
    tq = _divisor(length, tq)
    scale = X_HD ** -0.5

    def body(q_ref, k_ref, v_ref, do_ref, dq_ref, dk_ref, dv_ref):
        qv, kk, vv = q_ref[...].astype(MXU_DTYPE), k_ref[...].astype(MXU_DTYPE), v_ref[...].astype(MXU_DTYPE)
        dov = do_ref[...].astype(MXU_DTYPE)
        s = lax.dot_general(qv, kk, (((1,), (1,)), ((), ())), preferred_element_type=F32) * scale
        p = jnp.exp(s - jnp.max(s, axis=-1, keepdims=True))
        p = p / jnp.sum(p, axis=-1, keepdims=True)
        dp = lax.dot_general(dov, vv, (((1,), (1,)), ((), ())), preferred_element_type=F32)
        ds = p * (dp - jnp.sum(dp * p, axis=-1, keepdims=True)) * scale
        dsb = ds.astype(MXU_DTYPE)
        dq_ref[...] = jnp.dot(dsb, kk, preferred_element_type=F32).astype(dq_ref.dtype)
        dk = lax.dot_general(dsb, qv, (((0,), (0,)), ((), ())), preferred_element_type=F32)
        dv = lax.dot_general(p.astype(MXU_DTYPE), dov, (((0,), (0,)), ((), ())), preferred_element_type=F32)
        first = pl.program_id(2) == 0

        @pl.when(first)
        def _():
            dk_ref[...] = dk
            dv_ref[...] = dv

        @pl.when(jnp.logical_not(first))
        def _():
            dk_ref[...] += dk
            dv_ref[...] += dv

    qspec = pl.BlockSpec((None, tq, X_HD), lambda b, h, i: (b, i, h))
    kspec = pl.BlockSpec((None, MEM_LEN, X_HD), lambda b, h, i: (b, 0, h))
    return pl.pallas_call(
        body, name=name, grid=(bsz, X_HEADS, length // tq),
        in_specs=[qspec, kspec, pl.BlockSpec((None, MEM_LEN, X_HD), lambda b, h, i: (b, 0, X_HEADS + h)), qspec],
        out_specs=[qspec, kspec, kspec],
        out_shape=[jax.ShapeDtypeStruct(q.shape, BF16), jax.ShapeDtypeStruct((bsz, MEM_LEN, D_MODEL), F32),
                   jax.ShapeDtypeStruct((bsz, MEM_LEN, D_MODEL), F32)],
        compiler_params=pltpu.CompilerParams(dimension_semantics=("parallel", "parallel", "arbitrary")),
    )(q, kv, kv, do)


def _dma_sems(*counts):
    return [pltpu.SemaphoreType.DMA((max(c, 1),)) for c in counts]


def _gather_chips(name, blocks, axes):
    n = len(blocks)
    shapes = [b.shape for b in blocks]

    def body(*refs):
        ins, outs = refs[:n], refs[n:2 * n]
        lsem, lrsem, ssem, rsem, fssem, frsem = refs[2 * n:]
        x, y, c = _pos()
        me = 2 * x + y

        def region(a, chip, h):
            _, r, cc = shapes[a]
            hr = r // 2
            if axes[a] == 0:
                return outs[a].at[:, pl.ds(chip * r + h * hr, hr), :]
            return outs[a].at[:, pl.ds(h * hr, hr), pl.ds(chip * cc, cc)]

        def whole(a, chip):
            _, r, cc = shapes[a]
            if axes[a] == 0:
                return outs[a].at[:, pl.ds(chip * r, r), :]
            return outs[a].at[:, :, pl.ds(chip * cc, cc)]

        sends = []
        for a in range(n):
            cp = pltpu.make_async_remote_copy(src_ref=ins[a], dst_ref=whole(a, me), send_sem=lsem.at[a], recv_sem=lrsem.at[a],
                                              device_id=(x, y, 1 - c), device_id_type=MESH)
            cp.start()
            sends.append(cp)
        for a in range(n):
            hr = shapes[a][1] // 2
            for k, (dx, dy) in enumerate(_RELS):
                cp = pltpu.make_async_remote_copy(
                    src_ref=ins[a].at[:, pl.ds(c * hr, hr), :], dst_ref=region(a, me, c),
                    send_sem=ssem.at[3 * a + k], recv_sem=rsem.at[3 * a + k],
                    device_id=(_flip(x, dx), _flip(y, dy), c), device_id_type=MESH)
                cp.start()
                sends.append(cp)
        for a in range(n):
            for k, (dx, dy) in enumerate(_RELS):
                px, py = _flip(x, dx), _flip(y, dy)
                got = region(a, 2 * px + py, c)
                pltpu.make_async_remote_copy(
                    src_ref=got, dst_ref=got, send_sem=ssem.at[3 * a + k], recv_sem=rsem.at[3 * a + k],
                    device_id=(px, py, c), device_id_type=MESH).wait_recv()
                cp = pltpu.make_async_remote_copy(
                    src_ref=got, dst_ref=got, send_sem=fssem.at[3 * a + k], recv_sem=frsem.at[3 * a + k],
                    device_id=(x, y, 1 - c), device_id_type=MESH)
                cp.start()
                sends.append(cp)
        for a in range(n):
            for k, (dx, dy) in enumerate(_RELS):
                got = region(a, 2 * _flip(x, dx) + _flip(y, dy), 1 - c)
                pltpu.make_async_remote_copy(
                    src_ref=got, dst_ref=got, send_sem=fssem.at[3 * a + k], recv_sem=frsem.at[3 * a + k],
                    device_id=(x, y, 1 - c), device_id_type=MESH).wait_recv()
        for a in range(n):
            pltpu.make_async_remote_copy(src_ref=ins[a], dst_ref=whole(a, me), send_sem=lsem.at[a], recv_sem=lrsem.at[a],
                                         device_id=(x, y, 1 - c), device_id_type=MESH).wait_recv()
        for cp in sends:
            cp.wait_send()

    out_shape = [jax.ShapeDtypeStruct((l, 4 * r, c) if ax == 0 else (l, r, 4 * c), b.dtype)
                 for (l, r, c), ax, b in zip(shapes, axes, blocks)]
    return pl.pallas_call(
        body, name=name, in_specs=[ANY] * n, out_specs=[ANY] * n, out_shape=out_shape,
        scratch_shapes=_dma_sems(n, n, 3 * n, 3 * n, 3 * n, 3 * n),
    )(*blocks)


def _pos_vec():
    x, y, c = _pos()
    return jnp.stack([c, 2 * x + y]).astype(jnp.int32)


def _pair_send(name, parts, axes):
    n = len(parts)
    shapes = [p.shape for p in parts]
    ncopy = sum(4 if ax == 0 else 1 for ax in axes)

    def body(*refs):
        ins, theirs = refs[:n], refs[n:2 * n]
        ssem, rsem = refs[2 * n:]
        x, y, c = _pos()
        pending, j = [], 0
        for a in range(n):
            _, rf, _ = shapes[a]
            if axes[a] == 0:
                hr = rf // 8
                pieces = [(ins[a].at[:, pl.ds((2 * s + 1 - c) * hr, hr), :], theirs[a].at[:, s]) for s in range(N_CHIPS)]
            else:
                hr = rf // 2
                pieces = [(ins[a].at[:, pl.ds((1 - c) * hr, hr), :], theirs[a])]
            for give, give_dst in pieces:
                rc = pltpu.make_async_remote_copy(src_ref=give, dst_ref=give_dst, send_sem=ssem.at[j],
                                                  recv_sem=rsem.at[j], device_id=(x, y, 1 - c), device_id_type=MESH)
                rc.start()
                pending.append(rc)
                j += 1
        for cp in pending:
            cp.wait()

    def half_shape(s, ax):
        return (s[0], N_CHIPS, s[1] // 8, s[2]) if ax == 0 else (s[0], s[1] // 2, s[2])

    out_shape = [jax.ShapeDtypeStruct(half_shape(s, ax), p.dtype) for s, ax, p in zip(shapes, axes, parts)]
    return pl.pallas_call(
        body, name=name, in_specs=[ANY] * n, out_specs=[ANY] * n, out_shape=out_shape,
        scratch_shapes=_dma_sems(ncopy, ncopy),
    )(*parts)


def _chip_exchange(name, halves, axes):
    n = len(halves)
    shapes = [h.shape for h in halves]

    def body(*refs):
        ins, outs = refs[:n], refs[n:2 * n]
        ssem, rsem = refs[2 * n:]
        x, y, c = _pos()

        def part(a, chip):
            if axes[a] == 0:
                return ins[a].at[:, chip]
            cc = shapes[a][2] // N_CHIPS
            return ins[a].at[:, :, pl.ds(chip * cc, cc)]

        sends = []
        for a in range(n):
            for k, (dx, dy) in enumerate(_RELS):
                px, py = _flip(x, dx), _flip(y, dy)
                rc = pltpu.make_async_remote_copy(
                    src_ref=part(a, 2 * px + py), dst_ref=outs[a].at[:, k], send_sem=ssem.at[3 * a + k],
                    recv_sem=rsem.at[3 * a + k], device_id=(px, py, c), device_id_type=MESH)
                rc.start()
                sends.append(rc)
        for cp in sends:
            cp.wait()

    def slot_shape(s, ax):
        return (s[0], 3, s[2], s[3]) if ax == 0 else (s[0], 3, s[1], s[2] // N_CHIPS)

    out_shape = [jax.ShapeDtypeStruct(slot_shape(s, ax), h.dtype) for s, ax, h in zip(shapes, axes, halves)]
    return pl.pallas_call(
        body, name=name, in_specs=[ANY] * n, out_specs=[ANY] * n, out_shape=out_shape,
        scratch_shapes=_dma_sems(3 * n, 3 * n),
    )(*halves)


def _pair_join(name, shards):
    n = len(shards)

    def body(*refs):
        outs = refs[n:2 * n]
        ssem, rsem = refs[2 * n:]
        x, y, c = _pos()
        pending = []
        for a in range(n):
            hr = shards[a].shape[1] // 2
            mine = outs[a].at[:, pl.ds(c * hr, hr), :]
            rc = pltpu.make_async_remote_copy(src_ref=mine, dst_ref=mine, send_sem=ssem.at[a], recv_sem=rsem.at[a],
                                              device_id=(x, y, 1 - c), device_id_type=MESH)
            rc.start()
            pending.append(rc)
        for a in range(n):
            hr = shards[a].shape[1] // 2
            got = outs[a].at[:, pl.ds((1 - c) * hr, hr), :]
            pltpu.make_async_remote_copy(src_ref=got, dst_ref=got, send_sem=ssem.at[a], recv_sem=rsem.at[a],
                                         device_id=(x, y, 1 - c), device_id_type=MESH).wait_recv()
        for cp in pending:
            cp.wait_send()

    return pl.pallas_call(
        body, name=name, in_specs=[ANY] * n, out_specs=[ANY] * n,
        out_shape=[jax.ShapeDtypeStruct(s.shape, s.dtype) for s in shards],
        input_output_aliases={a: a for a in range(n)}, scratch_shapes=_dma_sems(n, n),
    )(*shards)


def _pair_add(name, part, theirs, axis, pos):
    layers, rf, cf = part.shape

    def body(pos_ref, a_ref, b_ref, o_ref):
        o_ref[...] = (a_ref[...] + b_ref[...]).astype(o_ref.dtype)

    if axis == 0:
        hr = rf // 8
        grid = (layers, N_CHIPS)
        in_specs = [pl.BlockSpec((None, hr, cf), lambda l, s, p: (l, 2 * s + p[0], 0)),
                    pl.BlockSpec((None, None, hr, cf), lambda l, s, p: (l, s, 0, 0))]
        out_spec = pl.BlockSpec((None, None, hr, cf), lambda l, s, p: (l, s, 0, 0))
    else:
        hr, t = rf // 2, 128
        grid = (layers, hr // t)
        in_specs = [pl.BlockSpec((None, t, cf), lambda l, i, p: (l, p[0] * (hr // t) + i, 0)),
                    pl.BlockSpec((None, t, cf), lambda l, i, p: (l, i, 0))]
        out_spec = pl.BlockSpec((None, t, cf), lambda l, i, p: (l, i, 0))
    return pl.pallas_call(
        body, name=name, out_shape=jax.ShapeDtypeStruct(theirs.shape, BF16),
        grid_spec=pltpu.PrefetchScalarGridSpec(num_scalar_prefetch=1, grid=grid, in_specs=in_specs, out_specs=out_spec),
        compiler_params=pltpu.CompilerParams(dimension_semantics=("arbitrary", "arbitrary")),
    )(pos, part, theirs)


def _chip_sum(name, part, theirs, slots, axis, pos):
    layers, _, hr, c = slots.shape

    def body(pos_ref, mine, sib, s0, s1, s2, o_ref):
        o_ref[...] = (((mine[...] + sib[...]) + s0[...].astype(F32)) + s1[...].astype(F32)) + s2[...].astype(F32)

    t = hr if axis == 0 else 128
    if axis == 0:
        own_specs = [pl.BlockSpec((None, t, c), lambda l, i, p: (l, 2 * p[1] + p[0], 0)),
                     pl.BlockSpec((None, None, t, c), lambda l, i, p: (l, p[1], 0, 0))]
    else:
        own_specs = [pl.BlockSpec((None, t, c), lambda l, i, p: (l, p[0] * (hr // t) + i, p[1])),
                     pl.BlockSpec((None, t, c), lambda l, i, p: (l, i, p[1]))]
    slot_specs = [pl.BlockSpec((None, None, t, c), functools.partial(lambda k, l, i, p: (l, k, i, 0), k)) for k in range(3)]
    return pl.pallas_call(
        body, name=name, out_shape=jax.ShapeDtypeStruct((layers, 2 * hr, c), F32),
        grid_spec=pltpu.PrefetchScalarGridSpec(
            num_scalar_prefetch=1, grid=(layers, hr // t), in_specs=own_specs + slot_specs,
            out_specs=pl.BlockSpec((None, t, c), lambda l, i, p: (l, p[0] * (hr // t) + i, 0))),
        compiler_params=pltpu.CompilerParams(dimension_semantics=("arbitrary", "arbitrary")),
    )(pos, part, theirs, slots, slots, slots)


def _sibling_swap(name, v):
    def body(v_ref, o_ref, ssem, rsem):
        x, y, c = _pos()
        cp = pltpu.make_async_remote_copy(src_ref=v_ref, dst_ref=o_ref, send_sem=ssem.at[0], recv_sem=rsem.at[0],
                                          device_id=(x, y, 1 - c), device_id_type=MESH)
        cp.start()
        cp.wait()

    return pl.pallas_call(body, name=name, in_specs=[ANY], out_specs=ANY, out_shape=jax.ShapeDtypeStruct(v.shape, v.dtype),
                          scratch_shapes=_dma_sems(1, 1))(v)


def _add2(name, a, b):
    shape = a.shape
    a2, b2 = a.reshape(-1, shape[-1]), b.reshape(-1, shape[-1])
    (o,) = _rowwise(name, lambda u, v: u + v, [a2, b2], out_rows=[(shape[-1], F32)], tile=512)
    return o.reshape(shape)


def _sum_slots(name, slots):
    _, hr, c = slots.shape
    t = _divisor(hr, 256)

    def body(s0, s1, s2, s3, o_ref):
        o_ref[...] = ((s0[...] + s1[...]) + s2[...]) + s3[...]

    return pl.pallas_call(
        body, name=name, grid=(hr // t,),
        in_specs=[pl.BlockSpec((None, t, c), functools.partial(lambda k, i: (k, i, 0), k)) for k in range(N_CHIPS)],
        out_specs=pl.BlockSpec((t, c), lambda i: (i, 0)), out_shape=jax.ShapeDtypeStruct((hr, c), F32),
        compiler_params=pltpu.CompilerParams(dimension_semantics=("arbitrary",)),
    )(slots, slots, slots, slots)


def _adam_tile(w, g, m, v):
    m = ADAM_B1 * m + (1.0 - ADAM_B1) * g
    v = ADAM_B2 * v + (1.0 - ADAM_B2) * (g * g)
    m_hat = m / (1.0 - ADAM_B1 ** ADAM_STEP)
    v_hat = v / (1.0 - ADAM_B2 ** ADAM_STEP)
    delta = -ADAM_LR * (m_hat / (jnp.sqrt(v_hat) + ADAM_EPS) + ADAM_WD * w)
    return delta, m, v


def _adam(name, w, g, m, v):
    shape = w.shape
    c = shape[-1]
    flat = [t.reshape(-1, c) for t in (w, g, m, v)]
    res = _rowwise(name, _adam_tile, flat, out_rows=[(c, F32)] * 3, tile=256)
    return [r.reshape(shape) for r in res]


ATT_T = 256
ATT_NEG = -1e30


def _branch_bias(length):
    nblk = length // ATT_T
    d = (np.arange(nblk)[:, None, None] * ATT_T + np.arange(ATT_T)[None, :, None] - np.arange(ATT_T)[None, None, :])
    cnt = np.zeros(d.shape, np.float32)
    for window, dil in B_DILATIONS:
        cnt += ((d >= 0) & (d % dil == 0) & (d <= window)).astype(np.float32)
    bias = np.where(cnt > 0, np.log(np.maximum(cnt, 1.0)), ATT_NEG).astype(np.float32)
    return jnp.asarray(np.concatenate([bias, np.full((1, ATT_T, ATT_T), ATT_NEG, np.float32)]))


def _key_block_pair(i, jj, nblk):
    j0, j1 = 2 * jj, 2 * jj + 1
    j1c = jnp.minimum(j1, nblk - 1)
    rows = [pl.ds(pl.multiple_of(j * ATT_T, ATT_T), ATT_T) for j in (j0, j1c)]
    return rows, [i - j0, jnp.where(j1 <= i, i - j1, nblk)]


def _rope_tables(length):
    half = B_HD // 2
    inv_freq = ROPE_THETA ** (-jnp.arange(half, dtype=F32) / half)
    ang = jnp.arange(length, dtype=F32)[:, None] * inv_freq[None, :]
    cos, sin = jnp.cos(ang), jnp.sin(ang)
    return jnp.concatenate([cos, cos], axis=1), jnp.concatenate([-sin, sin], axis=1)


def _swap_halves(t):
    return pltpu.roll(t, B_HD // 2, 1)


def _rope_qkv(name, z, cos, sin, t=256):
    bsz, length, _ = z.shape
    t = _divisor(length, t)

    def body(q_ref, k_ref, v_ref, c_ref, s_ref, qo, ko, vo):
        c, s = c_ref[...], s_ref[...]
        for src, dst in ((q_ref, qo), (k_ref, ko)):
            for h in range(B_HEADS):
                cols = slice(h * B_HD, (h + 1) * B_HD)
                xh = src[:, cols]
                dst[:, cols] = (xh * c + _swap_halves(xh) * s).astype(dst.dtype)
        vo[...] = v_ref[...].astype(vo.dtype)

    col0 = 4 * A_WIDTH // B_WIDTH
    specs = [pl.BlockSpec((None, t, B_WIDTH), functools.partial(lambda k, b, i: (b, i, col0 + k), k)) for k in range(3)]
    tab = pl.BlockSpec((t, B_HD), lambda b, i: (i, 0))
    out = pl.BlockSpec((None, t, B_WIDTH), lambda b, i: (b, i, 0))
    return pl.pallas_call(
        body, name=name, grid=(bsz, length // t), in_specs=specs + [tab, tab], out_specs=[out] * 3,
        out_shape=[jax.ShapeDtypeStruct((bsz, length, B_WIDTH), BF16)] * 3,
        compiler_params=pltpu.CompilerParams(dimension_semantics=("parallel", "parallel")),
    )(z, z, z, cos, sin)


def _dilated_fwd(name, q, k, v, cnt):
    bsz, length, _ = q.shape
    scale = B_HD ** -0.5
    nblk = length // ATT_T

    def body(cnt_ref, q_ref, k_ref, v_ref, o_ref, lse_ref):
        i = pl.program_id(2)
        qb = q_ref[...]

        def step(jj, carry):
            m, l, acc = carry
            rows, bias = _key_block_pair(i, jj, nblk)
            s = [lax.dot_general(qb, k_ref[r, :], (((1,), (1,)), ((), ())), preferred_element_type=F32) * scale
                 + cnt_ref[b] for r, b in zip(rows, bias)]
            m_new = jnp.maximum(m, jnp.maximum(jnp.max(s[0], axis=-1, keepdims=True), jnp.max(s[1], axis=-1, keepdims=True)))
            a = jnp.exp(m - m_new)
            p = [jnp.exp(t - m_new) for t in s]
            l = a * l + jnp.sum(p[0], axis=-1, keepdims=True) + jnp.sum(p[1], axis=-1, keepdims=True)
            acc = (a * acc + jnp.dot(p[0].astype(MXU_DTYPE), v_ref[rows[0], :], preferred_element_type=F32)
                   + jnp.dot(p[1].astype(MXU_DTYPE), v_ref[rows[1], :], preferred_element_type=F32))
            return m_new, l, acc

        init = (jnp.full((ATT_T, 1), ATT_NEG, F32), jnp.zeros((ATT_T, 1), F32), jnp.zeros((ATT_T, B_HD), F32))
        m, l, acc = lax.fori_loop(0, (i + 2) // 2, step, init)
        o_ref[...] = acc / l
        lse_ref[...] = jnp.broadcast_to(m + jnp.log(l), (ATT_T, B_HD))

    qspec = pl.BlockSpec((None, ATT_T, B_HD), lambda b, h, i: (b, i, h))
    kspec = pl.BlockSpec((None, length, B_HD), lambda b, h, i: (b, 0, h))
    return pl.pallas_call(
        body, name=name, grid=(bsz, B_HEADS, nblk),
        in_specs=[pl.BlockSpec(cnt.shape, lambda b, h, i: (0, 0, 0)), qspec, kspec, kspec],
        out_specs=[qspec, pl.BlockSpec((None, None, ATT_T, B_HD), lambda b, h, i: (b, h, i, 0))],
        out_shape=[jax.ShapeDtypeStruct((bsz, length, B_WIDTH), F32), jax.ShapeDtypeStruct((bsz, B_HEADS, length, B_HD), F32)],
        compiler_params=pltpu.CompilerParams(dimension_semantics=("parallel", "parallel", "arbitrary")),
    )(cnt, q, k, v)


def _dilated_bwd(name, q, k, v, o, lse, do, cnt, cos, sin, off=0):
    bsz, length, _ = q.shape
    scale = B_HD ** -0.5
    nblk = length // ATT_T

    def body(cnt_ref, q_ref, k_ref, v_ref, o_ref, lse_ref, do_ref, c_ref, s_ref, dq_ref, dk_ref, dv_ref, dq_acc, dk_acc, dv_acc):
        dk_acc[...] = jnp.zeros_like(dk_acc)
        dv_acc[...] = jnp.zeros_like(dv_acc)

        def outer(i, _):
            rq = pl.ds(pl.multiple_of(i * ATT_T, ATT_T), ATT_T)
            qi, doi = q_ref[rq, :], do_ref[rq, :]
            lsei = lse_ref[rq, :][:, 0:1]
            di = jnp.sum(doi * o_ref[rq, :], axis=-1, keepdims=True)
            dob = doi.astype(MXU_DTYPE)

            def inner(jj, dq):
                rows, bias = _key_block_pair(i, jj, nblk)
                for rk, b in zip(rows, bias):
                    kj, vj = k_ref[rk, :], v_ref[rk, :]
                    s = lax.dot_general(qi, kj, (((1,), (1,)), ((), ())), preferred_element_type=F32) * scale
                    p = jnp.exp(s + cnt_ref[b] - lsei)
                    dp = lax.dot_general(dob, vj, (((1,), (1,)), ((), ())), preferred_element_type=F32)
                    ds = (p * (dp - di) * scale).astype(MXU_DTYPE)
                    dk_acc[rk, :] += lax.dot_general(ds, qi, (((0,), (0,)), ((), ())), preferred_element_type=F32)
                    dv_acc[rk, :] += lax.dot_general(p.astype(MXU_DTYPE), dob, (((0,), (0,)), ((), ())), preferred_element_type=F32)
                    dq = dq + jnp.dot(ds, kj, preferred_element_type=F32)
                return dq

            dq_acc[rq, :] = lax.fori_loop(0, (i + 2) // 2, inner, jnp.zeros((ATT_T, B_HD), F32))
            return 0

        lax.fori_loop(0, nblk, outer, 0)
        c, s = c_ref[...], s_ref[...]
        for acc, dst in ((dq_acc, dq_ref), (dk_acc, dk_ref)):
            g = acc[...]
            dst[...] = (g * c + _swap_halves(g * s)).astype(dst.dtype)
        dv_ref[...] = dv_acc[...].astype(dv_ref.dtype)

    hspec = pl.BlockSpec((None, length, B_HD), lambda b, h: (b, 0, h))
    ospec = pl.BlockSpec((None, length, B_HD), lambda b, h: (b, 0, off + h))
    tab = pl.BlockSpec((length, B_HD), lambda b, h: (0, 0))
    return pl.pallas_call(
        body, name=name, grid=(bsz, B_HEADS),
        in_specs=[pl.BlockSpec(cnt.shape, lambda b, h: (0, 0, 0)), hspec, hspec, hspec, ospec,
                  pl.BlockSpec((None, None, length, B_HD), lambda b, h: (b, h, 0, 0)), ospec, tab, tab],
        out_specs=[hspec] * 3, out_shape=[jax.ShapeDtypeStruct((bsz, length, B_WIDTH), BF16)] * 3,
        scratch_shapes=[pltpu.VMEM((length, B_HD), F32)] * 3,
        compiler_params=pltpu.CompilerParams(dimension_semantics=("parallel", "parallel")),
    )(cnt, q, k, v, o, lse, do, cos, sin)


def _chunk_cumsum(t, reverse):
    n = t.shape[0]
    row = lax.broadcasted_iota(jnp.int32, t.shape, 0) & (A_CHUNK - 1)
    s = 1
    while s < A_CHUNK:
        if reverse:
            t = t + jnp.where(row < A_CHUNK - s, pltpu.roll(t, n - s, 0), 0.0)
        else:
            t = t + jnp.where(row >= s, pltpu.roll(t, s, 0), 0.0)
        s *= 2
    return t


def _hgrn_gates(fl, lb):
    sg = jax.nn.sigmoid(fl)
    f = lb + (1.0 - lb) * sg
    return sg, f


def _bmm(a, b, ca, cb):
    return lax.dot_general(a, b, (((ca,), (cb,)), ((0,), (0,))), preferred_element_type=F32)


def _hgrn_forward_chunks(nchunk, q, f, b, v_ref, st_s, dec_s):
    shape = (nchunk, A_CHUNK, A_DK)
    b3 = b.reshape(shape)
    dec = jnp.exp(b3[:, A_CHUNK - 1:A_CHUNK, :])
    dec_s[...] = dec
    qd = (q * jnp.exp(b)).reshape(shape)
    ki = ((1.0 - f) * jnp.exp(-b)).reshape(shape)
    qdb, kib, keb = qd.astype(MXU_DTYPE), ki.astype(MXU_DTYPE), (ki * dec).astype(MXU_DTYPE)
    v3 = v_ref[...].reshape(shape).astype(MXU_DTYPE)
    tri = (lax.broadcasted_iota(jnp.int32, (1, A_CHUNK, A_CHUNK), 1) >= lax.broadcasted_iota(jnp.int32, (1, A_CHUNK, A_CHUNK), 2))
    a = jnp.where(tri, _bmm(qdb, kib, 2, 2), 0.0).astype(MXU_DTYPE)
    st_s[...] = _bmm(v3, keb, 1, 1)

    def rec(n, st):
        u = st_s[n]
        st_s[n] = st
        return st * dec_s[n] + u

    lax.fori_loop(0, nchunk, rec, jnp.zeros((A_DK, A_DK), F32))
    o = _bmm(a, v3, 2, 1) + _bmm(qdb, st_s[...].astype(MXU_DTYPE), 2, 2)
    return dict(dec=dec, qd=qd, ki=ki, qdb=qdb, kib=kib, keb=keb, v3=v3, a=a, tri=tri), o


def _hgrn_fwd(name, z, lb, onw):
    bsz, length, _ = z.shape
    nchunk = length // A_CHUNK

    def body(q_ref, f_ref, v_ref, g_ref, lb_ref, w_ref, y_ref, st_s, dec_s):
        _, f = _hgrn_gates(f_ref[...], lb_ref[...])
        b = _chunk_cumsum(jnp.log(f), False)
        _, o = _hgrn_forward_chunks(nchunk, q_ref[...], f, b, v_ref, st_s, dec_s)
        o = o.reshape(length, A_DK)
        on = o * lax.rsqrt(jnp.mean(o * o, axis=-1, keepdims=True) + NORM_EPS)
        y_ref[...] = on * w_ref[...] * _silu(g_ref[...])

    cols = [pl.BlockSpec((None, length, A_DK), functools.partial(lambda k, b, h: (b, 0, k * A_HEADS + h), k)) for k in range(4)]
    vec = pl.BlockSpec((1, A_DK), lambda b, h: (0, h))
    return pl.pallas_call(
        body, name=name, grid=(bsz, A_HEADS), in_specs=cols + [vec, vec],
        out_specs=pl.BlockSpec((None, length, A_DK), lambda b, h: (b, 0, h)),
        out_shape=jax.ShapeDtypeStruct((bsz, length, A_WIDTH), F32),
        scratch_shapes=[pltpu.VMEM((nchunk, A_DK, A_DK), F32), pltpu.VMEM((nchunk, 1, A_DK), F32)],
        compiler_params=pltpu.CompilerParams(dimension_semantics=("parallel", "parallel")),
    )(z, z, z, z, lb, onw)


def _hgrn_bwd(name, z, lb, onw, dy):
    bsz, length, _ = z.shape
    nchunk = length // A_CHUNK
    shape = (nchunk, A_CHUNK, A_DK)

    def body(q_ref, f_ref, v_ref, g_ref, lb_ref, w_ref, dy_ref, dq_ref, df_ref, dv_ref, dg_ref, dlb_ref, dw_ref,
             st_s, dst_s, dec_s):
        lb = lb_ref[...]
        sg, f = _hgrn_gates(f_ref[...], lb)
        b = _chunk_cumsum(jnp.log(f), False)
        t, o = _hgrn_forward_chunks(nchunk, q_ref[...], f, b, v_ref, st_s, dec_s)
        o, g, w, dyv = o.reshape(length, A_DK), g_ref[...], w_ref[...], dy_ref[...]
        r = lax.rsqrt(jnp.mean(o * o, axis=-1, keepdims=True) + NORM_EPS)
        on = o * r
        sgg = jax.nn.sigmoid(g)
        gate = g * sgg
        dg_ref[...] = (dyv * on * w * (sgg * (1.0 + g * (1.0 - sgg)))).astype(dg_ref.dtype)
        dw = jnp.sum(dyv * on * gate, axis=0, keepdims=True)
        don = dyv * w * gate
        do = (r * (don - on * jnp.mean(don * on, axis=-1, keepdims=True))).reshape(shape).astype(MXU_DTYPE)
        da = jnp.where(t["tri"], _bmm(do, t["v3"], 2, 2), 0.0).astype(MXU_DTYPE)
        dst_s[...] = _bmm(do, t["qdb"], 1, 1)

        def rec(i, dst):
            n = nchunk - 1 - i
            u = dst_s[n]
            dst_s[n] = dst
            return dst * dec_s[n] + u

        lax.fori_loop(0, nchunk, rec, jnp.zeros((A_DK, A_DK), F32))
        dst, st = dst_s[...], st_s[...]
        dstb = dst.astype(MXU_DTYPE)
        dec, ki, qd = t["dec"], t["ki"], t["qd"]
        dv_ref[...] = (_bmm(t["a"], do, 1, 1) + _bmm(t["keb"], dstb, 2, 2)).reshape(length, A_DK).astype(dv_ref.dtype)
        dqd = _bmm(da, t["kib"], 2, 1) + _bmm(do, st.astype(MXU_DTYPE), 2, 1)
        dke = _bmm(t["v3"], dstb, 2, 1)
        dki = _bmm(da, t["qdb"], 1, 1) + dke * dec
        ddec = jnp.sum(dst * st, axis=1, keepdims=True) + jnp.sum(dke * ki, axis=1, keepdims=True)
        last = lax.broadcasted_iota(jnp.int32, (1, A_CHUNK, A_DK), 1) == A_CHUNK - 1
        db = (dqd * qd - dki * ki + jnp.where(last, ddec * dec, 0.0)).reshape(length, A_DK)
        dlf = _chunk_cumsum(db, True)
        dq_ref[...] = (dqd.reshape(length, A_DK) * jnp.exp(b)).astype(dq_ref.dtype)
        dfv = dlf / f - dki.reshape(length, A_DK) * jnp.exp(-b)
        df_ref[...] = (dfv * (1.0 - lb) * sg * (1.0 - sg)).astype(df_ref.dtype)
        dlb = jnp.sum(dfv * (1.0 - sg), axis=0, keepdims=True)
        first = pl.program_id(1) == 0

        @pl.when(first)
        def _():
            dlb_ref[...] = dlb
            dw_ref[...] = dw

        @pl.when(jnp.logical_not(first))
        def _():
            dlb_ref[...] += dlb
            dw_ref[...] += dw

    cols = [pl.BlockSpec((None, length, A_DK), functools.partial(lambda k, h, b: (b, 0, k * A_HEADS + h), k)) for k in range(4)]
    vec = pl.BlockSpec((1, A_DK), lambda h, b: (0, h))
    head = pl.BlockSpec((None, length, A_DK), lambda h, b: (b, 0, h))
    act = jax.ShapeDtypeStruct((bsz, length, A_WIDTH), BF16)
    return pl.pallas_call(
        body, name=name, grid=(A_HEADS, bsz), in_specs=cols + [vec, vec, head],
        out_specs=[head] * 4 + [vec, vec], out_shape=[act] * 4 + [jax.ShapeDtypeStruct((1, A_WIDTH), F32)] * 2,
        scratch_shapes=[pltpu.VMEM((nchunk, A_DK, A_DK), F32)] * 2 + [pltpu.VMEM((nchunk, 1, A_DK), F32)],
        compiler_params=pltpu.CompilerParams(dimension_semantics=("parallel", "arbitrary")),
    )(z, z, z, z, lb, onw, dy)


S5_SEG = 16
S5_W = 512
S5_LANES = C_GROUPS * C_STATE
S5_NB = 8
S5_CH = D_MODEL // S5_NB
S5_COLS = 2 * S5_LANES // S5_NB


def _seg_permute(t, bsz):
    n, c = t.shape
    return t.reshape(bsz, S5_SEG, n // bsz // S5_SEG, c).transpose(0, 2, 1, 3).reshape(n, c)


def _seg_unpermute(t, bsz):
    n, c = t.shape
    return t.reshape(bsz, n // bsz // S5_SEG, S5_SEG, c).transpose(0, 2, 1, 3).reshape(n, c)


def _s5_weights(lam_re, lam_im, log_dt, b_re, b_im, c_re, c_im):
    lr = jnp.minimum(lam_re, C_MIN_NEG_RE)
    li = lam_im
    dt = jnp.exp(log_dt)[:, None]
    mag = jnp.exp(dt * lr)
    ar, ai = mag * jnp.cos(dt * li), mag * jnp.sin(dt * li)
    den = lr * lr + li * li
    zr = ((ar - 1.0) * lr + ai * li) / den
    zi = (ai * lr - (ar - 1.0) * li) / den
    bbr = zr[..., None] * b_re - zi[..., None] * b_im
    bbi = zr[..., None] * b_im + zi[..., None] * b_re
    gpb = C_GROUPS // S5_NB
    eye = jnp.eye(gpb, dtype=F32)
    bb = jnp.stack([bbr, bbi]).reshape(2, S5_NB, gpb, C_STATE, C_GROUP)
    wb = jnp.einsum('ij,rbjpc->bicjpr', eye, bb).reshape(S5_NB, S5_CH, -1, S5_W, 2)
    wb = wb.transpose(0, 1, 2, 4, 3).reshape(S5_NB, S5_CH, S5_COLS)
    cc = jnp.stack([c_re, -c_im]).reshape(2, S5_NB, gpb, C_GROUP, C_STATE)
    wc = jnp.einsum('ij,rbjcp->bjpric', eye, cc).reshape(S5_NB, -1, S5_W, 2, S5_CH)
    wc = wc.transpose(0, 1, 3, 2, 4).reshape(S5_NB, S5_COLS, S5_CH)
    return ar.reshape(1, S5_LANES), ai.reshape(1, S5_LANES), wb, wc


def _scan_in_place(ref, c0, ar1, ai1, steps, reverse):
    w = S5_W
    ar = jnp.broadcast_to(ar1, (S5_SEG, w))
    ai = jnp.broadcast_to(-ai1 if reverse else ai1, (S5_SEG, w))
    zero = jnp.zeros((S5_SEG, w), F32)
    re, im = pl.ds(c0, w), pl.ds(c0 + w, w)

    def rows_of(j):
        jj = steps - 1 - j if reverse else j
        return pl.ds(pl.multiple_of(jj * S5_SEG, S5_SEG), S5_SEG)

    def local_step(j, st):
        sr, si = st
        rows = rows_of(j)
        nr = ar * sr - ai * si + ref[rows, re]
        ni = ar * si + ai * sr + ref[rows, im]
        ref[rows, re] = nr
        ref[rows, im] = ni
        return nr, ni

    er, ei = lax.fori_loop(0, steps, local_step, (zero, zero), unroll=4)
    pr, pi = ar[0:1], ai[0:1]
    for _ in range(steps.bit_length() - 1):
        pr, pi = pr * pr - pi * pi, 2.0 * pr * pi
    row = lax.broadcasted_iota(jnp.int32, (S5_SEG, w), 0)
    cr, ci = zero, zero
    inr, ini = jnp.zeros((1, w), F32), jnp.zeros((1, w), F32)
    order = list(range(S5_SEG))[::-1] if reverse else list(range(S5_SEG))
    for idx, s in enumerate(order):
        if idx:
            cr = jnp.where(row == s, inr, cr)
            ci = jnp.where(row == s, ini, ci)
        inr, ini = er[s:s + 1] + pr * inr - pi * ini, ei[s:s + 1] + pr * ini + pi * inr

    def carry_step(j, st):
        qr, qi = st
        rows = rows_of(j)
        ref[rows, re] += qr * cr - qi * ci
        ref[rows, im] += qr * ci + qi * cr
        return qr * ar - qi * ai, qr * ai + qi * ar

    lax.fori_loop(0, steps, carry_step, (ar, ai), unroll=4)


def _da_partial(x_ref, g_ref, c0, steps):
    w = S5_W
    re, im = pl.ds(c0, w), pl.ds(c0 + w, w)
    row = lax.broadcasted_iota(jnp.int32, (S5_SEG, w), 0)
    last = pl.ds((steps - 1) * S5_SEG, S5_SEG)
    xpr = jnp.where(row == 0, 0.0, pltpu.roll(x_ref[last, re], 1, 0))
    xpi = jnp.where(row == 0, 0.0, pltpu.roll(x_ref[last, im], 1, 0))
    zero = jnp.zeros((S5_SEG, w), F32)

    def step(j, st):
        pr, pi, accr, acci = st
        rows = pl.ds(pl.multiple_of(j * S5_SEG, S5_SEG), S5_SEG)
        gr, gi = g_ref[rows, re], g_ref[rows, im]
        return x_ref[rows, re], x_ref[rows, im], accr + gr * pr + gi * pi, acci + gi * pr - gr * pi

    _, _, accr, acci = lax.fori_loop(0, steps, step, (xpr, xpi, zero, zero), unroll=4)
    return accr, acci


S5_VMEM_LIMIT = 56 * 1024 * 1024


def _s5_states(name, hp, wb, wc, a_re, a_im, bsz):
    n = hp.shape[0]
    length = n // bsz
    steps = length // S5_SEG
    assert steps & (steps - 1) == 0
    nsub = S5_COLS // (2 * S5_W)

    def body(h_ref, wb_ref, wc_ref, ar_ref, ai_ref, x_ref, y_ref):
        x_ref[...] = jnp.dot(h_ref[...].astype(MXU_DTYPE), wb_ref[...], preferred_element_type=F32)
        for sub in range(nsub):
            lanes = slice(sub * S5_W, (sub + 1) * S5_W)
            _scan_in_place(x_ref, sub * 2 * S5_W, ar_ref[:, lanes], ai_ref[:, lanes], steps, False)
        y_ref[...] = jnp.dot(x_ref[...].astype(MXU_DTYPE), wc_ref[...], preferred_element_type=F32)

    chan = pl.BlockSpec((length, S5_CH), lambda b, j: (b, j))
    avec = pl.BlockSpec((1, nsub * S5_W), lambda b, j: (0, j))
    return pl.pallas_call(
        body, name=name, grid=(bsz, S5_NB),
        in_specs=[chan, pl.BlockSpec((None, S5_CH, S5_COLS), lambda b, j: (j, 0, 0)),
                  pl.BlockSpec((None, S5_COLS, S5_CH), lambda b, j: (j, 0, 0)), avec, avec],
        out_specs=[pl.BlockSpec((length, S5_COLS), lambda b, j: (b, j)), chan],
        out_shape=[jax.ShapeDtypeStruct((n, S5_NB * S5_COLS), F32), jax.ShapeDtypeStruct((n, D_MODEL), F32)],
        compiler_params=pltpu.CompilerParams(dimension_semantics=("parallel", "parallel"), vmem_limit_bytes=S5_VMEM_LIMIT),
    )(hp, wb, wc, a_re, a_im)


def _s5_states_bwd(name, dyp, xs, hp, wb, wc, a_re, a_im, bsz):
    n = hp.shape[0]
    length = n // bsz
    steps = length // S5_SEG
    nsub = S5_COLS // (2 * S5_W)

    def body(dy_ref, x_ref, h_ref, wb_ref, wc_ref, ar_ref, ai_ref, du_ref, dwb_ref, dwc_ref, da_ref, g_s):
        dy = dy_ref[...]
        g_s[...] = lax.dot_general(dy, wc_ref[...], (((1,), (1,)), ((), ())), preferred_element_type=F32)
        das = []
        for sub in range(nsub):
            lanes = slice(sub * S5_W, (sub + 1) * S5_W)
            _scan_in_place(g_s, sub * 2 * S5_W, ar_ref[:, lanes], ai_ref[:, lanes], steps, True)
            das += list(_da_partial(x_ref, g_s, sub * 2 * S5_W, steps))
        gb = g_s[...].astype(MXU_DTYPE)
        du_ref[...] = lax.dot_general(gb, wb_ref[...], (((1,), (1,)), ((), ())), preferred_element_type=F32)
        dwb = lax.dot_general(h_ref[...].astype(MXU_DTYPE), gb, (((0,), (0,)), ((), ())), preferred_element_type=F32)
        dwc = lax.dot_general(x_ref[...].astype(MXU_DTYPE), dy, (((0,), (0,)), ((), ())), preferred_element_type=F32)
        first = pl.program_id(1) == 0

        @pl.when(first)
        def _():
            dwb_ref[...] = dwb
            dwc_ref[...] = dwc
            for k, t in enumerate(das):
                da_ref[:, k * S5_W:(k + 1) * S5_W] = t

        @pl.when(jnp.logical_not(first))
        def _():
            dwb_ref[...] += dwb
            dwc_ref[...] += dwc
            for k, t in enumerate(das):
                da_ref[:, k * S5_W:(k + 1) * S5_W] += t

    chan = pl.BlockSpec((length, S5_CH), lambda j, b: (b, j))
    avec = pl.BlockSpec((1, nsub * S5_W), lambda j, b: (0, j))
    wbs = pl.BlockSpec((None, S5_CH, S5_COLS), lambda j, b: (j, 0, 0))
    wcs = pl.BlockSpec((None, S5_COLS, S5_CH), lambda j, b: (j, 0, 0))
    return pl.pallas_call(
        body, name=name, grid=(S5_NB, bsz),
        in_specs=[chan, pl.BlockSpec((length, S5_COLS), lambda j, b: (b, j)), chan, wbs, wcs, avec, avec],
        out_specs=[chan, wbs, wcs, pl.BlockSpec((S5_SEG, S5_COLS), lambda j, b: (0, j))],
        out_shape=[jax.ShapeDtypeStruct((n, D_MODEL), F32), jax.ShapeDtypeStruct(wb.shape, F32),
                   jax.ShapeDtypeStruct(wc.shape, F32), jax.ShapeDtypeStruct((S5_SEG, S5_NB * S5_COLS), F32)],
        scratch_shapes=[pltpu.VMEM((length, S5_COLS), F32)],
        compiler_params=pltpu.CompilerParams(dimension_semantics=("parallel", "arbitrary"), vmem_limit_bytes=S5_VMEM_LIMIT),
    )(dyp, xs, hp, wb, wc, a_re, a_im)


def _gelu(y):
    return 0.5 * y * (1.0 + lax.erf(y * math.sqrt(0.5)))


def _gelu_grad(y):
    return 0.5 * (1.0 + lax.erf(y * math.sqrt(0.5))) + y * jnp.exp(-0.5 * y * y) * (1.0 / math.sqrt(2.0 * math.pi))


def _s5_fwd(h, params, d_skip, bsz):
    (a_re, a_im, wb, wc), w_vjp = jax.vjp(_s5_weights, *params)
    wb, wc = wb.astype(BF16), wc.astype(BF16)
    hp = _seg_permute(h, bsz)
    xs, yc = _s5_states("s5_states_f", hp, wb, wc, a_re, a_im, bsz)
    ypre, glp = _rowwise("s5_gelu", lambda yy, uu, dd: (lambda t: (t, _gelu(t)))(yy + dd * uu), [yc, hp], [d_skip],
                         out_rows=[(D_MODEL, F32), (D_MODEL, BF16)])
    return _seg_unpermute(glp, bsz), dict(hp=hp, xs=xs, ypre=ypre, a_re=a_re, a_im=a_im, wb=wb, wc=wc, w_vjp=w_vjp)


def _s5_bwd(dgl, sv, d_skip, bsz):
    dyp, dskip, dd = _rowwise(
        "b_s5_gelu", lambda dg, yy, uu, ds: (lambda t: (t, t * ds, jnp.sum(t * uu, axis=0, keepdims=True)))(dg * _gelu_grad(yy)),
        [_seg_permute(dgl, bsz), sv["ypre"], sv["hp"]], [d_skip], out_rows=[(D_MODEL, BF16), (D_MODEL, F32)],
        out_sums=[D_MODEL])
    du, dwb, dwc, da = _s5_states_bwd("s5_states_b", dyp, sv["xs"], sv["hp"], sv["wb"], sv["wc"], sv["a_re"], sv["a_im"], bsz)
    da = jnp.sum(da, axis=0).reshape(S5_LANES // S5_W, 2, S5_W)
    dp = sv["w_vjp"]((da[:, 0].reshape(1, S5_LANES), da[:, 1].reshape(1, S5_LANES), dwb, dwc))
    return _seg_unpermute(du + dskip, bsz), dp, dd


def _pack_rows(arrays):
    rows = []
    for a in arrays:
        flat = a.reshape(-1).astype(F32)
        pad = (-flat.shape[0]) % PACK_COLS
        rows.append(jnp.pad(flat, (0, pad)).reshape(-1, PACK_COLS))
    out = jnp.concatenate(rows, axis=0)
    return jnp.pad(out, ((0, (-out.shape[0]) % 16), (0, 0)))


def _unpack_rows(packed, shapes):
    out, r = [], 0
    for s in shapes:
        size = int(np.prod(s))
        nr = -(-size // PACK_COLS)
        out.append(packed[r:r + nr].reshape(-1)[:size].reshape(s))
        r += nr
    return out


def kernel(x, mem, norm_w, mem_norm_w, ab_w_in, ab_w_out, hgrn_lb_logits, hgrn_out_norm_w, s5_lambda_re, s5_lambda_im, s5_log_dt, s5_b_re, s5_b_im, s5_c_re, s5_c_im, s5_d, s5_w_glu, xattn_wq, xattn_wkv, xattn_wo, ffn_w_in, ffn_w_out, loss_target, m_norm_w, m_mem_norm_w, m_ab_w_in, m_ab_w_out, m_hgrn_lb_logits, m_hgrn_out_norm_w, m_s5_lambda_re, m_s5_lambda_im, m_s5_log_dt, m_s5_b_re, m_s5_b_im, m_s5_c_re, m_s5_c_im, m_s5_d, m_s5_w_glu, m_xattn_wq, m_xattn_wkv, m_xattn_wo, m_ffn_w_in, m_ffn_w_out, v_norm_w, v_mem_norm_w, v_ab_w_in, v_ab_w_out, v_hgrn_lb_logits, v_hgrn_out_norm_w, v_s5_lambda_re, v_s5_lambda_im, v_s5_log_dt, v_s5_b_re, v_s5_b_im, v_s5_c_re, v_s5_c_im, v_s5_d, v_s5_w_glu, v_xattn_wq, v_xattn_wkv, v_xattn_wo, v_ffn_w_in, v_ffn_w_out):
    given = dict(locals())
    w = {n: given[n] for n in WEIGHTS}
    mom = {n: given["m_" + n] for n in WEIGHTS}
    var = {n: given["v_" + n] for n in WEIGHTS}
    bsz, length, _ = x.shape
    ntok = bsz * length
    chip = 2 * lax.axis_index("x") + lax.axis_index("y")

    big_axes = [ax for _, ax in BIG]
    full = _gather_chips("gather_weights", [w[n].astype(BF16) for n in BIG_NAMES], big_axes)
    wf = dict(zip(BIG_NAMES, full))
    small_block = jnp.concatenate([w['norm_w'].reshape(12, -1), w['s5_d'].reshape(1, -1), jnp.zeros((3, 256), F32)], axis=0)
    (small_full,) = _gather_chips("gather_norm_w", [small_block[None]], [1])
    nw = small_full[0, :12].reshape(2, 6, 1, D_MODEL)
    s5_d_full = small_full[0, 12:13]

    lb_table, lb_vjp = jax.vjp(lambda t: jnp.cumsum(jax.nn.softmax(t, axis=0), axis=0), w['hgrn_lb_logits'])
    xs = x.reshape(ntok, D_MODEL)
    mem2 = mem.reshape(bsz * MEM_LEN, D_MODEL)
    tgt = loss_target.reshape(ntok, D_MODEL)
    saved = []
    (h,) = _rowwise("norm_in", lambda a, g: _rms(a, g), [xs], [nw[0, 0]], out_rows=[(D_MODEL, BF16)])
    cur = xs
    for layer in range(2):
        sv = {"x": cur}
        if layer == 0:
            z = _mm("ab_in", h, wf['ab_w_in'][0]).reshape(bsz, length, -1)
            sv["h0"] = h
            rope_cos, rope_sin = _rope_tables(length)
            branch_cnt = _branch_bias(length)
            oa = _hgrn_fwd("hgrn_f", z, lb_table[0:1], w['hgrn_out_norm_w'])
            qr, kr, vb = _rope_qkv("rope_qkv", z, rope_cos, rope_sin)
            ob, lse = _dilated_fwd("dilated_f", qr, kr, vb, branch_cnt)
            core = jnp.concatenate([oa, ob], axis=-1).reshape(ntok, D_MODEL)
            sv.update(z=z, qr=qr, kr=kr, vb=vb, lse=lse, core=core)
            y = _mm("ab_out", core, wf['ab_w_out'][0])
        else:
            s5p = [w[n][0] for n in ('s5_lambda_re', 's5_lambda_im', 's5_log_dt', 's5_b_re', 's5_b_im', 's5_c_re', 's5_c_im')]
            gl, sv["s5"] = _s5_fwd(h, s5p, s5_d_full, bsz)
            sv["gl"] = gl
            y, sv["zga"], sv["zgb"] = _mm_gated("s5_glu", gl, wf['s5_w_glu'][0], lambda a, b: a * jax.nn.sigmoid(b), F32)
        sv["y1"] = y
        x1, h2 = _rowwise(f"resnorm_a{layer}", lambda a, b, g1, g2: (lambda s: (s, _rms(s, g2)))(a + _rms(b, g1)),
                          [cur, y], [nw[layer, 1], nw[layer, 2]], out_rows=[(D_MODEL, F32), (D_MODEL, BF16)])
        sv["x1"], sv["h2"] = x1, h2
        (mem_n,) = _rowwise(f"mem_norm{layer}", lambda a, g: _rms(a, g), [mem2], [w['mem_norm_w'][layer][None]],
                            out_rows=[(D_MODEL, BF16)])
        sv["mem_n"] = mem_n
        q = _mm(f"xq{layer}", h2, wf['xattn_wq'][layer], out_dtype=MXU_DTYPE)
        kv = _mm(f"xkv{layer}", mem_n, wf['xattn_wkv'][layer], out_dtype=MXU_DTYPE)
        sv["q"], sv["kv"] = q, kv
        o = _xattn_fwd(f"xattn_f{layer}", q.reshape(bsz, length, D_MODEL), kv.reshape(bsz, MEM_LEN, 2 * D_MODEL))
        o = o.reshape(ntok, D_MODEL)
        sv["o"] = o
        y2 = _mm(f"xo{layer}", o, wf['xattn_wo'][layer])
        sv["y2"] = y2
        x2, h4 = _rowwise(f"resnorm_b{layer}", lambda a, b, g1, g2: (lambda s: (s, _rms(s, g2)))(a + _rms(b, g1)),
                          [x1, y2], [nw[layer, 3], nw[layer, 4]], out_rows=[(D_MODEL, F32), (D_MODEL, BF16)])
        sv["x2"], sv["h4"] = x2, h4
        act, sv["za"], sv["zb"] = _mm_gated(f"ffn_in{layer}", h4, wf['ffn_w_in'][layer], lambda a, b: _silu(a) * b, BF16)
        sv["act"] = act
        y3 = _mm(f"ffn_out{layer}", act, wf['ffn_w_out'][layer])
        sv["y3"] = y3
        saved.append(sv)
        if layer == 0:
            cur, h = _rowwise("resnorm_c0", lambda a, b, g1, g2: (lambda s: (s, _rms(s, g2)))(a + _rms(b, g1)),
                              [x2, y3], [nw[0, 5], nw[1, 0]], out_rows=[(D_MODEL, F32), (D_MODEL, F32)])
    def loss_head(a, b, t, g1):
        e = a + _rms(b, g1) - t
        gg = e * (1.0 / D_MODEL)
        dy, dw = _rms_bwd(b, g1, gg)
        return gg, dy, jnp.sum(e * e, axis=0, keepdims=True), dw

    g, dy3_ahead, sq, dw5_ahead = _rowwise("loss_head", loss_head, [saved[1]["x2"], saved[1]["y3"], tgt], [nw[1, 5]],
                                           out_rows=[(D_MODEL, F32), (D_MODEL, BF16)], out_sums=[D_MODEL, D_MODEL])
    loss = lax.psum(0.5 * jnp.sum(sq) / D_MODEL, ("x", "y", "c"))

    gbig = {}
    gnw = [[None] * 6 for _ in range(2)]
    gmemnw = [None, None]
    gsmall = {}
    for layer in (1, 0):
        sv = saved[layer]
        dy3, gnw[layer][5] = dy3_ahead, dw5_ahead
        dact = _mm(f"b_ffn_out_dx{layer}", dy3, wf['ffn_w_out'][layer], tb=True, out_dtype=BF16)
        gw_out = _mm(f"b_ffn_out_dw{layer}", sv["act"], dy3, ta=True)

        def swiglu_bwd(a, b, da):
            a, b = a.astype(F32), b.astype(F32)
            sg = jax.nn.sigmoid(a)
            return jnp.concatenate([da * b * (sg * (1.0 + a * (1.0 - sg))), da * (a * sg)], axis=1)

        (dzf,) = _rowwise(f"b_swiglu{layer}", swiglu_bwd, [sv["za"], sv["zb"], dact], out_rows=[(2 * D_FF, BF16)], tile=256)
        dh4 = _mm(f"b_ffn_in_dx{layer}", dzf, wf['ffn_w_in'][layer], tb=True)
        gw_in = _mm(f"b_ffn_in_dw{layer}", sv["h4"], dzf, ta=True)
        gbig.setdefault('ffn_w_out', {})[layer] = gw_out
        gbig.setdefault('ffn_w_in', {})[layer] = gw_in

        def resnorm_bwd(gg, dh, xx, yy, g_in, g_res):
            dx, dw_in = _rms_bwd(xx, g_in, dh)
            tot = gg + dx
            dy, dw_res = _rms_bwd(yy, g_res, tot)
            return tot, dy, dw_in, dw_res

        g, dy2, gnw[layer][4], gnw[layer][3] = _rowwise(
            f"b_resnorm_b{layer}", resnorm_bwd, [g, dh4, sv["x2"], sv["y2"]], [nw[layer, 4], nw[layer, 3]],
            out_rows=[(D_MODEL, F32), (D_MODEL, BF16)], out_sums=[D_MODEL, D_MODEL])
        do = _mm(f"b_xo_dx{layer}", dy2, wf['xattn_wo'][layer], tb=True, out_dtype=MXU_DTYPE)
        gbig.setdefault('xattn_wo', {})[layer] = _mm(f"b_xo_dw{layer}", sv["o"], dy2, ta=True)
        dq, dk, dv = _xattn_bwd(f"xattn_b{layer}", sv["q"].reshape(bsz, length, D_MODEL),
                                sv["kv"].reshape(bsz, MEM_LEN, 2 * D_MODEL), do.reshape(bsz, length, D_MODEL))
        dq = dq.reshape(ntok, D_MODEL)
        dkv = jnp.concatenate([dk, dv], axis=-1).reshape(bsz * MEM_LEN, 2 * D_MODEL)
        dh2 = _mm(f"b_xq_dx{layer}", dq, wf['xattn_wq'][layer], tb=True)
        gbig.setdefault('xattn_wq', {})[layer] = _mm(f"b_xq_dw{layer}", sv["h2"], dq, ta=True)
        dmem_n = _mm(f"b_xkv_dx{layer}", dkv, wf['xattn_wkv'][layer], tb=True)
        gbig.setdefault('xattn_wkv', {})[layer] = _mm(f"b_xkv_dw{layer}", sv["mem_n"], dkv, ta=True)
        (gmemnw[layer],) = _rowwise(f"b_mem_norm{layer}", lambda dd, mm_, g1: _rms_bwd(mm_, g1, dd)[1], [dmem_n, mem2],
                                    [w['mem_norm_w'][layer][None]], out_sums=[D_MODEL])

        g, dy1, gnw[layer][2], gnw[layer][1] = _rowwise(
            f"b_resnorm_a{layer}", resnorm_bwd, [g, dh2, sv["x1"], sv["y1"]], [nw[layer, 2], nw[layer, 1]],
            out_rows=[(D_MODEL, F32), (D_MODEL, F32 if layer == 1 else BF16)], out_sums=[D_MODEL, D_MODEL])
        if layer == 1:
            def gate_bwd(a, b, dd):
                sg = jax.nn.sigmoid(b.astype(F32))
                return jnp.concatenate([dd * sg, dd * a.astype(F32) * sg * (1.0 - sg)], axis=1)

            (dzg,) = _rowwise("b_s5_gate", gate_bwd, [sv["zga"], sv["zgb"], dy1], out_rows=[(2 * D_MODEL, BF16)])
            dgl = _mm("b_s5_glu_dx", dzg, wf['s5_w_glu'][0], tb=True)
            gbig['s5_w_glu'] = {0: _mm("b_s5_glu_dw", sv["gl"], dzg, ta=True)}
            dh0, dp, gsmall['s5_d'] = _s5_bwd(dgl, sv["s5"], s5_d_full, bsz)
            for n, t in zip(('s5_lambda_re', 's5_lambda_im', 's5_log_dt', 's5_b_re', 's5_b_im', 's5_c_re', 's5_c_im'), dp):
                gsmall[n] = t[None]
            g, dy3_ahead, gnw[1][0], dw5_ahead = _rowwise(
                "b_norm_in1", resnorm_bwd, [g, dh0, sv["x"], saved[0]["y3"]], [nw[1, 0], nw[0, 5]],
                out_rows=[(D_MODEL, F32), (D_MODEL, BF16)], out_sums=[D_MODEL, D_MODEL])
        else:
            dcore = _mm("b_ab_out_dx", dy1, wf['ab_w_out'][0], tb=True)
            gbig['ab_w_out'] = {0: _mm("b_ab_out_dw", sv["core"], dy1, ta=True)}
            dcore = dcore.reshape(bsz, length, D_MODEL)
            core3 = sv["core"].reshape(bsz, length, D_MODEL)
            dqa, dfa, dia, dga, dlb0, gsmall['hgrn_out_norm_w'] = _hgrn_bwd("hgrn_b", sv["z"], lb_table[0:1], w['hgrn_out_norm_w'], dcore)
            dqb, dkb, dvb = _dilated_bwd("dilated_b", sv["qr"], sv["kr"], sv["vb"], core3, sv["lse"], dcore, branch_cnt,
                                         rope_cos, rope_sin, off=A_WIDTH // B_HD)
            (gsmall['hgrn_lb_logits'],) = lb_vjp(jnp.zeros_like(lb_table).at[0].set(dlb0[0]))
            dz = jnp.concatenate([dqa, dfa, dia, dga, dqb, dkb, dvb], axis=-1).reshape(ntok, -1)
            dh0 = _mm("b_ab_in_dx", dz, wf['ab_w_in'][0], tb=True)
            gbig['ab_w_in'] = {0: _mm("b_ab_in_dw", sv["h0"], dz, ta=True)}
            grad_x, gnw[0][0] = _rowwise("b_norm_in0", lambda gg, dh, xx, g1: (lambda r: (gg + r[0], r[1]))(_rms_bwd(xx, g1, dh)),
                                         [g, dh0, sv["x"]], [nw[0, 0]], out_rows=[(D_MODEL, F32)], out_sums=[D_MODEL])
    gsmall['norm_w'] = jnp.stack([jnp.concatenate(gnw[l], axis=0) for l in range(2)])
    gsmall['mem_norm_w'] = jnp.concatenate(gmemnw, axis=0)

    packed = _pack_rows([gsmall[n] for n in SMALL])
    theirs = _sibling_swap("small_swap", packed)
    chip_sum = _add2("small_pair_sum", packed, theirs)
    (all_chips,) = _gather_chips("small_gather", [chip_sum[None]], [0])
    small_sum = _sum_slots("small_sum", all_chips.reshape(N_CHIPS, packed.shape[0], PACK_COLS))
    full_shapes = [(2, 6, D_MODEL) if n == 'norm_w' else (1, D_MODEL) if n == 's5_d' else w[n].shape for n in SMALL]
    gs = dict(zip(SMALL, _unpack_rows(small_sum, full_shapes)))
    for n in SHARDED_SMALL:
        gs[n] = lax.dynamic_slice_in_dim(gs[n], chip * 256, 256, axis=gs[n].ndim - 1)

    pos = _pos_vec()
    parts = [jnp.stack([gbig[n][l] for l in sorted(gbig[n])]) for n in BIG_NAMES]
    theirs = _pair_send("grad_pair_send", parts, big_axes)
    pair = [_pair_add("grad_pair_sum_" + n, a, b, ax, pos) for (n, ax), a, b in zip(BIG, parts, theirs)]
    slots = _chip_exchange("grad_chip_exchange", pair, big_axes)
    shards = [_chip_sum("grad_chip_sum_" + n, a, b, s, ax, pos) for (n, ax), a, b, s in zip(BIG, parts, theirs, slots)]
    gfull = dict(zip(BIG_NAMES, _pair_join("grad_pair_join", shards)))

    grads, deltas, new_m, new_v = {}, {}, {}, {}
    for n in BIG_NAMES:
        grads[n] = gfull[n]
        deltas[n], new_m[n], new_v[n] = _adam("adam_" + n, w[n], gfull[n], mom[n], var[n])
    pk = [_pack_rows([t[n] for n in SMALL]) for t in (w, gs, mom, var)]
    small_out = _adam("adam_small", *pk)
    shard_shapes = [w[n].shape for n in SMALL]
    for dst, packed_out in zip((deltas, new_m, new_v), small_out):
        dst.update(zip(SMALL, _unpack_rows(packed_out, shard_shapes)))
    grads.update(gs)
    return (loss, grad_x.reshape(x.shape), *[grads[n] for n in WEIGHTS], *[deltas[n] for n in WEIGHTS],
            *[new_m[n] for n in WEIGHTS], *[new_v[n] for n in WEIGHTS])
```
